```python
import math
import jax, jax.numpy as jnp
from jax import lax
import numpy as np

D_MODEL = 1024
BATCH = 8
SEQ = 8192
DEPTH = 2

D_MIX = D_MODEL
D_FF = 2816
NORM_EPS = 1e-6

POOL_WINDOWS = (2, 4, 8, 16)
POOL_GROUPS = len(POOL_WINDOWS)
POOL_GROUP_DIM = 64
POOL_WIDTH = POOL_GROUPS * POOL_GROUP_DIM

ATTN_HEADS = 8
HEAD_DIM = 64
ATTN_WIDTH = ATTN_HEADS * HEAD_DIM
Q_BLOCK = 128

CONV_WIDTH_CH = D_MIX - POOL_WIDTH - ATTN_WIDTH
CONV_KERNEL = 31

IN_COLS = POOL_WIDTH + 3 * ATTN_WIDTH + ATTN_HEADS + 2 * CONV_WIDTH_CH

kernel_name = "hymba_style_pool_fox_conformer_macaron"


def rms_norm(x, g):
    x32 = x.astype(jnp.float32)
    y = x32 * lax.rsqrt(jnp.mean(x32 * x32, axis=-1, keepdims=True) + NORM_EPS)
    return (y * g.astype(jnp.float32)).astype(x.dtype)


def layer_norm(x, g, b):
    x32 = x.astype(jnp.float32)
    mu = jnp.mean(x32, axis=-1, keepdims=True)
    xc = x32 - mu
    var = jnp.mean(xc * xc, axis=-1, keepdims=True)
    y = xc * lax.rsqrt(var + NORM_EPS)
    return (y * g.astype(jnp.float32) + b.astype(jnp.float32)).astype(x.dtype)


def swiglu(h, w_gate, w_up, w_down):
    return (jax.nn.silu(h @ w_gate) * (h @ w_up)) @ w_down


def causal_window_mean(u, w):
    S = u.shape[1]
    u32 = u.astype(jnp.float32)
    cs = jnp.cumsum(u32, axis=1)
    lagged = jnp.pad(cs, ((0, 0), (w, 0), (0, 0)))[:, :S]
    count = jnp.minimum(jnp.arange(S) + 1, w).astype(jnp.float32)
    return ((cs - lagged) / count[None, :, None]).astype(u.dtype)


def pool_mixer(u, pool_w, pool_scale):
    B, S, _ = u.shape
    ug = u.reshape(B, S, POOL_GROUPS, POOL_GROUP_DIM)
    pooled = jnp.stack(
        [causal_window_mean(ug[:, :, g], w) - ug[:, :, g] for g, w in enumerate(POOL_WINDOWS)],
        axis=2)
    mixed = jnp.einsum('bsgc,gcd->bsgd', pooled, pool_w)
    return mixed.reshape(B, S, POOL_WIDTH) * pool_scale


def forgetting_attention(q, k, v, z_f, forget_bias):
    B, S, H, Dh = q.shape
    n_blk = S // Q_BLOCK
    scale = 1.0 / math.sqrt(Dh)
    log_f = jax.nn.log_sigmoid(z_f.astype(jnp.float32) + forget_bias.astype(jnp.float32))
    F = jnp.cumsum(log_f, axis=1).transpose(0, 2, 1)
    qh = q.transpose(0, 2, 1, 3)
    kh = k.transpose(0, 2, 1, 3)
    vh = v.transpose(0, 2, 1, 3)
    q_blocks = qh.reshape(B, H, n_blk, Q_BLOCK, Dh).transpose(2, 0, 1, 3, 4)
    F_blocks = F.reshape(B, H, n_blk, Q_BLOCK).transpose(2, 0, 1, 3)
    k_pos = jnp.arange(S)

    def one_block(args):
        q_i, F_i, i = args
        s = jnp.einsum('bhqd,bhkd->bhqk', q_i, kh).astype(jnp.float32) * scale
        s = s + F_i[..., None] - F[:, :, None, :]
        q_pos = i * Q_BLOCK + jnp.arange(Q_BLOCK)
        mask = k_pos[None, :] <= q_pos[:, None]
        s = jnp.where(mask[None, None], s, -jnp.inf)
        p = jax.nn.softmax(s, axis=-1)
        return jnp.einsum('bhqk,bhkd->bhqd', p.astype(vh.dtype), vh)

    out = lax.map(one_block, (q_blocks, F_blocks, jnp.arange(n_blk)))
    return out.transpose(1, 0, 3, 2, 4).reshape(B, S, H * Dh)


def conformer_conv(h_glu, conv_w, conv_b, ln_g, ln_b):
    a, g = jnp.split(h_glu, 2, axis=-1)
    u = a * jax.nn.sigmoid(g)
    y = lax.conv_general_dilated(
        u, conv_w[:, None, :].astype(u.dtype), window_strides=(1,),
        padding=[(CONV_KERNEL - 1, 0)], dimension_numbers=('NWC', 'WIO', 'NWC'),
        feature_group_count=CONV_WIDTH_CH) + conv_b
    return jax.nn.silu(layer_norm(y, ln_g, ln_b))


def token_mixer(h, w_in, pool_w, pool_scale, forget_bias, conv_w, conv_b, conv_ln_g, conv_ln_b, w_out):
    B, S, _ = h.shape
    p = h @ w_in
    o = 0
    u_pool = p[..., o:o + POOL_WIDTH]; o += POOL_WIDTH
    q = p[..., o:o + ATTN_WIDTH]; o += ATTN_WIDTH
    k = p[..., o:o + ATTN_WIDTH]; o += ATTN_WIDTH
    v = p[..., o:o + ATTN_WIDTH]; o += ATTN_WIDTH
    z_f = p[..., o:o + ATTN_HEADS]; o += ATTN_HEADS
    h_glu = p[..., o:o + 2 * CONV_WIDTH_CH]
    shp = (B, S, ATTN_HEADS, HEAD_DIM)
    y_a = pool_mixer(u_pool, pool_w, pool_scale)
    y_b = forgetting_attention(q.reshape(shp), k.reshape(shp), v.reshape(shp), z_f, forget_bias)
    y_c = conformer_conv(h_glu, conv_w, conv_b, conv_ln_g, conv_ln_b)
    return jnp.concatenate([y_a, y_b, y_c], axis=-1) @ w_out


def _fwd_setup_inputs(seed: int = 0) -> dict:
    key = jax.random.key(seed)
    ks = jax.random.split(key, 24)

    def nrm(k, shape, scale):
        return jax.random.normal(k, shape, jnp.float32) * scale

    def gain(k, shape):
        return 1.0 + 0.02 * jax.random.normal(k, shape, jnp.float32)

    L = DEPTH
    return {
        "x": nrm(ks[0], (BATCH, SEQ, D_MODEL), 1.0),
        "ffn1_norm": gain(ks[1], (L, D_MODEL)),
        "ffn1_w_gate": nrm(ks[2], (L, D_MODEL, D_FF), D_MODEL ** -0.5),
        "ffn1_w_up": nrm(ks[3], (L, D_MODEL, D_FF), D_MODEL ** -0.5),
        "ffn1_w_down": nrm(ks[4], (L, D_FF, D_MODEL), D_FF ** -0.5),
        "mix_norm": gain(ks[5], (L, D_MODEL)),
        "w_in": nrm(ks[6], (L, D_MODEL, IN_COLS), D_MODEL ** -0.5),
        "pool_w": nrm(ks[7], (L, POOL_GROUPS, POOL_GROUP_DIM, POOL_GROUP_DIM), POOL_GROUP_DIM ** -0.5),
        "pool_scale": gain(ks[8], (L, POOL_WIDTH)),
        "forget_bias": 2.0 + 0.1 * jax.random.normal(ks[9], (L, ATTN_HEADS), jnp.float32),
        "conv_w": nrm(ks[10], (L, CONV_KERNEL, CONV_WIDTH_CH), CONV_KERNEL ** -0.5),
        "conv_b": nrm(ks[11], (L, CONV_WIDTH_CH), 0.02),
        "conv_ln_g": gain(ks[12], (L, CONV_WIDTH_CH)),
        "conv_ln_b": nrm(ks[13], (L, CONV_WIDTH_CH), 0.02),
        "w_out": nrm(ks[14], (L, D_MIX, D_MODEL), D_MIX ** -0.5),
        "ffn2_norm": gain(ks[15], (L, D_MODEL)),
        "ffn2_w_gate": nrm(ks[16], (L, D_MODEL, D_FF), D_MODEL ** -0.5),
        "ffn2_w_up": nrm(ks[17], (L, D_MODEL, D_FF), D_MODEL ** -0.5),
        "ffn2_w_down": nrm(ks[18], (L, D_FF, D_MODEL), D_FF ** -0.5),
        "final_norm": gain(ks[19], (D_MODEL,)),
    }


def _fwd_reference(x, ffn1_norm, ffn1_w_gate, ffn1_w_up, ffn1_w_down, mix_norm, w_in, pool_w, pool_scale,
              forget_bias, conv_w, conv_b, conv_ln_g, conv_ln_b, w_out, ffn2_norm, ffn2_w_gate,
              ffn2_w_up, ffn2_w_down, final_norm):
    for l in range(DEPTH):
        x = x + 0.5 * swiglu(rms_norm(x, ffn1_norm[l]), ffn1_w_gate[l], ffn1_w_up[l], ffn1_w_down[l])
        x = x + token_mixer(rms_norm(x, mix_norm[l]), w_in[l], pool_w[l], pool_scale[l], forget_bias[l],
                            conv_w[l], conv_b[l], conv_ln_g[l], conv_ln_b[l], w_out[l])
        x = x + 0.5 * swiglu(rms_norm(x, ffn2_norm[l]), ffn2_w_gate[l], ffn2_w_up[l], ffn2_w_down[l])
    return rms_norm(x, final_norm)


import jax as _jax
import jax.numpy as _jnp

TWIN_FORMAT = 'train_step'
FWD_PARAMS = ['x', 'ffn1_norm', 'ffn1_w_gate', 'ffn1_w_up', 'ffn1_w_down', 'mix_norm', 'w_in', 'pool_w', 'pool_scale', 'forget_bias', 'conv_w', 'conv_b', 'conv_ln_g', 'conv_ln_b', 'w_out', 'ffn2_norm', 'ffn2_w_gate', 'ffn2_w_up', 'ffn2_w_down', 'final_norm']
TWIN_WEIGHTS = ['ffn1_norm', 'ffn1_w_gate', 'ffn1_w_up', 'ffn1_w_down', 'mix_norm', 'w_in', 'pool_w', 'pool_scale', 'forget_bias', 'conv_w', 'conv_b', 'conv_ln_g', 'conv_ln_b', 'w_out', 'ffn2_norm', 'ffn2_w_gate', 'ffn2_w_up', 'ffn2_w_down', 'final_norm']
TWIN_DIFF_INPUT = 'x'
TWIN_INPUTS = ['x', 'ffn1_norm', 'ffn1_w_gate', 'ffn1_w_up', 'ffn1_w_down', 'mix_norm', 'w_in', 'pool_w', 'pool_scale', 'forget_bias', 'conv_w', 'conv_b', 'conv_ln_g', 'conv_ln_b', 'w_out', 'ffn2_norm', 'ffn2_w_gate', 'ffn2_w_up', 'ffn2_w_down', 'final_norm', 'loss_target', 'm_ffn1_norm', 'm_ffn1_w_gate', 'm_ffn1_w_up', 'm_ffn1_w_down', 'm_mix_norm', 'm_w_in', 'm_pool_w', 'm_pool_scale', 'm_forget_bias', 'm_conv_w', 'm_conv_b', 'm_conv_ln_g', 'm_conv_ln_b', 'm_w_out', 'm_ffn2_norm', 'm_ffn2_w_gate', 'm_ffn2_w_up', 'm_ffn2_w_down', 'm_final_norm', 'v_ffn1_norm', 'v_ffn1_w_gate', 'v_ffn1_w_up', 'v_ffn1_w_down', 'v_mix_norm', 'v_w_in', 'v_pool_w', 'v_pool_scale', 'v_forget_bias', 'v_conv_w', 'v_conv_b', 'v_conv_ln_g', 'v_conv_ln_b', 'v_w_out', 'v_ffn2_norm', 'v_ffn2_w_gate', 'v_ffn2_w_up', 'v_ffn2_w_down', 'v_final_norm']
TWIN_OUTPUTS = ['loss', 'grad_x', 'grad_ffn1_norm', 'grad_ffn1_w_gate', 'grad_ffn1_w_up', 'grad_ffn1_w_down', 'grad_mix_norm', 'grad_w_in', 'grad_pool_w', 'grad_pool_scale', 'grad_forget_bias', 'grad_conv_w', 'grad_conv_b', 'grad_conv_ln_g', 'grad_conv_ln_b', 'grad_w_out', 'grad_ffn2_norm', 'grad_ffn2_w_gate', 'grad_ffn2_w_up', 'grad_ffn2_w_down', 'grad_final_norm', 'delta_ffn1_norm', 'delta_ffn1_w_gate', 'delta_ffn1_w_up', 'delta_ffn1_w_down', 'delta_mix_norm', 'delta_w_in', 'delta_pool_w', 'delta_pool_scale', 'delta_forget_bias', 'delta_conv_w', 'delta_conv_b', 'delta_conv_ln_g', 'delta_conv_ln_b', 'delta_w_out', 'delta_ffn2_norm', 'delta_ffn2_w_gate', 'delta_ffn2_w_up', 'delta_ffn2_w_down', 'delta_final_norm', 'new_m_ffn1_norm', 'new_m_ffn1_w_gate', 'new_m_ffn1_w_up', 'new_m_ffn1_w_down', 'new_m_mix_norm', 'new_m_w_in', 'new_m_pool_w', 'new_m_pool_scale', 'new_m_forget_bias', 'new_m_conv_w', 'new_m_conv_b', 'new_m_conv_ln_g', 'new_m_conv_ln_b', 'new_m_w_out', 'new_m_ffn2_norm', 'new_m_ffn2_w_gate', 'new_m_ffn2_w_up', 'new_m_ffn2_w_down', 'new_m_final_norm', 'new_v_ffn1_norm', 'new_v_ffn1_w_gate', 'new_v_ffn1_w_up', 'new_v_ffn1_w_down', 'new_v_mix_norm', 'new_v_w_in', 'new_v_pool_w', 'new_v_pool_scale', 'new_v_forget_bias', 'new_v_conv_w', 'new_v_conv_b', 'new_v_conv_ln_g', 'new_v_conv_ln_b', 'new_v_w_out', 'new_v_ffn2_norm', 'new_v_ffn2_w_gate', 'new_v_ffn2_w_up', 'new_v_ffn2_w_down', 'new_v_final_norm']
TWIN_LEAF_KINDS = {'loss': 'loss', 'grad_x': 'grad_x', 'grad_ffn1_norm': 'grad_w', 'grad_ffn1_w_gate': 'grad_w', 'grad_ffn1_w_up': 'grad_w', 'grad_ffn1_w_down': 'grad_w', 'grad_mix_norm': 'grad_w', 'grad_w_in': 'grad_w', 'grad_pool_w': 'grad_w', 'grad_pool_scale': 'grad_w', 'grad_forget_bias': 'grad_w', 'grad_conv_w': 'grad_w', 'grad_conv_b': 'grad_w', 'grad_conv_ln_g': 'grad_w', 'grad_conv_ln_b': 'grad_w', 'grad_w_out': 'grad_w', 'grad_ffn2_norm': 'grad_w', 'grad_ffn2_w_gate': 'grad_w', 'grad_ffn2_w_up': 'grad_w', 'grad_ffn2_w_down': 'grad_w', 'grad_final_norm': 'grad_w', 'delta_ffn1_norm': 'delta_w', 'delta_ffn1_w_gate': 'delta_w', 'delta_ffn1_w_up': 'delta_w', 'delta_ffn1_w_down': 'delta_w', 'delta_mix_norm': 'delta_w', 'delta_w_in': 'delta_w', 'delta_pool_w': 'delta_w', 'delta_pool_scale': 'delta_w', 'delta_forget_bias': 'delta_w', 'delta_conv_w': 'delta_w', 'delta_conv_b': 'delta_w', 'delta_conv_ln_g': 'delta_w', 'delta_conv_ln_b': 'delta_w', 'delta_w_out': 'delta_w', 'delta_ffn2_norm': 'delta_w', 'delta_ffn2_w_gate': 'delta_w', 'delta_ffn2_w_up': 'delta_w', 'delta_ffn2_w_down': 'delta_w', 'delta_final_norm': 'delta_w', 'new_m_ffn1_norm': 'new_m', 'new_m_ffn1_w_gate': 'new_m', 'new_m_ffn1_w_up': 'new_m', 'new_m_ffn1_w_down': 'new_m', 'new_m_mix_norm': 'new_m', 'new_m_w_in': 'new_m', 'new_m_pool_w': 'new_m', 'new_m_pool_scale': 'new_m', 'new_m_forget_bias': 'new_m', 'new_m_conv_w': 'new_m', 'new_m_conv_b': 'new_m', 'new_m_conv_ln_g': 'new_m', 'new_m_conv_ln_b': 'new_m', 'new_m_w_out': 'new_m', 'new_m_ffn2_norm': 'new_m', 'new_m_ffn2_w_gate': 'new_m', 'new_m_ffn2_w_up': 'new_m', 'new_m_ffn2_w_down': 'new_m', 'new_m_final_norm': 'new_m', 'new_v_ffn1_norm': 'new_v', 'new_v_ffn1_w_gate': 'new_v', 'new_v_ffn1_w_up': 'new_v', 'new_v_ffn1_w_down': 'new_v', 'new_v_mix_norm': 'new_v', 'new_v_w_in': 'new_v', 'new_v_pool_w': 'new_v', 'new_v_pool_scale': 'new_v', 'new_v_forget_bias': 'new_v', 'new_v_conv_w': 'new_v', 'new_v_conv_b': 'new_v', 'new_v_conv_ln_g': 'new_v', 'new_v_conv_ln_b': 'new_v', 'new_v_w_out': 'new_v', 'new_v_ffn2_norm': 'new_v', 'new_v_ffn2_w_gate': 'new_v', 'new_v_ffn2_w_up': 'new_v', 'new_v_ffn2_w_down': 'new_v', 'new_v_final_norm': 'new_v'}


def _forward(args):
    return _fwd_reference(*[args[k] for k in FWD_PARAMS])


def _output_shape():
    def fwd():
        inp = _fwd_setup_inputs(0)
        return _fwd_reference(*[inp[k] for k in FWD_PARAMS])
    out = _jax.eval_shape(fwd)
    return out.shape, out.dtype

N_MICROBATCH = 1
ADAM_LR = 0.001
ADAM_B1 = 0.9
ADAM_B2 = 0.999
ADAM_EPS = 1e-08
ADAM_WD = 0.01
ADAM_STEP = 10
PER_EXAMPLE_BATCH_AXIS = {'x': 0, 'loss_target': 0}
SHARED_INPUTS = []
_WEIGHT_DTYPES = {'ffn1_norm': _jnp.float32, 'ffn1_w_gate': _jnp.float32, 'ffn1_w_up': _jnp.float32, 'ffn1_w_down': _jnp.float32, 'mix_norm': _jnp.float32, 'w_in': _jnp.float32, 'pool_w': _jnp.float32, 'pool_scale': _jnp.float32, 'forget_bias': _jnp.float32, 'conv_w': _jnp.float32, 'conv_b': _jnp.float32, 'conv_ln_g': _jnp.float32, 'conv_ln_b': _jnp.float32, 'w_out': _jnp.float32, 'ffn2_norm': _jnp.float32, 'ffn2_w_gate': _jnp.float32, 'ffn2_w_up': _jnp.float32, 'ffn2_w_down': _jnp.float32, 'final_norm': _jnp.float32}
MOMENT_SCALE = {'ffn1_norm': 1.094154e-01, 'ffn1_w_gate': 4.553790e-02, 'ffn1_w_up': 4.410939e-02, 'ffn1_w_down': 7.327651e-02, 'mix_norm': 1.522008e-01, 'w_in': 9.694359e-02, 'pool_w': 1.807882e-01, 'pool_scale': 1.765368e-01, 'forget_bias': 4.601110e-01, 'conv_w': 1.314654e-01, 'conv_b': 2.803418e-01, 'conv_ln_g': 1.547584e-01, 'conv_ln_b': 1.352550e-01, 'w_out': 1.275635e-01, 'ffn2_norm': 8.580489e-02, 'ffn2_w_gate': 3.659039e-02, 'ffn2_w_up': 3.563172e-02, 'ffn2_w_down': 5.898833e-02, 'final_norm': 6.400445e+01}


def _to_microbatches(a, axis):
    t = _jnp.moveaxis(a, axis, 0)
    t = t.reshape((N_MICROBATCH, t.shape[0] // N_MICROBATCH) + t.shape[1:])
    return _jnp.moveaxis(t, 1, axis + 1)


def setup_inputs(seed: int = 0) -> dict:
    inp = _fwd_setup_inputs(seed)
    key = _jax.random.fold_in(_jax.random.key(seed), 7919)
    shape, _ = _output_shape()
    out = dict(inp)
    out["loss_target"] = _jax.random.normal(_jax.random.fold_in(key, 0), shape, _jnp.float32)
    for i, name in enumerate(TWIN_WEIGHTS):
        w = inp[name].astype(_jnp.float32)
        if MOMENT_SCALE is None:
            s = _jnp.sqrt(_jnp.mean(_jnp.square(w)) + 1e-30)
        else:
            s = MOMENT_SCALE[name]
        km, kv = _jax.random.split(_jax.random.fold_in(key, i + 1))
        out[name] = w
        out["m_" + name] = s * _jax.random.normal(km, w.shape, _jnp.float32)
        out["v_" + name] = (s * s) * _jax.random.uniform(kv, w.shape, _jnp.float32, 0.5, 1.5)
    if N_MICROBATCH > 1:
        for name, axis in PER_EXAMPLE_BATCH_AXIS.items():
            out[name] = _to_microbatches(out[name], axis)
    return {'x': out['x'], 'ffn1_norm': out['ffn1_norm'], 'ffn1_w_gate': out['ffn1_w_gate'], 'ffn1_w_up': out['ffn1_w_up'], 'ffn1_w_down': out['ffn1_w_down'], 'mix_norm': out['mix_norm'], 'w_in': out['w_in'], 'pool_w': out['pool_w'], 'pool_scale': out['pool_scale'], 'forget_bias': out['forget_bias'], 'conv_w': out['conv_w'], 'conv_b': out['conv_b'], 'conv_ln_g': out['conv_ln_g'], 'conv_ln_b': out['conv_ln_b'], 'w_out': out['w_out'], 'ffn2_norm': out['ffn2_norm'], 'ffn2_w_gate': out['ffn2_w_gate'], 'ffn2_w_up': out['ffn2_w_up'], 'ffn2_w_down': out['ffn2_w_down'], 'final_norm': out['final_norm'], 'loss_target': out['loss_target'], 'm_ffn1_norm': out['m_ffn1_norm'], 'm_ffn1_w_gate': out['m_ffn1_w_gate'], 'm_ffn1_w_up': out['m_ffn1_w_up'], 'm_ffn1_w_down': out['m_ffn1_w_down'], 'm_mix_norm': out['m_mix_norm'], 'm_w_in': out['m_w_in'], 'm_pool_w': out['m_pool_w'], 'm_pool_scale': out['m_pool_scale'], 'm_forget_bias': out['m_forget_bias'], 'm_conv_w': out['m_conv_w'], 'm_conv_b': out['m_conv_b'], 'm_conv_ln_g': out['m_conv_ln_g'], 'm_conv_ln_b': out['m_conv_ln_b'], 'm_w_out': out['m_w_out'], 'm_ffn2_norm': out['m_ffn2_norm'], 'm_ffn2_w_gate': out['m_ffn2_w_gate'], 'm_ffn2_w_up': out['m_ffn2_w_up'], 'm_ffn2_w_down': out['m_ffn2_w_down'], 'm_final_norm': out['m_final_norm'], 'v_ffn1_norm': out['v_ffn1_norm'], 'v_ffn1_w_gate': out['v_ffn1_w_gate'], 'v_ffn1_w_up': out['v_ffn1_w_up'], 'v_ffn1_w_down': out['v_ffn1_w_down'], 'v_mix_norm': out['v_mix_norm'], 'v_w_in': out['v_w_in'], 'v_pool_w': out['v_pool_w'], 'v_pool_scale': out['v_pool_scale'], 'v_forget_bias': out['v_forget_bias'], 'v_conv_w': out['v_conv_w'], 'v_conv_b': out['v_conv_b'], 'v_conv_ln_g': out['v_conv_ln_g'], 'v_conv_ln_b': out['v_conv_ln_b'], 'v_w_out': out['v_w_out'], 'v_ffn2_norm': out['v_ffn2_norm'], 'v_ffn2_w_gate': out['v_ffn2_w_gate'], 'v_ffn2_w_up': out['v_ffn2_w_up'], 'v_ffn2_w_down': out['v_ffn2_w_down'], 'v_final_norm': out['v_final_norm']}


def _loss(weights, diff, rest, loss_target):
    with _jax.named_scope("forward"):
        args = {**rest, TWIN_DIFF_INPUT: diff, **{k: w.astype(_WEIGHT_DTYPES[k]) for k, w in weights.items()}}
        y = _forward(args)
    with _jax.named_scope("loss_head"):
        err = _jnp.square(y.astype(_jnp.float32) - loss_target)
        return 0.5 * _jnp.sum(_jnp.mean(err, axis=-1)) if err.ndim else 0.5 * err


def _adamw(w, g, m, v):
    m = ADAM_B1 * m + (1.0 - ADAM_B1) * g
    v = ADAM_B2 * v + (1.0 - ADAM_B2) * _jnp.square(g)
    m_hat = m / (1.0 - ADAM_B1 ** ADAM_STEP)
    v_hat = v / (1.0 - ADAM_B2 ** ADAM_STEP)
    delta = -ADAM_LR * (m_hat / (_jnp.sqrt(v_hat) + ADAM_EPS) + ADAM_WD * w)
    return delta, m, v


def reference(x, ffn1_norm, ffn1_w_gate, ffn1_w_up, ffn1_w_down, mix_norm, w_in, pool_w, pool_scale, forget_bias, conv_w, conv_b, conv_ln_g, conv_ln_b, w_out, ffn2_norm, ffn2_w_gate, ffn2_w_up, ffn2_w_down, final_norm, loss_target, m_ffn1_norm, m_ffn1_w_gate, m_ffn1_w_up, m_ffn1_w_down, m_mix_norm, m_w_in, m_pool_w, m_pool_scale, m_forget_bias, m_conv_w, m_conv_b, m_conv_ln_g, m_conv_ln_b, m_w_out, m_ffn2_norm, m_ffn2_w_gate, m_ffn2_w_up, m_ffn2_w_down, m_final_norm, v_ffn1_norm, v_ffn1_w_gate, v_ffn1_w_up, v_ffn1_w_down, v_mix_norm, v_w_in, v_pool_w, v_pool_scale, v_forget_bias, v_conv_w, v_conv_b, v_conv_ln_g, v_conv_ln_b, v_w_out, v_ffn2_norm, v_ffn2_w_gate, v_ffn2_w_up, v_ffn2_w_down, v_final_norm):
    given = dict(x=x, ffn1_norm=ffn1_norm, ffn1_w_gate=ffn1_w_gate, ffn1_w_up=ffn1_w_up, ffn1_w_down=ffn1_w_down, mix_norm=mix_norm, w_in=w_in, pool_w=pool_w, pool_scale=pool_scale, forget_bias=forget_bias, conv_w=conv_w, conv_b=conv_b, conv_ln_g=conv_ln_g, conv_ln_b=conv_ln_b, w_out=w_out, ffn2_norm=ffn2_norm, ffn2_w_gate=ffn2_w_gate, ffn2_w_up=ffn2_w_up, ffn2_w_down=ffn2_w_down, final_norm=final_norm, loss_target=loss_target, m_ffn1_norm=m_ffn1_norm, m_ffn1_w_gate=m_ffn1_w_gate, m_ffn1_w_up=m_ffn1_w_up, m_ffn1_w_down=m_ffn1_w_down, m_mix_norm=m_mix_norm, m_w_in=m_w_in, m_pool_w=m_pool_w, m_pool_scale=m_pool_scale, m_forget_bias=m_forget_bias, m_conv_w=m_conv_w, m_conv_b=m_conv_b, m_conv_ln_g=m_conv_ln_g, m_conv_ln_b=m_conv_ln_b, m_w_out=m_w_out, m_ffn2_norm=m_ffn2_norm, m_ffn2_w_gate=m_ffn2_w_gate, m_ffn2_w_up=m_ffn2_w_up, m_ffn2_w_down=m_ffn2_w_down, m_final_norm=m_final_norm, v_ffn1_norm=v_ffn1_norm, v_ffn1_w_gate=v_ffn1_w_gate, v_ffn1_w_up=v_ffn1_w_up, v_ffn1_w_down=v_ffn1_w_down, v_mix_norm=v_mix_norm, v_w_in=v_w_in, v_pool_w=v_pool_w, v_pool_scale=v_pool_scale, v_forget_bias=v_forget_bias, v_conv_w=v_conv_w, v_conv_b=v_conv_b, v_conv_ln_g=v_conv_ln_g, v_conv_ln_b=v_conv_ln_b, v_w_out=v_w_out, v_ffn2_norm=v_ffn2_norm, v_ffn2_w_gate=v_ffn2_w_gate, v_ffn2_w_up=v_ffn2_w_up, v_ffn2_w_down=v_ffn2_w_down, v_final_norm=v_final_norm)
    weights = {n: given[n] for n in TWIN_WEIGHTS}
    shared = {n: given[n] for n in SHARED_INPUTS}
    per_example = {n: given[n] for n in ['x']}
    grad_fn = _jax.value_and_grad(_loss, argnums=(0, 1))

    def one_microbatch(ex, loss_target):
        ex = dict(ex)
        diff = ex.pop(TWIN_DIFF_INPUT)
        return grad_fn(weights, diff, {**shared, **ex}, loss_target)

    if N_MICROBATCH == 1:
        loss, (grad_w, grad_x) = one_microbatch(per_example, given["loss_target"])
    else:
        def body(carry, xs):
            loss_sum, grad_sum = carry
            l_k, (gw_k, gx_k) = one_microbatch(xs[0], xs[1])
            with _jax.named_scope("update"):
                return (loss_sum + l_k, _jax.tree.map(_jnp.add, grad_sum, gw_k)), gx_k

        init = (_jnp.zeros((), _jnp.float32), _jax.tree.map(_jnp.zeros_like, weights))
        (loss, grad_w), grad_x = _jax.lax.scan(body, init, (per_example, given["loss_target"]))
    with _jax.named_scope("update"):
        delta_w, new_m, new_v = {}, {}, {}
        for n in TWIN_WEIGHTS:
            delta_w[n], new_m[n], new_v[n] = _adamw(weights[n], grad_w[n], given["m_" + n], given["v_" + n])
    return (loss, grad_x, *[grad_w[n] for n in TWIN_WEIGHTS], *[delta_w[n] for n in TWIN_WEIGHTS],
            *[new_m[n] for n in TWIN_WEIGHTS], *[new_v[n] for n in TWIN_WEIGHTS])
```

```python
import functools
import math

import jax
import jax.numpy as jnp
from jax import lax
from jax.experimental import pallas as pl
from jax.experimental.pallas import tpu as pltpu

F32 = jnp.float32
MM = jnp.bfloat16
NORM_EPS = 1e-6
HEADS = 8
HEAD_DIM = 64
POOL_WINDOWS = (2, 4, 8, 16)
CONV_K = 31
LANES = 128
VMEM_LIMIT = 56 * 2**20

ADAM_LR = 0.001
ADAM_B1 = 0.9
ADAM_B2 = 0.999
ADAM_EPS = 1e-08
ADAM_WD = 0.01
ADAM_STEP = 10

MESH = pl.DeviceIdType.MESH
BS = pl.BlockSpec
SDS = jax.ShapeDtypeStruct
ANY = pl.BlockSpec(memory_space=pl.ANY)


def _dot(a, b):
    return jnp.dot(a, b, preferred_element_type=F32)


def _dot_nt(a, b):
    return lax.dot_general(a, b, (((1,), (1,)), ((), ())), preferred_element_type=F32)


def _dot_tn(a, b):
    return lax.dot_general(a, b, (((0,), (0,)), ((), ())), preferred_element_type=F32)


def _pc(body, name, grid, in_specs, out_specs, out_shape, scratch=()):
    return pl.pallas_call(
        body, out_shape=out_shape, grid=grid, in_specs=in_specs, out_specs=out_specs,
        scratch_shapes=list(scratch), name=name,
        compiler_params=pltpu.CompilerParams(
            dimension_semantics=("arbitrary",) * len(grid), vmem_limit_bytes=VMEM_LIMIT))


def _rms_fwd(x, g):
    r = lax.rsqrt(jnp.mean(x * x, axis=-1, keepdims=True) + NORM_EPS)
    xh = x * r
    return xh, r, xh * g


def _rms_bwd(dh, xh, r, g):
    dxh = dh * g
    dx = r * (dxh - xh * jnp.mean(dxh * xh, axis=-1, keepdims=True))
    return dx, jnp.sum(dh * xh, axis=0, keepdims=True)


def _sigmoid(x):
    return jax.nn.sigmoid(x)


def _ffn_fwd(x, g, wg, wu, wd):
    T, D = x.shape
    F = wg.shape[1]
    tm, nf = min(512, T), 2
    fc = F // nf

    def body(x_ref, g_ref, wg_ref, wu_ref, wd_ref, o_ref, h_scr, acc_scr):
        j = pl.program_id(1)

        @pl.when(j == 0)
        def _():
            _, _, hg = _rms_fwd(x_ref[...], g_ref[...])
            h_scr[...] = hg.astype(h_scr.dtype)
            acc_scr[...] = jnp.zeros_like(acc_scr)

        h = h_scr[...]
        a = _dot(h, wg_ref[...])
        b = _dot(h, wu_ref[...])
        s = (a * _sigmoid(a)) * b
        acc_scr[...] += _dot(s.astype(MM), wd_ref[...])

        @pl.when(j == nf - 1)
        def _():
            o_ref[...] = x_ref[...] + 0.5 * acc_scr[...]

    return _pc(
        body, "ffn_fwd", (T // tm, nf),
        [BS((tm, D), lambda i, j: (i, 0)), BS((1, D), lambda i, j: (0, 0)),
         BS((D, fc), lambda i, j: (0, j)), BS((D, fc), lambda i, j: (0, j)), BS((fc, D), lambda i, j: (j, 0))],
        BS((tm, D), lambda i, j: (i, 0)), SDS((T, D), F32),
        scratch=[pltpu.VMEM((tm, D), MM), pltpu.VMEM((tm, D), F32)])(x, g, wg, wu, wd)


def _ffn_bwd(x, dout, g, wg, wu, wd):
    T, D = x.shape
    F = wg.shape[1]
    tm, nf = min(256, T), 2
    fc = F // nf

    def body(x_ref, do_ref, g_ref, wg_ref, wu_ref, wd_ref,
             dx_ref, h_ref, dy_ref, da_ref, db_ref, s_ref, dg_ref, dh_scr):
        i, j = pl.program_id(0), pl.program_id(1)

        @pl.when(j == 0)
        def _():
            _, _, hg = _rms_fwd(x_ref[...], g_ref[...])
            h_ref[...] = hg.astype(h_ref.dtype)
            dy_ref[...] = (0.5 * do_ref[...]).astype(dy_ref.dtype)
            dh_scr[...] = jnp.zeros_like(dh_scr)

        @pl.when((i == 0) & (j == 0))
        def _():
            dg_ref[...] = jnp.zeros_like(dg_ref)

        h = h_ref[...]
        dy = dy_ref[...]
        a = _dot(h, wg_ref[...])
        b = _dot(h, wu_ref[...])
        ds = _dot_nt(dy, wd_ref[...])
        sig = _sigmoid(a)
        sl = a * sig
        s_ref[...] = (sl * b).astype(s_ref.dtype)
        db = (ds * sl).astype(MM)
        da = (ds * b * (sig * (1.0 + a * (1.0 - sig)))).astype(MM)
        da_ref[...] = da
        db_ref[...] = db
        dh_scr[...] += _dot_nt(da, wg_ref[...]) + _dot_nt(db, wu_ref[...])

        @pl.when(j == nf - 1)
        def _():
            gv = g_ref[...]
            xh, r, _ = _rms_fwd(x_ref[...], gv)
            dx, dg = _rms_bwd(dh_scr[...], xh, r, gv)
            dx_ref[...] = do_ref[...] + dx
            dg_ref[...] += dg

    tok = lambda i, j: (i, 0)
    return _pc(
        body, "ffn_bwd", (T // tm, nf),
        [BS((tm, D), tok), BS((tm, D), tok), BS((1, D), lambda i, j: (0, 0)),
         BS((D, fc), lambda i, j: (0, j)), BS((D, fc), lambda i, j: (0, j)), BS((fc, D), lambda i, j: (j, 0))],
        [BS((tm, D), tok), BS((tm, D), tok), BS((tm, D), tok),
         BS((tm, fc), lambda i, j: (i, j)), BS((tm, fc), lambda i, j: (i, j)), BS((tm, fc), lambda i, j: (i, j)),
         BS((1, D), lambda i, j: (0, 0))],
        [SDS((T, D), F32), SDS((T, D), MM), SDS((T, D), MM),
         SDS((T, F), MM), SDS((T, F), MM), SDS((T, F), MM), SDS((1, D), F32)],
        scratch=[pltpu.VMEM((tm, D), F32)])(x, dout, g, wg, wu, wd)


def _wgrad(a, b, name):
    T, K = a.shape
    N = b.shape[1]
    tt = min(512, T)
    tn = N
    for cand in (1408, 1280, 1024, 512, 256, 128):
        if N % cand == 0 and K * cand * 4 <= 6 * 2**20:
            tn = cand
            break

    def body(a_ref, b_ref, o_ref):
        @pl.when(pl.program_id(1) == 0)
        def _():
            o_ref[...] = jnp.zeros_like(o_ref)

        o_ref[...] += _dot_tn(a_ref[...].astype(MM), b_ref[...].astype(MM))

    return _pc(
        body, name, (N // tn, T // tt),
        [BS((tt, K), lambda n, t: (t, 0)), BS((tt, tn), lambda n, t: (t, n))],
        BS((K, tn), lambda n, t: (0, n)), SDS((K, N), F32))(a, b)


C_POOL, C_Q, C_K, C_V, C_CA, C_CG, C_ZF, C_END = 0, 256, 768, 1280, 1792, 2048, 2304, 2560


def _mix_in_fwd(x, g, w):
    T, D = x.shape
    tm = min(512, T)

    def body(x_ref, g_ref, w_ref, up_ref, q_ref, k_ref, v_ref, ca_ref, cg_ref, zf_ref):
        _, _, hg = _rms_fwd(x_ref[...], g_ref[...])
        p = _dot(hg.astype(MM), w_ref[...])
        up_ref[...] = p[:, C_POOL:C_Q]
        q_ref[...] = p[:, C_Q:C_K].astype(q_ref.dtype)
        k_ref[...] = p[:, C_K:C_V].astype(k_ref.dtype)
        v_ref[...] = p[:, C_V:C_CA].astype(v_ref.dtype)
        ca_ref[...] = p[:, C_CA:C_CG]
        cg_ref[...] = p[:, C_CG:C_ZF]
        zf_ref[...] = p[:, C_ZF:C_ZF + LANES]

    tok = lambda i: (i, 0)
    widths = (256, 512, 512, 512, 256, 256, 128)
    dtypes = (F32, MM, MM, MM, F32, F32, F32)
    return _pc(
        body, "mix_in_fwd", (T // tm,),
        [BS((tm, D), tok), BS((1, D), lambda i: (0, 0)), BS((D, C_END), lambda i: (0, 0))],
        [BS((tm, wd), tok) for wd in widths],
        [SDS((T, wd), dt) for wd, dt in zip(widths, dtypes)])(x, g, w)


def _mix_in_bwd(x, dout, g, w, dup, dq, dk, dv, dca, dcg, dzf):
    T, D = x.shape
    tm = min(512, T)

    def body(x_ref, do_ref, g_ref, w_ref, dup_ref, dq_ref, dk_ref, dv_ref, dca_ref, dcg_ref, dzf_ref,
             dx_ref, h_ref, dp_ref, dg_ref):
        @pl.when(pl.program_id(0) == 0)
        def _():
            dg_ref[...] = jnp.zeros_like(dg_ref)

        gv = g_ref[...]
        xh, r, hg = _rms_fwd(x_ref[...], gv)
        h_ref[...] = hg.astype(h_ref.dtype)
        for ref, lo, hi in ((dup_ref, C_POOL, C_Q), (dq_ref, C_Q, C_K), (dk_ref, C_K, C_V), (dv_ref, C_V, C_CA),
                            (dca_ref, C_CA, C_CG), (dcg_ref, C_CG, C_ZF), (dzf_ref, C_ZF, C_ZF + LANES)):
            dp_ref[:, lo:hi] = ref[...].astype(dp_ref.dtype)
        dp_ref[:, C_ZF + LANES:C_END] = jnp.zeros((tm, C_END - C_ZF - LANES), dp_ref.dtype)
        dh = _dot_nt(dp_ref[...], w_ref[...])
        dx, dg = _rms_bwd(dh, xh, r, gv)
        dx_ref[...] = do_ref[...] + dx
        dg_ref[...] += dg

    tok = lambda i: (i, 0)
    widths = (256, 512, 512, 512, 256, 256, 128)
    return _pc(
        body, "mix_in_bwd", (T // tm,),
        [BS((tm, D), tok), BS((tm, D), tok), BS((1, D), lambda i: (0, 0)), BS((D, C_END), lambda i: (0, 0))]
        + [BS((tm, wd), tok) for wd in widths],
        [BS((tm, D), tok), BS((tm, D), tok), BS((tm, C_END), tok), BS((1, D), lambda i: (0, 0))],
        [SDS((T, D), F32), SDS((T, D), MM), SDS((T, C_END), MM), SDS((1, D), F32)],
    )(x, dout, g, w, dup, dq, dk, dv, dca, dcg, dzf)


def _mix_out_fwd(x, ya, yb, yc, wo):
    T, D = x.shape
    tm = min(512, T)

    def body(x_ref, ya_ref, yb_ref, yc_ref, wo_ref, o_ref):
        o_ref[...] = (x_ref[...] + _dot(ya_ref[...].astype(MM), wo_ref[0:256, :])
                      + _dot(yb_ref[...].astype(MM), wo_ref[256:768, :])
                      + _dot(yc_ref[...].astype(MM), wo_ref[768:1024, :]))

    tok = lambda i: (i, 0)
    return _pc(
        body, "mix_out_fwd", (T // tm,),
        [BS((tm, D), tok), BS((tm, 256), tok), BS((tm, 512), tok), BS((tm, 256), tok), BS((D, D), lambda i: (0, 0))],
        BS((tm, D), tok), SDS((T, D), F32))(x, ya, yb, yc, wo)


def _mix_out_bwd(dx, wo):
    T, D = dx.shape
    tm = min(512, T)

    def body(dx_ref, wo_ref, dya_ref, dyb_ref, dyc_ref):
        dy = _dot_nt(dx_ref[...].astype(MM), wo_ref[...])
        dya_ref[...] = dy[:, 0:256]
        dyb_ref[...] = dy[:, 256:768]
        dyc_ref[...] = dy[:, 768:1024]

    tok = lambda i: (i, 0)
    return _pc(
        body, "mix_out_bwd", (T // tm,),
        [BS((tm, D), tok), BS((D, D), lambda i: (0, 0))],
        [BS((tm, 256), tok), BS((tm, 512), tok), BS((tm, 256), tok)],
        [SDS((T, 256), F32), SDS((T, 512), F32), SDS((T, 256), F32)])(dx, wo)


def _fgate_fwd(zf, bias):
    T = zf.shape[0]
    tc = min(256, T)

    def body(z_ref, b_ref, f_ref, carry):
        @pl.when(pl.program_id(0) == 0)
        def _():
            carry[...] = jnp.zeros_like(carry)

        z = z_ref[...] + b_ref[...]
        logf = jnp.minimum(z, 0.0) - jnp.log(1.0 + jnp.exp(-jnp.abs(z)))
        row = lax.broadcasted_iota(jnp.int32, (tc, tc), 0)
        col = lax.broadcasted_iota(jnp.int32, (tc, tc), 1)
        tri = (col <= row).astype(F32)
        f_ref[...] = jnp.dot(tri, logf, precision=lax.Precision.HIGHEST, preferred_element_type=F32) + carry[...]
        carry[...] += jnp.sum(logf, axis=0, keepdims=True)

    return _pc(
        body, "fgate_fwd", (T // tc,),
        [BS((tc, LANES), lambda i: (i, 0)), BS((1, LANES), lambda i: (0, 0))],
        BS((tc, LANES), lambda i: (i, 0)), SDS((T, LANES), F32),
        scratch=[pltpu.VMEM((1, LANES), F32)])(zf, bias)


def _fgate_bwd(zf, bias, dFq, dFk):
    T = zf.shape[0]
    tc = min(256, T)
    n = T // tc

    def body(z_ref, b_ref, dfq_ref, dfk_ref, dz_ref, db_ref, carry):
        @pl.when(pl.program_id(0) == 0)
        def _():
            carry[...] = jnp.zeros_like(carry)
            db_ref[...] = jnp.zeros_like(db_ref)

        df = dfq_ref[...] + dfk_ref[...]
        row = lax.broadcasted_iota(jnp.int32, (tc, tc), 0)
        col = lax.broadcasted_iota(jnp.int32, (tc, tc), 1)
        tri = (col >= row).astype(F32)
        dlogf = jnp.dot(tri, df, precision=lax.Precision.HIGHEST, preferred_element_type=F32) + carry[...]
        carry[...] += jnp.sum(df, axis=0, keepdims=True)
        lane = lax.broadcasted_iota(jnp.int32, (1, LANES), 1)
        dz = jnp.where(lane < HEADS, dlogf * _sigmoid(-(z_ref[...] + b_ref[...])), 0.0)
        dz_ref[...] = dz
        db_ref[...] += jnp.sum(dz, axis=0, keepdims=True)

    rev = lambda i: (n - 1 - i, 0)
    return _pc(
        body, "fgate_bwd", (n,),
        [BS((tc, LANES), rev), BS((1, LANES), lambda i: (0, 0)), BS((tc, LANES), rev), BS((tc, LANES), rev)],
        [BS((tc, LANES), rev), BS((1, LANES), lambda i: (0, 0))],
        [SDS((T, LANES), F32), SDS((1, LANES), F32)],
        scratch=[pltpu.VMEM((1, LANES), F32)])(zf, bias, dFq, dFk)


def _attn_fwd(q, k, v, fq, fk):
    H, T, Dh = q.shape
    tq = min(512, T)
    tk = tq
    nk = T // tk
    scale = 1.0 / math.sqrt(Dh)

    def body(q_ref, k_ref, v_ref, fq_ref, fk_ref, o_ref, lse_ref):
        ib = pl.program_id(1)
        qs = (q_ref[0].astype(F32) * scale).astype(MM)
        fqv = fq_ref[0]
        row = ib * tq + lax.broadcasted_iota(jnp.int32, (tq, tk), 0)
        colb = lax.broadcasted_iota(jnp.int32, (tq, tk), 1)

        def step(jb, carry):
            m, l, acc = carry
            off = pl.multiple_of(jb * tk, tk)
            kt = k_ref[0, pl.ds(off, tk), :]
            vt = v_ref[0, pl.ds(off, tk), :]
            s = _dot_nt(qs, kt) + (fqv - fk_ref[0, pl.ds(jb, 1), :])
            s = jnp.where(colb + jb * tk <= row, s, -jnp.inf)
            m2 = jnp.maximum(m, jnp.max(s, axis=-1, keepdims=True))
            p = jnp.exp(s - m2)
            al = jnp.exp(m - m2)
            l2 = l * al + jnp.sum(p, axis=-1, keepdims=True)
            acc2 = acc * al + _dot(p.astype(MM), vt)
            return m2, l2, acc2

        init = (jnp.full((tq, 1), -jnp.inf, F32), jnp.zeros((tq, 1), F32), jnp.zeros((tq, Dh), F32))
        m, l, acc = lax.fori_loop(0, ib + 1, step, init)
        o_ref[0] = acc / l
        lse_ref[0] = m + jnp.log(l)

    return _pc(
        body, "attn_fwd", (H, T // tq),
        [BS((1, tq, Dh), lambda h, i: (h, i, 0)), BS((1, T, Dh), lambda h, i: (h, 0, 0)),
         BS((1, T, Dh), lambda h, i: (h, 0, 0)), BS((1, tq, 1), lambda h, i: (h, i, 0)),
         BS((1, nk, tk), lambda h, i: (h, 0, 0))],
        [BS((1, tq, Dh), lambda h, i: (h, i, 0)), BS((1, tq, 1), lambda h, i: (h, i, 0))],
        [SDS((H, T, Dh), F32), SDS((H, T, 1), F32)])(q, k, v, fq, fk)


def _attn_bwd(q, k, v, fq, fk, o, lse, do):
    H, T, Dh = q.shape
    tq = min(512, T)
    tk = tq
    nk = T // tk
    scale = 1.0 / math.sqrt(Dh)

    def body(q_ref, k_ref, v_ref, fq_ref, fk_ref, o_ref, lse_ref, do_ref, dq_ref, dk_ref, dv_ref, dfq_ref, dfk_ref):
        ib = pl.program_id(1)

        @pl.when(ib == 0)
        def _():
            dk_ref[...] = jnp.zeros_like(dk_ref)
            dv_ref[...] = jnp.zeros_like(dv_ref)
            dfk_ref[...] = jnp.zeros_like(dfk_ref)

        qs = (q_ref[0].astype(F32) * scale).astype(MM)
        dov = do_ref[0]
        dob = dov.astype(MM)
        delta = jnp.sum(dob.astype(F32) * o_ref[0], axis=-1, keepdims=True)
        fqv = fq_ref[0]
        lse = lse_ref[0]
        row = ib * tq + lax.broadcasted_iota(jnp.int32, (tq, tk), 0)
        colb = lax.broadcasted_iota(jnp.int32, (tq, tk), 1)

        def step(jb, carry):
            dq, rs = carry
            off = pl.multiple_of(jb * tk, tk)
            kt = k_ref[0, pl.ds(off, tk), :]
            vt = v_ref[0, pl.ds(off, tk), :]
            s = _dot_nt(qs, kt) + (fqv - fk_ref[0, pl.ds(jb, 1), :])
            p = jnp.where(colb + jb * tk <= row, jnp.exp(s - lse), 0.0)
            dp = _dot_nt(dob, vt)
            ds = p * (dp - delta)
            dsb = ds.astype(MM)
            dv_ref[0, pl.ds(off, tk), :] += _dot_tn(p.astype(MM), dob)
            dk_ref[0, pl.ds(off, tk), :] += _dot_tn(dsb, qs)
            dfk_ref[0, pl.ds(jb, 1), :] -= jnp.sum(ds, axis=0, keepdims=True)
            return dq + _dot(dsb, kt), rs + jnp.sum(ds, axis=-1, keepdims=True)

        dq, rs = lax.fori_loop(0, ib + 1, step, (jnp.zeros((tq, Dh), F32), jnp.zeros((tq, 1), F32)))
        dq_ref[0] = dq * scale
        dfq_ref[0] = rs

    blk = lambda h, i: (h, i, 0)
    full = lambda h, i: (h, 0, 0)
    return _pc(
        body, "attn_bwd", (H, T // tq),
        [BS((1, tq, Dh), blk), BS((1, T, Dh), full), BS((1, T, Dh), full), BS((1, tq, 1), blk),
         BS((1, nk, tk), full), BS((1, tq, Dh), blk), BS((1, tq, 1), blk), BS((1, tq, Dh), blk)],
        [BS((1, tq, Dh), blk), BS((1, T, Dh), full), BS((1, T, Dh), full), BS((1, tq, 1), blk),
         BS((1, nk, tk), full)],
        [SDS((H, T, Dh), F32), SDS((H, T, Dh), F32), SDS((H, T, Dh), F32), SDS((H, T, 1), F32),
         SDS((H, nk, tk), F32)],
    )(q, k, v, fq, fk, o, lse, do)


POOL_HALO = 16
CONV_HALO = 32


def _group_select(lane, v0, v1, v2, v3):
    return jnp.where(lane < 64, v0, jnp.where(lane < 128, v1, jnp.where(lane < 192, v2, v3)))


def _roll_down(x, k):
    return x if k == 0 else pltpu.roll(x, k, 0)


def _roll_up(x, k):
    return x if k == 0 else pltpu.roll(x, x.shape[0] - k, 0)


def _pool_terms(u, u_prev, tile, tm):
    ext = jnp.concatenate([u_prev, u], axis=0)
    s2 = ext + _roll_down(ext, 1)
    s4 = s2 + _roll_down(s2, 2)
    s8 = s4 + _roll_down(s4, 4)
    s16 = s8 + _roll_down(s8, 8)
    lane = lax.broadcasted_iota(jnp.int32, (1, 256), 1)
    ws = _group_select(lane, s2, s4, s8, s16)[POOL_HALO:, :]
    wlen = _group_select(lane, 2.0, 4.0, 8.0, 16.0).astype(F32)
    return ws / _pool_count(tile, tm, tm, wlen) - u


def _pool_count(tile, tm, rows, wlen):
    t = (tile * tm + 1 + lax.broadcasted_iota(jnp.int32, (rows, 1), 0)).astype(F32)
    return jnp.minimum(t, wlen)


def _layer_norm(y, lg, lb):
    mu = jnp.mean(y, axis=-1, keepdims=True)
    yc = y - mu
    rstd = lax.rsqrt(jnp.mean(yc * yc, axis=-1, keepdims=True) + NORM_EPS)
    yh = yc * rstd
    return yh, rstd, yh * lg + lb


def _halo_specs(tm, T, halo, prev):
    per = tm // halo
    if prev:
        return BS((halo, 256), lambda i: (jnp.maximum(i * per - 1, 0), 0))
    return BS((halo, 256), lambda i: (jnp.minimum((i + 1) * per, T // halo - 1), 0))


def _local_fwd(up, ca, cg, bd, pscale, cw, cb, lg, lb):
    T = up.shape[0]
    tm = min(512, T)

    def body(up_ref, uph_ref, ca_ref, cah_ref, cg_ref, cgh_ref, bd_ref, ps_ref, cw_ref, cb_ref, lg_ref, lb_ref,
             ya_ref, yc_ref, u_ref, y_ref):
        i = pl.program_id(0)
        first = i == 0
        pooled = _pool_terms(up_ref[...], jnp.where(first, 0.0, uph_ref[...]), i, tm)
        ya_ref[...] = (_dot(pooled.astype(MM), bd_ref[...]) * ps_ref[...]).astype(ya_ref.dtype)

        u = ca_ref[...] * _sigmoid(cg_ref[...])
        uh = jnp.where(first, 0.0, cah_ref[...] * _sigmoid(cgh_ref[...]))
        ext = jnp.concatenate([uh, u], axis=0)
        y = jnp.zeros((tm, 256), F32) + cb_ref[...]
        for kk in range(CONV_K):
            y = y + cw_ref[kk:kk + 1, :] * _roll_up(ext, CONV_HALO - (CONV_K - 1) + kk)[:tm, :]
        _, _, z = _layer_norm(y, lg_ref[...], lb_ref[...])
        yc_ref[...] = (z * _sigmoid(z)).astype(yc_ref.dtype)
        u_ref[...] = u
        y_ref[...] = y

    tok = lambda i: (i, 0)
    par = lambda i: (0, 0)
    t256 = BS((tm, 256), tok)
    return _pc(
        body, "local_fwd", (T // tm,),
        [t256, _halo_specs(tm, T, POOL_HALO, True), t256, _halo_specs(tm, T, CONV_HALO, True),
         t256, _halo_specs(tm, T, CONV_HALO, True),
         BS((256, 256), par), BS((1, 256), par), BS((32, 256), par), BS((1, 256), par), BS((1, 256), par),
         BS((1, 256), par)],
        [t256, t256, t256, t256],
        [SDS((T, 256), MM), SDS((T, 256), MM), SDS((T, 256), F32), SDS((T, 256), F32)],
    )(up, up, ca, ca, cg, cg, bd, pscale, cw, cb, lg, lb)


def _local_bwd(up, dya, ca, cg, u, y, dyc, bd, pscale, cw, lg, lb):
    T = up.shape[0]
    tm = min(512, T)
    n = T // tm

    def body(up_ref, uph_ref, dya_ref, dyan_ref, ca_ref, cg_ref, u_ref, uh_ref, y_ref, yn_ref, dyc_ref, dycn_ref,
             bd_ref, ps_ref, cw_ref, lg_ref, lb_ref,
             dup_ref, dca_ref, dcg_ref, dbd_ref, dps_ref, dcw_ref, dcb_ref, dlg_ref, dlb_ref):
        i = pl.program_id(0)
        first = i == 0
        last = i == n - 1

        @pl.when(first)
        def _():
            for ref in (dbd_ref, dps_ref, dcw_ref, dcb_ref, dlg_ref, dlb_ref):
                ref[...] = jnp.zeros_like(ref)

        ps = ps_ref[...]
        pooled = _pool_terms(up_ref[...], jnp.where(first, 0.0, uph_ref[...]), i, tm).astype(MM)
        dya_t = dya_ref[...]
        dps_ref[...] += jnp.sum(dya_t * _dot(pooled, bd_ref[...]), axis=0, keepdims=True)
        dm = (jnp.concatenate([dya_t, jnp.where(last, 0.0, dyan_ref[...])], axis=0) * ps).astype(MM)
        dbd_ref[...] += _dot_tn(pooled, dm[:tm, :])
        dpool = _dot_nt(dm, bd_ref[...])
        lane = lax.broadcasted_iota(jnp.int32, (1, 256), 1)
        wlen = _group_select(lane, 2.0, 4.0, 8.0, 16.0).astype(F32)
        e = dpool / _pool_count(i, tm, tm + POOL_HALO, wlen)
        f2 = e + _roll_up(e, 1)
        f4 = f2 + _roll_up(f2, 2)
        f8 = f4 + _roll_up(f4, 4)
        f16 = f8 + _roll_up(f8, 8)
        dup_ref[...] = _group_select(lane, f2, f4, f8, f16)[:tm, :] - dpool[:tm, :]

        lgv = lg_ref[...]
        yext = jnp.concatenate([y_ref[...], yn_ref[...]], axis=0)
        dyc = jnp.concatenate([dyc_ref[...], jnp.where(last, 0.0, dycn_ref[...])], axis=0)
        yh, rstd, z = _layer_norm(yext, lgv, lb_ref[...])
        sig = _sigmoid(z)
        dz = dyc * (sig * (1.0 + z * (1.0 - sig)))
        dlg_ref[...] += jnp.sum((dz * yh)[:tm, :], axis=0, keepdims=True)
        dlb_ref[...] += jnp.sum(dz[:tm, :], axis=0, keepdims=True)
        dyh = dz * lgv
        dy = rstd * (dyh - jnp.mean(dyh, axis=-1, keepdims=True) - yh * jnp.mean(dyh * yh, axis=-1, keepdims=True))
        dy_t = dy[:tm, :]
        dcb_ref[...] += jnp.sum(dy_t, axis=0, keepdims=True)
        uext = jnp.concatenate([jnp.where(first, 0.0, uh_ref[...]), u_ref[...]], axis=0)
        du = jnp.zeros((tm, 256), F32)
        for kk in range(CONV_K):
            shifted = _roll_up(uext, CONV_HALO - (CONV_K - 1) + kk)[:tm, :]
            dcw_ref[kk:kk + 1, :] += jnp.sum(dy_t * shifted, axis=0, keepdims=True)
            du = du + cw_ref[kk:kk + 1, :] * _roll_up(dy, CONV_K - 1 - kk)[:tm, :]
        sg = _sigmoid(cg_ref[...])
        dca_ref[...] = du * sg
        dcg_ref[...] = du * ca_ref[...] * sg * (1.0 - sg)

    tok = lambda i: (i, 0)
    par = lambda i: (0, 0)
    t256 = BS((tm, 256), tok)
    p1 = BS((1, 256), par)
    return _pc(
        body, "local_bwd", (n,),
        [t256, _halo_specs(tm, T, POOL_HALO, True), t256, _halo_specs(tm, T, POOL_HALO, False), t256, t256,
         t256, _halo_specs(tm, T, CONV_HALO, True), t256, _halo_specs(tm, T, CONV_HALO, False),
         t256, _halo_specs(tm, T, CONV_HALO, False),
         BS((256, 256), par), p1, BS((32, 256), par), p1, p1],
        [t256, t256, t256, BS((256, 256), par), p1, BS((32, 256), par), p1, p1, p1],
        [SDS((T, 256), F32)] * 3 + [SDS((256, 256), F32), SDS((1, 256), F32), SDS((32, 256), F32)]
        + [SDS((1, 256), F32)] * 3,
    )(up, up, dya, dya, ca, cg, u, u, y, y, dyc, dyc, bd, pscale, cw, lg, lb)


def _head(x, g, target):
    T, D = x.shape
    tm = min(512, T)

    def body(x_ref, g_ref, t_ref, loss_ref, dx_ref, dg_ref):
        @pl.when(pl.program_id(0) == 0)
        def _():
            loss_ref[...] = jnp.zeros_like(loss_ref)
            dg_ref[...] = jnp.zeros_like(dg_ref)

        gv = g_ref[...]
        xh, r, yv = _rms_fwd(x_ref[...], gv)
        err = yv - t_ref[...]
        loss_ref[...] += 0.5 * jnp.sum(jnp.mean(err * err, axis=-1, keepdims=True), axis=0, keepdims=True)
        dx, dg = _rms_bwd(err * (1.0 / D), xh, r, gv)
        dx_ref[...] = dx
        dg_ref[...] += dg

    tok = lambda i: (i, 0)
    par = lambda i: (0, 0)
    return _pc(
        body, "head", (T // tm,),
        [BS((tm, D), tok), BS((1, D), par), BS((tm, D), tok)],
        [BS((1, LANES), par), BS((tm, D), tok), BS((1, D), par)],
        [SDS((1, LANES), F32), SDS((T, D), F32), SDS((1, D), F32)])(x, g, target)


def _adamw(w, gs, m, v, name):
    R, C = w.shape
    tr = R
    for cand in (512, 256, 128, 64, 32, 16, 8):
        if R % cand == 0:
            tr = cand
            break
    ng = len(gs)

    def body(*refs):
        w_ref, g_refs, m_ref, v_ref = refs[0], refs[1:1 + ng], refs[1 + ng], refs[2 + ng]
        g_ref, d_ref, m2_ref, v2_ref = refs[3 + ng:]
        g = g_refs[0][...]
        for r in g_refs[1:]:
            g = g + r[...]
        m2 = ADAM_B1 * m_ref[...] + (1.0 - ADAM_B1) * g
        v2 = ADAM_B2 * v_ref[...] + (1.0 - ADAM_B2) * jnp.square(g)
        m_hat = m2 / (1.0 - ADAM_B1 ** ADAM_STEP)
        v_hat = v2 / (1.0 - ADAM_B2 ** ADAM_STEP)
        g_ref[...] = g
        d_ref[...] = -ADAM_LR * (m_hat / (jnp.sqrt(v_hat) + ADAM_EPS) + ADAM_WD * w_ref[...])
        m2_ref[...] = m2
        v2_ref[...] = v2

    blk = BS((tr, C), lambda i: (i, 0))
    return _pc(body, name, (R // tr,), [blk] * (3 + ng), [blk] * 4, [SDS((R, C), F32)] * 4)(w, *gs, m, v)


def _sum_parts(own, recv, name):
    R, C = own.shape
    tr = 512

    def body(o_ref, r_ref, s_ref):
        s_ref[...] = ((o_ref[...] + r_ref[0].astype(F32)) + r_ref[1].astype(F32)) + r_ref[2].astype(F32)

    return _pc(body, name, (R // tr,),
               [BS((tr, C), lambda i: (i, 0)), BS((3, tr, C), lambda i: (0, i, 0))],
               BS((tr, C), lambda i: (i, 0)), SDS((R, C), F32))(own, recv)


def _sum8(parts, name):
    _, R, C = parts.shape

    def body(p_ref, s_ref):
        acc = p_ref[0]
        for d in range(1, 8):
            acc = acc + p_ref[d]
        s_ref[...] = acc

    return _pc(body, name, (1,), [BS((8, R, C), lambda i: (0, 0, 0))], BS((R, C), lambda i: (0, 0)),
               SDS((R, C), F32))(parts)


def _position():
    return lax.axis_index("x"), lax.axis_index("y"), lax.axis_index("c")


CHIP_FLIPS = ((1, 0), (0, 1), (1, 1))


def _gather_chips(block, name):
    R, C = block.shape

    def body(in_ref, out_ref, send_sems, recv_sems, local_sem):
        x, y, c = _position()
        mine = out_ref.at[2 * x + y]
        local = pltpu.make_async_copy(in_ref, mine, local_sem)
        local.start()
        copies = [
            pltpu.make_async_remote_copy(
                src_ref=in_ref, dst_ref=mine, send_sem=send_sems.at[k], recv_sem=recv_sems.at[k],
                device_id=(x ^ fx, y ^ fy, c), device_id_type=MESH)
            for k, (fx, fy) in enumerate(CHIP_FLIPS)]
        for cp in copies:
            cp.start()
        for k, (fx, fy) in enumerate(CHIP_FLIPS):
            theirs = out_ref.at[2 * (x ^ fx) + (y ^ fy)]
            pltpu.make_async_remote_copy(
                src_ref=in_ref, dst_ref=theirs, send_sem=send_sems.at[k], recv_sem=recv_sems.at[k],
                device_id=(x ^ fx, y ^ fy, c), device_id_type=MESH).wait_recv()
        for cp in copies:
            cp.wait_send()
        local.wait()

    return pl.pallas_call(
        body, out_shape=SDS((4, R, C), block.dtype), in_specs=[ANY], out_specs=ANY, name=name,
        scratch_shapes=[pltpu.SemaphoreType.DMA((3,)), pltpu.SemaphoreType.DMA((3,)), pltpu.SemaphoreType.DMA(())])(block)


def _scatter_chips(parts, name):
    _, R, C = parts.shape

    def body(in_ref, out_ref, send_sems, recv_sems):
        x, y, c = _position()
        copies = [
            pltpu.make_async_remote_copy(
                src_ref=in_ref.at[2 * (x ^ fx) + (y ^ fy)], dst_ref=out_ref.at[k],
                send_sem=send_sems.at[k], recv_sem=recv_sems.at[k],
                device_id=(x ^ fx, y ^ fy, c), device_id_type=MESH)
            for k, (fx, fy) in enumerate(CHIP_FLIPS)]
        for cp in copies:
            cp.start()
        for cp in copies:
            cp.wait_recv()
        for cp in copies:
            cp.wait_send()

    return pl.pallas_call(
        body, out_shape=SDS((3, R, C), parts.dtype), in_specs=[ANY], out_specs=ANY, name=name,
        scratch_shapes=[pltpu.SemaphoreType.DMA((3,)), pltpu.SemaphoreType.DMA((3,))])(parts)


def _swap_cores(block, name):
    def body(in_ref, out_ref, send_sem, recv_sem):
        x, y, c = _position()
        cp = pltpu.make_async_remote_copy(
            src_ref=in_ref, dst_ref=out_ref, send_sem=send_sem, recv_sem=recv_sem,
            device_id=(x, y, 1 - c), device_id_type=MESH)
        cp.start()
        cp.wait_recv()
        cp.wait_send()

    return pl.pallas_call(
        body, out_shape=SDS(block.shape, block.dtype), in_specs=[ANY], out_specs=ANY, name=name,
        scratch_shapes=[pltpu.SemaphoreType.DMA(()), pltpu.SemaphoreType.DMA(())])(block)


def _gather_all(block, name):
    R, C = block.shape
    flips = [(fx, fy, fc) for fx in (0, 1) for fy in (0, 1) for fc in (0, 1)][1:]

    def body(in_ref, out_ref, send_sems, recv_sems, local_sem):
        x, y, c = _position()
        mine = out_ref.at[4 * x + 2 * y + c]
        local = pltpu.make_async_copy(in_ref, mine, local_sem)
        local.start()
        copies = [
            pltpu.make_async_remote_copy(
                src_ref=in_ref, dst_ref=mine, send_sem=send_sems.at[k], recv_sem=recv_sems.at[k],
                device_id=(x ^ fx, y ^ fy, c ^ fc), device_id_type=MESH)
            for k, (fx, fy, fc) in enumerate(flips)]
        for cp in copies:
            cp.start()
        for k, (fx, fy, fc) in enumerate(flips):
            theirs = out_ref.at[4 * (x ^ fx) + 2 * (y ^ fy) + (c ^ fc)]
            pltpu.make_async_remote_copy(
                src_ref=in_ref, dst_ref=theirs, send_sem=send_sems.at[k], recv_sem=recv_sems.at[k],
                device_id=(x ^ fx, y ^ fy, c ^ fc), device_id_type=MESH).wait_recv()
        for cp in copies:
            cp.wait_send()
        local.wait()

    return pl.pallas_call(
        body, out_shape=SDS((8, R, C), block.dtype), in_specs=[ANY], out_specs=ANY, name=name,
        scratch_shapes=[pltpu.SemaphoreType.DMA((7,)), pltpu.SemaphoreType.DMA((7,)), pltpu.SemaphoreType.DMA(())])(block)


BIG = ("ffn1_w_gate", "ffn1_w_up", "ffn1_w_down", "w_in", "w_out", "ffn2_w_gate", "ffn2_w_up", "ffn2_w_down")
COL_SHARDED = ("ffn1_w_gate", "ffn1_w_up", "w_in", "ffn2_w_gate", "ffn2_w_up")
PACK_COLS = 1024
PACK_ROW_TILE = 512


def _pack(blocks):
    rows = [b.reshape(-1, PACK_COLS) for b in blocks]
    total = sum(r.shape[0] for r in rows)
    pad = -total % PACK_ROW_TILE
    if pad:
        rows.append(jnp.zeros((pad, PACK_COLS), rows[0].dtype))
    return jnp.concatenate(rows, axis=0)


def _unpack(buf, shapes):
    out, off = [], 0
    for shp in shapes:
        n = math.prod(shp) // PACK_COLS
        out.append(buf[off:off + n].reshape(shp))
        off += n
    return out


def _to_shards(name, full):
    L, r, c = full.shape
    if name in COL_SHARDED:
        return full.reshape(L, r, 4, c // 4).transpose(2, 0, 1, 3)
    return full.reshape(L, 4, r // 4, c).transpose(1, 0, 2, 3)


def _from_shards(name, sh):
    _, L, r, c = sh.shape
    if name in COL_SHARDED:
        return sh.transpose(1, 2, 0, 3).reshape(L, r, 4 * c)
    return sh.transpose(1, 0, 2, 3).reshape(L, 4 * r, c)


def _pad_w_in(w):
    return jnp.concatenate([w[:, :1792], w[:, 1800:2312], w[:, 1792:1800], jnp.zeros((w.shape[0], 248), w.dtype)], axis=1)


def _unpad_w_in(g):
    return jnp.concatenate([g[:, :1792], g[:, 2304:2312], g[:, 1792:2304]], axis=1)


def _block_diag(pw):
    out = jnp.zeros((256, 256), pw.dtype)
    for gidx in range(4):
        out = lax.dynamic_update_slice(out, pw[gidx], (64 * gidx, 64 * gidx))
    return out


SMALL = ("ffn1_norm", "mix_norm", "pool_w", "pool_scale", "forget_bias", "conv_b", "conv_ln_g", "conv_ln_b",
         "ffn2_norm", "final_norm")


def _pack_small(arrs):
    rows = []
    for a in arrs:
        flat = a.reshape(-1)
        flat = jnp.pad(flat, (0, -flat.shape[0] % LANES))
        rows.append(flat.reshape(-1, LANES))
    total = sum(r.shape[0] for r in rows)
    if total % 8:
        rows.append(jnp.zeros((-total % 8, LANES), F32))
    return jnp.concatenate(rows, axis=0)


def _unpack_small(buf, shapes):
    out, off = [], 0
    for shp in shapes:
        n = math.prod(shp)
        nr = -(-n // LANES)
        out.append(buf[off:off + nr].reshape(-1)[:n].reshape(shp))
        off += nr
    return out


def _heads(a):
    T = a.shape[0]
    return a.reshape(T, HEADS, HEAD_DIM).transpose(1, 0, 2)


def _unheads(a):
    T = a.shape[1]
    return a.transpose(1, 0, 2).reshape(T, HEADS * HEAD_DIM)


def _forward_backward(x, target, W):
    T = x.shape[0]
    L = W["ffn1_norm"].shape[0]
    tk = min(512, T)
    saved = []
    for l in range(L):
        g1, gm, g2 = (W[n][l][None, :] for n in ("ffn1_norm", "mix_norm", "ffn2_norm"))
        x1 = _ffn_fwd(x, g1, W["ffn1_w_gate"][l], W["ffn1_w_up"][l], W["ffn1_w_down"][l])
        w_in = _pad_w_in(W["w_in"][l])
        up, q, k, v, ca, cg, zf = _mix_in_fwd(x1, gm, w_in)
        fb = jnp.pad(W["forget_bias"][l], (0, LANES - HEADS))[None, :]
        Fc = _fgate_fwd(zf, fb)[:, :HEADS]
        fq = Fc.T[:, :, None]
        fk = Fc.T.reshape(HEADS, T // tk, tk)
        qh, kh, vh = _heads(q), _heads(k), _heads(v)
        o, lse = _attn_fwd(qh, kh, vh, fq, fk)
        bd = _block_diag(W["pool_w"][l]).astype(MM)
        ps, cb, lg, lb = (W[n][l][None, :] for n in ("pool_scale", "conv_b", "conv_ln_g", "conv_ln_b"))
        cw = jnp.pad(W["conv_w"][l], ((0, 1), (0, 0)))
        ya, yc, cu, cy = _local_fwd(up, ca, cg, bd, ps, cw, cb, lg, lb)
        yb = _unheads(o)
        x2 = _mix_out_fwd(x1, ya, yb, yc, W["w_out"][l])
        x3 = _ffn_fwd(x2, g2, W["ffn2_w_gate"][l], W["ffn2_w_up"][l], W["ffn2_w_down"][l])
        saved.append(dict(x0=x, x1=x1, x2=x2, w_in=w_in, up=up, ca=ca, cg=cg, zf=zf, fb=fb, fq=fq, fk=fk,
                          qh=qh, kh=kh, vh=vh, o=o, lse=lse, bd=bd, cw=cw, cu=cu, cy=cy, ya=ya, yb=yb, yc=yc))
        x = x3

    loss, dx, dgf = _head(x, W["final_norm"][None, :], target)
    grads = {n: [None] * L for n in W if n != "final_norm"}
    grads["final_norm"] = dgf[0]
    for l in reversed(range(L)):
        s = saved[l]
        g1, gm, g2 = (W[n][l][None, :] for n in ("ffn1_norm", "mix_norm", "ffn2_norm"))
        ps, lg, lb = (W[n][l][None, :] for n in ("pool_scale", "conv_ln_g", "conv_ln_b"))
        dx, h, dy, da, db, sact, dg = _ffn_bwd(s["x2"], dx, g2, W["ffn2_w_gate"][l], W["ffn2_w_up"][l], W["ffn2_w_down"][l])
        grads["ffn2_norm"][l] = dg[0]
        grads["ffn2_w_gate"][l] = _wgrad(h, da, "wgrad_gate")
        grads["ffn2_w_up"][l] = _wgrad(h, db, "wgrad_up")
        grads["ffn2_w_down"][l] = _wgrad(sact, dy, "wgrad_down")
        dya, dyb, dyc = _mix_out_bwd(dx, W["w_out"][l])
        grads["w_out"][l] = jnp.concatenate(
            [_wgrad(s["ya"], dx, "wgrad_out_a"), _wgrad(s["yb"], dx, "wgrad_out_b"), _wgrad(s["yc"], dx, "wgrad_out_c")], axis=0)
        dq, dk, dv, dfq, dfk = _attn_bwd(s["qh"], s["kh"], s["vh"], s["fq"], s["fk"], s["o"], s["lse"], _heads(dyb))
        pad_heads = lambda a: jnp.pad(a.reshape(HEADS, T).T, ((0, 0), (0, LANES - HEADS)))
        dzf, dfb = _fgate_bwd(s["zf"], s["fb"], pad_heads(dfq), pad_heads(dfk))
        grads["forget_bias"][l] = dfb[0, :HEADS]
        dup, dca, dcg, dbd, dps, dcw, dcb, dlg, dlb = _local_bwd(
            s["up"], dya, s["ca"], s["cg"], s["cu"], s["cy"], dyc, s["bd"], ps, s["cw"], lg, lb)
        grads["pool_w"][l] = jnp.stack([dbd[64 * i:64 * i + 64, 64 * i:64 * i + 64] for i in range(4)])
        grads["pool_scale"][l], grads["conv_b"][l] = dps[0], dcb[0]
        grads["conv_ln_g"][l], grads["conv_ln_b"][l] = dlg[0], dlb[0]
        grads["conv_w"][l] = dcw[:CONV_K]
        dx, h, dp, dg = _mix_in_bwd(s["x1"], dx, gm, s["w_in"], dup, _unheads(dq), _unheads(dk), _unheads(dv),
                                    dca, dcg, dzf)
        grads["mix_norm"][l] = dg[0]
        grads["w_in"][l] = _unpad_w_in(_wgrad(h, dp, "wgrad_in"))
        dx, h, dy, da, db, sact, dg = _ffn_bwd(s["x0"], dx, g1, W["ffn1_w_gate"][l], W["ffn1_w_up"][l], W["ffn1_w_down"][l])
        grads["ffn1_norm"][l] = dg[0]
        grads["ffn1_w_gate"][l] = _wgrad(h, da, "wgrad_gate")
        grads["ffn1_w_up"][l] = _wgrad(h, db, "wgrad_up")
        grads["ffn1_w_down"][l] = _wgrad(sact, dy, "wgrad_down")
    grads = {n: (jnp.stack(g) if isinstance(g, list) else g) for n, g in grads.items()}
    return loss, dx, grads


NAMES = ("ffn1_norm", "ffn1_w_gate", "ffn1_w_up", "ffn1_w_down", "mix_norm", "w_in", "pool_w", "pool_scale",
         "forget_bias", "conv_w", "conv_b", "conv_ln_g", "conv_ln_b", "w_out", "ffn2_norm", "ffn2_w_gate",
         "ffn2_w_up", "ffn2_w_down", "final_norm")


def kernel(x, ffn1_norm, ffn1_w_gate, ffn1_w_up, ffn1_w_down, mix_norm, w_in, pool_w, pool_scale, forget_bias, conv_w, conv_b, conv_ln_g, conv_ln_b, w_out, ffn2_norm, ffn2_w_gate, ffn2_w_up, ffn2_w_down, final_norm, loss_target, m_ffn1_norm, m_ffn1_w_gate, m_ffn1_w_up, m_ffn1_w_down, m_mix_norm, m_w_in, m_pool_w, m_pool_scale, m_forget_bias, m_conv_w, m_conv_b, m_conv_ln_g, m_conv_ln_b, m_w_out, m_ffn2_norm, m_ffn2_w_gate, m_ffn2_w_up, m_ffn2_w_down, m_final_norm, v_ffn1_norm, v_ffn1_w_gate, v_ffn1_w_up, v_ffn1_w_down, v_mix_norm, v_w_in, v_pool_w, v_pool_scale, v_forget_bias, v_conv_w, v_conv_b, v_conv_ln_g, v_conv_ln_b, v_w_out, v_ffn2_norm, v_ffn2_w_gate, v_ffn2_w_up, v_ffn2_w_down, v_final_norm):
    args = (ffn1_norm, ffn1_w_gate, ffn1_w_up, ffn1_w_down, mix_norm, w_in, pool_w, pool_scale, forget_bias, conv_w, conv_b, conv_ln_g, conv_ln_b, w_out, ffn2_norm, ffn2_w_gate, ffn2_w_up, ffn2_w_down, final_norm)
    ms = (m_ffn1_norm, m_ffn1_w_gate, m_ffn1_w_up, m_ffn1_w_down, m_mix_norm, m_w_in, m_pool_w, m_pool_scale, m_forget_bias, m_conv_w, m_conv_b, m_conv_ln_g, m_conv_ln_b, m_w_out, m_ffn2_norm, m_ffn2_w_gate, m_ffn2_w_up, m_ffn2_w_down, m_final_norm)
    vs = (v_ffn1_norm, v_ffn1_w_gate, v_ffn1_w_up, v_ffn1_w_down, v_mix_norm, v_w_in, v_pool_w, v_pool_scale, v_forget_bias, v_conv_w, v_conv_b, v_conv_ln_g, v_conv_ln_b, v_w_out, v_ffn2_norm, v_ffn2_w_gate, v_ffn2_w_up, v_ffn2_w_down, v_final_norm)
    P = dict(zip(NAMES, args))
    M = dict(zip(NAMES, ms))
    V = dict(zip(NAMES, vs))
    xi, yi, _ = _position()
    chip = 2 * xi + yi

    big_shapes = [P[n].shape for n in BIG]
    gathered = _gather_chips(_pack([P[n].astype(MM) for n in BIG]), "gather_weights")
    W = {n: P[n] for n in SMALL}
    for n, sh in zip(BIG, _unpack_chips(gathered, big_shapes)):
        W[n] = _from_shards(n, sh)
    cw_rows = _pack_small([P["conv_w"]])
    cw_all = _gather_chips(cw_rows, "gather_conv_w")
    cw_sh = jnp.stack([_unpack_small(cw_all[j], [P["conv_w"].shape])[0] for j in range(4)])
    W["conv_w"] = cw_sh.transpose(1, 2, 0, 3).reshape(P["conv_w"].shape[0], CONV_K, 256)

    loss_part, dx, G = _forward_backward(x[0], loss_target[0], W)
    loss = lax.psum(loss_part[0, 0], ("x", "y", "c"))

    small_shapes = [P[n].shape for n in SMALL] + [G["conv_w"].shape]
    small_parts = _gather_all(_pack_small([G[n] for n in SMALL] + [G["conv_w"]]), "gather_small_grads")
    small_sum = _sum8(small_parts, "sum_small_grads")
    nsmall = sum(-(-math.prod(s) // LANES) for s in small_shapes[:-1])
    nsmall_pad = nsmall + (-nsmall % 8)
    w_s, m_s, v_s = (_pack_small([D[n] for n in SMALL]) for D in (P, M, V))
    outs_small = _adamw(w_s, [small_sum[:nsmall_pad]], m_s, v_s, "adamw_small")
    res = {}
    for kind, buf in zip(("g", "d", "m", "v"), outs_small):
        for n, a in zip(SMALL, _unpack_small(buf, small_shapes[:-1])):
            res[(kind, n)] = a
    g_cw_full = _unpack_small(small_sum[nsmall:], [small_shapes[-1]])[0]
    g_cw = lax.dynamic_slice_in_dim(g_cw_full, chip * 64, 64, axis=2)
    outs_cw = _adamw(_pack_small([P["conv_w"]]), [_pack_small([g_cw])], _pack_small([M["conv_w"]]),
                     _pack_small([V["conv_w"]]), "adamw_conv_w")
    for kind, buf in zip(("g", "d", "m", "v"), outs_cw):
        res[(kind, "conv_w")] = _unpack_small(buf, [P["conv_w"].shape])[0]

    g32 = jnp.stack([_pack([_to_shards(n, G[n])[j] for n in BIG]) for j in range(4)])
    recv = _scatter_chips(g32.astype(MM), "scatter_grads")
    own = lax.dynamic_index_in_dim(g32, chip, axis=0, keepdims=False)
    part = _sum_parts(own, recv, "sum_chip_grads")
    other = _swap_cores(part, "swap_core_grads")
    for n, ga, gb in zip(BIG, _unpack(part, big_shapes), _unpack(other, big_shapes)):
        shp = P[n].shape
        two_d = (shp[0] * shp[1], shp[2])
        outs = _adamw(P[n].reshape(two_d), [ga.reshape(two_d), gb.reshape(two_d)],
                      M[n].reshape(two_d), V[n].reshape(two_d), "adamw_" + n)
        for kind, a in zip(("g", "d", "m", "v"), outs):
            res[(kind, n)] = a.reshape(shp)

    return (loss, dx[None], *[res[("g", n)] for n in NAMES], *[res[("d", n)] for n in NAMES],
            *[res[("m", n)] for n in NAMES], *[res[("v", n)] for n in NAMES])


def _unpack_chips(buf, shapes):
    out, off = [], 0
    for shp in shapes:
        n = math.prod(shp) // PACK_COLS
        out.append(buf[:, off:off + n].reshape((4,) + tuple(shp)))
        off += n
    return out
```

```python
import functools
import math

import jax
import jax.numpy as jnp
from jax import lax
from jax.experimental import pallas as pl
from jax.experimental.pallas import tpu as pltpu

F32 = jnp.float32
MM = jnp.bfloat16
NORM_EPS = 1e-6
HEADS = 8
HEAD_DIM = 64
POOL_WINDOWS = (2, 4, 8, 16)
CONV_K = 31
LANES = 128
VMEM_LIMIT = 56 * 2**20

ADAM_LR = 0.001
ADAM_B1 = 0.9
ADAM_B2 = 0.999
ADAM_EPS = 1e-08
ADAM_WD = 0.01
ADAM_STEP = 10

MESH = pl.DeviceIdType.MESH
BS = pl.BlockSpec
SDS = jax.ShapeDtypeStruct
ANY = pl.BlockSpec(memory_space=pl.ANY)


def _dot(a, b):
    return jnp.dot(a, b, preferred_element_type=F32)


def _dot_nt(a, b):
    return lax.dot_general(a, b, (((1,), (1,)), ((), ())), preferred_element_type=F32)


def _dot_tn(a, b):
    return lax.dot_general(a, b, (((0,), (0,)), ((), ())), preferred_element_type=F32)


def _pc(body, name, grid, in_specs, out_specs, out_shape, scratch=()):
    return pl.pallas_call(
        body, out_shape=out_shape, grid=grid, in_specs=in_specs, out_specs=out_specs,
        scratch_shapes=list(scratch), name=name,
        compiler_params=pltpu.CompilerParams(
            dimension_semantics=("arbitrary",) * len(grid), vmem_limit_bytes=VMEM_LIMIT))


def _rms_fwd(x, g):
    r = lax.rsqrt(jnp.mean(x * x, axis=-1, keepdims=True) + NORM_EPS)
    xh = x * r
    return xh, r, xh * g


def _rms_bwd(dh, xh, r, g):
    dxh = dh * g
    dx = r * (dxh - xh * jnp.mean(dxh * xh, axis=-1, keepdims=True))
    return dx, jnp.sum(dh * xh, axis=0, keepdims=True)


def _sigmoid(x):
    return jax.nn.sigmoid(x)


def _ffn_fwd(x, g, wg, wu, wd):
    T, D = x.shape
    F = wg.shape[1]
    tm, nf = min(512, T), 2
    fc = F // nf

    def body(x_ref, g_ref, wg_ref, wu_ref, wd_ref, o_ref, h_scr, acc_scr):
        j = pl.program_id(1)

        @pl.when(j == 0)
        def _():
            _, _, hg = _rms_fwd(x_ref[...], g_ref[...])
            h_scr[...] = hg.astype(h_scr.dtype)
            acc_scr[...] = jnp.zeros_like(acc_scr)

        h = h_scr[...]
        a = _dot(h, wg_ref[...])
        b = _dot(h, wu_ref[...])
        s = (a * _sigmoid(a)) * b
        acc_scr[...] += _dot(s.astype(MM), wd_ref[...])

        @pl.when(j == nf - 1)
        def _():
            o_ref[...] = x_ref[...] + 0.5 * acc_scr[...]

    return _pc(
        body, "ffn_fwd", (T // tm, nf),
        [BS((tm, D), lambda i, j: (i, 0)), BS((1, D), lambda i, j: (0, 0)),
         BS((D, fc), lambda i, j: (0, j)), BS((D, fc), lambda i, j: (0, j)), BS((fc, D), lambda i, j: (j, 0))],
        BS((tm, D), lambda i, j: (i, 0)), SDS((T, D), F32),
        scratch=[pltpu.VMEM((tm, D), MM), pltpu.VMEM((tm, D), F32)])(x, g, wg, wu, wd)


def _ffn_bwd(x, dout, g, wg, wu, wd):
    T, D = x.shape
    F = wg.shape[1]
    tm, nf = min(256, T), 2
    fc = F // nf

    def body(x_ref, do_ref, g_ref, wg_ref, wu_ref, wd_ref,
             dx_ref, h_ref, dy_ref, da_ref, db_ref, s_ref, dg_ref, dh_scr):
        i, j = pl.program_id(0), pl.program_id(1)

        @pl.when(j == 0)
        def _():
            _, _, hg = _rms_fwd(x_ref[...], g_ref[...])
            h_ref[...] = hg.astype(h_ref.dtype)
            dy_ref[...] = (0.5 * do_ref[...]).astype(dy_ref.dtype)
            dh_scr[...] = jnp.zeros_like(dh_scr)

        @pl.when((i == 0) & (j == 0))
        def _():
            dg_ref[...] = jnp.zeros_like(dg_ref)

        h = h_ref[...]
        dy = dy_ref[...]
        a = _dot(h, wg_ref[...])
        b = _dot(h, wu_ref[...])
        ds = _dot_nt(dy, wd_ref[...])
        sig = _sigmoid(a)
        sl = a * sig
        s_ref[...] = (sl * b).astype(s_ref.dtype)
        db = (ds * sl).astype(MM)
        da = (ds * b * (sig * (1.0 + a * (1.0 - sig)))).astype(MM)
        da_ref[...] = da
        db_ref[...] = db
        dh_scr[...] += _dot_nt(da, wg_ref[...]) + _dot_nt(db, wu_ref[...])

        @pl.when(j == nf - 1)
        def _():
            gv = g_ref[...]
            xh, r, _ = _rms_fwd(x_ref[...], gv)
            dx, dg = _rms_bwd(dh_scr[...], xh, r, gv)
            dx_ref[...] = do_ref[...] + dx
            dg_ref[...] += dg

    tok = lambda i, j: (i, 0)
    return _pc(
        body, "ffn_bwd", (T // tm, nf),
        [BS((tm, D), tok), BS((tm, D), tok), BS((1, D), lambda i, j: (0, 0)),
         BS((D, fc), lambda i, j: (0, j)), BS((D, fc), lambda i, j: (0, j)), BS((fc, D), lambda i, j: (j, 0))],
        [BS((tm, D), tok), BS((tm, D), tok), BS((tm, D), tok),
         BS((tm, fc), lambda i, j: (i, j)), BS((tm, fc), lambda i, j: (i, j)), BS((tm, fc), lambda i, j: (i, j)),
         BS((1, D), lambda i, j: (0, 0))],
        [SDS((T, D), F32), SDS((T, D), MM), SDS((T, D), MM),
         SDS((T, F), MM), SDS((T, F), MM), SDS((T, F), MM), SDS((1, D), F32)],
        scratch=[pltpu.VMEM((tm, D), F32)])(x, dout, g, wg, wu, wd)


def _wgrad(a, b, name):
    T, K = a.shape
    N = b.shape[1]
    tt = min(512, T)
    tn = N
    for cand in (1408, 1280, 1024, 512, 256, 128):
        if N % cand == 0 and K * cand * 4 <= 6 * 2**20:
            tn = cand
            break

    def body(a_ref, b_ref, o_ref):
        @pl.when(pl.program_id(1) == 0)
        def _():
            o_ref[...] = jnp.zeros_like(o_ref)

        o_ref[...] += _dot_tn(a_ref[...].astype(MM), b_ref[...].astype(MM))

    return _pc(
        body, name, (N // tn, T // tt),
        [BS((tt, K), lambda n, t: (t, 0)), BS((tt, tn), lambda n, t: (t, n))],
        BS((K, tn), lambda n, t: (0, n)), SDS((K, N), F32))(a, b)


C_POOL, C_Q, C_K, C_V, C_CA, C_CG, C_ZF, C_END = 0, 256, 768, 1280, 1792, 2048, 2304, 2560


def _mix_in_fwd(x, g, w):
    T, D = x.shape
    tm = min(512, T)

    def body(x_ref, g_ref, w_ref, up_ref, q_ref, k_ref, v_ref, ca_ref, cg_ref, zf_ref):
        _, _, hg = _rms_fwd(x_ref[...], g_ref[...])
        p = _dot(hg.astype(MM), w_ref[...])
        up_ref[...] = p[:, C_POOL:C_Q]
        q_ref[...] = p[:, C_Q:C_K].astype(q_ref.dtype)
        k_ref[...] = p[:, C_K:C_V].astype(k_ref.dtype)
        v_ref[...] = p[:, C_V:C_CA].astype(v_ref.dtype)
        ca_ref[...] = p[:, C_CA:C_CG]
        cg_ref[...] = p[:, C_CG:C_ZF]
        zf_ref[...] = p[:, C_ZF:C_ZF + LANES]

    tok = lambda i: (i, 0)
    widths = (256, 512, 512, 512, 256, 256, 128)
    dtypes = (F32, MM, MM, MM, F32, F32, F32)
    return _pc(
        body, "mix_in_fwd", (T // tm,),
        [BS((tm, D), tok), BS((1, D), lambda i: (0, 0)), BS((D, C_END), lambda i: (0, 0))],
        [BS((tm, wd), tok) for wd in widths],
        [SDS((T, wd), dt) for wd, dt in zip(widths, dtypes)])(x, g, w)


def _mix_in_bwd(x, dout, g, w, dup, dq, dk, dv, dca, dcg, dzf):
    T, D = x.shape
    tm = min(512, T)

    def body(x_ref, do_ref, g_ref, w_ref, dup_ref, dq_ref, dk_ref, dv_ref, dca_ref, dcg_ref, dzf_ref,
             dx_ref, h_ref, dp_ref, dg_ref):
        @pl.when(pl.program_id(0) == 0)
        def _():
            dg_ref[...] = jnp.zeros_like(dg_ref)

        gv = g_ref[...]
        xh, r, hg = _rms_fwd(x_ref[...], gv)
        h_ref[...] = hg.astype(h_ref.dtype)
        for ref, lo, hi in ((dup_ref, C_POOL, C_Q), (dq_ref, C_Q, C_K), (dk_ref, C_K, C_V), (dv_ref, C_V, C_CA),
                            (dca_ref, C_CA, C_CG), (dcg_ref, C_CG, C_ZF), (dzf_ref, C_ZF, C_ZF + LANES)):
            dp_ref[:, lo:hi] = ref[...].astype(dp_ref.dtype)
        dp_ref[:, C_ZF + LANES:C_END] = jnp.zeros((tm, C_END - C_ZF - LANES), dp_ref.dtype)
        dh = _dot_nt(dp_ref[...], w_ref[...])
        dx, dg = _rms_bwd(dh, xh, r, gv)
        dx_ref[...] = do_ref[...] + dx
        dg_ref[...] += dg

    tok = lambda i: (i, 0)
    widths = (256, 512, 512, 512, 256, 256, 128)
    return _pc(
        body, "mix_in_bwd", (T // tm,),
        [BS((tm, D), tok), BS((tm, D), tok), BS((1, D), lambda i: (0, 0)), BS((D, C_END), lambda i: (0, 0))]
        + [BS((tm, wd), tok) for wd in widths],
        [BS((tm, D), tok), BS((tm, D), tok), BS((tm, C_END), tok), BS((1, D), lambda i: (0, 0))],
        [SDS((T, D), F32), SDS((T, D), MM), SDS((T, C_END), MM), SDS((1, D), F32)],
    )(x, dout, g, w, dup, dq, dk, dv, dca, dcg, dzf)


def _mix_out_fwd(x, ya, yb, yc, wo):
    T, D = x.shape
    tm = min(512, T)

    def body(x_ref, ya_ref, yb_ref, yc_ref, wo_ref, o_ref):
        o_ref[...] = (x_ref[...] + _dot(ya_ref[...].astype(MM), wo_ref[0:256, :])
                      + _dot(yb_ref[...].astype(MM), wo_ref[256:768, :])
                      + _dot(yc_ref[...].astype(MM), wo_ref[768:1024, :]))

    tok = lambda i: (i, 0)
    return _pc(
        body, "mix_out_fwd", (T // tm,),
        [BS((tm, D), tok), BS((tm, 256), tok), BS((tm, 512), tok), BS((tm, 256), tok), BS((D, D), lambda i: (0, 0))],
        BS((tm, D), tok), SDS((T, D), F32))(x, ya, yb, yc, wo)


def _mix_out_bwd(dx, wo):
    T, D = dx.shape
    tm = min(512, T)

    def body(dx_ref, wo_ref, dya_ref, dyb_ref, dyc_ref):
        dy = _dot_nt(dx_ref[...].astype(MM), wo_ref[...])
        dya_ref[...] = dy[:, 0:256]
        dyb_ref[...] = dy[:, 256:768]
        dyc_ref[...] = dy[:, 768:1024]

    tok = lambda i: (i, 0)
    return _pc(
        body, "mix_out_bwd", (T // tm,),
        [BS((tm, D), tok), BS((D, D), lambda i: (0, 0))],
        [BS((tm, 256), tok), BS((tm, 512), tok), BS((tm, 256), tok)],
        [SDS((T, 256), F32), SDS((T, 512), F32), SDS((T, 256), F32)])(dx, wo)


def _fgate_fwd(zf, bias):
    T = zf.shape[0]
    tc = min(256, T)

    def body(z_ref, b_ref, f_ref, carry):
        @pl.when(pl.program_id(0) == 0)
        def _():
            carry[...] = jnp.zeros_like(carry)

        z = z_ref[...] + b_ref[...]
        logf = jnp.minimum(z, 0.0) - jnp.log(1.0 + jnp.exp(-jnp.abs(z)))
        row = lax.broadcasted_iota(jnp.int32, (tc, tc), 0)
        col = lax.broadcasted_iota(jnp.int32, (tc, tc), 1)
        tri = (col <= row).astype(F32)
        f_ref[...] = jnp.dot(tri, logf, precision=lax.Precision.HIGHEST, preferred_element_type=F32) + carry[...]
        carry[...] += jnp.sum(logf, axis=0, keepdims=True)

    return _pc(
        body, "fgate_fwd", (T // tc,),
        [BS((tc, LANES), lambda i: (i, 0)), BS((1, LANES), lambda i: (0, 0))],
        BS((tc, LANES), lambda i: (i, 0)), SDS((T, LANES), F32),
        scratch=[pltpu.VMEM((1, LANES), F32)])(zf, bias)


def _fgate_bwd(zf, bias, dFq, dFk):
    T = zf.shape[0]
    tc = min(256, T)
    n = T // tc
    slabs = dFq.shape[0]

    def body(z_ref, b_ref, dfq_ref, dfk_ref, dz_ref, db_ref, carry):
        @pl.when(pl.program_id(0) == 0)
        def _():
            carry[...] = jnp.zeros_like(carry)
            db_ref[...] = jnp.zeros_like(db_ref)

        df = dfk_ref[...]
        for sl in range(slabs):
            df = df + dfq_ref[sl]
        row = lax.broadcasted_iota(jnp.int32, (tc, tc), 0)
        col = lax.broadcasted_iota(jnp.int32, (tc, tc), 1)
        tri = (col >= row).astype(F32)
        dlogf = jnp.dot(tri, df, precision=lax.Precision.HIGHEST, preferred_element_type=F32) + carry[...]
        carry[...] += jnp.sum(df, axis=0, keepdims=True)
        lane = lax.broadcasted_iota(jnp.int32, (1, LANES), 1)
        dz = jnp.where(lane < HEADS, dlogf * _sigmoid(-(z_ref[...] + b_ref[...])), 0.0)
        dz_ref[...] = dz
        db_ref[...] += jnp.sum(dz, axis=0, keepdims=True)

    rev = lambda i: (n - 1 - i, 0)
    return _pc(
        body, "fgate_bwd", (n,),
        [BS((tc, LANES), rev), BS((1, LANES), lambda i: (0, 0)), BS((slabs, tc, LANES), lambda i: (0, n - 1 - i, 0)),
         BS((tc, LANES), rev)],
        [BS((tc, LANES), rev), BS((1, LANES), lambda i: (0, 0))],
        [SDS((T, LANES), F32), SDS((1, LANES), F32)],
        scratch=[pltpu.VMEM((1, LANES), F32)])(zf, bias, dFq, dFk)


LOG2E = 1.4426950408889634


def _split3(x):
    hi = x.astype(MM)
    r1 = x - hi.astype(F32)
    mid = r1.astype(MM)
    return hi, mid, (r1 - mid.astype(F32)).astype(MM)


def _place(lane, base, cols):
    out = jnp.zeros((cols[0].shape[0], LANES), MM)
    for i, c in enumerate(cols):
        out = jnp.where(lane == base + i, c, out)
    return out


def _head_col(block, lane, h):
    return jnp.sum(jnp.where(lane == h, block, 0.0), axis=-1, keepdims=True)


def _own_lanes(lane, hh):
    return (lane < HEAD_DIM) if hh == 0 else (lane >= HEAD_DIM)


def _attn_k_side(k_ref, f_ref, kb_ref, hp, T, rows, lse_ones):
    lane = lax.broadcasted_iota(jnp.int32, (1, LANES), 1)
    one = jnp.ones((rows, 1), MM)

    def chunk(c, _):
        r0 = pl.multiple_of(c * rows, rows)
        kp = k_ref[pl.ds(r0, rows), :]
        fblk = f_ref[pl.ds(r0, rows), :]
        for hh in range(2):
            hi, mid, lo = _split3(-_head_col(fblk, lane, 2 * hp + hh) * LOG2E)
            cols = [one, one, one, hi, mid, lo] + ([one, one, one] if lse_ones else [])
            bias = _place(lane, HEAD_DIM * (1 - hh), cols)
            kb_ref[hh, pl.ds(r0, rows), :] = jnp.where(_own_lanes(lane, hh), kp, bias)
        return 0

    lax.fori_loop(0, T // rows, chunk, 0)


def _attn_q_side(qp, fblk, lane, hp, scale, lse_blk=None):
    qc = qp.astype(F32) * (scale * LOG2E)
    qhi = qc.astype(MM)
    qlo = (qc - qhi.astype(F32)).astype(MM)
    one = jnp.ones((qp.shape[0], 1), MM)
    out = []
    for hh in range(2):
        cols = list(_split3(_head_col(fblk, lane, 2 * hp + hh) * LOG2E)) + [one, one, one]
        if lse_blk is not None:
            cols += list(_split3(-_head_col(lse_blk, lane, 2 * hp + hh)))
        bias = _place(lane, HEAD_DIM * (1 - hh), cols)
        own = _own_lanes(lane, hh)
        out.append(jnp.concatenate([jnp.where(own, qhi, jnp.zeros_like(qhi)), jnp.where(own, qlo, bias)], axis=1))
    return out


def _causal(tq, tk):
    return lax.broadcasted_iota(jnp.int32, (tq, tk), 1) <= lax.broadcasted_iota(jnp.int32, (tq, tk), 0)


def _hosted_specs(hosted):
    if hosted is None:
        return [], [], [], []
    kind, data = hosted
    return [ANY], [ANY], [kind.out_shape(data)], list(kind.scratch)


def _hosted_edges(hosted, refs, n_in, n_out, first, last):
    if hosted is None:
        return refs, lambda: None
    kind = hosted[0]
    nsem = len(kind.scratch)
    cin, cout, sems = refs[n_in], refs[n_in + 1 + n_out], refs[len(refs) - nsem:]

    @pl.when(first)
    def _():
        kind.start(cin, cout, *sems)

    def finish():
        @pl.when(last)
        def _():
            kind.wait(cin, cout, *sems)

    rest = refs[:n_in] + refs[n_in + 1:n_in + 1 + n_out] + refs[n_in + 2 + n_out:len(refs) - nsem]
    return rest, finish


def _attn_fwd(q, k, v, F, hosted=None):
    T = q.shape[0]
    tq = min(512, T)
    tk = tq
    nq = T // tq
    scale = 1.0 / math.sqrt(HEAD_DIM)
    h_in, h_out, h_shape, h_scratch = _hosted_specs(hosted)

    def body(*refs):
        hp, ib = pl.program_id(0), pl.program_id(1)
        refs, finish = _hosted_edges(hosted, refs, 5, 2, (hp == 0) & (ib == 0), (hp == HEADS // 2 - 1) & (ib == nq - 1))
        q_ref, k_ref, v_ref, fq_ref, f_ref, o_ref, lse_ref, kb_ref = refs
        lane = lax.broadcasted_iota(jnp.int32, (1, LANES), 1)

        @pl.when(ib == 0)
        def _():
            _attn_k_side(k_ref, f_ref, kb_ref, hp, T, tk, False)

        qa = _attn_q_side(q_ref[...], fq_ref[...], lane, hp, scale)

        def tile(jb, carry, masked):
            off = pl.multiple_of(jb * tk, tk)
            kp = k_ref[pl.ds(off, tk), :]
            vp = v_ref[pl.ds(off, tk), :]
            new = []
            for hh in range(2):
                m, l, acc = carry[hh]
                s = _dot_nt(qa[hh], jnp.concatenate([kp, kb_ref[hh, pl.ds(off, tk), :]], axis=1))
                if masked:
                    s = jnp.where(_causal(tq, tk), s, -jnp.inf)
                m2 = jnp.maximum(m, jnp.max(s, axis=-1, keepdims=True))
                p = jnp.exp2(s - m2)
                al = jnp.exp2(m - m2)
                new.append((m2, l * al + jnp.sum(p, axis=-1, keepdims=True), acc * al + _dot(p.astype(MM), vp)))
            return tuple(new)

        init = tuple((jnp.full((tq, 1), -jnp.inf, F32), jnp.zeros((tq, 1), F32), jnp.zeros((tq, LANES), F32))
                     for _ in range(2))
        carry = lax.fori_loop(0, ib, lambda jb, c: tile(jb, c, False), init)
        (m0, l0, a0), (m1, l1, a1) = tile(ib, carry, True)
        o_ref[...] = jnp.where(lane < HEAD_DIM, a0 / l0, a1 / l1)
        lse_ref[...] = jnp.where(lane == 2 * hp, m0 + jnp.log2(l0), jnp.where(lane == 2 * hp + 1, m1 + jnp.log2(l1), 0.0))
        finish()

    blk = lambda h, i: (i, h)
    full = lambda h, i: (0, h)
    return _pc(
        body, "attn_fwd" + ("_hosting" if hosted else ""), (HEADS // 2, nq),
        [BS((tq, LANES), blk), BS((T, LANES), full), BS((T, LANES), full), BS((tq, LANES), lambda h, i: (i, 0)),
         BS((T, LANES), lambda h, i: (0, 0))] + h_in,
        [BS((tq, LANES), blk), BS((None, tq, LANES), lambda h, i: (h, i, 0))] + h_out,
        [SDS((T, HEADS * HEAD_DIM), F32), SDS((HEADS // 2, T, LANES), F32)] + h_shape,
        scratch=[pltpu.VMEM((2, T, LANES), MM)] + h_scratch)(q, k, v, F, F, *([hosted[1]] if hosted else []))


def _attn_bwd(q, k, v, F, o, lse, do, hosted=None):
    T = q.shape[0]
    tq = min(512, T)
    tk = tq
    nq = T // tq
    scale = 1.0 / math.sqrt(HEAD_DIM)
    h_in, h_out, h_shape, h_scratch = _hosted_specs(hosted)

    def body(*refs):
        hp, ib = pl.program_id(0), pl.program_id(1)
        refs, finish = _hosted_edges(hosted, refs, 8, 5, (hp == 0) & (ib == 0), (hp == HEADS // 2 - 1) & (ib == nq - 1))
        (q_ref, k_ref, v_ref, fq_ref, f_ref, o_ref, lse_ref, do_ref,
         dq_ref, dk_ref, dv_ref, dfq_ref, dfk_ref, kb_ref) = refs
        lane = lax.broadcasted_iota(jnp.int32, (1, LANES), 1)

        @pl.when(ib == 0)
        def _():
            _attn_k_side(k_ref, f_ref, kb_ref, hp, T, tk, True)
            dk_ref[...] = jnp.zeros_like(dk_ref)
            dv_ref[...] = jnp.zeros_like(dv_ref)
            dfk_ref[...] = jnp.zeros_like(dfk_ref)

        qp = q_ref[...]
        qa = _attn_q_side(qp, fq_ref[...], lane, hp, scale, lse_ref[...])
        dob = do_ref[...].astype(MM)
        dprod = dob.astype(F32) * o_ref[...]
        qs = (qp.astype(F32) * scale).astype(MM)
        heads = []
        for hh in range(2):
            own = _own_lanes(lane, hh)
            heads.append((jnp.where(own, dob, jnp.zeros_like(dob)), jnp.where(own, qs, jnp.zeros_like(qs)),
                          jnp.sum(jnp.where(own, dprod, 0.0), axis=-1, keepdims=True)))

        def tile(jb, carry, masked):
            off = pl.multiple_of(jb * tk, tk)
            kp = k_ref[pl.ds(off, tk), :]
            vp = v_ref[pl.ds(off, tk), :]
            new = []
            dv_t = jnp.zeros((tk, LANES), F32)
            dk_t = jnp.zeros((tk, LANES), F32)
            for hh in range(2):
                dq, rs = carry[hh]
                dom, qm, delta = heads[hh]
                p = jnp.exp2(_dot_nt(qa[hh], jnp.concatenate([kp, kb_ref[hh, pl.ds(off, tk), :]], axis=1)))
                if masked:
                    p = jnp.where(_causal(tq, tk), p, 0.0)
                ds = p * (_dot_nt(dom, vp) - delta)
                dsb = ds.astype(MM)
                dv_t = dv_t + _dot_tn(p.astype(MM), dom)
                dk_t = dk_t + _dot_tn(dsb, qm)
                dfk_ref[jb, pl.ds(hh, 1), :] -= jnp.sum(ds, axis=0, keepdims=True)
                new.append((dq + _dot(dsb, kp), rs + jnp.sum(ds, axis=-1, keepdims=True)))
            dv_ref[pl.ds(off, tk), :] += dv_t
            dk_ref[pl.ds(off, tk), :] += dk_t
            return tuple(new)

        init = tuple((jnp.zeros((tq, LANES), F32), jnp.zeros((tq, 1), F32)) for _ in range(2))
        carry = lax.fori_loop(0, ib, lambda jb, c: tile(jb, c, False), init)
        (dq0, rs0), (dq1, rs1) = tile(ib, carry, True)
        dq_ref[...] = jnp.where(lane < HEAD_DIM, dq0, dq1) * scale
        dfq_ref[...] = jnp.where(lane == 2 * hp, rs0, jnp.where(lane == 2 * hp + 1, rs1, 0.0))
        finish()

    blk = lambda h, i: (i, h)
    full = lambda h, i: (0, h)
    slab = BS((None, tq, LANES), lambda h, i: (h, i, 0))
    return _pc(
        body, "attn_bwd" + ("_hosting" if hosted else ""), (HEADS // 2, nq),
        [BS((tq, LANES), blk), BS((T, LANES), full), BS((T, LANES), full), BS((tq, LANES), lambda h, i: (i, 0)),
         BS((T, LANES), lambda h, i: (0, 0)), BS((tq, LANES), blk), slab, BS((tq, LANES), blk)] + h_in,
        [BS((tq, LANES), blk), BS((T, LANES), full), BS((T, LANES), full), slab,
         BS((None, nq, 2, tk), lambda h, i: (h, 0, 0, 0))] + h_out,
        [SDS((T, HEADS * HEAD_DIM), F32)] * 3 + [SDS((HEADS // 2, T, LANES), F32), SDS((HEADS // 2, nq, 2, tk), F32)]
        + h_shape,
        scratch=[pltpu.VMEM((2, T, LANES), MM)] + h_scratch,
    )(q, k, v, F, F, o, lse, do, *([hosted[1]] if hosted else []))


POOL_HALO = 16
CONV_HALO = 32


def _group_select(lane, v0, v1, v2, v3):
    return jnp.where(lane < 64, v0, jnp.where(lane < 128, v1, jnp.where(lane < 192, v2, v3)))


def _roll_down(x, k):
    return x if k == 0 else pltpu.roll(x, k, 0)


def _roll_up(x, k):
    return x if k == 0 else pltpu.roll(x, x.shape[0] - k, 0)


def _pool_terms(u, u_prev, tile, tm):
    ext = jnp.concatenate([u_prev, u], axis=0)
    s2 = ext + _roll_down(ext, 1)
    s4 = s2 + _roll_down(s2, 2)
    s8 = s4 + _roll_down(s4, 4)
    s16 = s8 + _roll_down(s8, 8)
    lane = lax.broadcasted_iota(jnp.int32, (1, 256), 1)
    ws = _group_select(lane, s2, s4, s8, s16)[POOL_HALO:, :]
    wlen = _group_select(lane, 2.0, 4.0, 8.0, 16.0).astype(F32)
    return ws / _pool_count(tile, tm, tm, wlen) - u


def _pool_count(tile, tm, rows, wlen):
    t = (tile * tm + 1 + lax.broadcasted_iota(jnp.int32, (rows, 1), 0)).astype(F32)
    return jnp.minimum(t, wlen)


def _layer_norm(y, lg, lb):
    mu = jnp.mean(y, axis=-1, keepdims=True)
    yc = y - mu
    rstd = lax.rsqrt(jnp.mean(yc * yc, axis=-1, keepdims=True) + NORM_EPS)
    yh = yc * rstd
    return yh, rstd, yh * lg + lb


def _halo_specs(tm, T, halo, prev):
    per = tm // halo
    if prev:
        return BS((halo, 256), lambda i: (jnp.maximum(i * per - 1, 0), 0))
    return BS((halo, 256), lambda i: (jnp.minimum((i + 1) * per, T // halo - 1), 0))


def _local_fwd(up, ca, cg, bd, pscale, cw, cb, lg, lb):
    T = up.shape[0]
    tm = min(512, T)

    def body(up_ref, uph_ref, ca_ref, cah_ref, cg_ref, cgh_ref, bd_ref, ps_ref, cw_ref, cb_ref, lg_ref, lb_ref,
             ya_ref, yc_ref, u_ref, y_ref):
        i = pl.program_id(0)
        first = i == 0
        pooled = _pool_terms(up_ref[...], jnp.where(first, 0.0, uph_ref[...]), i, tm)
        ya_ref[...] = (_dot(pooled.astype(MM), bd_ref[...]) * ps_ref[...]).astype(ya_ref.dtype)

        u = ca_ref[...] * _sigmoid(cg_ref[...])
        uh = jnp.where(first, 0.0, cah_ref[...] * _sigmoid(cgh_ref[...]))
        ext = jnp.concatenate([uh, u], axis=0)
        y = jnp.zeros((tm, 256), F32) + cb_ref[...]
        for kk in range(CONV_K):
            y = y + cw_ref[kk:kk + 1, :] * _roll_up(ext, CONV_HALO - (CONV_K - 1) + kk)[:tm, :]
        _, _, z = _layer_norm(y, lg_ref[...], lb_ref[...])
        yc_ref[...] = (z * _sigmoid(z)).astype(yc_ref.dtype)
        u_ref[...] = u
        y_ref[...] = y

    tok = lambda i: (i, 0)
    par = lambda i: (0, 0)
    t256 = BS((tm, 256), tok)
    return _pc(
        body, "local_fwd", (T // tm,),
        [t256, _halo_specs(tm, T, POOL_HALO, True), t256, _halo_specs(tm, T, CONV_HALO, True),
         t256, _halo_specs(tm, T, CONV_HALO, True),
         BS((256, 256), par), BS((1, 256), par), BS((32, 256), par), BS((1, 256), par), BS((1, 256), par),
         BS((1, 256), par)],
        [t256, t256, t256, t256],
        [SDS((T, 256), MM), SDS((T, 256), MM), SDS((T, 256), F32), SDS((T, 256), F32)],
    )(up, up, ca, ca, cg, cg, bd, pscale, cw, cb, lg, lb)


def _local_bwd(up, dya, ca, cg, u, y, dyc, bd, pscale, cw, lg, lb):
    T = up.shape[0]
    tm = min(512, T)
    n = T // tm

    def body(up_ref, uph_ref, dya_ref, dyan_ref, ca_ref, cg_ref, u_ref, uh_ref, y_ref, yn_ref, dyc_ref, dycn_ref,
             bd_ref, ps_ref, cw_ref, lg_ref, lb_ref,
             dup_ref, dca_ref, dcg_ref, dbd_ref, dps_ref, dcw_ref, dcb_ref, dlg_ref, dlb_ref):
        i = pl.program_id(0)
        first = i == 0
        last = i == n - 1

        @pl.when(first)
        def _():
            for ref in (dbd_ref, dps_ref, dcw_ref, dcb_ref, dlg_ref, dlb_ref):
                ref[...] = jnp.zeros_like(ref)

        ps = ps_ref[...]
        pooled = _pool_terms(up_ref[...], jnp.where(first, 0.0, uph_ref[...]), i, tm).astype(MM)
        dya_t = dya_ref[...]
        dps_ref[...] += jnp.sum(dya_t * _dot(pooled, bd_ref[...]), axis=0, keepdims=True)
        dm = (jnp.concatenate([dya_t, jnp.where(last, 0.0, dyan_ref[...])], axis=0) * ps).astype(MM)
        dbd_ref[...] += _dot_tn(pooled, dm[:tm, :])
        dpool = _dot_nt(dm, bd_ref[...])
        lane = lax.broadcasted_iota(jnp.int32, (1, 256), 1)
        wlen = _group_select(lane, 2.0, 4.0, 8.0, 16.0).astype(F32)
        e = dpool / _pool_count(i, tm, tm + POOL_HALO, wlen)
        f2 = e + _roll_up(e, 1)
        f4 = f2 + _roll_up(f2, 2)
        f8 = f4 + _roll_up(f4, 4)
        f16 = f8 + _roll_up(f8, 8)
        dup_ref[...] = _group_select(lane, f2, f4, f8, f16)[:tm, :] - dpool[:tm, :]

        lgv = lg_ref[...]
        yext = jnp.concatenate([y_ref[...], yn_ref[...]], axis=0)
        dyc = jnp.concatenate([dyc_ref[...], jnp.where(last, 0.0, dycn_ref[...])], axis=0)
        yh, rstd, z = _layer_norm(yext, lgv, lb_ref[...])
        sig = _sigmoid(z)
        dz = dyc * (sig * (1.0 + z * (1.0 - sig)))
        dlg_ref[...] += jnp.sum((dz * yh)[:tm, :], axis=0, keepdims=True)
        dlb_ref[...] += jnp.sum(dz[:tm, :], axis=0, keepdims=True)
        dyh = dz * lgv
        dy = rstd * (dyh - jnp.mean(dyh, axis=-1, keepdims=True) - yh * jnp.mean(dyh * yh, axis=-1, keepdims=True))
        dy_t = dy[:tm, :]
        dcb_ref[...] += jnp.sum(dy_t, axis=0, keepdims=True)
        uext = jnp.concatenate([jnp.where(first, 0.0, uh_ref[...]), u_ref[...]], axis=0)
        du = jnp.zeros((tm, 256), F32)
        for kk in range(CONV_K):
            shifted = _roll_up(uext, CONV_HALO - (CONV_K - 1) + kk)[:tm, :]
            dcw_ref[kk:kk + 1, :] += jnp.sum(dy_t * shifted, axis=0, keepdims=True)
            du = du + cw_ref[kk:kk + 1, :] * _roll_up(dy, CONV_K - 1 - kk)[:tm, :]
        sg = _sigmoid(cg_ref[...])
        dca_ref[...] = du * sg
        dcg_ref[...] = du * ca_ref[...] * sg * (1.0 - sg)

    tok = lambda i: (i, 0)
    par = lambda i: (0, 0)
    t256 = BS((tm, 256), tok)
    p1 = BS((1, 256), par)
    return _pc(
        body, "local_bwd", (n,),
        [t256, _halo_specs(tm, T, POOL_HALO, True), t256, _halo_specs(tm, T, POOL_HALO, False), t256, t256,
         t256, _halo_specs(tm, T, CONV_HALO, True), t256, _halo_specs(tm, T, CONV_HALO, False),
         t256, _halo_specs(tm, T, CONV_HALO, False),
         BS((256, 256), par), p1, BS((32, 256), par), p1, p1],
        [t256, t256, t256, BS((256, 256), par), p1, BS((32, 256), par), p1, p1, p1],
        [SDS((T, 256), F32)] * 3 + [SDS((256, 256), F32), SDS((1, 256), F32), SDS((32, 256), F32)]
        + [SDS((1, 256), F32)] * 3,
    )(up, up, dya, dya, ca, cg, u, u, y, y, dyc, dyc, bd, pscale, cw, lg, lb)


def _head(x, g, target):
    T, D = x.shape
    tm = min(512, T)

    def body(x_ref, g_ref, t_ref, loss_ref, dx_ref, dg_ref):
        @pl.when(pl.program_id(0) == 0)
        def _():
            loss_ref[...] = jnp.zeros_like(loss_ref)
            dg_ref[...] = jnp.zeros_like(dg_ref)

        gv = g_ref[...]
        xh, r, yv = _rms_fwd(x_ref[...], gv)
        err = yv - t_ref[...]
        loss_ref[...] += 0.5 * jnp.sum(jnp.mean(err * err, axis=-1, keepdims=True), axis=0, keepdims=True)
        dx, dg = _rms_bwd(err * (1.0 / D), xh, r, gv)
        dx_ref[...] = dx
        dg_ref[...] += dg

    tok = lambda i: (i, 0)
    par = lambda i: (0, 0)
    return _pc(
        body, "head", (T // tm,),
        [BS((tm, D), tok), BS((1, D), par), BS((tm, D), tok)],
        [BS((1, LANES), par), BS((tm, D), tok), BS((1, D), par)],
        [SDS((1, LANES), F32), SDS((T, D), F32), SDS((1, D), F32)])(x, g, target)


def _adamw(w, gs, m, v, name):
    R, C = w.shape
    tr = R
    for cand in (512, 256, 128, 64, 32, 16, 8):
        if R % cand == 0:
            tr = cand
            break
    ng = len(gs)

    def body(*refs):
        w_ref, g_refs, m_ref, v_ref = refs[0], refs[1:1 + ng], refs[1 + ng], refs[2 + ng]
        g_ref, d_ref, m2_ref, v2_ref = refs[3 + ng:]
        g = g_refs[0][...]
        for r in g_refs[1:]:
            g = g + r[...]
        m2 = ADAM_B1 * m_ref[...] + (1.0 - ADAM_B1) * g
        v2 = ADAM_B2 * v_ref[...] + (1.0 - ADAM_B2) * jnp.square(g)
        m_hat = m2 / (1.0 - ADAM_B1 ** ADAM_STEP)
        v_hat = v2 / (1.0 - ADAM_B2 ** ADAM_STEP)
        g_ref[...] = g
        d_ref[...] = -ADAM_LR * (m_hat / (jnp.sqrt(v_hat) + ADAM_EPS) + ADAM_WD * w_ref[...])
        m2_ref[...] = m2
        v2_ref[...] = v2

    blk = BS((tr, C), lambda i: (i, 0))
    return _pc(body, name, (R // tr,), [blk] * (3 + ng), [blk] * 4, [SDS((R, C), F32)] * 4)(w, *gs, m, v)


def _sum_parts(own, recv, name):
    R, C = own.shape
    tr = PACK_ROW_TILE

    def body(o_ref, r_ref, s_ref):
        s_ref[...] = ((o_ref[...] + r_ref[0].astype(F32)) + r_ref[1].astype(F32)) + r_ref[2].astype(F32)

    return _pc(body, name, (R // tr,),
               [BS((tr, C), lambda i: (i, 0)), BS((3, tr, C), lambda i: (0, i, 0))],
               BS((tr, C), lambda i: (i, 0)), SDS((R, C), F32))(own, recv)


def _sum8(parts, name):
    _, R, C = parts.shape

    def body(p_ref, s_ref):
        acc = p_ref[0]
        for d in range(1, 8):
            acc = acc + p_ref[d]
        s_ref[...] = acc

    return _pc(body, name, (1,), [BS((8, R, C), lambda i: (0, 0, 0))], BS((R, C), lambda i: (0, 0)),
               SDS((R, C), F32))(parts)


def _position():
    return lax.axis_index("x"), lax.axis_index("y"), lax.axis_index("c")


CHIP_FLIPS = ((1, 0), (0, 1), (1, 1))


class _GatherChips:
    scratch = (pltpu.SemaphoreType.DMA((3,)), pltpu.SemaphoreType.DMA((3,)), pltpu.SemaphoreType.DMA(()))

    @staticmethod
    def out_shape(block):
        return SDS((4,) + tuple(block.shape), block.dtype)

    @staticmethod
    def _copies(in_ref, out_ref, send_sems, recv_sems, local_sem, arrivals):
        x, y, c = _position()
        local = pltpu.make_async_copy(in_ref, out_ref.at[2 * x + y], local_sem)
        remote = []
        for k, (fx, fy) in enumerate(CHIP_FLIPS):
            slot = 2 * (x ^ fx) + (y ^ fy) if arrivals else 2 * x + y
            remote.append(pltpu.make_async_remote_copy(
                src_ref=in_ref, dst_ref=out_ref.at[slot], send_sem=send_sems.at[k], recv_sem=recv_sems.at[k],
                device_id=(x ^ fx, y ^ fy, c), device_id_type=MESH))
        return local, remote

    @classmethod
    def start(cls, *refs):
        local, sends = cls._copies(*refs, arrivals=False)
        local.start()
        for cp in sends:
            cp.start()

    @classmethod
    def wait(cls, *refs):
        local, arrivals = cls._copies(*refs, arrivals=True)
        for cp in arrivals:
            cp.wait_recv()
        for cp in arrivals:
            cp.wait_send()
        local.wait()


class _ScatterChips:
    scratch = (pltpu.SemaphoreType.DMA((3,)), pltpu.SemaphoreType.DMA((3,)))

    @staticmethod
    def out_shape(parts):
        return SDS((3,) + tuple(parts.shape[1:]), parts.dtype)

    @staticmethod
    def _copies(in_ref, out_ref, send_sems, recv_sems):
        x, y, c = _position()
        return [
            pltpu.make_async_remote_copy(
                src_ref=in_ref.at[2 * (x ^ fx) + (y ^ fy)], dst_ref=out_ref.at[k],
                send_sem=send_sems.at[k], recv_sem=recv_sems.at[k],
                device_id=(x ^ fx, y ^ fy, c), device_id_type=MESH)
            for k, (fx, fy) in enumerate(CHIP_FLIPS)]

    @classmethod
    def start(cls, *refs):
        for cp in cls._copies(*refs):
            cp.start()

    @classmethod
    def wait(cls, *refs):
        copies = cls._copies(*refs)
        for cp in copies:
            cp.wait_recv()
        for cp in copies:
            cp.wait_send()


def _exchange(kind, data, name):
    def body(in_ref, out_ref, *sems):
        kind.start(in_ref, out_ref, *sems)
        kind.wait(in_ref, out_ref, *sems)

    return pl.pallas_call(body, out_shape=kind.out_shape(data), in_specs=[ANY], out_specs=ANY, name=name,
                          scratch_shapes=list(kind.scratch))(data)


def _swap_cores(block, name):
    def body(in_ref, out_ref, send_sem, recv_sem):
        x, y, c = _position()
        cp = pltpu.make_async_remote_copy(
            src_ref=in_ref, dst_ref=out_ref, send_sem=send_sem, recv_sem=recv_sem,
            device_id=(x, y, 1 - c), device_id_type=MESH)
        cp.start()
        cp.wait_recv()
        cp.wait_send()

    return pl.pallas_call(
        body, out_shape=SDS(block.shape, block.dtype), in_specs=[ANY], out_specs=ANY, name=name,
        scratch_shapes=[pltpu.SemaphoreType.DMA(()), pltpu.SemaphoreType.DMA(())])(block)


def _gather_all(block, name):
    R, C = block.shape
    flips = [(fx, fy, fc) for fx in (0, 1) for fy in (0, 1) for fc in (0, 1)][1:]

    def body(in_ref, out_ref, send_sems, recv_sems, local_sem):
        x, y, c = _position()
        mine = out_ref.at[4 * x + 2 * y + c]
        local = pltpu.make_async_copy(in_ref, mine, local_sem)
        local.start()
        copies = [
            pltpu.make_async_remote_copy(
                src_ref=in_ref, dst_ref=mine, send_sem=send_sems.at[k], recv_sem=recv_sems.at[k],
                device_id=(x ^ fx, y ^ fy, c ^ fc), device_id_type=MESH)
            for k, (fx, fy, fc) in enumerate(flips)]
        for cp in copies:
            cp.start()
        for k, (fx, fy, fc) in enumerate(flips):
            theirs = out_ref.at[4 * (x ^ fx) + 2 * (y ^ fy) + (c ^ fc)]
            pltpu.make_async_remote_copy(
                src_ref=in_ref, dst_ref=theirs, send_sem=send_sems.at[k], recv_sem=recv_sems.at[k],
                device_id=(x ^ fx, y ^ fy, c ^ fc), device_id_type=MESH).wait_recv()
        for cp in copies:
            cp.wait_send()
        local.wait()

    return pl.pallas_call(
        body, out_shape=SDS((8, R, C), block.dtype), in_specs=[ANY], out_specs=ANY, name=name,
        scratch_shapes=[pltpu.SemaphoreType.DMA((7,)), pltpu.SemaphoreType.DMA((7,)), pltpu.SemaphoreType.DMA(())])(block)


BIG = ("ffn1_w_gate", "ffn1_w_up", "ffn1_w_down", "w_in", "w_out", "ffn2_w_gate", "ffn2_w_up", "ffn2_w_down")
COL_SHARDED = ("ffn1_w_gate", "ffn1_w_up", "w_in", "ffn2_w_gate", "ffn2_w_up")
PACK_COLS = 1024
PACK_ROW_TILE = 256
EARLY = tuple((n, 0) for n in ("ffn1_w_gate", "ffn1_w_up", "ffn1_w_down", "w_in"))
LATE = tuple((n, 0) for n in ("w_out", "ffn2_w_gate", "ffn2_w_up", "ffn2_w_down")) + tuple((n, 1) for n in BIG)


def _pack(blocks):
    rows = [b.reshape(-1, PACK_COLS) for b in blocks]
    total = sum(r.shape[0] for r in rows)
    pad = -total % PACK_ROW_TILE
    if pad:
        rows.append(jnp.zeros((pad, PACK_COLS), rows[0].dtype))
    return jnp.concatenate(rows, axis=0)


def _unpack(buf, shapes):
    out, off = [], 0
    for shp in shapes:
        n = math.prod(shp) // PACK_COLS
        out.append(buf[..., off:off + n, :].reshape(buf.shape[:-2] + tuple(shp)))
        off += n
    return out


def _to_shards(name, full):
    r, c = full.shape
    if name in COL_SHARDED:
        return full.reshape(r, 4, c // 4).transpose(1, 0, 2)
    return full.reshape(4, r // 4, c)


def _own_shard(name, full, chip):
    r, c = full.shape
    if name in COL_SHARDED:
        return lax.dynamic_slice_in_dim(full, chip * (c // 4), c // 4, axis=1)
    return lax.dynamic_slice_in_dim(full, chip * (r // 4), r // 4, axis=0)


def _from_shards(name, sh):
    _, r, c = sh.shape
    if name in COL_SHARDED:
        return sh.transpose(1, 0, 2).reshape(r, 4 * c)
    return sh.reshape(4 * r, c)


def _pad_w_in(w):
    return jnp.concatenate([w[:, :1792], w[:, 1800:2312], w[:, 1792:1800], jnp.zeros((w.shape[0], 248), w.dtype)], axis=1)


def _unpad_w_in(g):
    return jnp.concatenate([g[:, :1792], g[:, 2304:2312], g[:, 1792:2304]], axis=1)


def _block_diag(pw):
    out = jnp.zeros((256, 256), pw.dtype)
    for gidx in range(4):
        out = lax.dynamic_update_slice(out, pw[gidx], (64 * gidx, 64 * gidx))
    return out


SMALL = ("ffn1_norm", "mix_norm", "pool_w", "pool_scale", "forget_bias", "conv_b", "conv_ln_g", "conv_ln_b",
         "ffn2_norm", "final_norm")


def _pack_small(arrs):
    rows = []
    for a in arrs:
        flat = a.reshape(-1)
        flat = jnp.pad(flat, (0, -flat.shape[0] % LANES))
        rows.append(flat.reshape(-1, LANES))
    total = sum(r.shape[0] for r in rows)
    if total % 8:
        rows.append(jnp.zeros((-total % 8, LANES), F32))
    return jnp.concatenate(rows, axis=0)


def _unpack_small(buf, shapes):
    out, off = [], 0
    for shp in shapes:
        n = math.prod(shp)
        nr = -(-n // LANES)
        out.append(buf[off:off + nr].reshape(-1)[:n].reshape(shp))
        off += nr
    return out


class _Late:
    def __init__(self, P):
        self.block = _pack([P[n][l].astype(MM) for n, l in LATE])
        self.shapes = [P[n].shape[1:] for n, _ in LATE]

    def install(self, W, gathered):
        for (n, l), sh in zip(LATE, _unpack(gathered, self.shapes)):
            W[n][l] = _from_shards(n, sh)

    @staticmethod
    def parts(grads):
        return _grad_parts(grads, LATE)


def _grad_parts(grads, pieces):
    return jnp.stack([_pack([_to_shards(n, grads[n][l])[j].astype(MM) for n, l in pieces]) for j in range(4)])


def _forward_backward(x, target, W, late=None):
    T = x.shape[0]
    L = W["ffn1_norm"].shape[0]
    saved = []
    late_recv = None
    for l in range(L):
        g1, gm, g2 = (W[n][l][None, :] for n in ("ffn1_norm", "mix_norm", "ffn2_norm"))
        x1 = _ffn_fwd(x, g1, W["ffn1_w_gate"][l], W["ffn1_w_up"][l], W["ffn1_w_down"][l])
        w_in = _pad_w_in(W["w_in"][l])
        up, q, k, v, ca, cg, zf = _mix_in_fwd(x1, gm, w_in)
        fb = jnp.pad(W["forget_bias"][l], (0, LANES - HEADS))[None, :]
        F = _fgate_fwd(zf, fb)
        if late is not None and l == 0:
            yb, lse, gathered = _attn_fwd(q, k, v, F, hosted=(_GatherChips, late.block))
            late.install(W, gathered)
        else:
            yb, lse = _attn_fwd(q, k, v, F)
        bd = _block_diag(W["pool_w"][l]).astype(MM)
        ps, cb, lg, lb = (W[n][l][None, :] for n in ("pool_scale", "conv_b", "conv_ln_g", "conv_ln_b"))
        cw = jnp.pad(W["conv_w"][l], ((0, 1), (0, 0)))
        ya, yc, cu, cy = _local_fwd(up, ca, cg, bd, ps, cw, cb, lg, lb)
        x2 = _mix_out_fwd(x1, ya, yb, yc, W["w_out"][l])
        x3 = _ffn_fwd(x2, g2, W["ffn2_w_gate"][l], W["ffn2_w_up"][l], W["ffn2_w_down"][l])
        saved.append(dict(x0=x, x1=x1, x2=x2, w_in=w_in, up=up, ca=ca, cg=cg, zf=zf, fb=fb, F=F,
                          q=q, k=k, v=v, lse=lse, bd=bd, cw=cw, cu=cu, cy=cy, ya=ya, yb=yb, yc=yc))
        x = x3

    loss, dx, dgf = _head(x, W["final_norm"][None, :], target)
    grads = {n: [None] * L for n in W if n != "final_norm"}
    grads["final_norm"] = dgf[0]
    for l in reversed(range(L)):
        s = saved[l]
        g1, gm, g2 = (W[n][l][None, :] for n in ("ffn1_norm", "mix_norm", "ffn2_norm"))
        ps, lg, lb = (W[n][l][None, :] for n in ("pool_scale", "conv_ln_g", "conv_ln_b"))
        dx, h, dy, da, db, sact, dg = _ffn_bwd(s["x2"], dx, g2, W["ffn2_w_gate"][l], W["ffn2_w_up"][l], W["ffn2_w_down"][l])
        grads["ffn2_norm"][l] = dg[0]
        grads["ffn2_w_gate"][l] = _wgrad(h, da, "wgrad_gate")
        grads["ffn2_w_up"][l] = _wgrad(h, db, "wgrad_up")
        grads["ffn2_w_down"][l] = _wgrad(sact, dy, "wgrad_down")
        dya, dyb, dyc = _mix_out_bwd(dx, W["w_out"][l])
        grads["w_out"][l] = jnp.concatenate(
            [_wgrad(s["ya"], dx, "wgrad_out_a"), _wgrad(s["yb"], dx, "wgrad_out_b"), _wgrad(s["yc"], dx, "wgrad_out_c")], axis=0)
        if late is not None and l == 0:
            dq, dk, dv, dfq, dfk, late_recv = _attn_bwd(s["q"], s["k"], s["v"], s["F"], s["yb"], s["lse"], dyb,
                                                       hosted=(_ScatterChips, late.parts(grads)))
        else:
            dq, dk, dv, dfq, dfk = _attn_bwd(s["q"], s["k"], s["v"], s["F"], s["yb"], s["lse"], dyb)
        dfk_cols = jnp.pad(dfk.transpose(0, 2, 1, 3).reshape(HEADS, T).T, ((0, 0), (0, LANES - HEADS)))
        dzf, dfb = _fgate_bwd(s["zf"], s["fb"], dfq, dfk_cols)
        grads["forget_bias"][l] = dfb[0, :HEADS]
        dup, dca, dcg, dbd, dps, dcw, dcb, dlg, dlb = _local_bwd(
            s["up"], dya, s["ca"], s["cg"], s["cu"], s["cy"], dyc, s["bd"], ps, s["cw"], lg, lb)
        grads["pool_w"][l] = jnp.stack([dbd[64 * i:64 * i + 64, 64 * i:64 * i + 64] for i in range(4)])
        grads["pool_scale"][l], grads["conv_b"][l] = dps[0], dcb[0]
        grads["conv_ln_g"][l], grads["conv_ln_b"][l] = dlg[0], dlb[0]
        grads["conv_w"][l] = dcw[:CONV_K]
        dx, h, dp, dg = _mix_in_bwd(s["x1"], dx, gm, s["w_in"], dup, dq, dk, dv, dca, dcg, dzf)
        grads["mix_norm"][l] = dg[0]
        grads["w_in"][l] = _unpad_w_in(_wgrad(h, dp, "wgrad_in"))
        dx, h, dy, da, db, sact, dg = _ffn_bwd(s["x0"], dx, g1, W["ffn1_w_gate"][l], W["ffn1_w_up"][l], W["ffn1_w_down"][l])
        grads["ffn1_norm"][l] = dg[0]
        grads["ffn1_w_gate"][l] = _wgrad(h, da, "wgrad_gate")
        grads["ffn1_w_up"][l] = _wgrad(h, db, "wgrad_up")
        grads["ffn1_w_down"][l] = _wgrad(sact, dy, "wgrad_down")
    grads = {n: (jnp.stack(g) if isinstance(g, list) and n not in BIG else g) for n, g in grads.items()}
    return loss, dx, grads, late_recv


NAMES = ("ffn1_norm", "ffn1_w_gate", "ffn1_w_up", "ffn1_w_down", "mix_norm", "w_in", "pool_w", "pool_scale",
         "forget_bias", "conv_w", "conv_b", "conv_ln_g", "conv_ln_b", "w_out", "ffn2_norm", "ffn2_w_gate",
         "ffn2_w_up", "ffn2_w_down", "final_norm")


def kernel(x, ffn1_norm, ffn1_w_gate, ffn1_w_up, ffn1_w_down, mix_norm, w_in, pool_w, pool_scale, forget_bias, conv_w, conv_b, conv_ln_g, conv_ln_b, w_out, ffn2_norm, ffn2_w_gate, ffn2_w_up, ffn2_w_down, final_norm, loss_target, m_ffn1_norm, m_ffn1_w_gate, m_ffn1_w_up, m_ffn1_w_down, m_mix_norm, m_w_in, m_pool_w, m_pool_scale, m_forget_bias, m_conv_w, m_conv_b, m_conv_ln_g, m_conv_ln_b, m_w_out, m_ffn2_norm, m_ffn2_w_gate, m_ffn2_w_up, m_ffn2_w_down, m_final_norm, v_ffn1_norm, v_ffn1_w_gate, v_ffn1_w_up, v_ffn1_w_down, v_mix_norm, v_w_in, v_pool_w, v_pool_scale, v_forget_bias, v_conv_w, v_conv_b, v_conv_ln_g, v_conv_ln_b, v_w_out, v_ffn2_norm, v_ffn2_w_gate, v_ffn2_w_up, v_ffn2_w_down, v_final_norm):
    args = (ffn1_norm, ffn1_w_gate, ffn1_w_up, ffn1_w_down, mix_norm, w_in, pool_w, pool_scale, forget_bias, conv_w, conv_b, conv_ln_g, conv_ln_b, w_out, ffn2_norm, ffn2_w_gate, ffn2_w_up, ffn2_w_down, final_norm)
    ms = (m_ffn1_norm, m_ffn1_w_gate, m_ffn1_w_up, m_ffn1_w_down, m_mix_norm, m_w_in, m_pool_w, m_pool_scale, m_forget_bias, m_conv_w, m_conv_b, m_conv_ln_g, m_conv_ln_b, m_w_out, m_ffn2_norm, m_ffn2_w_gate, m_ffn2_w_up, m_ffn2_w_down, m_final_norm)
    vs = (v_ffn1_norm, v_ffn1_w_gate, v_ffn1_w_up, v_ffn1_w_down, v_mix_norm, v_w_in, v_pool_w, v_pool_scale, v_forget_bias, v_conv_w, v_conv_b, v_conv_ln_g, v_conv_ln_b, v_w_out, v_ffn2_norm, v_ffn2_w_gate, v_ffn2_w_up, v_ffn2_w_down, v_final_norm)
    P = dict(zip(NAMES, args))
    M = dict(zip(NAMES, ms))
    V = dict(zip(NAMES, vs))
    xi, yi, _ = _position()
    chip = 2 * xi + yi

    W = {n: P[n] for n in SMALL}
    W.update({n: [None] * P[n].shape[0] for n in BIG})
    early = _exchange(_GatherChips, _pack([P[n][l].astype(MM) for n, l in EARLY]), "gather_early_weights")
    for (n, l), sh in zip(EARLY, _unpack(early, [P[n].shape[1:] for n, _ in EARLY])):
        W[n][l] = _from_shards(n, sh)
    cw_all = _exchange(_GatherChips, _pack_small([P["conv_w"]]), "gather_conv_w")
    cw_sh = jnp.stack([_unpack_small(cw_all[j], [P["conv_w"].shape])[0] for j in range(4)])
    W["conv_w"] = cw_sh.transpose(1, 2, 0, 3).reshape(P["conv_w"].shape[0], CONV_K, 256)

    loss_part, dx, G, recv_late = _forward_backward(x[0], loss_target[0], W, _Late(P))
    loss = lax.psum(loss_part[0, 0], ("x", "y", "c"))

    small_shapes = [P[n].shape for n in SMALL] + [G["conv_w"].shape]
    small_parts = _gather_all(_pack_small([G[n] for n in SMALL] + [G["conv_w"]]), "gather_small_grads")
    small_sum = _sum8(small_parts, "sum_small_grads")
    nsmall = sum(-(-math.prod(s) // LANES) for s in small_shapes[:-1])
    nsmall_pad = nsmall + (-nsmall % 8)
    w_s, m_s, v_s = (_pack_small([D[n] for n in SMALL]) for D in (P, M, V))
    outs_small = _adamw(w_s, [small_sum[:nsmall_pad]], m_s, v_s, "adamw_small")
    res = {}
    for kind, buf in zip(("g", "d", "m", "v"), outs_small):
        for n, a in zip(SMALL, _unpack_small(buf, small_shapes[:-1])):
            res[(kind, n)] = a
    g_cw_full = _unpack_small(small_sum[nsmall:], [small_shapes[-1]])[0]
    g_cw = lax.dynamic_slice_in_dim(g_cw_full, chip * 64, 64, axis=2)
    outs_cw = _adamw(_pack_small([P["conv_w"]]), [_pack_small([g_cw])], _pack_small([M["conv_w"]]),
                     _pack_small([V["conv_w"]]), "adamw_conv_w")
    for kind, buf in zip(("g", "d", "m", "v"), outs_cw):
        res[(kind, "conv_w")] = _unpack_small(buf, [P["conv_w"].shape])[0]

    recv_early = _exchange(_ScatterChips, _grad_parts(G, EARLY), "scatter_early_grads")
    sums = {}
    for tag, pieces, recv in (("late", LATE, recv_late), ("early", EARLY, recv_early)):
        own = _pack([_own_shard(n, G[n][l], chip) for n, l in pieces])
        part = _sum_parts(own, recv, "sum_%s_grads" % tag)
        other = _swap_cores(part, "swap_%s_grads" % tag)
        shapes = [P[n].shape[1:] for n, _ in pieces]
        for piece, ga, gb in zip(pieces, _unpack(part, shapes), _unpack(other, shapes)):
            sums[piece] = (ga, gb)
    for n in BIG:
        shp = P[n].shape
        two_d = (shp[0] * shp[1], shp[2])
        ga, gb = (jnp.stack([sums[(n, l)][i] for l in range(shp[0])]).reshape(two_d) for i in range(2))
        outs = _adamw(P[n].reshape(two_d), [ga, gb], M[n].reshape(two_d), V[n].reshape(two_d), "adamw_" + n)
        for kind, a in zip(("g", "d", "m", "v"), outs):
            res[(kind, n)] = a.reshape(shp)

    return (loss, dx[None], *[res[("g", n)] for n in NAMES], *[res[("d", n)] for n in NAMES],
            *[res[("m", n)] for n in NAMES], *[res[("v", n)] for n in NAMES])
```

```python
import functools
import math

import jax
import jax.numpy as jnp
from jax import lax
from jax.experimental import pallas as pl
from jax.experimental.pallas import tpu as pltpu

F32 = jnp.float32
MM = jnp.bfloat16
NORM_EPS = 1e-6
HEADS = 8
HEAD_DIM = 64
POOL_WINDOWS = (2, 4, 8, 16)
CONV_K = 31
LANES = 128
VMEM_LIMIT = 56 * 2**20
FFN_BWD_ROWS = 256

ADAM_LR = 0.001
ADAM_B1 = 0.9
ADAM_B2 = 0.999
ADAM_EPS = 1e-08
ADAM_WD = 0.01
ADAM_STEP = 10

MESH = pl.DeviceIdType.MESH
BS = pl.BlockSpec
SDS = jax.ShapeDtypeStruct
ANY = pl.BlockSpec(memory_space=pl.ANY)


def _dot(a, b):
    return jnp.dot(a, b, preferred_element_type=F32)


def _dot_nt(a, b):
    return lax.dot_general(a, b, (((1,), (1,)), ((), ())), preferred_element_type=F32)


def _dot_tn(a, b):
    return lax.dot_general(a, b, (((0,), (0,)), ((), ())), preferred_element_type=F32)


def _pc(body, name, grid, in_specs, out_specs, out_shape, scratch=()):
    return pl.pallas_call(
        body, out_shape=out_shape, grid=grid, in_specs=in_specs, out_specs=out_specs,
        scratch_shapes=list(scratch), name=name,
        compiler_params=pltpu.CompilerParams(
            dimension_semantics=("arbitrary",) * len(grid), vmem_limit_bytes=VMEM_LIMIT))


def _rms_fwd(x, g):
    r = lax.rsqrt(jnp.mean(x * x, axis=-1, keepdims=True) + NORM_EPS)
    xh = x * r
    return xh, r, xh * g


def _rms_bwd(dh, xh, r, g):
    dxh = dh * g
    dx = r * (dxh - xh * jnp.mean(dxh * xh, axis=-1, keepdims=True))
    return dx, jnp.sum(dh * xh, axis=0, keepdims=True)


def _sigmoid(x):
    return jax.nn.sigmoid(x)


def _ffn_fwd(x, g, wg, wu, wd):
    T, D = x.shape
    F = wg.shape[1]
    tm, nf = min(512, T), 2
    fc = F // nf

    def body(x_ref, g_ref, wg_ref, wu_ref, wd_ref, o_ref, h_scr, acc_scr):
        j = pl.program_id(1)

        @pl.when(j == 0)
        def _():
            _, _, hg = _rms_fwd(x_ref[...], g_ref[...])
            h_scr[...] = hg.astype(h_scr.dtype)
            acc_scr[...] = jnp.zeros_like(acc_scr)

        h = h_scr[...]
        a = _dot(h, wg_ref[...])
        b = _dot(h, wu_ref[...])
        s = (a * _sigmoid(a)) * b
        acc_scr[...] += _dot(s.astype(MM), wd_ref[...])

        @pl.when(j == nf - 1)
        def _():
            o_ref[...] = x_ref[...] + 0.5 * acc_scr[...]

    return _pc(
        body, "ffn_fwd", (T // tm, nf),
        [BS((tm, D), lambda i, j: (i, 0)), BS((1, D), lambda i, j: (0, 0)),
         BS((D, fc), lambda i, j: (0, j)), BS((D, fc), lambda i, j: (0, j)), BS((fc, D), lambda i, j: (j, 0))],
        BS((tm, D), lambda i, j: (i, 0)), SDS((T, D), F32),
        scratch=[pltpu.VMEM((tm, D), MM), pltpu.VMEM((tm, D), F32)])(x, g, wg, wu, wd)


def _ffn_bwd(x, dout, g, wg, wu, wd):
    T, D = x.shape
    F = wg.shape[1]
    tm, nf = min(FFN_BWD_ROWS, T), 2
    fc = F // nf

    def body(x_ref, do_ref, g_ref, wg_ref, wu_ref, wd_ref,
             dx_ref, h_ref, dy_ref, da_ref, db_ref, s_ref, dg_ref, dh_scr):
        i, j = pl.program_id(0), pl.program_id(1)

        @pl.when(j == 0)
        def _():
            _, _, hg = _rms_fwd(x_ref[...], g_ref[...])
            h_ref[...] = hg.astype(h_ref.dtype)
            dy_ref[...] = (0.5 * do_ref[...]).astype(dy_ref.dtype)
            dh_scr[...] = jnp.zeros_like(dh_scr)

        @pl.when((i == 0) & (j == 0))
        def _():
            dg_ref[...] = jnp.zeros_like(dg_ref)

        h = h_ref[...]
        dy = dy_ref[...]
        a = _dot(h, wg_ref[...])
        b = _dot(h, wu_ref[...])
        ds = _dot_nt(dy, wd_ref[...])
        sig = _sigmoid(a)
        sl = a * sig
        s_ref[...] = (sl * b).astype(s_ref.dtype)
        db = (ds * sl).astype(MM)
        da = (ds * b * (sig * (1.0 + a * (1.0 - sig)))).astype(MM)
        da_ref[...] = da
        db_ref[...] = db
        dh_scr[...] += _dot_nt(da, wg_ref[...]) + _dot_nt(db, wu_ref[...])

        @pl.when(j == nf - 1)
        def _():
            gv = g_ref[...]
            xh, r, _ = _rms_fwd(x_ref[...], gv)
            dx, dg = _rms_bwd(dh_scr[...], xh, r, gv)
            dx_ref[...] = do_ref[...] + dx
            dg_ref[...] += dg

    tok = lambda i, j: (i, 0)
    return _pc(
        body, "ffn_bwd", (T // tm, nf),
        [BS((tm, D), tok), BS((tm, D), tok), BS((1, D), lambda i, j: (0, 0)),
         BS((D, fc), lambda i, j: (0, j)), BS((D, fc), lambda i, j: (0, j)), BS((fc, D), lambda i, j: (j, 0))],
        [BS((tm, D), tok), BS((tm, D), tok), BS((tm, D), tok),
         BS((tm, fc), lambda i, j: (i, j)), BS((tm, fc), lambda i, j: (i, j)), BS((tm, fc), lambda i, j: (i, j)),
         BS((1, D), lambda i, j: (0, 0))],
        [SDS((T, D), F32), SDS((T, D), MM), SDS((T, D), MM),
         SDS((T, F), MM), SDS((T, F), MM), SDS((T, F), MM), SDS((1, D), F32)],
        scratch=[pltpu.VMEM((tm, D), F32)])(x, dout, g, wg, wu, wd)


def _wgrad(a, b, name):
    T, K = a.shape
    N = b.shape[1]
    tt = min(512, T)
    tn = N
    for cand in (1408, 1280, 1024, 512, 256, 128):
        if N % cand == 0 and K * cand * 4 <= 6 * 2**20:
            tn = cand
            break

    def body(a_ref, b_ref, o_ref):
        @pl.when(pl.program_id(1) == 0)
        def _():
            o_ref[...] = jnp.zeros_like(o_ref)

        o_ref[...] += _dot_tn(a_ref[...].astype(MM), b_ref[...].astype(MM))

    return _pc(
        body, name, (N // tn, T // tt),
        [BS((tt, K), lambda n, t: (t, 0)), BS((tt, tn), lambda n, t: (t, n))],
        BS((K, tn), lambda n, t: (0, n)), SDS((K, N), F32))(a, b)


C_POOL, C_Q, C_K, C_V, C_CA, C_CG, C_ZF, C_END = 0, 256, 768, 1280, 1792, 2048, 2304, 2560


def _mix_in_fwd(x, g, w):
    T, D = x.shape
    tm = min(512, T)

    def body(x_ref, g_ref, w_ref, up_ref, q_ref, k_ref, v_ref, ca_ref, cg_ref, zf_ref):
        _, _, hg = _rms_fwd(x_ref[...], g_ref[...])
        p = _dot(hg.astype(MM), w_ref[...])
        up_ref[...] = p[:, C_POOL:C_Q]
        q_ref[...] = p[:, C_Q:C_K].astype(q_ref.dtype)
        k_ref[...] = p[:, C_K:C_V].astype(k_ref.dtype)
        v_ref[...] = p[:, C_V:C_CA].astype(v_ref.dtype)
        ca_ref[...] = p[:, C_CA:C_CG]
        cg_ref[...] = p[:, C_CG:C_ZF]
        zf_ref[...] = p[:, C_ZF:C_ZF + LANES]

    tok = lambda i: (i, 0)
    widths = (256, 512, 512, 512, 256, 256, 128)
    dtypes = (F32, MM, MM, MM, F32, F32, F32)
    return _pc(
        body, "mix_in_fwd", (T // tm,),
        [BS((tm, D), tok), BS((1, D), lambda i: (0, 0)), BS((D, C_END), lambda i: (0, 0))],
        [BS((tm, wd), tok) for wd in widths],
        [SDS((T, wd), dt) for wd, dt in zip(widths, dtypes)])(x, g, w)


def _mix_in_bwd(x, dout, g, w, dup, dq, dk, dv, dca, dcg, dzf):
    T, D = x.shape
    tm = min(512, T)

    def body(x_ref, do_ref, g_ref, w_ref, dup_ref, dq_ref, dk_ref, dv_ref, dca_ref, dcg_ref, dzf_ref,
             dx_ref, h_ref, dp_ref, dg_ref):
        @pl.when(pl.program_id(0) == 0)
        def _():
            dg_ref[...] = jnp.zeros_like(dg_ref)

        gv = g_ref[...]
        xh, r, hg = _rms_fwd(x_ref[...], gv)
        h_ref[...] = hg.astype(h_ref.dtype)
        for ref, lo, hi in ((dup_ref, C_POOL, C_Q), (dq_ref, C_Q, C_K), (dk_ref, C_K, C_V), (dv_ref, C_V, C_CA),
                            (dca_ref, C_CA, C_CG), (dcg_ref, C_CG, C_ZF), (dzf_ref, C_ZF, C_ZF + LANES)):
            dp_ref[:, lo:hi] = ref[...].astype(dp_ref.dtype)
        dp_ref[:, C_ZF + LANES:C_END] = jnp.zeros((tm, C_END - C_ZF - LANES), dp_ref.dtype)
        dh = _dot_nt(dp_ref[...], w_ref[...])
        dx, dg = _rms_bwd(dh, xh, r, gv)
        dx_ref[...] = do_ref[...] + dx
        dg_ref[...] += dg

    tok = lambda i: (i, 0)
    widths = (256, 512, 512, 512, 256, 256, 128)
    return _pc(
        body, "mix_in_bwd", (T // tm,),
        [BS((tm, D), tok), BS((tm, D), tok), BS((1, D), lambda i: (0, 0)), BS((D, C_END), lambda i: (0, 0))]
        + [BS((tm, wd), tok) for wd in widths],
        [BS((tm, D), tok), BS((tm, D), tok), BS((tm, C_END), tok), BS((1, D), lambda i: (0, 0))],
        [SDS((T, D), F32), SDS((T, D), MM), SDS((T, C_END), MM), SDS((1, D), F32)],
    )(x, dout, g, w, dup, dq, dk, dv, dca, dcg, dzf)


def _mix_out_fwd(x, ya, yb, yc, wo):
    T, D = x.shape
    tm = min(512, T)

    def body(x_ref, ya_ref, yb_ref, yc_ref, wo_ref, o_ref):
        o_ref[...] = (x_ref[...] + _dot(ya_ref[...].astype(MM), wo_ref[0:256, :])
                      + _dot(yb_ref[...].astype(MM), wo_ref[256:768, :])
                      + _dot(yc_ref[...].astype(MM), wo_ref[768:1024, :]))

    tok = lambda i: (i, 0)
    return _pc(
        body, "mix_out_fwd", (T // tm,),
        [BS((tm, D), tok), BS((tm, 256), tok), BS((tm, 512), tok), BS((tm, 256), tok), BS((D, D), lambda i: (0, 0))],
        BS((tm, D), tok), SDS((T, D), F32))(x, ya, yb, yc, wo)


def _mix_out_bwd(dx, wo):
    T, D = dx.shape
    tm = min(512, T)

    def body(dx_ref, wo_ref, dya_ref, dyb_ref, dyc_ref):
        dy = _dot_nt(dx_ref[...].astype(MM), wo_ref[...])
        dya_ref[...] = dy[:, 0:256]
        dyb_ref[...] = dy[:, 256:768]
        dyc_ref[...] = dy[:, 768:1024]

    tok = lambda i: (i, 0)
    return _pc(
        body, "mix_out_bwd", (T // tm,),
        [BS((tm, D), tok), BS((D, D), lambda i: (0, 0))],
        [BS((tm, 256), tok), BS((tm, 512), tok), BS((tm, 256), tok)],
        [SDS((T, 256), F32), SDS((T, 512), F32), SDS((T, 256), F32)])(dx, wo)


def _fgate_fwd(zf, bias):
    T = zf.shape[0]
    tc = min(256, T)

    def body(z_ref, b_ref, f_ref, carry):
        @pl.when(pl.program_id(0) == 0)
        def _():
            carry[...] = jnp.zeros_like(carry)

        z = z_ref[...] + b_ref[...]
        logf = jnp.minimum(z, 0.0) - jnp.log(1.0 + jnp.exp(-jnp.abs(z)))
        row = lax.broadcasted_iota(jnp.int32, (tc, tc), 0)
        col = lax.broadcasted_iota(jnp.int32, (tc, tc), 1)
        tri = (col <= row).astype(F32)
        f_ref[...] = jnp.dot(tri, logf, precision=lax.Precision.HIGHEST, preferred_element_type=F32) + carry[...]
        carry[...] += jnp.sum(logf, axis=0, keepdims=True)

    return _pc(
        body, "fgate_fwd", (T // tc,),
        [BS((tc, LANES), lambda i: (i, 0)), BS((1, LANES), lambda i: (0, 0))],
        BS((tc, LANES), lambda i: (i, 0)), SDS((T, LANES), F32),
        scratch=[pltpu.VMEM((1, LANES), F32)])(zf, bias)


def _fgate_bwd(zf, bias, dFq, dFk):
    T = zf.shape[0]
    tc = min(256, T)
    n = T // tc
    slabs = dFq.shape[0]

    def body(z_ref, b_ref, dfq_ref, dfk_ref, dz_ref, db_ref, carry):
        @pl.when(pl.program_id(0) == 0)
        def _():
            carry[...] = jnp.zeros_like(carry)
            db_ref[...] = jnp.zeros_like(db_ref)

        df = dfk_ref[...]
        for sl in range(slabs):
            df = df + dfq_ref[sl]
        row = lax.broadcasted_iota(jnp.int32, (tc, tc), 0)
        col = lax.broadcasted_iota(jnp.int32, (tc, tc), 1)
        tri = (col >= row).astype(F32)
        dlogf = jnp.dot(tri, df, precision=lax.Precision.HIGHEST, preferred_element_type=F32) + carry[...]
        carry[...] += jnp.sum(df, axis=0, keepdims=True)
        lane = lax.broadcasted_iota(jnp.int32, (1, LANES), 1)
        dz = jnp.where(lane < HEADS, dlogf * _sigmoid(-(z_ref[...] + b_ref[...])), 0.0)
        dz_ref[...] = dz
        db_ref[...] += jnp.sum(dz, axis=0, keepdims=True)

    rev = lambda i: (n - 1 - i, 0)
    return _pc(
        body, "fgate_bwd", (n,),
        [BS((tc, LANES), rev), BS((1, LANES), lambda i: (0, 0)), BS((slabs, tc, LANES), lambda i: (0, n - 1 - i, 0)),
         BS((tc, LANES), rev)],
        [BS((tc, LANES), rev), BS((1, LANES), lambda i: (0, 0))],
        [SDS((T, LANES), F32), SDS((1, LANES), F32)],
        scratch=[pltpu.VMEM((1, LANES), F32)])(zf, bias, dFq, dFk)


LOG2E = 1.4426950408889634


def _split3(x):
    hi = x.astype(MM)
    r1 = x - hi.astype(F32)
    mid = r1.astype(MM)
    return hi, mid, (r1 - mid.astype(F32)).astype(MM)


def _place(lane, base, cols):
    out = jnp.zeros((cols[0].shape[0], LANES), MM)
    for i, c in enumerate(cols):
        out = jnp.where(lane == base + i, c, out)
    return out


def _head_col(block, lane, h):
    return jnp.sum(jnp.where(lane == h, block, 0.0), axis=-1, keepdims=True)


def _own_lanes(lane, hh):
    return (lane < HEAD_DIM) if hh == 0 else (lane >= HEAD_DIM)


def _attn_k_side(k_ref, f_ref, kb_ref, hp, T, rows, lse_ones):
    lane = lax.broadcasted_iota(jnp.int32, (1, LANES), 1)
    one = jnp.ones((rows, 1), MM)

    def chunk(c, _):
        r0 = pl.multiple_of(c * rows, rows)
        kp = k_ref[pl.ds(r0, rows), :]
        fblk = f_ref[pl.ds(r0, rows), :]
        for hh in range(2):
            hi, mid, lo = _split3(-_head_col(fblk, lane, 2 * hp + hh) * LOG2E)
            cols = [one, one, one, hi, mid, lo] + ([one, one, one] if lse_ones else [])
            bias = _place(lane, HEAD_DIM * (1 - hh), cols)
            kb_ref[hh, pl.ds(r0, rows), :] = jnp.where(_own_lanes(lane, hh), kp, bias)
        return 0

    lax.fori_loop(0, T // rows, chunk, 0)


def _attn_q_side(qp, fblk, lane, hp, scale, lse_blk=None):
    qc = qp.astype(F32) * (scale * LOG2E)
    qhi = qc.astype(MM)
    qlo = (qc - qhi.astype(F32)).astype(MM)
    one = jnp.ones((qp.shape[0], 1), MM)
    out = []
    for hh in range(2):
        cols = list(_split3(_head_col(fblk, lane, 2 * hp + hh) * LOG2E)) + [one, one, one]
        if lse_blk is not None:
            cols += list(_split3(-_head_col(lse_blk, lane, 2 * hp + hh)))
        bias = _place(lane, HEAD_DIM * (1 - hh), cols)
        own = _own_lanes(lane, hh)
        out.append(jnp.concatenate([jnp.where(own, qhi, jnp.zeros_like(qhi)), jnp.where(own, qlo, bias)], axis=1))
    return out


def _causal(tq, tk):
    return lax.broadcasted_iota(jnp.int32, (tq, tk), 1) <= lax.broadcasted_iota(jnp.int32, (tq, tk), 0)


def _hosted_specs(hosted):
    if hosted is None:
        return [], [], [], []
    kind, arrays = hosted
    n = len(arrays)
    return [ANY] * n, [ANY] * n, [kind.out_shape(a) for a in arrays], kind.scratch(n)


def _hosted_edges(hosted, refs, n_in, n_out, first, last):
    if hosted is None:
        return refs, lambda: None
    kind, arrays = hosted
    n = len(arrays)
    nsem = len(kind.scratch(n))
    o0 = n_in + n + n_out
    ins, outs, sems = refs[n_in:n_in + n], refs[o0:o0 + n], refs[len(refs) - nsem:]

    @pl.when(first)
    def _():
        kind.start(ins, outs, *sems)

    def finish():
        @pl.when(last)
        def _():
            kind.wait(ins, outs, *sems)

    return refs[:n_in] + refs[n_in + n:o0] + refs[o0 + n:len(refs) - nsem], finish


def _attn_fwd(q, k, v, F, hosted=None):
    T = q.shape[0]
    tq = min(512, T)
    tk = tq
    nq = T // tq
    scale = 1.0 / math.sqrt(HEAD_DIM)
    h_in, h_out, h_shape, h_scratch = _hosted_specs(hosted)

    def body(*refs):
        hp, ib = pl.program_id(0), pl.program_id(1)
        refs, finish = _hosted_edges(hosted, refs, 5, 2, (hp == 0) & (ib == 0), (hp == HEADS // 2 - 1) & (ib == nq - 1))
        q_ref, k_ref, v_ref, fq_ref, f_ref, o_ref, lse_ref, kb_ref = refs
        lane = lax.broadcasted_iota(jnp.int32, (1, LANES), 1)

        @pl.when(ib == 0)
        def _():
            _attn_k_side(k_ref, f_ref, kb_ref, hp, T, tk, False)

        qa = _attn_q_side(q_ref[...], fq_ref[...], lane, hp, scale)

        def tile(jb, carry, masked):
            off = pl.multiple_of(jb * tk, tk)
            kp = k_ref[pl.ds(off, tk), :]
            vp = v_ref[pl.ds(off, tk), :]
            new = []
            for hh in range(2):
                m, l, acc = carry[hh]
                s = _dot_nt(qa[hh], jnp.concatenate([kp, kb_ref[hh, pl.ds(off, tk), :]], axis=1))
                if masked:
                    s = jnp.where(_causal(tq, tk), s, -jnp.inf)
                m2 = jnp.maximum(m, jnp.max(s, axis=-1, keepdims=True))
                p = jnp.exp2(s - m2)
                al = jnp.exp2(m - m2)
                new.append((m2, l * al + jnp.sum(p, axis=-1, keepdims=True), acc * al + _dot(p.astype(MM), vp)))
            return tuple(new)

        init = tuple((jnp.full((tq, 1), -jnp.inf, F32), jnp.zeros((tq, 1), F32), jnp.zeros((tq, LANES), F32))
                     for _ in range(2))
        carry = lax.fori_loop(0, ib, lambda jb, c: tile(jb, c, False), init)
        (m0, l0, a0), (m1, l1, a1) = tile(ib, carry, True)
        o_ref[...] = jnp.where(lane < HEAD_DIM, a0 / l0, a1 / l1)
        lse_ref[...] = jnp.where(lane == 2 * hp, m0 + jnp.log2(l0), jnp.where(lane == 2 * hp + 1, m1 + jnp.log2(l1), 0.0))
        finish()

    blk = lambda h, i: (i, h)
    full = lambda h, i: (0, h)
    return _pc(
        body, "attn_fwd" + ("_hosting" if hosted else ""), (HEADS // 2, nq),
        [BS((tq, LANES), blk), BS((T, LANES), full), BS((T, LANES), full), BS((tq, LANES), lambda h, i: (i, 0)),
         BS((T, LANES), lambda h, i: (0, 0))] + h_in,
        [BS((tq, LANES), blk), BS((None, tq, LANES), lambda h, i: (h, i, 0))] + h_out,
        [SDS((T, HEADS * HEAD_DIM), F32), SDS((HEADS // 2, T, LANES), F32)] + h_shape,
        scratch=[pltpu.VMEM((2, T, LANES), MM)] + h_scratch)(q, k, v, F, F, *(hosted[1] if hosted else []))


def _attn_bwd(q, k, v, F, o, lse, do, hosted=None):
    T = q.shape[0]
    tq = min(512, T)
    tk = tq
    nq = T // tq
    scale = 1.0 / math.sqrt(HEAD_DIM)
    h_in, h_out, h_shape, h_scratch = _hosted_specs(hosted)

    def body(*refs):
        hp, ib = pl.program_id(0), pl.program_id(1)
        refs, finish = _hosted_edges(hosted, refs, 8, 5, (hp == 0) & (ib == 0), (hp == HEADS // 2 - 1) & (ib == nq - 1))
        (q_ref, k_ref, v_ref, fq_ref, f_ref, o_ref, lse_ref, do_ref,
         dq_ref, dk_ref, dv_ref, dfq_ref, dfk_ref, kb_ref) = refs
        lane = lax.broadcasted_iota(jnp.int32, (1, LANES), 1)

        @pl.when(ib == 0)
        def _():
            _attn_k_side(k_ref, f_ref, kb_ref, hp, T, tk, True)
            dk_ref[...] = jnp.zeros_like(dk_ref)
            dv_ref[...] = jnp.zeros_like(dv_ref)
            dfk_ref[...] = jnp.zeros_like(dfk_ref)

        qp = q_ref[...]
        qa = _attn_q_side(qp, fq_ref[...], lane, hp, scale, lse_ref[...])
        dob = do_ref[...].astype(MM)
        dprod = dob.astype(F32) * o_ref[...]
        qs = (qp.astype(F32) * scale).astype(MM)
        heads = []
        for hh in range(2):
            own = _own_lanes(lane, hh)
            heads.append((jnp.where(own, dob, jnp.zeros_like(dob)), jnp.where(own, qs, jnp.zeros_like(qs)),
                          jnp.sum(jnp.where(own, dprod, 0.0), axis=-1, keepdims=True)))

        def tile(jb, carry, masked):
            off = pl.multiple_of(jb * tk, tk)
            kp = k_ref[pl.ds(off, tk), :]
            vp = v_ref[pl.ds(off, tk), :]
            new = []
            dv_t = jnp.zeros((tk, LANES), F32)
            dk_t = jnp.zeros((tk, LANES), F32)
            for hh in range(2):
                dq, rs = carry[hh]
                dom, qm, delta = heads[hh]
                p = jnp.exp2(_dot_nt(qa[hh], jnp.concatenate([kp, kb_ref[hh, pl.ds(off, tk), :]], axis=1)))
                if masked:
                    p = jnp.where(_causal(tq, tk), p, 0.0)
                ds = p * (_dot_nt(dom, vp) - delta)
                dsb = ds.astype(MM)
                dv_t = dv_t + _dot_tn(p.astype(MM), dom)
                dk_t = dk_t + _dot_tn(dsb, qm)
                dfk_ref[jb, pl.ds(hh, 1), :] -= jnp.sum(ds, axis=0, keepdims=True)
                new.append((dq + _dot(dsb, kp), rs + jnp.sum(ds, axis=-1, keepdims=True)))
            dv_ref[pl.ds(off, tk), :] += dv_t
            dk_ref[pl.ds(off, tk), :] += dk_t
            return tuple(new)

        init = tuple((jnp.zeros((tq, LANES), F32), jnp.zeros((tq, 1), F32)) for _ in range(2))
        carry = lax.fori_loop(0, ib, lambda jb, c: tile(jb, c, False), init)
        (dq0, rs0), (dq1, rs1) = tile(ib, carry, True)
        dq_ref[...] = jnp.where(lane < HEAD_DIM, dq0, dq1) * scale
        dfq_ref[...] = jnp.where(lane == 2 * hp, rs0, jnp.where(lane == 2 * hp + 1, rs1, 0.0))
        finish()

    blk = lambda h, i: (i, h)
    full = lambda h, i: (0, h)
    slab = BS((None, tq, LANES), lambda h, i: (h, i, 0))
    return _pc(
        body, "attn_bwd" + ("_hosting" if hosted else ""), (HEADS // 2, nq),
        [BS((tq, LANES), blk), BS((T, LANES), full), BS((T, LANES), full), BS((tq, LANES), lambda h, i: (i, 0)),
         BS((T, LANES), lambda h, i: (0, 0)), BS((tq, LANES), blk), slab, BS((tq, LANES), blk)] + h_in,
        [BS((tq, LANES), blk), BS((T, LANES), full), BS((T, LANES), full), slab,
         BS((None, nq, 2, tk), lambda h, i: (h, 0, 0, 0))] + h_out,
        [SDS((T, HEADS * HEAD_DIM), F32)] * 3 + [SDS((HEADS // 2, T, LANES), F32), SDS((HEADS // 2, nq, 2, tk), F32)]
        + h_shape,
        scratch=[pltpu.VMEM((2, T, LANES), MM)] + h_scratch,
    )(q, k, v, F, F, o, lse, do, *(hosted[1] if hosted else []))


POOL_HALO = 16
CONV_HALO = 32


def _group_select(lane, v0, v1, v2, v3):
    return jnp.where(lane < 64, v0, jnp.where(lane < 128, v1, jnp.where(lane < 192, v2, v3)))


def _roll_down(x, k):
    return x if k == 0 else pltpu.roll(x, k, 0)


def _roll_up(x, k):
    return x if k == 0 else pltpu.roll(x, x.shape[0] - k, 0)


def _pool_terms(u, u_prev, tile, tm):
    ext = jnp.concatenate([u_prev, u], axis=0)
    s2 = ext + _roll_down(ext, 1)
    s4 = s2 + _roll_down(s2, 2)
    s8 = s4 + _roll_down(s4, 4)
    s16 = s8 + _roll_down(s8, 8)
    lane = lax.broadcasted_iota(jnp.int32, (1, 256), 1)
    ws = _group_select(lane, s2, s4, s8, s16)[POOL_HALO:, :]
    wlen = _group_select(lane, 2.0, 4.0, 8.0, 16.0).astype(F32)
    return ws / _pool_count(tile, tm, tm, wlen) - u


def _pool_count(tile, tm, rows, wlen):
    t = (tile * tm + 1 + lax.broadcasted_iota(jnp.int32, (rows, 1), 0)).astype(F32)
    return jnp.minimum(t, wlen)


def _layer_norm(y, lg, lb):
    mu = jnp.mean(y, axis=-1, keepdims=True)
    yc = y - mu
    rstd = lax.rsqrt(jnp.mean(yc * yc, axis=-1, keepdims=True) + NORM_EPS)
    yh = yc * rstd
    return yh, rstd, yh * lg + lb


def _halo_specs(tm, T, halo, prev):
    per = tm // halo
    if prev:
        return BS((halo, 256), lambda i: (jnp.maximum(i * per - 1, 0), 0))
    return BS((halo, 256), lambda i: (jnp.minimum((i + 1) * per, T // halo - 1), 0))


def _local_fwd(up, ca, cg, bd, pscale, cw, cb, lg, lb):
    T = up.shape[0]
    tm = min(512, T)

    def body(up_ref, uph_ref, ca_ref, cah_ref, cg_ref, cgh_ref, bd_ref, ps_ref, cw_ref, cb_ref, lg_ref, lb_ref,
             ya_ref, yc_ref, u_ref, y_ref):
        i = pl.program_id(0)
        first = i == 0
        pooled = _pool_terms(up_ref[...], jnp.where(first, 0.0, uph_ref[...]), i, tm)
        ya_ref[...] = (_dot(pooled.astype(MM), bd_ref[...]) * ps_ref[...]).astype(ya_ref.dtype)

        u = ca_ref[...] * _sigmoid(cg_ref[...])
        uh = jnp.where(first, 0.0, cah_ref[...] * _sigmoid(cgh_ref[...]))
        ext = jnp.concatenate([uh, u], axis=0)
        y = jnp.zeros((tm, 256), F32) + cb_ref[...]
        for kk in range(CONV_K):
            y = y + cw_ref[kk:kk + 1, :] * _roll_up(ext, CONV_HALO - (CONV_K - 1) + kk)[:tm, :]
        _, _, z = _layer_norm(y, lg_ref[...], lb_ref[...])
        yc_ref[...] = (z * _sigmoid(z)).astype(yc_ref.dtype)
        u_ref[...] = u
        y_ref[...] = y

    tok = lambda i: (i, 0)
    par = lambda i: (0, 0)
    t256 = BS((tm, 256), tok)
    return _pc(
        body, "local_fwd", (T // tm,),
        [t256, _halo_specs(tm, T, POOL_HALO, True), t256, _halo_specs(tm, T, CONV_HALO, True),
         t256, _halo_specs(tm, T, CONV_HALO, True),
         BS((256, 256), par), BS((1, 256), par), BS((32, 256), par), BS((1, 256), par), BS((1, 256), par),
         BS((1, 256), par)],
        [t256, t256, t256, t256],
        [SDS((T, 256), MM), SDS((T, 256), MM), SDS((T, 256), F32), SDS((T, 256), F32)],
    )(up, up, ca, ca, cg, cg, bd, pscale, cw, cb, lg, lb)


def _local_bwd(up, dya, ca, cg, u, y, dyc, bd, pscale, cw, lg, lb):
    T = up.shape[0]
    tm = min(512, T)
    n = T // tm

    def body(up_ref, uph_ref, dya_ref, dyan_ref, ca_ref, cg_ref, u_ref, uh_ref, y_ref, yn_ref, dyc_ref, dycn_ref,
             bd_ref, ps_ref, cw_ref, lg_ref, lb_ref,
             dup_ref, dca_ref, dcg_ref, dbd_ref, dps_ref, dcw_ref, dcb_ref, dlg_ref, dlb_ref):
        i = pl.program_id(0)
        first = i == 0
        last = i == n - 1

        @pl.when(first)
        def _():
            for ref in (dbd_ref, dps_ref, dcw_ref, dcb_ref, dlg_ref, dlb_ref):
                ref[...] = jnp.zeros_like(ref)

        ps = ps_ref[...]
        pooled = _pool_terms(up_ref[...], jnp.where(first, 0.0, uph_ref[...]), i, tm).astype(MM)
        dya_t = dya_ref[...]
        dps_ref[...] += jnp.sum(dya_t * _dot(pooled, bd_ref[...]), axis=0, keepdims=True)
        dm = (jnp.concatenate([dya_t, jnp.where(last, 0.0, dyan_ref[...])], axis=0) * ps).astype(MM)
        dbd_ref[...] += _dot_tn(pooled, dm[:tm, :])
        dpool = _dot_nt(dm, bd_ref[...])
        lane = lax.broadcasted_iota(jnp.int32, (1, 256), 1)
        wlen = _group_select(lane, 2.0, 4.0, 8.0, 16.0).astype(F32)
        e = dpool / _pool_count(i, tm, tm + POOL_HALO, wlen)
        f2 = e + _roll_up(e, 1)
        f4 = f2 + _roll_up(f2, 2)
        f8 = f4 + _roll_up(f4, 4)
        f16 = f8 + _roll_up(f8, 8)
        dup_ref[...] = _group_select(lane, f2, f4, f8, f16)[:tm, :] - dpool[:tm, :]

        lgv = lg_ref[...]
        yext = jnp.concatenate([y_ref[...], yn_ref[...]], axis=0)
        dyc = jnp.concatenate([dyc_ref[...], jnp.where(last, 0.0, dycn_ref[...])], axis=0)
        yh, rstd, z = _layer_norm(yext, lgv, lb_ref[...])
        sig = _sigmoid(z)
        dz = dyc * (sig * (1.0 + z * (1.0 - sig)))
        dlg_ref[...] += jnp.sum((dz * yh)[:tm, :], axis=0, keepdims=True)
        dlb_ref[...] += jnp.sum(dz[:tm, :], axis=0, keepdims=True)
        dyh = dz * lgv
        dy = rstd * (dyh - jnp.mean(dyh, axis=-1, keepdims=True) - yh * jnp.mean(dyh * yh, axis=-1, keepdims=True))
        dy_t = dy[:tm, :]
        dcb_ref[...] += jnp.sum(dy_t, axis=0, keepdims=True)
        uext = jnp.concatenate([jnp.where(first, 0.0, uh_ref[...]), u_ref[...]], axis=0)
        du = jnp.zeros((tm, 256), F32)
        for kk in range(CONV_K):
            shifted = _roll_up(uext, CONV_HALO - (CONV_K - 1) + kk)[:tm, :]
            dcw_ref[kk:kk + 1, :] += jnp.sum(dy_t * shifted, axis=0, keepdims=True)
            du = du + cw_ref[kk:kk + 1, :] * _roll_up(dy, CONV_K - 1 - kk)[:tm, :]
        sg = _sigmoid(cg_ref[...])
        dca_ref[...] = du * sg
        dcg_ref[...] = du * ca_ref[...] * sg * (1.0 - sg)

    tok = lambda i: (i, 0)
    par = lambda i: (0, 0)
    t256 = BS((tm, 256), tok)
    p1 = BS((1, 256), par)
    return _pc(
        body, "local_bwd", (n,),
        [t256, _halo_specs(tm, T, POOL_HALO, True), t256, _halo_specs(tm, T, POOL_HALO, False), t256, t256,
         t256, _halo_specs(tm, T, CONV_HALO, True), t256, _halo_specs(tm, T, CONV_HALO, False),
         t256, _halo_specs(tm, T, CONV_HALO, False),
         BS((256, 256), par), p1, BS((32, 256), par), p1, p1],
        [t256, t256, t256, BS((256, 256), par), p1, BS((32, 256), par), p1, p1, p1],
        [SDS((T, 256), F32)] * 3 + [SDS((256, 256), F32), SDS((1, 256), F32), SDS((32, 256), F32)]
        + [SDS((1, 256), F32)] * 3,
    )(up, up, dya, dya, ca, cg, u, u, y, y, dyc, dyc, bd, pscale, cw, lg, lb)


def _head(x, g, target):
    T, D = x.shape
    tm = min(512, T)

    def body(x_ref, g_ref, t_ref, loss_ref, dx_ref, dg_ref):
        @pl.when(pl.program_id(0) == 0)
        def _():
            loss_ref[...] = jnp.zeros_like(loss_ref)
            dg_ref[...] = jnp.zeros_like(dg_ref)

        gv = g_ref[...]
        xh, r, yv = _rms_fwd(x_ref[...], gv)
        err = yv - t_ref[...]
        loss_ref[...] += 0.5 * jnp.sum(jnp.mean(err * err, axis=-1, keepdims=True), axis=0, keepdims=True)
        dx, dg = _rms_bwd(err * (1.0 / D), xh, r, gv)
        dx_ref[...] = dx
        dg_ref[...] += dg

    tok = lambda i: (i, 0)
    par = lambda i: (0, 0)
    return _pc(
        body, "head", (T // tm,),
        [BS((tm, D), tok), BS((1, D), par), BS((tm, D), tok)],
        [BS((1, LANES), par), BS((tm, D), tok), BS((1, D), par)],
        [SDS((1, LANES), F32), SDS((T, D), F32), SDS((1, D), F32)])(x, g, target)


def _adamw(w, gs, m, v, name):
    R, C = w.shape
    tr = R
    for cand in (512, 256, 128, 64, 32, 16, 8):
        if R % cand == 0:
            tr = cand
            break
    ng = len(gs)

    def body(*refs):
        w_ref, g_refs, m_ref, v_ref = refs[0], refs[1:1 + ng], refs[1 + ng], refs[2 + ng]
        g_ref, d_ref, m2_ref, v2_ref = refs[3 + ng:]
        g = g_refs[0][...]
        for r in g_refs[1:]:
            g = g + r[...]
        m2 = ADAM_B1 * m_ref[...] + (1.0 - ADAM_B1) * g
        v2 = ADAM_B2 * v_ref[...] + (1.0 - ADAM_B2) * jnp.square(g)
        m_hat = m2 / (1.0 - ADAM_B1 ** ADAM_STEP)
        v_hat = v2 / (1.0 - ADAM_B2 ** ADAM_STEP)
        g_ref[...] = g
        d_ref[...] = -ADAM_LR * (m_hat / (jnp.sqrt(v_hat) + ADAM_EPS) + ADAM_WD * w_ref[...])
        m2_ref[...] = m2
        v2_ref[...] = v2

    blk = BS((tr, C), lambda i: (i, 0))
    return _pc(body, name, (R // tr,), [blk] * (3 + ng), [blk] * 4, [SDS((R, C), F32)] * 4)(w, *gs, m, v)


def _sum_parts(owns, recvs, name):
    L = len(owns)
    R, C = owns[0].shape
    tr = next(t for t in (512, 256, 128, 64, 32, 16) if R % t == 0)

    def body(*refs):
        l = pl.program_id(0)
        s_ref = refs[2 * L]
        for ll in range(L):
            @pl.when(l == ll)
            def _(o_ref=refs[ll], r_ref=refs[L + ll]):
                s_ref[...] = ((o_ref[...] + r_ref[0].astype(F32)) + r_ref[1].astype(F32)) + r_ref[2].astype(F32)

    own_specs = [BS((tr, C), lambda l, i, ll=ll: (jnp.where(l == ll, i, 0), 0)) for ll in range(L)]
    recv_specs = [BS((3, tr, C), lambda l, i, ll=ll: (0, jnp.where(l == ll, i, 0), 0)) for ll in range(L)]
    return _pc(body, name, (L, R // tr), own_specs + recv_specs,
               BS((None, tr, C), lambda l, i: (l, i, 0)), SDS((L, R, C), F32))(*owns, *recvs)


def _sum8(parts, name):
    _, R, C = parts.shape

    def body(p_ref, s_ref):
        acc = p_ref[0]
        for d in range(1, 8):
            acc = acc + p_ref[d]
        s_ref[...] = acc

    return _pc(body, name, (1,), [BS((8, R, C), lambda i: (0, 0, 0))], BS((R, C), lambda i: (0, 0)),
               SDS((R, C), F32))(parts)


def _position():
    return lax.axis_index("x"), lax.axis_index("y"), lax.axis_index("c")


CHIP_FLIPS = ((1, 0), (0, 1), (1, 1))


class _GatherChips:
    @staticmethod
    def scratch(n):
        return [pltpu.SemaphoreType.DMA((3 * n,)), pltpu.SemaphoreType.DMA((3 * n,)), pltpu.SemaphoreType.DMA((n,))]

    @staticmethod
    def out_shape(block):
        return SDS((4,) + tuple(block.shape), block.dtype)

    @staticmethod
    def _copies(ins, outs, send_sems, recv_sems, local_sems, arrivals):
        x, y, c = _position()
        local, remote = [], []
        for i, (in_ref, out_ref) in enumerate(zip(ins, outs)):
            local.append(pltpu.make_async_copy(in_ref, out_ref.at[2 * x + y], local_sems.at[i]))
            for k, (fx, fy) in enumerate(CHIP_FLIPS):
                slot = 2 * (x ^ fx) + (y ^ fy) if arrivals else 2 * x + y
                remote.append(pltpu.make_async_remote_copy(
                    src_ref=in_ref, dst_ref=out_ref.at[slot], send_sem=send_sems.at[3 * i + k],
                    recv_sem=recv_sems.at[3 * i + k], device_id=(x ^ fx, y ^ fy, c), device_id_type=MESH))
        return local, remote

    @classmethod
    def start(cls, ins, outs, *sems):
        local, sends = cls._copies(ins, outs, *sems, arrivals=False)
        for cp in local + sends:
            cp.start()

    @classmethod
    def wait(cls, ins, outs, *sems):
        local, arrivals = cls._copies(ins, outs, *sems, arrivals=True)
        for cp in arrivals:
            cp.wait_recv()
        for cp in arrivals:
            cp.wait_send()
        for cp in local:
            cp.wait()


class _Symmetric:
    @classmethod
    def start(cls, ins, outs, *sems):
        for cp in cls._copies(ins, outs, *sems):
            cp.start()

    @classmethod
    def wait(cls, ins, outs, *sems):
        copies = cls._copies(ins, outs, *sems)
        for cp in copies:
            cp.wait_recv()
        for cp in copies:
            cp.wait_send()


class _ScatterChips(_Symmetric):
    @staticmethod
    def scratch(n):
        return [pltpu.SemaphoreType.DMA((3 * n,)), pltpu.SemaphoreType.DMA((3 * n,))]

    @staticmethod
    def out_shape(parts):
        return SDS((3,) + tuple(parts.shape[1:]), parts.dtype)

    @staticmethod
    def _copies(ins, outs, send_sems, recv_sems):
        x, y, c = _position()
        return [
            pltpu.make_async_remote_copy(
                src_ref=in_ref.at[2 * (x ^ fx) + (y ^ fy)], dst_ref=out_ref.at[k],
                send_sem=send_sems.at[3 * i + k], recv_sem=recv_sems.at[3 * i + k],
                device_id=(x ^ fx, y ^ fy, c), device_id_type=MESH)
            for i, (in_ref, out_ref) in enumerate(zip(ins, outs)) for k, (fx, fy) in enumerate(CHIP_FLIPS)]


class _SwapCores(_Symmetric):
    @staticmethod
    def scratch(n):
        return [pltpu.SemaphoreType.DMA((n,)), pltpu.SemaphoreType.DMA((n,))]

    @staticmethod
    def out_shape(block):
        return SDS(block.shape, block.dtype)

    @staticmethod
    def _copies(ins, outs, send_sems, recv_sems):
        x, y, c = _position()
        return [
            pltpu.make_async_remote_copy(
                src_ref=in_ref, dst_ref=out_ref, send_sem=send_sems.at[i], recv_sem=recv_sems.at[i],
                device_id=(x, y, 1 - c), device_id_type=MESH)
            for i, (in_ref, out_ref) in enumerate(zip(ins, outs))]


def _exchange(kind, arrays, name):
    n = len(arrays)

    def body(*refs):
        ins, outs, sems = refs[:n], refs[n:2 * n], refs[2 * n:]
        kind.start(ins, outs, *sems)
        kind.wait(ins, outs, *sems)

    return pl.pallas_call(body, out_shape=[kind.out_shape(a) for a in arrays], in_specs=[ANY] * n,
                          out_specs=[ANY] * n, name=name, scratch_shapes=kind.scratch(n))(*arrays)


def _gather_all(block, name):
    R, C = block.shape
    flips = [(fx, fy, fc) for fx in (0, 1) for fy in (0, 1) for fc in (0, 1)][1:]

    def body(in_ref, out_ref, send_sems, recv_sems, local_sem):
        x, y, c = _position()
        mine = out_ref.at[4 * x + 2 * y + c]
        local = pltpu.make_async_copy(in_ref, mine, local_sem)
        local.start()
        copies = [
            pltpu.make_async_remote_copy(
                src_ref=in_ref, dst_ref=mine, send_sem=send_sems.at[k], recv_sem=recv_sems.at[k],
                device_id=(x ^ fx, y ^ fy, c ^ fc), device_id_type=MESH)
            for k, (fx, fy, fc) in enumerate(flips)]
        for cp in copies:
            cp.start()
        for k, (fx, fy, fc) in enumerate(flips):
            theirs = out_ref.at[4 * (x ^ fx) + 2 * (y ^ fy) + (c ^ fc)]
            pltpu.make_async_remote_copy(
                src_ref=in_ref, dst_ref=theirs, send_sem=send_sems.at[k], recv_sem=recv_sems.at[k],
                device_id=(x ^ fx, y ^ fy, c ^ fc), device_id_type=MESH).wait_recv()
        for cp in copies:
            cp.wait_send()
        local.wait()

    return pl.pallas_call(
        body, out_shape=SDS((8, R, C), block.dtype), in_specs=[ANY], out_specs=ANY, name=name,
        scratch_shapes=[pltpu.SemaphoreType.DMA((7,)), pltpu.SemaphoreType.DMA((7,)), pltpu.SemaphoreType.DMA(())])(block)


BIG = ("ffn1_w_gate", "ffn1_w_up", "ffn1_w_down", "w_in", "w_out", "ffn2_w_gate", "ffn2_w_up", "ffn2_w_down")
COL_SHARDED = ("ffn1_w_gate", "ffn1_w_up", "w_in", "ffn2_w_gate", "ffn2_w_up")
EARLY = tuple((n, 0) for n in ("ffn1_w_gate", "ffn1_w_up", "ffn1_w_down", "w_in"))
LATE = tuple((n, 0) for n in ("w_out", "ffn2_w_gate", "ffn2_w_up", "ffn2_w_down")) + tuple((n, 1) for n in BIG)


def _to_shards(name, full):
    r, c = full.shape
    if name in COL_SHARDED:
        return full.reshape(r, 4, c // 4).transpose(1, 0, 2)
    return full.reshape(4, r // 4, c)


def _own_shard(name, full, chip):
    r, c = full.shape
    if name in COL_SHARDED:
        return lax.dynamic_slice_in_dim(full, chip * (c // 4), c // 4, axis=1)
    return lax.dynamic_slice_in_dim(full, chip * (r // 4), r // 4, axis=0)


def _from_shards(name, sh):
    _, r, c = sh.shape
    if name in COL_SHARDED:
        return sh.transpose(1, 0, 2).reshape(r, 4 * c)
    return sh.reshape(4 * r, c)


def _pad_w_in(w):
    return jnp.concatenate([w[:, :1792], w[:, 1800:2312], w[:, 1792:1800], jnp.zeros((w.shape[0], 248), w.dtype)], axis=1)


def _unpad_w_in(g):
    return jnp.concatenate([g[:, :1792], g[:, 2304:2312], g[:, 1792:2304]], axis=1)


def _block_diag(pw):
    out = jnp.zeros((256, 256), pw.dtype)
    for gidx in range(4):
        out = lax.dynamic_update_slice(out, pw[gidx], (64 * gidx, 64 * gidx))
    return out


SMALL = ("ffn1_norm", "mix_norm", "pool_w", "pool_scale", "forget_bias", "conv_b", "conv_ln_g", "conv_ln_b",
         "ffn2_norm", "final_norm")


def _pack_small(arrs):
    rows = []
    for a in arrs:
        flat = a.reshape(-1)
        flat = jnp.pad(flat, (0, -flat.shape[0] % LANES))
        rows.append(flat.reshape(-1, LANES))
    total = sum(r.shape[0] for r in rows)
    if total % 8:
        rows.append(jnp.zeros((-total % 8, LANES), F32))
    return jnp.concatenate(rows, axis=0)


def _unpack_small(buf, shapes):
    out, off = [], 0
    for shp in shapes:
        n = math.prod(shp)
        nr = -(-n // LANES)
        out.append(buf[off:off + nr].reshape(-1)[:n].reshape(shp))
        off += nr
    return out


class _Late:
    def __init__(self, shards):
        self.blocks = [shards[n][l] for n, l in LATE]

    @staticmethod
    def install(W, gathered):
        for (n, l), sh in zip(LATE, gathered):
            W[n][l] = _from_shards(n, sh)

    @staticmethod
    def parts(grads):
        return _grad_parts(grads, LATE)


def _grad_parts(grads, pieces):
    return [_to_shards(n, grads[n][l]).astype(MM) for n, l in pieces]


def _forward_backward(x, target, W, late=None):
    T = x.shape[0]
    L = W["ffn1_norm"].shape[0]
    saved = []
    late_recv = None
    for l in range(L):
        g1, gm, g2 = (W[n][l][None, :] for n in ("ffn1_norm", "mix_norm", "ffn2_norm"))
        x1 = _ffn_fwd(x, g1, W["ffn1_w_gate"][l], W["ffn1_w_up"][l], W["ffn1_w_down"][l])
        w_in = _pad_w_in(W["w_in"][l])
        up, q, k, v, ca, cg, zf = _mix_in_fwd(x1, gm, w_in)
        fb = jnp.pad(W["forget_bias"][l], (0, LANES - HEADS))[None, :]
        F = _fgate_fwd(zf, fb)
        if late is not None and l == 0:
            yb, lse, *gathered = _attn_fwd(q, k, v, F, hosted=(_GatherChips, late.blocks))
            late.install(W, gathered)
        else:
            yb, lse = _attn_fwd(q, k, v, F)
        bd = _block_diag(W["pool_w"][l]).astype(MM)
        ps, cb, lg, lb = (W[n][l][None, :] for n in ("pool_scale", "conv_b", "conv_ln_g", "conv_ln_b"))
        cw = jnp.pad(W["conv_w"][l], ((0, 1), (0, 0)))
        ya, yc, cu, cy = _local_fwd(up, ca, cg, bd, ps, cw, cb, lg, lb)
        x2 = _mix_out_fwd(x1, ya, yb, yc, W["w_out"][l])
        x3 = _ffn_fwd(x2, g2, W["ffn2_w_gate"][l], W["ffn2_w_up"][l], W["ffn2_w_down"][l])
        saved.append(dict(x0=x, x1=x1, x2=x2, w_in=w_in, up=up, ca=ca, cg=cg, zf=zf, fb=fb, F=F,
                          q=q, k=k, v=v, lse=lse, bd=bd, cw=cw, cu=cu, cy=cy, ya=ya, yb=yb, yc=yc))
        x = x3

    loss, dx, dgf = _head(x, W["final_norm"][None, :], target)
    grads = {n: [None] * L for n in W if n != "final_norm"}
    grads["final_norm"] = dgf[0]
    for l in reversed(range(L)):
        s = saved[l]
        g1, gm, g2 = (W[n][l][None, :] for n in ("ffn1_norm", "mix_norm", "ffn2_norm"))
        ps, lg, lb = (W[n][l][None, :] for n in ("pool_scale", "conv_ln_g", "conv_ln_b"))
        dx, h, dy, da, db, sact, dg = _ffn_bwd(s["x2"], dx, g2, W["ffn2_w_gate"][l], W["ffn2_w_up"][l], W["ffn2_w_down"][l])
        grads["ffn2_norm"][l] = dg[0]
        grads["ffn2_w_gate"][l] = _wgrad(h, da, "wgrad_gate")
        grads["ffn2_w_up"][l] = _wgrad(h, db, "wgrad_up")
        grads["ffn2_w_down"][l] = _wgrad(sact, dy, "wgrad_down")
        dya, dyb, dyc = _mix_out_bwd(dx, W["w_out"][l])
        grads["w_out"][l] = jnp.concatenate(
            [_wgrad(s["ya"], dx, "wgrad_out_a"), _wgrad(s["yb"], dx, "wgrad_out_b"), _wgrad(s["yc"], dx, "wgrad_out_c")], axis=0)
        if late is not None and l == 0:
            dq, dk, dv, dfq, dfk, *late_recv = _attn_bwd(s["q"], s["k"], s["v"], s["F"], s["yb"], s["lse"], dyb,
                                                        hosted=(_ScatterChips, late.parts(grads)))
        else:
            dq, dk, dv, dfq, dfk = _attn_bwd(s["q"], s["k"], s["v"], s["F"], s["yb"], s["lse"], dyb)
        dfk_cols = jnp.pad(dfk.transpose(0, 2, 1, 3).reshape(HEADS, T).T, ((0, 0), (0, LANES - HEADS)))
        dzf, dfb = _fgate_bwd(s["zf"], s["fb"], dfq, dfk_cols)
        grads["forget_bias"][l] = dfb[0, :HEADS]
        dup, dca, dcg, dbd, dps, dcw, dcb, dlg, dlb = _local_bwd(
            s["up"], dya, s["ca"], s["cg"], s["cu"], s["cy"], dyc, s["bd"], ps, s["cw"], lg, lb)
        grads["pool_w"][l] = jnp.stack([dbd[64 * i:64 * i + 64, 64 * i:64 * i + 64] for i in range(4)])
        grads["pool_scale"][l], grads["conv_b"][l] = dps[0], dcb[0]
        grads["conv_ln_g"][l], grads["conv_ln_b"][l] = dlg[0], dlb[0]
        grads["conv_w"][l] = dcw[:CONV_K]
        dx, h, dp, dg = _mix_in_bwd(s["x1"], dx, gm, s["w_in"], dup, dq, dk, dv, dca, dcg, dzf)
        grads["mix_norm"][l] = dg[0]
        grads["w_in"][l] = _unpad_w_in(_wgrad(h, dp, "wgrad_in"))
        dx, h, dy, da, db, sact, dg = _ffn_bwd(s["x0"], dx, g1, W["ffn1_w_gate"][l], W["ffn1_w_up"][l], W["ffn1_w_down"][l])
        grads["ffn1_norm"][l] = dg[0]
        grads["ffn1_w_gate"][l] = _wgrad(h, da, "wgrad_gate")
        grads["ffn1_w_up"][l] = _wgrad(h, db, "wgrad_up")
        grads["ffn1_w_down"][l] = _wgrad(sact, dy, "wgrad_down")
    grads = {n: (jnp.stack(g) if isinstance(g, list) and n not in BIG else g) for n, g in grads.items()}
    return loss, dx, grads, late_recv


NAMES = ("ffn1_norm", "ffn1_w_gate", "ffn1_w_up", "ffn1_w_down", "mix_norm", "w_in", "pool_w", "pool_scale",
         "forget_bias", "conv_w", "conv_b", "conv_ln_g", "conv_ln_b", "w_out", "ffn2_norm", "ffn2_w_gate",
         "ffn2_w_up", "ffn2_w_down", "final_norm")


def kernel(x, ffn1_norm, ffn1_w_gate, ffn1_w_up, ffn1_w_down, mix_norm, w_in, pool_w, pool_scale, forget_bias, conv_w, conv_b, conv_ln_g, conv_ln_b, w_out, ffn2_norm, ffn2_w_gate, ffn2_w_up, ffn2_w_down, final_norm, loss_target, m_ffn1_norm, m_ffn1_w_gate, m_ffn1_w_up, m_ffn1_w_down, m_mix_norm, m_w_in, m_pool_w, m_pool_scale, m_forget_bias, m_conv_w, m_conv_b, m_conv_ln_g, m_conv_ln_b, m_w_out, m_ffn2_norm, m_ffn2_w_gate, m_ffn2_w_up, m_ffn2_w_down, m_final_norm, v_ffn1_norm, v_ffn1_w_gate, v_ffn1_w_up, v_ffn1_w_down, v_mix_norm, v_w_in, v_pool_w, v_pool_scale, v_forget_bias, v_conv_w, v_conv_b, v_conv_ln_g, v_conv_ln_b, v_w_out, v_ffn2_norm, v_ffn2_w_gate, v_ffn2_w_up, v_ffn2_w_down, v_final_norm):
    args = (ffn1_norm, ffn1_w_gate, ffn1_w_up, ffn1_w_down, mix_norm, w_in, pool_w, pool_scale, forget_bias, conv_w, conv_b, conv_ln_g, conv_ln_b, w_out, ffn2_norm, ffn2_w_gate, ffn2_w_up, ffn2_w_down, final_norm)
    ms = (m_ffn1_norm, m_ffn1_w_gate, m_ffn1_w_up, m_ffn1_w_down, m_mix_norm, m_w_in, m_pool_w, m_pool_scale, m_forget_bias, m_conv_w, m_conv_b, m_conv_ln_g, m_conv_ln_b, m_w_out, m_ffn2_norm, m_ffn2_w_gate, m_ffn2_w_up, m_ffn2_w_down, m_final_norm)
    vs = (v_ffn1_norm, v_ffn1_w_gate, v_ffn1_w_up, v_ffn1_w_down, v_mix_norm, v_w_in, v_pool_w, v_pool_scale, v_forget_bias, v_conv_w, v_conv_b, v_conv_ln_g, v_conv_ln_b, v_w_out, v_ffn2_norm, v_ffn2_w_gate, v_ffn2_w_up, v_ffn2_w_down, v_final_norm)
    P = dict(zip(NAMES, args))
    M = dict(zip(NAMES, ms))
    V = dict(zip(NAMES, vs))
    xi, yi, _ = _position()
    chip = 2 * xi + yi

    W = {n: P[n] for n in SMALL}
    W.update({n: [None] * P[n].shape[0] for n in BIG})
    shards = {n: [P[n][l].astype(MM) for l in range(P[n].shape[0])] for n in BIG}
    early = _exchange(_GatherChips, [shards[n][l] for n, l in EARLY] + [P["conv_w"]], "gather_early_weights")
    for (n, l), sh in zip(EARLY, early):
        W[n][l] = _from_shards(n, sh)
    W["conv_w"] = early[-1].transpose(1, 2, 0, 3).reshape(P["conv_w"].shape[0], CONV_K, 256)

    loss_part, dx, G, recv_late = _forward_backward(x[0], loss_target[0], W, _Late(shards))
    loss = lax.psum(loss_part[0, 0], ("x", "y", "c"))

    small_shapes = [P[n].shape for n in SMALL] + [G["conv_w"].shape]
    small_parts = _gather_all(_pack_small([G[n] for n in SMALL] + [G["conv_w"]]), "gather_small_grads")
    small_sum = _sum8(small_parts, "sum_small_grads")
    nsmall = sum(-(-math.prod(s) // LANES) for s in small_shapes[:-1])
    nsmall_pad = nsmall + (-nsmall % 8)
    w_s, m_s, v_s = (_pack_small([D[n] for n in SMALL]) for D in (P, M, V))
    outs_small = _adamw(w_s, [small_sum[:nsmall_pad]], m_s, v_s, "adamw_small")
    res = {}
    for kind, buf in zip(("g", "d", "m", "v"), outs_small):
        for n, a in zip(SMALL, _unpack_small(buf, small_shapes[:-1])):
            res[(kind, n)] = a
    g_cw_full = _unpack_small(small_sum[nsmall:], [small_shapes[-1]])[0]
    g_cw = lax.dynamic_slice_in_dim(g_cw_full, chip * 64, 64, axis=2)
    outs_cw = _adamw(_pack_small([P["conv_w"]]), [_pack_small([g_cw])], _pack_small([M["conv_w"]]),
                     _pack_small([V["conv_w"]]), "adamw_conv_w")
    for kind, buf in zip(("g", "d", "m", "v"), outs_cw):
        res[(kind, "conv_w")] = _unpack_small(buf, [P["conv_w"].shape])[0]

    recv_early = _exchange(_ScatterChips, _grad_parts(G, EARLY), "scatter_early_grads")
    recv = dict(zip(LATE + EARLY, list(recv_late) + list(recv_early)))
    parts = [_sum_parts([_own_shard(n, G[n][l], chip) for l in range(P[n].shape[0])],
                        [recv[(n, l)] for l in range(P[n].shape[0])], "sum_" + n) for n in BIG]
    others = _exchange(_SwapCores, parts, "swap_core_grads")
    for n, ga, gb in zip(BIG, parts, others):
        shp = P[n].shape
        two_d = (shp[0] * shp[1], shp[2])
        outs = _adamw(P[n].reshape(two_d), [ga.reshape(two_d), gb.reshape(two_d)], M[n].reshape(two_d),
                      V[n].reshape(two_d), "adamw_" + n)
        for kind, a in zip(("g", "d", "m", "v"), outs):
            res[(kind, n)] = a.reshape(shp)

    return (loss, dx[None], *[res[("g", n)] for n in NAMES], *[res[("d", n)] for n in NAMES],
            *[res[("m", n)] for n in NAMES], *[res[("v", n)] for n in NAMES])
```

```python
import functools
import math

import jax
import jax.numpy as jnp
from jax import lax
from jax.experimental import pallas as pl
from jax.experimental.pallas import tpu as pltpu

F32 = jnp.float32
MM = jnp.bfloat16
NORM_EPS = 1e-6
HEADS = 8
HEAD_DIM = 64
POOL_WINDOWS = (2, 4, 8, 16)
CONV_K = 31
LANES = 128
VMEM_LIMIT = 56 * 2**20
FFN_BWD_ROWS = 256

ADAM_LR = 0.001
ADAM_B1 = 0.9
ADAM_B2 = 0.999
ADAM_EPS = 1e-08
ADAM_WD = 0.01
ADAM_STEP = 10

MESH = pl.DeviceIdType.MESH
BS = pl.BlockSpec
SDS = jax.ShapeDtypeStruct
ANY = pl.BlockSpec(memory_space=pl.ANY)


def _dot(a, b):
    return jnp.dot(a, b, preferred_element_type=F32)


def _dot_nt(a, b):
    return lax.dot_general(a, b, (((1,), (1,)), ((), ())), preferred_element_type=F32)


def _dot_tn(a, b):
    return lax.dot_general(a, b, (((0,), (0,)), ((), ())), preferred_element_type=F32)


def _pc(body, name, grid, in_specs, out_specs, out_shape, scratch=()):
    return pl.pallas_call(
        body, out_shape=out_shape, grid=grid, in_specs=in_specs, out_specs=out_specs,
        scratch_shapes=list(scratch), name=name,
        compiler_params=pltpu.CompilerParams(
            dimension_semantics=("arbitrary",) * len(grid), vmem_limit_bytes=VMEM_LIMIT))


def _rms_fwd(x, g):
    r = lax.rsqrt(jnp.mean(x * x, axis=-1, keepdims=True) + NORM_EPS)
    xh = x * r
    return xh, r, xh * g


def _rms_bwd(dh, xh, r, g):
    dxh = dh * g
    dx = r * (dxh - xh * jnp.mean(dxh * xh, axis=-1, keepdims=True))
    return dx, jnp.sum(dh * xh, axis=0, keepdims=True)


def _sigmoid(x):
    return jax.nn.sigmoid(x)


def _ffn_fwd(x, g, wg, wu, wd, hosted=None):
    T, D = x.shape
    F = wg.shape[1]
    tm, nf = min(512, T), 2
    fc = F // nf
    nt = T // tm
    h_in, h_out, h_shape, h_scratch = _hosted_specs(hosted)

    def body(*refs):
        i, j = pl.program_id(0), pl.program_id(1)
        refs, finish = _hosted_edges(hosted, refs, 5, 1, (i == 0) & (j == 0), (i == nt - 1) & (j == nf - 1))
        x_ref, g_ref, wg_ref, wu_ref, wd_ref, o_ref, h_scr, acc_scr = refs

        @pl.when(j == 0)
        def _():
            _, _, hg = _rms_fwd(x_ref[...], g_ref[...])
            h_scr[...] = hg.astype(h_scr.dtype)
            acc_scr[...] = jnp.zeros_like(acc_scr)

        h = h_scr[...]
        a = _dot(h, wg_ref[...])
        b = _dot(h, wu_ref[...])
        s = (a * _sigmoid(a)) * b
        acc_scr[...] += _dot(s.astype(MM), wd_ref[...])

        @pl.when(j == nf - 1)
        def _():
            o_ref[...] = x_ref[...] + 0.5 * acc_scr[...]

        finish()

    return _pc(
        body, "ffn_fwd" + ("_hosting" if hosted else ""), (nt, nf),
        [BS((tm, D), lambda i, j: (i, 0)), BS((1, D), lambda i, j: (0, 0)),
         BS((D, fc), lambda i, j: (0, j)), BS((D, fc), lambda i, j: (0, j)), BS((fc, D), lambda i, j: (j, 0))] + h_in,
        [BS((tm, D), lambda i, j: (i, 0))] + h_out, [SDS((T, D), F32)] + h_shape,
        scratch=[pltpu.VMEM((tm, D), MM), pltpu.VMEM((tm, D), F32)] + h_scratch,
    )(x, g, wg, wu, wd, *(hosted[1] if hosted else []))


def _ffn_bwd(x, dout, g, wg, wu, wd, hosted=None):
    T, D = x.shape
    F = wg.shape[1]
    tm, nf = min(FFN_BWD_ROWS, T), 2
    fc = F // nf
    nt = T // tm
    h_in, h_out, h_shape, h_scratch = _hosted_specs(hosted)

    def body(*refs):
        i, j = pl.program_id(0), pl.program_id(1)
        refs, finish = _hosted_edges(hosted, refs, 6, 7, (i == 0) & (j == 0), (i == nt - 1) & (j == nf - 1))
        (x_ref, do_ref, g_ref, wg_ref, wu_ref, wd_ref,
         dx_ref, h_ref, dy_ref, da_ref, db_ref, s_ref, dg_ref, dh_scr) = refs

        @pl.when(j == 0)
        def _():
            _, _, hg = _rms_fwd(x_ref[...], g_ref[...])
            h_ref[...] = hg.astype(h_ref.dtype)
            dy_ref[...] = (0.5 * do_ref[...]).astype(dy_ref.dtype)
            dh_scr[...] = jnp.zeros_like(dh_scr)

        @pl.when((i == 0) & (j == 0))
        def _():
            dg_ref[...] = jnp.zeros_like(dg_ref)

        h = h_ref[...]
        dy = dy_ref[...]
        a = _dot(h, wg_ref[...])
        b = _dot(h, wu_ref[...])
        ds = _dot_nt(dy, wd_ref[...])
        sig = _sigmoid(a)
        sl = a * sig
        s_ref[...] = (sl * b).astype(s_ref.dtype)
        db = (ds * sl).astype(MM)
        da = (ds * b * (sig * (1.0 + a * (1.0 - sig)))).astype(MM)
        da_ref[...] = da
        db_ref[...] = db
        dh_scr[...] += _dot_nt(da, wg_ref[...]) + _dot_nt(db, wu_ref[...])

        @pl.when(j == nf - 1)
        def _():
            gv = g_ref[...]
            xh, r, _ = _rms_fwd(x_ref[...], gv)
            dx, dg = _rms_bwd(dh_scr[...], xh, r, gv)
            dx_ref[...] = do_ref[...] + dx
            dg_ref[...] += dg

        finish()

    tok = lambda i, j: (i, 0)
    return _pc(
        body, "ffn_bwd" + ("_hosting" if hosted else ""), (nt, nf),
        [BS((tm, D), tok), BS((tm, D), tok), BS((1, D), lambda i, j: (0, 0)),
         BS((D, fc), lambda i, j: (0, j)), BS((D, fc), lambda i, j: (0, j)), BS((fc, D), lambda i, j: (j, 0))] + h_in,
        [BS((tm, D), tok), BS((tm, D), tok), BS((tm, D), tok),
         BS((tm, fc), lambda i, j: (i, j)), BS((tm, fc), lambda i, j: (i, j)), BS((tm, fc), lambda i, j: (i, j)),
         BS((1, D), lambda i, j: (0, 0))] + h_out,
        [SDS((T, D), F32), SDS((T, D), MM), SDS((T, D), MM),
         SDS((T, F), MM), SDS((T, F), MM), SDS((T, F), MM), SDS((1, D), F32)] + h_shape,
        scratch=[pltpu.VMEM((tm, D), F32)] + h_scratch)(x, dout, g, wg, wu, wd, *(hosted[1] if hosted else []))


def _wgrad(a, b, name, hosted=None):
    T, K = a.shape
    N = b.shape[1]
    tt = min(512, T)
    tn = N
    for cand in (1408, 1280, 1024, 512, 256, 128):
        if N % cand == 0 and K * cand * 4 <= 6 * 2**20:
            tn = cand
            break
    nn, nt = N // tn, T // tt
    h_in, h_out, h_shape, h_scratch = _hosted_specs(hosted)

    def body(*refs):
        n, t = pl.program_id(0), pl.program_id(1)
        refs, finish = _hosted_edges(hosted, refs, 2, 1, (n == 0) & (t == 0), (n == nn - 1) & (t == nt - 1))
        a_ref, b_ref, o_ref = refs

        @pl.when(t == 0)
        def _():
            o_ref[...] = jnp.zeros_like(o_ref)

        o_ref[...] += _dot_tn(a_ref[...].astype(MM), b_ref[...].astype(MM))
        finish()

    res = _pc(
        body, name + ("_hosting" if hosted else ""), (nn, nt),
        [BS((tt, K), lambda n, t: (t, 0)), BS((tt, tn), lambda n, t: (t, n))] + h_in,
        [BS((K, tn), lambda n, t: (0, n))] + h_out, [SDS((K, N), F32)] + h_shape,
        scratch=h_scratch)(a, b, *(hosted[1] if hosted else []))
    return res if hosted else res[0]


C_POOL, C_Q, C_K, C_V, C_CA, C_CG, C_ZF, C_END = 0, 256, 768, 1280, 1792, 2048, 2304, 2560


def _mix_in_fwd(x, g, w):
    T, D = x.shape
    tm = min(512, T)

    def body(x_ref, g_ref, w_ref, up_ref, q_ref, k_ref, v_ref, ca_ref, cg_ref, zf_ref):
        _, _, hg = _rms_fwd(x_ref[...], g_ref[...])
        p = _dot(hg.astype(MM), w_ref[...])
        up_ref[...] = p[:, C_POOL:C_Q]
        q_ref[...] = p[:, C_Q:C_K].astype(q_ref.dtype)
        k_ref[...] = p[:, C_K:C_V].astype(k_ref.dtype)
        v_ref[...] = p[:, C_V:C_CA].astype(v_ref.dtype)
        ca_ref[...] = p[:, C_CA:C_CG]
        cg_ref[...] = p[:, C_CG:C_ZF]
        zf_ref[...] = p[:, C_ZF:C_ZF + LANES]

    tok = lambda i: (i, 0)
    widths = (256, 512, 512, 512, 256, 256, 128)
    dtypes = (F32, MM, MM, MM, F32, F32, F32)
    return _pc(
        body, "mix_in_fwd", (T // tm,),
        [BS((tm, D), tok), BS((1, D), lambda i: (0, 0)), BS((D, C_END), lambda i: (0, 0))],
        [BS((tm, wd), tok) for wd in widths],
        [SDS((T, wd), dt) for wd, dt in zip(widths, dtypes)])(x, g, w)


def _mix_in_bwd(x, dout, g, w, dup, dq, dk, dv, dca, dcg, dzf):
    T, D = x.shape
    tm = min(512, T)

    def body(x_ref, do_ref, g_ref, w_ref, dup_ref, dq_ref, dk_ref, dv_ref, dca_ref, dcg_ref, dzf_ref,
             dx_ref, h_ref, dp_ref, dg_ref):
        @pl.when(pl.program_id(0) == 0)
        def _():
            dg_ref[...] = jnp.zeros_like(dg_ref)

        gv = g_ref[...]
        xh, r, hg = _rms_fwd(x_ref[...], gv)
        h_ref[...] = hg.astype(h_ref.dtype)
        for ref, lo, hi in ((dup_ref, C_POOL, C_Q), (dq_ref, C_Q, C_K), (dk_ref, C_K, C_V), (dv_ref, C_V, C_CA),
                            (dca_ref, C_CA, C_CG), (dcg_ref, C_CG, C_ZF), (dzf_ref, C_ZF, C_ZF + LANES)):
            dp_ref[:, lo:hi] = ref[...].astype(dp_ref.dtype)
        dp_ref[:, C_ZF + LANES:C_END] = jnp.zeros((tm, C_END - C_ZF - LANES), dp_ref.dtype)
        dh = _dot_nt(dp_ref[...], w_ref[...])
        dx, dg = _rms_bwd(dh, xh, r, gv)
        dx_ref[...] = do_ref[...] + dx
        dg_ref[...] += dg

    tok = lambda i: (i, 0)
    widths = (256, 512, 512, 512, 256, 256, 128)
    return _pc(
        body, "mix_in_bwd", (T // tm,),
        [BS((tm, D), tok), BS((tm, D), tok), BS((1, D), lambda i: (0, 0)), BS((D, C_END), lambda i: (0, 0))]
        + [BS((tm, wd), tok) for wd in widths],
        [BS((tm, D), tok), BS((tm, D), tok), BS((tm, C_END), tok), BS((1, D), lambda i: (0, 0))],
        [SDS((T, D), F32), SDS((T, D), MM), SDS((T, C_END), MM), SDS((1, D), F32)],
    )(x, dout, g, w, dup, dq, dk, dv, dca, dcg, dzf)


def _mix_out_fwd(x, ya, yb, yc, wo):
    T, D = x.shape
    tm = min(512, T)

    def body(x_ref, ya_ref, yb_ref, yc_ref, wo_ref, o_ref):
        o_ref[...] = (x_ref[...] + _dot(ya_ref[...].astype(MM), wo_ref[0:256, :])
                      + _dot(yb_ref[...].astype(MM), wo_ref[256:768, :])
                      + _dot(yc_ref[...].astype(MM), wo_ref[768:1024, :]))

    tok = lambda i: (i, 0)
    return _pc(
        body, "mix_out_fwd", (T // tm,),
        [BS((tm, D), tok), BS((tm, 256), tok), BS((tm, 512), tok), BS((tm, 256), tok), BS((D, D), lambda i: (0, 0))],
        BS((tm, D), tok), SDS((T, D), F32))(x, ya, yb, yc, wo)


def _mix_out_bwd(dx, wo):
    T, D = dx.shape
    tm = min(512, T)

    def body(dx_ref, wo_ref, dya_ref, dyb_ref, dyc_ref):
        dy = _dot_nt(dx_ref[...].astype(MM), wo_ref[...])
        dya_ref[...] = dy[:, 0:256]
        dyb_ref[...] = dy[:, 256:768]
        dyc_ref[...] = dy[:, 768:1024]

    tok = lambda i: (i, 0)
    return _pc(
        body, "mix_out_bwd", (T // tm,),
        [BS((tm, D), tok), BS((D, D), lambda i: (0, 0))],
        [BS((tm, 256), tok), BS((tm, 512), tok), BS((tm, 256), tok)],
        [SDS((T, 256), F32), SDS((T, 512), F32), SDS((T, 256), F32)])(dx, wo)


def _fgate_fwd(zf, bias):
    T = zf.shape[0]
    tc = min(256, T)

    def body(z_ref, b_ref, f_ref, carry):
        @pl.when(pl.program_id(0) == 0)
        def _():
            carry[...] = jnp.zeros_like(carry)

        z = z_ref[...] + b_ref[...]
        logf = jnp.minimum(z, 0.0) - jnp.log(1.0 + jnp.exp(-jnp.abs(z)))
        row = lax.broadcasted_iota(jnp.int32, (tc, tc), 0)
        col = lax.broadcasted_iota(jnp.int32, (tc, tc), 1)
        tri = (col <= row).astype(F32)
        f_ref[...] = jnp.dot(tri, logf, precision=lax.Precision.HIGHEST, preferred_element_type=F32) + carry[...]
        carry[...] += jnp.sum(logf, axis=0, keepdims=True)

    return _pc(
        body, "fgate_fwd", (T // tc,),
        [BS((tc, LANES), lambda i: (i, 0)), BS((1, LANES), lambda i: (0, 0))],
        BS((tc, LANES), lambda i: (i, 0)), SDS((T, LANES), F32),
        scratch=[pltpu.VMEM((1, LANES), F32)])(zf, bias)


def _fgate_bwd(zf, bias, dFq, dFk):
    T = zf.shape[0]
    tc = min(256, T)
    n = T // tc
    slabs = dFq.shape[0]

    def body(z_ref, b_ref, dfq_ref, dfk_ref, dz_ref, db_ref, carry):
        @pl.when(pl.program_id(0) == 0)
        def _():
            carry[...] = jnp.zeros_like(carry)
            db_ref[...] = jnp.zeros_like(db_ref)

        df = dfk_ref[...]
        for sl in range(slabs):
            df = df + dfq_ref[sl]
        row = lax.broadcasted_iota(jnp.int32, (tc, tc), 0)
        col = lax.broadcasted_iota(jnp.int32, (tc, tc), 1)
        tri = (col >= row).astype(F32)
        dlogf = jnp.dot(tri, df, precision=lax.Precision.HIGHEST, preferred_element_type=F32) + carry[...]
        carry[...] += jnp.sum(df, axis=0, keepdims=True)
        lane = lax.broadcasted_iota(jnp.int32, (1, LANES), 1)
        dz = jnp.where(lane < HEADS, dlogf * _sigmoid(-(z_ref[...] + b_ref[...])), 0.0)
        dz_ref[...] = dz
        db_ref[...] += jnp.sum(dz, axis=0, keepdims=True)

    rev = lambda i: (n - 1 - i, 0)
    return _pc(
        body, "fgate_bwd", (n,),
        [BS((tc, LANES), rev), BS((1, LANES), lambda i: (0, 0)), BS((slabs, tc, LANES), lambda i: (0, n - 1 - i, 0)),
         BS((tc, LANES), rev)],
        [BS((tc, LANES), rev), BS((1, LANES), lambda i: (0, 0))],
        [SDS((T, LANES), F32), SDS((1, LANES), F32)],
        scratch=[pltpu.VMEM((1, LANES), F32)])(zf, bias, dFq, dFk)


LOG2E = 1.4426950408889634


def _split3(x):
    hi = x.astype(MM)
    r1 = x - hi.astype(F32)
    mid = r1.astype(MM)
    return hi, mid, (r1 - mid.astype(F32)).astype(MM)


def _place(lane, base, cols):
    out = jnp.zeros((cols[0].shape[0], LANES), MM)
    for i, c in enumerate(cols):
        out = jnp.where(lane == base + i, c, out)
    return out


def _head_col(block, lane, h):
    return jnp.sum(jnp.where(lane == h, block, 0.0), axis=-1, keepdims=True)


def _own_lanes(lane, hh):
    return (lane < HEAD_DIM) if hh == 0 else (lane >= HEAD_DIM)


def _attn_k_side(k_ref, f_ref, kb_ref, hp, T, rows, lse_ones, v_ref=None, vb_ref=None):
    lane = lax.broadcasted_iota(jnp.int32, (1, LANES), 1)
    one = jnp.ones((rows, 1), MM)

    def chunk(c, _):
        r0 = pl.multiple_of(c * rows, rows)
        kp = k_ref[pl.ds(r0, rows), :]
        fblk = f_ref[pl.ds(r0, rows), :]
        for hh in range(2):
            hi, mid, lo = _split3(-_head_col(fblk, lane, 2 * hp + hh) * LOG2E)
            cols = [one, one, one, hi, mid, lo] + ([one, one, one] if lse_ones else [])
            bias = _place(lane, HEAD_DIM * (1 - hh), cols)
            kb_ref[hh, pl.ds(r0, rows), :] = jnp.where(_own_lanes(lane, hh), kp, bias)
            if vb_ref is not None:
                vb_ref[hh, pl.ds(r0, rows), :] = jnp.where(_own_lanes(lane, hh), v_ref[pl.ds(r0, rows), :],
                                                           jnp.ones((rows, LANES), MM))
        return 0

    lax.fori_loop(0, T // rows, chunk, 0)


def _attn_q_side(qp, fblk, lane, hp, scale, lse_blk=None):
    qc = qp.astype(F32) * (scale * LOG2E)
    qhi = qc.astype(MM)
    qlo = (qc - qhi.astype(F32)).astype(MM)
    one = jnp.ones((qp.shape[0], 1), MM)
    out = []
    for hh in range(2):
        cols = list(_split3(_head_col(fblk, lane, 2 * hp + hh) * LOG2E)) + [one, one, one]
        if lse_blk is not None:
            cols += list(_split3(-_head_col(lse_blk, lane, 2 * hp + hh)))
        bias = _place(lane, HEAD_DIM * (1 - hh), cols)
        own = _own_lanes(lane, hh)
        out.append(jnp.concatenate([jnp.where(own, qhi, jnp.zeros_like(qhi)), jnp.where(own, qlo, bias)], axis=1))
    return out


def _causal(tq, tk):
    return lax.broadcasted_iota(jnp.int32, (tq, tk), 1) <= lax.broadcasted_iota(jnp.int32, (tq, tk), 0)


def _hosted_specs(hosted):
    if hosted is None:
        return [], [], [], []
    kind, arrays = hosted
    n = len(arrays)
    return [ANY] * n, [ANY] * n, [kind.out_shape(a) for a in arrays], kind.scratch(n)


def _hosted_edges(hosted, refs, n_in, n_out, first, last):
    if hosted is None:
        return refs, lambda: None
    kind, arrays = hosted
    n = len(arrays)
    nsem = len(kind.scratch(n))
    o0 = n_in + n + n_out
    ins, outs, sems = refs[n_in:n_in + n], refs[o0:o0 + n], refs[len(refs) - nsem:]

    @pl.when(first)
    def _():
        kind.start(ins, outs, *sems)

    def finish():
        @pl.when(last)
        def _():
            kind.wait(ins, outs, *sems)

    return refs[:n_in] + refs[n_in + n:o0] + refs[o0 + n:len(refs) - nsem], finish


def _attn_fwd(q, k, v, F, hosted=None):
    T = q.shape[0]
    tq = min(512, T)
    tk = tq
    nq = T // tq
    scale = 1.0 / math.sqrt(HEAD_DIM)
    h_in, h_out, h_shape, h_scratch = _hosted_specs(hosted)

    def body(*refs):
        hp, ib = pl.program_id(0), pl.program_id(1)
        refs, finish = _hosted_edges(hosted, refs, 5, 2, (hp == 0) & (ib == 0), (hp == HEADS // 2 - 1) & (ib == nq - 1))
        q_ref, k_ref, v_ref, fq_ref, f_ref, o_ref, lse_ref, kb_ref, vb_ref = refs
        lane = lax.broadcasted_iota(jnp.int32, (1, LANES), 1)

        @pl.when(ib == 0)
        def _():
            _attn_k_side(k_ref, f_ref, kb_ref, hp, T, tk, False, v_ref, vb_ref)

        qa = _attn_q_side(q_ref[...], fq_ref[...], lane, hp, scale)

        def tile(jb, carry, masked):
            off = pl.multiple_of(jb * tk, tk)
            kp = k_ref[pl.ds(off, tk), :]
            new = []
            for hh in range(2):
                m, acc = carry[hh]
                s = _dot_nt(qa[hh], jnp.concatenate([kp, kb_ref[hh, pl.ds(off, tk), :]], axis=1))
                if masked:
                    s = jnp.where(_causal(tq, tk), s, -jnp.inf)
                m2 = jnp.maximum(m, jnp.max(s, axis=-1, keepdims=True))
                p = jnp.exp2(s - m2)
                new.append((m2, acc * jnp.exp2(m - m2) + _dot(p.astype(MM), vb_ref[hh, pl.ds(off, tk), :])))
            return tuple(new)

        init = tuple((jnp.full((tq, 1), -jnp.inf, F32), jnp.zeros((tq, LANES), F32)) for _ in range(2))
        carry = lax.fori_loop(0, ib, lambda jb, c: tile(jb, c, False), init)
        (m0, a0), (m1, a1) = tile(ib, carry, True)
        l0, l1 = a0[:, HEAD_DIM:HEAD_DIM + 1], a1[:, 0:1]
        o_ref[...] = jnp.where(lane < HEAD_DIM, a0 / l0, a1 / l1)
        lse_ref[...] = jnp.where(lane == 2 * hp, m0 + jnp.log2(l0), jnp.where(lane == 2 * hp + 1, m1 + jnp.log2(l1), 0.0))
        finish()

    blk = lambda h, i: (i, h)
    full = lambda h, i: (0, h)
    return _pc(
        body, "attn_fwd" + ("_hosting" if hosted else ""), (HEADS // 2, nq),
        [BS((tq, LANES), blk), BS((T, LANES), full), BS((T, LANES), full), BS((tq, LANES), lambda h, i: (i, 0)),
         BS((T, LANES), lambda h, i: (0, 0))] + h_in,
        [BS((tq, LANES), blk), BS((None, tq, LANES), lambda h, i: (h, i, 0))] + h_out,
        [SDS((T, HEADS * HEAD_DIM), F32), SDS((HEADS // 2, T, LANES), F32)] + h_shape,
        scratch=[pltpu.VMEM((2, T, LANES), MM)] * 2 + h_scratch)(q, k, v, F, F, *(hosted[1] if hosted else []))


def _attn_bwd(q, k, v, F, o, lse, do, hosted=None):
    T = q.shape[0]
    tq = min(512, T)
    tk = tq
    nq = T // tq
    scale = 1.0 / math.sqrt(HEAD_DIM)
    h_in, h_out, h_shape, h_scratch = _hosted_specs(hosted)

    def body(*refs):
        hp, ib = pl.program_id(0), pl.program_id(1)
        refs, finish = _hosted_edges(hosted, refs, 8, 5, (hp == 0) & (ib == 0), (hp == HEADS // 2 - 1) & (ib == nq - 1))
        (q_ref, k_ref, v_ref, fq_ref, f_ref, o_ref, lse_ref, do_ref,
         dq_ref, dk_ref, dv_ref, dfq_ref, dfk_ref, kb_ref) = refs
        lane = lax.broadcasted_iota(jnp.int32, (1, LANES), 1)

        @pl.when(ib == 0)
        def _():
            _attn_k_side(k_ref, f_ref, kb_ref, hp, T, tk, True)
            dk_ref[...] = jnp.zeros_like(dk_ref)
            dv_ref[...] = jnp.zeros_like(dv_ref)
            dfk_ref[...] = jnp.zeros_like(dfk_ref)

        qp = q_ref[...]
        qa = _attn_q_side(qp, fq_ref[...], lane, hp, scale, lse_ref[...])
        dob = do_ref[...].astype(MM)
        dprod = dob.astype(F32) * o_ref[...]
        qs = (qp.astype(F32) * scale).astype(MM)
        heads = []
        for hh in range(2):
            own = _own_lanes(lane, hh)
            heads.append((jnp.where(own, dob, jnp.zeros_like(dob)), jnp.where(own, qs, jnp.zeros_like(qs)),
                          jnp.sum(jnp.where(own, dprod, 0.0), axis=-1, keepdims=True)))

        def tile(jb, carry, masked):
            off = pl.multiple_of(jb * tk, tk)
            kp = k_ref[pl.ds(off, tk), :]
            vp = v_ref[pl.ds(off, tk), :]
            new = []
            dv_t = jnp.zeros((tk, LANES), F32)
            dk_t = jnp.zeros((tk, LANES), F32)
            for hh in range(2):
                dq, rs = carry[hh]
                dom, qm, delta = heads[hh]
                p = jnp.exp2(_dot_nt(qa[hh], jnp.concatenate([kp, kb_ref[hh, pl.ds(off, tk), :]], axis=1)))
                if masked:
                    p = jnp.where(_causal(tq, tk), p, 0.0)
                ds = p * (_dot_nt(dom, vp) - delta)
                dsb = ds.astype(MM)
                dv_t = dv_t + _dot_tn(p.astype(MM), dom)
                dk_t = dk_t + _dot_tn(dsb, qm)
                dfk_ref[jb, pl.ds(hh, 1), :] -= jnp.sum(ds, axis=0, keepdims=True)
                new.append((dq + _dot(dsb, kp), rs + jnp.sum(ds, axis=-1, keepdims=True)))
            dv_ref[pl.ds(off, tk), :] += dv_t
            dk_ref[pl.ds(off, tk), :] += dk_t
            return tuple(new)

        init = tuple((jnp.zeros((tq, LANES), F32), jnp.zeros((tq, 1), F32)) for _ in range(2))
        carry = lax.fori_loop(0, ib, lambda jb, c: tile(jb, c, False), init)
        (dq0, rs0), (dq1, rs1) = tile(ib, carry, True)
        dq_ref[...] = jnp.where(lane < HEAD_DIM, dq0, dq1) * scale
        dfq_ref[...] = jnp.where(lane == 2 * hp, rs0, jnp.where(lane == 2 * hp + 1, rs1, 0.0))
        finish()

    blk = lambda h, i: (i, h)
    full = lambda h, i: (0, h)
    slab = BS((None, tq, LANES), lambda h, i: (h, i, 0))
    return _pc(
        body, "attn_bwd" + ("_hosting" if hosted else ""), (HEADS // 2, nq),
        [BS((tq, LANES), blk), BS((T, LANES), full), BS((T, LANES), full), BS((tq, LANES), lambda h, i: (i, 0)),
         BS((T, LANES), lambda h, i: (0, 0)), BS((tq, LANES), blk), slab, BS((tq, LANES), blk)] + h_in,
        [BS((tq, LANES), blk), BS((T, LANES), full), BS((T, LANES), full), slab,
         BS((None, nq, 2, tk), lambda h, i: (h, 0, 0, 0))] + h_out,
        [SDS((T, HEADS * HEAD_DIM), F32)] * 3 + [SDS((HEADS // 2, T, LANES), F32), SDS((HEADS // 2, nq, 2, tk), F32)]
        + h_shape,
        scratch=[pltpu.VMEM((2, T, LANES), MM)] + h_scratch,
    )(q, k, v, F, F, o, lse, do, *(hosted[1] if hosted else []))


POOL_HALO = 16
CONV_HALO = 32


def _group_select(lane, v0, v1, v2, v3):
    return jnp.where(lane < 64, v0, jnp.where(lane < 128, v1, jnp.where(lane < 192, v2, v3)))


def _roll_down(x, k):
    return x if k == 0 else pltpu.roll(x, k, 0)


def _roll_up(x, k):
    return x if k == 0 else pltpu.roll(x, x.shape[0] - k, 0)


def _pool_terms(u, u_prev, tile, tm):
    ext = jnp.concatenate([u_prev, u], axis=0)
    s2 = ext + _roll_down(ext, 1)
    s4 = s2 + _roll_down(s2, 2)
    s8 = s4 + _roll_down(s4, 4)
    s16 = s8 + _roll_down(s8, 8)
    lane = lax.broadcasted_iota(jnp.int32, (1, 256), 1)
    ws = _group_select(lane, s2, s4, s8, s16)[POOL_HALO:, :]
    wlen = _group_select(lane, 2.0, 4.0, 8.0, 16.0).astype(F32)
    return ws / _pool_count(tile, tm, tm, wlen) - u


def _pool_count(tile, tm, rows, wlen):
    t = (tile * tm + 1 + lax.broadcasted_iota(jnp.int32, (rows, 1), 0)).astype(F32)
    return jnp.minimum(t, wlen)


def _layer_norm(y, lg, lb):
    mu = jnp.mean(y, axis=-1, keepdims=True)
    yc = y - mu
    rstd = lax.rsqrt(jnp.mean(yc * yc, axis=-1, keepdims=True) + NORM_EPS)
    yh = yc * rstd
    return yh, rstd, yh * lg + lb


def _halo_specs(tm, T, halo, prev):
    per = tm // halo
    if prev:
        return BS((halo, 256), lambda i: (jnp.maximum(i * per - 1, 0), 0))
    return BS((halo, 256), lambda i: (jnp.minimum((i + 1) * per, T // halo - 1), 0))


def _local_fwd(up, ca, cg, bd, pscale, cw, cb, lg, lb):
    T = up.shape[0]
    tm = min(512, T)

    def body(up_ref, uph_ref, ca_ref, cah_ref, cg_ref, cgh_ref, bd_ref, ps_ref, cw_ref, cb_ref, lg_ref, lb_ref,
             ya_ref, yc_ref, u_ref, y_ref):
        i = pl.program_id(0)
        first = i == 0
        pooled = _pool_terms(up_ref[...], jnp.where(first, 0.0, uph_ref[...]), i, tm)
        ya_ref[...] = (_dot(pooled.astype(MM), bd_ref[...]) * ps_ref[...]).astype(ya_ref.dtype)

        u = ca_ref[...] * _sigmoid(cg_ref[...])
        uh = jnp.where(first, 0.0, cah_ref[...] * _sigmoid(cgh_ref[...]))
        ext = jnp.concatenate([uh, u], axis=0)
        y = jnp.zeros((tm, 256), F32) + cb_ref[...]
        for kk in range(CONV_K):
            y = y + cw_ref[kk:kk + 1, :] * _roll_up(ext, CONV_HALO - (CONV_K - 1) + kk)[:tm, :]
        _, _, z = _layer_norm(y, lg_ref[...], lb_ref[...])
        yc_ref[...] = (z * _sigmoid(z)).astype(yc_ref.dtype)
        u_ref[...] = u
        y_ref[...] = y

    tok = lambda i: (i, 0)
    par = lambda i: (0, 0)
    t256 = BS((tm, 256), tok)
    return _pc(
        body, "local_fwd", (T // tm,),
        [t256, _halo_specs(tm, T, POOL_HALO, True), t256, _halo_specs(tm, T, CONV_HALO, True),
         t256, _halo_specs(tm, T, CONV_HALO, True),
         BS((256, 256), par), BS((1, 256), par), BS((32, 256), par), BS((1, 256), par), BS((1, 256), par),
         BS((1, 256), par)],
        [t256, t256, t256, t256],
        [SDS((T, 256), MM), SDS((T, 256), MM), SDS((T, 256), F32), SDS((T, 256), F32)],
    )(up, up, ca, ca, cg, cg, bd, pscale, cw, cb, lg, lb)


def _local_bwd(up, dya, ca, cg, u, y, dyc, bd, pscale, cw, lg, lb):
    T = up.shape[0]
    tm = min(512, T)
    n = T // tm

    def body(up_ref, uph_ref, dya_ref, dyan_ref, ca_ref, cg_ref, u_ref, uh_ref, y_ref, yn_ref, dyc_ref, dycn_ref,
             bd_ref, ps_ref, cw_ref, lg_ref, lb_ref,
             dup_ref, dca_ref, dcg_ref, dbd_ref, dps_ref, dcw_ref, dcb_ref, dlg_ref, dlb_ref):
        i = pl.program_id(0)
        first = i == 0
        last = i == n - 1

        @pl.when(first)
        def _():
            for ref in (dbd_ref, dps_ref, dcw_ref, dcb_ref, dlg_ref, dlb_ref):
                ref[...] = jnp.zeros_like(ref)

        ps = ps_ref[...]
        pooled = _pool_terms(up_ref[...], jnp.where(first, 0.0, uph_ref[...]), i, tm).astype(MM)
        dya_t = dya_ref[...]
        dps_ref[...] += jnp.sum(dya_t * _dot(pooled, bd_ref[...]), axis=0, keepdims=True)
        dm = (jnp.concatenate([dya_t, jnp.where(last, 0.0, dyan_ref[...])], axis=0) * ps).astype(MM)
        dbd_ref[...] += _dot_tn(pooled, dm[:tm, :])
        dpool = _dot_nt(dm, bd_ref[...])
        lane = lax.broadcasted_iota(jnp.int32, (1, 256), 1)
        wlen = _group_select(lane, 2.0, 4.0, 8.0, 16.0).astype(F32)
        e = dpool / _pool_count(i, tm, tm + POOL_HALO, wlen)
        f2 = e + _roll_up(e, 1)
        f4 = f2 + _roll_up(f2, 2)
        f8 = f4 + _roll_up(f4, 4)
        f16 = f8 + _roll_up(f8, 8)
        dup_ref[...] = _group_select(lane, f2, f4, f8, f16)[:tm, :] - dpool[:tm, :]

        lgv = lg_ref[...]
        yext = jnp.concatenate([y_ref[...], yn_ref[...]], axis=0)
        dyc = jnp.concatenate([dyc_ref[...], jnp.where(last, 0.0, dycn_ref[...])], axis=0)
        yh, rstd, z = _layer_norm(yext, lgv, lb_ref[...])
        sig = _sigmoid(z)
        dz = dyc * (sig * (1.0 + z * (1.0 - sig)))
        dlg_ref[...] += jnp.sum((dz * yh)[:tm, :], axis=0, keepdims=True)
        dlb_ref[...] += jnp.sum(dz[:tm, :], axis=0, keepdims=True)
        dyh = dz * lgv
        dy = rstd * (dyh - jnp.mean(dyh, axis=-1, keepdims=True) - yh * jnp.mean(dyh * yh, axis=-1, keepdims=True))
        dy_t = dy[:tm, :]
        dcb_ref[...] += jnp.sum(dy_t, axis=0, keepdims=True)
        uext = jnp.concatenate([jnp.where(first, 0.0, uh_ref[...]), u_ref[...]], axis=0)
        du = jnp.zeros((tm, 256), F32)
        for kk in range(CONV_K):
            shifted = _roll_up(uext, CONV_HALO - (CONV_K - 1) + kk)[:tm, :]
            dcw_ref[kk:kk + 1, :] += jnp.sum(dy_t * shifted, axis=0, keepdims=True)
            du = du + cw_ref[kk:kk + 1, :] * _roll_up(dy, CONV_K - 1 - kk)[:tm, :]
        sg = _sigmoid(cg_ref[...])
        dca_ref[...] = du * sg
        dcg_ref[...] = du * ca_ref[...] * sg * (1.0 - sg)

    tok = lambda i: (i, 0)
    par = lambda i: (0, 0)
    t256 = BS((tm, 256), tok)
    p1 = BS((1, 256), par)
    return _pc(
        body, "local_bwd", (n,),
        [t256, _halo_specs(tm, T, POOL_HALO, True), t256, _halo_specs(tm, T, POOL_HALO, False), t256, t256,
         t256, _halo_specs(tm, T, CONV_HALO, True), t256, _halo_specs(tm, T, CONV_HALO, False),
         t256, _halo_specs(tm, T, CONV_HALO, False),
         BS((256, 256), par), p1, BS((32, 256), par), p1, p1],
        [t256, t256, t256, BS((256, 256), par), p1, BS((32, 256), par), p1, p1, p1],
        [SDS((T, 256), F32)] * 3 + [SDS((256, 256), F32), SDS((1, 256), F32), SDS((32, 256), F32)]
        + [SDS((1, 256), F32)] * 3,
    )(up, up, dya, dya, ca, cg, u, u, y, y, dyc, dyc, bd, pscale, cw, lg, lb)


def _head(x, g, target):
    T, D = x.shape
    tm = min(512, T)

    def body(x_ref, g_ref, t_ref, loss_ref, dx_ref, dg_ref):
        @pl.when(pl.program_id(0) == 0)
        def _():
            loss_ref[...] = jnp.zeros_like(loss_ref)
            dg_ref[...] = jnp.zeros_like(dg_ref)

        gv = g_ref[...]
        xh, r, yv = _rms_fwd(x_ref[...], gv)
        err = yv - t_ref[...]
        loss_ref[...] += 0.5 * jnp.sum(jnp.mean(err * err, axis=-1, keepdims=True), axis=0, keepdims=True)
        dx, dg = _rms_bwd(err * (1.0 / D), xh, r, gv)
        dx_ref[...] = dx
        dg_ref[...] += dg

    tok = lambda i: (i, 0)
    par = lambda i: (0, 0)
    return _pc(
        body, "head", (T // tm,),
        [BS((tm, D), tok), BS((1, D), par), BS((tm, D), tok)],
        [BS((1, LANES), par), BS((tm, D), tok), BS((1, D), par)],
        [SDS((1, LANES), F32), SDS((T, D), F32), SDS((1, D), F32)])(x, g, target)


def _adamw(w, gs, m, v, name):
    R, C = w.shape
    tr = R
    for cand in (512, 256, 128, 64, 32, 16, 8):
        if R % cand == 0:
            tr = cand
            break
    ng = len(gs)

    def body(*refs):
        w_ref, g_refs, m_ref, v_ref = refs[0], refs[1:1 + ng], refs[1 + ng], refs[2 + ng]
        g_ref, d_ref, m2_ref, v2_ref = refs[3 + ng:]
        g = g_refs[0][...]
        for r in g_refs[1:]:
            g = g + r[...]
        m2 = ADAM_B1 * m_ref[...] + (1.0 - ADAM_B1) * g
        v2 = ADAM_B2 * v_ref[...] + (1.0 - ADAM_B2) * jnp.square(g)
        m_hat = m2 / (1.0 - ADAM_B1 ** ADAM_STEP)
        v_hat = v2 / (1.0 - ADAM_B2 ** ADAM_STEP)
        g_ref[...] = g
        d_ref[...] = -ADAM_LR * (m_hat / (jnp.sqrt(v_hat) + ADAM_EPS) + ADAM_WD * w_ref[...])
        m2_ref[...] = m2
        v2_ref[...] = v2

    blk = BS((tr, C), lambda i: (i, 0))
    return _pc(body, name, (R // tr,), [blk] * (3 + ng), [blk] * 4, [SDS((R, C), F32)] * 4)(w, *gs, m, v)


def _sum_parts(owns, recvs, name):
    L = len(owns)
    R, C = owns[0].shape
    tr = next(t for t in (512, 256, 128, 64, 32, 16) if R % t == 0)

    def body(*refs):
        l = pl.program_id(0)
        s_ref = refs[2 * L]
        for ll in range(L):
            @pl.when(l == ll)
            def _(o_ref=refs[ll], r_ref=refs[L + ll]):
                s_ref[...] = ((o_ref[...] + r_ref[0].astype(F32)) + r_ref[1].astype(F32)) + r_ref[2].astype(F32)

    own_specs = [BS((tr, C), lambda l, i, ll=ll: (jnp.where(l == ll, i, 0), 0)) for ll in range(L)]
    recv_specs = [BS((3, tr, C), lambda l, i, ll=ll: (0, jnp.where(l == ll, i, 0), 0)) for ll in range(L)]
    return _pc(body, name, (L, R // tr), own_specs + recv_specs,
               BS((None, tr, C), lambda l, i: (l, i, 0)), SDS((L, R, C), F32))(*owns, *recvs)


def _sum8(parts, name):
    _, R, C = parts.shape

    def body(p_ref, s_ref):
        acc = p_ref[0]
        for d in range(1, 8):
            acc = acc + p_ref[d]
        s_ref[...] = acc

    return _pc(body, name, (1,), [BS((8, R, C), lambda i: (0, 0, 0))], BS((R, C), lambda i: (0, 0)),
               SDS((R, C), F32))(parts)


def _position():
    return lax.axis_index("x"), lax.axis_index("y"), lax.axis_index("c")


CHIP_FLIPS = ((1, 0), (0, 1), (1, 1))


class _GatherChips:
    @staticmethod
    def scratch(n):
        return [pltpu.SemaphoreType.DMA((3 * n,)), pltpu.SemaphoreType.DMA((3 * n,)), pltpu.SemaphoreType.DMA((n,))]

    @staticmethod
    def out_shape(block):
        return SDS((4,) + tuple(block.shape), block.dtype)

    @staticmethod
    def _copies(ins, outs, send_sems, recv_sems, local_sems, arrivals):
        x, y, c = _position()
        local, remote = [], []
        for i, (in_ref, out_ref) in enumerate(zip(ins, outs)):
            local.append(pltpu.make_async_copy(in_ref, out_ref.at[2 * x + y], local_sems.at[i]))
            for k, (fx, fy) in enumerate(CHIP_FLIPS):
                slot = 2 * (x ^ fx) + (y ^ fy) if arrivals else 2 * x + y
                remote.append(pltpu.make_async_remote_copy(
                    src_ref=in_ref, dst_ref=out_ref.at[slot], send_sem=send_sems.at[3 * i + k],
                    recv_sem=recv_sems.at[3 * i + k], device_id=(x ^ fx, y ^ fy, c), device_id_type=MESH))
        return local, remote

    @classmethod
    def start(cls, ins, outs, *sems):
        local, sends = cls._copies(ins, outs, *sems, arrivals=False)
        for cp in local + sends:
            cp.start()

    @classmethod
    def wait(cls, ins, outs, *sems):
        local, arrivals = cls._copies(ins, outs, *sems, arrivals=True)
        for cp in arrivals:
            cp.wait_recv()
        for cp in arrivals:
            cp.wait_send()
        for cp in local:
            cp.wait()


class _Symmetric:
    @classmethod
    def start(cls, ins, outs, *sems):
        for cp in cls._copies(ins, outs, *sems):
            cp.start()

    @classmethod
    def wait(cls, ins, outs, *sems):
        copies = cls._copies(ins, outs, *sems)
        for cp in copies:
            cp.wait_recv()
        for cp in copies:
            cp.wait_send()


class _ScatterChips(_Symmetric):
    @staticmethod
    def scratch(n):
        return [pltpu.SemaphoreType.DMA((3 * n,)), pltpu.SemaphoreType.DMA((3 * n,))]

    @staticmethod
    def out_shape(parts):
        return SDS((3,) + tuple(parts.shape[1:]), parts.dtype)

    @staticmethod
    def _copies(ins, outs, send_sems, recv_sems):
        x, y, c = _position()
        return [
            pltpu.make_async_remote_copy(
                src_ref=in_ref.at[2 * (x ^ fx) + (y ^ fy)], dst_ref=out_ref.at[k],
                send_sem=send_sems.at[3 * i + k], recv_sem=recv_sems.at[3 * i + k],
                device_id=(x ^ fx, y ^ fy, c), device_id_type=MESH)
            for i, (in_ref, out_ref) in enumerate(zip(ins, outs)) for k, (fx, fy) in enumerate(CHIP_FLIPS)]


class _SwapCores(_Symmetric):
    @staticmethod
    def scratch(n):
        return [pltpu.SemaphoreType.DMA((n,)), pltpu.SemaphoreType.DMA((n,))]

    @staticmethod
    def out_shape(block):
        return SDS(block.shape, block.dtype)

    @staticmethod
    def _copies(ins, outs, send_sems, recv_sems):
        x, y, c = _position()
        return [
            pltpu.make_async_remote_copy(
                src_ref=in_ref, dst_ref=out_ref, send_sem=send_sems.at[i], recv_sem=recv_sems.at[i],
                device_id=(x, y, 1 - c), device_id_type=MESH)
            for i, (in_ref, out_ref) in enumerate(zip(ins, outs))]


def _exchange(kind, arrays, name):
    n = len(arrays)

    def body(*refs):
        ins, outs, sems = refs[:n], refs[n:2 * n], refs[2 * n:]
        kind.start(ins, outs, *sems)
        kind.wait(ins, outs, *sems)

    return pl.pallas_call(body, out_shape=[kind.out_shape(a) for a in arrays], in_specs=[ANY] * n,
                          out_specs=[ANY] * n, name=name, scratch_shapes=kind.scratch(n))(*arrays)


def _gather_all(block, name):
    R, C = block.shape
    flips = [(fx, fy, fc) for fx in (0, 1) for fy in (0, 1) for fc in (0, 1)][1:]

    def body(in_ref, out_ref, send_sems, recv_sems, local_sem):
        x, y, c = _position()
        mine = out_ref.at[4 * x + 2 * y + c]
        local = pltpu.make_async_copy(in_ref, mine, local_sem)
        local.start()
        copies = [
            pltpu.make_async_remote_copy(
                src_ref=in_ref, dst_ref=mine, send_sem=send_sems.at[k], recv_sem=recv_sems.at[k],
                device_id=(x ^ fx, y ^ fy, c ^ fc), device_id_type=MESH)
            for k, (fx, fy, fc) in enumerate(flips)]
        for cp in copies:
            cp.start()
        for k, (fx, fy, fc) in enumerate(flips):
            theirs = out_ref.at[4 * (x ^ fx) + 2 * (y ^ fy) + (c ^ fc)]
            pltpu.make_async_remote_copy(
                src_ref=in_ref, dst_ref=theirs, send_sem=send_sems.at[k], recv_sem=recv_sems.at[k],
                device_id=(x ^ fx, y ^ fy, c ^ fc), device_id_type=MESH).wait_recv()
        for cp in copies:
            cp.wait_send()
        local.wait()

    return pl.pallas_call(
        body, out_shape=SDS((8, R, C), block.dtype), in_specs=[ANY], out_specs=ANY, name=name,
        scratch_shapes=[pltpu.SemaphoreType.DMA((7,)), pltpu.SemaphoreType.DMA((7,)), pltpu.SemaphoreType.DMA(())])(block)


BIG = ("ffn1_w_gate", "ffn1_w_up", "ffn1_w_down", "w_in", "w_out", "ffn2_w_gate", "ffn2_w_up", "ffn2_w_down")
COL_SHARDED = ("ffn1_w_gate", "ffn1_w_up", "w_in", "ffn2_w_gate", "ffn2_w_up")
FIRST = tuple((n, 0) for n in ("ffn1_w_gate", "ffn1_w_up", "ffn1_w_down"))
LATE = tuple((n, 0) for n in ("w_out", "ffn2_w_gate", "ffn2_w_up", "ffn2_w_down")) + tuple((n, 1) for n in BIG)


def _to_shards(name, full):
    r, c = full.shape
    if name in COL_SHARDED:
        return full.reshape(r, 4, c // 4).transpose(1, 0, 2)
    return full.reshape(4, r // 4, c)


def _own_shard(name, full, chip):
    r, c = full.shape
    if name in COL_SHARDED:
        return lax.dynamic_slice_in_dim(full, chip * (c // 4), c // 4, axis=1)
    return lax.dynamic_slice_in_dim(full, chip * (r // 4), r // 4, axis=0)


def _from_shards(name, sh):
    _, r, c = sh.shape
    if name in COL_SHARDED:
        return sh.transpose(1, 0, 2).reshape(r, 4 * c)
    return sh.reshape(4 * r, c)


def _pad_w_in(w):
    return jnp.concatenate([w[:, :1792], w[:, 1800:2312], w[:, 1792:1800], jnp.zeros((w.shape[0], 248), w.dtype)], axis=1)


def _unpad_w_in(g):
    return jnp.concatenate([g[:, :1792], g[:, 2304:2312], g[:, 1792:2304]], axis=1)


def _block_diag(pw):
    out = jnp.zeros((256, 256), pw.dtype)
    for gidx in range(4):
        out = lax.dynamic_update_slice(out, pw[gidx], (64 * gidx, 64 * gidx))
    return out


SMALL = ("ffn1_norm", "mix_norm", "pool_w", "pool_scale", "forget_bias", "conv_b", "conv_ln_g", "conv_ln_b",
         "ffn2_norm", "final_norm")


def _pack_small(arrs):
    rows = []
    for a in arrs:
        flat = a.reshape(-1)
        flat = jnp.pad(flat, (0, -flat.shape[0] % LANES))
        rows.append(flat.reshape(-1, LANES))
    total = sum(r.shape[0] for r in rows)
    if total % 8:
        rows.append(jnp.zeros((-total % 8, LANES), F32))
    return jnp.concatenate(rows, axis=0)


def _unpack_small(buf, shapes):
    out, off = [], 0
    for shp in shapes:
        n = math.prod(shp)
        nr = -(-n // LANES)
        out.append(buf[off:off + nr].reshape(-1)[:n].reshape(shp))
        off += nr
    return out


def _grad_parts(grads, pieces):
    return [_to_shards(n, grads[n][l]).astype(MM) for n, l in pieces]


def _forward_backward(x, target, W, shards=None):
    T = x.shape[0]
    L = W["ffn1_norm"].shape[0]
    saved = []
    recv = {}
    for l in range(L):
        g1, gm, g2 = (W[n][l][None, :] for n in ("ffn1_norm", "mix_norm", "ffn2_norm"))
        first = shards is not None and l == 0
        hosted = (_GatherChips, [shards["w_in"][0], shards["conv_w"]]) if first else None
        x1, *got = _ffn_fwd(x, g1, W["ffn1_w_gate"][l], W["ffn1_w_up"][l], W["ffn1_w_down"][l], hosted=hosted)
        if first:
            W["w_in"][0] = _from_shards("w_in", got[0])
            W["conv_w"] = got[1].transpose(1, 2, 0, 3).reshape(L, CONV_K, 256)
        w_in = _pad_w_in(W["w_in"][l])
        up, q, k, v, ca, cg, zf = _mix_in_fwd(x1, gm, w_in)
        fb = jnp.pad(W["forget_bias"][l], (0, LANES - HEADS))[None, :]
        F = _fgate_fwd(zf, fb)
        if first:
            yb, lse, *got = _attn_fwd(q, k, v, F, hosted=(_GatherChips, [shards[n][ll] for n, ll in LATE]))
            for (n, ll), sh in zip(LATE, got):
                W[n][ll] = _from_shards(n, sh)
        else:
            yb, lse = _attn_fwd(q, k, v, F)
        bd = _block_diag(W["pool_w"][l]).astype(MM)
        ps, cb, lg, lb = (W[n][l][None, :] for n in ("pool_scale", "conv_b", "conv_ln_g", "conv_ln_b"))
        cw = jnp.pad(W["conv_w"][l], ((0, 1), (0, 0)))
        ya, yc, cu, cy = _local_fwd(up, ca, cg, bd, ps, cw, cb, lg, lb)
        x2 = _mix_out_fwd(x1, ya, yb, yc, W["w_out"][l])
        x3 = _ffn_fwd(x2, g2, W["ffn2_w_gate"][l], W["ffn2_w_up"][l], W["ffn2_w_down"][l])[0]
        saved.append(dict(x0=x, x1=x1, x2=x2, w_in=w_in, up=up, ca=ca, cg=cg, zf=zf, fb=fb, F=F,
                          q=q, k=k, v=v, lse=lse, bd=bd, cw=cw, cu=cu, cy=cy, ya=ya, yb=yb, yc=yc))
        x = x3

    loss, dx, dgf = _head(x, W["final_norm"][None, :], target)
    grads = {n: [None] * L for n in W if n != "final_norm"}
    grads["final_norm"] = dgf[0]
    for l in reversed(range(L)):
        s = saved[l]
        g1, gm, g2 = (W[n][l][None, :] for n in ("ffn1_norm", "mix_norm", "ffn2_norm"))
        ps, lg, lb = (W[n][l][None, :] for n in ("pool_scale", "conv_ln_g", "conv_ln_b"))
        dx, h, dy, da, db, sact, dg = _ffn_bwd(s["x2"], dx, g2, W["ffn2_w_gate"][l], W["ffn2_w_up"][l], W["ffn2_w_down"][l])
        grads["ffn2_norm"][l] = dg[0]
        grads["ffn2_w_gate"][l] = _wgrad(h, da, "wgrad_gate")
        grads["ffn2_w_up"][l] = _wgrad(h, db, "wgrad_up")
        grads["ffn2_w_down"][l] = _wgrad(sact, dy, "wgrad_down")
        dya, dyb, dyc = _mix_out_bwd(dx, W["w_out"][l])
        grads["w_out"][l] = jnp.concatenate(
            [_wgrad(s["ya"], dx, "wgrad_out_a"), _wgrad(s["yb"], dx, "wgrad_out_b"), _wgrad(s["yc"], dx, "wgrad_out_c")], axis=0)
        first = shards is not None and l == 0
        if first:
            dq, dk, dv, dfq, dfk, *got = _attn_bwd(s["q"], s["k"], s["v"], s["F"], s["yb"], s["lse"], dyb,
                                                  hosted=(_ScatterChips, _grad_parts(grads, LATE)))
            recv.update(zip(LATE, got))
        else:
            dq, dk, dv, dfq, dfk = _attn_bwd(s["q"], s["k"], s["v"], s["F"], s["yb"], s["lse"], dyb)
        dfk_cols = jnp.pad(dfk.transpose(0, 2, 1, 3).reshape(HEADS, T).T, ((0, 0), (0, LANES - HEADS)))
        dzf, dfb = _fgate_bwd(s["zf"], s["fb"], dfq, dfk_cols)
        grads["forget_bias"][l] = dfb[0, :HEADS]
        dup, dca, dcg, dbd, dps, dcw, dcb, dlg, dlb = _local_bwd(
            s["up"], dya, s["ca"], s["cg"], s["cu"], s["cy"], dyc, s["bd"], ps, s["cw"], lg, lb)
        grads["pool_w"][l] = jnp.stack([dbd[64 * i:64 * i + 64, 64 * i:64 * i + 64] for i in range(4)])
        grads["pool_scale"][l], grads["conv_b"][l] = dps[0], dcb[0]
        grads["conv_ln_g"][l], grads["conv_ln_b"][l] = dlg[0], dlb[0]
        grads["conv_w"][l] = dcw[:CONV_K]
        dx, h, dp, dg = _mix_in_bwd(s["x1"], dx, gm, s["w_in"], dup, dq, dk, dv, dca, dcg, dzf)
        grads["mix_norm"][l] = dg[0]
        grads["w_in"][l] = _unpad_w_in(_wgrad(h, dp, "wgrad_in"))
        ffn1 = (W["ffn1_w_gate"][l], W["ffn1_w_up"][l], W["ffn1_w_down"][l])
        if not first:
            dx, h, dy, da, db, sact, dg = _ffn_bwd(s["x0"], dx, g1, *ffn1)
            grads["ffn1_w_gate"][l] = _wgrad(h, da, "wgrad_gate")
            grads["ffn1_w_up"][l] = _wgrad(h, db, "wgrad_up")
            grads["ffn1_w_down"][l] = _wgrad(sact, dy, "wgrad_down")
        else:
            scatter = lambda n: (_ScatterChips, _grad_parts(grads, [(n, 0)]))
            dx, h, dy, da, db, sact, dg, recv[("w_in", 0)] = _ffn_bwd(s["x0"], dx, g1, *ffn1, hosted=scatter("w_in"))
            grads["ffn1_w_gate"][0] = _wgrad(h, da, "wgrad_gate")
            grads["ffn1_w_up"][0], recv[("ffn1_w_gate", 0)] = _wgrad(h, db, "wgrad_up", hosted=scatter("ffn1_w_gate"))
            grads["ffn1_w_down"][0], recv[("ffn1_w_up", 0)] = _wgrad(sact, dy, "wgrad_down", hosted=scatter("ffn1_w_up"))
            recv[("ffn1_w_down", 0)] = _exchange(*scatter("ffn1_w_down"), "scatter_last_grad")[0]
        grads["ffn1_norm"][l] = dg[0]
    grads = {n: (jnp.stack(g) if isinstance(g, list) and n not in BIG else g) for n, g in grads.items()}
    return loss, dx, grads, recv


NAMES = ("ffn1_norm", "ffn1_w_gate", "ffn1_w_up", "ffn1_w_down", "mix_norm", "w_in", "pool_w", "pool_scale",
         "forget_bias", "conv_w", "conv_b", "conv_ln_g", "conv_ln_b", "w_out", "ffn2_norm", "ffn2_w_gate",
         "ffn2_w_up", "ffn2_w_down", "final_norm")


def kernel(x, ffn1_norm, ffn1_w_gate, ffn1_w_up, ffn1_w_down, mix_norm, w_in, pool_w, pool_scale, forget_bias, conv_w, conv_b, conv_ln_g, conv_ln_b, w_out, ffn2_norm, ffn2_w_gate, ffn2_w_up, ffn2_w_down, final_norm, loss_target, m_ffn1_norm, m_ffn1_w_gate, m_ffn1_w_up, m_ffn1_w_down, m_mix_norm, m_w_in, m_pool_w, m_pool_scale, m_forget_bias, m_conv_w, m_conv_b, m_conv_ln_g, m_conv_ln_b, m_w_out, m_ffn2_norm, m_ffn2_w_gate, m_ffn2_w_up, m_ffn2_w_down, m_final_norm, v_ffn1_norm, v_ffn1_w_gate, v_ffn1_w_up, v_ffn1_w_down, v_mix_norm, v_w_in, v_pool_w, v_pool_scale, v_forget_bias, v_conv_w, v_conv_b, v_conv_ln_g, v_conv_ln_b, v_w_out, v_ffn2_norm, v_ffn2_w_gate, v_ffn2_w_up, v_ffn2_w_down, v_final_norm):
    args = (ffn1_norm, ffn1_w_gate, ffn1_w_up, ffn1_w_down, mix_norm, w_in, pool_w, pool_scale, forget_bias, conv_w, conv_b, conv_ln_g, conv_ln_b, w_out, ffn2_norm, ffn2_w_gate, ffn2_w_up, ffn2_w_down, final_norm)
    ms = (m_ffn1_norm, m_ffn1_w_gate, m_ffn1_w_up, m_ffn1_w_down, m_mix_norm, m_w_in, m_pool_w, m_pool_scale, m_forget_bias, m_conv_w, m_conv_b, m_conv_ln_g, m_conv_ln_b, m_w_out, m_ffn2_norm, m_ffn2_w_gate, m_ffn2_w_up, m_ffn2_w_down, m_final_norm)
    vs = (v_ffn1_norm, v_ffn1_w_gate, v_ffn1_w_up, v_ffn1_w_down, v_mix_norm, v_w_in, v_pool_w, v_pool_scale, v_forget_bias, v_conv_w, v_conv_b, v_conv_ln_g, v_conv_ln_b, v_w_out, v_ffn2_norm, v_ffn2_w_gate, v_ffn2_w_up, v_ffn2_w_down, v_final_norm)
    P = dict(zip(NAMES, args))
    M = dict(zip(NAMES, ms))
    V = dict(zip(NAMES, vs))
    xi, yi, _ = _position()
    chip = 2 * xi + yi

    W = {n: P[n] for n in SMALL}
    W.update({n: [None] * P[n].shape[0] for n in BIG})
    shards = {n: [P[n][l].astype(MM) for l in range(P[n].shape[0])] for n in BIG}
    shards["conv_w"] = P["conv_w"]
    for (n, l), sh in zip(FIRST, _exchange(_GatherChips, [shards[n][l] for n, l in FIRST], "gather_first_weights")):
        W[n][l] = _from_shards(n, sh)

    loss_part, dx, G, recv = _forward_backward(x[0], loss_target[0], W, shards)
    loss = lax.psum(loss_part[0, 0], ("x", "y", "c"))

    small_shapes = [P[n].shape for n in SMALL] + [G["conv_w"].shape]
    small_parts = _gather_all(_pack_small([G[n] for n in SMALL] + [G["conv_w"]]), "gather_small_grads")
    small_sum = _sum8(small_parts, "sum_small_grads")
    nsmall = sum(-(-math.prod(s) // LANES) for s in small_shapes[:-1])
    nsmall_pad = nsmall + (-nsmall % 8)
    w_s, m_s, v_s = (_pack_small([D[n] for n in SMALL]) for D in (P, M, V))
    outs_small = _adamw(w_s, [small_sum[:nsmall_pad]], m_s, v_s, "adamw_small")
    res = {}
    for kind, buf in zip(("g", "d", "m", "v"), outs_small):
        for n, a in zip(SMALL, _unpack_small(buf, small_shapes[:-1])):
            res[(kind, n)] = a
    g_cw_full = _unpack_small(small_sum[nsmall:], [small_shapes[-1]])[0]
    g_cw = lax.dynamic_slice_in_dim(g_cw_full, chip * 64, 64, axis=2)
    outs_cw = _adamw(_pack_small([P["conv_w"]]), [_pack_small([g_cw])], _pack_small([M["conv_w"]]),
                     _pack_small([V["conv_w"]]), "adamw_conv_w")
    for kind, buf in zip(("g", "d", "m", "v"), outs_cw):
        res[(kind, "conv_w")] = _unpack_small(buf, [P["conv_w"].shape])[0]

    parts =[_sum_parts([_own_shard(n, G[n][l], chip) for l in range(P[n].shape[0])],
                        [recv[(n, l)] for l in range(P[n].shape[0])], "sum_" + n) for n in BIG]
    others = _exchange(_SwapCores, parts, "swap_core_grads")
    for n, ga, gb in zip(BIG, parts, others):
        shp = P[n].shape
        two_d = (shp[0] * shp[1], shp[2])
        outs = _adamw(P[n].reshape(two_d), [ga.reshape(two_d), gb.reshape(two_d)], M[n].reshape(two_d),
                      V[n].reshape(two_d), "adamw_" + n)
        for kind, a in zip(("g", "d", "m", "v"), outs):
            res[(kind, n)] = a.reshape(shp)

    return (loss, dx[None], *[res[("g", n)] for n in NAMES], *[res[("d", n)] for n in NAMES],
            *[res[("m", n)] for n in NAMES], *[res[("v", n)] for n in NAMES])
```

```python
import functools
import math

import jax
import jax.numpy as jnp
from jax import lax
from jax.experimental import pallas as pl
from jax.experimental.pallas import tpu as pltpu

F32 = jnp.float32
MM = jnp.bfloat16
NORM_EPS = 1e-6
HEADS = 8
HEAD_DIM = 64
POOL_WINDOWS = (2, 4, 8, 16)
CONV_K = 31
LANES = 128
VMEM_LIMIT = 56 * 2**20
FFN_BWD_ROWS = 256

ADAM_LR = 0.001
ADAM_B1 = 0.9
ADAM_B2 = 0.999
ADAM_EPS = 1e-08
ADAM_WD = 0.01
ADAM_STEP = 10

MESH = pl.DeviceIdType.MESH
BS = pl.BlockSpec
SDS = jax.ShapeDtypeStruct
ANY = pl.BlockSpec(memory_space=pl.ANY)


def _dot(a, b):
    return jnp.dot(a, b, preferred_element_type=F32)


def _dot_nt(a, b):
    return lax.dot_general(a, b, (((1,), (1,)), ((), ())), preferred_element_type=F32)


def _dot_tn(a, b):
    return lax.dot_general(a, b, (((0,), (0,)), ((), ())), preferred_element_type=F32)


def _pc(body, name, grid, in_specs, out_specs, out_shape, scratch=()):
    return pl.pallas_call(
        body, out_shape=out_shape, grid=grid, in_specs=in_specs, out_specs=out_specs,
        scratch_shapes=list(scratch), name=name,
        compiler_params=pltpu.CompilerParams(
            dimension_semantics=("arbitrary",) * len(grid), vmem_limit_bytes=VMEM_LIMIT))


def _rms_fwd(x, g):
    r = lax.rsqrt(jnp.mean(x * x, axis=-1, keepdims=True) + NORM_EPS)
    xh = x * r
    return xh, r, xh * g


def _rms_bwd(dh, xh, r, g):
    dxh = dh * g
    dx = r * (dxh - xh * jnp.mean(dxh * xh, axis=-1, keepdims=True))
    return dx, jnp.sum(dh * xh, axis=0, keepdims=True)


def _sigmoid(x):
    return jax.nn.sigmoid(x)


def _ffn_fwd(x, g, wg, wu, wd, hosted=None):
    T, D = x.shape
    F = wg.shape[1]
    tm, nf = min(512, T), 2
    fc = F // nf
    nt = T // tm
    h_in, h_out, h_shape, h_scratch = _hosted_specs(hosted)

    def body(*refs):
        i, j = pl.program_id(0), pl.program_id(1)
        refs, finish = _hosted_edges(hosted, refs, 5, 3, (i == 0) & (j == 0), (i == nt - 1) & (j == nf - 1))
        x_ref, g_ref, wg_ref, wu_ref, wd_ref, o_ref, a_ref, b_ref, h_scr, acc_scr = refs

        @pl.when(j == 0)
        def _():
            _, _, hg = _rms_fwd(x_ref[...], g_ref[...])
            h_scr[...] = hg.astype(h_scr.dtype)
            acc_scr[...] = jnp.zeros_like(acc_scr)

        h = h_scr[...]
        a = _dot(h, wg_ref[...])
        b = _dot(h, wu_ref[...])
        a_ref[...] = a.astype(a_ref.dtype)
        b_ref[...] = b.astype(b_ref.dtype)
        s = (a * _sigmoid(a)) * b
        acc_scr[...] += _dot(s.astype(MM), wd_ref[...])

        @pl.when(j == nf - 1)
        def _():
            o_ref[...] = x_ref[...] + 0.5 * acc_scr[...]

        finish()

    return _pc(
        body, "ffn_fwd" + ("_hosting" if hosted else ""), (nt, nf),
        [BS((tm, D), lambda i, j: (i, 0)), BS((1, D), lambda i, j: (0, 0)),
         BS((D, fc), lambda i, j: (0, j)), BS((D, fc), lambda i, j: (0, j)), BS((fc, D), lambda i, j: (j, 0))] + h_in,
        [BS((tm, D), lambda i, j: (i, 0)), BS((tm, fc), lambda i, j: (i, j)), BS((tm, fc), lambda i, j: (i, j))] + h_out,
        [SDS((T, D), F32), SDS((T, F), MM), SDS((T, F), MM)] + h_shape,
        scratch=[pltpu.VMEM((tm, D), MM), pltpu.VMEM((tm, D), F32)] + h_scratch,
    )(x, g, wg, wu, wd, *(hosted[1] if hosted else []))


def _ffn_bwd(x, dout, g, a, b, wg, wu, wd, hosted=None):
    T, D = x.shape
    F = wg.shape[1]
    tm, nf = min(FFN_BWD_ROWS, T), 2
    fc = F // nf
    nt = T // tm
    h_in, h_out, h_shape, h_scratch = _hosted_specs(hosted)

    def body(*refs):
        i, j = pl.program_id(0), pl.program_id(1)
        refs, finish = _hosted_edges(hosted, refs, 8, 7, (i == 0) & (j == 0), (i == nt - 1) & (j == nf - 1))
        (x_ref, do_ref, g_ref, a_ref, b_ref, wg_ref, wu_ref, wd_ref,
         dx_ref, h_ref, dy_ref, da_ref, db_ref, s_ref, dg_ref, dh_scr) = refs

        @pl.when(j == 0)
        def _():
            _, _, hg = _rms_fwd(x_ref[...], g_ref[...])
            h_ref[...] = hg.astype(h_ref.dtype)
            dy_ref[...] = (0.5 * do_ref[...]).astype(dy_ref.dtype)
            dh_scr[...] = jnp.zeros_like(dh_scr)

        @pl.when((i == 0) & (j == 0))
        def _():
            dg_ref[...] = jnp.zeros_like(dg_ref)

        a = a_ref[...].astype(F32)
        b = b_ref[...].astype(F32)
        ds = _dot_nt(dy_ref[...], wd_ref[...])
        sig = _sigmoid(a)
        sl = a * sig
        s_ref[...] = (sl * b).astype(s_ref.dtype)
        db = (ds * sl).astype(MM)
        da = (ds * b * (sig * (1.0 + a * (1.0 - sig)))).astype(MM)
        da_ref[...] = da
        db_ref[...] = db
        dh_scr[...] += _dot_nt(da, wg_ref[...]) + _dot_nt(db, wu_ref[...])

        @pl.when(j == nf - 1)
        def _():
            gv = g_ref[...]
            xh, r, _ = _rms_fwd(x_ref[...], gv)
            dx, dg = _rms_bwd(dh_scr[...], xh, r, gv)
            dx_ref[...] = do_ref[...] + dx
            dg_ref[...] += dg

        finish()

    tok = lambda i, j: (i, 0)
    hid = BS((tm, fc), lambda i, j: (i, j))
    return _pc(
        body, "ffn_bwd" + ("_hosting" if hosted else ""), (nt, nf),
        [BS((tm, D), tok), BS((tm, D), tok), BS((1, D), lambda i, j: (0, 0)), hid, hid,
         BS((D, fc), lambda i, j: (0, j)), BS((D, fc), lambda i, j: (0, j)), BS((fc, D), lambda i, j: (j, 0))] + h_in,
        [BS((tm, D), tok), BS((tm, D), tok), BS((tm, D), tok), hid, hid, hid, BS((1, D), lambda i, j: (0, 0))] + h_out,
        [SDS((T, D), F32), SDS((T, D), MM), SDS((T, D), MM),
         SDS((T, F), MM), SDS((T, F), MM), SDS((T, F), MM), SDS((1, D), F32)] + h_shape,
        scratch=[pltpu.VMEM((tm, D), F32)] + h_scratch,
    )(x, dout, g, a, b, wg, wu, wd, *(hosted[1] if hosted else []))


def _wgrad(a, b, name, hosted=None):
    T, K = a.shape
    N = b.shape[1]
    tt = min(512, T)
    tn = N
    for cand in (1408, 1280, 1024, 512, 256, 128):
        if N % cand == 0 and K * cand * 4 <= 6 * 2**20:
            tn = cand
            break
    nn, nt = N // tn, T // tt
    h_in, h_out, h_shape, h_scratch = _hosted_specs(hosted)

    def body(*refs):
        n, t = pl.program_id(0), pl.program_id(1)
        refs, finish = _hosted_edges(hosted, refs, 2, 1, (n == 0) & (t == 0), (n == nn - 1) & (t == nt - 1))
        a_ref, b_ref, o_ref = refs

        @pl.when(t == 0)
        def _():
            o_ref[...] = jnp.zeros_like(o_ref)

        o_ref[...] += _dot_tn(a_ref[...].astype(MM), b_ref[...].astype(MM))
        finish()

    res = _pc(
        body, name + ("_hosting" if hosted else ""), (nn, nt),
        [BS((tt, K), lambda n, t: (t, 0)), BS((tt, tn), lambda n, t: (t, n))] + h_in,
        [BS((K, tn), lambda n, t: (0, n))] + h_out, [SDS((K, N), F32)] + h_shape,
        scratch=h_scratch)(a, b, *(hosted[1] if hosted else []))
    return res if hosted else res[0]


C_POOL, C_Q, C_K, C_V, C_CA, C_CG, C_ZF, C_END = 0, 256, 768, 1280, 1792, 2048, 2304, 2560


def _mix_in_fwd(x, g, w):
    T, D = x.shape
    tm = min(512, T)

    def body(x_ref, g_ref, w_ref, up_ref, q_ref, k_ref, v_ref, ca_ref, cg_ref, zf_ref):
        _, _, hg = _rms_fwd(x_ref[...], g_ref[...])
        p = _dot(hg.astype(MM), w_ref[...])
        up_ref[...] = p[:, C_POOL:C_Q]
        q_ref[...] = p[:, C_Q:C_K].astype(q_ref.dtype)
        k_ref[...] = p[:, C_K:C_V].astype(k_ref.dtype)
        v_ref[...] = p[:, C_V:C_CA].astype(v_ref.dtype)
        ca_ref[...] = p[:, C_CA:C_CG]
        cg_ref[...] = p[:, C_CG:C_ZF]
        zf_ref[...] = p[:, C_ZF:C_ZF + LANES]

    tok = lambda i: (i, 0)
    widths = (256, 512, 512, 512, 256, 256, 128)
    dtypes = (F32, MM, MM, MM, F32, F32, F32)
    return _pc(
        body, "mix_in_fwd", (T // tm,),
        [BS((tm, D), tok), BS((1, D), lambda i: (0, 0)), BS((D, C_END), lambda i: (0, 0))],
        [BS((tm, wd), tok) for wd in widths],
        [SDS((T, wd), dt) for wd, dt in zip(widths, dtypes)])(x, g, w)


def _mix_in_bwd(x, dout, g, w, dup, dq, dk, dv, dca, dcg, dzf):
    T, D = x.shape
    tm = min(512, T)

    def body(x_ref, do_ref, g_ref, w_ref, dup_ref, dq_ref, dk_ref, dv_ref, dca_ref, dcg_ref, dzf_ref,
             dx_ref, h_ref, dp_ref, dg_ref):
        @pl.when(pl.program_id(0) == 0)
        def _():
            dg_ref[...] = jnp.zeros_like(dg_ref)

        gv = g_ref[...]
        xh, r, hg = _rms_fwd(x_ref[...], gv)
        h_ref[...] = hg.astype(h_ref.dtype)
        for ref, lo, hi in ((dup_ref, C_POOL, C_Q), (dq_ref, C_Q, C_K), (dk_ref, C_K, C_V), (dv_ref, C_V, C_CA),
                            (dca_ref, C_CA, C_CG), (dcg_ref, C_CG, C_ZF), (dzf_ref, C_ZF, C_ZF + LANES)):
            dp_ref[:, lo:hi] = ref[...].astype(dp_ref.dtype)
        dp_ref[:, C_ZF + LANES:C_END] = jnp.zeros((tm, C_END - C_ZF - LANES), dp_ref.dtype)
        dh = _dot_nt(dp_ref[...], w_ref[...])
        dx, dg = _rms_bwd(dh, xh, r, gv)
        dx_ref[...] = do_ref[...] + dx
        dg_ref[...] += dg

    tok = lambda i: (i, 0)
    widths = (256, 512, 512, 512, 256, 256, 128)
    return _pc(
        body, "mix_in_bwd", (T // tm,),
        [BS((tm, D), tok), BS((tm, D), tok), BS((1, D), lambda i: (0, 0)), BS((D, C_END), lambda i: (0, 0))]
        + [BS((tm, wd), tok) for wd in widths],
        [BS((tm, D), tok), BS((tm, D), tok), BS((tm, C_END), tok), BS((1, D), lambda i: (0, 0))],
        [SDS((T, D), F32), SDS((T, D), MM), SDS((T, C_END), MM), SDS((1, D), F32)],
    )(x, dout, g, w, dup, dq, dk, dv, dca, dcg, dzf)


def _mix_out_fwd(x, ya, yb, yc, wo):
    T, D = x.shape
    tm = min(512, T)

    def body(x_ref, ya_ref, yb_ref, yc_ref, wo_ref, o_ref):
        o_ref[...] = (x_ref[...] + _dot(ya_ref[...].astype(MM), wo_ref[0:256, :])
                      + _dot(yb_ref[...].astype(MM), wo_ref[256:768, :])
                      + _dot(yc_ref[...].astype(MM), wo_ref[768:1024, :]))

    tok = lambda i: (i, 0)
    return _pc(
        body, "mix_out_fwd", (T // tm,),
        [BS((tm, D), tok), BS((tm, 256), tok), BS((tm, 512), tok), BS((tm, 256), tok), BS((D, D), lambda i: (0, 0))],
        BS((tm, D), tok), SDS((T, D), F32))(x, ya, yb, yc, wo)


def _mix_out_bwd(dx, wo):
    T, D = dx.shape
    tm = min(512, T)

    def body(dx_ref, wo_ref, dya_ref, dyb_ref, dyc_ref):
        dy = _dot_nt(dx_ref[...].astype(MM), wo_ref[...])
        dya_ref[...] = dy[:, 0:256]
        dyb_ref[...] = dy[:, 256:768]
        dyc_ref[...] = dy[:, 768:1024]

    tok = lambda i: (i, 0)
    return _pc(
        body, "mix_out_bwd", (T // tm,),
        [BS((tm, D), tok), BS((D, D), lambda i: (0, 0))],
        [BS((tm, 256), tok), BS((tm, 512), tok), BS((tm, 256), tok)],
        [SDS((T, 256), F32), SDS((T, 512), F32), SDS((T, 256), F32)])(dx, wo)


def _fgate_fwd(zf, bias):
    T = zf.shape[0]
    tc = min(256, T)

    def body(z_ref, b_ref, f_ref, carry):
        @pl.when(pl.program_id(0) == 0)
        def _():
            carry[...] = jnp.zeros_like(carry)

        z = z_ref[...] + b_ref[...]
        logf = jnp.minimum(z, 0.0) - jnp.log(1.0 + jnp.exp(-jnp.abs(z)))
        row = lax.broadcasted_iota(jnp.int32, (tc, tc), 0)
        col = lax.broadcasted_iota(jnp.int32, (tc, tc), 1)
        tri = (col <= row).astype(F32)
        f_ref[...] = jnp.dot(tri, logf, precision=lax.Precision.HIGHEST, preferred_element_type=F32) + carry[...]
        carry[...] += jnp.sum(logf, axis=0, keepdims=True)

    return _pc(
        body, "fgate_fwd", (T // tc,),
        [BS((tc, LANES), lambda i: (i, 0)), BS((1, LANES), lambda i: (0, 0))],
        BS((tc, LANES), lambda i: (i, 0)), SDS((T, LANES), F32),
        scratch=[pltpu.VMEM((1, LANES), F32)])(zf, bias)


def _fgate_bwd(zf, bias, dFq, dFk):
    T = zf.shape[0]
    tc = min(256, T)
    n = T // tc
    slabs = dFq.shape[0]

    def body(z_ref, b_ref, dfq_ref, dfk_ref, dz_ref, db_ref, carry):
        @pl.when(pl.program_id(0) == 0)
        def _():
            carry[...] = jnp.zeros_like(carry)
            db_ref[...] = jnp.zeros_like(db_ref)

        df = dfk_ref[...]
        for sl in range(slabs):
            df = df + dfq_ref[sl]
        row = lax.broadcasted_iota(jnp.int32, (tc, tc), 0)
        col = lax.broadcasted_iota(jnp.int32, (tc, tc), 1)
        tri = (col >= row).astype(F32)
        dlogf = jnp.dot(tri, df, precision=lax.Precision.HIGHEST, preferred_element_type=F32) + carry[...]
        carry[...] += jnp.sum(df, axis=0, keepdims=True)
        lane = lax.broadcasted_iota(jnp.int32, (1, LANES), 1)
        dz = jnp.where(lane < HEADS, dlogf * _sigmoid(-(z_ref[...] + b_ref[...])), 0.0)
        dz_ref[...] = dz
        db_ref[...] += jnp.sum(dz, axis=0, keepdims=True)

    rev = lambda i: (n - 1 - i, 0)
    return _pc(
        body, "fgate_bwd", (n,),
        [BS((tc, LANES), rev), BS((1, LANES), lambda i: (0, 0)), BS((slabs, tc, LANES), lambda i: (0, n - 1 - i, 0)),
         BS((tc, LANES), rev)],
        [BS((tc, LANES), rev), BS((1, LANES), lambda i: (0, 0))],
        [SDS((T, LANES), F32), SDS((1, LANES), F32)],
        scratch=[pltpu.VMEM((1, LANES), F32)])(zf, bias, dFq, dFk)


LOG2E = 1.4426950408889634


def _split3(x):
    hi = x.astype(MM)
    r1 = x - hi.astype(F32)
    mid = r1.astype(MM)
    return hi, mid, (r1 - mid.astype(F32)).astype(MM)


def _place(lane, base, cols):
    out = jnp.zeros((cols[0].shape[0], LANES), MM)
    for i, c in enumerate(cols):
        out = jnp.where(lane == base + i, c, out)
    return out


def _head_col(block, lane, h):
    return jnp.sum(jnp.where(lane == h, block, 0.0), axis=-1, keepdims=True)


def _own_lanes(lane, hh):
    return (lane < HEAD_DIM) if hh == 0 else (lane >= HEAD_DIM)


def _attn_k_side(k_ref, f_ref, kb_ref, hp, T, rows, lse_ones, v_ref=None, vb_ref=None):
    lane = lax.broadcasted_iota(jnp.int32, (1, LANES), 1)
    one = jnp.ones((rows, 1), MM)

    def chunk(c, _):
        r0 = pl.multiple_of(c * rows, rows)
        kp = k_ref[pl.ds(r0, rows), :]
        fblk = f_ref[pl.ds(r0, rows), :]
        for hh in range(2):
            hi, mid, lo = _split3(-_head_col(fblk, lane, 2 * hp + hh) * LOG2E)
            cols = [one, one, one, hi, mid, lo] + ([one, one, one] if lse_ones else [])
            bias = _place(lane, HEAD_DIM * (1 - hh), cols)
            kb_ref[hh, pl.ds(r0, rows), :] = jnp.where(_own_lanes(lane, hh), kp, bias)
            if vb_ref is not None:
                vb_ref[hh, pl.ds(r0, rows), :] = jnp.where(_own_lanes(lane, hh), v_ref[pl.ds(r0, rows), :],
                                                           jnp.ones((rows, LANES), MM))
        return 0

    lax.fori_loop(0, T // rows, chunk, 0)


def _attn_q_side(qp, fblk, lane, hp, scale, lse_blk=None):
    qc = qp.astype(F32) * (scale * LOG2E)
    qhi = qc.astype(MM)
    qlo = (qc - qhi.astype(F32)).astype(MM)
    one = jnp.ones((qp.shape[0], 1), MM)
    out = []
    for hh in range(2):
        cols = list(_split3(_head_col(fblk, lane, 2 * hp + hh) * LOG2E)) + [one, one, one]
        if lse_blk is not None:
            cols += list(_split3(-_head_col(lse_blk, lane, 2 * hp + hh)))
        bias = _place(lane, HEAD_DIM * (1 - hh), cols)
        own = _own_lanes(lane, hh)
        out.append(jnp.concatenate([jnp.where(own, qhi, jnp.zeros_like(qhi)), jnp.where(own, qlo, bias)], axis=1))
    return out


def _causal(tq, tk):
    return lax.broadcasted_iota(jnp.int32, (tq, tk), 1) <= lax.broadcasted_iota(jnp.int32, (tq, tk), 0)


def _hosted_specs(hosted):
    if hosted is None:
        return [], [], [], []
    kind, arrays = hosted
    n = len(arrays)
    return [ANY] * n, [ANY] * n, [kind.out_shape(a) for a in arrays], kind.scratch(n)


def _hosted_edges(hosted, refs, n_in, n_out, first, last):
    if hosted is None:
        return refs, lambda: None
    kind, arrays = hosted
    n = len(arrays)
    nsem = len(kind.scratch(n))
    o0 = n_in + n + n_out
    ins, outs, sems = refs[n_in:n_in + n], refs[o0:o0 + n], refs[len(refs) - nsem:]

    @pl.when(first)
    def _():
        kind.start(ins, outs, *sems)

    def finish():
        @pl.when(last)
        def _():
            kind.wait(ins, outs, *sems)

    return refs[:n_in] + refs[n_in + n:o0] + refs[o0 + n:len(refs) - nsem], finish


def _attn_fwd(q, k, v, F, hosted=None):
    T = q.shape[0]
    tq = min(512, T)
    tk = tq
    nq = T // tq
    scale = 1.0 / math.sqrt(HEAD_DIM)
    h_in, h_out, h_shape, h_scratch = _hosted_specs(hosted)

    def body(*refs):
        hp, ib = pl.program_id(0), pl.program_id(1)
        refs, finish = _hosted_edges(hosted, refs, 5, 2, (hp == 0) & (ib == 0), (hp == HEADS // 2 - 1) & (ib == nq - 1))
        q_ref, k_ref, v_ref, fq_ref, f_ref, o_ref, lse_ref, kb_ref, vb_ref = refs
        lane = lax.broadcasted_iota(jnp.int32, (1, LANES), 1)

        @pl.when(ib == 0)
        def _():
            _attn_k_side(k_ref, f_ref, kb_ref, hp, T, tk, False, v_ref, vb_ref)

        qa = _attn_q_side(q_ref[...], fq_ref[...], lane, hp, scale)

        def tile(jb, carry, masked):
            off = pl.multiple_of(jb * tk, tk)
            kp = k_ref[pl.ds(off, tk), :]
            new = []
            for hh in range(2):
                m, acc = carry[hh]
                s = _dot_nt(qa[hh], jnp.concatenate([kp, kb_ref[hh, pl.ds(off, tk), :]], axis=1))
                if masked:
                    s = jnp.where(_causal(tq, tk), s, -jnp.inf)
                m2 = jnp.maximum(m, jnp.max(s, axis=-1, keepdims=True))
                p = jnp.exp2(s - m2)
                new.append((m2, acc * jnp.exp2(m - m2) + _dot(p.astype(MM), vb_ref[hh, pl.ds(off, tk), :])))
            return tuple(new)

        init = tuple((jnp.full((tq, 1), -jnp.inf, F32), jnp.zeros((tq, LANES), F32)) for _ in range(2))
        carry = lax.fori_loop(0, ib, lambda jb, c: tile(jb, c, False), init)
        (m0, a0), (m1, a1) = tile(ib, carry, True)
        l0, l1 = a0[:, HEAD_DIM:HEAD_DIM + 1], a1[:, 0:1]
        o_ref[...] = jnp.where(lane < HEAD_DIM, a0 / l0, a1 / l1)
        lse_ref[...] = jnp.where(lane == 2 * hp, m0 + jnp.log2(l0), jnp.where(lane == 2 * hp + 1, m1 + jnp.log2(l1), 0.0))
        finish()

    blk = lambda h, i: (i, h)
    full = lambda h, i: (0, h)
    return _pc(
        body, "attn_fwd" + ("_hosting" if hosted else ""), (HEADS // 2, nq),
        [BS((tq, LANES), blk), BS((T, LANES), full), BS((T, LANES), full), BS((tq, LANES), lambda h, i: (i, 0)),
         BS((T, LANES), lambda h, i: (0, 0))] + h_in,
        [BS((tq, LANES), blk), BS((None, tq, LANES), lambda h, i: (h, i, 0))] + h_out,
        [SDS((T, HEADS * HEAD_DIM), F32), SDS((HEADS // 2, T, LANES), F32)] + h_shape,
        scratch=[pltpu.VMEM((2, T, LANES), MM)] * 2 + h_scratch)(q, k, v, F, F, *(hosted[1] if hosted else []))


def _attn_bwd(q, k, v, F, o, lse, do, hosted=None):
    T = q.shape[0]
    tq = min(512, T)
    tk = tq
    nq = T // tq
    scale = 1.0 / math.sqrt(HEAD_DIM)
    h_in, h_out, h_shape, h_scratch = _hosted_specs(hosted)

    def body(*refs):
        hp, ib = pl.program_id(0), pl.program_id(1)
        refs, finish = _hosted_edges(hosted, refs, 8, 5, (hp == 0) & (ib == 0), (hp == HEADS // 2 - 1) & (ib == nq - 1))
        (q_ref, k_ref, v_ref, fq_ref, f_ref, o_ref, lse_ref, do_ref,
         dq_ref, dk_ref, dv_ref, dfq_ref, dfk_ref, kb_ref) = refs
        lane = lax.broadcasted_iota(jnp.int32, (1, LANES), 1)

        @pl.when(ib == 0)
        def _():
            _attn_k_side(k_ref, f_ref, kb_ref, hp, T, tk, True)
            dk_ref[...] = jnp.zeros_like(dk_ref)
            dv_ref[...] = jnp.zeros_like(dv_ref)
            dfk_ref[...] = jnp.zeros_like(dfk_ref)

        qp = q_ref[...]
        qa = _attn_q_side(qp, fq_ref[...], lane, hp, scale, lse_ref[...])
        dob = do_ref[...].astype(MM)
        dprod = dob.astype(F32) * o_ref[...]
        qs = (qp.astype(F32) * scale).astype(MM)
        heads = []
        for hh in range(2):
            own = _own_lanes(lane, hh)
            heads.append((jnp.where(own, dob, jnp.zeros_like(dob)), jnp.where(own, qs, jnp.zeros_like(qs)),
                          jnp.sum(jnp.where(own, dprod, 0.0), axis=-1, keepdims=True)))

        def tile(jb, carry, masked):
            off = pl.multiple_of(jb * tk, tk)
            kp = k_ref[pl.ds(off, tk), :]
            vp = v_ref[pl.ds(off, tk), :]
            new = []
            dv_t = jnp.zeros((tk, LANES), F32)
            dk_t = jnp.zeros((tk, LANES), F32)
            for hh in range(2):
                dq, rs = carry[hh]
                dom, qm, delta = heads[hh]
                p = jnp.exp2(_dot_nt(qa[hh], jnp.concatenate([kp, kb_ref[hh, pl.ds(off, tk), :]], axis=1)))
                if masked:
                    p = jnp.where(_causal(tq, tk), p, 0.0)
                ds = p * (_dot_nt(dom, vp) - delta)
                dsb = ds.astype(MM)
                dv_t = dv_t + _dot_tn(p.astype(MM), dom)
                dk_t = dk_t + _dot_tn(dsb, qm)
                dfk_ref[jb, pl.ds(hh, 1), :] -= jnp.sum(ds, axis=0, keepdims=True)
                new.append((dq + _dot(dsb, kp), rs + jnp.sum(ds, axis=-1, keepdims=True)))
            dv_ref[pl.ds(off, tk), :] += dv_t
            dk_ref[pl.ds(off, tk), :] += dk_t
            return tuple(new)

        init = tuple((jnp.zeros((tq, LANES), F32), jnp.zeros((tq, 1), F32)) for _ in range(2))
        carry = lax.fori_loop(0, ib, lambda jb, c: tile(jb, c, False), init)
        (dq0, rs0), (dq1, rs1) = tile(ib, carry, True)
        dq_ref[...] = jnp.where(lane < HEAD_DIM, dq0, dq1) * scale
        dfq_ref[...] = jnp.where(lane == 2 * hp, rs0, jnp.where(lane == 2 * hp + 1, rs1, 0.0))
        finish()

    blk = lambda h, i: (i, h)
    full = lambda h, i: (0, h)
    slab = BS((None, tq, LANES), lambda h, i: (h, i, 0))
    return _pc(
        body, "attn_bwd" + ("_hosting" if hosted else ""), (HEADS // 2, nq),
        [BS((tq, LANES), blk), BS((T, LANES), full), BS((T, LANES), full), BS((tq, LANES), lambda h, i: (i, 0)),
         BS((T, LANES), lambda h, i: (0, 0)), BS((tq, LANES), blk), slab, BS((tq, LANES), blk)] + h_in,
        [BS((tq, LANES), blk), BS((T, LANES), full), BS((T, LANES), full), slab,
         BS((None, nq, 2, tk), lambda h, i: (h, 0, 0, 0))] + h_out,
        [SDS((T, HEADS * HEAD_DIM), F32)] * 3 + [SDS((HEADS // 2, T, LANES), F32), SDS((HEADS // 2, nq, 2, tk), F32)]
        + h_shape,
        scratch=[pltpu.VMEM((2, T, LANES), MM)] + h_scratch,
    )(q, k, v, F, F, o, lse, do, *(hosted[1] if hosted else []))


POOL_HALO = 16
CONV_HALO = 32


def _group_select(lane, v0, v1, v2, v3):
    return jnp.where(lane < 64, v0, jnp.where(lane < 128, v1, jnp.where(lane < 192, v2, v3)))


def _roll_down(x, k):
    return x if k == 0 else pltpu.roll(x, k, 0)


def _roll_up(x, k):
    return x if k == 0 else pltpu.roll(x, x.shape[0] - k, 0)


def _pool_terms(u, u_prev, tile, tm):
    ext = jnp.concatenate([u_prev, u], axis=0)
    s2 = ext + _roll_down(ext, 1)
    s4 = s2 + _roll_down(s2, 2)
    s8 = s4 + _roll_down(s4, 4)
    s16 = s8 + _roll_down(s8, 8)
    lane = lax.broadcasted_iota(jnp.int32, (1, 256), 1)
    ws = _group_select(lane, s2, s4, s8, s16)[POOL_HALO:, :]
    wlen = _group_select(lane, 2.0, 4.0, 8.0, 16.0).astype(F32)
    return ws / _pool_count(tile, tm, tm, wlen) - u


def _pool_count(tile, tm, rows, wlen):
    t = (tile * tm + 1 + lax.broadcasted_iota(jnp.int32, (rows, 1), 0)).astype(F32)
    return jnp.minimum(t, wlen)


def _layer_norm(y, lg, lb):
    mu = jnp.mean(y, axis=-1, keepdims=True)
    yc = y - mu
    rstd = lax.rsqrt(jnp.mean(yc * yc, axis=-1, keepdims=True) + NORM_EPS)
    yh = yc * rstd
    return yh, rstd, yh * lg + lb


def _halo_specs(tm, T, halo, prev):
    per = tm // halo
    if prev:
        return BS((halo, 256), lambda i: (jnp.maximum(i * per - 1, 0), 0))
    return BS((halo, 256), lambda i: (jnp.minimum((i + 1) * per, T // halo - 1), 0))


def _local_fwd(up, ca, cg, bd, pscale, cw, cb, lg, lb):
    T = up.shape[0]
    tm = min(512, T)

    def body(up_ref, uph_ref, ca_ref, cah_ref, cg_ref, cgh_ref, bd_ref, ps_ref, cw_ref, cb_ref, lg_ref, lb_ref,
             ya_ref, yc_ref, u_ref, y_ref):
        i = pl.program_id(0)
        first = i == 0
        pooled = _pool_terms(up_ref[...], jnp.where(first, 0.0, uph_ref[...]), i, tm)
        ya_ref[...] = (_dot(pooled.astype(MM), bd_ref[...]) * ps_ref[...]).astype(ya_ref.dtype)

        u = ca_ref[...] * _sigmoid(cg_ref[...])
        uh = jnp.where(first, 0.0, cah_ref[...] * _sigmoid(cgh_ref[...]))
        ext = jnp.concatenate([uh, u], axis=0)
        y = jnp.zeros((tm, 256), F32) + cb_ref[...]
        for kk in range(CONV_K):
            y = y + cw_ref[kk:kk + 1, :] * _roll_up(ext, CONV_HALO - (CONV_K - 1) + kk)[:tm, :]
        _, _, z = _layer_norm(y, lg_ref[...], lb_ref[...])
        yc_ref[...] = (z * _sigmoid(z)).astype(yc_ref.dtype)
        u_ref[...] = u
        y_ref[...] = y

    tok = lambda i: (i, 0)
    par = lambda i: (0, 0)
    t256 = BS((tm, 256), tok)
    return _pc(
        body, "local_fwd", (T // tm,),
        [t256, _halo_specs(tm, T, POOL_HALO, True), t256, _halo_specs(tm, T, CONV_HALO, True),
         t256, _halo_specs(tm, T, CONV_HALO, True),
         BS((256, 256), par), BS((1, 256), par), BS((32, 256), par), BS((1, 256), par), BS((1, 256), par),
         BS((1, 256), par)],
        [t256, t256, t256, t256],
        [SDS((T, 256), MM), SDS((T, 256), MM), SDS((T, 256), F32), SDS((T, 256), F32)],
    )(up, up, ca, ca, cg, cg, bd, pscale, cw, cb, lg, lb)


def _local_bwd(up, dya, ca, cg, u, y, dyc, bd, pscale, cw, lg, lb):
    T = up.shape[0]
    tm = min(512, T)
    n = T // tm

    def body(up_ref, uph_ref, dya_ref, dyan_ref, ca_ref, cg_ref, u_ref, uh_ref, y_ref, yn_ref, dyc_ref, dycn_ref,
             bd_ref, ps_ref, cw_ref, lg_ref, lb_ref,
             dup_ref, dca_ref, dcg_ref, dbd_ref, dps_ref, dcw_ref, dcb_ref, dlg_ref, dlb_ref):
        i = pl.program_id(0)
        first = i == 0
        last = i == n - 1

        @pl.when(first)
        def _():
            for ref in (dbd_ref, dps_ref, dcw_ref, dcb_ref, dlg_ref, dlb_ref):
                ref[...] = jnp.zeros_like(ref)

        ps = ps_ref[...]
        pooled = _pool_terms(up_ref[...], jnp.where(first, 0.0, uph_ref[...]), i, tm).astype(MM)
        dya_t = dya_ref[...]
        dps_ref[...] += jnp.sum(dya_t * _dot(pooled, bd_ref[...]), axis=0, keepdims=True)
        dm = (jnp.concatenate([dya_t, jnp.where(last, 0.0, dyan_ref[...])], axis=0) * ps).astype(MM)
        dbd_ref[...] += _dot_tn(pooled, dm[:tm, :])
        dpool = _dot_nt(dm, bd_ref[...])
        lane = lax.broadcasted_iota(jnp.int32, (1, 256), 1)
        wlen = _group_select(lane, 2.0, 4.0, 8.0, 16.0).astype(F32)
        e = dpool / _pool_count(i, tm, tm + POOL_HALO, wlen)
        f2 = e + _roll_up(e, 1)
        f4 = f2 + _roll_up(f2, 2)
        f8 = f4 + _roll_up(f4, 4)
        f16 = f8 + _roll_up(f8, 8)
        dup_ref[...] = _group_select(lane, f2, f4, f8, f16)[:tm, :] - dpool[:tm, :]

        lgv = lg_ref[...]
        yext = jnp.concatenate([y_ref[...], yn_ref[...]], axis=0)
        dyc = jnp.concatenate([dyc_ref[...], jnp.where(last, 0.0, dycn_ref[...])], axis=0)
        yh, rstd, z = _layer_norm(yext, lgv, lb_ref[...])
        sig = _sigmoid(z)
        dz = dyc * (sig * (1.0 + z * (1.0 - sig)))
        dlg_ref[...] += jnp.sum((dz * yh)[:tm, :], axis=0, keepdims=True)
        dlb_ref[...] += jnp.sum(dz[:tm, :], axis=0, keepdims=True)
        dyh = dz * lgv
        dy = rstd * (dyh - jnp.mean(dyh, axis=-1, keepdims=True) - yh * jnp.mean(dyh * yh, axis=-1, keepdims=True))
        dy_t = dy[:tm, :]
        dcb_ref[...] += jnp.sum(dy_t, axis=0, keepdims=True)
        uext = jnp.concatenate([jnp.where(first, 0.0, uh_ref[...]), u_ref[...]], axis=0)
        du = jnp.zeros((tm, 256), F32)
        for kk in range(CONV_K):
            shifted = _roll_up(uext, CONV_HALO - (CONV_K - 1) + kk)[:tm, :]
            dcw_ref[kk:kk + 1, :] += jnp.sum(dy_t * shifted, axis=0, keepdims=True)
            du = du + cw_ref[kk:kk + 1, :] * _roll_up(dy, CONV_K - 1 - kk)[:tm, :]
        sg = _sigmoid(cg_ref[...])
        dca_ref[...] = du * sg
        dcg_ref[...] = du * ca_ref[...] * sg * (1.0 - sg)

    tok = lambda i: (i, 0)
    par = lambda i: (0, 0)
    t256 = BS((tm, 256), tok)
    p1 = BS((1, 256), par)
    return _pc(
        body, "local_bwd", (n,),
        [t256, _halo_specs(tm, T, POOL_HALO, True), t256, _halo_specs(tm, T, POOL_HALO, False), t256, t256,
         t256, _halo_specs(tm, T, CONV_HALO, True), t256, _halo_specs(tm, T, CONV_HALO, False),
         t256, _halo_specs(tm, T, CONV_HALO, False),
         BS((256, 256), par), p1, BS((32, 256), par), p1, p1],
        [t256, t256, t256, BS((256, 256), par), p1, BS((32, 256), par), p1, p1, p1],
        [SDS((T, 256), F32)] * 3 + [SDS((256, 256), F32), SDS((1, 256), F32), SDS((32, 256), F32)]
        + [SDS((1, 256), F32)] * 3,
    )(up, up, dya, dya, ca, cg, u, u, y, y, dyc, dyc, bd, pscale, cw, lg, lb)


def _head(x, g, target):
    T, D = x.shape
    tm = min(512, T)

    def body(x_ref, g_ref, t_ref, loss_ref, dx_ref, dg_ref):
        @pl.when(pl.program_id(0) == 0)
        def _():
            loss_ref[...] = jnp.zeros_like(loss_ref)
            dg_ref[...] = jnp.zeros_like(dg_ref)

        gv = g_ref[...]
        xh, r, yv = _rms_fwd(x_ref[...], gv)
        err = yv - t_ref[...]
        loss_ref[...] += 0.5 * jnp.sum(jnp.mean(err * err, axis=-1, keepdims=True), axis=0, keepdims=True)
        dx, dg = _rms_bwd(err * (1.0 / D), xh, r, gv)
        dx_ref[...] = dx
        dg_ref[...] += dg

    tok = lambda i: (i, 0)
    par = lambda i: (0, 0)
    return _pc(
        body, "head", (T // tm,),
        [BS((tm, D), tok), BS((1, D), par), BS((tm, D), tok)],
        [BS((1, LANES), par), BS((tm, D), tok), BS((1, D), par)],
        [SDS((1, LANES), F32), SDS((T, D), F32), SDS((1, D), F32)])(x, g, target)


def _adamw(w, gs, m, v, name):
    R, C = w.shape
    tr = R
    for cand in (512, 256, 128, 64, 32, 16, 8):
        if R % cand == 0:
            tr = cand
            break
    ng = len(gs)

    def body(*refs):
        w_ref, g_refs, m_ref, v_ref = refs[0], refs[1:1 + ng], refs[1 + ng], refs[2 + ng]
        g_ref, d_ref, m2_ref, v2_ref = refs[3 + ng:]
        g = g_refs[0][...]
        for r in g_refs[1:]:
            g = g + r[...]
        m2 = ADAM_B1 * m_ref[...] + (1.0 - ADAM_B1) * g
        v2 = ADAM_B2 * v_ref[...] + (1.0 - ADAM_B2) * jnp.square(g)
        m_hat = m2 / (1.0 - ADAM_B1 ** ADAM_STEP)
        v_hat = v2 / (1.0 - ADAM_B2 ** ADAM_STEP)
        g_ref[...] = g
        d_ref[...] = -ADAM_LR * (m_hat / (jnp.sqrt(v_hat) + ADAM_EPS) + ADAM_WD * w_ref[...])
        m2_ref[...] = m2
        v2_ref[...] = v2

    blk = BS((tr, C), lambda i: (i, 0))
    return _pc(body, name, (R // tr,), [blk] * (3 + ng), [blk] * 4, [SDS((R, C), F32)] * 4)(w, *gs, m, v)


def _sum_parts(owns, recvs, name):
    L = len(owns)
    R, C = owns[0].shape
    tr = next(t for t in (512, 256, 128, 64, 32, 16) if R % t == 0)

    def body(*refs):
        l = pl.program_id(0)
        s_ref = refs[2 * L]
        for ll in range(L):
            @pl.when(l == ll)
            def _(o_ref=refs[ll], r_ref=refs[L + ll]):
                s_ref[...] = ((o_ref[...] + r_ref[0].astype(F32)) + r_ref[1].astype(F32)) + r_ref[2].astype(F32)

    own_specs = [BS((tr, C), lambda l, i, ll=ll: (jnp.where(l == ll, i, 0), 0)) for ll in range(L)]
    recv_specs = [BS((3, tr, C), lambda l, i, ll=ll: (0, jnp.where(l == ll, i, 0), 0)) for ll in range(L)]
    return _pc(body, name, (L, R // tr), own_specs + recv_specs,
               BS((None, tr, C), lambda l, i: (l, i, 0)), SDS((L, R, C), F32))(*owns, *recvs)


def _sum8(parts, name):
    _, R, C = parts.shape

    def body(p_ref, s_ref):
        acc = p_ref[0]
        for d in range(1, 8):
            acc = acc + p_ref[d]
        s_ref[...] = acc

    return _pc(body, name, (1,), [BS((8, R, C), lambda i: (0, 0, 0))], BS((R, C), lambda i: (0, 0)),
               SDS((R, C), F32))(parts)


def _position():
    return lax.axis_index("x"), lax.axis_index("y"), lax.axis_index("c")


CHIP_FLIPS = ((1, 0), (0, 1), (1, 1))


class _GatherChips:
    @staticmethod
    def scratch(n):
        return [pltpu.SemaphoreType.DMA((3 * n,)), pltpu.SemaphoreType.DMA((3 * n,)), pltpu.SemaphoreType.DMA((n,))]

    @staticmethod
    def out_shape(block):
        return SDS((4,) + tuple(block.shape), block.dtype)

    @staticmethod
    def _copies(ins, outs, send_sems, recv_sems, local_sems, arrivals):
        x, y, c = _position()
        local, remote = [], []
        for i, (in_ref, out_ref) in enumerate(zip(ins, outs)):
            local.append(pltpu.make_async_copy(in_ref, out_ref.at[2 * x + y], local_sems.at[i]))
            for k, (fx, fy) in enumerate(CHIP_FLIPS):
                slot = 2 * (x ^ fx) + (y ^ fy) if arrivals else 2 * x + y
                remote.append(pltpu.make_async_remote_copy(
                    src_ref=in_ref, dst_ref=out_ref.at[slot], send_sem=send_sems.at[3 * i + k],
                    recv_sem=recv_sems.at[3 * i + k], device_id=(x ^ fx, y ^ fy, c), device_id_type=MESH))
        return local, remote

    @classmethod
    def start(cls, ins, outs, *sems):
        local, sends = cls._copies(ins, outs, *sems, arrivals=False)
        for cp in local + sends:
            cp.start()

    @classmethod
    def wait(cls, ins, outs, *sems):
        local, arrivals = cls._copies(ins, outs, *sems, arrivals=True)
        for cp in arrivals:
            cp.wait_recv()
        for cp in arrivals:
            cp.wait_send()
        for cp in local:
            cp.wait()


class _GatherChipsSplit(_GatherChips):
    @staticmethod
    def scratch(n):
        return [pltpu.SemaphoreType.DMA((6 * n,)), pltpu.SemaphoreType.DMA((6 * n,)), pltpu.SemaphoreType.DMA((n,))]

    @staticmethod
    def _half(ref, which):
        rows = ref.shape[0] // 2
        return ref.at[pl.ds(pl.multiple_of(which * rows, 16), rows)]

    @staticmethod
    def _local(ins, outs, local_sems):
        x, y, _ = _position()
        return [pltpu.make_async_copy(in_ref, out_ref.at[2 * x + y], local_sems.at[i])
                for i, (in_ref, out_ref) in enumerate(zip(ins, outs))]

    @classmethod
    def _between_chips(cls, ins, outs, send_sems, recv_sems, arrivals):
        x, y, c = _position()
        return [
            pltpu.make_async_remote_copy(
                src_ref=cls._half(in_ref, c),
                dst_ref=cls._half(out_ref.at[2 * (x ^ fx) + (y ^ fy) if arrivals else 2 * x + y], c),
                send_sem=send_sems.at[6 * i + k], recv_sem=recv_sems.at[6 * i + k],
                device_id=(x ^ fx, y ^ fy, c), device_id_type=MESH)
            for i, (in_ref, out_ref) in enumerate(zip(ins, outs)) for k, (fx, fy) in enumerate(CHIP_FLIPS)]

    @classmethod
    def _between_cores(cls, outs, send_sems, recv_sems, arrivals):
        x, y, c = _position()
        copies = []
        for i, out_ref in enumerate(outs):
            for k, (fx, fy) in enumerate(CHIP_FLIPS):
                half = cls._half(out_ref.at[2 * (x ^ fx) + (y ^ fy)], 1 - c if arrivals else c)
                copies.append(pltpu.make_async_remote_copy(
                    src_ref=half, dst_ref=half, send_sem=send_sems.at[6 * i + 3 + k],
                    recv_sem=recv_sems.at[6 * i + 3 + k], device_id=(x, y, 1 - c), device_id_type=MESH))
        return copies

    @classmethod
    def start(cls, ins, outs, send_sems, recv_sems, local_sems):
        for cp in cls._local(ins, outs, local_sems) + cls._between_chips(ins, outs, send_sems, recv_sems, False):
            cp.start()

    @classmethod
    def wait(cls, ins, outs, send_sems, recv_sems, local_sems):
        arrivals = cls._between_chips(ins, outs, send_sems, recv_sems, True)
        onward = cls._between_cores(outs, send_sems, recv_sems, False)
        for cp, nxt in zip(arrivals, onward):
            cp.wait_recv()
            nxt.start()
        for cp in cls._between_cores(outs, send_sems, recv_sems, True):
            cp.wait_recv()
        for cp in arrivals + onward:
            cp.wait_send()
        for cp in cls._local(ins, outs, local_sems):
            cp.wait()


class _Symmetric:
    @classmethod
    def start(cls, ins, outs, *sems):
        for cp in cls._copies(ins, outs, *sems):
            cp.start()

    @classmethod
    def wait(cls, ins, outs, *sems):
        copies = cls._copies(ins, outs, *sems)
        for cp in copies:
            cp.wait_recv()
        for cp in copies:
            cp.wait_send()


class _ScatterChips(_Symmetric):
    @staticmethod
    def scratch(n):
        return [pltpu.SemaphoreType.DMA((3 * n,)), pltpu.SemaphoreType.DMA((3 * n,))]

    @staticmethod
    def out_shape(parts):
        return SDS((3,) + tuple(parts.shape[1:]), parts.dtype)

    @staticmethod
    def _copies(ins, outs, send_sems, recv_sems):
        x, y, c = _position()
        return [
            pltpu.make_async_remote_copy(
                src_ref=in_ref.at[2 * (x ^ fx) + (y ^ fy)], dst_ref=out_ref.at[k],
                send_sem=send_sems.at[3 * i + k], recv_sem=recv_sems.at[3 * i + k],
                device_id=(x ^ fx, y ^ fy, c), device_id_type=MESH)
            for i, (in_ref, out_ref) in enumerate(zip(ins, outs)) for k, (fx, fy) in enumerate(CHIP_FLIPS)]


class _SwapCores(_Symmetric):
    @staticmethod
    def scratch(n):
        return [pltpu.SemaphoreType.DMA((n,)), pltpu.SemaphoreType.DMA((n,))]

    @staticmethod
    def out_shape(block):
        return SDS(block.shape, block.dtype)

    @staticmethod
    def _copies(ins, outs, send_sems, recv_sems):
        x, y, c = _position()
        return [
            pltpu.make_async_remote_copy(
                src_ref=in_ref, dst_ref=out_ref, send_sem=send_sems.at[i], recv_sem=recv_sems.at[i],
                device_id=(x, y, 1 - c), device_id_type=MESH)
            for i, (in_ref, out_ref) in enumerate(zip(ins, outs))]


def _exchange(kind, arrays, name):
    n = len(arrays)

    def body(*refs):
        ins, outs, sems = refs[:n], refs[n:2 * n], refs[2 * n:]
        kind.start(ins, outs, *sems)
        kind.wait(ins, outs, *sems)

    return pl.pallas_call(body, out_shape=[kind.out_shape(a) for a in arrays], in_specs=[ANY] * n,
                          out_specs=[ANY] * n, name=name, scratch_shapes=kind.scratch(n))(*arrays)


def _gather_all(block, name):
    R, C = block.shape
    flips = [(fx, fy, fc) for fx in (0, 1) for fy in (0, 1) for fc in (0, 1)][1:]

    def body(in_ref, out_ref, send_sems, recv_sems, local_sem):
        x, y, c = _position()
        mine = out_ref.at[4 * x + 2 * y + c]
        local = pltpu.make_async_copy(in_ref, mine, local_sem)
        local.start()
        copies = [
            pltpu.make_async_remote_copy(
                src_ref=in_ref, dst_ref=mine, send_sem=send_sems.at[k], recv_sem=recv_sems.at[k],
                device_id=(x ^ fx, y ^ fy, c ^ fc), device_id_type=MESH)
            for k, (fx, fy, fc) in enumerate(flips)]
        for cp in copies:
            cp.start()
        for k, (fx, fy, fc) in enumerate(flips):
            theirs = out_ref.at[4 * (x ^ fx) + 2 * (y ^ fy) + (c ^ fc)]
            pltpu.make_async_remote_copy(
                src_ref=in_ref, dst_ref=theirs, send_sem=send_sems.at[k], recv_sem=recv_sems.at[k],
                device_id=(x ^ fx, y ^ fy, c ^ fc), device_id_type=MESH).wait_recv()
        for cp in copies:
            cp.wait_send()
        local.wait()

    return pl.pallas_call(
        body, out_shape=SDS((8, R, C), block.dtype), in_specs=[ANY], out_specs=ANY, name=name,
        scratch_shapes=[pltpu.SemaphoreType.DMA((7,)), pltpu.SemaphoreType.DMA((7,)), pltpu.SemaphoreType.DMA(())])(block)


BIG = ("ffn1_w_gate", "ffn1_w_up", "ffn1_w_down", "w_in", "w_out", "ffn2_w_gate", "ffn2_w_up", "ffn2_w_down")
COL_SHARDED = ("ffn1_w_gate", "ffn1_w_up", "w_in", "ffn2_w_gate", "ffn2_w_up")
FIRST = tuple((n, 0) for n in ("ffn1_w_gate", "ffn1_w_up", "ffn1_w_down"))
LATE = tuple((n, 0) for n in ("w_out", "ffn2_w_gate", "ffn2_w_up", "ffn2_w_down")) + tuple((n, 1) for n in BIG)


def _to_shards(name, full):
    r, c = full.shape
    if name in COL_SHARDED:
        return full.reshape(r, 4, c // 4).transpose(1, 0, 2)
    return full.reshape(4, r // 4, c)


def _own_shard(name, full, chip):
    r, c = full.shape
    if name in COL_SHARDED:
        return lax.dynamic_slice_in_dim(full, chip * (c // 4), c // 4, axis=1)
    return lax.dynamic_slice_in_dim(full, chip * (r // 4), r // 4, axis=0)


def _from_shards(name, sh):
    _, r, c = sh.shape
    if name in COL_SHARDED:
        return sh.transpose(1, 0, 2).reshape(r, 4 * c)
    return sh.reshape(4 * r, c)


def _pad_w_in(w):
    return jnp.concatenate([w[:, :1792], w[:, 1800:2312], w[:, 1792:1800], jnp.zeros((w.shape[0], 248), w.dtype)], axis=1)


def _unpad_w_in(g):
    return jnp.concatenate([g[:, :1792], g[:, 2304:2312], g[:, 1792:2304]], axis=1)


def _block_diag(pw):
    out = jnp.zeros((256, 256), pw.dtype)
    for gidx in range(4):
        out = lax.dynamic_update_slice(out, pw[gidx], (64 * gidx, 64 * gidx))
    return out


SMALL = ("ffn1_norm", "mix_norm", "pool_w", "pool_scale", "forget_bias", "conv_b", "conv_ln_g", "conv_ln_b",
         "ffn2_norm", "final_norm")


def _pack_small(arrs):
    rows = []
    for a in arrs:
        flat = a.reshape(-1)
        flat = jnp.pad(flat, (0, -flat.shape[0] % LANES))
        rows.append(flat.reshape(-1, LANES))
    total = sum(r.shape[0] for r in rows)
    if total % 8:
        rows.append(jnp.zeros((-total % 8, LANES), F32))
    return jnp.concatenate(rows, axis=0)


def _unpack_small(buf, shapes):
    out, off = [], 0
    for shp in shapes:
        n = math.prod(shp)
        nr = -(-n // LANES)
        out.append(buf[off:off + nr].reshape(-1)[:n].reshape(shp))
        off += nr
    return out


def _grad_parts(grads, pieces):
    return [_to_shards(n, grads[n][l]).astype(MM) for n, l in pieces]


def _forward_backward(x, target, W, shards=None):
    T = x.shape[0]
    L = W["ffn1_norm"].shape[0]
    saved = []
    recv = {}
    for l in range(L):
        g1, gm, g2 = (W[n][l][None, :] for n in ("ffn1_norm", "mix_norm", "ffn2_norm"))
        first = shards is not None and l == 0
        hosted = (_GatherChips, [shards["w_in"][0], shards["conv_w"]]) if first else None
        x1, a1, b1, *got = _ffn_fwd(x, g1, W["ffn1_w_gate"][l], W["ffn1_w_up"][l], W["ffn1_w_down"][l], hosted=hosted)
        if first:
            W["w_in"][0] = _from_shards("w_in", got[0])
            W["conv_w"] = got[1].transpose(1, 2, 0, 3).reshape(L, CONV_K, 256)
        w_in = _pad_w_in(W["w_in"][l])
        up, q, k, v, ca, cg, zf = _mix_in_fwd(x1, gm, w_in)
        fb = jnp.pad(W["forget_bias"][l], (0, LANES - HEADS))[None, :]
        F = _fgate_fwd(zf, fb)
        if first:
            yb, lse, *got = _attn_fwd(q, k, v, F, hosted=(_GatherChips, [shards[n][ll] for n, ll in LATE]))
            for (n, ll), sh in zip(LATE, got):
                W[n][ll] = _from_shards(n, sh)
        else:
            yb, lse = _attn_fwd(q, k, v, F)
        bd = _block_diag(W["pool_w"][l]).astype(MM)
        ps, cb, lg, lb = (W[n][l][None, :] for n in ("pool_scale", "conv_b", "conv_ln_g", "conv_ln_b"))
        cw = jnp.pad(W["conv_w"][l], ((0, 1), (0, 0)))
        ya, yc, cu, cy = _local_fwd(up, ca, cg, bd, ps, cw, cb, lg, lb)
        x2 = _mix_out_fwd(x1, ya, yb, yc, W["w_out"][l])
        x3, a2, b2 = _ffn_fwd(x2, g2, W["ffn2_w_gate"][l], W["ffn2_w_up"][l], W["ffn2_w_down"][l])
        saved.append(dict(x0=x, x1=x1, x2=x2, ab1=(a1, b1), ab2=(a2, b2), w_in=w_in, up=up, ca=ca, cg=cg, zf=zf, fb=fb, F=F,
                          q=q, k=k, v=v, lse=lse, bd=bd, cw=cw, cu=cu, cy=cy, ya=ya, yb=yb, yc=yc))
        x = x3

    loss, dx, dgf = _head(x, W["final_norm"][None, :], target)
    grads = {n: [None] * L for n in W if n != "final_norm"}
    grads["final_norm"] = dgf[0]
    for l in reversed(range(L)):
        s = saved[l]
        g1, gm, g2 = (W[n][l][None, :] for n in ("ffn1_norm", "mix_norm", "ffn2_norm"))
        ps, lg, lb = (W[n][l][None, :] for n in ("pool_scale", "conv_ln_g", "conv_ln_b"))
        dx, h, dy, da, db, sact, dg = _ffn_bwd(s["x2"], dx, g2, *s["ab2"], W["ffn2_w_gate"][l], W["ffn2_w_up"][l],
                                               W["ffn2_w_down"][l])
        grads["ffn2_norm"][l] = dg[0]
        grads["ffn2_w_gate"][l] = _wgrad(h, da, "wgrad_gate")
        grads["ffn2_w_up"][l] = _wgrad(h, db, "wgrad_up")
        grads["ffn2_w_down"][l] = _wgrad(sact, dy, "wgrad_down")
        dya, dyb, dyc = _mix_out_bwd(dx, W["w_out"][l])
        grads["w_out"][l] = jnp.concatenate(
            [_wgrad(s["ya"], dx, "wgrad_out_a"), _wgrad(s["yb"], dx, "wgrad_out_b"), _wgrad(s["yc"], dx, "wgrad_out_c")], axis=0)
        first = shards is not None and l == 0
        if first:
            dq, dk, dv, dfq, dfk, *got = _attn_bwd(s["q"], s["k"], s["v"], s["F"], s["yb"], s["lse"], dyb,
                                                  hosted=(_ScatterChips, _grad_parts(grads, LATE)))
            recv.update(zip(LATE, got))
        else:
            dq, dk, dv, dfq, dfk = _attn_bwd(s["q"], s["k"], s["v"], s["F"], s["yb"], s["lse"], dyb)
        dfk_cols = jnp.pad(dfk.transpose(0, 2, 1, 3).reshape(HEADS, T).T, ((0, 0), (0, LANES - HEADS)))
        dzf, dfb = _fgate_bwd(s["zf"], s["fb"], dfq, dfk_cols)
        grads["forget_bias"][l] = dfb[0, :HEADS]
        dup, dca, dcg, dbd, dps, dcw, dcb, dlg, dlb = _local_bwd(
            s["up"], dya, s["ca"], s["cg"], s["cu"], s["cy"], dyc, s["bd"], ps, s["cw"], lg, lb)
        grads["pool_w"][l] = jnp.stack([dbd[64 * i:64 * i + 64, 64 * i:64 * i + 64] for i in range(4)])
        grads["pool_scale"][l], grads["conv_b"][l] = dps[0], dcb[0]
        grads["conv_ln_g"][l], grads["conv_ln_b"][l] = dlg[0], dlb[0]
        grads["conv_w"][l] = dcw[:CONV_K]
        dx, h, dp, dg = _mix_in_bwd(s["x1"], dx, gm, s["w_in"], dup, dq, dk, dv, dca, dcg, dzf)
        grads["mix_norm"][l] = dg[0]
        grads["w_in"][l] = _unpad_w_in(_wgrad(h, dp, "wgrad_in"))
        ffn1 = (W["ffn1_w_gate"][l], W["ffn1_w_up"][l], W["ffn1_w_down"][l])
        if not first:
            dx, h, dy, da, db, sact, dg = _ffn_bwd(s["x0"], dx, g1, *s["ab1"], *ffn1)
            grads["ffn1_w_gate"][l] = _wgrad(h, da, "wgrad_gate")
            grads["ffn1_w_up"][l] = _wgrad(h, db, "wgrad_up")
            grads["ffn1_w_down"][l] = _wgrad(sact, dy, "wgrad_down")
        else:
            scatter = lambda n: (_ScatterChips, _grad_parts(grads, [(n, 0)]))
            dx, h, dy, da, db, sact, dg, recv[("w_in", 0)] = _ffn_bwd(s["x0"], dx, g1, *s["ab1"], *ffn1,
                                                                      hosted=scatter("w_in"))
            grads["ffn1_w_gate"][0] = _wgrad(h, da, "wgrad_gate")
            grads["ffn1_w_up"][0], recv[("ffn1_w_gate", 0)] = _wgrad(h, db, "wgrad_up", hosted=scatter("ffn1_w_gate"))
            grads["ffn1_w_down"][0], recv[("ffn1_w_up", 0)] = _wgrad(sact, dy, "wgrad_down", hosted=scatter("ffn1_w_up"))
            recv[("ffn1_w_down", 0)] = _exchange(*scatter("ffn1_w_down"), "scatter_last_grad")[0]
        grads["ffn1_norm"][l] = dg[0]
    grads = {n: (jnp.stack(g) if isinstance(g, list) and n not in BIG else g) for n, g in grads.items()}
    return loss, dx, grads, recv


NAMES = ("ffn1_norm", "ffn1_w_gate", "ffn1_w_up", "ffn1_w_down", "mix_norm", "w_in", "pool_w", "pool_scale",
         "forget_bias", "conv_w", "conv_b", "conv_ln_g", "conv_ln_b", "w_out", "ffn2_norm", "ffn2_w_gate",
         "ffn2_w_up", "ffn2_w_down", "final_norm")


def kernel(x, ffn1_norm, ffn1_w_gate, ffn1_w_up, ffn1_w_down, mix_norm, w_in, pool_w, pool_scale, forget_bias, conv_w, conv_b, conv_ln_g, conv_ln_b, w_out, ffn2_norm, ffn2_w_gate, ffn2_w_up, ffn2_w_down, final_norm, loss_target, m_ffn1_norm, m_ffn1_w_gate, m_ffn1_w_up, m_ffn1_w_down, m_mix_norm, m_w_in, m_pool_w, m_pool_scale, m_forget_bias, m_conv_w, m_conv_b, m_conv_ln_g, m_conv_ln_b, m_w_out, m_ffn2_norm, m_ffn2_w_gate, m_ffn2_w_up, m_ffn2_w_down, m_final_norm, v_ffn1_norm, v_ffn1_w_gate, v_ffn1_w_up, v_ffn1_w_down, v_mix_norm, v_w_in, v_pool_w, v_pool_scale, v_forget_bias, v_conv_w, v_conv_b, v_conv_ln_g, v_conv_ln_b, v_w_out, v_ffn2_norm, v_ffn2_w_gate, v_ffn2_w_up, v_ffn2_w_down, v_final_norm):
    args = (ffn1_norm, ffn1_w_gate, ffn1_w_up, ffn1_w_down, mix_norm, w_in, pool_w, pool_scale, forget_bias, conv_w, conv_b, conv_ln_g, conv_ln_b, w_out, ffn2_norm, ffn2_w_gate, ffn2_w_up, ffn2_w_down, final_norm)
    ms = (m_ffn1_norm, m_ffn1_w_gate, m_ffn1_w_up, m_ffn1_w_down, m_mix_norm, m_w_in, m_pool_w, m_pool_scale, m_forget_bias, m_conv_w, m_conv_b, m_conv_ln_g, m_conv_ln_b, m_w_out, m_ffn2_norm, m_ffn2_w_gate, m_ffn2_w_up, m_ffn2_w_down, m_final_norm)
    vs = (v_ffn1_norm, v_ffn1_w_gate, v_ffn1_w_up, v_ffn1_w_down, v_mix_norm, v_w_in, v_pool_w, v_pool_scale, v_forget_bias, v_conv_w, v_conv_b, v_conv_ln_g, v_conv_ln_b, v_w_out, v_ffn2_norm, v_ffn2_w_gate, v_ffn2_w_up, v_ffn2_w_down, v_final_norm)
    P = dict(zip(NAMES, args))
    M = dict(zip(NAMES, ms))
    V = dict(zip(NAMES, vs))
    xi, yi, _ = _position()
    chip = 2 * xi + yi

    W = {n: P[n] for n in SMALL}
    W.update({n: [None] * P[n].shape[0] for n in BIG})
    shards = {n: [P[n][l].astype(MM) for l in range(P[n].shape[0])] for n in BIG}
    shards["conv_w"] = P["conv_w"]
    for (n, l), sh in zip(FIRST, _exchange(_GatherChipsSplit, [shards[n][l] for n, l in FIRST], "gather_first_weights")):
        W[n][l] = _from_shards(n, sh)

    loss_part, dx, G, recv = _forward_backward(x[0], loss_target[0], W, shards)
    loss = lax.psum(loss_part[0, 0], ("x", "y", "c"))

    small_shapes = [P[n].shape for n in SMALL] + [G["conv_w"].shape]
    small_parts = _gather_all(_pack_small([G[n] for n in SMALL] + [G["conv_w"]]), "gather_small_grads")
    small_sum = _sum8(small_parts, "sum_small_grads")
    nsmall = sum(-(-math.prod(s) // LANES) for s in small_shapes[:-1])
    nsmall_pad = nsmall + (-nsmall % 8)
    w_s, m_s, v_s = (_pack_small([D[n] for n in SMALL]) for D in (P, M, V))
    outs_small = _adamw(w_s, [small_sum[:nsmall_pad]], m_s, v_s, "adamw_small")
    res = {}
    for kind, buf in zip(("g", "d", "m", "v"), outs_small):
        for n, a in zip(SMALL, _unpack_small(buf, small_shapes[:-1])):
            res[(kind, n)] = a
    g_cw_full = _unpack_small(small_sum[nsmall:], [small_shapes[-1]])[0]
    g_cw = lax.dynamic_slice_in_dim(g_cw_full, chip * 64, 64, axis=2)
    outs_cw = _adamw(_pack_small([P["conv_w"]]), [_pack_small([g_cw])], _pack_small([M["conv_w"]]),
                     _pack_small([V["conv_w"]]), "adamw_conv_w")
    for kind, buf in zip(("g", "d", "m", "v"), outs_cw):
        res[(kind, "conv_w")] = _unpack_small(buf, [P["conv_w"].shape])[0]

    parts =[_sum_parts([_own_shard(n, G[n][l], chip) for l in range(P[n].shape[0])],
                        [recv[(n, l)] for l in range(P[n].shape[0])], "sum_" + n) for n in BIG]
    others = _exchange(_SwapCores, parts, "swap_core_grads")
    for n, ga, gb in zip(BIG, parts, others):
        shp = P[n].shape
        two_d = (shp[0] * shp[1], shp[2])
        outs = _adamw(P[n].reshape(two_d), [ga.reshape(two_d), gb.reshape(two_d)], M[n].reshape(two_d),
                      V[n].reshape(two_d), "adamw_" + n)
        for kind, a in zip(("g", "d", "m", "v"), outs):
            res[(kind, n)] = a.reshape(shp)

    return (loss, dx[None], *[res[("g", n)] for n in NAMES], *[res[("d", n)] for n in NAMES],
            *[res[("m", n)] for n in NAMES], *[res[("v", n)] for n in NAMES])
```

```python
import functools
import math

import jax
import jax.numpy as jnp
from jax import lax
from jax.experimental import pallas as pl
from jax.experimental.pallas import tpu as pltpu

F32 = jnp.float32
MM = jnp.bfloat16
NORM_EPS = 1e-6
HEADS = 8
HEAD_DIM = 64
POOL_WINDOWS = (2, 4, 8, 16)
CONV_K = 31
LANES = 128
VMEM_LIMIT = 56 * 2**20
FFN_BWD_ROWS = 256
FFN_COLS = 768

ADAM_LR = 0.001
ADAM_B1 = 0.9
ADAM_B2 = 0.999
ADAM_EPS = 1e-08
ADAM_WD = 0.01
ADAM_STEP = 10

MESH = pl.DeviceIdType.MESH
BS = pl.BlockSpec
SDS = jax.ShapeDtypeStruct
ANY = pl.BlockSpec(memory_space=pl.ANY)


def _dot(a, b):
    return jnp.dot(a, b, preferred_element_type=F32)


def _dot_nt(a, b):
    return lax.dot_general(a, b, (((1,), (1,)), ((), ())), preferred_element_type=F32)


def _dot_tn(a, b):
    return lax.dot_general(a, b, (((0,), (0,)), ((), ())), preferred_element_type=F32)


def _pc(body, name, grid, in_specs, out_specs, out_shape, scratch=()):
    return pl.pallas_call(
        body, out_shape=out_shape, grid=grid, in_specs=in_specs, out_specs=out_specs,
        scratch_shapes=list(scratch), name=name,
        compiler_params=pltpu.CompilerParams(
            dimension_semantics=("arbitrary",) * len(grid), vmem_limit_bytes=VMEM_LIMIT))


def _rms_fwd(x, g):
    r = lax.rsqrt(jnp.mean(x * x, axis=-1, keepdims=True) + NORM_EPS)
    xh = x * r
    return xh, r, xh * g


def _rms_bwd(dh, xh, r, g):
    dxh = dh * g
    dx = r * (dxh - xh * jnp.mean(dxh * xh, axis=-1, keepdims=True))
    return dx, jnp.sum(dh * xh, axis=0, keepdims=True)


def _sigmoid(x):
    return jax.nn.sigmoid(x)


def _ffn_fwd(x, g, wg, wu, wd, hosted=None):
    T, D = x.shape
    F = wg.shape[1]
    tm = min(512, T)
    nt = T // tm
    pieces = [(c0, min(FFN_COLS, F - c0)) for c0 in range(0, F, FFN_COLS)]
    h_in, h_out, h_shape, h_scratch = _hosted_specs(hosted)

    def body(*refs):
        i = pl.program_id(0)
        refs, finish = _hosted_edges(hosted, refs, 5, 3, i == 0, i == nt - 1)
        x_ref, g_ref, wg_ref, wu_ref, wd_ref, o_ref, a_ref, b_ref = refs
        xv = x_ref[...]
        h = _rms_fwd(xv, g_ref[...])[2].astype(MM)
        acc = jnp.zeros((tm, D), F32)
        for c0, w in pieces:
            a = _dot(h, wg_ref[:, c0:c0 + w])
            b = _dot(h, wu_ref[:, c0:c0 + w])
            a_ref[:, c0:c0 + w] = a.astype(a_ref.dtype)
            b_ref[:, c0:c0 + w] = b.astype(b_ref.dtype)
            acc = acc + _dot(((a * _sigmoid(a)) * b).astype(MM), wd_ref[c0:c0 + w, :])
        o_ref[...] = xv + 0.5 * acc
        finish()

    tok = lambda i: (i, 0)
    par = lambda i: (0, 0)
    resident = lambda shape: BS(shape, par, pipeline_mode=pl.Buffered(1))
    return _pc(
        body, "ffn_fwd" + ("_hosting" if hosted else ""), (nt,),
        [BS((tm, D), tok), BS((1, D), par), resident((D, F)), resident((D, F)), resident((F, D))] + h_in,
        [BS((tm, D), tok), BS((tm, F), tok), BS((tm, F), tok)] + h_out,
        [SDS((T, D), F32), SDS((T, F), MM), SDS((T, F), MM)] + h_shape,
        scratch=h_scratch)(x, g, wg, wu, wd, *(hosted[1] if hosted else []))


def _ffn_bwd(x, dout, g, a, b, wg, wu, wd, hosted=None):
    T, D = x.shape
    F = wg.shape[1]
    tm = min(FFN_BWD_ROWS, T)
    nt = T // tm
    pieces = [(c0, min(FFN_COLS, F - c0)) for c0 in range(0, F, FFN_COLS)]
    h_in, h_out, h_shape, h_scratch = _hosted_specs(hosted)

    def body(*refs):
        i = pl.program_id(0)
        refs, finish = _hosted_edges(hosted, refs, 8, 7, i == 0, i == nt - 1)
        (x_ref, do_ref, g_ref, a_ref, b_ref, wg_ref, wu_ref, wd_ref,
         dx_ref, h_ref, dy_ref, da_ref, db_ref, s_ref, dg_ref) = refs

        @pl.when(i == 0)
        def _():
            dg_ref[...] = jnp.zeros_like(dg_ref)

        gv = g_ref[...]
        xh, r, hg = _rms_fwd(x_ref[...], gv)
        h_ref[...] = hg.astype(h_ref.dtype)
        dy = (0.5 * do_ref[...]).astype(MM)
        dy_ref[...] = dy
        dh = jnp.zeros((tm, D), F32)
        for c0, w in pieces:
            a = a_ref[:, c0:c0 + w].astype(F32)
            b = b_ref[:, c0:c0 + w].astype(F32)
            ds = _dot_nt(dy, wd_ref[c0:c0 + w, :])
            sig = _sigmoid(a)
            sl = a * sig
            s_ref[:, c0:c0 + w] = (sl * b).astype(s_ref.dtype)
            db = (ds * sl).astype(MM)
            da = (ds * b * (sig * (1.0 + a * (1.0 - sig)))).astype(MM)
            da_ref[:, c0:c0 + w] = da
            db_ref[:, c0:c0 + w] = db
            dh = dh + _dot_nt(da, wg_ref[:, c0:c0 + w]) + _dot_nt(db, wu_ref[:, c0:c0 + w])
        dx, dg = _rms_bwd(dh, xh, r, gv)
        dx_ref[...] = do_ref[...] + dx
        dg_ref[...] += dg
        finish()

    tok = lambda i: (i, 0)
    par = lambda i: (0, 0)
    hid = BS((tm, F), tok)
    resident = lambda shape: BS(shape, par, pipeline_mode=pl.Buffered(1))
    return _pc(
        body, "ffn_bwd" + ("_hosting" if hosted else ""), (nt,),
        [BS((tm, D), tok), BS((tm, D), tok), BS((1, D), par), hid, hid,
         resident((D, F)), resident((D, F)), resident((F, D))] + h_in,
        [BS((tm, D), tok), BS((tm, D), tok), BS((tm, D), tok), hid, hid, hid, BS((1, D), par)] + h_out,
        [SDS((T, D), F32), SDS((T, D), MM), SDS((T, D), MM),
         SDS((T, F), MM), SDS((T, F), MM), SDS((T, F), MM), SDS((1, D), F32)] + h_shape,
        scratch=h_scratch,
    )(x, dout, g, a, b, wg, wu, wd, *(hosted[1] if hosted else []))


def _wgrad(a, b, name, hosted=None):
    T, K = a.shape
    N = b.shape[1]
    tt = min(512, T)
    tn = N
    for cand in (1408, 1280, 1024, 512, 256, 128):
        if N % cand == 0 and K * cand * 4 <= 6 * 2**20:
            tn = cand
            break
    nn, nt = N // tn, T // tt
    h_in, h_out, h_shape, h_scratch = _hosted_specs(hosted)

    def body(*refs):
        n, t = pl.program_id(0), pl.program_id(1)
        refs, finish = _hosted_edges(hosted, refs, 2, 1, (n == 0) & (t == 0), (n == nn - 1) & (t == nt - 1))
        a_ref, b_ref, o_ref = refs

        @pl.when(t == 0)
        def _():
            o_ref[...] = jnp.zeros_like(o_ref)

        o_ref[...] += _dot_tn(a_ref[...].astype(MM), b_ref[...].astype(MM))
        finish()

    res = _pc(
        body, name + ("_hosting" if hosted else ""), (nn, nt),
        [BS((tt, K), lambda n, t: (t, 0)), BS((tt, tn), lambda n, t: (t, n))] + h_in,
        [BS((K, tn), lambda n, t: (0, n))] + h_out, [SDS((K, N), F32)] + h_shape,
        scratch=h_scratch)(a, b, *(hosted[1] if hosted else []))
    return res if hosted else res[0]


C_POOL, C_Q, C_K, C_V, C_CA, C_CG, C_ZF, C_END = 0, 256, 768, 1280, 1792, 2048, 2304, 2560


def _mix_in_fwd(x, g, w):
    T, D = x.shape
    tm = min(512, T)

    def body(x_ref, g_ref, w_ref, up_ref, q_ref, k_ref, v_ref, ca_ref, cg_ref, zf_ref):
        _, _, hg = _rms_fwd(x_ref[...], g_ref[...])
        p = _dot(hg.astype(MM), w_ref[...])
        up_ref[...] = p[:, C_POOL:C_Q]
        q_ref[...] = p[:, C_Q:C_K].astype(q_ref.dtype)
        k_ref[...] = p[:, C_K:C_V].astype(k_ref.dtype)
        v_ref[...] = p[:, C_V:C_CA].astype(v_ref.dtype)
        ca_ref[...] = p[:, C_CA:C_CG]
        cg_ref[...] = p[:, C_CG:C_ZF]
        zf_ref[...] = p[:, C_ZF:C_ZF + LANES]

    tok = lambda i: (i, 0)
    widths = (256, 512, 512, 512, 256, 256, 128)
    dtypes = (F32, MM, MM, MM, F32, F32, F32)
    return _pc(
        body, "mix_in_fwd", (T // tm,),
        [BS((tm, D), tok), BS((1, D), lambda i: (0, 0)), BS((D, C_END), lambda i: (0, 0))],
        [BS((tm, wd), tok) for wd in widths],
        [SDS((T, wd), dt) for wd, dt in zip(widths, dtypes)])(x, g, w)


def _mix_in_bwd(x, dout, g, w, dup, dq, dk, dv, dca, dcg, dzf):
    T, D = x.shape
    tm = min(512, T)

    def body(x_ref, do_ref, g_ref, w_ref, dup_ref, dq_ref, dk_ref, dv_ref, dca_ref, dcg_ref, dzf_ref,
             dx_ref, h_ref, dp_ref, dg_ref):
        @pl.when(pl.program_id(0) == 0)
        def _():
            dg_ref[...] = jnp.zeros_like(dg_ref)

        gv = g_ref[...]
        xh, r, hg = _rms_fwd(x_ref[...], gv)
        h_ref[...] = hg.astype(h_ref.dtype)
        for ref, lo, hi in ((dup_ref, C_POOL, C_Q), (dq_ref, C_Q, C_K), (dk_ref, C_K, C_V), (dv_ref, C_V, C_CA),
                            (dca_ref, C_CA, C_CG), (dcg_ref, C_CG, C_ZF), (dzf_ref, C_ZF, C_ZF + LANES)):
            dp_ref[:, lo:hi] = ref[...].astype(dp_ref.dtype)
        dp_ref[:, C_ZF + LANES:C_END] = jnp.zeros((tm, C_END - C_ZF - LANES), dp_ref.dtype)
        dh = _dot_nt(dp_ref[...], w_ref[...])
        dx, dg = _rms_bwd(dh, xh, r, gv)
        dx_ref[...] = do_ref[...] + dx
        dg_ref[...] += dg

    tok = lambda i: (i, 0)
    widths = (256, 512, 512, 512, 256, 256, 128)
    return _pc(
        body, "mix_in_bwd", (T // tm,),
        [BS((tm, D), tok), BS((tm, D), tok), BS((1, D), lambda i: (0, 0)), BS((D, C_END), lambda i: (0, 0))]
        + [BS((tm, wd), tok) for wd in widths],
        [BS((tm, D), tok), BS((tm, D), tok), BS((tm, C_END), tok), BS((1, D), lambda i: (0, 0))],
        [SDS((T, D), F32), SDS((T, D), MM), SDS((T, C_END), MM), SDS((1, D), F32)],
    )(x, dout, g, w, dup, dq, dk, dv, dca, dcg, dzf)


def _mix_out_fwd(x, ya, yb, yc, wo):
    T, D = x.shape
    tm = min(512, T)

    def body(x_ref, ya_ref, yb_ref, yc_ref, wo_ref, o_ref):
        o_ref[...] = (x_ref[...] + _dot(ya_ref[...].astype(MM), wo_ref[0:256, :])
                      + _dot(yb_ref[...].astype(MM), wo_ref[256:768, :])
                      + _dot(yc_ref[...].astype(MM), wo_ref[768:1024, :]))

    tok = lambda i: (i, 0)
    return _pc(
        body, "mix_out_fwd", (T // tm,),
        [BS((tm, D), tok), BS((tm, 256), tok), BS((tm, 512), tok), BS((tm, 256), tok), BS((D, D), lambda i: (0, 0))],
        BS((tm, D), tok), SDS((T, D), F32))(x, ya, yb, yc, wo)


def _mix_out_bwd(dx, wo):
    T, D = dx.shape
    tm = min(512, T)

    def body(dx_ref, wo_ref, dya_ref, dyb_ref, dyc_ref):
        dy = _dot_nt(dx_ref[...].astype(MM), wo_ref[...])
        dya_ref[...] = dy[:, 0:256]
        dyb_ref[...] = dy[:, 256:768]
        dyc_ref[...] = dy[:, 768:1024]

    tok = lambda i: (i, 0)
    return _pc(
        body, "mix_out_bwd", (T // tm,),
        [BS((tm, D), tok), BS((D, D), lambda i: (0, 0))],
        [BS((tm, 256), tok), BS((tm, 512), tok), BS((tm, 256), tok)],
        [SDS((T, 256), F32), SDS((T, 512), F32), SDS((T, 256), F32)])(dx, wo)


def _fgate_fwd(zf, bias):
    T = zf.shape[0]
    tc = min(256, T)

    def body(z_ref, b_ref, f_ref, carry):
        @pl.when(pl.program_id(0) == 0)
        def _():
            carry[...] = jnp.zeros_like(carry)

        z = z_ref[...] + b_ref[...]
        logf = jnp.minimum(z, 0.0) - jnp.log(1.0 + jnp.exp(-jnp.abs(z)))
        row = lax.broadcasted_iota(jnp.int32, (tc, tc), 0)
        col = lax.broadcasted_iota(jnp.int32, (tc, tc), 1)
        tri = (col <= row).astype(F32)
        f_ref[...] = jnp.dot(tri, logf, precision=lax.Precision.HIGHEST, preferred_element_type=F32) + carry[...]
        carry[...] += jnp.sum(logf, axis=0, keepdims=True)

    return _pc(
        body, "fgate_fwd", (T // tc,),
        [BS((tc, LANES), lambda i: (i, 0)), BS((1, LANES), lambda i: (0, 0))],
        BS((tc, LANES), lambda i: (i, 0)), SDS((T, LANES), F32),
        scratch=[pltpu.VMEM((1, LANES), F32)])(zf, bias)


def _fgate_bwd(zf, bias, dFq, dFk):
    T = zf.shape[0]
    tc = min(256, T)
    n = T // tc
    slabs = dFq.shape[0]

    def body(z_ref, b_ref, dfq_ref, dfk_ref, dz_ref, db_ref, carry):
        @pl.when(pl.program_id(0) == 0)
        def _():
            carry[...] = jnp.zeros_like(carry)
            db_ref[...] = jnp.zeros_like(db_ref)

        df = dfk_ref[...]
        for sl in range(slabs):
            df = df + dfq_ref[sl]
        row = lax.broadcasted_iota(jnp.int32, (tc, tc), 0)
        col = lax.broadcasted_iota(jnp.int32, (tc, tc), 1)
        tri = (col >= row).astype(F32)
        dlogf = jnp.dot(tri, df, precision=lax.Precision.HIGHEST, preferred_element_type=F32) + carry[...]
        carry[...] += jnp.sum(df, axis=0, keepdims=True)
        lane = lax.broadcasted_iota(jnp.int32, (1, LANES), 1)
        dz = jnp.where(lane < HEADS, dlogf * _sigmoid(-(z_ref[...] + b_ref[...])), 0.0)
        dz_ref[...] = dz
        db_ref[...] += jnp.sum(dz, axis=0, keepdims=True)

    rev = lambda i: (n - 1 - i, 0)
    return _pc(
        body, "fgate_bwd", (n,),
        [BS((tc, LANES), rev), BS((1, LANES), lambda i: (0, 0)), BS((slabs, tc, LANES), lambda i: (0, n - 1 - i, 0)),
         BS((tc, LANES), rev)],
        [BS((tc, LANES), rev), BS((1, LANES), lambda i: (0, 0))],
        [SDS((T, LANES), F32), SDS((1, LANES), F32)],
        scratch=[pltpu.VMEM((1, LANES), F32)])(zf, bias, dFq, dFk)


LOG2E = 1.4426950408889634


def _split3(x):
    hi = x.astype(MM)
    r1 = x - hi.astype(F32)
    mid = r1.astype(MM)
    return hi, mid, (r1 - mid.astype(F32)).astype(MM)


def _place(lane, base, cols):
    out = jnp.zeros((cols[0].shape[0], LANES), MM)
    for i, c in enumerate(cols):
        out = jnp.where(lane == base + i, c, out)
    return out


def _head_col(block, lane, h):
    return jnp.sum(jnp.where(lane == h, block, 0.0), axis=-1, keepdims=True)


def _own_lanes(lane, hh):
    return (lane < HEAD_DIM) if hh == 0 else (lane >= HEAD_DIM)


def _attn_k_side(k_ref, f_ref, kb_ref, hp, T, rows, lse_ones, v_ref=None, vb_ref=None):
    lane = lax.broadcasted_iota(jnp.int32, (1, LANES), 1)
    one = jnp.ones((rows, 1), MM)

    def chunk(c, _):
        r0 = pl.multiple_of(c * rows, rows)
        kp = k_ref[pl.ds(r0, rows), :]
        fblk = f_ref[pl.ds(r0, rows), :]
        for hh in range(2):
            hi, mid, lo = _split3(-_head_col(fblk, lane, 2 * hp + hh) * LOG2E)
            cols = [one, one, one, hi, mid, lo] + ([one, one, one] if lse_ones else [])
            bias = _place(lane, HEAD_DIM * (1 - hh), cols)
            kb_ref[hh, pl.ds(r0, rows), :] = jnp.where(_own_lanes(lane, hh), kp, bias)
            if vb_ref is not None:
                vb_ref[hh, pl.ds(r0, rows), :] = jnp.where(_own_lanes(lane, hh), v_ref[pl.ds(r0, rows), :],
                                                           jnp.ones((rows, LANES), MM))
        return 0

    lax.fori_loop(0, T // rows, chunk, 0)


def _attn_q_side(qp, fblk, lane, hp, scale, lse_blk=None):
    qc = qp.astype(F32) * (scale * LOG2E)
    qhi = qc.astype(MM)
    qlo = (qc - qhi.astype(F32)).astype(MM)
    one = jnp.ones((qp.shape[0], 1), MM)
    out = []
    for hh in range(2):
        cols = list(_split3(_head_col(fblk, lane, 2 * hp + hh) * LOG2E)) + [one, one, one]
        if lse_blk is not None:
            cols += list(_split3(-_head_col(lse_blk, lane, 2 * hp + hh)))
        bias = _place(lane, HEAD_DIM * (1 - hh), cols)
        own = _own_lanes(lane, hh)
        out.append(jnp.concatenate([jnp.where(own, qhi, jnp.zeros_like(qhi)), jnp.where(own, qlo, bias)], axis=1))
    return out


def _causal(tq, tk):
    return lax.broadcasted_iota(jnp.int32, (tq, tk), 1) <= lax.broadcasted_iota(jnp.int32, (tq, tk), 0)


def _hosted_specs(hosted):
    if hosted is None:
        return [], [], [], []
    kind, arrays = hosted
    n = len(arrays)
    return [ANY] * n, [ANY] * n, [kind.out_shape(a) for a in arrays], kind.scratch(n)


def _hosted_edges(hosted, refs, n_in, n_out, first, last):
    if hosted is None:
        return refs, lambda: None
    kind, arrays = hosted
    n = len(arrays)
    nsem = len(kind.scratch(n))
    o0 = n_in + n + n_out
    ins, outs, sems = refs[n_in:n_in + n], refs[o0:o0 + n], refs[len(refs) - nsem:]

    @pl.when(first)
    def _():
        kind.start(ins, outs, *sems)

    def finish():
        @pl.when(last)
        def _():
            kind.wait(ins, outs, *sems)

    return refs[:n_in] + refs[n_in + n:o0] + refs[o0 + n:len(refs) - nsem], finish


def _attn_fwd(q, k, v, F, hosted=None):
    T = q.shape[0]
    tq = min(512, T)
    tk = tq
    nq = T // tq
    scale = 1.0 / math.sqrt(HEAD_DIM)
    h_in, h_out, h_shape, h_scratch = _hosted_specs(hosted)

    def body(*refs):
        hp, ib = pl.program_id(0), pl.program_id(1)
        refs, finish = _hosted_edges(hosted, refs, 5, 2, (hp == 0) & (ib == 0), (hp == HEADS // 2 - 1) & (ib == nq - 1))
        q_ref, k_ref, v_ref, fq_ref, f_ref, o_ref, lse_ref, kb_ref, vb_ref = refs
        lane = lax.broadcasted_iota(jnp.int32, (1, LANES), 1)

        @pl.when(ib == 0)
        def _():
            _attn_k_side(k_ref, f_ref, kb_ref, hp, T, tk, False, v_ref, vb_ref)

        qa = _attn_q_side(q_ref[...], fq_ref[...], lane, hp, scale)

        def tile(jb, carry, masked):
            off = pl.multiple_of(jb * tk, tk)
            kp = k_ref[pl.ds(off, tk), :]
            new = []
            for hh in range(2):
                m, acc = carry[hh]
                s = _dot_nt(qa[hh], jnp.concatenate([kp, kb_ref[hh, pl.ds(off, tk), :]], axis=1))
                if masked:
                    s = jnp.where(_causal(tq, tk), s, -jnp.inf)
                m2 = jnp.maximum(m, jnp.max(s, axis=-1, keepdims=True))
                p = jnp.exp2(s - m2)
                new.append((m2, acc * jnp.exp2(m - m2) + _dot(p.astype(MM), vb_ref[hh, pl.ds(off, tk), :])))
            return tuple(new)

        init = tuple((jnp.full((tq, 1), -jnp.inf, F32), jnp.zeros((tq, LANES), F32)) for _ in range(2))
        carry = lax.fori_loop(0, ib, lambda jb, c: tile(jb, c, False), init)
        (m0, a0), (m1, a1) = tile(ib, carry, True)
        l0, l1 = a0[:, HEAD_DIM:HEAD_DIM + 1], a1[:, 0:1]
        o_ref[...] = jnp.where(lane < HEAD_DIM, a0 / l0, a1 / l1)
        lse_ref[...] = jnp.where(lane == 2 * hp, m0 + jnp.log2(l0), jnp.where(lane == 2 * hp + 1, m1 + jnp.log2(l1), 0.0))
        finish()

    blk = lambda h, i: (i, h)
    full = lambda h, i: (0, h)
    return _pc(
        body, "attn_fwd" + ("_hosting" if hosted else ""), (HEADS // 2, nq),
        [BS((tq, LANES), blk), BS((T, LANES), full), BS((T, LANES), full), BS((tq, LANES), lambda h, i: (i, 0)),
         BS((T, LANES), lambda h, i: (0, 0))] + h_in,
        [BS((tq, LANES), blk), BS((None, tq, LANES), lambda h, i: (h, i, 0))] + h_out,
        [SDS((T, HEADS * HEAD_DIM), F32), SDS((HEADS // 2, T, LANES), F32)] + h_shape,
        scratch=[pltpu.VMEM((2, T, LANES), MM)] * 2 + h_scratch)(q, k, v, F, F, *(hosted[1] if hosted else []))


def _attn_bwd(q, k, v, F, o, lse, do, hosted=None):
    T = q.shape[0]
    tq = min(512, T)
    tk = tq
    nq = T // tq
    scale = 1.0 / math.sqrt(HEAD_DIM)
    h_in, h_out, h_shape, h_scratch = _hosted_specs(hosted)

    def body(*refs):
        hp, ib = pl.program_id(0), pl.program_id(1)
        refs, finish = _hosted_edges(hosted, refs, 8, 5, (hp == 0) & (ib == 0), (hp == HEADS // 2 - 1) & (ib == nq - 1))
        (q_ref, k_ref, v_ref, fq_ref, f_ref, o_ref, lse_ref, do_ref,
         dq_ref, dk_ref, dv_ref, dfq_ref, dfk_ref, kb_ref) = refs
        lane = lax.broadcasted_iota(jnp.int32, (1, LANES), 1)

        @pl.when(ib == 0)
        def _():
            _attn_k_side(k_ref, f_ref, kb_ref, hp, T, tk, True)
            dk_ref[...] = jnp.zeros_like(dk_ref)
            dv_ref[...] = jnp.zeros_like(dv_ref)
            dfk_ref[...] = jnp.zeros_like(dfk_ref)

        qp = q_ref[...]
        qa = _attn_q_side(qp, fq_ref[...], lane, hp, scale, lse_ref[...])
        dob = do_ref[...].astype(MM)
        dprod = dob.astype(F32) * o_ref[...]
        qs = (qp.astype(F32) * scale).astype(MM)
        heads = []
        for hh in range(2):
            own = _own_lanes(lane, hh)
            heads.append((jnp.where(own, dob, jnp.zeros_like(dob)), jnp.where(own, qs, jnp.zeros_like(qs)),
                          jnp.sum(jnp.where(own, dprod, 0.0), axis=-1, keepdims=True)))

        def tile(jb, carry, masked):
            off = pl.multiple_of(jb * tk, tk)
            kp = k_ref[pl.ds(off, tk), :]
            vp = v_ref[pl.ds(off, tk), :]
            new = []
            dv_t = jnp.zeros((tk, LANES), F32)
            dk_t = jnp.zeros((tk, LANES), F32)
            for hh in range(2):
                dq, rs = carry[hh]
                dom, qm, delta = heads[hh]
                p = jnp.exp2(_dot_nt(qa[hh], jnp.concatenate([kp, kb_ref[hh, pl.ds(off, tk), :]], axis=1)))
                if masked:
                    p = jnp.where(_causal(tq, tk), p, 0.0)
                ds = p * (_dot_nt(dom, vp) - delta)
                dsb = ds.astype(MM)
                dv_t = dv_t + _dot_tn(p.astype(MM), dom)
                dk_t = dk_t + _dot_tn(dsb, qm)
                dfk_ref[jb, pl.ds(hh, 1), :] -= jnp.sum(ds, axis=0, keepdims=True)
                new.append((dq + _dot(dsb, kp), rs + jnp.sum(ds, axis=-1, keepdims=True)))
            dv_ref[pl.ds(off, tk), :] += dv_t
            dk_ref[pl.ds(off, tk), :] += dk_t
            return tuple(new)

        init = tuple((jnp.zeros((tq, LANES), F32), jnp.zeros((tq, 1), F32)) for _ in range(2))
        carry = lax.fori_loop(0, ib, lambda jb, c: tile(jb, c, False), init)
        (dq0, rs0), (dq1, rs1) = tile(ib, carry, True)
        dq_ref[...] = jnp.where(lane < HEAD_DIM, dq0, dq1) * scale
        dfq_ref[...] = jnp.where(lane == 2 * hp, rs0, jnp.where(lane == 2 * hp + 1, rs1, 0.0))
        finish()

    blk = lambda h, i: (i, h)
    full = lambda h, i: (0, h)
    slab = BS((None, tq, LANES), lambda h, i: (h, i, 0))
    return _pc(
        body, "attn_bwd" + ("_hosting" if hosted else ""), (HEADS // 2, nq),
        [BS((tq, LANES), blk), BS((T, LANES), full), BS((T, LANES), full), BS((tq, LANES), lambda h, i: (i, 0)),
         BS((T, LANES), lambda h, i: (0, 0)), BS((tq, LANES), blk), slab, BS((tq, LANES), blk)] + h_in,
        [BS((tq, LANES), blk), BS((T, LANES), full), BS((T, LANES), full), slab,
         BS((None, nq, 2, tk), lambda h, i: (h, 0, 0, 0))] + h_out,
        [SDS((T, HEADS * HEAD_DIM), F32)] * 3 + [SDS((HEADS // 2, T, LANES), F32), SDS((HEADS // 2, nq, 2, tk), F32)]
        + h_shape,
        scratch=[pltpu.VMEM((2, T, LANES), MM)] + h_scratch,
    )(q, k, v, F, F, o, lse, do, *(hosted[1] if hosted else []))


POOL_HALO = 16
CONV_HALO = 32


def _group_select(lane, v0, v1, v2, v3):
    return jnp.where(lane < 64, v0, jnp.where(lane < 128, v1, jnp.where(lane < 192, v2, v3)))


def _roll_down(x, k):
    return x if k == 0 else pltpu.roll(x, k, 0)


def _roll_up(x, k):
    return x if k == 0 else pltpu.roll(x, x.shape[0] - k, 0)


def _pool_terms(u, u_prev, tile, tm):
    ext = jnp.concatenate([u_prev, u], axis=0)
    s2 = ext + _roll_down(ext, 1)
    s4 = s2 + _roll_down(s2, 2)
    s8 = s4 + _roll_down(s4, 4)
    s16 = s8 + _roll_down(s8, 8)
    lane = lax.broadcasted_iota(jnp.int32, (1, 256), 1)
    ws = _group_select(lane, s2, s4, s8, s16)[POOL_HALO:, :]
    wlen = _group_select(lane, 2.0, 4.0, 8.0, 16.0).astype(F32)
    return ws / _pool_count(tile, tm, tm, wlen) - u


def _pool_count(tile, tm, rows, wlen):
    t = (tile * tm + 1 + lax.broadcasted_iota(jnp.int32, (rows, 1), 0)).astype(F32)
    return jnp.minimum(t, wlen)


def _layer_norm(y, lg, lb):
    mu = jnp.mean(y, axis=-1, keepdims=True)
    yc = y - mu
    rstd = lax.rsqrt(jnp.mean(yc * yc, axis=-1, keepdims=True) + NORM_EPS)
    yh = yc * rstd
    return yh, rstd, yh * lg + lb


def _halo_specs(tm, T, halo, prev):
    per = tm // halo
    if prev:
        return BS((halo, 256), lambda i: (jnp.maximum(i * per - 1, 0), 0))
    return BS((halo, 256), lambda i: (jnp.minimum((i + 1) * per, T // halo - 1), 0))


def _local_fwd(up, ca, cg, bd, pscale, cw, cb, lg, lb):
    T = up.shape[0]
    tm = min(512, T)

    def body(up_ref, uph_ref, ca_ref, cah_ref, cg_ref, cgh_ref, bd_ref, ps_ref, cw_ref, cb_ref, lg_ref, lb_ref,
             ya_ref, yc_ref, u_ref, y_ref):
        i = pl.program_id(0)
        first = i == 0
        pooled = _pool_terms(up_ref[...], jnp.where(first, 0.0, uph_ref[...]), i, tm)
        ya_ref[...] = (_dot(pooled.astype(MM), bd_ref[...]) * ps_ref[...]).astype(ya_ref.dtype)

        u = ca_ref[...] * _sigmoid(cg_ref[...])
        uh = jnp.where(first, 0.0, cah_ref[...] * _sigmoid(cgh_ref[...]))
        ext = jnp.concatenate([uh, u], axis=0)
        y = jnp.zeros((tm, 256), F32) + cb_ref[...]
        for kk in range(CONV_K):
            y = y + cw_ref[kk:kk + 1, :] * _roll_up(ext, CONV_HALO - (CONV_K - 1) + kk)[:tm, :]
        _, _, z = _layer_norm(y, lg_ref[...], lb_ref[...])
        yc_ref[...] = (z * _sigmoid(z)).astype(yc_ref.dtype)
        u_ref[...] = u
        y_ref[...] = y

    tok = lambda i: (i, 0)
    par = lambda i: (0, 0)
    t256 = BS((tm, 256), tok)
    return _pc(
        body, "local_fwd", (T // tm,),
        [t256, _halo_specs(tm, T, POOL_HALO, True), t256, _halo_specs(tm, T, CONV_HALO, True),
         t256, _halo_specs(tm, T, CONV_HALO, True),
         BS((256, 256), par), BS((1, 256), par), BS((32, 256), par), BS((1, 256), par), BS((1, 256), par),
         BS((1, 256), par)],
        [t256, t256, t256, t256],
        [SDS((T, 256), MM), SDS((T, 256), MM), SDS((T, 256), F32), SDS((T, 256), F32)],
    )(up, up, ca, ca, cg, cg, bd, pscale, cw, cb, lg, lb)


def _local_bwd(up, dya, ca, cg, u, y, dyc, bd, pscale, cw, lg, lb):
    T = up.shape[0]
    tm = min(512, T)
    n = T // tm

    def body(up_ref, uph_ref, dya_ref, dyan_ref, ca_ref, cg_ref, u_ref, uh_ref, y_ref, yn_ref, dyc_ref, dycn_ref,
             bd_ref, ps_ref, cw_ref, lg_ref, lb_ref,
             dup_ref, dca_ref, dcg_ref, dbd_ref, dps_ref, dcw_ref, dcb_ref, dlg_ref, dlb_ref):
        i = pl.program_id(0)
        first = i == 0
        last = i == n - 1

        @pl.when(first)
        def _():
            for ref in (dbd_ref, dps_ref, dcw_ref, dcb_ref, dlg_ref, dlb_ref):
                ref[...] = jnp.zeros_like(ref)

        ps = ps_ref[...]
        pooled = _pool_terms(up_ref[...], jnp.where(first, 0.0, uph_ref[...]), i, tm).astype(MM)
        dya_t = dya_ref[...]
        dps_ref[...] += jnp.sum(dya_t * _dot(pooled, bd_ref[...]), axis=0, keepdims=True)
        dm = (jnp.concatenate([dya_t, jnp.where(last, 0.0, dyan_ref[...])], axis=0) * ps).astype(MM)
        dbd_ref[...] += _dot_tn(pooled, dm[:tm, :])
        dpool = _dot_nt(dm, bd_ref[...])
        lane = lax.broadcasted_iota(jnp.int32, (1, 256), 1)
        wlen = _group_select(lane, 2.0, 4.0, 8.0, 16.0).astype(F32)
        e = dpool / _pool_count(i, tm, tm + POOL_HALO, wlen)
        f2 = e + _roll_up(e, 1)
        f4 = f2 + _roll_up(f2, 2)
        f8 = f4 + _roll_up(f4, 4)
        f16 = f8 + _roll_up(f8, 8)
        dup_ref[...] = _group_select(lane, f2, f4, f8, f16)[:tm, :] - dpool[:tm, :]

        lgv = lg_ref[...]
        yext = jnp.concatenate([y_ref[...], yn_ref[...]], axis=0)
        dyc = jnp.concatenate([dyc_ref[...], jnp.where(last, 0.0, dycn_ref[...])], axis=0)
        yh, rstd, z = _layer_norm(yext, lgv, lb_ref[...])
        sig = _sigmoid(z)
        dz = dyc * (sig * (1.0 + z * (1.0 - sig)))
        dlg_ref[...] += jnp.sum((dz * yh)[:tm, :], axis=0, keepdims=True)
        dlb_ref[...] += jnp.sum(dz[:tm, :], axis=0, keepdims=True)
        dyh = dz * lgv
        dy = rstd * (dyh - jnp.mean(dyh, axis=-1, keepdims=True) - yh * jnp.mean(dyh * yh, axis=-1, keepdims=True))
        dy_t = dy[:tm, :]
        dcb_ref[...] += jnp.sum(dy_t, axis=0, keepdims=True)
        uext = jnp.concatenate([jnp.where(first, 0.0, uh_ref[...]), u_ref[...]], axis=0)
        du = jnp.zeros((tm, 256), F32)
        for kk in range(CONV_K):
            shifted = _roll_up(uext, CONV_HALO - (CONV_K - 1) + kk)[:tm, :]
            dcw_ref[kk:kk + 1, :] += jnp.sum(dy_t * shifted, axis=0, keepdims=True)
            du = du + cw_ref[kk:kk + 1, :] * _roll_up(dy, CONV_K - 1 - kk)[:tm, :]
        sg = _sigmoid(cg_ref[...])
        dca_ref[...] = du * sg
        dcg_ref[...] = du * ca_ref[...] * sg * (1.0 - sg)

    tok = lambda i: (i, 0)
    par = lambda i: (0, 0)
    t256 = BS((tm, 256), tok)
    p1 = BS((1, 256), par)
    return _pc(
        body, "local_bwd", (n,),
        [t256, _halo_specs(tm, T, POOL_HALO, True), t256, _halo_specs(tm, T, POOL_HALO, False), t256, t256,
         t256, _halo_specs(tm, T, CONV_HALO, True), t256, _halo_specs(tm, T, CONV_HALO, False),
         t256, _halo_specs(tm, T, CONV_HALO, False),
         BS((256, 256), par), p1, BS((32, 256), par), p1, p1],
        [t256, t256, t256, BS((256, 256), par), p1, BS((32, 256), par), p1, p1, p1],
        [SDS((T, 256), F32)] * 3 + [SDS((256, 256), F32), SDS((1, 256), F32), SDS((32, 256), F32)]
        + [SDS((1, 256), F32)] * 3,
    )(up, up, dya, dya, ca, cg, u, u, y, y, dyc, dyc, bd, pscale, cw, lg, lb)


def _head(x, g, target):
    T, D = x.shape
    tm = min(512, T)

    def body(x_ref, g_ref, t_ref, loss_ref, dx_ref, dg_ref):
        @pl.when(pl.program_id(0) == 0)
        def _():
            loss_ref[...] = jnp.zeros_like(loss_ref)
            dg_ref[...] = jnp.zeros_like(dg_ref)

        gv = g_ref[...]
        xh, r, yv = _rms_fwd(x_ref[...], gv)
        err = yv - t_ref[...]
        loss_ref[...] += 0.5 * jnp.sum(jnp.mean(err * err, axis=-1, keepdims=True), axis=0, keepdims=True)
        dx, dg = _rms_bwd(err * (1.0 / D), xh, r, gv)
        dx_ref[...] = dx
        dg_ref[...] += dg

    tok = lambda i: (i, 0)
    par = lambda i: (0, 0)
    return _pc(
        body, "head", (T // tm,),
        [BS((tm, D), tok), BS((1, D), par), BS((tm, D), tok)],
        [BS((1, LANES), par), BS((tm, D), tok), BS((1, D), par)],
        [SDS((1, LANES), F32), SDS((T, D), F32), SDS((1, D), F32)])(x, g, target)


def _adamw(w, gs, m, v, name):
    R, C = w.shape
    tr = R
    for cand in (512, 256, 128, 64, 32, 16, 8):
        if R % cand == 0:
            tr = cand
            break
    ng = len(gs)

    def body(*refs):
        w_ref, g_refs, m_ref, v_ref = refs[0], refs[1:1 + ng], refs[1 + ng], refs[2 + ng]
        g_ref, d_ref, m2_ref, v2_ref = refs[3 + ng:]
        g = g_refs[0][...]
        for r in g_refs[1:]:
            g = g + r[...]
        m2 = ADAM_B1 * m_ref[...] + (1.0 - ADAM_B1) * g
        v2 = ADAM_B2 * v_ref[...] + (1.0 - ADAM_B2) * jnp.square(g)
        m_hat = m2 / (1.0 - ADAM_B1 ** ADAM_STEP)
        v_hat = v2 / (1.0 - ADAM_B2 ** ADAM_STEP)
        g_ref[...] = g
        d_ref[...] = -ADAM_LR * (m_hat / (jnp.sqrt(v_hat) + ADAM_EPS) + ADAM_WD * w_ref[...])
        m2_ref[...] = m2
        v2_ref[...] = v2

    blk = BS((tr, C), lambda i: (i, 0))
    return _pc(body, name, (R // tr,), [blk] * (3 + ng), [blk] * 4, [SDS((R, C), F32)] * 4)(w, *gs, m, v)


def _sum_parts(owns, recvs, name):
    L = len(owns)
    R, C = owns[0].shape
    tr = next(t for t in (512, 256, 128, 64, 32, 16) if R % t == 0)

    def body(*refs):
        l = pl.program_id(0)
        s_ref = refs[2 * L]
        for ll in range(L):
            @pl.when(l == ll)
            def _(o_ref=refs[ll], r_ref=refs[L + ll]):
                s_ref[...] = ((o_ref[...] + r_ref[0].astype(F32)) + r_ref[1].astype(F32)) + r_ref[2].astype(F32)

    own_specs = [BS((tr, C), lambda l, i, ll=ll: (jnp.where(l == ll, i, 0), 0)) for ll in range(L)]
    recv_specs = [BS((3, tr, C), lambda l, i, ll=ll: (0, jnp.where(l == ll, i, 0), 0)) for ll in range(L)]
    return _pc(body, name, (L, R // tr), own_specs + recv_specs,
               BS((None, tr, C), lambda l, i: (l, i, 0)), SDS((L, R, C), F32))(*owns, *recvs)


def _sum8(parts, name):
    _, R, C = parts.shape

    def body(p_ref, s_ref):
        acc = p_ref[0]
        for d in range(1, 8):
            acc = acc + p_ref[d]
        s_ref[...] = acc

    return _pc(body, name, (1,), [BS((8, R, C), lambda i: (0, 0, 0))], BS((R, C), lambda i: (0, 0)),
               SDS((R, C), F32))(parts)


def _position():
    return lax.axis_index("x"), lax.axis_index("y"), lax.axis_index("c")


CHIP_FLIPS = ((1, 0), (0, 1), (1, 1))


class _GatherChips:
    @staticmethod
    def scratch(n):
        return [pltpu.SemaphoreType.DMA((3 * n,)), pltpu.SemaphoreType.DMA((3 * n,)), pltpu.SemaphoreType.DMA((n,))]

    @staticmethod
    def out_shape(block):
        return SDS((4,) + tuple(block.shape), block.dtype)

    @staticmethod
    def _copies(ins, outs, send_sems, recv_sems, local_sems, arrivals):
        x, y, c = _position()
        local, remote = [], []
        for i, (in_ref, out_ref) in enumerate(zip(ins, outs)):
            local.append(pltpu.make_async_copy(in_ref, out_ref.at[2 * x + y], local_sems.at[i]))
            for k, (fx, fy) in enumerate(CHIP_FLIPS):
                slot = 2 * (x ^ fx) + (y ^ fy) if arrivals else 2 * x + y
                remote.append(pltpu.make_async_remote_copy(
                    src_ref=in_ref, dst_ref=out_ref.at[slot], send_sem=send_sems.at[3 * i + k],
                    recv_sem=recv_sems.at[3 * i + k], device_id=(x ^ fx, y ^ fy, c), device_id_type=MESH))
        return local, remote

    @classmethod
    def start(cls, ins, outs, *sems):
        local, sends = cls._copies(ins, outs, *sems, arrivals=False)
        for cp in local + sends:
            cp.start()

    @classmethod
    def wait(cls, ins, outs, *sems):
        local, arrivals = cls._copies(ins, outs, *sems, arrivals=True)
        for cp in arrivals:
            cp.wait_recv()
        for cp in arrivals:
            cp.wait_send()
        for cp in local:
            cp.wait()


class _GatherChipsSplit(_GatherChips):
    @staticmethod
    def scratch(n):
        return [pltpu.SemaphoreType.DMA((6 * n,)), pltpu.SemaphoreType.DMA((6 * n,)), pltpu.SemaphoreType.DMA((n,))]

    @staticmethod
    def _half(ref, which):
        rows = ref.shape[0] // 2
        return ref.at[pl.ds(pl.multiple_of(which * rows, 16), rows)]

    @staticmethod
    def _local(ins, outs, local_sems):
        x, y, _ = _position()
        return [pltpu.make_async_copy(in_ref, out_ref.at[2 * x + y], local_sems.at[i])
                for i, (in_ref, out_ref) in enumerate(zip(ins, outs))]

    @classmethod
    def _between_chips(cls, ins, outs, send_sems, recv_sems, arrivals):
        x, y, c = _position()
        return [
            pltpu.make_async_remote_copy(
                src_ref=cls._half(in_ref, c),
                dst_ref=cls._half(out_ref.at[2 * (x ^ fx) + (y ^ fy) if arrivals else 2 * x + y], c),
                send_sem=send_sems.at[6 * i + k], recv_sem=recv_sems.at[6 * i + k],
                device_id=(x ^ fx, y ^ fy, c), device_id_type=MESH)
            for i, (in_ref, out_ref) in enumerate(zip(ins, outs)) for k, (fx, fy) in enumerate(CHIP_FLIPS)]

    @classmethod
    def _between_cores(cls, outs, send_sems, recv_sems, arrivals):
        x, y, c = _position()
        copies = []
        for i, out_ref in enumerate(outs):
            for k, (fx, fy) in enumerate(CHIP_FLIPS):
                half = cls._half(out_ref.at[2 * (x ^ fx) + (y ^ fy)], 1 - c if arrivals else c)
                copies.append(pltpu.make_async_remote_copy(
                    src_ref=half, dst_ref=half, send_sem=send_sems.at[6 * i + 3 + k],
                    recv_sem=recv_sems.at[6 * i + 3 + k], device_id=(x, y, 1 - c), device_id_type=MESH))
        return copies

    @classmethod
    def start(cls, ins, outs, send_sems, recv_sems, local_sems):
        for cp in cls._local(ins, outs, local_sems) + cls._between_chips(ins, outs, send_sems, recv_sems, False):
            cp.start()

    @classmethod
    def wait(cls, ins, outs, send_sems, recv_sems, local_sems):
        arrivals = cls._between_chips(ins, outs, send_sems, recv_sems, True)
        onward = cls._between_cores(outs, send_sems, recv_sems, False)
        for cp, nxt in zip(arrivals, onward):
            cp.wait_recv()
            nxt.start()
        for cp in cls._between_cores(outs, send_sems, recv_sems, True):
            cp.wait_recv()
        for cp in arrivals + onward:
            cp.wait_send()
        for cp in cls._local(ins, outs, local_sems):
            cp.wait()


class _Symmetric:
    @classmethod
    def start(cls, ins, outs, *sems):
        for cp in cls._copies(ins, outs, *sems):
            cp.start()

    @classmethod
    def wait(cls, ins, outs, *sems):
        copies = cls._copies(ins, outs, *sems)
        for cp in copies:
            cp.wait_recv()
        for cp in copies:
            cp.wait_send()


class _ScatterChips(_Symmetric):
    @staticmethod
    def scratch(n):
        return [pltpu.SemaphoreType.DMA((3 * n,)), pltpu.SemaphoreType.DMA((3 * n,))]

    @staticmethod
    def out_shape(parts):
        return SDS((3,) + tuple(parts.shape[1:]), parts.dtype)

    @staticmethod
    def _copies(ins, outs, send_sems, recv_sems):
        x, y, c = _position()
        return [
            pltpu.make_async_remote_copy(
                src_ref=in_ref.at[2 * (x ^ fx) + (y ^ fy)], dst_ref=out_ref.at[k],
                send_sem=send_sems.at[3 * i + k], recv_sem=recv_sems.at[3 * i + k],
                device_id=(x ^ fx, y ^ fy, c), device_id_type=MESH)
            for i, (in_ref, out_ref) in enumerate(zip(ins, outs)) for k, (fx, fy) in enumerate(CHIP_FLIPS)]


class _SwapCores(_Symmetric):
    @staticmethod
    def scratch(n):
        return [pltpu.SemaphoreType.DMA((n,)), pltpu.SemaphoreType.DMA((n,))]

    @staticmethod
    def out_shape(block):
        return SDS(block.shape, block.dtype)

    @staticmethod
    def _copies(ins, outs, send_sems, recv_sems):
        x, y, c = _position()
        return [
            pltpu.make_async_remote_copy(
                src_ref=in_ref, dst_ref=out_ref, send_sem=send_sems.at[i], recv_sem=recv_sems.at[i],
                device_id=(x, y, 1 - c), device_id_type=MESH)
            for i, (in_ref, out_ref) in enumerate(zip(ins, outs))]


def _exchange(kind, arrays, name):
    n = len(arrays)

    def body(*refs):
        ins, outs, sems = refs[:n], refs[n:2 * n], refs[2 * n:]
        kind.start(ins, outs, *sems)
        kind.wait(ins, outs, *sems)

    return pl.pallas_call(body, out_shape=[kind.out_shape(a) for a in arrays], in_specs=[ANY] * n,
                          out_specs=[ANY] * n, name=name, scratch_shapes=kind.scratch(n))(*arrays)


def _gather_all(block, name):
    R, C = block.shape
    flips = [(fx, fy, fc) for fx in (0, 1) for fy in (0, 1) for fc in (0, 1)][1:]

    def body(in_ref, out_ref, send_sems, recv_sems, local_sem):
        x, y, c = _position()
        mine = out_ref.at[4 * x + 2 * y + c]
        local = pltpu.make_async_copy(in_ref, mine, local_sem)
        local.start()
        copies = [
            pltpu.make_async_remote_copy(
                src_ref=in_ref, dst_ref=mine, send_sem=send_sems.at[k], recv_sem=recv_sems.at[k],
                device_id=(x ^ fx, y ^ fy, c ^ fc), device_id_type=MESH)
            for k, (fx, fy, fc) in enumerate(flips)]
        for cp in copies:
            cp.start()
        for k, (fx, fy, fc) in enumerate(flips):
            theirs = out_ref.at[4 * (x ^ fx) + 2 * (y ^ fy) + (c ^ fc)]
            pltpu.make_async_remote_copy(
                src_ref=in_ref, dst_ref=theirs, send_sem=send_sems.at[k], recv_sem=recv_sems.at[k],
                device_id=(x ^ fx, y ^ fy, c ^ fc), device_id_type=MESH).wait_recv()
        for cp in copies:
            cp.wait_send()
        local.wait()

    return pl.pallas_call(
        body, out_shape=SDS((8, R, C), block.dtype), in_specs=[ANY], out_specs=ANY, name=name,
        scratch_shapes=[pltpu.SemaphoreType.DMA((7,)), pltpu.SemaphoreType.DMA((7,)), pltpu.SemaphoreType.DMA(())])(block)


BIG = ("ffn1_w_gate", "ffn1_w_up", "ffn1_w_down", "w_in", "w_out", "ffn2_w_gate", "ffn2_w_up", "ffn2_w_down")
COL_SHARDED = ("ffn1_w_gate", "ffn1_w_up", "w_in", "ffn2_w_gate", "ffn2_w_up")
FIRST = tuple((n, 0) for n in ("ffn1_w_gate", "ffn1_w_up", "ffn1_w_down"))
LATE = tuple((n, 0) for n in ("w_out", "ffn2_w_gate", "ffn2_w_up", "ffn2_w_down")) + tuple((n, 1) for n in BIG)


def _to_shards(name, full):
    r, c = full.shape
    if name in COL_SHARDED:
        return full.reshape(r, 4, c // 4).transpose(1, 0, 2)
    return full.reshape(4, r // 4, c)


def _own_shard(name, full, chip):
    r, c = full.shape
    if name in COL_SHARDED:
        return lax.dynamic_slice_in_dim(full, chip * (c // 4), c // 4, axis=1)
    return lax.dynamic_slice_in_dim(full, chip * (r // 4), r // 4, axis=0)


def _from_shards(name, sh):
    _, r, c = sh.shape
    if name in COL_SHARDED:
        return sh.transpose(1, 0, 2).reshape(r, 4 * c)
    return sh.reshape(4 * r, c)


def _pad_w_in(w):
    return jnp.concatenate([w[:, :1792], w[:, 1800:2312], w[:, 1792:1800], jnp.zeros((w.shape[0], 248), w.dtype)], axis=1)


def _unpad_w_in(g):
    return jnp.concatenate([g[:, :1792], g[:, 2304:2312], g[:, 1792:2304]], axis=1)


def _block_diag(pw):
    out = jnp.zeros((256, 256), pw.dtype)
    for gidx in range(4):
        out = lax.dynamic_update_slice(out, pw[gidx], (64 * gidx, 64 * gidx))
    return out


SMALL = ("ffn1_norm", "mix_norm", "pool_w", "pool_scale", "forget_bias", "conv_b", "conv_ln_g", "conv_ln_b",
         "ffn2_norm", "final_norm")


def _pack_small(arrs):
    rows = []
    for a in arrs:
        flat = a.reshape(-1)
        flat = jnp.pad(flat, (0, -flat.shape[0] % LANES))
        rows.append(flat.reshape(-1, LANES))
    total = sum(r.shape[0] for r in rows)
    if total % 8:
        rows.append(jnp.zeros((-total % 8, LANES), F32))
    return jnp.concatenate(rows, axis=0)


def _unpack_small(buf, shapes):
    out, off = [], 0
    for shp in shapes:
        n = math.prod(shp)
        nr = -(-n // LANES)
        out.append(buf[off:off + nr].reshape(-1)[:n].reshape(shp))
        off += nr
    return out


def _grad_parts(grads, pieces):
    return [_to_shards(n, grads[n][l]).astype(MM) for n, l in pieces]


def _forward_backward(x, target, W, shards=None):
    T = x.shape[0]
    L = W["ffn1_norm"].shape[0]
    saved = []
    recv = {}
    for l in range(L):
        g1, gm, g2 = (W[n][l][None, :] for n in ("ffn1_norm", "mix_norm", "ffn2_norm"))
        first = shards is not None and l == 0
        hosted = (_GatherChips, [shards["w_in"][0], shards["conv_w"]]) if first else None
        x1, a1, b1, *got = _ffn_fwd(x, g1, W["ffn1_w_gate"][l], W["ffn1_w_up"][l], W["ffn1_w_down"][l], hosted=hosted)
        if first:
            W["w_in"][0] = _from_shards("w_in", got[0])
            W["conv_w"] = got[1].transpose(1, 2, 0, 3).reshape(L, CONV_K, 256)
        w_in = _pad_w_in(W["w_in"][l])
        up, q, k, v, ca, cg, zf = _mix_in_fwd(x1, gm, w_in)
        fb = jnp.pad(W["forget_bias"][l], (0, LANES - HEADS))[None, :]
        F = _fgate_fwd(zf, fb)
        if first:
            yb, lse, *got = _attn_fwd(q, k, v, F, hosted=(_GatherChips, [shards[n][ll] for n, ll in LATE]))
            for (n, ll), sh in zip(LATE, got):
                W[n][ll] = _from_shards(n, sh)
        else:
            yb, lse = _attn_fwd(q, k, v, F)
        bd = _block_diag(W["pool_w"][l]).astype(MM)
        ps, cb, lg, lb = (W[n][l][None, :] for n in ("pool_scale", "conv_b", "conv_ln_g", "conv_ln_b"))
        cw = jnp.pad(W["conv_w"][l], ((0, 1), (0, 0)))
        ya, yc, cu, cy = _local_fwd(up, ca, cg, bd, ps, cw, cb, lg, lb)
        x2 = _mix_out_fwd(x1, ya, yb, yc, W["w_out"][l])
        x3, a2, b2 = _ffn_fwd(x2, g2, W["ffn2_w_gate"][l], W["ffn2_w_up"][l], W["ffn2_w_down"][l])
        saved.append(dict(x0=x, x1=x1, x2=x2, ab1=(a1, b1), ab2=(a2, b2), w_in=w_in, up=up, ca=ca, cg=cg, zf=zf, fb=fb, F=F,
                          q=q, k=k, v=v, lse=lse, bd=bd, cw=cw, cu=cu, cy=cy, ya=ya, yb=yb, yc=yc))
        x = x3

    loss, dx, dgf = _head(x, W["final_norm"][None, :], target)
    grads = {n: [None] * L for n in W if n != "final_norm"}
    grads["final_norm"] = dgf[0]
    for l in reversed(range(L)):
        s = saved[l]
        g1, gm, g2 = (W[n][l][None, :] for n in ("ffn1_norm", "mix_norm", "ffn2_norm"))
        ps, lg, lb = (W[n][l][None, :] for n in ("pool_scale", "conv_ln_g", "conv_ln_b"))
        dx, h, dy, da, db, sact, dg = _ffn_bwd(s["x2"], dx, g2, *s["ab2"], W["ffn2_w_gate"][l], W["ffn2_w_up"][l],
                                               W["ffn2_w_down"][l])
        grads["ffn2_norm"][l] = dg[0]
        grads["ffn2_w_gate"][l] = _wgrad(h, da, "wgrad_gate")
        grads["ffn2_w_up"][l] = _wgrad(h, db, "wgrad_up")
        grads["ffn2_w_down"][l] = _wgrad(sact, dy, "wgrad_down")
        dya, dyb, dyc = _mix_out_bwd(dx, W["w_out"][l])
        grads["w_out"][l] = jnp.concatenate(
            [_wgrad(s["ya"], dx, "wgrad_out_a"), _wgrad(s["yb"], dx, "wgrad_out_b"), _wgrad(s["yc"], dx, "wgrad_out_c")], axis=0)
        first = shards is not None and l == 0
        if first:
            dq, dk, dv, dfq, dfk, *got = _attn_bwd(s["q"], s["k"], s["v"], s["F"], s["yb"], s["lse"], dyb,
                                                  hosted=(_ScatterChips, _grad_parts(grads, LATE)))
            recv.update(zip(LATE, got))
        else:
            dq, dk, dv, dfq, dfk = _attn_bwd(s["q"], s["k"], s["v"], s["F"], s["yb"], s["lse"], dyb)
        dfk_cols = jnp.pad(dfk.transpose(0, 2, 1, 3).reshape(HEADS, T).T, ((0, 0), (0, LANES - HEADS)))
        dzf, dfb = _fgate_bwd(s["zf"], s["fb"], dfq, dfk_cols)
        grads["forget_bias"][l] = dfb[0, :HEADS]
        dup, dca, dcg, dbd, dps, dcw, dcb, dlg, dlb = _local_bwd(
            s["up"], dya, s["ca"], s["cg"], s["cu"], s["cy"], dyc, s["bd"], ps, s["cw"], lg, lb)
        grads["pool_w"][l] = jnp.stack([dbd[64 * i:64 * i + 64, 64 * i:64 * i + 64] for i in range(4)])
        grads["pool_scale"][l], grads["conv_b"][l] = dps[0], dcb[0]
        grads["conv_ln_g"][l], grads["conv_ln_b"][l] = dlg[0], dlb[0]
        grads["conv_w"][l] = dcw[:CONV_K]
        dx, h, dp, dg = _mix_in_bwd(s["x1"], dx, gm, s["w_in"], dup, dq, dk, dv, dca, dcg, dzf)
        grads["mix_norm"][l] = dg[0]
        grads["w_in"][l] = _unpad_w_in(_wgrad(h, dp, "wgrad_in"))
        ffn1 = (W["ffn1_w_gate"][l], W["ffn1_w_up"][l], W["ffn1_w_down"][l])
        if not first:
            dx, h, dy, da, db, sact, dg = _ffn_bwd(s["x0"], dx, g1, *s["ab1"], *ffn1)
            grads["ffn1_w_gate"][l] = _wgrad(h, da, "wgrad_gate")
            grads["ffn1_w_up"][l] = _wgrad(h, db, "wgrad_up")
            grads["ffn1_w_down"][l] = _wgrad(sact, dy, "wgrad_down")
        else:
            scatter = lambda n: (_ScatterChips, _grad_parts(grads, [(n, 0)]))
            dx, h, dy, da, db, sact, dg, recv[("w_in", 0)] = _ffn_bwd(s["x0"], dx, g1, *s["ab1"], *ffn1,
                                                                      hosted=scatter("w_in"))
            grads["ffn1_w_gate"][0] = _wgrad(h, da, "wgrad_gate")
            grads["ffn1_w_up"][0], recv[("ffn1_w_gate", 0)] = _wgrad(h, db, "wgrad_up", hosted=scatter("ffn1_w_gate"))
            grads["ffn1_w_down"][0], recv[("ffn1_w_up", 0)] = _wgrad(sact, dy, "wgrad_down", hosted=scatter("ffn1_w_up"))
            recv[("ffn1_w_down", 0)] = _exchange(*scatter("ffn1_w_down"), "scatter_last_grad")[0]
        grads["ffn1_norm"][l] = dg[0]
    grads = {n: (jnp.stack(g) if isinstance(g, list) and n not in BIG else g) for n, g in grads.items()}
    return loss, dx, grads, recv


NAMES = ("ffn1_norm", "ffn1_w_gate", "ffn1_w_up", "ffn1_w_down", "mix_norm", "w_in", "pool_w", "pool_scale",
         "forget_bias", "conv_w", "conv_b", "conv_ln_g", "conv_ln_b", "w_out", "ffn2_norm", "ffn2_w_gate",
         "ffn2_w_up", "ffn2_w_down", "final_norm")


def kernel(x, ffn1_norm, ffn1_w_gate, ffn1_w_up, ffn1_w_down, mix_norm, w_in, pool_w, pool_scale, forget_bias, conv_w, conv_b, conv_ln_g, conv_ln_b, w_out, ffn2_norm, ffn2_w_gate, ffn2_w_up, ffn2_w_down, final_norm, loss_target, m_ffn1_norm, m_ffn1_w_gate, m_ffn1_w_up, m_ffn1_w_down, m_mix_norm, m_w_in, m_pool_w, m_pool_scale, m_forget_bias, m_conv_w, m_conv_b, m_conv_ln_g, m_conv_ln_b, m_w_out, m_ffn2_norm, m_ffn2_w_gate, m_ffn2_w_up, m_ffn2_w_down, m_final_norm, v_ffn1_norm, v_ffn1_w_gate, v_ffn1_w_up, v_ffn1_w_down, v_mix_norm, v_w_in, v_pool_w, v_pool_scale, v_forget_bias, v_conv_w, v_conv_b, v_conv_ln_g, v_conv_ln_b, v_w_out, v_ffn2_norm, v_ffn2_w_gate, v_ffn2_w_up, v_ffn2_w_down, v_final_norm):
    args = (ffn1_norm, ffn1_w_gate, ffn1_w_up, ffn1_w_down, mix_norm, w_in, pool_w, pool_scale, forget_bias, conv_w, conv_b, conv_ln_g, conv_ln_b, w_out, ffn2_norm, ffn2_w_gate, ffn2_w_up, ffn2_w_down, final_norm)
    ms = (m_ffn1_norm, m_ffn1_w_gate, m_ffn1_w_up, m_ffn1_w_down, m_mix_norm, m_w_in, m_pool_w, m_pool_scale, m_forget_bias, m_conv_w, m_conv_b, m_conv_ln_g, m_conv_ln_b, m_w_out, m_ffn2_norm, m_ffn2_w_gate, m_ffn2_w_up, m_ffn2_w_down, m_final_norm)
    vs = (v_ffn1_norm, v_ffn1_w_gate, v_ffn1_w_up, v_ffn1_w_down, v_mix_norm, v_w_in, v_pool_w, v_pool_scale, v_forget_bias, v_conv_w, v_conv_b, v_conv_ln_g, v_conv_ln_b, v_w_out, v_ffn2_norm, v_ffn2_w_gate, v_ffn2_w_up, v_ffn2_w_down, v_final_norm)
    P = dict(zip(NAMES, args))
    M = dict(zip(NAMES, ms))
    V = dict(zip(NAMES, vs))
    xi, yi, _ = _position()
    chip = 2 * xi + yi

    W = {n: P[n] for n in SMALL}
    W.update({n: [None] * P[n].shape[0] for n in BIG})
    shards = {n: [P[n][l].astype(MM) for l in range(P[n].shape[0])] for n in BIG}
    shards["conv_w"] = P["conv_w"]
    for (n, l), sh in zip(FIRST, _exchange(_GatherChipsSplit, [shards[n][l] for n, l in FIRST], "gather_first_weights")):
        W[n][l] = _from_shards(n, sh)

    loss_part, dx, G, recv = _forward_backward(x[0], loss_target[0], W, shards)
    loss = lax.psum(loss_part[0, 0], ("x", "y", "c"))

    small_shapes = [P[n].shape for n in SMALL] + [G["conv_w"].shape]
    small_parts = _gather_all(_pack_small([G[n] for n in SMALL] + [G["conv_w"]]), "gather_small_grads")
    small_sum = _sum8(small_parts, "sum_small_grads")
    nsmall = sum(-(-math.prod(s) // LANES) for s in small_shapes[:-1])
    nsmall_pad = nsmall + (-nsmall % 8)
    w_s, m_s, v_s = (_pack_small([D[n] for n in SMALL]) for D in (P, M, V))
    outs_small = _adamw(w_s, [small_sum[:nsmall_pad]], m_s, v_s, "adamw_small")
    res = {}
    for kind, buf in zip(("g", "d", "m", "v"), outs_small):
        for n, a in zip(SMALL, _unpack_small(buf, small_shapes[:-1])):
            res[(kind, n)] = a
    g_cw_full = _unpack_small(small_sum[nsmall:], [small_shapes[-1]])[0]
    g_cw = lax.dynamic_slice_in_dim(g_cw_full, chip * 64, 64, axis=2)
    outs_cw = _adamw(_pack_small([P["conv_w"]]), [_pack_small([g_cw])], _pack_small([M["conv_w"]]),
                     _pack_small([V["conv_w"]]), "adamw_conv_w")
    for kind, buf in zip(("g", "d", "m", "v"), outs_cw):
        res[(kind, "conv_w")] = _unpack_small(buf, [P["conv_w"].shape])[0]

    parts =[_sum_parts([_own_shard(n, G[n][l], chip) for l in range(P[n].shape[0])],
                        [recv[(n, l)] for l in range(P[n].shape[0])], "sum_" + n) for n in BIG]
    others = _exchange(_SwapCores, parts, "swap_core_grads")
    for n, ga, gb in zip(BIG, parts, others):
        shp = P[n].shape
        two_d = (shp[0] * shp[1], shp[2])
        outs = _adamw(P[n].reshape(two_d), [ga.reshape(two_d), gb.reshape(two_d)], M[n].reshape(two_d),
                      V[n].reshape(two_d), "adamw_" + n)
        for kind, a in zip(("g", "d", "m", "v"), outs):
            res[(kind, n)] = a.reshape(shp)

    return (loss, dx[None], *[res[("g", n)] for n in NAMES], *[res[("d", n)] for n in NAMES],
            *[res[("m", n)] for n in NAMES], *[res[("v", n)] for n in NAMES])
```

```python
import functools
import math

import jax
import jax.numpy as jnp
from jax import lax
from jax.experimental import pallas as pl
from jax.experimental.pallas import tpu as pltpu

F32 = jnp.float32
MM = jnp.bfloat16
NORM_EPS = 1e-6
HEADS = 8
HEAD_DIM = 64
POOL_WINDOWS = (2, 4, 8, 16)
CONV_K = 31
LANES = 128
VMEM_LIMIT = 56 * 2**20
FFN_BWD_ROWS = 256
FFN_COLS = 768
ATTN_FWD_TILE = 1024
WGRAD_COLS = 768
WGRAD_ACC_BYTES = 12 * 2**20

ADAM_LR = 0.001
ADAM_B1 = 0.9
ADAM_B2 = 0.999
ADAM_EPS = 1e-08
ADAM_WD = 0.01
ADAM_STEP = 10

MESH = pl.DeviceIdType.MESH
BS = pl.BlockSpec
SDS = jax.ShapeDtypeStruct
ANY = pl.BlockSpec(memory_space=pl.ANY)


def _dot(a, b):
    return jnp.dot(a, b, preferred_element_type=F32)


def _dot_nt(a, b):
    return lax.dot_general(a, b, (((1,), (1,)), ((), ())), preferred_element_type=F32)


def _dot_tn(a, b):
    return lax.dot_general(a, b, (((0,), (0,)), ((), ())), preferred_element_type=F32)


def _pc(body, name, grid, in_specs, out_specs, out_shape, scratch=()):
    return pl.pallas_call(
        body, out_shape=out_shape, grid=grid, in_specs=in_specs, out_specs=out_specs,
        scratch_shapes=list(scratch), name=name,
        compiler_params=pltpu.CompilerParams(
            dimension_semantics=("arbitrary",) * len(grid), vmem_limit_bytes=VMEM_LIMIT))


def _rms_fwd(x, g):
    r = lax.rsqrt(jnp.mean(x * x, axis=-1, keepdims=True) + NORM_EPS)
    xh = x * r
    return xh, r, xh * g


def _rms_bwd(dh, xh, r, g):
    dxh = dh * g
    dx = r * (dxh - xh * jnp.mean(dxh * xh, axis=-1, keepdims=True))
    return dx, jnp.sum(dh * xh, axis=0, keepdims=True)


def _sigmoid(x):
    return jax.nn.sigmoid(x)


def _ffn_fwd(x, g, wg, wu, wd, hosted=None):
    T, D = x.shape
    F = wg.shape[1]
    tm = min(512, T)
    nt = T // tm
    pieces = [(c0, min(FFN_COLS, F - c0)) for c0 in range(0, F, FFN_COLS)]
    h_in, h_out, h_shape, h_scratch = _hosted_specs(hosted)

    def body(*refs):
        i = pl.program_id(0)
        refs, finish = _hosted_edges(hosted, refs, 5, 3, i == 0, i == nt - 1)
        x_ref, g_ref, wg_ref, wu_ref, wd_ref, o_ref, a_ref, b_ref = refs
        xv = x_ref[...]
        h = _rms_fwd(xv, g_ref[...])[2].astype(MM)
        acc = jnp.zeros((tm, D), F32)
        for c0, w in pieces:
            a = _dot(h, wg_ref[:, c0:c0 + w])
            b = _dot(h, wu_ref[:, c0:c0 + w])
            a_ref[:, c0:c0 + w] = a.astype(a_ref.dtype)
            b_ref[:, c0:c0 + w] = b.astype(b_ref.dtype)
            acc = acc + _dot(((a * _sigmoid(a)) * b).astype(MM), wd_ref[c0:c0 + w, :])
        o_ref[...] = xv + 0.5 * acc
        finish()

    tok = lambda i: (i, 0)
    par = lambda i: (0, 0)
    resident = lambda shape: BS(shape, par, pipeline_mode=pl.Buffered(1))
    return _pc(
        body, "ffn_fwd" + ("_hosting" if hosted else ""), (nt,),
        [BS((tm, D), tok), BS((1, D), par), resident((D, F)), resident((D, F)), resident((F, D))] + h_in,
        [BS((tm, D), tok), BS((tm, F), tok), BS((tm, F), tok)] + h_out,
        [SDS((T, D), F32), SDS((T, F), MM), SDS((T, F), MM)] + h_shape,
        scratch=h_scratch)(x, g, wg, wu, wd, *(hosted[1] if hosted else []))


def _ffn_bwd(x, dout, g, a, b, wg, wu, wd, hosted=None):
    T, D = x.shape
    F = wg.shape[1]
    tm = min(FFN_BWD_ROWS, T)
    nt = T // tm
    pieces = [(c0, min(FFN_COLS, F - c0)) for c0 in range(0, F, FFN_COLS)]
    h_in, h_out, h_shape, h_scratch = _hosted_specs(hosted)

    def body(*refs):
        i = pl.program_id(0)
        refs, finish = _hosted_edges(hosted, refs, 8, 7, i == 0, i == nt - 1)
        (x_ref, do_ref, g_ref, a_ref, b_ref, wg_ref, wu_ref, wd_ref,
         dx_ref, h_ref, dy_ref, da_ref, db_ref, s_ref, dg_ref) = refs

        @pl.when(i == 0)
        def _():
            dg_ref[...] = jnp.zeros_like(dg_ref)

        gv = g_ref[...]
        xh, r, hg = _rms_fwd(x_ref[...], gv)
        h_ref[...] = hg.astype(h_ref.dtype)
        dy = (0.5 * do_ref[...]).astype(MM)
        dy_ref[...] = dy
        dh = jnp.zeros((tm, D), F32)
        for c0, w in pieces:
            a = a_ref[:, c0:c0 + w].astype(F32)
            b = b_ref[:, c0:c0 + w].astype(F32)
            ds = _dot_nt(dy, wd_ref[c0:c0 + w, :])
            sig = _sigmoid(a)
            sl = a * sig
            s_ref[:, c0:c0 + w] = (sl * b).astype(s_ref.dtype)
            db = (ds * sl).astype(MM)
            da = (ds * b * (sig * (1.0 + a * (1.0 - sig)))).astype(MM)
            da_ref[:, c0:c0 + w] = da
            db_ref[:, c0:c0 + w] = db
            dh = dh + _dot_nt(da, wg_ref[:, c0:c0 + w]) + _dot_nt(db, wu_ref[:, c0:c0 + w])
        dx, dg = _rms_bwd(dh, xh, r, gv)
        dx_ref[...] = do_ref[...] + dx
        dg_ref[...] += dg
        finish()

    tok = lambda i: (i, 0)
    par = lambda i: (0, 0)
    hid = BS((tm, F), tok)
    resident = lambda shape: BS(shape, par, pipeline_mode=pl.Buffered(1))
    return _pc(
        body, "ffn_bwd" + ("_hosting" if hosted else ""), (nt,),
        [BS((tm, D), tok), BS((tm, D), tok), BS((1, D), par), hid, hid,
         resident((D, F)), resident((D, F)), resident((F, D))] + h_in,
        [BS((tm, D), tok), BS((tm, D), tok), BS((tm, D), tok), hid, hid, hid, BS((1, D), par)] + h_out,
        [SDS((T, D), F32), SDS((T, D), MM), SDS((T, D), MM),
         SDS((T, F), MM), SDS((T, F), MM), SDS((T, F), MM), SDS((1, D), F32)] + h_shape,
        scratch=h_scratch,
    )(x, dout, g, a, b, wg, wu, wd, *(hosted[1] if hosted else []))


def _wgrad(a, b, name, hosted=None):
    T, K = a.shape
    N = b.shape[1]
    tt = min(512, T)
    tn = next(c for c in (N, 1408, 1280, 1024, 512, 256, 128) if N % c == 0 and K * c * 4 <= WGRAD_ACC_BYTES)
    pieces = [(c0, min(WGRAD_COLS, tn - c0)) for c0 in range(0, tn, WGRAD_COLS)]
    nn, nt = N // tn, T // tt
    h_in, h_out, h_shape, h_scratch = _hosted_specs(hosted)

    def body(*refs):
        n, t = pl.program_id(0), pl.program_id(1)
        refs, finish = _hosted_edges(hosted, refs, 2, 1, (n == 0) & (t == 0), (n == nn - 1) & (t == nt - 1))
        a_ref, b_ref, o_ref = refs

        @pl.when(t == 0)
        def _():
            o_ref[...] = jnp.zeros_like(o_ref)

        av = a_ref[...].astype(MM)
        for c0, w in pieces:
            o_ref[:, c0:c0 + w] += _dot_tn(av, b_ref[:, c0:c0 + w].astype(MM))
        finish()

    res = _pc(
        body, name + ("_hosting" if hosted else ""), (nn, nt),
        [BS((tt, K), lambda n, t: (t, 0)), BS((tt, tn), lambda n, t: (t, n))] + h_in,
        [BS((K, tn), lambda n, t: (0, n))] + h_out, [SDS((K, N), F32)] + h_shape,
        scratch=h_scratch)(a, b, *(hosted[1] if hosted else []))
    return res if hosted else res[0]


C_POOL, C_Q, C_K, C_V, C_CA, C_CG, C_ZF, C_END = 0, 256, 768, 1280, 1792, 2048, 2304, 2560


def _mix_in_fwd(x, g, w):
    T, D = x.shape
    tm = min(512, T)

    def body(x_ref, g_ref, w_ref, up_ref, q_ref, k_ref, v_ref, ca_ref, cg_ref, zf_ref):
        _, _, hg = _rms_fwd(x_ref[...], g_ref[...])
        p = _dot(hg.astype(MM), w_ref[...])
        up_ref[...] = p[:, C_POOL:C_Q]
        q_ref[...] = p[:, C_Q:C_K].astype(q_ref.dtype)
        k_ref[...] = p[:, C_K:C_V].astype(k_ref.dtype)
        v_ref[...] = p[:, C_V:C_CA].astype(v_ref.dtype)
        ca_ref[...] = p[:, C_CA:C_CG]
        cg_ref[...] = p[:, C_CG:C_ZF]
        zf_ref[...] = p[:, C_ZF:C_ZF + LANES]

    tok = lambda i: (i, 0)
    widths = (256, 512, 512, 512, 256, 256, 128)
    dtypes = (F32, MM, MM, MM, F32, F32, F32)
    return _pc(
        body, "mix_in_fwd", (T // tm,),
        [BS((tm, D), tok), BS((1, D), lambda i: (0, 0)), BS((D, C_END), lambda i: (0, 0))],
        [BS((tm, wd), tok) for wd in widths],
        [SDS((T, wd), dt) for wd, dt in zip(widths, dtypes)])(x, g, w)


def _mix_in_bwd(x, dout, g, w, dup, dq, dk, dv, dca, dcg, dzf):
    T, D = x.shape
    tm = min(512, T)

    def body(x_ref, do_ref, g_ref, w_ref, dup_ref, dq_ref, dk_ref, dv_ref, dca_ref, dcg_ref, dzf_ref,
             dx_ref, h_ref, dp_ref, dg_ref):
        @pl.when(pl.program_id(0) == 0)
        def _():
            dg_ref[...] = jnp.zeros_like(dg_ref)

        gv = g_ref[...]
        xh, r, hg = _rms_fwd(x_ref[...], gv)
        h_ref[...] = hg.astype(h_ref.dtype)
        for ref, lo, hi in ((dup_ref, C_POOL, C_Q), (dq_ref, C_Q, C_K), (dk_ref, C_K, C_V), (dv_ref, C_V, C_CA),
                            (dca_ref, C_CA, C_CG), (dcg_ref, C_CG, C_ZF), (dzf_ref, C_ZF, C_ZF + LANES)):
            dp_ref[:, lo:hi] = ref[...].astype(dp_ref.dtype)
        dp_ref[:, C_ZF + LANES:C_END] = jnp.zeros((tm, C_END - C_ZF - LANES), dp_ref.dtype)
        dh = _dot_nt(dp_ref[...], w_ref[...])
        dx, dg = _rms_bwd(dh, xh, r, gv)
        dx_ref[...] = do_ref[...] + dx
        dg_ref[...] += dg

    tok = lambda i: (i, 0)
    widths = (256, 512, 512, 512, 256, 256, 128)
    return _pc(
        body, "mix_in_bwd", (T // tm,),
        [BS((tm, D), tok), BS((tm, D), tok), BS((1, D), lambda i: (0, 0)), BS((D, C_END), lambda i: (0, 0))]
        + [BS((tm, wd), tok) for wd in widths],
        [BS((tm, D), tok), BS((tm, D), tok), BS((tm, C_END), tok), BS((1, D), lambda i: (0, 0))],
        [SDS((T, D), F32), SDS((T, D), MM), SDS((T, C_END), MM), SDS((1, D), F32)],
    )(x, dout, g, w, dup, dq, dk, dv, dca, dcg, dzf)


def _mix_out_fwd(x, ya, yb, yc, wo):
    T, D = x.shape
    tm = min(512, T)

    def body(x_ref, ya_ref, yb_ref, yc_ref, wo_ref, o_ref):
        o_ref[...] = (x_ref[...] + _dot(ya_ref[...].astype(MM), wo_ref[0:256, :])
                      + _dot(yb_ref[...].astype(MM), wo_ref[256:768, :])
                      + _dot(yc_ref[...].astype(MM), wo_ref[768:1024, :]))

    tok = lambda i: (i, 0)
    return _pc(
        body, "mix_out_fwd", (T // tm,),
        [BS((tm, D), tok), BS((tm, 256), tok), BS((tm, 512), tok), BS((tm, 256), tok), BS((D, D), lambda i: (0, 0))],
        BS((tm, D), tok), SDS((T, D), F32))(x, ya, yb, yc, wo)


def _mix_out_bwd(dx, wo):
    T, D = dx.shape
    tm = min(512, T)

    def body(dx_ref, wo_ref, dya_ref, dyb_ref, dyc_ref):
        dy = _dot_nt(dx_ref[...].astype(MM), wo_ref[...])
        dya_ref[...] = dy[:, 0:256]
        dyb_ref[...] = dy[:, 256:768]
        dyc_ref[...] = dy[:, 768:1024]

    tok = lambda i: (i, 0)
    return _pc(
        body, "mix_out_bwd", (T // tm,),
        [BS((tm, D), tok), BS((D, D), lambda i: (0, 0))],
        [BS((tm, 256), tok), BS((tm, 512), tok), BS((tm, 256), tok)],
        [SDS((T, 256), F32), SDS((T, 512), F32), SDS((T, 256), F32)])(dx, wo)


def _fgate_fwd(zf, bias):
    T = zf.shape[0]
    tc = min(256, T)

    def body(z_ref, b_ref, f_ref, carry):
        @pl.when(pl.program_id(0) == 0)
        def _():
            carry[...] = jnp.zeros_like(carry)

        z = z_ref[...] + b_ref[...]
        logf = jnp.minimum(z, 0.0) - jnp.log(1.0 + jnp.exp(-jnp.abs(z)))
        row = lax.broadcasted_iota(jnp.int32, (tc, tc), 0)
        col = lax.broadcasted_iota(jnp.int32, (tc, tc), 1)
        tri = (col <= row).astype(F32)
        f_ref[...] = jnp.dot(tri, logf, precision=lax.Precision.HIGHEST, preferred_element_type=F32) + carry[...]
        carry[...] += jnp.sum(logf, axis=0, keepdims=True)

    return _pc(
        body, "fgate_fwd", (T // tc,),
        [BS((tc, LANES), lambda i: (i, 0)), BS((1, LANES), lambda i: (0, 0))],
        BS((tc, LANES), lambda i: (i, 0)), SDS((T, LANES), F32),
        scratch=[pltpu.VMEM((1, LANES), F32)])(zf, bias)


def _fgate_bwd(zf, bias, dFq, dFk):
    T = zf.shape[0]
    tc = min(256, T)
    n = T // tc
    slabs = dFq.shape[0]

    def body(z_ref, b_ref, dfq_ref, dfk_ref, dz_ref, db_ref, carry):
        @pl.when(pl.program_id(0) == 0)
        def _():
            carry[...] = jnp.zeros_like(carry)
            db_ref[...] = jnp.zeros_like(db_ref)

        df = dfk_ref[...]
        for sl in range(slabs):
            df = df + dfq_ref[sl]
        row = lax.broadcasted_iota(jnp.int32, (tc, tc), 0)
        col = lax.broadcasted_iota(jnp.int32, (tc, tc), 1)
        tri = (col >= row).astype(F32)
        dlogf = jnp.dot(tri, df, precision=lax.Precision.HIGHEST, preferred_element_type=F32) + carry[...]
        carry[...] += jnp.sum(df, axis=0, keepdims=True)
        lane = lax.broadcasted_iota(jnp.int32, (1, LANES), 1)
        dz = jnp.where(lane < HEADS, dlogf * _sigmoid(-(z_ref[...] + b_ref[...])), 0.0)
        dz_ref[...] = dz
        db_ref[...] += jnp.sum(dz, axis=0, keepdims=True)

    rev = lambda i: (n - 1 - i, 0)
    return _pc(
        body, "fgate_bwd", (n,),
        [BS((tc, LANES), rev), BS((1, LANES), lambda i: (0, 0)), BS((slabs, tc, LANES), lambda i: (0, n - 1 - i, 0)),
         BS((tc, LANES), rev)],
        [BS((tc, LANES), rev), BS((1, LANES), lambda i: (0, 0))],
        [SDS((T, LANES), F32), SDS((1, LANES), F32)],
        scratch=[pltpu.VMEM((1, LANES), F32)])(zf, bias, dFq, dFk)


LOG2E = 1.4426950408889634


def _split3(x):
    hi = x.astype(MM)
    r1 = x - hi.astype(F32)
    mid = r1.astype(MM)
    return hi, mid, (r1 - mid.astype(F32)).astype(MM)


def _place(lane, base, cols):
    out = jnp.zeros((cols[0].shape[0], LANES), MM)
    for i, c in enumerate(cols):
        out = jnp.where(lane == base + i, c, out)
    return out


def _head_col(block, lane, h):
    return jnp.sum(jnp.where(lane == h, block, 0.0), axis=-1, keepdims=True)


def _own_lanes(lane, hh):
    return (lane < HEAD_DIM) if hh == 0 else (lane >= HEAD_DIM)


def _attn_k_side(k_ref, f_ref, kb_ref, hp, T, rows, lse_ones, v_ref=None, vb_ref=None):
    lane = lax.broadcasted_iota(jnp.int32, (1, LANES), 1)
    one = jnp.ones((rows, 1), MM)

    def chunk(c, _):
        r0 = pl.multiple_of(c * rows, rows)
        kp = k_ref[pl.ds(r0, rows), :]
        fblk = f_ref[pl.ds(r0, rows), :]
        for hh in range(2):
            hi, mid, lo = _split3(-_head_col(fblk, lane, 2 * hp + hh) * LOG2E)
            cols = [one, one, one, hi, mid, lo] + ([one, one, one] if lse_ones else [])
            bias = _place(lane, HEAD_DIM * (1 - hh), cols)
            kb_ref[hh, pl.ds(r0, rows), :] = jnp.where(_own_lanes(lane, hh), kp, bias)
            if vb_ref is not None:
                vb_ref[hh, pl.ds(r0, rows), :] = jnp.where(_own_lanes(lane, hh), v_ref[pl.ds(r0, rows), :],
                                                           jnp.ones((rows, LANES), MM))
        return 0

    lax.fori_loop(0, T // rows, chunk, 0)


def _attn_q_side(qp, fblk, lane, hp, scale, lse_blk=None):
    qc = qp.astype(F32) * (scale * LOG2E)
    qhi = qc.astype(MM)
    qlo = (qc - qhi.astype(F32)).astype(MM)
    one = jnp.ones((qp.shape[0], 1), MM)
    out = []
    for hh in range(2):
        cols = list(_split3(_head_col(fblk, lane, 2 * hp + hh) * LOG2E)) + [one, one, one]
        if lse_blk is not None:
            cols += list(_split3(-_head_col(lse_blk, lane, 2 * hp + hh)))
        bias = _place(lane, HEAD_DIM * (1 - hh), cols)
        own = _own_lanes(lane, hh)
        out.append(jnp.concatenate([jnp.where(own, qhi, jnp.zeros_like(qhi)), jnp.where(own, qlo, bias)], axis=1))
    return out


def _causal(tq, tk):
    return lax.broadcasted_iota(jnp.int32, (tq, tk), 1) <= lax.broadcasted_iota(jnp.int32, (tq, tk), 0)


def _hosted_specs(hosted):
    if hosted is None:
        return [], [], [], []
    kind, arrays = hosted
    n = len(arrays)
    return [ANY] * n, [ANY] * n, [kind.out_shape(a) for a in arrays], kind.scratch(n)


def _hosted_edges(hosted, refs, n_in, n_out, first, last):
    if hosted is None:
        return refs, lambda: None
    kind, arrays = hosted
    n = len(arrays)
    nsem = len(kind.scratch(n))
    o0 = n_in + n + n_out
    ins, outs, sems = refs[n_in:n_in + n], refs[o0:o0 + n], refs[len(refs) - nsem:]

    @pl.when(first)
    def _():
        kind.start(ins, outs, *sems)

    def finish():
        @pl.when(last)
        def _():
            kind.wait(ins, outs, *sems)

    return refs[:n_in] + refs[n_in + n:o0] + refs[o0 + n:len(refs) - nsem], finish


def _attn_fwd(q, k, v, F, hosted=None):
    T = q.shape[0]
    tq = min(ATTN_FWD_TILE, T)
    tk = tq
    nq = T // tq
    scale = 1.0 / math.sqrt(HEAD_DIM)
    h_in, h_out, h_shape, h_scratch = _hosted_specs(hosted)

    def body(*refs):
        hp, ib = pl.program_id(0), pl.program_id(1)
        refs, finish = _hosted_edges(hosted, refs, 5, 2, (hp == 0) & (ib == 0), (hp == HEADS // 2 - 1) & (ib == nq - 1))
        q_ref, k_ref, v_ref, fq_ref, f_ref, o_ref, lse_ref, kb_ref, vb_ref = refs
        lane = lax.broadcasted_iota(jnp.int32, (1, LANES), 1)

        @pl.when(ib == 0)
        def _():
            _attn_k_side(k_ref, f_ref, kb_ref, hp, T, min(512, T), False, v_ref, vb_ref)

        qa = _attn_q_side(q_ref[...], fq_ref[...], lane, hp, scale)

        def tile(jb, carry, masked):
            off = pl.multiple_of(jb * tk, tk)
            kp = k_ref[pl.ds(off, tk), :]
            new = []
            for hh in range(2):
                m, acc = carry[hh]
                s = _dot_nt(qa[hh], jnp.concatenate([kp, kb_ref[hh, pl.ds(off, tk), :]], axis=1))
                if masked:
                    s = jnp.where(_causal(tq, tk), s, -jnp.inf)
                m2 = jnp.maximum(m, jnp.max(s, axis=-1, keepdims=True))
                p = jnp.exp2(s - m2)
                new.append((m2, acc * jnp.exp2(m - m2) + _dot(p.astype(MM), vb_ref[hh, pl.ds(off, tk), :])))
            return tuple(new)

        init = tuple((jnp.full((tq, 1), -jnp.inf, F32), jnp.zeros((tq, LANES), F32)) for _ in range(2))
        carry = lax.fori_loop(0, ib, lambda jb, c: tile(jb, c, False), init)
        (m0, a0), (m1, a1) = tile(ib, carry, True)
        l0, l1 = a0[:, HEAD_DIM:HEAD_DIM + 1], a1[:, 0:1]
        o_ref[...] = jnp.where(lane < HEAD_DIM, a0 / l0, a1 / l1)
        lse_ref[...] = jnp.where(lane == 2 * hp, m0 + jnp.log2(l0), jnp.where(lane == 2 * hp + 1, m1 + jnp.log2(l1), 0.0))
        finish()

    blk = lambda h, i: (i, h)
    full = lambda h, i: (0, h)
    return _pc(
        body, "attn_fwd" + ("_hosting" if hosted else ""), (HEADS // 2, nq),
        [BS((tq, LANES), blk), BS((T, LANES), full), BS((T, LANES), full), BS((tq, LANES), lambda h, i: (i, 0)),
         BS((T, LANES), lambda h, i: (0, 0))] + h_in,
        [BS((tq, LANES), blk), BS((None, tq, LANES), lambda h, i: (h, i, 0))] + h_out,
        [SDS((T, HEADS * HEAD_DIM), F32), SDS((HEADS // 2, T, LANES), F32)] + h_shape,
        scratch=[pltpu.VMEM((2, T, LANES), MM)] * 2 + h_scratch)(q, k, v, F, F, *(hosted[1] if hosted else []))


def _attn_bwd(q, k, v, F, o, lse, do, hosted=None):
    T = q.shape[0]
    tq = min(512, T)
    tk = tq
    nq = T // tq
    scale = 1.0 / math.sqrt(HEAD_DIM)
    h_in, h_out, h_shape, h_scratch = _hosted_specs(hosted)

    def body(*refs):
        hp, ib = pl.program_id(0), pl.program_id(1)
        refs, finish = _hosted_edges(hosted, refs, 8, 5, (hp == 0) & (ib == 0), (hp == HEADS // 2 - 1) & (ib == nq - 1))
        (q_ref, k_ref, v_ref, fq_ref, f_ref, o_ref, lse_ref, do_ref,
         dq_ref, dk_ref, dv_ref, dfq_ref, dfk_ref, kb_ref) = refs
        lane = lax.broadcasted_iota(jnp.int32, (1, LANES), 1)

        @pl.when(ib == 0)
        def _():
            _attn_k_side(k_ref, f_ref, kb_ref, hp, T, tk, True)
            dk_ref[...] = jnp.zeros_like(dk_ref)
            dv_ref[...] = jnp.zeros_like(dv_ref)
            dfk_ref[...] = jnp.zeros_like(dfk_ref)

        qp = q_ref[...]
        qa = _attn_q_side(qp, fq_ref[...], lane, hp, scale, lse_ref[...])
        dob = do_ref[...].astype(MM)
        dprod = dob.astype(F32) * o_ref[...]
        qs = (qp.astype(F32) * scale).astype(MM)
        heads = []
        for hh in range(2):
            own = _own_lanes(lane, hh)
            heads.append((jnp.where(own, dob, jnp.zeros_like(dob)), jnp.where(own, qs, jnp.zeros_like(qs)),
                          jnp.sum(jnp.where(own, dprod, 0.0), axis=-1, keepdims=True)))

        def tile(jb, carry, masked):
            off = pl.multiple_of(jb * tk, tk)
            kp = k_ref[pl.ds(off, tk), :]
            vp = v_ref[pl.ds(off, tk), :]
            new = []
            dv_t = jnp.zeros((tk, LANES), F32)
            dk_t = jnp.zeros((tk, LANES), F32)
            for hh in range(2):
                dq, rs = carry[hh]
                dom, qm, delta = heads[hh]
                p = jnp.exp2(_dot_nt(qa[hh], jnp.concatenate([kp, kb_ref[hh, pl.ds(off, tk), :]], axis=1)))
                if masked:
                    p = jnp.where(_causal(tq, tk), p, 0.0)
                ds = p * (_dot_nt(dom, vp) - delta)
                dsb = ds.astype(MM)
                dv_t = dv_t + _dot_tn(p.astype(MM), dom)
                dk_t = dk_t + _dot_tn(dsb, qm)
                dfk_ref[jb, pl.ds(hh, 1), :] -= jnp.sum(ds, axis=0, keepdims=True)
                new.append((dq + _dot(dsb, kp), rs + jnp.sum(ds, axis=-1, keepdims=True)))
            dv_ref[pl.ds(off, tk), :] += dv_t
            dk_ref[pl.ds(off, tk), :] += dk_t
            return tuple(new)

        init = tuple((jnp.zeros((tq, LANES), F32), jnp.zeros((tq, 1), F32)) for _ in range(2))
        carry = lax.fori_loop(0, ib, lambda jb, c: tile(jb, c, False), init)
        (dq0, rs0), (dq1, rs1) = tile(ib, carry, True)
        dq_ref[...] = jnp.where(lane < HEAD_DIM, dq0, dq1) * scale
        dfq_ref[...] = jnp.where(lane == 2 * hp, rs0, jnp.where(lane == 2 * hp + 1, rs1, 0.0))
        finish()

    blk = lambda h, i: (i, h)
    full = lambda h, i: (0, h)
    slab = BS((None, tq, LANES), lambda h, i: (h, i, 0))
    return _pc(
        body, "attn_bwd" + ("_hosting" if hosted else ""), (HEADS // 2, nq),
        [BS((tq, LANES), blk), BS((T, LANES), full), BS((T, LANES), full), BS((tq, LANES), lambda h, i: (i, 0)),
         BS((T, LANES), lambda h, i: (0, 0)), BS((tq, LANES), blk), slab, BS((tq, LANES), blk)] + h_in,
        [BS((tq, LANES), blk), BS((T, LANES), full), BS((T, LANES), full), slab,
         BS((None, nq, 2, tk), lambda h, i: (h, 0, 0, 0))] + h_out,
        [SDS((T, HEADS * HEAD_DIM), F32)] * 3 + [SDS((HEADS // 2, T, LANES), F32), SDS((HEADS // 2, nq, 2, tk), F32)]
        + h_shape,
        scratch=[pltpu.VMEM((2, T, LANES), MM)] + h_scratch,
    )(q, k, v, F, F, o, lse, do, *(hosted[1] if hosted else []))


POOL_HALO = 16
CONV_HALO = 32


def _group_select(lane, v0, v1, v2, v3):
    return jnp.where(lane < 64, v0, jnp.where(lane < 128, v1, jnp.where(lane < 192, v2, v3)))


def _roll_down(x, k):
    return x if k == 0 else pltpu.roll(x, k, 0)


def _roll_up(x, k):
    return x if k == 0 else pltpu.roll(x, x.shape[0] - k, 0)


def _pool_terms(u, u_prev, tile, tm):
    ext = jnp.concatenate([u_prev, u], axis=0)
    s2 = ext + _roll_down(ext, 1)
    s4 = s2 + _roll_down(s2, 2)
    s8 = s4 + _roll_down(s4, 4)
    s16 = s8 + _roll_down(s8, 8)
    lane = lax.broadcasted_iota(jnp.int32, (1, 256), 1)
    ws = _group_select(lane, s2, s4, s8, s16)[POOL_HALO:, :]
    wlen = _group_select(lane, 2.0, 4.0, 8.0, 16.0).astype(F32)
    return ws / _pool_count(tile, tm, tm, wlen) - u


def _pool_count(tile, tm, rows, wlen):
    t = (tile * tm + 1 + lax.broadcasted_iota(jnp.int32, (rows, 1), 0)).astype(F32)
    return jnp.minimum(t, wlen)


def _layer_norm(y, lg, lb):
    mu = jnp.mean(y, axis=-1, keepdims=True)
    yc = y - mu
    rstd = lax.rsqrt(jnp.mean(yc * yc, axis=-1, keepdims=True) + NORM_EPS)
    yh = yc * rstd
    return yh, rstd, yh * lg + lb


def _halo_specs(tm, T, halo, prev):
    per = tm // halo
    if prev:
        return BS((halo, 256), lambda i: (jnp.maximum(i * per - 1, 0), 0))
    return BS((halo, 256), lambda i: (jnp.minimum((i + 1) * per, T // halo - 1), 0))


def _local_fwd(up, ca, cg, bd, pscale, cw, cb, lg, lb):
    T = up.shape[0]
    tm = min(512, T)

    def body(up_ref, uph_ref, ca_ref, cah_ref, cg_ref, cgh_ref, bd_ref, ps_ref, cw_ref, cb_ref, lg_ref, lb_ref,
             ya_ref, yc_ref, u_ref, y_ref):
        i = pl.program_id(0)
        first = i == 0
        pooled = _pool_terms(up_ref[...], jnp.where(first, 0.0, uph_ref[...]), i, tm)
        ya_ref[...] = (_dot(pooled.astype(MM), bd_ref[...]) * ps_ref[...]).astype(ya_ref.dtype)

        u = ca_ref[...] * _sigmoid(cg_ref[...])
        uh = jnp.where(first, 0.0, cah_ref[...] * _sigmoid(cgh_ref[...]))
        ext = jnp.concatenate([uh, u], axis=0)
        y = jnp.zeros((tm, 256), F32) + cb_ref[...]
        for kk in range(CONV_K):
            y = y + cw_ref[kk:kk + 1, :] * _roll_up(ext, CONV_HALO - (CONV_K - 1) + kk)[:tm, :]
        _, _, z = _layer_norm(y, lg_ref[...], lb_ref[...])
        yc_ref[...] = (z * _sigmoid(z)).astype(yc_ref.dtype)
        u_ref[...] = u
        y_ref[...] = y

    tok = lambda i: (i, 0)
    par = lambda i: (0, 0)
    t256 = BS((tm, 256), tok)
    return _pc(
        body, "local_fwd", (T // tm,),
        [t256, _halo_specs(tm, T, POOL_HALO, True), t256, _halo_specs(tm, T, CONV_HALO, True),
         t256, _halo_specs(tm, T, CONV_HALO, True),
         BS((256, 256), par), BS((1, 256), par), BS((32, 256), par), BS((1, 256), par), BS((1, 256), par),
         BS((1, 256), par)],
        [t256, t256, t256, t256],
        [SDS((T, 256), MM), SDS((T, 256), MM), SDS((T, 256), F32), SDS((T, 256), F32)],
    )(up, up, ca, ca, cg, cg, bd, pscale, cw, cb, lg, lb)


def _local_bwd(up, dya, ca, cg, u, y, dyc, bd, pscale, cw, lg, lb):
    T = up.shape[0]
    tm = min(512, T)
    n = T // tm

    def body(up_ref, uph_ref, dya_ref, dyan_ref, ca_ref, cg_ref, u_ref, uh_ref, y_ref, yn_ref, dyc_ref, dycn_ref,
             bd_ref, ps_ref, cw_ref, lg_ref, lb_ref,
             dup_ref, dca_ref, dcg_ref, dbd_ref, dps_ref, dcw_ref, dcb_ref, dlg_ref, dlb_ref):
        i = pl.program_id(0)
        first = i == 0
        last = i == n - 1

        @pl.when(first)
        def _():
            for ref in (dbd_ref, dps_ref, dcw_ref, dcb_ref, dlg_ref, dlb_ref):
                ref[...] = jnp.zeros_like(ref)

        ps = ps_ref[...]
        pooled = _pool_terms(up_ref[...], jnp.where(first, 0.0, uph_ref[...]), i, tm).astype(MM)
        dya_t = dya_ref[...]
        dps_ref[...] += jnp.sum(dya_t * _dot(pooled, bd_ref[...]), axis=0, keepdims=True)
        dm = (jnp.concatenate([dya_t, jnp.where(last, 0.0, dyan_ref[...])], axis=0) * ps).astype(MM)
        dbd_ref[...] += _dot_tn(pooled, dm[:tm, :])
        dpool = _dot_nt(dm, bd_ref[...])
        lane = lax.broadcasted_iota(jnp.int32, (1, 256), 1)
        wlen = _group_select(lane, 2.0, 4.0, 8.0, 16.0).astype(F32)
        e = dpool / _pool_count(i, tm, tm + POOL_HALO, wlen)
        f2 = e + _roll_up(e, 1)
        f4 = f2 + _roll_up(f2, 2)
        f8 = f4 + _roll_up(f4, 4)
        f16 = f8 + _roll_up(f8, 8)
        dup_ref[...] = _group_select(lane, f2, f4, f8, f16)[:tm, :] - dpool[:tm, :]

        lgv = lg_ref[...]
        yext = jnp.concatenate([y_ref[...], yn_ref[...]], axis=0)
        dyc = jnp.concatenate([dyc_ref[...], jnp.where(last, 0.0, dycn_ref[...])], axis=0)
        yh, rstd, z = _layer_norm(yext, lgv, lb_ref[...])
        sig = _sigmoid(z)
        dz = dyc * (sig * (1.0 + z * (1.0 - sig)))
        dlg_ref[...] += jnp.sum((dz * yh)[:tm, :], axis=0, keepdims=True)
        dlb_ref[...] += jnp.sum(dz[:tm, :], axis=0, keepdims=True)
        dyh = dz * lgv
        dy = rstd * (dyh - jnp.mean(dyh, axis=-1, keepdims=True) - yh * jnp.mean(dyh * yh, axis=-1, keepdims=True))
        dy_t = dy[:tm, :]
        dcb_ref[...] += jnp.sum(dy_t, axis=0, keepdims=True)
        uext = jnp.concatenate([jnp.where(first, 0.0, uh_ref[...]), u_ref[...]], axis=0)
        du = jnp.zeros((tm, 256), F32)
        for kk in range(CONV_K):
            shifted = _roll_up(uext, CONV_HALO - (CONV_K - 1) + kk)[:tm, :]
            dcw_ref[kk:kk + 1, :] += jnp.sum(dy_t * shifted, axis=0, keepdims=True)
            du = du + cw_ref[kk:kk + 1, :] * _roll_up(dy, CONV_K - 1 - kk)[:tm, :]
        sg = _sigmoid(cg_ref[...])
        dca_ref[...] = du * sg
        dcg_ref[...] = du * ca_ref[...] * sg * (1.0 - sg)

    tok = lambda i: (i, 0)
    par = lambda i: (0, 0)
    t256 = BS((tm, 256), tok)
    p1 = BS((1, 256), par)
    return _pc(
        body, "local_bwd", (n,),
        [t256, _halo_specs(tm, T, POOL_HALO, True), t256, _halo_specs(tm, T, POOL_HALO, False), t256, t256,
         t256, _halo_specs(tm, T, CONV_HALO, True), t256, _halo_specs(tm, T, CONV_HALO, False),
         t256, _halo_specs(tm, T, CONV_HALO, False),
         BS((256, 256), par), p1, BS((32, 256), par), p1, p1],
        [t256, t256, t256, BS((256, 256), par), p1, BS((32, 256), par), p1, p1, p1],
        [SDS((T, 256), F32)] * 3 + [SDS((256, 256), F32), SDS((1, 256), F32), SDS((32, 256), F32)]
        + [SDS((1, 256), F32)] * 3,
    )(up, up, dya, dya, ca, cg, u, u, y, y, dyc, dyc, bd, pscale, cw, lg, lb)


def _head(x, g, target):
    T, D = x.shape
    tm = min(512, T)

    def body(x_ref, g_ref, t_ref, loss_ref, dx_ref, dg_ref):
        @pl.when(pl.program_id(0) == 0)
        def _():
            loss_ref[...] = jnp.zeros_like(loss_ref)
            dg_ref[...] = jnp.zeros_like(dg_ref)

        gv = g_ref[...]
        xh, r, yv = _rms_fwd(x_ref[...], gv)
        err = yv - t_ref[...]
        loss_ref[...] += 0.5 * jnp.sum(jnp.mean(err * err, axis=-1, keepdims=True), axis=0, keepdims=True)
        dx, dg = _rms_bwd(err * (1.0 / D), xh, r, gv)
        dx_ref[...] = dx
        dg_ref[...] += dg

    tok = lambda i: (i, 0)
    par = lambda i: (0, 0)
    return _pc(
        body, "head", (T // tm,),
        [BS((tm, D), tok), BS((1, D), par), BS((tm, D), tok)],
        [BS((1, LANES), par), BS((tm, D), tok), BS((1, D), par)],
        [SDS((1, LANES), F32), SDS((T, D), F32), SDS((1, D), F32)])(x, g, target)


def _adamw(w, gs, m, v, name):
    R, C = w.shape
    tr = R
    for cand in (512, 256, 128, 64, 32, 16, 8):
        if R % cand == 0:
            tr = cand
            break
    ng = len(gs)

    def body(*refs):
        w_ref, g_refs, m_ref, v_ref = refs[0], refs[1:1 + ng], refs[1 + ng], refs[2 + ng]
        g_ref, d_ref, m2_ref, v2_ref = refs[3 + ng:]
        g = g_refs[0][...]
        for r in g_refs[1:]:
            g = g + r[...]
        m2 = ADAM_B1 * m_ref[...] + (1.0 - ADAM_B1) * g
        v2 = ADAM_B2 * v_ref[...] + (1.0 - ADAM_B2) * jnp.square(g)
        m_hat = m2 / (1.0 - ADAM_B1 ** ADAM_STEP)
        v_hat = v2 / (1.0 - ADAM_B2 ** ADAM_STEP)
        g_ref[...] = g
        d_ref[...] = -ADAM_LR * (m_hat / (jnp.sqrt(v_hat) + ADAM_EPS) + ADAM_WD * w_ref[...])
        m2_ref[...] = m2
        v2_ref[...] = v2

    blk = BS((tr, C), lambda i: (i, 0))
    return _pc(body, name, (R // tr,), [blk] * (3 + ng), [blk] * 4, [SDS((R, C), F32)] * 4)(w, *gs, m, v)


def _sum_parts(owns, recvs, name):
    L = len(owns)
    R, C = owns[0].shape
    tr = next(t for t in (512, 256, 128, 64, 32, 16) if R % t == 0)

    def body(*refs):
        l = pl.program_id(0)
        s_ref = refs[2 * L]
        for ll in range(L):
            @pl.when(l == ll)
            def _(o_ref=refs[ll], r_ref=refs[L + ll]):
                s_ref[...] = ((o_ref[...] + r_ref[0].astype(F32)) + r_ref[1].astype(F32)) + r_ref[2].astype(F32)

    own_specs = [BS((tr, C), lambda l, i, ll=ll: (jnp.where(l == ll, i, 0), 0)) for ll in range(L)]
    recv_specs = [BS((3, tr, C), lambda l, i, ll=ll: (0, jnp.where(l == ll, i, 0), 0)) for ll in range(L)]
    return _pc(body, name, (L, R // tr), own_specs + recv_specs,
               BS((None, tr, C), lambda l, i: (l, i, 0)), SDS((L, R, C), F32))(*owns, *recvs)


def _sum8(parts, name):
    _, R, C = parts.shape

    def body(p_ref, s_ref):
        acc = p_ref[0]
        for d in range(1, 8):
            acc = acc + p_ref[d]
        s_ref[...] = acc

    return _pc(body, name, (1,), [BS((8, R, C), lambda i: (0, 0, 0))], BS((R, C), lambda i: (0, 0)),
               SDS((R, C), F32))(parts)


def _position():
    return lax.axis_index("x"), lax.axis_index("y"), lax.axis_index("c")


CHIP_FLIPS = ((1, 0), (0, 1), (1, 1))


class _GatherChips:
    @staticmethod
    def scratch(n):
        return [pltpu.SemaphoreType.DMA((3 * n,)), pltpu.SemaphoreType.DMA((3 * n,)), pltpu.SemaphoreType.DMA((n,))]

    @staticmethod
    def out_shape(block):
        return SDS((4,) + tuple(block.shape), block.dtype)

    @staticmethod
    def _copies(ins, outs, send_sems, recv_sems, local_sems, arrivals):
        x, y, c = _position()
        local, remote = [], []
        for i, (in_ref, out_ref) in enumerate(zip(ins, outs)):
            local.append(pltpu.make_async_copy(in_ref, out_ref.at[2 * x + y], local_sems.at[i]))
            for k, (fx, fy) in enumerate(CHIP_FLIPS):
                slot = 2 * (x ^ fx) + (y ^ fy) if arrivals else 2 * x + y
                remote.append(pltpu.make_async_remote_copy(
                    src_ref=in_ref, dst_ref=out_ref.at[slot], send_sem=send_sems.at[3 * i + k],
                    recv_sem=recv_sems.at[3 * i + k], device_id=(x ^ fx, y ^ fy, c), device_id_type=MESH))
        return local, remote

    @classmethod
    def start(cls, ins, outs, *sems):
        local, sends = cls._copies(ins, outs, *sems, arrivals=False)
        for cp in local + sends:
            cp.start()

    @classmethod
    def wait(cls, ins, outs, *sems):
        local, arrivals = cls._copies(ins, outs, *sems, arrivals=True)
        for cp in arrivals:
            cp.wait_recv()
        for cp in arrivals:
            cp.wait_send()
        for cp in local:
            cp.wait()


class _GatherChipsSplit(_GatherChips):
    @staticmethod
    def scratch(n):
        return [pltpu.SemaphoreType.DMA((6 * n,)), pltpu.SemaphoreType.DMA((6 * n,)), pltpu.SemaphoreType.DMA((n,))]

    @staticmethod
    def _half(ref, which):
        rows = ref.shape[0] // 2
        return ref.at[pl.ds(pl.multiple_of(which * rows, 16), rows)]

    @staticmethod
    def _local(ins, outs, local_sems):
        x, y, _ = _position()
        return [pltpu.make_async_copy(in_ref, out_ref.at[2 * x + y], local_sems.at[i])
                for i, (in_ref, out_ref) in enumerate(zip(ins, outs))]

    @classmethod
    def _between_chips(cls, ins, outs, send_sems, recv_sems, arrivals):
        x, y, c = _position()
        return [
            pltpu.make_async_remote_copy(
                src_ref=cls._half(in_ref, c),
                dst_ref=cls._half(out_ref.at[2 * (x ^ fx) + (y ^ fy) if arrivals else 2 * x + y], c),
                send_sem=send_sems.at[6 * i + k], recv_sem=recv_sems.at[6 * i + k],
                device_id=(x ^ fx, y ^ fy, c), device_id_type=MESH)
            for i, (in_ref, out_ref) in enumerate(zip(ins, outs)) for k, (fx, fy) in enumerate(CHIP_FLIPS)]

    @classmethod
    def _between_cores(cls, outs, send_sems, recv_sems, arrivals):
        x, y, c = _position()
        copies = []
        for i, out_ref in enumerate(outs):
            for k, (fx, fy) in enumerate(CHIP_FLIPS):
                half = cls._half(out_ref.at[2 * (x ^ fx) + (y ^ fy)], 1 - c if arrivals else c)
                copies.append(pltpu.make_async_remote_copy(
                    src_ref=half, dst_ref=half, send_sem=send_sems.at[6 * i + 3 + k],
                    recv_sem=recv_sems.at[6 * i + 3 + k], device_id=(x, y, 1 - c), device_id_type=MESH))
        return copies

    @classmethod
    def start(cls, ins, outs, send_sems, recv_sems, local_sems):
        for cp in cls._local(ins, outs, local_sems) + cls._between_chips(ins, outs, send_sems, recv_sems, False):
            cp.start()

    @classmethod
    def wait(cls, ins, outs, send_sems, recv_sems, local_sems):
        arrivals = cls._between_chips(ins, outs, send_sems, recv_sems, True)
        onward = cls._between_cores(outs, send_sems, recv_sems, False)
        for cp, nxt in zip(arrivals, onward):
            cp.wait_recv()
            nxt.start()
        for cp in cls._between_cores(outs, send_sems, recv_sems, True):
            cp.wait_recv()
        for cp in arrivals + onward:
            cp.wait_send()
        for cp in cls._local(ins, outs, local_sems):
            cp.wait()


class _Symmetric:
    @classmethod
    def start(cls, ins, outs, *sems):
        for cp in cls._copies(ins, outs, *sems):
            cp.start()

    @classmethod
    def wait(cls, ins, outs, *sems):
        copies = cls._copies(ins, outs, *sems)
        for cp in copies:
            cp.wait_recv()
        for cp in copies:
            cp.wait_send()


class _ScatterChips(_Symmetric):
    @staticmethod
    def scratch(n):
        return [pltpu.SemaphoreType.DMA((3 * n,)), pltpu.SemaphoreType.DMA((3 * n,))]

    @staticmethod
    def out_shape(parts):
        return SDS((3,) + tuple(parts.shape[1:]), parts.dtype)

    @staticmethod
    def _copies(ins, outs, send_sems, recv_sems):
        x, y, c = _position()
        return [
            pltpu.make_async_remote_copy(
                src_ref=in_ref.at[2 * (x ^ fx) + (y ^ fy)], dst_ref=out_ref.at[k],
                send_sem=send_sems.at[3 * i + k], recv_sem=recv_sems.at[3 * i + k],
                device_id=(x ^ fx, y ^ fy, c), device_id_type=MESH)
            for i, (in_ref, out_ref) in enumerate(zip(ins, outs)) for k, (fx, fy) in enumerate(CHIP_FLIPS)]


class _SwapCores(_Symmetric):
    @staticmethod
    def scratch(n):
        return [pltpu.SemaphoreType.DMA((n,)), pltpu.SemaphoreType.DMA((n,))]

    @staticmethod
    def out_shape(block):
        return SDS(block.shape, block.dtype)

    @staticmethod
    def _copies(ins, outs, send_sems, recv_sems):
        x, y, c = _position()
        return [
            pltpu.make_async_remote_copy(
                src_ref=in_ref, dst_ref=out_ref, send_sem=send_sems.at[i], recv_sem=recv_sems.at[i],
                device_id=(x, y, 1 - c), device_id_type=MESH)
            for i, (in_ref, out_ref) in enumerate(zip(ins, outs))]


def _exchange(kind, arrays, name):
    n = len(arrays)

    def body(*refs):
        ins, outs, sems = refs[:n], refs[n:2 * n], refs[2 * n:]
        kind.start(ins, outs, *sems)
        kind.wait(ins, outs, *sems)

    return pl.pallas_call(body, out_shape=[kind.out_shape(a) for a in arrays], in_specs=[ANY] * n,
                          out_specs=[ANY] * n, name=name, scratch_shapes=kind.scratch(n))(*arrays)


def _gather_all(block, name):
    R, C = block.shape
    flips = [(fx, fy, fc) for fx in (0, 1) for fy in (0, 1) for fc in (0, 1)][1:]

    def body(in_ref, out_ref, send_sems, recv_sems, local_sem):
        x, y, c = _position()
        mine = out_ref.at[4 * x + 2 * y + c]
        local = pltpu.make_async_copy(in_ref, mine, local_sem)
        local.start()
        copies = [
            pltpu.make_async_remote_copy(
                src_ref=in_ref, dst_ref=mine, send_sem=send_sems.at[k], recv_sem=recv_sems.at[k],
                device_id=(x ^ fx, y ^ fy, c ^ fc), device_id_type=MESH)
            for k, (fx, fy, fc) in enumerate(flips)]
        for cp in copies:
            cp.start()
        for k, (fx, fy, fc) in enumerate(flips):
            theirs = out_ref.at[4 * (x ^ fx) + 2 * (y ^ fy) + (c ^ fc)]
            pltpu.make_async_remote_copy(
                src_ref=in_ref, dst_ref=theirs, send_sem=send_sems.at[k], recv_sem=recv_sems.at[k],
                device_id=(x ^ fx, y ^ fy, c ^ fc), device_id_type=MESH).wait_recv()
        for cp in copies:
            cp.wait_send()
        local.wait()

    return pl.pallas_call(
        body, out_shape=SDS((8, R, C), block.dtype), in_specs=[ANY], out_specs=ANY, name=name,
        scratch_shapes=[pltpu.SemaphoreType.DMA((7,)), pltpu.SemaphoreType.DMA((7,)), pltpu.SemaphoreType.DMA(())])(block)


BIG = ("ffn1_w_gate", "ffn1_w_up", "ffn1_w_down", "w_in", "w_out", "ffn2_w_gate", "ffn2_w_up", "ffn2_w_down")
COL_SHARDED = ("ffn1_w_gate", "ffn1_w_up", "w_in", "ffn2_w_gate", "ffn2_w_up")
FIRST = tuple((n, 0) for n in ("ffn1_w_gate", "ffn1_w_up", "ffn1_w_down"))
LATE = tuple((n, 0) for n in ("w_out", "ffn2_w_gate", "ffn2_w_up", "ffn2_w_down")) + tuple((n, 1) for n in BIG)


def _to_shards(name, full):
    r, c = full.shape
    if name in COL_SHARDED:
        return full.reshape(r, 4, c // 4).transpose(1, 0, 2)
    return full.reshape(4, r // 4, c)


def _own_shard(name, full, chip):
    r, c = full.shape
    if name in COL_SHARDED:
        return lax.dynamic_slice_in_dim(full, chip * (c // 4), c // 4, axis=1)
    return lax.dynamic_slice_in_dim(full, chip * (r // 4), r // 4, axis=0)


def _from_shards(name, sh):
    _, r, c = sh.shape
    if name in COL_SHARDED:
        return sh.transpose(1, 0, 2).reshape(r, 4 * c)
    return sh.reshape(4 * r, c)


def _pad_w_in(w):
    return jnp.concatenate([w[:, :1792], w[:, 1800:2312], w[:, 1792:1800], jnp.zeros((w.shape[0], 248), w.dtype)], axis=1)


def _unpad_w_in(g):
    return jnp.concatenate([g[:, :1792], g[:, 2304:2312], g[:, 1792:2304]], axis=1)


def _block_diag(pw):
    out = jnp.zeros((256, 256), pw.dtype)
    for gidx in range(4):
        out = lax.dynamic_update_slice(out, pw[gidx], (64 * gidx, 64 * gidx))
    return out


SMALL = ("ffn1_norm", "mix_norm", "pool_w", "pool_scale", "forget_bias", "conv_b", "conv_ln_g", "conv_ln_b",
         "ffn2_norm", "final_norm")


def _pack_small(arrs):
    rows = []
    for a in arrs:
        flat = a.reshape(-1)
        flat = jnp.pad(flat, (0, -flat.shape[0] % LANES))
        rows.append(flat.reshape(-1, LANES))
    total = sum(r.shape[0] for r in rows)
    if total % 8:
        rows.append(jnp.zeros((-total % 8, LANES), F32))
    return jnp.concatenate(rows, axis=0)


def _unpack_small(buf, shapes):
    out, off = [], 0
    for shp in shapes:
        n = math.prod(shp)
        nr = -(-n // LANES)
        out.append(buf[off:off + nr].reshape(-1)[:n].reshape(shp))
        off += nr
    return out


def _grad_parts(grads, pieces):
    return [_to_shards(n, grads[n][l]).astype(MM) for n, l in pieces]


def _forward_backward(x, target, W, shards=None):
    T = x.shape[0]
    L = W["ffn1_norm"].shape[0]
    saved = []
    recv = {}
    for l in range(L):
        g1, gm, g2 = (W[n][l][None, :] for n in ("ffn1_norm", "mix_norm", "ffn2_norm"))
        first = shards is not None and l == 0
        hosted = (_GatherChips, [shards["w_in"][0], shards["conv_w"]]) if first else None
        x1, a1, b1, *got = _ffn_fwd(x, g1, W["ffn1_w_gate"][l], W["ffn1_w_up"][l], W["ffn1_w_down"][l], hosted=hosted)
        if first:
            W["w_in"][0] = _from_shards("w_in", got[0])
            W["conv_w"] = got[1].transpose(1, 2, 0, 3).reshape(L, CONV_K, 256)
        w_in = _pad_w_in(W["w_in"][l])
        up, q, k, v, ca, cg, zf = _mix_in_fwd(x1, gm, w_in)
        fb = jnp.pad(W["forget_bias"][l], (0, LANES - HEADS))[None, :]
        F = _fgate_fwd(zf, fb)
        if first:
            yb, lse, *got = _attn_fwd(q, k, v, F, hosted=(_GatherChips, [shards[n][ll] for n, ll in LATE]))
            for (n, ll), sh in zip(LATE, got):
                W[n][ll] = _from_shards(n, sh)
        else:
            yb, lse = _attn_fwd(q, k, v, F)
        bd = _block_diag(W["pool_w"][l]).astype(MM)
        ps, cb, lg, lb = (W[n][l][None, :] for n in ("pool_scale", "conv_b", "conv_ln_g", "conv_ln_b"))
        cw = jnp.pad(W["conv_w"][l], ((0, 1), (0, 0)))
        ya, yc, cu, cy = _local_fwd(up, ca, cg, bd, ps, cw, cb, lg, lb)
        x2 = _mix_out_fwd(x1, ya, yb, yc, W["w_out"][l])
        x3, a2, b2 = _ffn_fwd(x2, g2, W["ffn2_w_gate"][l], W["ffn2_w_up"][l], W["ffn2_w_down"][l])
        saved.append(dict(x0=x, x1=x1, x2=x2, ab1=(a1, b1), ab2=(a2, b2), w_in=w_in, up=up, ca=ca, cg=cg, zf=zf, fb=fb, F=F,
                          q=q, k=k, v=v, lse=lse, bd=bd, cw=cw, cu=cu, cy=cy, ya=ya, yb=yb, yc=yc))
        x = x3

    loss, dx, dgf = _head(x, W["final_norm"][None, :], target)
    grads = {n: [None] * L for n in W if n != "final_norm"}
    grads["final_norm"] = dgf[0]
    for l in reversed(range(L)):
        s = saved[l]
        g1, gm, g2 = (W[n][l][None, :] for n in ("ffn1_norm", "mix_norm", "ffn2_norm"))
        ps, lg, lb = (W[n][l][None, :] for n in ("pool_scale", "conv_ln_g", "conv_ln_b"))
        dx, h, dy, da, db, sact, dg = _ffn_bwd(s["x2"], dx, g2, *s["ab2"], W["ffn2_w_gate"][l], W["ffn2_w_up"][l],
                                               W["ffn2_w_down"][l])
        grads["ffn2_norm"][l] = dg[0]
        grads["ffn2_w_gate"][l] = _wgrad(h, da, "wgrad_gate")
        grads["ffn2_w_up"][l] = _wgrad(h, db, "wgrad_up")
        grads["ffn2_w_down"][l] = _wgrad(sact, dy, "wgrad_down")
        dya, dyb, dyc = _mix_out_bwd(dx, W["w_out"][l])
        grads["w_out"][l] = jnp.concatenate(
            [_wgrad(s["ya"], dx, "wgrad_out_a"), _wgrad(s["yb"], dx, "wgrad_out_b"), _wgrad(s["yc"], dx, "wgrad_out_c")], axis=0)
        first = shards is not None and l == 0
        if first:
            dq, dk, dv, dfq, dfk, *got = _attn_bwd(s["q"], s["k"], s["v"], s["F"], s["yb"], s["lse"], dyb,
                                                  hosted=(_ScatterChips, _grad_parts(grads, LATE)))
            recv.update(zip(LATE, got))
        else:
            dq, dk, dv, dfq, dfk = _attn_bwd(s["q"], s["k"], s["v"], s["F"], s["yb"], s["lse"], dyb)
        dfk_cols = jnp.pad(dfk.transpose(0, 2, 1, 3).reshape(HEADS, T).T, ((0, 0), (0, LANES - HEADS)))
        dzf, dfb = _fgate_bwd(s["zf"], s["fb"], dfq, dfk_cols)
        grads["forget_bias"][l] = dfb[0, :HEADS]
        dup, dca, dcg, dbd, dps, dcw, dcb, dlg, dlb = _local_bwd(
            s["up"], dya, s["ca"], s["cg"], s["cu"], s["cy"], dyc, s["bd"], ps, s["cw"], lg, lb)
        grads["pool_w"][l] = jnp.stack([dbd[64 * i:64 * i + 64, 64 * i:64 * i + 64] for i in range(4)])
        grads["pool_scale"][l], grads["conv_b"][l] = dps[0], dcb[0]
        grads["conv_ln_g"][l], grads["conv_ln_b"][l] = dlg[0], dlb[0]
        grads["conv_w"][l] = dcw[:CONV_K]
        dx, h, dp, dg = _mix_in_bwd(s["x1"], dx, gm, s["w_in"], dup, dq, dk, dv, dca, dcg, dzf)
        grads["mix_norm"][l] = dg[0]
        grads["w_in"][l] = _unpad_w_in(_wgrad(h, dp, "wgrad_in"))
        ffn1 = (W["ffn1_w_gate"][l], W["ffn1_w_up"][l], W["ffn1_w_down"][l])
        if not first:
            dx, h, dy, da, db, sact, dg = _ffn_bwd(s["x0"], dx, g1, *s["ab1"], *ffn1)
            grads["ffn1_w_gate"][l] = _wgrad(h, da, "wgrad_gate")
            grads["ffn1_w_up"][l] = _wgrad(h, db, "wgrad_up")
            grads["ffn1_w_down"][l] = _wgrad(sact, dy, "wgrad_down")
        else:
            scatter = lambda n: (_ScatterChips, _grad_parts(grads, [(n, 0)]))
            dx, h, dy, da, db, sact, dg, recv[("w_in", 0)] = _ffn_bwd(s["x0"], dx, g1, *s["ab1"], *ffn1,
                                                                      hosted=scatter("w_in"))
            grads["ffn1_w_gate"][0] = _wgrad(h, da, "wgrad_gate")
            grads["ffn1_w_up"][0], recv[("ffn1_w_gate", 0)] = _wgrad(h, db, "wgrad_up", hosted=scatter("ffn1_w_gate"))
            grads["ffn1_w_down"][0], recv[("ffn1_w_up", 0)] = _wgrad(sact, dy, "wgrad_down", hosted=scatter("ffn1_w_up"))
            recv[("ffn1_w_down", 0)] = _exchange(*scatter("ffn1_w_down"), "scatter_last_grad")[0]
        grads["ffn1_norm"][l] = dg[0]
    grads = {n: (jnp.stack(g) if isinstance(g, list) and n not in BIG else g) for n, g in grads.items()}
    return loss, dx, grads, recv


NAMES = ("ffn1_norm", "ffn1_w_gate", "ffn1_w_up", "ffn1_w_down", "mix_norm", "w_in", "pool_w", "pool_scale",
         "forget_bias", "conv_w", "conv_b", "conv_ln_g", "conv_ln_b", "w_out", "ffn2_norm", "ffn2_w_gate",
         "ffn2_w_up", "ffn2_w_down", "final_norm")


def kernel(x, ffn1_norm, ffn1_w_gate, ffn1_w_up, ffn1_w_down, mix_norm, w_in, pool_w, pool_scale, forget_bias, conv_w, conv_b, conv_ln_g, conv_ln_b, w_out, ffn2_norm, ffn2_w_gate, ffn2_w_up, ffn2_w_down, final_norm, loss_target, m_ffn1_norm, m_ffn1_w_gate, m_ffn1_w_up, m_ffn1_w_down, m_mix_norm, m_w_in, m_pool_w, m_pool_scale, m_forget_bias, m_conv_w, m_conv_b, m_conv_ln_g, m_conv_ln_b, m_w_out, m_ffn2_norm, m_ffn2_w_gate, m_ffn2_w_up, m_ffn2_w_down, m_final_norm, v_ffn1_norm, v_ffn1_w_gate, v_ffn1_w_up, v_ffn1_w_down, v_mix_norm, v_w_in, v_pool_w, v_pool_scale, v_forget_bias, v_conv_w, v_conv_b, v_conv_ln_g, v_conv_ln_b, v_w_out, v_ffn2_norm, v_ffn2_w_gate, v_ffn2_w_up, v_ffn2_w_down, v_final_norm):
    args = (ffn1_norm, ffn1_w_gate, ffn1_w_up, ffn1_w_down, mix_norm, w_in, pool_w, pool_scale, forget_bias, conv_w, conv_b, conv_ln_g, conv_ln_b, w_out, ffn2_norm, ffn2_w_gate, ffn2_w_up, ffn2_w_down, final_norm)
    ms = (m_ffn1_norm, m_ffn1_w_gate, m_ffn1_w_up, m_ffn1_w_down, m_mix_norm, m_w_in, m_pool_w, m_pool_scale, m_forget_bias, m_conv_w, m_conv_b, m_conv_ln_g, m_conv_ln_b, m_w_out, m_ffn2_norm, m_ffn2_w_gate, m_ffn2_w_up, m_ffn2_w_down, m_final_norm)
    vs = (v_ffn1_norm, v_ffn1_w_gate, v_ffn1_w_up, v_ffn1_w_down, v_mix_norm, v_w_in, v_pool_w, v_pool_scale, v_forget_bias, v_conv_w, v_conv_b, v_conv_ln_g, v_conv_ln_b, v_w_out, v_ffn2_norm, v_ffn2_w_gate, v_ffn2_w_up, v_ffn2_w_down, v_final_norm)
    P = dict(zip(NAMES, args))
    M = dict(zip(NAMES, ms))
    V = dict(zip(NAMES, vs))
    xi, yi, _ = _position()
    chip = 2 * xi + yi

    W = {n: P[n] for n in SMALL}
    W.update({n: [None] * P[n].shape[0] for n in BIG})
    shards = {n: [P[n][l].astype(MM) for l in range(P[n].shape[0])] for n in BIG}
    shards["conv_w"] = P["conv_w"]
    for (n, l), sh in zip(FIRST, _exchange(_GatherChipsSplit, [shards[n][l] for n, l in FIRST], "gather_first_weights")):
        W[n][l] = _from_shards(n, sh)

    loss_part, dx, G, recv = _forward_backward(x[0], loss_target[0], W, shards)
    loss = lax.psum(loss_part[0, 0], ("x", "y", "c"))

    small_shapes = [P[n].shape for n in SMALL] + [G["conv_w"].shape]
    small_parts = _gather_all(_pack_small([G[n] for n in SMALL] + [G["conv_w"]]), "gather_small_grads")
    small_sum = _sum8(small_parts, "sum_small_grads")
    nsmall = sum(-(-math.prod(s) // LANES) for s in small_shapes[:-1])
    nsmall_pad = nsmall + (-nsmall % 8)
    w_s, m_s, v_s = (_pack_small([D[n] for n in SMALL]) for D in (P, M, V))
    outs_small = _adamw(w_s, [small_sum[:nsmall_pad]], m_s, v_s, "adamw_small")
    res = {}
    for kind, buf in zip(("g", "d", "m", "v"), outs_small):
        for n, a in zip(SMALL, _unpack_small(buf, small_shapes[:-1])):
            res[(kind, n)] = a
    g_cw_full = _unpack_small(small_sum[nsmall:], [small_shapes[-1]])[0]
    g_cw = lax.dynamic_slice_in_dim(g_cw_full, chip * 64, 64, axis=2)
    outs_cw = _adamw(_pack_small([P["conv_w"]]), [_pack_small([g_cw])], _pack_small([M["conv_w"]]),
                     _pack_small([V["conv_w"]]), "adamw_conv_w")
    for kind, buf in zip(("g", "d", "m", "v"), outs_cw):
        res[(kind, "conv_w")] = _unpack_small(buf, [P["conv_w"].shape])[0]

    parts =[_sum_parts([_own_shard(n, G[n][l], chip) for l in range(P[n].shape[0])],
                        [recv[(n, l)] for l in range(P[n].shape[0])], "sum_" + n) for n in BIG]
    others = _exchange(_SwapCores, parts, "swap_core_grads")
    for n, ga, gb in zip(BIG, parts, others):
        shp = P[n].shape
        two_d = (shp[0] * shp[1], shp[2])
        outs = _adamw(P[n].reshape(two_d), [ga.reshape(two_d), gb.reshape(two_d)], M[n].reshape(two_d),
                      V[n].reshape(two_d), "adamw_" + n)
        for kind, a in zip(("g", "d", "m", "v"), outs):
            res[(kind, n)] = a.reshape(shp)

    return (loss, dx[None], *[res[("g", n)] for n in NAMES], *[res[("d", n)] for n in NAMES],
            *[res[("m", n)] for n in NAMES], *[res[("v", n)] for n in NAMES])
```

```python
import functools
import math

import jax
import jax.numpy as jnp
from jax import lax
from jax.experimental import pallas as pl
from jax.experimental.pallas import tpu as pltpu

F32 = jnp.float32
MM = jnp.bfloat16
NORM_EPS = 1e-6
HEADS = 8
HEAD_DIM = 64
POOL_WINDOWS = (2, 4, 8, 16)
CONV_K = 31
LANES = 128
VMEM_LIMIT = 56 * 2**20
FFN_BWD_ROWS = 256
FFN_COLS = 768
ATTN_FWD_TILE = 1024
WGRAD_COLS = 768
WGRAD_ACC_BYTES = 12 * 2**20

ADAM_LR = 0.001
ADAM_B1 = 0.9
ADAM_B2 = 0.999
ADAM_EPS = 1e-08
ADAM_WD = 0.01
ADAM_STEP = 10

MESH = pl.DeviceIdType.MESH
BS = pl.BlockSpec
SDS = jax.ShapeDtypeStruct
ANY = pl.BlockSpec(memory_space=pl.ANY)


def _dot(a, b):
    return jnp.dot(a, b, preferred_element_type=F32)


def _dot_nt(a, b):
    return lax.dot_general(a, b, (((1,), (1,)), ((), ())), preferred_element_type=F32)


def _dot_tn(a, b):
    return lax.dot_general(a, b, (((0,), (0,)), ((), ())), preferred_element_type=F32)


def _pc(body, name, grid, in_specs, out_specs, out_shape, scratch=()):
    return pl.pallas_call(
        body, out_shape=out_shape, grid=grid, in_specs=in_specs, out_specs=out_specs,
        scratch_shapes=list(scratch), name=name,
        compiler_params=pltpu.CompilerParams(
            dimension_semantics=("arbitrary",) * len(grid), vmem_limit_bytes=VMEM_LIMIT))


def _rms_fwd(x, g):
    r = lax.rsqrt(jnp.mean(x * x, axis=-1, keepdims=True) + NORM_EPS)
    xh = x * r
    return xh, r, xh * g


def _rms_bwd(dh, xh, r, g):
    dxh = dh * g
    dx = r * (dxh - xh * jnp.mean(dxh * xh, axis=-1, keepdims=True))
    return dx, jnp.sum(dh * xh, axis=0, keepdims=True)


def _sigmoid(x):
    return jax.nn.sigmoid(x)


def _ffn_fwd(x, g, wg, wu, wd, hosted=None):
    T, D = x.shape
    F = wg.shape[1]
    tm = min(512, T)
    nt = T // tm
    pieces = [(c0, min(FFN_COLS, F - c0)) for c0 in range(0, F, FFN_COLS)]
    h_in, h_out, h_shape, h_scratch = _hosted_specs(hosted)

    def body(*refs):
        i = pl.program_id(0)
        refs, finish = _hosted_edges(hosted, refs, 5, 3, i == 0, i == nt - 1)
        x_ref, g_ref, wg_ref, wu_ref, wd_ref, o_ref, a_ref, b_ref = refs
        xv = x_ref[...]
        h = _rms_fwd(xv, g_ref[...])[2].astype(MM)
        acc = jnp.zeros((tm, D), F32)
        for c0, w in pieces:
            a = _dot(h, wg_ref[:, c0:c0 + w])
            b = _dot(h, wu_ref[:, c0:c0 + w])
            a_ref[:, c0:c0 + w] = a.astype(a_ref.dtype)
            b_ref[:, c0:c0 + w] = b.astype(b_ref.dtype)
            acc = acc + _dot(((a * _sigmoid(a)) * b).astype(MM), wd_ref[c0:c0 + w, :])
        o_ref[...] = xv + 0.5 * acc
        finish()

    tok = lambda i: (i, 0)
    par = lambda i: (0, 0)
    resident = lambda shape: BS(shape, par, pipeline_mode=pl.Buffered(1))
    return _pc(
        body, "ffn_fwd" + ("_hosting" if hosted else ""), (nt,),
        [BS((tm, D), tok), BS((1, D), par), resident((D, F)), resident((D, F)), resident((F, D))] + h_in,
        [BS((tm, D), tok), BS((tm, F), tok), BS((tm, F), tok)] + h_out,
        [SDS((T, D), F32), SDS((T, F), MM), SDS((T, F), MM)] + h_shape,
        scratch=h_scratch)(x, g, wg, wu, wd, *(hosted[1] if hosted else []))


def _ffn_bwd(x, dout, g, a, b, wg, wu, wd, hosted=None):
    T, D = x.shape
    F = wg.shape[1]
    tm = min(FFN_BWD_ROWS, T)
    nt = T // tm
    pieces = [(c0, min(FFN_COLS, F - c0)) for c0 in range(0, F, FFN_COLS)]
    h_in, h_out, h_shape, h_scratch = _hosted_specs(hosted)

    def body(*refs):
        i = pl.program_id(0)
        refs, finish = _hosted_edges(hosted, refs, 8, 7, i == 0, i == nt - 1)
        (x_ref, do_ref, g_ref, a_ref, b_ref, wg_ref, wu_ref, wd_ref,
         dx_ref, h_ref, dy_ref, da_ref, db_ref, s_ref, dg_ref) = refs

        @pl.when(i == 0)
        def _():
            dg_ref[...] = jnp.zeros_like(dg_ref)

        gv = g_ref[...]
        xh, r, hg = _rms_fwd(x_ref[...], gv)
        h_ref[...] = hg.astype(h_ref.dtype)
        dy = (0.5 * do_ref[...]).astype(MM)
        dy_ref[...] = dy
        dh = jnp.zeros((tm, D), F32)
        for c0, w in pieces:
            a = a_ref[:, c0:c0 + w].astype(F32)
            b = b_ref[:, c0:c0 + w].astype(F32)
            ds = _dot_nt(dy, wd_ref[c0:c0 + w, :])
            sig = _sigmoid(a)
            sl = a * sig
            s_ref[:, c0:c0 + w] = (sl * b).astype(s_ref.dtype)
            db = (ds * sl).astype(MM)
            da = (ds * b * (sig * (1.0 + a * (1.0 - sig)))).astype(MM)
            da_ref[:, c0:c0 + w] = da
            db_ref[:, c0:c0 + w] = db
            dh = dh + _dot_nt(da, wg_ref[:, c0:c0 + w]) + _dot_nt(db, wu_ref[:, c0:c0 + w])
        dx, dg = _rms_bwd(dh, xh, r, gv)
        dx_ref[...] = do_ref[...] + dx
        dg_ref[...] += dg
        finish()

    tok = lambda i: (i, 0)
    par = lambda i: (0, 0)
    hid = BS((tm, F), tok)
    resident = lambda shape: BS(shape, par, pipeline_mode=pl.Buffered(1))
    return _pc(
        body, "ffn_bwd" + ("_hosting" if hosted else ""), (nt,),
        [BS((tm, D), tok), BS((tm, D), tok), BS((1, D), par), hid, hid,
         resident((D, F)), resident((D, F)), resident((F, D))] + h_in,
        [BS((tm, D), tok), BS((tm, D), tok), BS((tm, D), tok), hid, hid, hid, BS((1, D), par)] + h_out,
        [SDS((T, D), F32), SDS((T, D), MM), SDS((T, D), MM),
         SDS((T, F), MM), SDS((T, F), MM), SDS((T, F), MM), SDS((1, D), F32)] + h_shape,
        scratch=h_scratch,
    )(x, dout, g, a, b, wg, wu, wd, *(hosted[1] if hosted else []))


def _wgrad(a, b, name, hosted=None):
    T, K = a.shape
    N = b.shape[1]
    tt = min(512, T)
    tn = next(c for c in (N, 1408, 1280, 1024, 512, 256, 128) if N % c == 0 and K * c * 4 <= WGRAD_ACC_BYTES)
    pieces = [(c0, min(WGRAD_COLS, tn - c0)) for c0 in range(0, tn, WGRAD_COLS)]
    nn, nt = N // tn, T // tt
    h_in, h_out, h_shape, h_scratch = _hosted_specs(hosted)

    def body(*refs):
        n, t = pl.program_id(0), pl.program_id(1)
        refs, finish = _hosted_edges(hosted, refs, 2, 1, (n == 0) & (t == 0), (n == nn - 1) & (t == nt - 1))
        a_ref, b_ref, o_ref = refs

        @pl.when(t == 0)
        def _():
            o_ref[...] = jnp.zeros_like(o_ref)

        av = a_ref[...].astype(MM)
        for c0, w in pieces:
            o_ref[:, c0:c0 + w] += _dot_tn(av, b_ref[:, c0:c0 + w].astype(MM))
        finish()

    res = _pc(
        body, name + ("_hosting" if hosted else ""), (nn, nt),
        [BS((tt, K), lambda n, t: (t, 0)), BS((tt, tn), lambda n, t: (t, n))] + h_in,
        [BS((K, tn), lambda n, t: (0, n))] + h_out, [SDS((K, N), F32)] + h_shape,
        scratch=h_scratch)(a, b, *(hosted[1] if hosted else []))
    return res if hosted else res[0]


C_POOL, C_Q, C_K, C_V, C_CA, C_CG, C_ZF, C_END = 0, 256, 768, 1280, 1792, 2048, 2304, 2560


def _mix_in_fwd(x, g, w):
    T, D = x.shape
    tm = min(512, T)

    def body(x_ref, g_ref, w_ref, up_ref, q_ref, k_ref, v_ref, ca_ref, cg_ref, zf_ref):
        _, _, hg = _rms_fwd(x_ref[...], g_ref[...])
        p = _dot(hg.astype(MM), w_ref[...])
        up_ref[...] = p[:, C_POOL:C_Q]
        q_ref[...] = p[:, C_Q:C_K].astype(q_ref.dtype)
        k_ref[...] = p[:, C_K:C_V].astype(k_ref.dtype)
        v_ref[...] = p[:, C_V:C_CA].astype(v_ref.dtype)
        ca_ref[...] = p[:, C_CA:C_CG]
        cg_ref[...] = p[:, C_CG:C_ZF]
        zf_ref[...] = p[:, C_ZF:C_ZF + LANES]

    tok = lambda i: (i, 0)
    widths = (256, 512, 512, 512, 256, 256, 128)
    dtypes = (F32, MM, MM, MM, F32, F32, F32)
    return _pc(
        body, "mix_in_fwd", (T // tm,),
        [BS((tm, D), tok), BS((1, D), lambda i: (0, 0)), BS((D, C_END), lambda i: (0, 0))],
        [BS((tm, wd), tok) for wd in widths],
        [SDS((T, wd), dt) for wd, dt in zip(widths, dtypes)])(x, g, w)


def _mix_in_bwd(x, dout, g, w, dup, dq, dk, dv, dca, dcg, dzf):
    T, D = x.shape
    tm = min(512, T)

    def body(x_ref, do_ref, g_ref, w_ref, dup_ref, dq_ref, dk_ref, dv_ref, dca_ref, dcg_ref, dzf_ref,
             dx_ref, h_ref, dp_ref, dg_ref):
        @pl.when(pl.program_id(0) == 0)
        def _():
            dg_ref[...] = jnp.zeros_like(dg_ref)

        gv = g_ref[...]
        xh, r, hg = _rms_fwd(x_ref[...], gv)
        h_ref[...] = hg.astype(h_ref.dtype)
        for ref, lo, hi in ((dup_ref, C_POOL, C_Q), (dq_ref, C_Q, C_K), (dk_ref, C_K, C_V), (dv_ref, C_V, C_CA),
                            (dca_ref, C_CA, C_CG), (dcg_ref, C_CG, C_ZF), (dzf_ref, C_ZF, C_ZF + LANES)):
            dp_ref[:, lo:hi] = ref[...].astype(dp_ref.dtype)
        dp_ref[:, C_ZF + LANES:C_END] = jnp.zeros((tm, C_END - C_ZF - LANES), dp_ref.dtype)
        dh = _dot_nt(dp_ref[...], w_ref[...])
        dx, dg = _rms_bwd(dh, xh, r, gv)
        dx_ref[...] = do_ref[...] + dx
        dg_ref[...] += dg

    tok = lambda i: (i, 0)
    widths = (256, 512, 512, 512, 256, 256, 128)
    return _pc(
        body, "mix_in_bwd", (T // tm,),
        [BS((tm, D), tok), BS((tm, D), tok), BS((1, D), lambda i: (0, 0)), BS((D, C_END), lambda i: (0, 0))]
        + [BS((tm, wd), tok) for wd in widths],
        [BS((tm, D), tok), BS((tm, D), tok), BS((tm, C_END), tok), BS((1, D), lambda i: (0, 0))],
        [SDS((T, D), F32), SDS((T, D), MM), SDS((T, C_END), MM), SDS((1, D), F32)],
    )(x, dout, g, w, dup, dq, dk, dv, dca, dcg, dzf)


def _mix_out_fwd(x, ya, yb, yc, wo):
    T, D = x.shape
    tm = min(512, T)

    def body(x_ref, ya_ref, yb_ref, yc_ref, wo_ref, o_ref):
        o_ref[...] = (x_ref[...] + _dot(ya_ref[...].astype(MM), wo_ref[0:256, :])
                      + _dot(yb_ref[...].astype(MM), wo_ref[256:768, :])
                      + _dot(yc_ref[...].astype(MM), wo_ref[768:1024, :]))

    tok = lambda i: (i, 0)
    return _pc(
        body, "mix_out_fwd", (T // tm,),
        [BS((tm, D), tok), BS((tm, 256), tok), BS((tm, 512), tok), BS((tm, 256), tok), BS((D, D), lambda i: (0, 0))],
        BS((tm, D), tok), SDS((T, D), F32))(x, ya, yb, yc, wo)


def _mix_out_bwd(dx, wo):
    T, D = dx.shape
    tm = min(512, T)

    def body(dx_ref, wo_ref, dya_ref, dyb_ref, dyc_ref):
        dy = _dot_nt(dx_ref[...].astype(MM), wo_ref[...])
        dya_ref[...] = dy[:, 0:256]
        dyb_ref[...] = dy[:, 256:768]
        dyc_ref[...] = dy[:, 768:1024]

    tok = lambda i: (i, 0)
    return _pc(
        body, "mix_out_bwd", (T // tm,),
        [BS((tm, D), tok), BS((D, D), lambda i: (0, 0))],
        [BS((tm, 256), tok), BS((tm, 512), tok), BS((tm, 256), tok)],
        [SDS((T, 256), F32), SDS((T, 512), F32), SDS((T, 256), F32)])(dx, wo)


def _fgate_fwd(zf, bias):
    T = zf.shape[0]
    tc = min(256, T)

    def body(z_ref, b_ref, f_ref, carry):
        @pl.when(pl.program_id(0) == 0)
        def _():
            carry[...] = jnp.zeros_like(carry)

        z = z_ref[...] + b_ref[...]
        logf = jnp.minimum(z, 0.0) - jnp.log(1.0 + jnp.exp(-jnp.abs(z)))
        row = lax.broadcasted_iota(jnp.int32, (tc, tc), 0)
        col = lax.broadcasted_iota(jnp.int32, (tc, tc), 1)
        tri = (col <= row).astype(F32)
        f_ref[...] = jnp.dot(tri, logf, precision=lax.Precision.HIGHEST, preferred_element_type=F32) + carry[...]
        carry[...] += jnp.sum(logf, axis=0, keepdims=True)

    return _pc(
        body, "fgate_fwd", (T // tc,),
        [BS((tc, LANES), lambda i: (i, 0)), BS((1, LANES), lambda i: (0, 0))],
        BS((tc, LANES), lambda i: (i, 0)), SDS((T, LANES), F32),
        scratch=[pltpu.VMEM((1, LANES), F32)])(zf, bias)


def _fgate_bwd(zf, bias, dFq, dFk):
    T = zf.shape[0]
    tc = min(256, T)
    n = T // tc
    slabs = dFq.shape[0]

    def body(z_ref, b_ref, dfq_ref, dfk_ref, dz_ref, db_ref, carry):
        @pl.when(pl.program_id(0) == 0)
        def _():
            carry[...] = jnp.zeros_like(carry)
            db_ref[...] = jnp.zeros_like(db_ref)

        df = dfk_ref[...]
        for sl in range(slabs):
            df = df + dfq_ref[sl]
        row = lax.broadcasted_iota(jnp.int32, (tc, tc), 0)
        col = lax.broadcasted_iota(jnp.int32, (tc, tc), 1)
        tri = (col >= row).astype(F32)
        dlogf = jnp.dot(tri, df, precision=lax.Precision.HIGHEST, preferred_element_type=F32) + carry[...]
        carry[...] += jnp.sum(df, axis=0, keepdims=True)
        lane = lax.broadcasted_iota(jnp.int32, (1, LANES), 1)
        dz = jnp.where(lane < HEADS, dlogf * _sigmoid(-(z_ref[...] + b_ref[...])), 0.0)
        dz_ref[...] = dz
        db_ref[...] += jnp.sum(dz, axis=0, keepdims=True)

    rev = lambda i: (n - 1 - i, 0)
    return _pc(
        body, "fgate_bwd", (n,),
        [BS((tc, LANES), rev), BS((1, LANES), lambda i: (0, 0)), BS((slabs, tc, LANES), lambda i: (0, n - 1 - i, 0)),
         BS((tc, LANES), rev)],
        [BS((tc, LANES), rev), BS((1, LANES), lambda i: (0, 0))],
        [SDS((T, LANES), F32), SDS((1, LANES), F32)],
        scratch=[pltpu.VMEM((1, LANES), F32)])(zf, bias, dFq, dFk)


LOG2E = 1.4426950408889634


def _split3(x):
    hi = x.astype(MM)
    r1 = x - hi.astype(F32)
    mid = r1.astype(MM)
    return hi, mid, (r1 - mid.astype(F32)).astype(MM)


def _place(lane, base, cols):
    out = jnp.zeros((cols[0].shape[0], LANES), MM)
    for i, c in enumerate(cols):
        out = jnp.where(lane == base + i, c, out)
    return out


def _head_col(block, lane, h):
    return jnp.sum(jnp.where(lane == h, block, 0.0), axis=-1, keepdims=True)


def _own_lanes(lane, hh):
    return (lane < HEAD_DIM) if hh == 0 else (lane >= HEAD_DIM)


def _attn_k_side(k_ref, f_ref, kb_ref, hp, T, rows, lse_ones, v_ref=None, vb_ref=None):
    lane = lax.broadcasted_iota(jnp.int32, (1, LANES), 1)
    one = jnp.ones((rows, 1), MM)

    def chunk(c, _):
        r0 = pl.multiple_of(c * rows, rows)
        kp = k_ref[pl.ds(r0, rows), :]
        fblk = f_ref[pl.ds(r0, rows), :]
        for hh in range(2):
            hi, mid, lo = _split3(-_head_col(fblk, lane, 2 * hp + hh) * LOG2E)
            cols = [one, one, one, hi, mid, lo] + ([one, one, one] if lse_ones else [])
            bias = _place(lane, HEAD_DIM * (1 - hh), cols)
            kb_ref[hh, pl.ds(r0, rows), :] = jnp.where(_own_lanes(lane, hh), kp, bias)
            if vb_ref is not None:
                vb_ref[hh, pl.ds(r0, rows), :] = jnp.where(_own_lanes(lane, hh), v_ref[pl.ds(r0, rows), :],
                                                           jnp.ones((rows, LANES), MM))
        return 0

    lax.fori_loop(0, T // rows, chunk, 0)


def _attn_q_side(qp, fblk, lane, hp, scale, lse_blk=None):
    qc = qp.astype(F32) * (scale * LOG2E)
    qhi = qc.astype(MM)
    qlo = (qc - qhi.astype(F32)).astype(MM)
    one = jnp.ones((qp.shape[0], 1), MM)
    out = []
    for hh in range(2):
        cols = list(_split3(_head_col(fblk, lane, 2 * hp + hh) * LOG2E)) + [one, one, one]
        if lse_blk is not None:
            cols += list(_split3(-_head_col(lse_blk, lane, 2 * hp + hh)))
        bias = _place(lane, HEAD_DIM * (1 - hh), cols)
        own = _own_lanes(lane, hh)
        out.append(jnp.concatenate([jnp.where(own, qhi, jnp.zeros_like(qhi)), jnp.where(own, qlo, bias)], axis=1))
    return out


def _causal(tq, tk):
    return lax.broadcasted_iota(jnp.int32, (tq, tk), 1) <= lax.broadcasted_iota(jnp.int32, (tq, tk), 0)


def _hosted_specs(hosted):
    if hosted is None:
        return [], [], [], []
    kind, arrays = hosted
    n = len(arrays)
    return [ANY] * n, [ANY] * n, [kind.out_shape(a) for a in arrays], kind.scratch(n)


def _hosted_edges(hosted, refs, n_in, n_out, first, last, mid=None):
    if hosted is None:
        return refs, lambda: None
    kind, arrays = hosted
    n = len(arrays)
    nsem = len(kind.scratch(n))
    o0 = n_in + n + n_out
    ins, outs, sems = refs[n_in:n_in + n], refs[o0:o0 + n], refs[len(refs) - nsem:]
    relayed = mid is not None and hasattr(kind, "relay")

    @pl.when(first)
    def _():
        kind.start(ins, outs, *sems)

    if relayed:
        @pl.when(mid)
        def _():
            kind.relay(ins, outs, *sems)

    def finish():
        @pl.when(last)
        def _():
            kind.wait(ins, outs, *sems, **({"relayed": True} if relayed else {}))

    return refs[:n_in] + refs[n_in + n:o0] + refs[o0 + n:len(refs) - nsem], finish


def _attn_fwd(q, k, v, F, hosted=None):
    T = q.shape[0]
    tq = min(ATTN_FWD_TILE, T)
    tk = tq
    nq = T // tq
    scale = 1.0 / math.sqrt(HEAD_DIM)
    h_in, h_out, h_shape, h_scratch = _hosted_specs(hosted)

    def body(*refs):
        hp, ib = pl.program_id(0), pl.program_id(1)
        refs, finish = _hosted_edges(hosted, refs, 5, 2, (hp == 0) & (ib == 0), (hp == HEADS // 2 - 1) & (ib == nq - 1),
                                     mid=(hp == HEADS // 2 - 1) & (ib == 0))
        q_ref, k_ref, v_ref, fq_ref, f_ref, o_ref, lse_ref, kb_ref, vb_ref = refs
        lane = lax.broadcasted_iota(jnp.int32, (1, LANES), 1)

        @pl.when(ib == 0)
        def _():
            _attn_k_side(k_ref, f_ref, kb_ref, hp, T, min(512, T), False, v_ref, vb_ref)

        qa = _attn_q_side(q_ref[...], fq_ref[...], lane, hp, scale)

        def tile(jb, carry, masked):
            off = pl.multiple_of(jb * tk, tk)
            kp = k_ref[pl.ds(off, tk), :]
            new = []
            for hh in range(2):
                m, acc = carry[hh]
                s = _dot_nt(qa[hh], jnp.concatenate([kp, kb_ref[hh, pl.ds(off, tk), :]], axis=1))
                if masked:
                    s = jnp.where(_causal(tq, tk), s, -jnp.inf)
                m2 = jnp.maximum(m, jnp.max(s, axis=-1, keepdims=True))
                p = jnp.exp2(s - m2)
                new.append((m2, acc * jnp.exp2(m - m2) + _dot(p.astype(MM), vb_ref[hh, pl.ds(off, tk), :])))
            return tuple(new)

        init = tuple((jnp.full((tq, 1), -jnp.inf, F32), jnp.zeros((tq, LANES), F32)) for _ in range(2))
        carry = lax.fori_loop(0, ib, lambda jb, c: tile(jb, c, False), init)
        (m0, a0), (m1, a1) = tile(ib, carry, True)
        l0, l1 = a0[:, HEAD_DIM:HEAD_DIM + 1], a1[:, 0:1]
        o_ref[...] = jnp.where(lane < HEAD_DIM, a0 / l0, a1 / l1)
        lse_ref[...] = jnp.where(lane == 2 * hp, m0 + jnp.log2(l0), jnp.where(lane == 2 * hp + 1, m1 + jnp.log2(l1), 0.0))
        finish()

    blk = lambda h, i: (i, h)
    full = lambda h, i: (0, h)
    return _pc(
        body, "attn_fwd" + ("_hosting" if hosted else ""), (HEADS // 2, nq),
        [BS((tq, LANES), blk), BS((T, LANES), full), BS((T, LANES), full), BS((tq, LANES), lambda h, i: (i, 0)),
         BS((T, LANES), lambda h, i: (0, 0))] + h_in,
        [BS((tq, LANES), blk), BS((None, tq, LANES), lambda h, i: (h, i, 0))] + h_out,
        [SDS((T, HEADS * HEAD_DIM), F32), SDS((HEADS // 2, T, LANES), F32)] + h_shape,
        scratch=[pltpu.VMEM((2, T, LANES), MM)] * 2 + h_scratch)(q, k, v, F, F, *(hosted[1] if hosted else []))


def _attn_bwd(q, k, v, F, o, lse, do, hosted=None):
    T = q.shape[0]
    tq = min(512, T)
    tk = tq
    nq = T // tq
    scale = 1.0 / math.sqrt(HEAD_DIM)
    h_in, h_out, h_shape, h_scratch = _hosted_specs(hosted)

    def body(*refs):
        hp, ib = pl.program_id(0), pl.program_id(1)
        refs, finish = _hosted_edges(hosted, refs, 8, 5, (hp == 0) & (ib == 0), (hp == HEADS // 2 - 1) & (ib == nq - 1))
        (q_ref, k_ref, v_ref, fq_ref, f_ref, o_ref, lse_ref, do_ref,
         dq_ref, dk_ref, dv_ref, dfq_ref, dfk_ref, kb_ref) = refs
        lane = lax.broadcasted_iota(jnp.int32, (1, LANES), 1)

        @pl.when(ib == 0)
        def _():
            _attn_k_side(k_ref, f_ref, kb_ref, hp, T, tk, True)
            dk_ref[...] = jnp.zeros_like(dk_ref)
            dv_ref[...] = jnp.zeros_like(dv_ref)
            dfk_ref[...] = jnp.zeros_like(dfk_ref)

        qp = q_ref[...]
        qa = _attn_q_side(qp, fq_ref[...], lane, hp, scale, lse_ref[...])
        dob = do_ref[...].astype(MM)
        dprod = dob.astype(F32) * o_ref[...]
        qs = (qp.astype(F32) * scale).astype(MM)
        heads = []
        for hh in range(2):
            own = _own_lanes(lane, hh)
            heads.append((jnp.where(own, dob, jnp.zeros_like(dob)), jnp.where(own, qs, jnp.zeros_like(qs)),
                          jnp.sum(jnp.where(own, dprod, 0.0), axis=-1, keepdims=True)))

        def tile(jb, carry, masked):
            off = pl.multiple_of(jb * tk, tk)
            kp = k_ref[pl.ds(off, tk), :]
            vp = v_ref[pl.ds(off, tk), :]
            new = []
            dv_t = jnp.zeros((tk, LANES), F32)
            dk_t = jnp.zeros((tk, LANES), F32)
            for hh in range(2):
                dq, rs = carry[hh]
                dom, qm, delta = heads[hh]
                p = jnp.exp2(_dot_nt(qa[hh], jnp.concatenate([kp, kb_ref[hh, pl.ds(off, tk), :]], axis=1)))
                if masked:
                    p = jnp.where(_causal(tq, tk), p, 0.0)
                ds = p * (_dot_nt(dom, vp) - delta)
                dsb = ds.astype(MM)
                dv_t = dv_t + _dot_tn(p.astype(MM), dom)
                dk_t = dk_t + _dot_tn(dsb, qm)
                dfk_ref[jb, pl.ds(hh, 1), :] -= jnp.sum(ds, axis=0, keepdims=True)
                new.append((dq + _dot(dsb, kp), rs + jnp.sum(ds, axis=-1, keepdims=True)))
            dv_ref[pl.ds(off, tk), :] += dv_t
            dk_ref[pl.ds(off, tk), :] += dk_t
            return tuple(new)

        init = tuple((jnp.zeros((tq, LANES), F32), jnp.zeros((tq, 1), F32)) for _ in range(2))
        carry = lax.fori_loop(0, ib, lambda jb, c: tile(jb, c, False), init)
        (dq0, rs0), (dq1, rs1) = tile(ib, carry, True)
        dq_ref[...] = jnp.where(lane < HEAD_DIM, dq0, dq1) * scale
        dfq_ref[...] = jnp.where(lane == 2 * hp, rs0, jnp.where(lane == 2 * hp + 1, rs1, 0.0))
        finish()

    blk = lambda h, i: (i, h)
    full = lambda h, i: (0, h)
    slab = BS((None, tq, LANES), lambda h, i: (h, i, 0))
    return _pc(
        body, "attn_bwd" + ("_hosting" if hosted else ""), (HEADS // 2, nq),
        [BS((tq, LANES), blk), BS((T, LANES), full), BS((T, LANES), full), BS((tq, LANES), lambda h, i: (i, 0)),
         BS((T, LANES), lambda h, i: (0, 0)), BS((tq, LANES), blk), slab, BS((tq, LANES), blk)] + h_in,
        [BS((tq, LANES), blk), BS((T, LANES), full), BS((T, LANES), full), slab,
         BS((None, nq, 2, tk), lambda h, i: (h, 0, 0, 0))] + h_out,
        [SDS((T, HEADS * HEAD_DIM), F32)] * 3 + [SDS((HEADS // 2, T, LANES), F32), SDS((HEADS // 2, nq, 2, tk), F32)]
        + h_shape,
        scratch=[pltpu.VMEM((2, T, LANES), MM)] + h_scratch,
    )(q, k, v, F, F, o, lse, do, *(hosted[1] if hosted else []))


POOL_HALO = 16
CONV_HALO = 32


def _group_select(lane, v0, v1, v2, v3):
    return jnp.where(lane < 64, v0, jnp.where(lane < 128, v1, jnp.where(lane < 192, v2, v3)))


def _roll_down(x, k):
    return x if k == 0 else pltpu.roll(x, k, 0)


def _roll_up(x, k):
    return x if k == 0 else pltpu.roll(x, x.shape[0] - k, 0)


def _pool_terms(u, u_prev, tile, tm):
    ext = jnp.concatenate([u_prev, u], axis=0)
    s2 = ext + _roll_down(ext, 1)
    s4 = s2 + _roll_down(s2, 2)
    s8 = s4 + _roll_down(s4, 4)
    s16 = s8 + _roll_down(s8, 8)
    lane = lax.broadcasted_iota(jnp.int32, (1, 256), 1)
    ws = _group_select(lane, s2, s4, s8, s16)[POOL_HALO:, :]
    wlen = _group_select(lane, 2.0, 4.0, 8.0, 16.0).astype(F32)
    return ws / _pool_count(tile, tm, tm, wlen) - u


def _pool_count(tile, tm, rows, wlen):
    t = (tile * tm + 1 + lax.broadcasted_iota(jnp.int32, (rows, 1), 0)).astype(F32)
    return jnp.minimum(t, wlen)


def _layer_norm(y, lg, lb):
    mu = jnp.mean(y, axis=-1, keepdims=True)
    yc = y - mu
    rstd = lax.rsqrt(jnp.mean(yc * yc, axis=-1, keepdims=True) + NORM_EPS)
    yh = yc * rstd
    return yh, rstd, yh * lg + lb


def _halo_specs(tm, T, halo, prev):
    per = tm // halo
    if prev:
        return BS((halo, 256), lambda i: (jnp.maximum(i * per - 1, 0), 0))
    return BS((halo, 256), lambda i: (jnp.minimum((i + 1) * per, T // halo - 1), 0))


def _local_fwd(up, ca, cg, bd, pscale, cw, cb, lg, lb):
    T = up.shape[0]
    tm = min(512, T)

    def body(up_ref, uph_ref, ca_ref, cah_ref, cg_ref, cgh_ref, bd_ref, ps_ref, cw_ref, cb_ref, lg_ref, lb_ref,
             ya_ref, yc_ref, u_ref, y_ref):
        i = pl.program_id(0)
        first = i == 0
        pooled = _pool_terms(up_ref[...], jnp.where(first, 0.0, uph_ref[...]), i, tm)
        ya_ref[...] = (_dot(pooled.astype(MM), bd_ref[...]) * ps_ref[...]).astype(ya_ref.dtype)

        u = ca_ref[...] * _sigmoid(cg_ref[...])
        uh = jnp.where(first, 0.0, cah_ref[...] * _sigmoid(cgh_ref[...]))
        ext = jnp.concatenate([uh, u], axis=0)
        y = jnp.zeros((tm, 256), F32) + cb_ref[...]
        for kk in range(CONV_K):
            y = y + cw_ref[kk:kk + 1, :] * _roll_up(ext, CONV_HALO - (CONV_K - 1) + kk)[:tm, :]
        _, _, z = _layer_norm(y, lg_ref[...], lb_ref[...])
        yc_ref[...] = (z * _sigmoid(z)).astype(yc_ref.dtype)
        u_ref[...] = u
        y_ref[...] = y

    tok = lambda i: (i, 0)
    par = lambda i: (0, 0)
    t256 = BS((tm, 256), tok)
    return _pc(
        body, "local_fwd", (T // tm,),
        [t256, _halo_specs(tm, T, POOL_HALO, True), t256, _halo_specs(tm, T, CONV_HALO, True),
         t256, _halo_specs(tm, T, CONV_HALO, True),
         BS((256, 256), par), BS((1, 256), par), BS((32, 256), par), BS((1, 256), par), BS((1, 256), par),
         BS((1, 256), par)],
        [t256, t256, t256, t256],
        [SDS((T, 256), MM), SDS((T, 256), MM), SDS((T, 256), F32), SDS((T, 256), F32)],
    )(up, up, ca, ca, cg, cg, bd, pscale, cw, cb, lg, lb)


def _local_bwd(up, dya, ca, cg, u, y, dyc, bd, pscale, cw, lg, lb):
    T = up.shape[0]
    tm = min(512, T)
    n = T // tm

    def body(up_ref, uph_ref, dya_ref, dyan_ref, ca_ref, cg_ref, u_ref, uh_ref, y_ref, yn_ref, dyc_ref, dycn_ref,
             bd_ref, ps_ref, cw_ref, lg_ref, lb_ref,
             dup_ref, dca_ref, dcg_ref, dbd_ref, dps_ref, dcw_ref, dcb_ref, dlg_ref, dlb_ref):
        i = pl.program_id(0)
        first = i == 0
        last = i == n - 1

        @pl.when(first)
        def _():
            for ref in (dbd_ref, dps_ref, dcw_ref, dcb_ref, dlg_ref, dlb_ref):
                ref[...] = jnp.zeros_like(ref)

        ps = ps_ref[...]
        pooled = _pool_terms(up_ref[...], jnp.where(first, 0.0, uph_ref[...]), i, tm).astype(MM)
        dya_t = dya_ref[...]
        dps_ref[...] += jnp.sum(dya_t * _dot(pooled, bd_ref[...]), axis=0, keepdims=True)
        dm = (jnp.concatenate([dya_t, jnp.where(last, 0.0, dyan_ref[...])], axis=0) * ps).astype(MM)
        dbd_ref[...] += _dot_tn(pooled, dm[:tm, :])
        dpool = _dot_nt(dm, bd_ref[...])
        lane = lax.broadcasted_iota(jnp.int32, (1, 256), 1)
        wlen = _group_select(lane, 2.0, 4.0, 8.0, 16.0).astype(F32)
        e = dpool / _pool_count(i, tm, tm + POOL_HALO, wlen)
        f2 = e + _roll_up(e, 1)
        f4 = f2 + _roll_up(f2, 2)
        f8 = f4 + _roll_up(f4, 4)
        f16 = f8 + _roll_up(f8, 8)
        dup_ref[...] = _group_select(lane, f2, f4, f8, f16)[:tm, :] - dpool[:tm, :]

        lgv = lg_ref[...]
        yext = jnp.concatenate([y_ref[...], yn_ref[...]], axis=0)
        dyc = jnp.concatenate([dyc_ref[...], jnp.where(last, 0.0, dycn_ref[...])], axis=0)
        yh, rstd, z = _layer_norm(yext, lgv, lb_ref[...])
        sig = _sigmoid(z)
        dz = dyc * (sig * (1.0 + z * (1.0 - sig)))
        dlg_ref[...] += jnp.sum((dz * yh)[:tm, :], axis=0, keepdims=True)
        dlb_ref[...] += jnp.sum(dz[:tm, :], axis=0, keepdims=True)
        dyh = dz * lgv
        dy = rstd * (dyh - jnp.mean(dyh, axis=-1, keepdims=True) - yh * jnp.mean(dyh * yh, axis=-1, keepdims=True))
        dy_t = dy[:tm, :]
        dcb_ref[...] += jnp.sum(dy_t, axis=0, keepdims=True)
        uext = jnp.concatenate([jnp.where(first, 0.0, uh_ref[...]), u_ref[...]], axis=0)
        du = jnp.zeros((tm, 256), F32)
        for kk in range(CONV_K):
            shifted = _roll_up(uext, CONV_HALO - (CONV_K - 1) + kk)[:tm, :]
            dcw_ref[kk:kk + 1, :] += jnp.sum(dy_t * shifted, axis=0, keepdims=True)
            du = du + cw_ref[kk:kk + 1, :] * _roll_up(dy, CONV_K - 1 - kk)[:tm, :]
        sg = _sigmoid(cg_ref[...])
        dca_ref[...] = du * sg
        dcg_ref[...] = du * ca_ref[...] * sg * (1.0 - sg)

    tok = lambda i: (i, 0)
    par = lambda i: (0, 0)
    t256 = BS((tm, 256), tok)
    p1 = BS((1, 256), par)
    return _pc(
        body, "local_bwd", (n,),
        [t256, _halo_specs(tm, T, POOL_HALO, True), t256, _halo_specs(tm, T, POOL_HALO, False), t256, t256,
         t256, _halo_specs(tm, T, CONV_HALO, True), t256, _halo_specs(tm, T, CONV_HALO, False),
         t256, _halo_specs(tm, T, CONV_HALO, False),
         BS((256, 256), par), p1, BS((32, 256), par), p1, p1],
        [t256, t256, t256, BS((256, 256), par), p1, BS((32, 256), par), p1, p1, p1],
        [SDS((T, 256), F32)] * 3 + [SDS((256, 256), F32), SDS((1, 256), F32), SDS((32, 256), F32)]
        + [SDS((1, 256), F32)] * 3,
    )(up, up, dya, dya, ca, cg, u, u, y, y, dyc, dyc, bd, pscale, cw, lg, lb)


def _head(x, g, target):
    T, D = x.shape
    tm = min(512, T)

    def body(x_ref, g_ref, t_ref, loss_ref, dx_ref, dg_ref):
        @pl.when(pl.program_id(0) == 0)
        def _():
            loss_ref[...] = jnp.zeros_like(loss_ref)
            dg_ref[...] = jnp.zeros_like(dg_ref)

        gv = g_ref[...]
        xh, r, yv = _rms_fwd(x_ref[...], gv)
        err = yv - t_ref[...]
        loss_ref[...] += 0.5 * jnp.sum(jnp.mean(err * err, axis=-1, keepdims=True), axis=0, keepdims=True)
        dx, dg = _rms_bwd(err * (1.0 / D), xh, r, gv)
        dx_ref[...] = dx
        dg_ref[...] += dg

    tok = lambda i: (i, 0)
    par = lambda i: (0, 0)
    return _pc(
        body, "head", (T // tm,),
        [BS((tm, D), tok), BS((1, D), par), BS((tm, D), tok)],
        [BS((1, LANES), par), BS((tm, D), tok), BS((1, D), par)],
        [SDS((1, LANES), F32), SDS((T, D), F32), SDS((1, D), F32)])(x, g, target)


def _adamw(w, gs, m, v, name):
    R, C = w.shape
    tr = R
    for cand in (512, 256, 128, 64, 32, 16, 8):
        if R % cand == 0:
            tr = cand
            break
    ng = len(gs)

    def body(*refs):
        w_ref, g_refs, m_ref, v_ref = refs[0], refs[1:1 + ng], refs[1 + ng], refs[2 + ng]
        g_ref, d_ref, m2_ref, v2_ref = refs[3 + ng:]
        g = g_refs[0][...]
        for r in g_refs[1:]:
            g = g + r[...]
        m2 = ADAM_B1 * m_ref[...] + (1.0 - ADAM_B1) * g
        v2 = ADAM_B2 * v_ref[...] + (1.0 - ADAM_B2) * jnp.square(g)
        m_hat = m2 / (1.0 - ADAM_B1 ** ADAM_STEP)
        v_hat = v2 / (1.0 - ADAM_B2 ** ADAM_STEP)
        g_ref[...] = g
        d_ref[...] = -ADAM_LR * (m_hat / (jnp.sqrt(v_hat) + ADAM_EPS) + ADAM_WD * w_ref[...])
        m2_ref[...] = m2
        v2_ref[...] = v2

    blk = BS((tr, C), lambda i: (i, 0))
    return _pc(body, name, (R // tr,), [blk] * (3 + ng), [blk] * 4, [SDS((R, C), F32)] * 4)(w, *gs, m, v)


def _sum_parts(owns, recvs, name):
    L = len(owns)
    R, C = owns[0].shape
    tr = next(t for t in (512, 256, 128, 64, 32, 16) if R % t == 0)

    def body(*refs):
        l = pl.program_id(0)
        s_ref = refs[2 * L]
        for ll in range(L):
            @pl.when(l == ll)
            def _(o_ref=refs[ll], r_ref=refs[L + ll]):
                s_ref[...] = ((o_ref[...] + r_ref[0].astype(F32)) + r_ref[1].astype(F32)) + r_ref[2].astype(F32)

    own_specs = [BS((tr, C), lambda l, i, ll=ll: (jnp.where(l == ll, i, 0), 0)) for ll in range(L)]
    recv_specs = [BS((3, tr, C), lambda l, i, ll=ll: (0, jnp.where(l == ll, i, 0), 0)) for ll in range(L)]
    return _pc(body, name, (L, R // tr), own_specs + recv_specs,
               BS((None, tr, C), lambda l, i: (l, i, 0)), SDS((L, R, C), F32))(*owns, *recvs)


def _sum8(parts, name):
    _, R, C = parts.shape

    def body(p_ref, s_ref):
        acc = p_ref[0]
        for d in range(1, 8):
            acc = acc + p_ref[d]
        s_ref[...] = acc

    return _pc(body, name, (1,), [BS((8, R, C), lambda i: (0, 0, 0))], BS((R, C), lambda i: (0, 0)),
               SDS((R, C), F32))(parts)


def _position():
    return lax.axis_index("x"), lax.axis_index("y"), lax.axis_index("c")


CHIP_FLIPS = ((1, 0), (0, 1), (1, 1))


class _GatherChips:
    @staticmethod
    def scratch(n):
        return [pltpu.SemaphoreType.DMA((3 * n,)), pltpu.SemaphoreType.DMA((3 * n,)), pltpu.SemaphoreType.DMA((n,))]

    @staticmethod
    def out_shape(block):
        return SDS((4,) + tuple(block.shape), block.dtype)

    @staticmethod
    def _copies(ins, outs, send_sems, recv_sems, local_sems, arrivals):
        x, y, c = _position()
        local, remote = [], []
        for i, (in_ref, out_ref) in enumerate(zip(ins, outs)):
            local.append(pltpu.make_async_copy(in_ref, out_ref.at[2 * x + y], local_sems.at[i]))
            for k, (fx, fy) in enumerate(CHIP_FLIPS):
                slot = 2 * (x ^ fx) + (y ^ fy) if arrivals else 2 * x + y
                remote.append(pltpu.make_async_remote_copy(
                    src_ref=in_ref, dst_ref=out_ref.at[slot], send_sem=send_sems.at[3 * i + k],
                    recv_sem=recv_sems.at[3 * i + k], device_id=(x ^ fx, y ^ fy, c), device_id_type=MESH))
        return local, remote

    @classmethod
    def start(cls, ins, outs, *sems):
        local, sends = cls._copies(ins, outs, *sems, arrivals=False)
        for cp in local + sends:
            cp.start()

    @classmethod
    def wait(cls, ins, outs, *sems):
        local, arrivals = cls._copies(ins, outs, *sems, arrivals=True)
        for cp in arrivals:
            cp.wait_recv()
        for cp in arrivals:
            cp.wait_send()
        for cp in local:
            cp.wait()


class _GatherChipsSplit(_GatherChips):
    @staticmethod
    def scratch(n):
        return [pltpu.SemaphoreType.DMA((6 * n,)), pltpu.SemaphoreType.DMA((6 * n,)), pltpu.SemaphoreType.DMA((n,))]

    @staticmethod
    def _half(ref, which):
        rows = ref.shape[0] // 2
        return ref.at[pl.ds(pl.multiple_of(which * rows, 16), rows)]

    @staticmethod
    def _local(ins, outs, local_sems):
        x, y, _ = _position()
        return [pltpu.make_async_copy(in_ref, out_ref.at[2 * x + y], local_sems.at[i])
                for i, (in_ref, out_ref) in enumerate(zip(ins, outs))]

    @classmethod
    def _between_chips(cls, ins, outs, send_sems, recv_sems, arrivals):
        x, y, c = _position()
        return [
            pltpu.make_async_remote_copy(
                src_ref=cls._half(in_ref, c),
                dst_ref=cls._half(out_ref.at[2 * (x ^ fx) + (y ^ fy) if arrivals else 2 * x + y], c),
                send_sem=send_sems.at[6 * i + k], recv_sem=recv_sems.at[6 * i + k],
                device_id=(x ^ fx, y ^ fy, c), device_id_type=MESH)
            for i, (in_ref, out_ref) in enumerate(zip(ins, outs)) for k, (fx, fy) in enumerate(CHIP_FLIPS)]

    @classmethod
    def _between_cores(cls, outs, send_sems, recv_sems, arrivals):
        x, y, c = _position()
        copies = []
        for i, out_ref in enumerate(outs):
            for k, (fx, fy) in enumerate(CHIP_FLIPS):
                half = cls._half(out_ref.at[2 * (x ^ fx) + (y ^ fy)], 1 - c if arrivals else c)
                copies.append(pltpu.make_async_remote_copy(
                    src_ref=half, dst_ref=half, send_sem=send_sems.at[6 * i + 3 + k],
                    recv_sem=recv_sems.at[6 * i + 3 + k], device_id=(x, y, 1 - c), device_id_type=MESH))
        return copies

    @classmethod
    def start(cls, ins, outs, send_sems, recv_sems, local_sems):
        for cp in cls._local(ins, outs, local_sems) + cls._between_chips(ins, outs, send_sems, recv_sems, False):
            cp.start()

    @classmethod
    def relay(cls, ins, outs, send_sems, recv_sems, local_sems):
        arrivals = cls._between_chips(ins, outs, send_sems, recv_sems, True)
        onward = cls._between_cores(outs, send_sems, recv_sems, False)
        for cp, nxt in zip(arrivals, onward):
            cp.wait_recv()
            nxt.start()

    @classmethod
    def wait(cls, ins, outs, send_sems, recv_sems, local_sems, relayed=False):
        if not relayed:
            cls.relay(ins, outs, send_sems, recv_sems, local_sems)
        for cp in cls._between_cores(outs, send_sems, recv_sems, True):
            cp.wait_recv()
        for cp in (cls._between_chips(ins, outs, send_sems, recv_sems, True)
                   + cls._between_cores(outs, send_sems, recv_sems, False)):
            cp.wait_send()
        for cp in cls._local(ins, outs, local_sems):
            cp.wait()


class _Symmetric:
    @classmethod
    def start(cls, ins, outs, *sems):
        for cp in cls._copies(ins, outs, *sems):
            cp.start()

    @classmethod
    def wait(cls, ins, outs, *sems):
        copies = cls._copies(ins, outs, *sems)
        for cp in copies:
            cp.wait_recv()
        for cp in copies:
            cp.wait_send()


class _ScatterChips(_Symmetric):
    @staticmethod
    def scratch(n):
        return [pltpu.SemaphoreType.DMA((3 * n,)), pltpu.SemaphoreType.DMA((3 * n,))]

    @staticmethod
    def out_shape(parts):
        return SDS((3,) + tuple(parts.shape[1:]), parts.dtype)

    @staticmethod
    def _copies(ins, outs, send_sems, recv_sems):
        x, y, c = _position()
        return [
            pltpu.make_async_remote_copy(
                src_ref=in_ref.at[2 * (x ^ fx) + (y ^ fy)], dst_ref=out_ref.at[k],
                send_sem=send_sems.at[3 * i + k], recv_sem=recv_sems.at[3 * i + k],
                device_id=(x ^ fx, y ^ fy, c), device_id_type=MESH)
            for i, (in_ref, out_ref) in enumerate(zip(ins, outs)) for k, (fx, fy) in enumerate(CHIP_FLIPS)]


class _SwapCores(_Symmetric):
    @staticmethod
    def scratch(n):
        return [pltpu.SemaphoreType.DMA((n,)), pltpu.SemaphoreType.DMA((n,))]

    @staticmethod
    def out_shape(block):
        return SDS(block.shape, block.dtype)

    @staticmethod
    def _copies(ins, outs, send_sems, recv_sems):
        x, y, c = _position()
        return [
            pltpu.make_async_remote_copy(
                src_ref=in_ref, dst_ref=out_ref, send_sem=send_sems.at[i], recv_sem=recv_sems.at[i],
                device_id=(x, y, 1 - c), device_id_type=MESH)
            for i, (in_ref, out_ref) in enumerate(zip(ins, outs))]


def _exchange(kind, arrays, name):
    n = len(arrays)

    def body(*refs):
        ins, outs, sems = refs[:n], refs[n:2 * n], refs[2 * n:]
        kind.start(ins, outs, *sems)
        kind.wait(ins, outs, *sems)

    return pl.pallas_call(body, out_shape=[kind.out_shape(a) for a in arrays], in_specs=[ANY] * n,
                          out_specs=[ANY] * n, name=name, scratch_shapes=kind.scratch(n))(*arrays)


def _gather_all(block, name):
    R, C = block.shape
    flips = [(fx, fy, fc) for fx in (0, 1) for fy in (0, 1) for fc in (0, 1)][1:]

    def body(in_ref, out_ref, send_sems, recv_sems, local_sem):
        x, y, c = _position()
        mine = out_ref.at[4 * x + 2 * y + c]
        local = pltpu.make_async_copy(in_ref, mine, local_sem)
        local.start()
        copies = [
            pltpu.make_async_remote_copy(
                src_ref=in_ref, dst_ref=mine, send_sem=send_sems.at[k], recv_sem=recv_sems.at[k],
                device_id=(x ^ fx, y ^ fy, c ^ fc), device_id_type=MESH)
            for k, (fx, fy, fc) in enumerate(flips)]
        for cp in copies:
            cp.start()
        for k, (fx, fy, fc) in enumerate(flips):
            theirs = out_ref.at[4 * (x ^ fx) + 2 * (y ^ fy) + (c ^ fc)]
            pltpu.make_async_remote_copy(
                src_ref=in_ref, dst_ref=theirs, send_sem=send_sems.at[k], recv_sem=recv_sems.at[k],
                device_id=(x ^ fx, y ^ fy, c ^ fc), device_id_type=MESH).wait_recv()
        for cp in copies:
            cp.wait_send()
        local.wait()

    return pl.pallas_call(
        body, out_shape=SDS((8, R, C), block.dtype), in_specs=[ANY], out_specs=ANY, name=name,
        scratch_shapes=[pltpu.SemaphoreType.DMA((7,)), pltpu.SemaphoreType.DMA((7,)), pltpu.SemaphoreType.DMA(())])(block)


BIG = ("ffn1_w_gate", "ffn1_w_up", "ffn1_w_down", "w_in", "w_out", "ffn2_w_gate", "ffn2_w_up", "ffn2_w_down")
COL_SHARDED = ("ffn1_w_gate", "ffn1_w_up", "w_in", "ffn2_w_gate", "ffn2_w_up")
FIRST = tuple((n, 0) for n in ("ffn1_w_gate", "ffn1_w_up", "ffn1_w_down"))
LATE = tuple((n, 0) for n in ("w_out", "ffn2_w_gate", "ffn2_w_up", "ffn2_w_down")) + tuple((n, 1) for n in BIG)


def _to_shards(name, full):
    r, c = full.shape
    if name in COL_SHARDED:
        return full.reshape(r, 4, c // 4).transpose(1, 0, 2)
    return full.reshape(4, r // 4, c)


def _own_shard(name, full, chip):
    r, c = full.shape
    if name in COL_SHARDED:
        return lax.dynamic_slice_in_dim(full, chip * (c // 4), c // 4, axis=1)
    return lax.dynamic_slice_in_dim(full, chip * (r // 4), r // 4, axis=0)


def _from_shards(name, sh):
    _, r, c = sh.shape
    if name in COL_SHARDED:
        return sh.transpose(1, 0, 2).reshape(r, 4 * c)
    return sh.reshape(4 * r, c)


def _pad_w_in(w):
    return jnp.concatenate([w[:, :1792], w[:, 1800:2312], w[:, 1792:1800], jnp.zeros((w.shape[0], 248), w.dtype)], axis=1)


def _unpad_w_in(g):
    return jnp.concatenate([g[:, :1792], g[:, 2304:2312], g[:, 1792:2304]], axis=1)


def _block_diag(pw):
    out = jnp.zeros((256, 256), pw.dtype)
    for gidx in range(4):
        out = lax.dynamic_update_slice(out, pw[gidx], (64 * gidx, 64 * gidx))
    return out


SMALL = ("ffn1_norm", "mix_norm", "pool_w", "pool_scale", "forget_bias", "conv_b", "conv_ln_g", "conv_ln_b",
         "ffn2_norm", "final_norm")


def _pack_small(arrs):
    rows = []
    for a in arrs:
        flat = a.reshape(-1)
        flat = jnp.pad(flat, (0, -flat.shape[0] % LANES))
        rows.append(flat.reshape(-1, LANES))
    total = sum(r.shape[0] for r in rows)
    if total % 8:
        rows.append(jnp.zeros((-total % 8, LANES), F32))
    return jnp.concatenate(rows, axis=0)


def _unpack_small(buf, shapes):
    out, off = [], 0
    for shp in shapes:
        n = math.prod(shp)
        nr = -(-n // LANES)
        out.append(buf[off:off + nr].reshape(-1)[:n].reshape(shp))
        off += nr
    return out


def _grad_parts(grads, pieces):
    return [_to_shards(n, grads[n][l]).astype(MM) for n, l in pieces]


def _forward_backward(x, target, W, shards=None):
    T = x.shape[0]
    L = W["ffn1_norm"].shape[0]
    saved = []
    recv = {}
    for l in range(L):
        g1, gm, g2 = (W[n][l][None, :] for n in ("ffn1_norm", "mix_norm", "ffn2_norm"))
        first = shards is not None and l == 0
        hosted = (_GatherChips, [shards["w_in"][0], shards["conv_w"]]) if first else None
        x1, a1, b1, *got = _ffn_fwd(x, g1, W["ffn1_w_gate"][l], W["ffn1_w_up"][l], W["ffn1_w_down"][l], hosted=hosted)
        if first:
            W["w_in"][0] = _from_shards("w_in", got[0])
            W["conv_w"] = got[1].transpose(1, 2, 0, 3).reshape(L, CONV_K, 256)
        w_in = _pad_w_in(W["w_in"][l])
        up, q, k, v, ca, cg, zf = _mix_in_fwd(x1, gm, w_in)
        fb = jnp.pad(W["forget_bias"][l], (0, LANES - HEADS))[None, :]
        F = _fgate_fwd(zf, fb)
        if first:
            yb, lse, *got = _attn_fwd(q, k, v, F, hosted=(_GatherChipsSplit, [shards[n][ll] for n, ll in LATE]))
            for (n, ll), sh in zip(LATE, got):
                W[n][ll] = _from_shards(n, sh)
        else:
            yb, lse = _attn_fwd(q, k, v, F)
        bd = _block_diag(W["pool_w"][l]).astype(MM)
        ps, cb, lg, lb = (W[n][l][None, :] for n in ("pool_scale", "conv_b", "conv_ln_g", "conv_ln_b"))
        cw = jnp.pad(W["conv_w"][l], ((0, 1), (0, 0)))
        ya, yc, cu, cy = _local_fwd(up, ca, cg, bd, ps, cw, cb, lg, lb)
        x2 = _mix_out_fwd(x1, ya, yb, yc, W["w_out"][l])
        x3, a2, b2 = _ffn_fwd(x2, g2, W["ffn2_w_gate"][l], W["ffn2_w_up"][l], W["ffn2_w_down"][l])
        saved.append(dict(x0=x, x1=x1, x2=x2, ab1=(a1, b1), ab2=(a2, b2), w_in=w_in, up=up, ca=ca, cg=cg, zf=zf, fb=fb, F=F,
                          q=q, k=k, v=v, lse=lse, bd=bd, cw=cw, cu=cu, cy=cy, ya=ya, yb=yb, yc=yc))
        x = x3

    loss, dx, dgf = _head(x, W["final_norm"][None, :], target)
    grads = {n: [None] * L for n in W if n != "final_norm"}
    grads["final_norm"] = dgf[0]
    for l in reversed(range(L)):
        s = saved[l]
        g1, gm, g2 = (W[n][l][None, :] for n in ("ffn1_norm", "mix_norm", "ffn2_norm"))
        ps, lg, lb = (W[n][l][None, :] for n in ("pool_scale", "conv_ln_g", "conv_ln_b"))
        dx, h, dy, da, db, sact, dg = _ffn_bwd(s["x2"], dx, g2, *s["ab2"], W["ffn2_w_gate"][l], W["ffn2_w_up"][l],
                                               W["ffn2_w_down"][l])
        grads["ffn2_norm"][l] = dg[0]
        grads["ffn2_w_gate"][l] = _wgrad(h, da, "wgrad_gate")
        grads["ffn2_w_up"][l] = _wgrad(h, db, "wgrad_up")
        grads["ffn2_w_down"][l] = _wgrad(sact, dy, "wgrad_down")
        dya, dyb, dyc = _mix_out_bwd(dx, W["w_out"][l])
        grads["w_out"][l] = jnp.concatenate(
            [_wgrad(s["ya"], dx, "wgrad_out_a"), _wgrad(s["yb"], dx, "wgrad_out_b"), _wgrad(s["yc"], dx, "wgrad_out_c")], axis=0)
        first = shards is not None and l == 0
        if first:
            dq, dk, dv, dfq, dfk, *got = _attn_bwd(s["q"], s["k"], s["v"], s["F"], s["yb"], s["lse"], dyb,
                                                  hosted=(_ScatterChips, _grad_parts(grads, LATE)))
            recv.update(zip(LATE, got))
        else:
            dq, dk, dv, dfq, dfk = _attn_bwd(s["q"], s["k"], s["v"], s["F"], s["yb"], s["lse"], dyb)
        dfk_cols = jnp.pad(dfk.transpose(0, 2, 1, 3).reshape(HEADS, T).T, ((0, 0), (0, LANES - HEADS)))
        dzf, dfb = _fgate_bwd(s["zf"], s["fb"], dfq, dfk_cols)
        grads["forget_bias"][l] = dfb[0, :HEADS]
        dup, dca, dcg, dbd, dps, dcw, dcb, dlg, dlb = _local_bwd(
            s["up"], dya, s["ca"], s["cg"], s["cu"], s["cy"], dyc, s["bd"], ps, s["cw"], lg, lb)
        grads["pool_w"][l] = jnp.stack([dbd[64 * i:64 * i + 64, 64 * i:64 * i + 64] for i in range(4)])
        grads["pool_scale"][l], grads["conv_b"][l] = dps[0], dcb[0]
        grads["conv_ln_g"][l], grads["conv_ln_b"][l] = dlg[0], dlb[0]
        grads["conv_w"][l] = dcw[:CONV_K]
        dx, h, dp, dg = _mix_in_bwd(s["x1"], dx, gm, s["w_in"], dup, dq, dk, dv, dca, dcg, dzf)
        grads["mix_norm"][l] = dg[0]
        grads["w_in"][l] = _unpad_w_in(_wgrad(h, dp, "wgrad_in"))
        ffn1 = (W["ffn1_w_gate"][l], W["ffn1_w_up"][l], W["ffn1_w_down"][l])
        if not first:
            dx, h, dy, da, db, sact, dg = _ffn_bwd(s["x0"], dx, g1, *s["ab1"], *ffn1)
            grads["ffn1_w_gate"][l] = _wgrad(h, da, "wgrad_gate")
            grads["ffn1_w_up"][l] = _wgrad(h, db, "wgrad_up")
            grads["ffn1_w_down"][l] = _wgrad(sact, dy, "wgrad_down")
        else:
            scatter = lambda n: (_ScatterChips, _grad_parts(grads, [(n, 0)]))
            dx, h, dy, da, db, sact, dg, recv[("w_in", 0)] = _ffn_bwd(s["x0"], dx, g1, *s["ab1"], *ffn1,
                                                                      hosted=scatter("w_in"))
            grads["ffn1_w_gate"][0] = _wgrad(h, da, "wgrad_gate")
            grads["ffn1_w_up"][0], recv[("ffn1_w_gate", 0)] = _wgrad(h, db, "wgrad_up", hosted=scatter("ffn1_w_gate"))
            grads["ffn1_w_down"][0], recv[("ffn1_w_up", 0)] = _wgrad(sact, dy, "wgrad_down", hosted=scatter("ffn1_w_up"))
            recv[("ffn1_w_down", 0)] = _exchange(*scatter("ffn1_w_down"), "scatter_last_grad")[0]
        grads["ffn1_norm"][l] = dg[0]
    grads = {n: (jnp.stack(g) if isinstance(g, list) and n not in BIG else g) for n, g in grads.items()}
    return loss, dx, grads, recv


NAMES = ("ffn1_norm", "ffn1_w_gate", "ffn1_w_up", "ffn1_w_down", "mix_norm", "w_in", "pool_w", "pool_scale",
         "forget_bias", "conv_w", "conv_b", "conv_ln_g", "conv_ln_b", "w_out", "ffn2_norm", "ffn2_w_gate",
         "ffn2_w_up", "ffn2_w_down", "final_norm")


def kernel(x, ffn1_norm, ffn1_w_gate, ffn1_w_up, ffn1_w_down, mix_norm, w_in, pool_w, pool_scale, forget_bias, conv_w, conv_b, conv_ln_g, conv_ln_b, w_out, ffn2_norm, ffn2_w_gate, ffn2_w_up, ffn2_w_down, final_norm, loss_target, m_ffn1_norm, m_ffn1_w_gate, m_ffn1_w_up, m_ffn1_w_down, m_mix_norm, m_w_in, m_pool_w, m_pool_scale, m_forget_bias, m_conv_w, m_conv_b, m_conv_ln_g, m_conv_ln_b, m_w_out, m_ffn2_norm, m_ffn2_w_gate, m_ffn2_w_up, m_ffn2_w_down, m_final_norm, v_ffn1_norm, v_ffn1_w_gate, v_ffn1_w_up, v_ffn1_w_down, v_mix_norm, v_w_in, v_pool_w, v_pool_scale, v_forget_bias, v_conv_w, v_conv_b, v_conv_ln_g, v_conv_ln_b, v_w_out, v_ffn2_norm, v_ffn2_w_gate, v_ffn2_w_up, v_ffn2_w_down, v_final_norm):
    args = (ffn1_norm, ffn1_w_gate, ffn1_w_up, ffn1_w_down, mix_norm, w_in, pool_w, pool_scale, forget_bias, conv_w, conv_b, conv_ln_g, conv_ln_b, w_out, ffn2_norm, ffn2_w_gate, ffn2_w_up, ffn2_w_down, final_norm)
    ms = (m_ffn1_norm, m_ffn1_w_gate, m_ffn1_w_up, m_ffn1_w_down, m_mix_norm, m_w_in, m_pool_w, m_pool_scale, m_forget_bias, m_conv_w, m_conv_b, m_conv_ln_g, m_conv_ln_b, m_w_out, m_ffn2_norm, m_ffn2_w_gate, m_ffn2_w_up, m_ffn2_w_down, m_final_norm)
    vs = (v_ffn1_norm, v_ffn1_w_gate, v_ffn1_w_up, v_ffn1_w_down, v_mix_norm, v_w_in, v_pool_w, v_pool_scale, v_forget_bias, v_conv_w, v_conv_b, v_conv_ln_g, v_conv_ln_b, v_w_out, v_ffn2_norm, v_ffn2_w_gate, v_ffn2_w_up, v_ffn2_w_down, v_final_norm)
    P = dict(zip(NAMES, args))
    M = dict(zip(NAMES, ms))
    V = dict(zip(NAMES, vs))
    xi, yi, _ = _position()
    chip = 2 * xi + yi

    W = {n: P[n] for n in SMALL}
    W.update({n: [None] * P[n].shape[0] for n in BIG})
    shards = {n: [P[n][l].astype(MM) for l in range(P[n].shape[0])] for n in BIG}
    shards["conv_w"] = P["conv_w"]
    for (n, l), sh in zip(FIRST, _exchange(_GatherChipsSplit, [shards[n][l] for n, l in FIRST], "gather_first_weights")):
        W[n][l] = _from_shards(n, sh)

    loss_part, dx, G, recv = _forward_backward(x[0], loss_target[0], W, shards)
    loss = lax.psum(loss_part[0, 0], ("x", "y", "c"))

    small_shapes = [P[n].shape for n in SMALL] + [G["conv_w"].shape]
    small_parts = _gather_all(_pack_small([G[n] for n in SMALL] + [G["conv_w"]]), "gather_small_grads")
    small_sum = _sum8(small_parts, "sum_small_grads")
    nsmall = sum(-(-math.prod(s) // LANES) for s in small_shapes[:-1])
    nsmall_pad = nsmall + (-nsmall % 8)
    w_s, m_s, v_s = (_pack_small([D[n] for n in SMALL]) for D in (P, M, V))
    outs_small = _adamw(w_s, [small_sum[:nsmall_pad]], m_s, v_s, "adamw_small")
    res = {}
    for kind, buf in zip(("g", "d", "m", "v"), outs_small):
        for n, a in zip(SMALL, _unpack_small(buf, small_shapes[:-1])):
            res[(kind, n)] = a
    g_cw_full = _unpack_small(small_sum[nsmall:], [small_shapes[-1]])[0]
    g_cw = lax.dynamic_slice_in_dim(g_cw_full, chip * 64, 64, axis=2)
    outs_cw = _adamw(_pack_small([P["conv_w"]]), [_pack_small([g_cw])], _pack_small([M["conv_w"]]),
                     _pack_small([V["conv_w"]]), "adamw_conv_w")
    for kind, buf in zip(("g", "d", "m", "v"), outs_cw):
        res[(kind, "conv_w")] = _unpack_small(buf, [P["conv_w"].shape])[0]

    parts =[_sum_parts([_own_shard(n, G[n][l], chip) for l in range(P[n].shape[0])],
                        [recv[(n, l)] for l in range(P[n].shape[0])], "sum_" + n) for n in BIG]
    others = _exchange(_SwapCores, parts, "swap_core_grads")
    for n, ga, gb in zip(BIG, parts, others):
        shp = P[n].shape
        two_d = (shp[0] * shp[1], shp[2])
        outs = _adamw(P[n].reshape(two_d), [ga.reshape(two_d), gb.reshape(two_d)], M[n].reshape(two_d),
                      V[n].reshape(two_d), "adamw_" + n)
        for kind, a in zip(("g", "d", "m", "v"), outs):
            res[(kind, n)] = a.reshape(shp)

    return (loss, dx[None], *[res[("g", n)] for n in NAMES], *[res[("d", n)] for n in NAMES],
            *[res[("m", n)] for n in NAMES], *[res[("v", n)] for n in NAMES])
```

```python
import math

import jax
import jax.numpy as jnp
from jax import lax
from jax.experimental import pallas as pl
from jax.experimental.pallas import tpu as pltpu

F32 = jnp.float32
MM = jnp.bfloat16
NORM_EPS = 1e-6
HEADS = 8
HEAD_DIM = 64
POOL_WINDOWS = (2, 4, 8, 16)
CONV_K = 31
LANES = 128
VMEM_LIMIT = 56 * 2**20
FFN_BWD_ROWS = 256
FFN_COLS = 768
ATTN_FWD_TILE = 1024
WGRAD_COLS = 768
WGRAD_ACC_BYTES = 12 * 2**20

ADAM_LR = 0.001
ADAM_B1 = 0.9
ADAM_B2 = 0.999
ADAM_EPS = 1e-08
ADAM_WD = 0.01
ADAM_STEP = 10

MESH = pl.DeviceIdType.MESH
BS = pl.BlockSpec
SDS = jax.ShapeDtypeStruct
ANY = pl.BlockSpec(memory_space=pl.ANY)


def _dot(a, b):
    return jnp.dot(a, b, preferred_element_type=F32)


def _dot_nt(a, b):
    return lax.dot_general(a, b, (((1,), (1,)), ((), ())), preferred_element_type=F32)


def _dot_tn(a, b):
    return lax.dot_general(a, b, (((0,), (0,)), ((), ())), preferred_element_type=F32)


def _pc(body, name, grid, in_specs, out_specs, out_shape, scratch=()):
    return pl.pallas_call(
        body, out_shape=out_shape, grid=grid, in_specs=in_specs, out_specs=out_specs,
        scratch_shapes=list(scratch), name=name,
        compiler_params=pltpu.CompilerParams(
            dimension_semantics=("arbitrary",) * len(grid), vmem_limit_bytes=VMEM_LIMIT))


def _rms_fwd(x, g):
    r = lax.rsqrt(jnp.mean(x * x, axis=-1, keepdims=True) + NORM_EPS)
    xh = x * r
    return xh, r, xh * g


def _rms_bwd(dh, xh, r, g):
    dxh = dh * g
    dx = r * (dxh - xh * jnp.mean(dxh * xh, axis=-1, keepdims=True))
    return dx, jnp.sum(dh * xh, axis=0, keepdims=True)


def _sigmoid(x):
    return jax.nn.sigmoid(x)


def _ffn_fwd(x, g, wg, wu, wd, hosted=None):
    T, D = x.shape
    F = wg.shape[1]
    tm = min(512, T)
    nt = T // tm
    pieces = [(c0, min(FFN_COLS, F - c0)) for c0 in range(0, F, FFN_COLS)]
    h_in, h_out, h_shape, h_scratch = _hosted_specs(hosted)

    def body(*refs):
        i = pl.program_id(0)
        refs, finish = _hosted_edges(hosted, refs, 5, 3, i == 0, i == nt - 1)
        x_ref, g_ref, wg_ref, wu_ref, wd_ref, o_ref, a_ref, b_ref = refs
        xv = x_ref[...]
        h = _rms_fwd(xv, g_ref[...])[2].astype(MM)
        acc = jnp.zeros((tm, D), F32)
        for c0, w in pieces:
            a = _dot(h, wg_ref[:, c0:c0 + w])
            b = _dot(h, wu_ref[:, c0:c0 + w])
            a_ref[:, c0:c0 + w] = a.astype(a_ref.dtype)
            b_ref[:, c0:c0 + w] = b.astype(b_ref.dtype)
            acc = acc + _dot(((a * _sigmoid(a)) * b).astype(MM), wd_ref[c0:c0 + w, :])
        o_ref[...] = xv + 0.5 * acc
        finish()

    tok = lambda i: (i, 0)
    par = lambda i: (0, 0)
    resident = lambda shape: BS(shape, par, pipeline_mode=pl.Buffered(1))
    return _pc(
        body, "ffn_fwd" + ("_hosting" if hosted else ""), (nt,),
        [BS((tm, D), tok), BS((1, D), par), resident((D, F)), resident((D, F)), resident((F, D))] + h_in,
        [BS((tm, D), tok), BS((tm, F), tok), BS((tm, F), tok)] + h_out,
        [SDS((T, D), F32), SDS((T, F), MM), SDS((T, F), MM)] + h_shape,
        scratch=h_scratch)(x, g, wg, wu, wd, *(hosted[1] if hosted else []))


def _ffn_bwd(x, dout, g, a, b, wg, wu, wd, hosted=None):
    T, D = x.shape
    F = wg.shape[1]
    tm = min(FFN_BWD_ROWS, T)
    nt = T // tm
    pieces = [(c0, min(FFN_COLS, F - c0)) for c0 in range(0, F, FFN_COLS)]
    h_in, h_out, h_shape, h_scratch = _hosted_specs(hosted)

    def body(*refs):
        i = pl.program_id(0)
        refs, finish = _hosted_edges(hosted, refs, 8, 7, i == 0, i == nt - 1)
        (x_ref, do_ref, g_ref, a_ref, b_ref, wg_ref, wu_ref, wd_ref,
         dx_ref, h_ref, dy_ref, da_ref, db_ref, s_ref, dg_ref) = refs

        @pl.when(i == 0)
        def _():
            dg_ref[...] = jnp.zeros_like(dg_ref)

        gv = g_ref[...]
        xh, r, hg = _rms_fwd(x_ref[...], gv)
        h_ref[...] = hg.astype(h_ref.dtype)
        dy = (0.5 * do_ref[...]).astype(MM)
        dy_ref[...] = dy
        dh = jnp.zeros((tm, D), F32)
        for c0, w in pieces:
            a = a_ref[:, c0:c0 + w].astype(F32)
            b = b_ref[:, c0:c0 + w].astype(F32)
            ds = _dot_nt(dy, wd_ref[c0:c0 + w, :])
            sig = _sigmoid(a)
            sl = a * sig
            s_ref[:, c0:c0 + w] = (sl * b).astype(s_ref.dtype)
            db = (ds * sl).astype(MM)
            da = (ds * b * (sig * (1.0 + a * (1.0 - sig)))).astype(MM)
            da_ref[:, c0:c0 + w] = da
            db_ref[:, c0:c0 + w] = db
            dh = dh + _dot_nt(da, wg_ref[:, c0:c0 + w]) + _dot_nt(db, wu_ref[:, c0:c0 + w])
        dx, dg = _rms_bwd(dh, xh, r, gv)
        dx_ref[...] = do_ref[...] + dx
        dg_ref[...] += dg
        finish()

    tok = lambda i: (i, 0)
    par = lambda i: (0, 0)
    hid = BS((tm, F), tok)
    resident = lambda shape: BS(shape, par, pipeline_mode=pl.Buffered(1))
    return _pc(
        body, "ffn_bwd" + ("_hosting" if hosted else ""), (nt,),
        [BS((tm, D), tok), BS((tm, D), tok), BS((1, D), par), hid, hid,
         resident((D, F)), resident((D, F)), resident((F, D))] + h_in,
        [BS((tm, D), tok), BS((tm, D), tok), BS((tm, D), tok), hid, hid, hid, BS((1, D), par)] + h_out,
        [SDS((T, D), F32), SDS((T, D), MM), SDS((T, D), MM),
         SDS((T, F), MM), SDS((T, F), MM), SDS((T, F), MM), SDS((1, D), F32)] + h_shape,
        scratch=h_scratch,
    )(x, dout, g, a, b, wg, wu, wd, *(hosted[1] if hosted else []))


def _wgrad(a, b, name, hosted=None):
    T, K = a.shape
    N = b.shape[1]
    tt = min(512, T)
    tn = next(c for c in (N, 1408, 1280, 1024, 512, 256, 128) if N % c == 0 and K * c * 4 <= WGRAD_ACC_BYTES)
    pieces = [(c0, min(WGRAD_COLS, tn - c0)) for c0 in range(0, tn, WGRAD_COLS)]
    nn, nt = N // tn, T // tt
    h_in, h_out, h_shape, h_scratch = _hosted_specs(hosted)

    def body(*refs):
        n, t = pl.program_id(0), pl.program_id(1)
        refs, finish = _hosted_edges(hosted, refs, 2, 1, (n == 0) & (t == 0), (n == nn - 1) & (t == nt - 1))
        a_ref, b_ref, o_ref = refs

        @pl.when(t == 0)
        def _():
            o_ref[...] = jnp.zeros_like(o_ref)

        av = a_ref[...].astype(MM)
        for c0, w in pieces:
            o_ref[:, c0:c0 + w] += _dot_tn(av, b_ref[:, c0:c0 + w].astype(MM))
        finish()

    res = _pc(
        body, name + ("_hosting" if hosted else ""), (nn, nt),
        [BS((tt, K), lambda n, t: (t, 0)), BS((tt, tn), lambda n, t: (t, n))] + h_in,
        [BS((K, tn), lambda n, t: (0, n))] + h_out, [SDS((K, N), F32)] + h_shape,
        scratch=h_scratch)(a, b, *(hosted[1] if hosted else []))
    return res if hosted else res[0]


C_POOL, C_Q, C_K, C_V, C_CA, C_CG, C_ZF, C_END = 0, 256, 768, 1280, 1792, 2048, 2304, 2560


def _mix_in_fwd(x, g, w):
    T, D = x.shape
    tm = min(512, T)

    def body(x_ref, g_ref, w_ref, up_ref, q_ref, k_ref, v_ref, ca_ref, cg_ref, zf_ref):
        _, _, hg = _rms_fwd(x_ref[...], g_ref[...])
        p = _dot(hg.astype(MM), w_ref[...])
        up_ref[...] = p[:, C_POOL:C_Q]
        q_ref[...] = p[:, C_Q:C_K].astype(q_ref.dtype)
        k_ref[...] = p[:, C_K:C_V].astype(k_ref.dtype)
        v_ref[...] = p[:, C_V:C_CA].astype(v_ref.dtype)
        ca_ref[...] = p[:, C_CA:C_CG]
        cg_ref[...] = p[:, C_CG:C_ZF]
        zf_ref[...] = p[:, C_ZF:C_ZF + LANES]

    tok = lambda i: (i, 0)
    widths = (256, 512, 512, 512, 256, 256, 128)
    dtypes = (F32, MM, MM, MM, F32, F32, F32)
    return _pc(
        body, "mix_in_fwd", (T // tm,),
        [BS((tm, D), tok), BS((1, D), lambda i: (0, 0)), BS((D, C_END), lambda i: (0, 0))],
        [BS((tm, wd), tok) for wd in widths],
        [SDS((T, wd), dt) for wd, dt in zip(widths, dtypes)])(x, g, w)


def _mix_in_bwd(x, dout, g, w, dup, dq, dk, dv, dca, dcg, dzf):
    T, D = x.shape
    tm = min(512, T)

    def body(x_ref, do_ref, g_ref, w_ref, dup_ref, dq_ref, dk_ref, dv_ref, dca_ref, dcg_ref, dzf_ref,
             dx_ref, h_ref, dp_ref, dg_ref):
        @pl.when(pl.program_id(0) == 0)
        def _():
            dg_ref[...] = jnp.zeros_like(dg_ref)

        gv = g_ref[...]
        xh, r, hg = _rms_fwd(x_ref[...], gv)
        h_ref[...] = hg.astype(h_ref.dtype)
        for ref, lo, hi in ((dup_ref, C_POOL, C_Q), (dq_ref, C_Q, C_K), (dk_ref, C_K, C_V), (dv_ref, C_V, C_CA),
                            (dca_ref, C_CA, C_CG), (dcg_ref, C_CG, C_ZF), (dzf_ref, C_ZF, C_ZF + LANES)):
            dp_ref[:, lo:hi] = ref[...].astype(dp_ref.dtype)
        dp_ref[:, C_ZF + LANES:C_END] = jnp.zeros((tm, C_END - C_ZF - LANES), dp_ref.dtype)
        dh = _dot_nt(dp_ref[...], w_ref[...])
        dx, dg = _rms_bwd(dh, xh, r, gv)
        dx_ref[...] = do_ref[...] + dx
        dg_ref[...] += dg

    tok = lambda i: (i, 0)
    widths = (256, 512, 512, 512, 256, 256, 128)
    return _pc(
        body, "mix_in_bwd", (T // tm,),
        [BS((tm, D), tok), BS((tm, D), tok), BS((1, D), lambda i: (0, 0)), BS((D, C_END), lambda i: (0, 0))]
        + [BS((tm, wd), tok) for wd in widths],
        [BS((tm, D), tok), BS((tm, D), tok), BS((tm, C_END), tok), BS((1, D), lambda i: (0, 0))],
        [SDS((T, D), F32), SDS((T, D), MM), SDS((T, C_END), MM), SDS((1, D), F32)],
    )(x, dout, g, w, dup, dq, dk, dv, dca, dcg, dzf)


def _mix_out_fwd(x, ya, yb, yc, wo):
    T, D = x.shape
    tm = min(512, T)

    def body(x_ref, ya_ref, yb_ref, yc_ref, wo_ref, o_ref):
        o_ref[...] = (x_ref[...] + _dot(ya_ref[...].astype(MM), wo_ref[0:256, :])
                      + _dot(yb_ref[...].astype(MM), wo_ref[256:768, :])
                      + _dot(yc_ref[...].astype(MM), wo_ref[768:1024, :]))

    tok = lambda i: (i, 0)
    return _pc(
        body, "mix_out_fwd", (T // tm,),
        [BS((tm, D), tok), BS((tm, 256), tok), BS((tm, 512), tok), BS((tm, 256), tok), BS((D, D), lambda i: (0, 0))],
        BS((tm, D), tok), SDS((T, D), F32))(x, ya, yb, yc, wo)


def _mix_out_bwd(dx, wo):
    T, D = dx.shape
    tm = min(512, T)

    def body(dx_ref, wo_ref, dya_ref, dyb_ref, dyc_ref):
        dy = _dot_nt(dx_ref[...].astype(MM), wo_ref[...])
        dya_ref[...] = dy[:, 0:256]
        dyb_ref[...] = dy[:, 256:768]
        dyc_ref[...] = dy[:, 768:1024]

    tok = lambda i: (i, 0)
    return _pc(
        body, "mix_out_bwd", (T // tm,),
        [BS((tm, D), tok), BS((D, D), lambda i: (0, 0))],
        [BS((tm, 256), tok), BS((tm, 512), tok), BS((tm, 256), tok)],
        [SDS((T, 256), F32), SDS((T, 512), F32), SDS((T, 256), F32)])(dx, wo)


def _fgate_fwd(zf, bias):
    T = zf.shape[0]
    tc = min(256, T)

    def body(z_ref, b_ref, f_ref, carry):
        @pl.when(pl.program_id(0) == 0)
        def _():
            carry[...] = jnp.zeros_like(carry)

        z = z_ref[...] + b_ref[...]
        logf = jnp.minimum(z, 0.0) - jnp.log(1.0 + jnp.exp(-jnp.abs(z)))
        row = lax.broadcasted_iota(jnp.int32, (tc, tc), 0)
        col = lax.broadcasted_iota(jnp.int32, (tc, tc), 1)
        tri = (col <= row).astype(F32)
        f_ref[...] = jnp.dot(tri, logf, precision=lax.Precision.HIGHEST, preferred_element_type=F32) + carry[...]
        carry[...] += jnp.sum(logf, axis=0, keepdims=True)

    return _pc(
        body, "fgate_fwd", (T // tc,),
        [BS((tc, LANES), lambda i: (i, 0)), BS((1, LANES), lambda i: (0, 0))],
        BS((tc, LANES), lambda i: (i, 0)), SDS((T, LANES), F32),
        scratch=[pltpu.VMEM((1, LANES), F32)])(zf, bias)


def _fgate_bwd(zf, bias, dFq, dFk):
    T = zf.shape[0]
    tc = min(256, T)
    n = T // tc
    slabs = dFq.shape[0]

    def body(z_ref, b_ref, dfq_ref, dfk_ref, dz_ref, db_ref, carry):
        @pl.when(pl.program_id(0) == 0)
        def _():
            carry[...] = jnp.zeros_like(carry)
            db_ref[...] = jnp.zeros_like(db_ref)

        df = dfk_ref[...]
        for sl in range(slabs):
            df = df + dfq_ref[sl]
        row = lax.broadcasted_iota(jnp.int32, (tc, tc), 0)
        col = lax.broadcasted_iota(jnp.int32, (tc, tc), 1)
        tri = (col >= row).astype(F32)
        dlogf = jnp.dot(tri, df, precision=lax.Precision.HIGHEST, preferred_element_type=F32) + carry[...]
        carry[...] += jnp.sum(df, axis=0, keepdims=True)
        lane = lax.broadcasted_iota(jnp.int32, (1, LANES), 1)
        dz = jnp.where(lane < HEADS, dlogf * _sigmoid(-(z_ref[...] + b_ref[...])), 0.0)
        dz_ref[...] = dz
        db_ref[...] += jnp.sum(dz, axis=0, keepdims=True)

    rev = lambda i: (n - 1 - i, 0)
    return _pc(
        body, "fgate_bwd", (n,),
        [BS((tc, LANES), rev), BS((1, LANES), lambda i: (0, 0)), BS((slabs, tc, LANES), lambda i: (0, n - 1 - i, 0)),
         BS((tc, LANES), rev)],
        [BS((tc, LANES), rev), BS((1, LANES), lambda i: (0, 0))],
        [SDS((T, LANES), F32), SDS((1, LANES), F32)],
        scratch=[pltpu.VMEM((1, LANES), F32)])(zf, bias, dFq, dFk)


LOG2E = 1.4426950408889634


def _split3(x):
    hi = x.astype(MM)
    r1 = x - hi.astype(F32)
    mid = r1.astype(MM)
    return hi, mid, (r1 - mid.astype(F32)).astype(MM)


def _place(lane, base, cols):
    out = jnp.zeros((cols[0].shape[0], LANES), MM)
    for i, c in enumerate(cols):
        out = jnp.where(lane == base + i, c, out)
    return out


def _head_col(block, lane, h):
    return jnp.sum(jnp.where(lane == h, block, 0.0), axis=-1, keepdims=True)


def _own_lanes(lane, hh):
    return (lane < HEAD_DIM) if hh == 0 else (lane >= HEAD_DIM)


def _attn_k_side(k_ref, f_ref, kb_ref, hp, T, rows, lse_ones, v_ref=None, vb_ref=None):
    lane = lax.broadcasted_iota(jnp.int32, (1, LANES), 1)
    one = jnp.ones((rows, 1), MM)

    def chunk(c, _):
        r0 = pl.multiple_of(c * rows, rows)
        kp = k_ref[pl.ds(r0, rows), :]
        fblk = f_ref[pl.ds(r0, rows), :]
        for hh in range(2):
            hi, mid, lo = _split3(-_head_col(fblk, lane, 2 * hp + hh) * LOG2E)
            cols = [one, one, one, hi, mid, lo] + ([one, one, one] if lse_ones else [])
            bias = _place(lane, HEAD_DIM * (1 - hh), cols)
            kb_ref[hh, pl.ds(r0, rows), :] = jnp.where(_own_lanes(lane, hh), kp, bias)
            if vb_ref is not None:
                vb_ref[hh, pl.ds(r0, rows), :] = jnp.where(_own_lanes(lane, hh), v_ref[pl.ds(r0, rows), :],
                                                           jnp.ones((rows, LANES), MM))
        return 0

    lax.fori_loop(0, T // rows, chunk, 0)


def _attn_q_side(qp, fblk, lane, hp, scale, lse_blk=None):
    qc = qp.astype(F32) * (scale * LOG2E)
    qhi = qc.astype(MM)
    qlo = (qc - qhi.astype(F32)).astype(MM)
    one = jnp.ones((qp.shape[0], 1), MM)
    out = []
    for hh in range(2):
        cols = list(_split3(_head_col(fblk, lane, 2 * hp + hh) * LOG2E)) + [one, one, one]
        if lse_blk is not None:
            cols += list(_split3(-_head_col(lse_blk, lane, 2 * hp + hh)))
        bias = _place(lane, HEAD_DIM * (1 - hh), cols)
        own = _own_lanes(lane, hh)
        out.append(jnp.concatenate([jnp.where(own, qhi, jnp.zeros_like(qhi)), jnp.where(own, qlo, bias)], axis=1))
    return out


def _causal(tq, tk):
    return lax.broadcasted_iota(jnp.int32, (tq, tk), 1) <= lax.broadcasted_iota(jnp.int32, (tq, tk), 0)


def _hosted_specs(hosted):
    if hosted is None:
        return [], [], [], []
    kind, arrays = hosted
    n = len(arrays)
    return [ANY] * n, [ANY] * n, [kind.out_shape(a) for a in arrays], kind.scratch(n)


def _hosted_edges(hosted, refs, n_in, n_out, first, last, mid=None):
    if hosted is None:
        return refs, lambda: None
    kind, arrays = hosted
    n = len(arrays)
    nsem = len(kind.scratch(n))
    o0 = n_in + n + n_out
    ins, outs, sems = refs[n_in:n_in + n], refs[o0:o0 + n], refs[len(refs) - nsem:]
    relayed = mid is not None and hasattr(kind, "relay")

    @pl.when(first)
    def _():
        kind.start(ins, outs, *sems)

    if relayed:
        @pl.when(mid)
        def _():
            kind.relay(ins, outs, *sems)

    def finish():
        @pl.when(last)
        def _():
            kind.wait(ins, outs, *sems, **({"relayed": True} if relayed else {}))

    return refs[:n_in] + refs[n_in + n:o0] + refs[o0 + n:len(refs) - nsem], finish


def _attn_fwd(q, k, v, F, hosted=None):
    T = q.shape[0]
    tq = min(ATTN_FWD_TILE, T)
    tk = tq
    nq = T // tq
    scale = 1.0 / math.sqrt(HEAD_DIM)
    h_in, h_out, h_shape, h_scratch = _hosted_specs(hosted)

    def body(*refs):
        hp, ib = pl.program_id(0), pl.program_id(1)
        refs, finish = _hosted_edges(hosted, refs, 5, 2, (hp == 0) & (ib == 0), (hp == HEADS // 2 - 1) & (ib == nq - 1),
                                     mid=(hp == HEADS // 2 - 1) & (ib == 0))
        q_ref, k_ref, v_ref, fq_ref, f_ref, o_ref, lse_ref, kb_ref, vb_ref = refs
        lane = lax.broadcasted_iota(jnp.int32, (1, LANES), 1)

        @pl.when(ib == 0)
        def _():
            _attn_k_side(k_ref, f_ref, kb_ref, hp, T, min(512, T), False, v_ref, vb_ref)

        qa = _attn_q_side(q_ref[...], fq_ref[...], lane, hp, scale)

        def tile(jb, carry, masked):
            off = pl.multiple_of(jb * tk, tk)
            kp = k_ref[pl.ds(off, tk), :]
            new = []
            for hh in range(2):
                m, acc = carry[hh]
                s = _dot_nt(qa[hh], jnp.concatenate([kp, kb_ref[hh, pl.ds(off, tk), :]], axis=1))
                if masked:
                    s = jnp.where(_causal(tq, tk), s, -jnp.inf)
                m2 = jnp.maximum(m, jnp.max(s, axis=-1, keepdims=True))
                p = jnp.exp2(s - m2)
                new.append((m2, acc * jnp.exp2(m - m2) + _dot(p.astype(MM), vb_ref[hh, pl.ds(off, tk), :])))
            return tuple(new)

        init = tuple((jnp.full((tq, 1), -jnp.inf, F32), jnp.zeros((tq, LANES), F32)) for _ in range(2))
        carry = lax.fori_loop(0, ib, lambda jb, c: tile(jb, c, False), init)
        (m0, a0), (m1, a1) = tile(ib, carry, True)
        l0, l1 = a0[:, HEAD_DIM:HEAD_DIM + 1], a1[:, 0:1]
        o_ref[...] = jnp.where(lane < HEAD_DIM, a0 / l0, a1 / l1)
        lse_ref[...] = jnp.where(lane == 2 * hp, m0 + jnp.log2(l0), jnp.where(lane == 2 * hp + 1, m1 + jnp.log2(l1), 0.0))
        finish()

    blk = lambda h, i: (i, h)
    full = lambda h, i: (0, h)
    return _pc(
        body, "attn_fwd" + ("_hosting" if hosted else ""), (HEADS // 2, nq),
        [BS((tq, LANES), blk), BS((T, LANES), full), BS((T, LANES), full), BS((tq, LANES), lambda h, i: (i, 0)),
         BS((T, LANES), lambda h, i: (0, 0))] + h_in,
        [BS((tq, LANES), blk), BS((None, tq, LANES), lambda h, i: (h, i, 0))] + h_out,
        [SDS((T, HEADS * HEAD_DIM), F32), SDS((HEADS // 2, T, LANES), F32)] + h_shape,
        scratch=[pltpu.VMEM((2, T, LANES), MM)] * 2 + h_scratch)(q, k, v, F, F, *(hosted[1] if hosted else []))


def _attn_bwd(q, k, v, F, o, lse, do, hosted=None):
    T = q.shape[0]
    tq = min(512, T)
    tk = tq
    nq = T // tq
    scale = 1.0 / math.sqrt(HEAD_DIM)
    h_in, h_out, h_shape, h_scratch = _hosted_specs(hosted)

    def body(*refs):
        hp, ib = pl.program_id(0), pl.program_id(1)
        refs, finish = _hosted_edges(hosted, refs, 8, 5, (hp == 0) & (ib == 0), (hp == HEADS // 2 - 1) & (ib == nq - 1))
        (q_ref, k_ref, v_ref, fq_ref, f_ref, o_ref, lse_ref, do_ref,
         dq_ref, dk_ref, dv_ref, dfq_ref, dfk_ref, kb_ref, dk_acc, dv_acc) = refs
        lane = lax.broadcasted_iota(jnp.int32, (1, LANES), 1)

        @pl.when(ib == 0)
        def _():
            _attn_k_side(k_ref, f_ref, kb_ref, hp, T, tk, True)
            dk_acc[...] = jnp.zeros_like(dk_acc)
            dv_acc[...] = jnp.zeros_like(dv_acc)
            dfk_ref[...] = jnp.zeros_like(dfk_ref)

        qp = q_ref[...]
        qa = _attn_q_side(qp, fq_ref[...], lane, hp, scale, lse_ref[...])
        dob = do_ref[...].astype(MM)
        dprod = dob.astype(F32) * o_ref[...]
        qs = (qp.astype(F32) * scale).astype(MM)
        heads = []
        for hh in range(2):
            own = _own_lanes(lane, hh)
            heads.append((jnp.where(own, dob, jnp.zeros_like(dob)), jnp.where(own, qs, jnp.zeros_like(qs)),
                          jnp.sum(jnp.where(own, dprod, 0.0), axis=-1, keepdims=True)))

        def tile(jb, carry, masked):
            off = pl.multiple_of(jb * tk, tk)
            kp = k_ref[pl.ds(off, tk), :]
            vp = v_ref[pl.ds(off, tk), :]
            new = []
            dv_t = jnp.zeros((tk, LANES), F32)
            dk_t = jnp.zeros((tk, LANES), F32)
            for hh in range(2):
                dq, rs = carry[hh]
                dom, qm, delta = heads[hh]
                p = jnp.exp2(_dot_nt(qa[hh], jnp.concatenate([kp, kb_ref[hh, pl.ds(off, tk), :]], axis=1)))
                if masked:
                    p = jnp.where(_causal(tq, tk), p, 0.0)
                ds = p * (_dot_nt(dom, vp) - delta)
                dsb = ds.astype(MM)
                dv_t = dv_t + _dot_tn(p.astype(MM), dom)
                dk_t = dk_t + _dot_tn(dsb, qm)
                dfk_ref[jb, pl.ds(hh, 1), :] -= jnp.sum(ds, axis=0, keepdims=True)
                new.append((dq + _dot(dsb, kp), rs + jnp.sum(ds, axis=-1, keepdims=True)))
            dv_acc[pl.ds(off, tk), :] += dv_t
            dk_acc[pl.ds(off, tk), :] += dk_t
            return tuple(new)

        init = tuple((jnp.zeros((tq, LANES), F32), jnp.zeros((tq, 1), F32)) for _ in range(2))
        carry = lax.fori_loop(0, ib, lambda jb, c: tile(jb, c, False), init)
        (dq0, rs0), (dq1, rs1) = tile(ib, carry, True)
        dq_ref[...] = (jnp.where(lane < HEAD_DIM, dq0, dq1) * scale).astype(dq_ref.dtype)
        dfq_ref[...] = jnp.where(lane == 2 * hp, rs0, jnp.where(lane == 2 * hp + 1, rs1, 0.0))

        @pl.when(ib == nq - 1)
        def _():
            dk_ref[...] = dk_acc[...].astype(dk_ref.dtype)
            dv_ref[...] = dv_acc[...].astype(dv_ref.dtype)

        finish()

    blk = lambda h, i: (i, h)
    full = lambda h, i: (0, h)
    slab = BS((None, tq, LANES), lambda h, i: (h, i, 0))
    return _pc(
        body, "attn_bwd" + ("_hosting" if hosted else ""), (HEADS // 2, nq),
        [BS((tq, LANES), blk), BS((T, LANES), full), BS((T, LANES), full), BS((tq, LANES), lambda h, i: (i, 0)),
         BS((T, LANES), lambda h, i: (0, 0)), BS((tq, LANES), blk), slab, BS((tq, LANES), blk)] + h_in,
        [BS((tq, LANES), blk), BS((T, LANES), full), BS((T, LANES), full), slab,
         BS((None, nq, 2, tk), lambda h, i: (h, 0, 0, 0))] + h_out,
        [SDS((T, HEADS * HEAD_DIM), MM)] * 3 + [SDS((HEADS // 2, T, LANES), F32), SDS((HEADS // 2, nq, 2, tk), F32)]
        + h_shape,
        scratch=[pltpu.VMEM((2, T, LANES), MM), pltpu.VMEM((T, LANES), F32), pltpu.VMEM((T, LANES), F32)] + h_scratch,
    )(q, k, v, F, F, o, lse, do, *(hosted[1] if hosted else []))


POOL_HALO = 16
CONV_HALO = 32


def _group_select(lane, v0, v1, v2, v3):
    return jnp.where(lane < 64, v0, jnp.where(lane < 128, v1, jnp.where(lane < 192, v2, v3)))


def _roll_down(x, k):
    return x if k == 0 else pltpu.roll(x, k, 0)


def _roll_up(x, k):
    return x if k == 0 else pltpu.roll(x, x.shape[0] - k, 0)


def _pool_terms(u, u_prev, tile, tm):
    ext = jnp.concatenate([u_prev, u], axis=0)
    s2 = ext + _roll_down(ext, 1)
    s4 = s2 + _roll_down(s2, 2)
    s8 = s4 + _roll_down(s4, 4)
    s16 = s8 + _roll_down(s8, 8)
    lane = lax.broadcasted_iota(jnp.int32, (1, 256), 1)
    ws = _group_select(lane, s2, s4, s8, s16)[POOL_HALO:, :]
    wlen = _group_select(lane, *map(float, POOL_WINDOWS)).astype(F32)
    return ws / _pool_count(tile, tm, tm, wlen) - u


def _pool_count(tile, tm, rows, wlen):
    t = (tile * tm + 1 + lax.broadcasted_iota(jnp.int32, (rows, 1), 0)).astype(F32)
    return jnp.minimum(t, wlen)


def _layer_norm(y, lg, lb):
    mu = jnp.mean(y, axis=-1, keepdims=True)
    yc = y - mu
    rstd = lax.rsqrt(jnp.mean(yc * yc, axis=-1, keepdims=True) + NORM_EPS)
    yh = yc * rstd
    return yh, rstd, yh * lg + lb


def _halo_specs(tm, T, halo, prev):
    per = tm // halo
    if prev:
        return BS((halo, 256), lambda i: (jnp.maximum(i * per - 1, 0), 0))
    return BS((halo, 256), lambda i: (jnp.minimum((i + 1) * per, T // halo - 1), 0))


def _local_fwd(up, ca, cg, bd, pscale, cw, cb, lg, lb):
    T = up.shape[0]
    tm = min(512, T)

    def body(up_ref, uph_ref, ca_ref, cah_ref, cg_ref, cgh_ref, bd_ref, ps_ref, cw_ref, cb_ref, lg_ref, lb_ref,
             ya_ref, yc_ref, u_ref, y_ref):
        i = pl.program_id(0)
        first = i == 0
        pooled = _pool_terms(up_ref[...], jnp.where(first, 0.0, uph_ref[...]), i, tm)
        ya_ref[...] = (_dot(pooled.astype(MM), bd_ref[...]) * ps_ref[...]).astype(ya_ref.dtype)

        u = ca_ref[...] * _sigmoid(cg_ref[...])
        uh = jnp.where(first, 0.0, cah_ref[...] * _sigmoid(cgh_ref[...]))
        ext = jnp.concatenate([uh, u], axis=0)
        y = jnp.zeros((tm, 256), F32) + cb_ref[...]
        for kk in range(CONV_K):
            y = y + cw_ref[kk:kk + 1, :] * _roll_up(ext, CONV_HALO - (CONV_K - 1) + kk)[:tm, :]
        _, _, z = _layer_norm(y, lg_ref[...], lb_ref[...])
        yc_ref[...] = (z * _sigmoid(z)).astype(yc_ref.dtype)
        u_ref[...] = u
        y_ref[...] = y

    tok = lambda i: (i, 0)
    par = lambda i: (0, 0)
    t256 = BS((tm, 256), tok)
    return _pc(
        body, "local_fwd", (T // tm,),
        [t256, _halo_specs(tm, T, POOL_HALO, True), t256, _halo_specs(tm, T, CONV_HALO, True),
         t256, _halo_specs(tm, T, CONV_HALO, True),
         BS((256, 256), par), BS((1, 256), par), BS((32, 256), par), BS((1, 256), par), BS((1, 256), par),
         BS((1, 256), par)],
        [t256, t256, t256, t256],
        [SDS((T, 256), MM), SDS((T, 256), MM), SDS((T, 256), F32), SDS((T, 256), F32)],
    )(up, up, ca, ca, cg, cg, bd, pscale, cw, cb, lg, lb)


def _local_bwd(up, dya, ca, cg, u, y, dyc, bd, pscale, cw, lg, lb):
    T = up.shape[0]
    tm = min(512, T)
    n = T // tm

    def body(up_ref, uph_ref, dya_ref, dyan_ref, ca_ref, cg_ref, u_ref, y_ref, yn_ref, dyc_ref, dycn_ref,
             bd_ref, ps_ref, cw_ref, lg_ref, lb_ref,
             dup_ref, dca_ref, dcg_ref, dbd_ref, dps_ref, dcw_ref, dcb_ref, dlg_ref, dlb_ref):
        i = pl.program_id(0)
        first = i == 0
        last = i == n - 1

        @pl.when(first)
        def _():
            for ref in (dbd_ref, dps_ref, dcw_ref, dcb_ref, dlg_ref, dlb_ref):
                ref[...] = jnp.zeros_like(ref)

        ps = ps_ref[...]
        pooled = _pool_terms(up_ref[...], jnp.where(first, 0.0, uph_ref[...]), i, tm).astype(MM)
        dya_t = dya_ref[...]
        dps_ref[...] += jnp.sum(dya_t * _dot(pooled, bd_ref[...]), axis=0, keepdims=True)
        dm = (jnp.concatenate([dya_t, jnp.where(last, 0.0, dyan_ref[...])], axis=0) * ps).astype(MM)
        dbd_ref[...] += _dot_tn(pooled, dm[:tm, :])
        dpool = _dot_nt(dm, bd_ref[...])
        lane = lax.broadcasted_iota(jnp.int32, (1, 256), 1)
        wlen = _group_select(lane, *map(float, POOL_WINDOWS)).astype(F32)
        e = dpool / _pool_count(i, tm, tm + POOL_HALO, wlen)
        f2 = e + _roll_up(e, 1)
        f4 = f2 + _roll_up(f2, 2)
        f8 = f4 + _roll_up(f4, 4)
        f16 = f8 + _roll_up(f8, 8)
        dup_ref[...] = _group_select(lane, f2, f4, f8, f16)[:tm, :] - dpool[:tm, :]

        lgv = lg_ref[...]
        yext = jnp.concatenate([y_ref[...], yn_ref[...]], axis=0)
        dyc = jnp.concatenate([dyc_ref[...], jnp.where(last, 0.0, dycn_ref[...])], axis=0)
        yh, rstd, z = _layer_norm(yext, lgv, lb_ref[...])
        sig = _sigmoid(z)
        dz = dyc * (sig * (1.0 + z * (1.0 - sig)))
        dlg_ref[...] += jnp.sum((dz * yh)[:tm, :], axis=0, keepdims=True)
        dlb_ref[...] += jnp.sum(dz[:tm, :], axis=0, keepdims=True)
        dyh = dz * lgv
        dy = rstd * (dyh - jnp.mean(dyh, axis=-1, keepdims=True) - yh * jnp.mean(dyh * yh, axis=-1, keepdims=True))
        dcb_ref[...] += jnp.sum(dy[:tm, :], axis=0, keepdims=True)
        uv = u_ref[...]
        du = jnp.zeros((tm, 256), F32)
        for kk in range(CONV_K):
            ahead = _roll_up(dy, CONV_K - 1 - kk)[:tm, :]
            dcw_ref[kk:kk + 1, :] += jnp.sum(uv * ahead, axis=0, keepdims=True)
            du = du + cw_ref[kk:kk + 1, :] * ahead
        sg = _sigmoid(cg_ref[...])
        dca_ref[...] = du * sg
        dcg_ref[...] = du * ca_ref[...] * sg * (1.0 - sg)

    tok = lambda i: (i, 0)
    par = lambda i: (0, 0)
    t256 = BS((tm, 256), tok)
    p1 = BS((1, 256), par)
    return _pc(
        body, "local_bwd", (n,),
        [t256, _halo_specs(tm, T, POOL_HALO, True), t256, _halo_specs(tm, T, POOL_HALO, False), t256, t256,
         t256, t256, _halo_specs(tm, T, CONV_HALO, False), t256, _halo_specs(tm, T, CONV_HALO, False),
         BS((256, 256), par), p1, BS((32, 256), par), p1, p1],
        [t256, t256, t256, BS((256, 256), par), p1, BS((32, 256), par), p1, p1, p1],
        [SDS((T, 256), F32)] * 3 + [SDS((256, 256), F32), SDS((1, 256), F32), SDS((32, 256), F32)]
        + [SDS((1, 256), F32)] * 3,
    )(up, up, dya, dya, ca, cg, u, y, y, dyc, dyc, bd, pscale, cw, lg, lb)


def _head(x, g, target):
    T, D = x.shape
    tm = min(512, T)

    def body(x_ref, g_ref, t_ref, loss_ref, dx_ref, dg_ref):
        @pl.when(pl.program_id(0) == 0)
        def _():
            loss_ref[...] = jnp.zeros_like(loss_ref)
            dg_ref[...] = jnp.zeros_like(dg_ref)

        gv = g_ref[...]
        xh, r, yv = _rms_fwd(x_ref[...], gv)
        err = yv - t_ref[...]
        loss_ref[...] += 0.5 * jnp.sum(jnp.mean(err * err, axis=-1, keepdims=True), axis=0, keepdims=True)
        dx, dg = _rms_bwd(err * (1.0 / D), xh, r, gv)
        dx_ref[...] = dx
        dg_ref[...] += dg

    tok = lambda i: (i, 0)
    par = lambda i: (0, 0)
    return _pc(
        body, "head", (T // tm,),
        [BS((tm, D), tok), BS((1, D), par), BS((tm, D), tok)],
        [BS((1, LANES), par), BS((tm, D), tok), BS((1, D), par)],
        [SDS((1, LANES), F32), SDS((T, D), F32), SDS((1, D), F32)])(x, g, target)


def _adamw(w, gs, m, v, name):
    R, C = w.shape
    tr = R
    for cand in (512, 256, 128, 64, 32, 16, 8):
        if R % cand == 0:
            tr = cand
            break
    ng = len(gs)

    def body(*refs):
        w_ref, g_refs, m_ref, v_ref = refs[0], refs[1:1 + ng], refs[1 + ng], refs[2 + ng]
        g_ref, d_ref, m2_ref, v2_ref = refs[3 + ng:]
        g = g_refs[0][...]
        for r in g_refs[1:]:
            g = g + r[...]
        m2 = ADAM_B1 * m_ref[...] + (1.0 - ADAM_B1) * g
        v2 = ADAM_B2 * v_ref[...] + (1.0 - ADAM_B2) * jnp.square(g)
        m_hat = m2 / (1.0 - ADAM_B1 ** ADAM_STEP)
        v_hat = v2 / (1.0 - ADAM_B2 ** ADAM_STEP)
        g_ref[...] = g
        d_ref[...] = -ADAM_LR * (m_hat / (jnp.sqrt(v_hat) + ADAM_EPS) + ADAM_WD * w_ref[...])
        m2_ref[...] = m2
        v2_ref[...] = v2

    blk = BS((tr, C), lambda i: (i, 0))
    return _pc(body, name, (R // tr,), [blk] * (3 + ng), [blk] * 4, [SDS((R, C), F32)] * 4)(w, *gs, m, v)


def _sum_parts(owns, recvs, name):
    L = len(owns)
    R, C = owns[0].shape
    tr = next(t for t in (512, 256, 128, 64, 32, 16) if R % t == 0)

    def body(*refs):
        l = pl.program_id(0)
        s_ref = refs[2 * L]
        for ll in range(L):
            @pl.when(l == ll)
            def _(o_ref=refs[ll], r_ref=refs[L + ll]):
                s_ref[...] = ((o_ref[...] + r_ref[0].astype(F32)) + r_ref[1].astype(F32)) + r_ref[2].astype(F32)

    own_specs = [BS((tr, C), lambda l, i, ll=ll: (jnp.where(l == ll, i, 0), 0)) for ll in range(L)]
    recv_specs = [BS((3, tr, C), lambda l, i, ll=ll: (0, jnp.where(l == ll, i, 0), 0)) for ll in range(L)]
    return _pc(body, name, (L, R // tr), own_specs + recv_specs,
               BS((None, tr, C), lambda l, i: (l, i, 0)), SDS((L, R, C), F32))(*owns, *recvs)


def _sum8(parts, name):
    _, R, C = parts.shape

    def body(p_ref, s_ref):
        acc = p_ref[0]
        for d in range(1, 8):
            acc = acc + p_ref[d]
        s_ref[...] = acc

    return _pc(body, name, (1,), [BS((8, R, C), lambda i: (0, 0, 0))], BS((R, C), lambda i: (0, 0)),
               SDS((R, C), F32))(parts)


def _position():
    return lax.axis_index("x"), lax.axis_index("y"), lax.axis_index("c")


CHIP_FLIPS = ((1, 0), (0, 1), (1, 1))


class _GatherChips:
    @staticmethod
    def scratch(n):
        return [pltpu.SemaphoreType.DMA((3 * n,)), pltpu.SemaphoreType.DMA((3 * n,)), pltpu.SemaphoreType.DMA((n,))]

    @staticmethod
    def out_shape(block):
        return SDS((4,) + tuple(block.shape), block.dtype)

    @staticmethod
    def _copies(ins, outs, send_sems, recv_sems, local_sems, arrivals):
        x, y, c = _position()
        local, remote = [], []
        for i, (in_ref, out_ref) in enumerate(zip(ins, outs)):
            local.append(pltpu.make_async_copy(in_ref, out_ref.at[2 * x + y], local_sems.at[i]))
            for k, (fx, fy) in enumerate(CHIP_FLIPS):
                slot = 2 * (x ^ fx) + (y ^ fy) if arrivals else 2 * x + y
                remote.append(pltpu.make_async_remote_copy(
                    src_ref=in_ref, dst_ref=out_ref.at[slot], send_sem=send_sems.at[3 * i + k],
                    recv_sem=recv_sems.at[3 * i + k], device_id=(x ^ fx, y ^ fy, c), device_id_type=MESH))
        return local, remote

    @classmethod
    def start(cls, ins, outs, *sems):
        local, sends = cls._copies(ins, outs, *sems, arrivals=False)
        for cp in local + sends:
            cp.start()

    @classmethod
    def wait(cls, ins, outs, *sems):
        local, arrivals = cls._copies(ins, outs, *sems, arrivals=True)
        for cp in arrivals:
            cp.wait_recv()
        for cp in arrivals:
            cp.wait_send()
        for cp in local:
            cp.wait()


class _GatherChipsSplit(_GatherChips):
    @staticmethod
    def scratch(n):
        return [pltpu.SemaphoreType.DMA((6 * n,)), pltpu.SemaphoreType.DMA((6 * n,)), pltpu.SemaphoreType.DMA((n,))]

    @staticmethod
    def _half(ref, which):
        rows = ref.shape[0] // 2
        return ref.at[pl.ds(pl.multiple_of(which * rows, 16), rows)]

    @staticmethod
    def _local(ins, outs, local_sems):
        x, y, _ = _position()
        return [pltpu.make_async_copy(in_ref, out_ref.at[2 * x + y], local_sems.at[i])
                for i, (in_ref, out_ref) in enumerate(zip(ins, outs))]

    @classmethod
    def _between_chips(cls, ins, outs, send_sems, recv_sems, arrivals):
        x, y, c = _position()
        return [
            pltpu.make_async_remote_copy(
                src_ref=cls._half(in_ref, c),
                dst_ref=cls._half(out_ref.at[2 * (x ^ fx) + (y ^ fy) if arrivals else 2 * x + y], c),
                send_sem=send_sems.at[6 * i + k], recv_sem=recv_sems.at[6 * i + k],
                device_id=(x ^ fx, y ^ fy, c), device_id_type=MESH)
            for i, (in_ref, out_ref) in enumerate(zip(ins, outs)) for k, (fx, fy) in enumerate(CHIP_FLIPS)]

    @classmethod
    def _between_cores(cls, outs, send_sems, recv_sems, arrivals):
        x, y, c = _position()
        copies = []
        for i, out_ref in enumerate(outs):
            for k, (fx, fy) in enumerate(CHIP_FLIPS):
                half = cls._half(out_ref.at[2 * (x ^ fx) + (y ^ fy)], 1 - c if arrivals else c)
                copies.append(pltpu.make_async_remote_copy(
                    src_ref=half, dst_ref=half, send_sem=send_sems.at[6 * i + 3 + k],
                    recv_sem=recv_sems.at[6 * i + 3 + k], device_id=(x, y, 1 - c), device_id_type=MESH))
        return copies

    @classmethod
    def start(cls, ins, outs, send_sems, recv_sems, local_sems):
        for cp in cls._local(ins, outs, local_sems) + cls._between_chips(ins, outs, send_sems, recv_sems, False):
            cp.start()

    @classmethod
    def relay(cls, ins, outs, send_sems, recv_sems, local_sems):
        arrivals = cls._between_chips(ins, outs, send_sems, recv_sems, True)
        onward = cls._between_cores(outs, send_sems, recv_sems, False)
        for cp, nxt in zip(arrivals, onward):
            cp.wait_recv()
            nxt.start()

    @classmethod
    def wait(cls, ins, outs, send_sems, recv_sems, local_sems, relayed=False):
        if not relayed:
            cls.relay(ins, outs, send_sems, recv_sems, local_sems)
        for cp in cls._between_cores(outs, send_sems, recv_sems, True):
            cp.wait_recv()
        for cp in (cls._between_chips(ins, outs, send_sems, recv_sems, True)
                   + cls._between_cores(outs, send_sems, recv_sems, False)):
            cp.wait_send()
        for cp in cls._local(ins, outs, local_sems):
            cp.wait()


class _Symmetric:
    @classmethod
    def start(cls, ins, outs, *sems):
        for cp in cls._copies(ins, outs, *sems):
            cp.start()

    @classmethod
    def wait(cls, ins, outs, *sems):
        copies = cls._copies(ins, outs, *sems)
        for cp in copies:
            cp.wait_recv()
        for cp in copies:
            cp.wait_send()


class _ScatterChips(_Symmetric):
    @staticmethod
    def scratch(n):
        return [pltpu.SemaphoreType.DMA((3 * n,)), pltpu.SemaphoreType.DMA((3 * n,))]

    @staticmethod
    def out_shape(parts):
        return SDS((3,) + tuple(parts.shape[1:]), parts.dtype)

    @staticmethod
    def _copies(ins, outs, send_sems, recv_sems):
        x, y, c = _position()
        return [
            pltpu.make_async_remote_copy(
                src_ref=in_ref.at[2 * (x ^ fx) + (y ^ fy)], dst_ref=out_ref.at[k],
                send_sem=send_sems.at[3 * i + k], recv_sem=recv_sems.at[3 * i + k],
                device_id=(x ^ fx, y ^ fy, c), device_id_type=MESH)
            for i, (in_ref, out_ref) in enumerate(zip(ins, outs)) for k, (fx, fy) in enumerate(CHIP_FLIPS)]


class _SwapCores(_Symmetric):
    @staticmethod
    def scratch(n):
        return [pltpu.SemaphoreType.DMA((n,)), pltpu.SemaphoreType.DMA((n,))]

    @staticmethod
    def out_shape(block):
        return SDS(block.shape, block.dtype)

    @staticmethod
    def _copies(ins, outs, send_sems, recv_sems):
        x, y, c = _position()
        return [
            pltpu.make_async_remote_copy(
                src_ref=in_ref, dst_ref=out_ref, send_sem=send_sems.at[i], recv_sem=recv_sems.at[i],
                device_id=(x, y, 1 - c), device_id_type=MESH)
            for i, (in_ref, out_ref) in enumerate(zip(ins, outs))]


def _exchange(kind, arrays, name):
    n = len(arrays)

    def body(*refs):
        ins, outs, sems = refs[:n], refs[n:2 * n], refs[2 * n:]
        kind.start(ins, outs, *sems)
        kind.wait(ins, outs, *sems)

    return pl.pallas_call(body, out_shape=[kind.out_shape(a) for a in arrays], in_specs=[ANY] * n,
                          out_specs=[ANY] * n, name=name, scratch_shapes=kind.scratch(n))(*arrays)


def _gather_all(block, name):
    R, C = block.shape
    flips = [(fx, fy, fc) for fx in (0, 1) for fy in (0, 1) for fc in (0, 1)][1:]

    def body(in_ref, out_ref, send_sems, recv_sems, local_sem):
        x, y, c = _position()
        mine = out_ref.at[4 * x + 2 * y + c]
        local = pltpu.make_async_copy(in_ref, mine, local_sem)
        local.start()
        copies = [
            pltpu.make_async_remote_copy(
                src_ref=in_ref, dst_ref=mine, send_sem=send_sems.at[k], recv_sem=recv_sems.at[k],
                device_id=(x ^ fx, y ^ fy, c ^ fc), device_id_type=MESH)
            for k, (fx, fy, fc) in enumerate(flips)]
        for cp in copies:
            cp.start()
        for k, (fx, fy, fc) in enumerate(flips):
            theirs = out_ref.at[4 * (x ^ fx) + 2 * (y ^ fy) + (c ^ fc)]
            pltpu.make_async_remote_copy(
                src_ref=in_ref, dst_ref=theirs, send_sem=send_sems.at[k], recv_sem=recv_sems.at[k],
                device_id=(x ^ fx, y ^ fy, c ^ fc), device_id_type=MESH).wait_recv()
        for cp in copies:
            cp.wait_send()
        local.wait()

    return pl.pallas_call(
        body, out_shape=SDS((8, R, C), block.dtype), in_specs=[ANY], out_specs=ANY, name=name,
        scratch_shapes=[pltpu.SemaphoreType.DMA((7,)), pltpu.SemaphoreType.DMA((7,)), pltpu.SemaphoreType.DMA(())])(block)


BIG = ("ffn1_w_gate", "ffn1_w_up", "ffn1_w_down", "w_in", "w_out", "ffn2_w_gate", "ffn2_w_up", "ffn2_w_down")
COL_SHARDED = ("ffn1_w_gate", "ffn1_w_up", "w_in", "ffn2_w_gate", "ffn2_w_up")
FIRST = tuple((n, 0) for n in ("ffn1_w_gate", "ffn1_w_up", "ffn1_w_down"))
LATE = tuple((n, 0) for n in ("w_out", "ffn2_w_gate", "ffn2_w_up", "ffn2_w_down")) + tuple((n, 1) for n in BIG)


def _to_shards(name, full):
    r, c = full.shape
    if name in COL_SHARDED:
        return full.reshape(r, 4, c // 4).transpose(1, 0, 2)
    return full.reshape(4, r // 4, c)


def _own_shard(name, full, chip):
    r, c = full.shape
    if name in COL_SHARDED:
        return lax.dynamic_slice_in_dim(full, chip * (c // 4), c // 4, axis=1)
    return lax.dynamic_slice_in_dim(full, chip * (r // 4), r // 4, axis=0)


def _from_shards(name, sh):
    _, r, c = sh.shape
    if name in COL_SHARDED:
        return sh.transpose(1, 0, 2).reshape(r, 4 * c)
    return sh.reshape(4 * r, c)


def _pad_w_in(w):
    return jnp.concatenate([w[:, :1792], w[:, 1800:2312], w[:, 1792:1800], jnp.zeros((w.shape[0], 248), w.dtype)], axis=1)


def _unpad_w_in(g):
    return jnp.concatenate([g[:, :1792], g[:, 2304:2312], g[:, 1792:2304]], axis=1)


def _block_diag(pw):
    out = jnp.zeros((256, 256), pw.dtype)
    for gidx in range(4):
        out = lax.dynamic_update_slice(out, pw[gidx], (64 * gidx, 64 * gidx))
    return out


SMALL = ("ffn1_norm", "mix_norm", "pool_w", "pool_scale", "forget_bias", "conv_b", "conv_ln_g", "conv_ln_b",
         "ffn2_norm", "final_norm")


def _pack_small(arrs):
    rows = []
    for a in arrs:
        flat = a.reshape(-1)
        flat = jnp.pad(flat, (0, -flat.shape[0] % LANES))
        rows.append(flat.reshape(-1, LANES))
    total = sum(r.shape[0] for r in rows)
    if total % 8:
        rows.append(jnp.zeros((-total % 8, LANES), F32))
    return jnp.concatenate(rows, axis=0)


def _unpack_small(buf, shapes):
    out, off = [], 0
    for shp in shapes:
        n = math.prod(shp)
        nr = -(-n // LANES)
        out.append(buf[off:off + nr].reshape(-1)[:n].reshape(shp))
        off += nr
    return out


def _grad_parts(grads, pieces):
    return [_to_shards(n, grads[n][l]).astype(MM) for n, l in pieces]


def _forward_backward(x, target, W, shards=None):
    T = x.shape[0]
    L = W["ffn1_norm"].shape[0]
    saved = []
    recv = {}
    for l in range(L):
        g1, gm, g2 = (W[n][l][None, :] for n in ("ffn1_norm", "mix_norm", "ffn2_norm"))
        first = shards is not None and l == 0
        hosted = (_GatherChips, [shards["w_in"][0], shards["conv_w"]]) if first else None
        x1, a1, b1, *got = _ffn_fwd(x, g1, W["ffn1_w_gate"][l], W["ffn1_w_up"][l], W["ffn1_w_down"][l], hosted=hosted)
        if first:
            W["w_in"][0] = _from_shards("w_in", got[0])
            W["conv_w"] = got[1].transpose(1, 2, 0, 3).reshape(L, CONV_K, 256)
        w_in = _pad_w_in(W["w_in"][l])
        up, q, k, v, ca, cg, zf = _mix_in_fwd(x1, gm, w_in)
        fb = jnp.pad(W["forget_bias"][l], (0, LANES - HEADS))[None, :]
        F = _fgate_fwd(zf, fb)
        if first:
            yb, lse, *got = _attn_fwd(q, k, v, F, hosted=(_GatherChipsSplit, [shards[n][ll] for n, ll in LATE]))
            for (n, ll), sh in zip(LATE, got):
                W[n][ll] = _from_shards(n, sh)
        else:
            yb, lse = _attn_fwd(q, k, v, F)
        bd = _block_diag(W["pool_w"][l]).astype(MM)
        ps, cb, lg, lb = (W[n][l][None, :] for n in ("pool_scale", "conv_b", "conv_ln_g", "conv_ln_b"))
        cw = jnp.pad(W["conv_w"][l], ((0, 1), (0, 0)))
        ya, yc, cu, cy = _local_fwd(up, ca, cg, bd, ps, cw, cb, lg, lb)
        x2 = _mix_out_fwd(x1, ya, yb, yc, W["w_out"][l])
        x3, a2, b2 = _ffn_fwd(x2, g2, W["ffn2_w_gate"][l], W["ffn2_w_up"][l], W["ffn2_w_down"][l])
        saved.append(dict(x0=x, x1=x1, x2=x2, ab1=(a1, b1), ab2=(a2, b2), w_in=w_in, up=up, ca=ca, cg=cg, zf=zf, fb=fb, F=F,
                          q=q, k=k, v=v, lse=lse, bd=bd, cw=cw, cu=cu, cy=cy, ya=ya, yb=yb, yc=yc))
        x = x3

    loss, dx, dgf = _head(x, W["final_norm"][None, :], target)
    grads = {n: [None] * L for n in W if n != "final_norm"}
    grads["final_norm"] = dgf[0]
    for l in reversed(range(L)):
        s = saved[l]
        g1, gm, g2 = (W[n][l][None, :] for n in ("ffn1_norm", "mix_norm", "ffn2_norm"))
        ps, lg, lb = (W[n][l][None, :] for n in ("pool_scale", "conv_ln_g", "conv_ln_b"))
        dx, h, dy, da, db, sact, dg = _ffn_bwd(s["x2"], dx, g2, *s["ab2"], W["ffn2_w_gate"][l], W["ffn2_w_up"][l],
                                               W["ffn2_w_down"][l])
        grads["ffn2_norm"][l] = dg[0]
        grads["ffn2_w_gate"][l] = _wgrad(h, da, "wgrad_gate")
        grads["ffn2_w_up"][l] = _wgrad(h, db, "wgrad_up")
        grads["ffn2_w_down"][l] = _wgrad(sact, dy, "wgrad_down")
        dya, dyb, dyc = _mix_out_bwd(dx, W["w_out"][l])
        grads["w_out"][l] = jnp.concatenate(
            [_wgrad(s["ya"], dx, "wgrad_out_a"), _wgrad(s["yb"], dx, "wgrad_out_b"), _wgrad(s["yc"], dx, "wgrad_out_c")], axis=0)
        first = shards is not None and l == 0
        if first:
            dq, dk, dv, dfq, dfk, *got = _attn_bwd(s["q"], s["k"], s["v"], s["F"], s["yb"], s["lse"], dyb,
                                                  hosted=(_ScatterChips, _grad_parts(grads, LATE)))
            recv.update(zip(LATE, got))
        else:
            dq, dk, dv, dfq, dfk = _attn_bwd(s["q"], s["k"], s["v"], s["F"], s["yb"], s["lse"], dyb)
        dfk_cols = jnp.pad(dfk.transpose(0, 2, 1, 3).reshape(HEADS, T).T, ((0, 0), (0, LANES - HEADS)))
        dzf, dfb = _fgate_bwd(s["zf"], s["fb"], dfq, dfk_cols)
        grads["forget_bias"][l] = dfb[0, :HEADS]
        dup, dca, dcg, dbd, dps, dcw, dcb, dlg, dlb = _local_bwd(
            s["up"], dya, s["ca"], s["cg"], s["cu"], s["cy"], dyc, s["bd"], ps, s["cw"], lg, lb)
        grads["pool_w"][l] = jnp.stack([dbd[64 * i:64 * i + 64, 64 * i:64 * i + 64] for i in range(4)])
        grads["pool_scale"][l], grads["conv_b"][l] = dps[0], dcb[0]
        grads["conv_ln_g"][l], grads["conv_ln_b"][l] = dlg[0], dlb[0]
        grads["conv_w"][l] = dcw[:CONV_K]
        dx, h, dp, dg = _mix_in_bwd(s["x1"], dx, gm, s["w_in"], dup, dq, dk, dv, dca, dcg, dzf)
        grads["mix_norm"][l] = dg[0]
        grads["w_in"][l] = _unpad_w_in(_wgrad(h, dp, "wgrad_in"))
        ffn1 = (W["ffn1_w_gate"][l], W["ffn1_w_up"][l], W["ffn1_w_down"][l])
        if not first:
            dx, h, dy, da, db, sact, dg = _ffn_bwd(s["x0"], dx, g1, *s["ab1"], *ffn1)
            grads["ffn1_w_gate"][l] = _wgrad(h, da, "wgrad_gate")
            grads["ffn1_w_up"][l] = _wgrad(h, db, "wgrad_up")
            grads["ffn1_w_down"][l] = _wgrad(sact, dy, "wgrad_down")
        else:
            scatter = lambda n: (_ScatterChips, _grad_parts(grads, [(n, 0)]))
            dx, h, dy, da, db, sact, dg, recv[("w_in", 0)] = _ffn_bwd(s["x0"], dx, g1, *s["ab1"], *ffn1,
                                                                      hosted=scatter("w_in"))
            grads["ffn1_w_gate"][0] = _wgrad(h, da, "wgrad_gate")
            grads["ffn1_w_up"][0], recv[("ffn1_w_gate", 0)] = _wgrad(h, db, "wgrad_up", hosted=scatter("ffn1_w_gate"))
            grads["ffn1_w_down"][0], recv[("ffn1_w_up", 0)] = _wgrad(sact, dy, "wgrad_down", hosted=scatter("ffn1_w_up"))
            recv[("ffn1_w_down", 0)] = _exchange(*scatter("ffn1_w_down"), "scatter_last_grad")[0]
        grads["ffn1_norm"][l] = dg[0]
    grads = {n: (jnp.stack(g) if isinstance(g, list) and n not in BIG else g) for n, g in grads.items()}
    return loss, dx, grads, recv


NAMES = ("ffn1_norm", "ffn1_w_gate", "ffn1_w_up", "ffn1_w_down", "mix_norm", "w_in", "pool_w", "pool_scale",
         "forget_bias", "conv_w", "conv_b", "conv_ln_g", "conv_ln_b", "w_out", "ffn2_norm", "ffn2_w_gate",
         "ffn2_w_up", "ffn2_w_down", "final_norm")


def kernel(x, ffn1_norm, ffn1_w_gate, ffn1_w_up, ffn1_w_down, mix_norm, w_in, pool_w, pool_scale, forget_bias, conv_w, conv_b, conv_ln_g, conv_ln_b, w_out, ffn2_norm, ffn2_w_gate, ffn2_w_up, ffn2_w_down, final_norm, loss_target, m_ffn1_norm, m_ffn1_w_gate, m_ffn1_w_up, m_ffn1_w_down, m_mix_norm, m_w_in, m_pool_w, m_pool_scale, m_forget_bias, m_conv_w, m_conv_b, m_conv_ln_g, m_conv_ln_b, m_w_out, m_ffn2_norm, m_ffn2_w_gate, m_ffn2_w_up, m_ffn2_w_down, m_final_norm, v_ffn1_norm, v_ffn1_w_gate, v_ffn1_w_up, v_ffn1_w_down, v_mix_norm, v_w_in, v_pool_w, v_pool_scale, v_forget_bias, v_conv_w, v_conv_b, v_conv_ln_g, v_conv_ln_b, v_w_out, v_ffn2_norm, v_ffn2_w_gate, v_ffn2_w_up, v_ffn2_w_down, v_final_norm):
    args = (ffn1_norm, ffn1_w_gate, ffn1_w_up, ffn1_w_down, mix_norm, w_in, pool_w, pool_scale, forget_bias, conv_w, conv_b, conv_ln_g, conv_ln_b, w_out, ffn2_norm, ffn2_w_gate, ffn2_w_up, ffn2_w_down, final_norm)
    ms = (m_ffn1_norm, m_ffn1_w_gate, m_ffn1_w_up, m_ffn1_w_down, m_mix_norm, m_w_in, m_pool_w, m_pool_scale, m_forget_bias, m_conv_w, m_conv_b, m_conv_ln_g, m_conv_ln_b, m_w_out, m_ffn2_norm, m_ffn2_w_gate, m_ffn2_w_up, m_ffn2_w_down, m_final_norm)
    vs = (v_ffn1_norm, v_ffn1_w_gate, v_ffn1_w_up, v_ffn1_w_down, v_mix_norm, v_w_in, v_pool_w, v_pool_scale, v_forget_bias, v_conv_w, v_conv_b, v_conv_ln_g, v_conv_ln_b, v_w_out, v_ffn2_norm, v_ffn2_w_gate, v_ffn2_w_up, v_ffn2_w_down, v_final_norm)
    P = dict(zip(NAMES, args))
    M = dict(zip(NAMES, ms))
    V = dict(zip(NAMES, vs))
    xi, yi, _ = _position()
    chip = 2 * xi + yi

    W = {n: P[n] for n in SMALL}
    W.update({n: [None] * P[n].shape[0] for n in BIG})
    shards = {n: [P[n][l].astype(MM) for l in range(P[n].shape[0])] for n in BIG}
    shards["conv_w"] = P["conv_w"]
    for (n, l), sh in zip(FIRST, _exchange(_GatherChipsSplit, [shards[n][l] for n, l in FIRST], "gather_first_weights")):
        W[n][l] = _from_shards(n, sh)

    loss_part, dx, G, recv = _forward_backward(x[0], loss_target[0], W, shards)
    loss = lax.psum(loss_part[0, 0], ("x", "y", "c"))

    small_shapes = [P[n].shape for n in SMALL] + [G["conv_w"].shape]
    small_parts = _gather_all(_pack_small([G[n] for n in SMALL] + [G["conv_w"]]), "gather_small_grads")
    small_sum = _sum8(small_parts, "sum_small_grads")
    nsmall = sum(-(-math.prod(s) // LANES) for s in small_shapes[:-1])
    nsmall_pad = nsmall + (-nsmall % 8)
    w_s, m_s, v_s = (_pack_small([D[n] for n in SMALL]) for D in (P, M, V))
    outs_small = _adamw(w_s, [small_sum[:nsmall_pad]], m_s, v_s, "adamw_small")
    res = {}
    for kind, buf in zip(("g", "d", "m", "v"), outs_small):
        for n, a in zip(SMALL, _unpack_small(buf, small_shapes[:-1])):
            res[(kind, n)] = a
    g_cw_full = _unpack_small(small_sum[nsmall:], [small_shapes[-1]])[0]
    g_cw = lax.dynamic_slice_in_dim(g_cw_full, chip * 64, 64, axis=2)
    outs_cw = _adamw(_pack_small([P["conv_w"]]), [_pack_small([g_cw])], _pack_small([M["conv_w"]]),
                     _pack_small([V["conv_w"]]), "adamw_conv_w")
    for kind, buf in zip(("g", "d", "m", "v"), outs_cw):
        res[(kind, "conv_w")] = _unpack_small(buf, [P["conv_w"].shape])[0]

    parts =[_sum_parts([_own_shard(n, G[n][l], chip) for l in range(P[n].shape[0])],
                        [recv[(n, l)] for l in range(P[n].shape[0])], "sum_" + n) for n in BIG]
    others = _exchange(_SwapCores, parts, "swap_core_grads")
    for n, ga, gb in zip(BIG, parts, others):
        shp = P[n].shape
        two_d = (shp[0] * shp[1], shp[2])
        outs = _adamw(P[n].reshape(two_d), [ga.reshape(two_d), gb.reshape(two_d)], M[n].reshape(two_d),
                      V[n].reshape(two_d), "adamw_" + n)
        for kind, a in zip(("g", "d", "m", "v"), outs):
            res[(kind, n)] = a.reshape(shp)

    return (loss, dx[None], *[res[("g", n)] for n in NAMES], *[res[("d", n)] for n in NAMES],
            *[res[("m", n)] for n in NAMES], *[res[("v", n)] for n in NAMES])
```

```python
import math

import jax
import jax.numpy as jnp
from jax import lax
from jax.experimental import pallas as pl
from jax.experimental.pallas import tpu as pltpu

F32 = jnp.float32
MM = jnp.bfloat16
NORM_EPS = 1e-6
HEADS = 8
HEAD_DIM = 64
POOL_WINDOWS = (2, 4, 8, 16)
CONV_K = 31
LANES = 128
VMEM_LIMIT = 56 * 2**20
FFN_BWD_ROWS = 256
FFN_COLS = 768
ATTN_FWD_TILE = 1024
WGRAD_COLS = 768
WGRAD_ACC_BYTES = 12 * 2**20

ADAM_LR = 0.001
ADAM_B1 = 0.9
ADAM_B2 = 0.999
ADAM_EPS = 1e-08
ADAM_WD = 0.01
ADAM_STEP = 10

MESH = pl.DeviceIdType.MESH
BS = pl.BlockSpec
SDS = jax.ShapeDtypeStruct
ANY = pl.BlockSpec(memory_space=pl.ANY)


def _dot(a, b):
    return jnp.dot(a, b, preferred_element_type=F32)


def _dot_nt(a, b):
    return lax.dot_general(a, b, (((1,), (1,)), ((), ())), preferred_element_type=F32)


def _dot_tn(a, b):
    return lax.dot_general(a, b, (((0,), (0,)), ((), ())), preferred_element_type=F32)


def _pc(body, name, grid, in_specs, out_specs, out_shape, scratch=()):
    return pl.pallas_call(
        body, out_shape=out_shape, grid=grid, in_specs=in_specs, out_specs=out_specs,
        scratch_shapes=list(scratch), name=name,
        compiler_params=pltpu.CompilerParams(
            dimension_semantics=("arbitrary",) * len(grid), vmem_limit_bytes=VMEM_LIMIT))


def _rms_fwd(x, g):
    r = lax.rsqrt(jnp.mean(x * x, axis=-1, keepdims=True) + NORM_EPS)
    xh = x * r
    return xh, r, xh * g


def _rms_bwd(dh, xh, r, g):
    dxh = dh * g
    dx = r * (dxh - xh * jnp.mean(dxh * xh, axis=-1, keepdims=True))
    return dx, jnp.sum(dh * xh, axis=0, keepdims=True)


def _sigmoid(x):
    return jax.nn.sigmoid(x)


def _ffn_fwd(x, g, wg, wu, wd, hosted=None):
    T, D = x.shape
    F = wg.shape[1]
    tm = min(512, T)
    nt = T // tm
    pieces = [(c0, min(FFN_COLS, F - c0)) for c0 in range(0, F, FFN_COLS)]
    h_in, h_out, h_shape, h_scratch = _hosted_specs(hosted)

    def body(*refs):
        i = pl.program_id(0)
        refs, finish = _hosted_edges(hosted, refs, 5, 3, i == 0, i == nt - 1)
        x_ref, g_ref, wg_ref, wu_ref, wd_ref, o_ref, a_ref, b_ref = refs
        xv = x_ref[...]
        h = _rms_fwd(xv, g_ref[...])[2].astype(MM)
        acc = jnp.zeros((tm, D), F32)
        for c0, w in pieces:
            a = _dot(h, wg_ref[:, c0:c0 + w])
            b = _dot(h, wu_ref[:, c0:c0 + w])
            a_ref[:, c0:c0 + w] = a.astype(a_ref.dtype)
            b_ref[:, c0:c0 + w] = b.astype(b_ref.dtype)
            acc = acc + _dot(((a * _sigmoid(a)) * b).astype(MM), wd_ref[c0:c0 + w, :])
        o_ref[...] = xv + 0.5 * acc
        finish()

    tok = lambda i: (i, 0)
    par = lambda i: (0, 0)
    resident = lambda shape: BS(shape, par, pipeline_mode=pl.Buffered(1))
    return _pc(
        body, "ffn_fwd" + ("_hosting" if hosted else ""), (nt,),
        [BS((tm, D), tok), BS((1, D), par), resident((D, F)), resident((D, F)), resident((F, D))] + h_in,
        [BS((tm, D), tok), BS((tm, F), tok), BS((tm, F), tok)] + h_out,
        [SDS((T, D), F32), SDS((T, F), MM), SDS((T, F), MM)] + h_shape,
        scratch=h_scratch)(x, g, wg, wu, wd, *(hosted[1] if hosted else []))


def _ffn_bwd(x, dout, g, a, b, wg, wu, wd, hosted=None):
    T, D = x.shape
    F = wg.shape[1]
    tm = min(FFN_BWD_ROWS, T)
    nt = T // tm
    pieces = [(c0, min(FFN_COLS, F - c0)) for c0 in range(0, F, FFN_COLS)]
    h_in, h_out, h_shape, h_scratch = _hosted_specs(hosted)

    def body(*refs):
        i = pl.program_id(0)
        refs, finish = _hosted_edges(hosted, refs, 8, 7, i == 0, i == nt - 1)
        (x_ref, do_ref, g_ref, a_ref, b_ref, wg_ref, wu_ref, wd_ref,
         dx_ref, h_ref, dy_ref, da_ref, db_ref, s_ref, dg_ref) = refs

        @pl.when(i == 0)
        def _():
            dg_ref[...] = jnp.zeros_like(dg_ref)

        gv = g_ref[...]
        xh, r, hg = _rms_fwd(x_ref[...], gv)
        h_ref[...] = hg.astype(h_ref.dtype)
        dy = (0.5 * do_ref[...]).astype(MM)
        dy_ref[...] = dy
        dh = jnp.zeros((tm, D), F32)
        for c0, w in pieces:
            a = a_ref[:, c0:c0 + w].astype(F32)
            b = b_ref[:, c0:c0 + w].astype(F32)
            ds = _dot_nt(dy, wd_ref[c0:c0 + w, :])
            sig = _sigmoid(a)
            sl = a * sig
            s_ref[:, c0:c0 + w] = (sl * b).astype(s_ref.dtype)
            db = (ds * sl).astype(MM)
            da = (ds * b * (sig * (1.0 + a * (1.0 - sig)))).astype(MM)
            da_ref[:, c0:c0 + w] = da
            db_ref[:, c0:c0 + w] = db
            dh = dh + _dot_nt(da, wg_ref[:, c0:c0 + w]) + _dot_nt(db, wu_ref[:, c0:c0 + w])
        dx, dg = _rms_bwd(dh, xh, r, gv)
        dx_ref[...] = do_ref[...] + dx
        dg_ref[...] += dg
        finish()

    tok = lambda i: (i, 0)
    par = lambda i: (0, 0)
    hid = BS((tm, F), tok)
    resident = lambda shape: BS(shape, par, pipeline_mode=pl.Buffered(1))
    return _pc(
        body, "ffn_bwd" + ("_hosting" if hosted else ""), (nt,),
        [BS((tm, D), tok), BS((tm, D), tok), BS((1, D), par), hid, hid,
         resident((D, F)), resident((D, F)), resident((F, D))] + h_in,
        [BS((tm, D), tok), BS((tm, D), tok), BS((tm, D), tok), hid, hid, hid, BS((1, D), par)] + h_out,
        [SDS((T, D), F32), SDS((T, D), MM), SDS((T, D), MM),
         SDS((T, F), MM), SDS((T, F), MM), SDS((T, F), MM), SDS((1, D), F32)] + h_shape,
        scratch=h_scratch,
    )(x, dout, g, a, b, wg, wu, wd, *(hosted[1] if hosted else []))


def _wgrad(a, b, name, hosted=None):
    T, K = a.shape
    N = b.shape[1]
    tt = min(512, T)
    tn = next(c for c in (N, 1408, 1280, 1024, 512, 256, 128) if N % c == 0 and K * c * 4 <= WGRAD_ACC_BYTES)
    pieces = [(c0, min(WGRAD_COLS, tn - c0)) for c0 in range(0, tn, WGRAD_COLS)]
    nn, nt = N // tn, T // tt
    h_in, h_out, h_shape, h_scratch = _hosted_specs(hosted)

    def body(*refs):
        n, t = pl.program_id(0), pl.program_id(1)
        refs, finish = _hosted_edges(hosted, refs, 2, 1, (n == 0) & (t == 0), (n == nn - 1) & (t == nt - 1))
        a_ref, b_ref, o_ref = refs

        @pl.when(t == 0)
        def _():
            o_ref[...] = jnp.zeros_like(o_ref)

        av = a_ref[...].astype(MM)
        for c0, w in pieces:
            o_ref[:, c0:c0 + w] += _dot_tn(av, b_ref[:, c0:c0 + w].astype(MM))
        finish()

    res = _pc(
        body, name + ("_hosting" if hosted else ""), (nn, nt),
        [BS((tt, K), lambda n, t: (t, 0)), BS((tt, tn), lambda n, t: (t, n))] + h_in,
        [BS((K, tn), lambda n, t: (0, n))] + h_out, [SDS((K, N), F32)] + h_shape,
        scratch=h_scratch)(a, b, *(hosted[1] if hosted else []))
    return res if hosted else res[0]


C_POOL, C_Q, C_K, C_V, C_CA, C_CG, C_ZF, C_END = 0, 256, 768, 1280, 1792, 2048, 2304, 2560


def _mix_in_fwd(x, g, w):
    T, D = x.shape
    tm = min(512, T)

    def body(x_ref, g_ref, w_ref, up_ref, q_ref, k_ref, v_ref, ca_ref, cg_ref, zf_ref):
        _, _, hg = _rms_fwd(x_ref[...], g_ref[...])
        p = _dot(hg.astype(MM), w_ref[...])
        up_ref[...] = p[:, C_POOL:C_Q]
        q_ref[...] = p[:, C_Q:C_K].astype(q_ref.dtype)
        k_ref[...] = p[:, C_K:C_V].astype(k_ref.dtype)
        v_ref[...] = p[:, C_V:C_CA].astype(v_ref.dtype)
        ca_ref[...] = p[:, C_CA:C_CG]
        cg_ref[...] = p[:, C_CG:C_ZF]
        zf_ref[...] = p[:, C_ZF:C_ZF + LANES]

    tok = lambda i: (i, 0)
    widths = (256, 512, 512, 512, 256, 256, 128)
    dtypes = (F32, MM, MM, MM, F32, F32, F32)
    return _pc(
        body, "mix_in_fwd", (T // tm,),
        [BS((tm, D), tok), BS((1, D), lambda i: (0, 0)), BS((D, C_END), lambda i: (0, 0))],
        [BS((tm, wd), tok) for wd in widths],
        [SDS((T, wd), dt) for wd, dt in zip(widths, dtypes)])(x, g, w)


def _mix_in_bwd(x, dout, g, w, dup, dq, dk, dv, dca, dcg, dzf):
    T, D = x.shape
    tm = min(512, T)

    def body(x_ref, do_ref, g_ref, w_ref, dup_ref, dq_ref, dk_ref, dv_ref, dca_ref, dcg_ref, dzf_ref,
             dx_ref, h_ref, dp_ref, dg_ref):
        @pl.when(pl.program_id(0) == 0)
        def _():
            dg_ref[...] = jnp.zeros_like(dg_ref)

        gv = g_ref[...]
        xh, r, hg = _rms_fwd(x_ref[...], gv)
        h_ref[...] = hg.astype(h_ref.dtype)
        for ref, lo, hi in ((dup_ref, C_POOL, C_Q), (dq_ref, C_Q, C_K), (dk_ref, C_K, C_V), (dv_ref, C_V, C_CA),
                            (dca_ref, C_CA, C_CG), (dcg_ref, C_CG, C_ZF), (dzf_ref, C_ZF, C_ZF + LANES)):
            dp_ref[:, lo:hi] = ref[...].astype(dp_ref.dtype)
        dp_ref[:, C_ZF + LANES:C_END] = jnp.zeros((tm, C_END - C_ZF - LANES), dp_ref.dtype)
        dh = _dot_nt(dp_ref[...], w_ref[...])
        dx, dg = _rms_bwd(dh, xh, r, gv)
        dx_ref[...] = do_ref[...] + dx
        dg_ref[...] += dg

    tok = lambda i: (i, 0)
    widths = (256, 512, 512, 512, 256, 256, 128)
    return _pc(
        body, "mix_in_bwd", (T // tm,),
        [BS((tm, D), tok), BS((tm, D), tok), BS((1, D), lambda i: (0, 0)), BS((D, C_END), lambda i: (0, 0))]
        + [BS((tm, wd), tok) for wd in widths],
        [BS((tm, D), tok), BS((tm, D), tok), BS((tm, C_END), tok), BS((1, D), lambda i: (0, 0))],
        [SDS((T, D), F32), SDS((T, D), MM), SDS((T, C_END), MM), SDS((1, D), F32)],
    )(x, dout, g, w, dup, dq, dk, dv, dca, dcg, dzf)


def _mix_out_fwd(x, ya, yb, yc, wo):
    T, D = x.shape
    tm = min(512, T)

    def body(x_ref, ya_ref, yb_ref, yc_ref, wo_ref, o_ref):
        o_ref[...] = (x_ref[...] + _dot(ya_ref[...].astype(MM), wo_ref[0:256, :])
                      + _dot(yb_ref[...].astype(MM), wo_ref[256:768, :])
                      + _dot(yc_ref[...].astype(MM), wo_ref[768:1024, :]))

    tok = lambda i: (i, 0)
    return _pc(
        body, "mix_out_fwd", (T // tm,),
        [BS((tm, D), tok), BS((tm, 256), tok), BS((tm, 512), tok), BS((tm, 256), tok), BS((D, D), lambda i: (0, 0))],
        BS((tm, D), tok), SDS((T, D), F32))(x, ya, yb, yc, wo)


def _mix_out_bwd(dx, wo):
    T, D = dx.shape
    tm = min(512, T)

    def body(dx_ref, wo_ref, dya_ref, dyb_ref, dyc_ref):
        dy = _dot_nt(dx_ref[...].astype(MM), wo_ref[...])
        dya_ref[...] = dy[:, 0:256]
        dyb_ref[...] = dy[:, 256:768]
        dyc_ref[...] = dy[:, 768:1024]

    tok = lambda i: (i, 0)
    return _pc(
        body, "mix_out_bwd", (T // tm,),
        [BS((tm, D), tok), BS((D, D), lambda i: (0, 0))],
        [BS((tm, 256), tok), BS((tm, 512), tok), BS((tm, 256), tok)],
        [SDS((T, 256), F32), SDS((T, 512), F32), SDS((T, 256), F32)])(dx, wo)


def _fgate_fwd(zf, bias):
    T = zf.shape[0]
    tc = min(256, T)

    def body(z_ref, b_ref, f_ref, carry):
        @pl.when(pl.program_id(0) == 0)
        def _():
            carry[...] = jnp.zeros_like(carry)

        z = z_ref[...] + b_ref[...]
        logf = jnp.minimum(z, 0.0) - jnp.log(1.0 + jnp.exp(-jnp.abs(z)))
        row = lax.broadcasted_iota(jnp.int32, (tc, tc), 0)
        col = lax.broadcasted_iota(jnp.int32, (tc, tc), 1)
        tri = (col <= row).astype(F32)
        f_ref[...] = jnp.dot(tri, logf, precision=lax.Precision.HIGHEST, preferred_element_type=F32) + carry[...]
        carry[...] += jnp.sum(logf, axis=0, keepdims=True)

    return _pc(
        body, "fgate_fwd", (T // tc,),
        [BS((tc, LANES), lambda i: (i, 0)), BS((1, LANES), lambda i: (0, 0))],
        BS((tc, LANES), lambda i: (i, 0)), SDS((T, LANES), F32),
        scratch=[pltpu.VMEM((1, LANES), F32)])(zf, bias)


def _fgate_bwd(zf, bias, dFq, dFk):
    T = zf.shape[0]
    tc = min(256, T)
    n = T // tc
    slabs = dFq.shape[0]

    def body(z_ref, b_ref, dfq_ref, dfk_ref, dz_ref, db_ref, carry):
        @pl.when(pl.program_id(0) == 0)
        def _():
            carry[...] = jnp.zeros_like(carry)
            db_ref[...] = jnp.zeros_like(db_ref)

        df = dfk_ref[...]
        for sl in range(slabs):
            df = df + dfq_ref[sl]
        row = lax.broadcasted_iota(jnp.int32, (tc, tc), 0)
        col = lax.broadcasted_iota(jnp.int32, (tc, tc), 1)
        tri = (col >= row).astype(F32)
        dlogf = jnp.dot(tri, df, precision=lax.Precision.HIGHEST, preferred_element_type=F32) + carry[...]
        carry[...] += jnp.sum(df, axis=0, keepdims=True)
        lane = lax.broadcasted_iota(jnp.int32, (1, LANES), 1)
        dz = jnp.where(lane < HEADS, dlogf * _sigmoid(-(z_ref[...] + b_ref[...])), 0.0)
        dz_ref[...] = dz
        db_ref[...] += jnp.sum(dz, axis=0, keepdims=True)

    rev = lambda i: (n - 1 - i, 0)
    return _pc(
        body, "fgate_bwd", (n,),
        [BS((tc, LANES), rev), BS((1, LANES), lambda i: (0, 0)), BS((slabs, tc, LANES), lambda i: (0, n - 1 - i, 0)),
         BS((tc, LANES), rev)],
        [BS((tc, LANES), rev), BS((1, LANES), lambda i: (0, 0))],
        [SDS((T, LANES), F32), SDS((1, LANES), F32)],
        scratch=[pltpu.VMEM((1, LANES), F32)])(zf, bias, dFq, dFk)


LOG2E = 1.4426950408889634


def _split3(x):
    hi = x.astype(MM)
    r1 = x - hi.astype(F32)
    mid = r1.astype(MM)
    return hi, mid, (r1 - mid.astype(F32)).astype(MM)


def _place(lane, base, cols):
    out = jnp.zeros((cols[0].shape[0], LANES), MM)
    for i, c in enumerate(cols):
        out = jnp.where(lane == base + i, c, out)
    return out


def _head_col(block, lane, h):
    return jnp.sum(jnp.where(lane == h, block, 0.0), axis=-1, keepdims=True)


def _own_lanes(lane, hh):
    return (lane < HEAD_DIM) if hh == 0 else (lane >= HEAD_DIM)


def _attn_k_side(k_ref, f_ref, kb_ref, hp, T, rows, lse_ones, v_ref=None, vb_ref=None):
    lane = lax.broadcasted_iota(jnp.int32, (1, LANES), 1)
    one = jnp.ones((rows, 1), MM)

    def chunk(c, _):
        r0 = pl.multiple_of(c * rows, rows)
        kp = k_ref[pl.ds(r0, rows), :]
        fblk = f_ref[pl.ds(r0, rows), :]
        for hh in range(2):
            hi, mid, lo = _split3(-_head_col(fblk, lane, 2 * hp + hh) * LOG2E)
            cols = [one, one, one, hi, mid, lo] + ([one, one, one] if lse_ones else [])
            bias = _place(lane, HEAD_DIM * (1 - hh), cols)
            kb_ref[hh, pl.ds(r0, rows), :] = jnp.where(_own_lanes(lane, hh), kp, bias)
            if vb_ref is not None:
                vb_ref[hh, pl.ds(r0, rows), :] = jnp.where(_own_lanes(lane, hh), v_ref[pl.ds(r0, rows), :],
                                                           jnp.ones((rows, LANES), MM))
        return 0

    lax.fori_loop(0, T // rows, chunk, 0)


def _attn_q_side(qp, fblk, lane, hp, scale, lse_blk=None):
    qc = qp.astype(F32) * (scale * LOG2E)
    qhi = qc.astype(MM)
    qlo = (qc - qhi.astype(F32)).astype(MM)
    one = jnp.ones((qp.shape[0], 1), MM)
    out = []
    for hh in range(2):
        cols = list(_split3(_head_col(fblk, lane, 2 * hp + hh) * LOG2E)) + [one, one, one]
        if lse_blk is not None:
            cols += list(_split3(-_head_col(lse_blk, lane, 2 * hp + hh)))
        bias = _place(lane, HEAD_DIM * (1 - hh), cols)
        own = _own_lanes(lane, hh)
        out.append(jnp.concatenate([jnp.where(own, qhi, jnp.zeros_like(qhi)), jnp.where(own, qlo, bias)], axis=1))
    return out


def _causal(tq, tk):
    return lax.broadcasted_iota(jnp.int32, (tq, tk), 1) <= lax.broadcasted_iota(jnp.int32, (tq, tk), 0)


def _hosted_specs(hosted):
    if hosted is None:
        return [], [], [], []
    kind, arrays = hosted
    n = len(arrays)
    return [ANY] * n, [ANY] * n, [kind.out_shape(a) for a in arrays], kind.scratch(n)


def _hosted_edges(hosted, refs, n_in, n_out, first, last, mid=None):
    if hosted is None:
        return refs, lambda: None
    kind, arrays = hosted
    n = len(arrays)
    nsem = len(kind.scratch(n))
    o0 = n_in + n + n_out
    ins, outs, sems = refs[n_in:n_in + n], refs[o0:o0 + n], refs[len(refs) - nsem:]
    relayed = mid is not None and hasattr(kind, "relay")

    @pl.when(first)
    def _():
        kind.start(ins, outs, *sems)

    if relayed:
        @pl.when(mid)
        def _():
            kind.relay(ins, outs, *sems)

    def finish():
        @pl.when(last)
        def _():
            kind.wait(ins, outs, *sems, **({"relayed": True} if relayed else {}))

    return refs[:n_in] + refs[n_in + n:o0] + refs[o0 + n:len(refs) - nsem], finish


def _attn_fwd(q, k, v, F, hosted=None):
    T = q.shape[0]
    tq = min(ATTN_FWD_TILE, T)
    tk = tq
    nq = T // tq
    scale = 1.0 / math.sqrt(HEAD_DIM)
    h_in, h_out, h_shape, h_scratch = _hosted_specs(hosted)

    def body(*refs):
        hp, ib = pl.program_id(0), pl.program_id(1)
        refs, finish = _hosted_edges(hosted, refs, 5, 2, (hp == 0) & (ib == 0), (hp == HEADS // 2 - 1) & (ib == nq - 1),
                                     mid=(hp == HEADS // 2 - 1) & (ib == 0))
        q_ref, k_ref, v_ref, fq_ref, f_ref, o_ref, lse_ref, kb_ref, vb_ref = refs
        lane = lax.broadcasted_iota(jnp.int32, (1, LANES), 1)

        @pl.when(ib == 0)
        def _():
            _attn_k_side(k_ref, f_ref, kb_ref, hp, T, min(512, T), False, v_ref, vb_ref)

        qa = _attn_q_side(q_ref[...], fq_ref[...], lane, hp, scale)

        def tile(jb, carry, masked):
            off = pl.multiple_of(jb * tk, tk)
            kp = k_ref[pl.ds(off, tk), :]
            new = []
            for hh in range(2):
                m, acc = carry[hh]
                s = _dot_nt(qa[hh], jnp.concatenate([kp, kb_ref[hh, pl.ds(off, tk), :]], axis=1))
                if masked:
                    s = jnp.where(_causal(tq, tk), s, -jnp.inf)
                m2 = jnp.maximum(m, jnp.max(s, axis=-1, keepdims=True))
                p = jnp.exp2(s - m2)
                new.append((m2, acc * jnp.exp2(m - m2) + _dot(p.astype(MM), vb_ref[hh, pl.ds(off, tk), :])))
            return tuple(new)

        init = tuple((jnp.full((tq, 1), -jnp.inf, F32), jnp.zeros((tq, LANES), F32)) for _ in range(2))
        carry = lax.fori_loop(0, ib, lambda jb, c: tile(jb, c, False), init)
        (m0, a0), (m1, a1) = tile(ib, carry, True)
        l0, l1 = a0[:, HEAD_DIM:HEAD_DIM + 1], a1[:, 0:1]
        o_ref[...] = jnp.where(lane < HEAD_DIM, a0 / l0, a1 / l1)
        lse_ref[...] = jnp.where(lane == 2 * hp, m0 + jnp.log2(l0), jnp.where(lane == 2 * hp + 1, m1 + jnp.log2(l1), 0.0))
        finish()

    blk = lambda h, i: (i, h)
    full = lambda h, i: (0, h)
    return _pc(
        body, "attn_fwd" + ("_hosting" if hosted else ""), (HEADS // 2, nq),
        [BS((tq, LANES), blk), BS((T, LANES), full), BS((T, LANES), full), BS((tq, LANES), lambda h, i: (i, 0)),
         BS((T, LANES), lambda h, i: (0, 0))] + h_in,
        [BS((tq, LANES), blk), BS((None, tq, LANES), lambda h, i: (h, i, 0))] + h_out,
        [SDS((T, HEADS * HEAD_DIM), F32), SDS((HEADS // 2, T, LANES), F32)] + h_shape,
        scratch=[pltpu.VMEM((2, T, LANES), MM)] * 2 + h_scratch)(q, k, v, F, F, *(hosted[1] if hosted else []))


def _attn_bwd(q, k, v, F, o, lse, do, hosted=None):
    T = q.shape[0]
    tq = min(512, T)
    tk = tq
    nq = T // tq
    scale = 1.0 / math.sqrt(HEAD_DIM)
    h_in, h_out, h_shape, h_scratch = _hosted_specs(hosted)

    def body(*refs):
        hp, ib = pl.program_id(0), pl.program_id(1)
        refs, finish = _hosted_edges(hosted, refs, 8, 5, (hp == 0) & (ib == 0), (hp == HEADS // 2 - 1) & (ib == nq - 1))
        (q_ref, k_ref, v_ref, fq_ref, f_ref, o_ref, lse_ref, do_ref,
         dq_ref, dk_ref, dv_ref, dfq_ref, dfk_ref, kb_ref, dk_acc, dv_acc) = refs
        lane = lax.broadcasted_iota(jnp.int32, (1, LANES), 1)

        @pl.when(ib == 0)
        def _():
            _attn_k_side(k_ref, f_ref, kb_ref, hp, T, tk, True)
            dk_acc[...] = jnp.zeros_like(dk_acc)
            dv_acc[...] = jnp.zeros_like(dv_acc)
            dfk_ref[...] = jnp.zeros_like(dfk_ref)

        qp = q_ref[...]
        qa = _attn_q_side(qp, fq_ref[...], lane, hp, scale, lse_ref[...])
        dob = do_ref[...].astype(MM)
        dprod = dob.astype(F32) * o_ref[...]
        qs = (qp.astype(F32) * scale).astype(MM)
        heads = []
        for hh in range(2):
            own = _own_lanes(lane, hh)
            heads.append((jnp.where(own, dob, jnp.zeros_like(dob)), jnp.where(own, qs, jnp.zeros_like(qs)),
                          jnp.sum(jnp.where(own, dprod, 0.0), axis=-1, keepdims=True)))

        def tile(jb, carry, masked):
            off = pl.multiple_of(jb * tk, tk)
            kp = k_ref[pl.ds(off, tk), :]
            vp = v_ref[pl.ds(off, tk), :]
            new = []
            dv_t = jnp.zeros((tk, LANES), F32)
            dk_t = jnp.zeros((tk, LANES), F32)
            for hh in range(2):
                dq, rs = carry[hh]
                dom, qm, delta = heads[hh]
                p = jnp.exp2(_dot_nt(qa[hh], jnp.concatenate([kp, kb_ref[hh, pl.ds(off, tk), :]], axis=1)))
                if masked:
                    p = jnp.where(_causal(tq, tk), p, 0.0)
                ds = p * (_dot_nt(dom, vp) - delta)
                dsb = ds.astype(MM)
                dv_t = dv_t + _dot_tn(p.astype(MM), dom)
                dk_t = dk_t + _dot_tn(dsb, qm)
                dfk_ref[jb, pl.ds(hh, 1), :] -= jnp.sum(ds, axis=0, keepdims=True)
                new.append((dq + _dot(dsb, kp), rs + jnp.sum(ds, axis=-1, keepdims=True)))
            dv_acc[pl.ds(off, tk), :] += dv_t
            dk_acc[pl.ds(off, tk), :] += dk_t
            return tuple(new)

        init = tuple((jnp.zeros((tq, LANES), F32), jnp.zeros((tq, 1), F32)) for _ in range(2))
        carry = lax.fori_loop(0, ib, lambda jb, c: tile(jb, c, False), init)
        (dq0, rs0), (dq1, rs1) = tile(ib, carry, True)
        dq_ref[...] = (jnp.where(lane < HEAD_DIM, dq0, dq1) * scale).astype(dq_ref.dtype)
        dfq_ref[...] = jnp.where(lane == 2 * hp, rs0, jnp.where(lane == 2 * hp + 1, rs1, 0.0))

        @pl.when(ib == nq - 1)
        def _():
            dk_ref[...] = dk_acc[...].astype(dk_ref.dtype)
            dv_ref[...] = dv_acc[...].astype(dv_ref.dtype)

        finish()

    blk = lambda h, i: (i, h)
    full = lambda h, i: (0, h)
    slab = BS((None, tq, LANES), lambda h, i: (h, i, 0))
    return _pc(
        body, "attn_bwd" + ("_hosting" if hosted else ""), (HEADS // 2, nq),
        [BS((tq, LANES), blk), BS((T, LANES), full), BS((T, LANES), full), BS((tq, LANES), lambda h, i: (i, 0)),
         BS((T, LANES), lambda h, i: (0, 0)), BS((tq, LANES), blk), slab, BS((tq, LANES), blk)] + h_in,
        [BS((tq, LANES), blk), BS((T, LANES), full), BS((T, LANES), full), slab,
         BS((None, nq, 2, tk), lambda h, i: (h, 0, 0, 0))] + h_out,
        [SDS((T, HEADS * HEAD_DIM), MM)] * 3 + [SDS((HEADS // 2, T, LANES), F32), SDS((HEADS // 2, nq, 2, tk), F32)]
        + h_shape,
        scratch=[pltpu.VMEM((2, T, LANES), MM), pltpu.VMEM((T, LANES), F32), pltpu.VMEM((T, LANES), F32)] + h_scratch,
    )(q, k, v, F, F, o, lse, do, *(hosted[1] if hosted else []))


POOL_HALO = 16
CONV_HALO = 32


def _group_select(lane, v0, v1, v2, v3):
    return jnp.where(lane < 64, v0, jnp.where(lane < 128, v1, jnp.where(lane < 192, v2, v3)))


def _roll_down(x, k):
    return x if k == 0 else pltpu.roll(x, k, 0)


def _roll_up(x, k):
    return x if k == 0 else pltpu.roll(x, x.shape[0] - k, 0)


def _pool_terms(u, u_prev, tile, tm):
    ext = jnp.concatenate([u_prev, u], axis=0)
    s2 = ext + _roll_down(ext, 1)
    s4 = s2 + _roll_down(s2, 2)
    s8 = s4 + _roll_down(s4, 4)
    s16 = s8 + _roll_down(s8, 8)
    lane = lax.broadcasted_iota(jnp.int32, (1, 256), 1)
    ws = _group_select(lane, s2, s4, s8, s16)[POOL_HALO:, :]
    wlen = _group_select(lane, *map(float, POOL_WINDOWS)).astype(F32)
    return ws / _pool_count(tile, tm, tm, wlen) - u


def _pool_count(tile, tm, rows, wlen):
    t = (tile * tm + 1 + lax.broadcasted_iota(jnp.int32, (rows, 1), 0)).astype(F32)
    return jnp.minimum(t, wlen)


def _layer_norm(y, lg, lb):
    mu = jnp.mean(y, axis=-1, keepdims=True)
    yc = y - mu
    rstd = lax.rsqrt(jnp.mean(yc * yc, axis=-1, keepdims=True) + NORM_EPS)
    yh = yc * rstd
    return yh, rstd, yh * lg + lb


def _halo_specs(tm, T, halo, prev):
    per = tm // halo
    if prev:
        return BS((halo, 256), lambda i: (jnp.maximum(i * per - 1, 0), 0))
    return BS((halo, 256), lambda i: (jnp.minimum((i + 1) * per, T // halo - 1), 0))


def _local_fwd(up, ca, cg, bd, pscale, cw, cb, lg, lb):
    T = up.shape[0]
    tm = min(512, T)

    def body(up_ref, uph_ref, ca_ref, cah_ref, cg_ref, cgh_ref, bd_ref, ps_ref, cw_ref, cb_ref, lg_ref, lb_ref,
             ya_ref, yc_ref, u_ref, y_ref):
        i = pl.program_id(0)
        first = i == 0
        pooled = _pool_terms(up_ref[...], jnp.where(first, 0.0, uph_ref[...]), i, tm)
        ya_ref[...] = (_dot(pooled.astype(MM), bd_ref[...]) * ps_ref[...]).astype(ya_ref.dtype)

        u = ca_ref[...] * _sigmoid(cg_ref[...])
        uh = jnp.where(first, 0.0, cah_ref[...] * _sigmoid(cgh_ref[...]))
        ext = jnp.concatenate([uh, u], axis=0)
        y = jnp.zeros((tm, 256), F32) + cb_ref[...]
        for kk in range(CONV_K):
            y = y + cw_ref[kk:kk + 1, :] * _roll_up(ext, CONV_HALO - (CONV_K - 1) + kk)[:tm, :]
        _, _, z = _layer_norm(y, lg_ref[...], lb_ref[...])
        yc_ref[...] = (z * _sigmoid(z)).astype(yc_ref.dtype)
        u_ref[...] = u
        y_ref[...] = y

    tok = lambda i: (i, 0)
    par = lambda i: (0, 0)
    t256 = BS((tm, 256), tok)
    return _pc(
        body, "local_fwd", (T // tm,),
        [t256, _halo_specs(tm, T, POOL_HALO, True), t256, _halo_specs(tm, T, CONV_HALO, True),
         t256, _halo_specs(tm, T, CONV_HALO, True),
         BS((256, 256), par), BS((1, 256), par), BS((32, 256), par), BS((1, 256), par), BS((1, 256), par),
         BS((1, 256), par)],
        [t256, t256, t256, t256],
        [SDS((T, 256), MM), SDS((T, 256), MM), SDS((T, 256), F32), SDS((T, 256), F32)],
    )(up, up, ca, ca, cg, cg, bd, pscale, cw, cb, lg, lb)


def _local_bwd(up, dya, ca, cg, u, y, dyc, bd, pscale, cw, lg, lb):
    T = up.shape[0]
    tm = min(512, T)
    n = T // tm

    def body(up_ref, uph_ref, dya_ref, dyan_ref, ca_ref, cg_ref, u_ref, y_ref, yn_ref, dyc_ref, dycn_ref,
             bd_ref, ps_ref, cw_ref, lg_ref, lb_ref,
             dup_ref, dca_ref, dcg_ref, dbd_ref, dps_ref, dcw_ref, dcb_ref, dlg_ref, dlb_ref):
        i = pl.program_id(0)
        first = i == 0
        last = i == n - 1

        @pl.when(first)
        def _():
            for ref in (dbd_ref, dps_ref, dcw_ref, dcb_ref, dlg_ref, dlb_ref):
                ref[...] = jnp.zeros_like(ref)

        ps = ps_ref[...]
        pooled = _pool_terms(up_ref[...], jnp.where(first, 0.0, uph_ref[...]), i, tm).astype(MM)
        dya_t = dya_ref[...]
        dps_ref[...] += jnp.sum(dya_t * _dot(pooled, bd_ref[...]), axis=0, keepdims=True)
        dm = (jnp.concatenate([dya_t, jnp.where(last, 0.0, dyan_ref[...])], axis=0) * ps).astype(MM)
        dbd_ref[...] += _dot_tn(pooled, dm[:tm, :])
        dpool = _dot_nt(dm, bd_ref[...])
        lane = lax.broadcasted_iota(jnp.int32, (1, 256), 1)
        wlen = _group_select(lane, *map(float, POOL_WINDOWS)).astype(F32)
        e = dpool / _pool_count(i, tm, tm + POOL_HALO, wlen)
        f2 = e + _roll_up(e, 1)
        f4 = f2 + _roll_up(f2, 2)
        f8 = f4 + _roll_up(f4, 4)
        f16 = f8 + _roll_up(f8, 8)
        dup_ref[...] = _group_select(lane, f2, f4, f8, f16)[:tm, :] - dpool[:tm, :]

        lgv = lg_ref[...]
        yext = jnp.concatenate([y_ref[...], yn_ref[...]], axis=0)
        dyc = jnp.concatenate([dyc_ref[...], jnp.where(last, 0.0, dycn_ref[...])], axis=0)
        yh, rstd, z = _layer_norm(yext, lgv, lb_ref[...])
        sig = _sigmoid(z)
        dz = dyc * (sig * (1.0 + z * (1.0 - sig)))
        dlg_ref[...] += jnp.sum((dz * yh)[:tm, :], axis=0, keepdims=True)
        dlb_ref[...] += jnp.sum(dz[:tm, :], axis=0, keepdims=True)
        dyh = dz * lgv
        dy = rstd * (dyh - jnp.mean(dyh, axis=-1, keepdims=True) - yh * jnp.mean(dyh * yh, axis=-1, keepdims=True))
        dcb_ref[...] += jnp.sum(dy[:tm, :], axis=0, keepdims=True)
        uv = u_ref[...]
        du = jnp.zeros((tm, 256), F32)
        for kk in range(CONV_K):
            ahead = _roll_up(dy, CONV_K - 1 - kk)[:tm, :]
            dcw_ref[kk:kk + 1, :] += jnp.sum(uv * ahead, axis=0, keepdims=True)
            du = du + cw_ref[kk:kk + 1, :] * ahead
        sg = _sigmoid(cg_ref[...])
        dca_ref[...] = du * sg
        dcg_ref[...] = du * ca_ref[...] * sg * (1.0 - sg)

    tok = lambda i: (i, 0)
    par = lambda i: (0, 0)
    t256 = BS((tm, 256), tok)
    p1 = BS((1, 256), par)
    return _pc(
        body, "local_bwd", (n,),
        [t256, _halo_specs(tm, T, POOL_HALO, True), t256, _halo_specs(tm, T, POOL_HALO, False), t256, t256,
         t256, t256, _halo_specs(tm, T, CONV_HALO, False), t256, _halo_specs(tm, T, CONV_HALO, False),
         BS((256, 256), par), p1, BS((32, 256), par), p1, p1],
        [t256, t256, t256, BS((256, 256), par), p1, BS((32, 256), par), p1, p1, p1],
        [SDS((T, 256), F32)] * 3 + [SDS((256, 256), F32), SDS((1, 256), F32), SDS((32, 256), F32)]
        + [SDS((1, 256), F32)] * 3,
    )(up, up, dya, dya, ca, cg, u, y, y, dyc, dyc, bd, pscale, cw, lg, lb)


def _head(x, g, target):
    T, D = x.shape
    tm = min(512, T)

    def body(x_ref, g_ref, t_ref, loss_ref, dx_ref, dg_ref):
        @pl.when(pl.program_id(0) == 0)
        def _():
            loss_ref[...] = jnp.zeros_like(loss_ref)
            dg_ref[...] = jnp.zeros_like(dg_ref)

        gv = g_ref[...]
        xh, r, yv = _rms_fwd(x_ref[...], gv)
        err = yv - t_ref[...]
        loss_ref[...] += 0.5 * jnp.sum(jnp.mean(err * err, axis=-1, keepdims=True), axis=0, keepdims=True)
        dx, dg = _rms_bwd(err * (1.0 / D), xh, r, gv)
        dx_ref[...] = dx
        dg_ref[...] += dg

    tok = lambda i: (i, 0)
    par = lambda i: (0, 0)
    return _pc(
        body, "head", (T // tm,),
        [BS((tm, D), tok), BS((1, D), par), BS((tm, D), tok)],
        [BS((1, LANES), par), BS((tm, D), tok), BS((1, D), par)],
        [SDS((1, LANES), F32), SDS((T, D), F32), SDS((1, D), F32)])(x, g, target)


def _adamw(w, gs, m, v, name):
    R, C = w.shape
    tr = R
    for cand in (512, 256, 128, 64, 32, 16, 8):
        if R % cand == 0:
            tr = cand
            break
    stacked = not isinstance(gs, (list, tuple))
    ng = 1 if stacked else len(gs)

    def body(*refs):
        w_ref, g_refs, m_ref, v_ref = refs[0], refs[1:1 + ng], refs[1 + ng], refs[2 + ng]
        g_ref, d_ref, m2_ref, v2_ref = refs[3 + ng:]
        terms = [g_refs[0][d] for d in range(gs.shape[0])] if stacked else [r[...] for r in g_refs]
        g = terms[0]
        for term in terms[1:]:
            g = g + term
        m2 = ADAM_B1 * m_ref[...] + (1.0 - ADAM_B1) * g
        v2 = ADAM_B2 * v_ref[...] + (1.0 - ADAM_B2) * jnp.square(g)
        m_hat = m2 / (1.0 - ADAM_B1 ** ADAM_STEP)
        v_hat = v2 / (1.0 - ADAM_B2 ** ADAM_STEP)
        g_ref[...] = g
        d_ref[...] = -ADAM_LR * (m_hat / (jnp.sqrt(v_hat) + ADAM_EPS) + ADAM_WD * w_ref[...])
        m2_ref[...] = m2
        v2_ref[...] = v2

    blk = BS((tr, C), lambda i: (i, 0))
    g_specs = [BS((gs.shape[0], tr, C), lambda i: (0, i, 0))] if stacked else [blk] * ng
    return _pc(body, name, (R // tr,), [blk] + g_specs + [blk, blk], [blk] * 4,
               [SDS((R, C), F32)] * 4)(w, *([gs] if stacked else gs), m, v)


def _sum_parts(owns, recvs, name):
    L = len(owns)
    R, C = owns[0].shape
    tr = next(t for t in (512, 256, 128, 64, 32, 16) if R % t == 0)

    def body(*refs):
        l = pl.program_id(0)
        s_ref = refs[2 * L]
        for ll in range(L):
            @pl.when(l == ll)
            def _(o_ref=refs[ll], r_ref=refs[L + ll]):
                s_ref[...] = ((o_ref[...] + r_ref[0].astype(F32)) + r_ref[1].astype(F32)) + r_ref[2].astype(F32)

    own_specs = [BS((tr, C), lambda l, i, ll=ll: (jnp.where(l == ll, i, 0), 0)) for ll in range(L)]
    recv_specs = [BS((3, tr, C), lambda l, i, ll=ll: (0, jnp.where(l == ll, i, 0), 0)) for ll in range(L)]
    return _pc(body, name, (L, R // tr), own_specs + recv_specs,
               BS((None, tr, C), lambda l, i: (l, i, 0)), SDS((L, R, C), F32))(*owns, *recvs)


def _sum8(parts, name):
    _, R, C = parts.shape

    def body(p_ref, s_ref):
        acc = p_ref[0]
        for d in range(1, 8):
            acc = acc + p_ref[d]
        s_ref[...] = acc

    return _pc(body, name, (1,), [BS((8, R, C), lambda i: (0, 0, 0))], BS((R, C), lambda i: (0, 0)),
               SDS((R, C), F32))(parts)


def _position():
    return lax.axis_index("x"), lax.axis_index("y"), lax.axis_index("c")


CHIP_FLIPS = ((1, 0), (0, 1), (1, 1))


class _GatherChips:
    @staticmethod
    def scratch(n):
        return [pltpu.SemaphoreType.DMA((3 * n,)), pltpu.SemaphoreType.DMA((3 * n,)), pltpu.SemaphoreType.DMA((n,))]

    @staticmethod
    def out_shape(block):
        return SDS((4,) + tuple(block.shape), block.dtype)

    @staticmethod
    def _copies(ins, outs, send_sems, recv_sems, local_sems, arrivals):
        x, y, c = _position()
        local, remote = [], []
        for i, (in_ref, out_ref) in enumerate(zip(ins, outs)):
            local.append(pltpu.make_async_copy(in_ref, out_ref.at[2 * x + y], local_sems.at[i]))
            for k, (fx, fy) in enumerate(CHIP_FLIPS):
                slot = 2 * (x ^ fx) + (y ^ fy) if arrivals else 2 * x + y
                remote.append(pltpu.make_async_remote_copy(
                    src_ref=in_ref, dst_ref=out_ref.at[slot], send_sem=send_sems.at[3 * i + k],
                    recv_sem=recv_sems.at[3 * i + k], device_id=(x ^ fx, y ^ fy, c), device_id_type=MESH))
        return local, remote

    @classmethod
    def start(cls, ins, outs, *sems):
        local, sends = cls._copies(ins, outs, *sems, arrivals=False)
        for cp in local + sends:
            cp.start()

    @classmethod
    def wait(cls, ins, outs, *sems):
        local, arrivals = cls._copies(ins, outs, *sems, arrivals=True)
        for cp in arrivals:
            cp.wait_recv()
        for cp in arrivals:
            cp.wait_send()
        for cp in local:
            cp.wait()


class _GatherChipsSplit(_GatherChips):
    @staticmethod
    def scratch(n):
        return [pltpu.SemaphoreType.DMA((6 * n,)), pltpu.SemaphoreType.DMA((6 * n,)), pltpu.SemaphoreType.DMA((n,))]

    @staticmethod
    def _half(ref, which):
        rows = ref.shape[0] // 2
        return ref.at[pl.ds(pl.multiple_of(which * rows, 16), rows)]

    @staticmethod
    def _local(ins, outs, local_sems):
        x, y, _ = _position()
        return [pltpu.make_async_copy(in_ref, out_ref.at[2 * x + y], local_sems.at[i])
                for i, (in_ref, out_ref) in enumerate(zip(ins, outs))]

    @classmethod
    def _between_chips(cls, ins, outs, send_sems, recv_sems, arrivals):
        x, y, c = _position()
        return [
            pltpu.make_async_remote_copy(
                src_ref=cls._half(in_ref, c),
                dst_ref=cls._half(out_ref.at[2 * (x ^ fx) + (y ^ fy) if arrivals else 2 * x + y], c),
                send_sem=send_sems.at[6 * i + k], recv_sem=recv_sems.at[6 * i + k],
                device_id=(x ^ fx, y ^ fy, c), device_id_type=MESH)
            for i, (in_ref, out_ref) in enumerate(zip(ins, outs)) for k, (fx, fy) in enumerate(CHIP_FLIPS)]

    @classmethod
    def _between_cores(cls, outs, send_sems, recv_sems, arrivals):
        x, y, c = _position()
        copies = []
        for i, out_ref in enumerate(outs):
            for k, (fx, fy) in enumerate(CHIP_FLIPS):
                half = cls._half(out_ref.at[2 * (x ^ fx) + (y ^ fy)], 1 - c if arrivals else c)
                copies.append(pltpu.make_async_remote_copy(
                    src_ref=half, dst_ref=half, send_sem=send_sems.at[6 * i + 3 + k],
                    recv_sem=recv_sems.at[6 * i + 3 + k], device_id=(x, y, 1 - c), device_id_type=MESH))
        return copies

    @classmethod
    def start(cls, ins, outs, send_sems, recv_sems, local_sems):
        for cp in cls._local(ins, outs, local_sems) + cls._between_chips(ins, outs, send_sems, recv_sems, False):
            cp.start()

    @classmethod
    def relay(cls, ins, outs, send_sems, recv_sems, local_sems):
        arrivals = cls._between_chips(ins, outs, send_sems, recv_sems, True)
        onward = cls._between_cores(outs, send_sems, recv_sems, False)
        for cp, nxt in zip(arrivals, onward):
            cp.wait_recv()
            nxt.start()

    @classmethod
    def wait(cls, ins, outs, send_sems, recv_sems, local_sems, relayed=False):
        if not relayed:
            cls.relay(ins, outs, send_sems, recv_sems, local_sems)
        for cp in cls._between_cores(outs, send_sems, recv_sems, True):
            cp.wait_recv()
        for cp in (cls._between_chips(ins, outs, send_sems, recv_sems, True)
                   + cls._between_cores(outs, send_sems, recv_sems, False)):
            cp.wait_send()
        for cp in cls._local(ins, outs, local_sems):
            cp.wait()


class _Symmetric:
    @classmethod
    def start(cls, ins, outs, *sems):
        for cp in cls._copies(ins, outs, *sems):
            cp.start()

    @classmethod
    def wait(cls, ins, outs, *sems):
        copies = cls._copies(ins, outs, *sems)
        for cp in copies:
            cp.wait_recv()
        for cp in copies:
            cp.wait_send()


class _ScatterChips(_Symmetric):
    @staticmethod
    def scratch(n):
        return [pltpu.SemaphoreType.DMA((3 * n,)), pltpu.SemaphoreType.DMA((3 * n,))]

    @staticmethod
    def out_shape(parts):
        return SDS((3,) + tuple(parts.shape[1:]), parts.dtype)

    @staticmethod
    def _copies(ins, outs, send_sems, recv_sems):
        x, y, c = _position()
        return [
            pltpu.make_async_remote_copy(
                src_ref=in_ref.at[2 * (x ^ fx) + (y ^ fy)], dst_ref=out_ref.at[k],
                send_sem=send_sems.at[3 * i + k], recv_sem=recv_sems.at[3 * i + k],
                device_id=(x ^ fx, y ^ fy, c), device_id_type=MESH)
            for i, (in_ref, out_ref) in enumerate(zip(ins, outs)) for k, (fx, fy) in enumerate(CHIP_FLIPS)]


class _SwapCores(_Symmetric):
    @staticmethod
    def scratch(n):
        return [pltpu.SemaphoreType.DMA((n,)), pltpu.SemaphoreType.DMA((n,))]

    @staticmethod
    def out_shape(block):
        return SDS(block.shape, block.dtype)

    @staticmethod
    def _copies(ins, outs, send_sems, recv_sems):
        x, y, c = _position()
        return [
            pltpu.make_async_remote_copy(
                src_ref=in_ref, dst_ref=out_ref, send_sem=send_sems.at[i], recv_sem=recv_sems.at[i],
                device_id=(x, y, 1 - c), device_id_type=MESH)
            for i, (in_ref, out_ref) in enumerate(zip(ins, outs))]


def _exchange(kind, arrays, name):
    n = len(arrays)

    def body(*refs):
        ins, outs, sems = refs[:n], refs[n:2 * n], refs[2 * n:]
        kind.start(ins, outs, *sems)
        kind.wait(ins, outs, *sems)

    return pl.pallas_call(body, out_shape=[kind.out_shape(a) for a in arrays], in_specs=[ANY] * n,
                          out_specs=[ANY] * n, name=name, scratch_shapes=kind.scratch(n))(*arrays)


DEVICE_FLIPS = tuple((fx, fy, fc) for fx in (0, 1) for fy in (0, 1) for fc in (0, 1))[1:]


class _GatherDevices(_GatherChips):
    @staticmethod
    def scratch(n):
        return [pltpu.SemaphoreType.DMA((7 * n,)), pltpu.SemaphoreType.DMA((7 * n,)), pltpu.SemaphoreType.DMA((n,))]

    @staticmethod
    def out_shape(block):
        return SDS((8,) + tuple(block.shape), block.dtype)

    @staticmethod
    def _copies(ins, outs, send_sems, recv_sems, local_sems, arrivals):
        x, y, c = _position()
        local, remote = [], []
        for i, (in_ref, out_ref) in enumerate(zip(ins, outs)):
            local.append(pltpu.make_async_copy(in_ref, out_ref.at[4 * x + 2 * y + c], local_sems.at[i]))
            for k, (fx, fy, fc) in enumerate(DEVICE_FLIPS):
                slot = 4 * (x ^ fx) + 2 * (y ^ fy) + (c ^ fc) if arrivals else 4 * x + 2 * y + c
                remote.append(pltpu.make_async_remote_copy(
                    src_ref=in_ref, dst_ref=out_ref.at[slot], send_sem=send_sems.at[7 * i + k],
                    recv_sem=recv_sems.at[7 * i + k], device_id=(x ^ fx, y ^ fy, c ^ fc), device_id_type=MESH))
        return local, remote


BIG = ("ffn1_w_gate", "ffn1_w_up", "ffn1_w_down", "w_in", "w_out", "ffn2_w_gate", "ffn2_w_up", "ffn2_w_down")
COL_SHARDED = ("ffn1_w_gate", "ffn1_w_up", "w_in", "ffn2_w_gate", "ffn2_w_up")
FIRST = tuple((n, 0) for n in ("ffn1_w_gate", "ffn1_w_up", "ffn1_w_down"))
LATE = tuple((n, 0) for n in ("w_out", "ffn2_w_gate", "ffn2_w_up", "ffn2_w_down")) + tuple((n, 1) for n in BIG)


def _to_shards(name, full):
    r, c = full.shape
    if name in COL_SHARDED:
        return full.reshape(r, 4, c // 4).transpose(1, 0, 2)
    return full.reshape(4, r // 4, c)


def _own_shard(name, full, chip):
    r, c = full.shape
    if name in COL_SHARDED:
        return lax.dynamic_slice_in_dim(full, chip * (c // 4), c // 4, axis=1)
    return lax.dynamic_slice_in_dim(full, chip * (r // 4), r // 4, axis=0)


def _from_shards(name, sh):
    _, r, c = sh.shape
    if name in COL_SHARDED:
        return sh.transpose(1, 0, 2).reshape(r, 4 * c)
    return sh.reshape(4 * r, c)


def _pad_w_in(w):
    return jnp.concatenate([w[:, :1792], w[:, 1800:2312], w[:, 1792:1800], jnp.zeros((w.shape[0], 248), w.dtype)], axis=1)


def _unpad_w_in(g):
    return jnp.concatenate([g[:, :1792], g[:, 2304:2312], g[:, 1792:2304]], axis=1)


def _block_diag(pw):
    out = jnp.zeros((256, 256), pw.dtype)
    for gidx in range(4):
        out = lax.dynamic_update_slice(out, pw[gidx], (64 * gidx, 64 * gidx))
    return out


SMALL = ("ffn1_norm", "mix_norm", "pool_w", "pool_scale", "forget_bias", "conv_b", "conv_ln_g", "conv_ln_b",
         "ffn2_norm", "final_norm")


def _grad_parts(grads, pieces):
    return [_to_shards(n, grads[n][l]).astype(MM) for n, l in pieces]


def _forward_backward(x, target, W, shards=None):
    T = x.shape[0]
    L = W["ffn1_norm"].shape[0]
    saved = []
    recv = {}
    for l in range(L):
        g1, gm, g2 = (W[n][l][None, :] for n in ("ffn1_norm", "mix_norm", "ffn2_norm"))
        first = shards is not None and l == 0
        hosted = (_GatherChips, [shards["w_in"][0], shards["conv_w"]]) if first else None
        x1, a1, b1, *got = _ffn_fwd(x, g1, W["ffn1_w_gate"][l], W["ffn1_w_up"][l], W["ffn1_w_down"][l], hosted=hosted)
        if first:
            W["w_in"][0] = _from_shards("w_in", got[0])
            W["conv_w"] = got[1].transpose(1, 2, 0, 3).reshape(L, CONV_K, 256)
        w_in = _pad_w_in(W["w_in"][l])
        up, q, k, v, ca, cg, zf = _mix_in_fwd(x1, gm, w_in)
        fb = jnp.pad(W["forget_bias"][l], (0, LANES - HEADS))[None, :]
        F = _fgate_fwd(zf, fb)
        if first:
            yb, lse, *got = _attn_fwd(q, k, v, F, hosted=(_GatherChipsSplit, [shards[n][ll] for n, ll in LATE]))
            for (n, ll), sh in zip(LATE, got):
                W[n][ll] = _from_shards(n, sh)
        else:
            yb, lse = _attn_fwd(q, k, v, F)
        bd = _block_diag(W["pool_w"][l]).astype(MM)
        ps, cb, lg, lb = (W[n][l][None, :] for n in ("pool_scale", "conv_b", "conv_ln_g", "conv_ln_b"))
        cw = jnp.pad(W["conv_w"][l], ((0, 1), (0, 0)))
        ya, yc, cu, cy = _local_fwd(up, ca, cg, bd, ps, cw, cb, lg, lb)
        x2 = _mix_out_fwd(x1, ya, yb, yc, W["w_out"][l])
        x3, a2, b2 = _ffn_fwd(x2, g2, W["ffn2_w_gate"][l], W["ffn2_w_up"][l], W["ffn2_w_down"][l])
        saved.append(dict(x0=x, x1=x1, x2=x2, ab1=(a1, b1), ab2=(a2, b2), w_in=w_in, up=up, ca=ca, cg=cg, zf=zf, fb=fb, F=F,
                          q=q, k=k, v=v, lse=lse, bd=bd, cw=cw, cu=cu, cy=cy, ya=ya, yb=yb, yc=yc))
        x = x3

    loss, dx, dgf = _head(x, W["final_norm"][None, :], target)
    grads = {n: [None] * L for n in W if n != "final_norm"}
    grads["final_norm"] = dgf[0]
    for l in reversed(range(L)):
        s = saved[l]
        g1, gm, g2 = (W[n][l][None, :] for n in ("ffn1_norm", "mix_norm", "ffn2_norm"))
        ps, lg, lb = (W[n][l][None, :] for n in ("pool_scale", "conv_ln_g", "conv_ln_b"))
        dx, h, dy, da, db, sact, dg = _ffn_bwd(s["x2"], dx, g2, *s["ab2"], W["ffn2_w_gate"][l], W["ffn2_w_up"][l],
                                               W["ffn2_w_down"][l])
        grads["ffn2_norm"][l] = dg[0]
        grads["ffn2_w_gate"][l] = _wgrad(h, da, "wgrad_gate")
        grads["ffn2_w_up"][l] = _wgrad(h, db, "wgrad_up")
        grads["ffn2_w_down"][l] = _wgrad(sact, dy, "wgrad_down")
        dya, dyb, dyc = _mix_out_bwd(dx, W["w_out"][l])
        grads["w_out"][l] = jnp.concatenate(
            [_wgrad(s["ya"], dx, "wgrad_out_a"), _wgrad(s["yb"], dx, "wgrad_out_b"), _wgrad(s["yc"], dx, "wgrad_out_c")], axis=0)
        first = shards is not None and l == 0
        if first:
            dq, dk, dv, dfq, dfk, *got = _attn_bwd(s["q"], s["k"], s["v"], s["F"], s["yb"], s["lse"], dyb,
                                                  hosted=(_ScatterChips, _grad_parts(grads, LATE)))
            recv.update(zip(LATE, got))
        else:
            dq, dk, dv, dfq, dfk = _attn_bwd(s["q"], s["k"], s["v"], s["F"], s["yb"], s["lse"], dyb)
        dfk_cols = jnp.pad(dfk.transpose(0, 2, 1, 3).reshape(HEADS, T).T, ((0, 0), (0, LANES - HEADS)))
        dzf, dfb = _fgate_bwd(s["zf"], s["fb"], dfq, dfk_cols)
        grads["forget_bias"][l] = dfb[0, :HEADS]
        dup, dca, dcg, dbd, dps, dcw, dcb, dlg, dlb = _local_bwd(
            s["up"], dya, s["ca"], s["cg"], s["cu"], s["cy"], dyc, s["bd"], ps, s["cw"], lg, lb)
        grads["pool_w"][l] = jnp.stack([dbd[64 * i:64 * i + 64, 64 * i:64 * i + 64] for i in range(4)])
        grads["pool_scale"][l], grads["conv_b"][l] = dps[0], dcb[0]
        grads["conv_ln_g"][l], grads["conv_ln_b"][l] = dlg[0], dlb[0]
        grads["conv_w"][l] = dcw[:CONV_K]
        dx, h, dp, dg = _mix_in_bwd(s["x1"], dx, gm, s["w_in"], dup, dq, dk, dv, dca, dcg, dzf)
        grads["mix_norm"][l] = dg[0]
        grads["w_in"][l] = _unpad_w_in(_wgrad(h, dp, "wgrad_in"))
        ffn1 = (W["ffn1_w_gate"][l], W["ffn1_w_up"][l], W["ffn1_w_down"][l])
        if not first:
            dx, h, dy, da, db, sact, dg = _ffn_bwd(s["x0"], dx, g1, *s["ab1"], *ffn1)
            grads["ffn1_w_gate"][l] = _wgrad(h, da, "wgrad_gate")
            grads["ffn1_w_up"][l] = _wgrad(h, db, "wgrad_up")
            grads["ffn1_w_down"][l] = _wgrad(sact, dy, "wgrad_down")
        else:
            scatter = lambda n: (_ScatterChips, _grad_parts(grads, [(n, 0)]))
            dx, h, dy, da, db, sact, dg, recv[("w_in", 0)] = _ffn_bwd(s["x0"], dx, g1, *s["ab1"], *ffn1,
                                                                      hosted=scatter("w_in"))
            grads["ffn1_w_gate"][0] = _wgrad(h, da, "wgrad_gate")
            grads["ffn1_w_up"][0], recv[("ffn1_w_gate", 0)] = _wgrad(h, db, "wgrad_up", hosted=scatter("ffn1_w_gate"))
            grads["ffn1_w_down"][0], recv[("ffn1_w_up", 0)] = _wgrad(sact, dy, "wgrad_down", hosted=scatter("ffn1_w_up"))
            recv[("ffn1_w_down", 0)] = _exchange(*scatter("ffn1_w_down"), "scatter_last_grad")[0]
        grads["ffn1_norm"][l] = dg[0]
    grads = {n: (jnp.stack(g) if isinstance(g, list) and n not in BIG else g) for n, g in grads.items()}
    return loss, dx, grads, recv


NAMES = ("ffn1_norm", "ffn1_w_gate", "ffn1_w_up", "ffn1_w_down", "mix_norm", "w_in", "pool_w", "pool_scale",
         "forget_bias", "conv_w", "conv_b", "conv_ln_g", "conv_ln_b", "w_out", "ffn2_norm", "ffn2_w_gate",
         "ffn2_w_up", "ffn2_w_down", "final_norm")


def kernel(x, ffn1_norm, ffn1_w_gate, ffn1_w_up, ffn1_w_down, mix_norm, w_in, pool_w, pool_scale, forget_bias, conv_w, conv_b, conv_ln_g, conv_ln_b, w_out, ffn2_norm, ffn2_w_gate, ffn2_w_up, ffn2_w_down, final_norm, loss_target, m_ffn1_norm, m_ffn1_w_gate, m_ffn1_w_up, m_ffn1_w_down, m_mix_norm, m_w_in, m_pool_w, m_pool_scale, m_forget_bias, m_conv_w, m_conv_b, m_conv_ln_g, m_conv_ln_b, m_w_out, m_ffn2_norm, m_ffn2_w_gate, m_ffn2_w_up, m_ffn2_w_down, m_final_norm, v_ffn1_norm, v_ffn1_w_gate, v_ffn1_w_up, v_ffn1_w_down, v_mix_norm, v_w_in, v_pool_w, v_pool_scale, v_forget_bias, v_conv_w, v_conv_b, v_conv_ln_g, v_conv_ln_b, v_w_out, v_ffn2_norm, v_ffn2_w_gate, v_ffn2_w_up, v_ffn2_w_down, v_final_norm):
    args = (ffn1_norm, ffn1_w_gate, ffn1_w_up, ffn1_w_down, mix_norm, w_in, pool_w, pool_scale, forget_bias, conv_w, conv_b, conv_ln_g, conv_ln_b, w_out, ffn2_norm, ffn2_w_gate, ffn2_w_up, ffn2_w_down, final_norm)
    ms = (m_ffn1_norm, m_ffn1_w_gate, m_ffn1_w_up, m_ffn1_w_down, m_mix_norm, m_w_in, m_pool_w, m_pool_scale, m_forget_bias, m_conv_w, m_conv_b, m_conv_ln_g, m_conv_ln_b, m_w_out, m_ffn2_norm, m_ffn2_w_gate, m_ffn2_w_up, m_ffn2_w_down, m_final_norm)
    vs = (v_ffn1_norm, v_ffn1_w_gate, v_ffn1_w_up, v_ffn1_w_down, v_mix_norm, v_w_in, v_pool_w, v_pool_scale, v_forget_bias, v_conv_w, v_conv_b, v_conv_ln_g, v_conv_ln_b, v_w_out, v_ffn2_norm, v_ffn2_w_gate, v_ffn2_w_up, v_ffn2_w_down, v_final_norm)
    P = dict(zip(NAMES, args))
    M = dict(zip(NAMES, ms))
    V = dict(zip(NAMES, vs))
    xi, yi, _ = _position()
    chip = 2 * xi + yi

    W = {n: P[n] for n in SMALL}
    W.update({n: [None] * P[n].shape[0] for n in BIG})
    shards = {n: [P[n][l].astype(MM) for l in range(P[n].shape[0])] for n in BIG}
    shards["conv_w"] = P["conv_w"]
    for (n, l), sh in zip(FIRST, _exchange(_GatherChipsSplit, [shards[n][l] for n, l in FIRST], "gather_first_weights")):
        W[n][l] = _from_shards(n, sh)

    loss_part, dx, G, recv = _forward_backward(x[0], loss_target[0], W, shards)
    loss = lax.psum(loss_part[0, 0], ("x", "y", "c"))

    res = {}
    gathered = _exchange(_GatherDevices, [G[n] for n in SMALL] + [G["conv_w"]], "gather_small_grads")
    for n, g8 in zip(SMALL, gathered):
        shp = P[n].shape
        two_d = (math.prod(shp[:-1]), shp[-1])
        outs = _adamw(P[n].reshape(two_d), g8.reshape((8,) + two_d), M[n].reshape(two_d), V[n].reshape(two_d),
                      "adamw_" + n)
        for kind, a in zip(("g", "d", "m", "v"), outs):
            res[(kind, n)] = a.reshape(shp)
    shp = P["conv_w"].shape
    g_cw_full = _sum8(gathered[-1].reshape(8, shp[0] * CONV_K, 256), "sum_conv_w_grads")
    g_cw = lax.dynamic_slice_in_dim(g_cw_full, chip * shp[2], shp[2], axis=1)
    two_d = (shp[0] * CONV_K, shp[2])
    outs = _adamw(P["conv_w"].reshape(two_d), [g_cw], M["conv_w"].reshape(two_d), V["conv_w"].reshape(two_d),
                  "adamw_conv_w")
    for kind, a in zip(("g", "d", "m", "v"), outs):
        res[(kind, "conv_w")] = a.reshape(shp)

    parts =[_sum_parts([_own_shard(n, G[n][l], chip) for l in range(P[n].shape[0])],
                        [recv[(n, l)] for l in range(P[n].shape[0])], "sum_" + n) for n in BIG]
    others = _exchange(_SwapCores, parts, "swap_core_grads")
    for n, ga, gb in zip(BIG, parts, others):
        shp = P[n].shape
        two_d = (shp[0] * shp[1], shp[2])
        outs = _adamw(P[n].reshape(two_d), [ga.reshape(two_d), gb.reshape(two_d)], M[n].reshape(two_d),
                      V[n].reshape(two_d), "adamw_" + n)
        for kind, a in zip(("g", "d", "m", "v"), outs):
            res[(kind, n)] = a.reshape(shp)

    return (loss, dx[None], *[res[("g", n)] for n in NAMES], *[res[("d", n)] for n in NAMES],
            *[res[("m", n)] for n in NAMES], *[res[("v", n)] for n in NAMES])
```

```python
import math

import jax
import jax.numpy as jnp
from jax import lax
from jax.experimental import pallas as pl
from jax.experimental.pallas import tpu as pltpu

F32 = jnp.float32
MM = jnp.bfloat16
NORM_EPS = 1e-6
HEADS = 8
HEAD_DIM = 64
POOL_WINDOWS = (2, 4, 8, 16)
CONV_K = 31
LANES = 128
VMEM_LIMIT = 56 * 2**20
FFN_BWD_ROWS = 256
FFN_COLS = 768
ATTN_FWD_TILE = 1024
WGRAD_COLS = 768
WGRAD_ACC_BYTES = 12 * 2**20

ADAM_LR = 0.001
ADAM_B1 = 0.9
ADAM_B2 = 0.999
ADAM_EPS = 1e-08
ADAM_WD = 0.01
ADAM_STEP = 10

MESH = pl.DeviceIdType.MESH
BS = pl.BlockSpec
SDS = jax.ShapeDtypeStruct
ANY = pl.BlockSpec(memory_space=pl.ANY)


def _dot(a, b):
    return jnp.dot(a, b, preferred_element_type=F32)


def _dot_nt(a, b):
    return lax.dot_general(a, b, (((1,), (1,)), ((), ())), preferred_element_type=F32)


def _dot_tn(a, b):
    return lax.dot_general(a, b, (((0,), (0,)), ((), ())), preferred_element_type=F32)


def _pc(body, name, grid, in_specs, out_specs, out_shape, scratch=()):
    return pl.pallas_call(
        body, out_shape=out_shape, grid=grid, in_specs=in_specs, out_specs=out_specs,
        scratch_shapes=list(scratch), name=name,
        compiler_params=pltpu.CompilerParams(
            dimension_semantics=("arbitrary",) * len(grid), vmem_limit_bytes=VMEM_LIMIT))


def _rms_fwd(x, g):
    r = lax.rsqrt(jnp.mean(x * x, axis=-1, keepdims=True) + NORM_EPS)
    xh = x * r
    return xh, r, xh * g


def _rms_bwd(dh, xh, r, g):
    dxh = dh * g
    dx = r * (dxh - xh * jnp.mean(dxh * xh, axis=-1, keepdims=True))
    return dx, jnp.sum(dh * xh, axis=0, keepdims=True)


def _sigmoid(x):
    return jax.nn.sigmoid(x)


def _ffn_fwd(x, g, wg, wu, wd, hosted=None):
    T, D = x.shape
    F = wg.shape[1]
    tm = min(512, T)
    nt = T // tm
    pieces = [(c0, min(FFN_COLS, F - c0)) for c0 in range(0, F, FFN_COLS)]
    h_in, h_out, h_shape, h_scratch = _hosted_specs(hosted)

    def body(*refs):
        i = pl.program_id(0)
        refs, finish = _hosted_edges(hosted, refs, 5, 3, i == 0, i == nt - 1)
        x_ref, g_ref, wg_ref, wu_ref, wd_ref, o_ref, a_ref, b_ref = refs
        xv = x_ref[...]
        h = _rms_fwd(xv, g_ref[...])[2].astype(MM)
        acc = jnp.zeros((tm, D), F32)
        for c0, w in pieces:
            a = _dot(h, wg_ref[:, c0:c0 + w])
            b = _dot(h, wu_ref[:, c0:c0 + w])
            a_ref[:, c0:c0 + w] = a.astype(a_ref.dtype)
            b_ref[:, c0:c0 + w] = b.astype(b_ref.dtype)
            acc = acc + _dot(((a * _sigmoid(a)) * b).astype(MM), wd_ref[c0:c0 + w, :])
        o_ref[...] = xv + 0.5 * acc
        finish()

    tok = lambda i: (i, 0)
    par = lambda i: (0, 0)
    resident = lambda shape: BS(shape, par, pipeline_mode=pl.Buffered(1))
    return _pc(
        body, "ffn_fwd" + ("_hosting" if hosted else ""), (nt,),
        [BS((tm, D), tok), BS((1, D), par), resident((D, F)), resident((D, F)), resident((F, D))] + h_in,
        [BS((tm, D), tok), BS((tm, F), tok), BS((tm, F), tok)] + h_out,
        [SDS((T, D), F32), SDS((T, F), MM), SDS((T, F), MM)] + h_shape,
        scratch=h_scratch)(x, g, wg, wu, wd, *(hosted[1] if hosted else []))


def _ffn_bwd(x, dout, g, a, b, wg, wu, wd, hosted=None):
    T, D = x.shape
    F = wg.shape[1]
    tm = min(FFN_BWD_ROWS, T)
    nt = T // tm
    pieces = [(c0, min(FFN_COLS, F - c0)) for c0 in range(0, F, FFN_COLS)]
    h_in, h_out, h_shape, h_scratch = _hosted_specs(hosted)

    def body(*refs):
        i = pl.program_id(0)
        refs, finish = _hosted_edges(hosted, refs, 8, 7, i == 0, i == nt - 1)
        (x_ref, do_ref, g_ref, a_ref, b_ref, wg_ref, wu_ref, wd_ref,
         dx_ref, h_ref, dy_ref, da_ref, db_ref, s_ref, dg_ref) = refs

        @pl.when(i == 0)
        def _():
            dg_ref[...] = jnp.zeros_like(dg_ref)

        gv = g_ref[...]
        xh, r, hg = _rms_fwd(x_ref[...], gv)
        h_ref[...] = hg.astype(h_ref.dtype)
        dy = (0.5 * do_ref[...]).astype(MM)
        dy_ref[...] = dy
        dh = jnp.zeros((tm, D), F32)
        for c0, w in pieces:
            a = a_ref[:, c0:c0 + w].astype(F32)
            b = b_ref[:, c0:c0 + w].astype(F32)
            ds = _dot_nt(dy, wd_ref[c0:c0 + w, :])
            sig = _sigmoid(a)
            sl = a * sig
            s_ref[:, c0:c0 + w] = (sl * b).astype(s_ref.dtype)
            db = (ds * sl).astype(MM)
            da = (ds * b * (sig * (1.0 + a * (1.0 - sig)))).astype(MM)
            da_ref[:, c0:c0 + w] = da
            db_ref[:, c0:c0 + w] = db
            dh = dh + _dot_nt(da, wg_ref[:, c0:c0 + w]) + _dot_nt(db, wu_ref[:, c0:c0 + w])
        dx, dg = _rms_bwd(dh, xh, r, gv)
        dx_ref[...] = do_ref[...] + dx
        dg_ref[...] += dg
        finish()

    tok = lambda i: (i, 0)
    par = lambda i: (0, 0)
    hid = BS((tm, F), tok)
    resident = lambda shape: BS(shape, par, pipeline_mode=pl.Buffered(1))
    return _pc(
        body, "ffn_bwd" + ("_hosting" if hosted else ""), (nt,),
        [BS((tm, D), tok), BS((tm, D), tok), BS((1, D), par), hid, hid,
         resident((D, F)), resident((D, F)), resident((F, D))] + h_in,
        [BS((tm, D), tok), BS((tm, D), tok), BS((tm, D), tok), hid, hid, hid, BS((1, D), par)] + h_out,
        [SDS((T, D), F32), SDS((T, D), MM), SDS((T, D), MM),
         SDS((T, F), MM), SDS((T, F), MM), SDS((T, F), MM), SDS((1, D), F32)] + h_shape,
        scratch=h_scratch,
    )(x, dout, g, a, b, wg, wu, wd, *(hosted[1] if hosted else []))


def _wgrad(a, b, name, hosted=None):
    T, K = a.shape
    N = b.shape[1]
    tt = min(512, T)
    tn = next(c for c in (N, 1408, 1280, 1024, 512, 256, 128) if N % c == 0 and K * c * 4 <= WGRAD_ACC_BYTES)
    pieces = [(c0, min(WGRAD_COLS, tn - c0)) for c0 in range(0, tn, WGRAD_COLS)]
    nn, nt = N // tn, T // tt
    h_in, h_out, h_shape, h_scratch = _hosted_specs(hosted)

    def body(*refs):
        n, t = pl.program_id(0), pl.program_id(1)
        refs, finish = _hosted_edges(hosted, refs, 2, 1, (n == 0) & (t == 0), (n == nn - 1) & (t == nt - 1))
        a_ref, b_ref, o_ref = refs

        @pl.when(t == 0)
        def _():
            o_ref[...] = jnp.zeros_like(o_ref)

        av = a_ref[...].astype(MM)
        for c0, w in pieces:
            o_ref[:, c0:c0 + w] += _dot_tn(av, b_ref[:, c0:c0 + w].astype(MM))
        finish()

    res = _pc(
        body, name + ("_hosting" if hosted else ""), (nn, nt),
        [BS((tt, K), lambda n, t: (t, 0)), BS((tt, tn), lambda n, t: (t, n))] + h_in,
        [BS((K, tn), lambda n, t: (0, n))] + h_out, [SDS((K, N), F32)] + h_shape,
        scratch=h_scratch)(a, b, *(hosted[1] if hosted else []))
    return res if hosted else res[0]


C_POOL, C_Q, C_K, C_V, C_CA, C_CG, C_ZF, C_END = 0, 256, 768, 1280, 1792, 2048, 2304, 2560


def _mix_in_fwd(x, g, w):
    T, D = x.shape
    tm = min(512, T)

    def body(x_ref, g_ref, w_ref, up_ref, q_ref, k_ref, v_ref, ca_ref, cg_ref, zf_ref):
        _, _, hg = _rms_fwd(x_ref[...], g_ref[...])
        p = _dot(hg.astype(MM), w_ref[...])
        up_ref[...] = p[:, C_POOL:C_Q]
        q_ref[...] = p[:, C_Q:C_K].astype(q_ref.dtype)
        k_ref[...] = p[:, C_K:C_V].astype(k_ref.dtype)
        v_ref[...] = p[:, C_V:C_CA].astype(v_ref.dtype)
        ca_ref[...] = p[:, C_CA:C_CG]
        cg_ref[...] = p[:, C_CG:C_ZF]
        zf_ref[...] = p[:, C_ZF:C_ZF + LANES]

    tok = lambda i: (i, 0)
    widths = (256, 512, 512, 512, 256, 256, 128)
    dtypes = (F32, MM, MM, MM, F32, F32, F32)
    return _pc(
        body, "mix_in_fwd", (T // tm,),
        [BS((tm, D), tok), BS((1, D), lambda i: (0, 0)), BS((D, C_END), lambda i: (0, 0))],
        [BS((tm, wd), tok) for wd in widths],
        [SDS((T, wd), dt) for wd, dt in zip(widths, dtypes)])(x, g, w)


def _mix_in_bwd(x, dout, g, w, dup, dq, dk, dv, dca, dcg, dzf):
    T, D = x.shape
    tm = min(512, T)

    def body(x_ref, do_ref, g_ref, w_ref, dup_ref, dq_ref, dk_ref, dv_ref, dca_ref, dcg_ref, dzf_ref,
             dx_ref, h_ref, dp_ref, dg_ref):
        @pl.when(pl.program_id(0) == 0)
        def _():
            dg_ref[...] = jnp.zeros_like(dg_ref)

        gv = g_ref[...]
        xh, r, hg = _rms_fwd(x_ref[...], gv)
        h_ref[...] = hg.astype(h_ref.dtype)
        for ref, lo, hi in ((dup_ref, C_POOL, C_Q), (dq_ref, C_Q, C_K), (dk_ref, C_K, C_V), (dv_ref, C_V, C_CA),
                            (dca_ref, C_CA, C_CG), (dcg_ref, C_CG, C_ZF), (dzf_ref, C_ZF, C_ZF + LANES)):
            dp_ref[:, lo:hi] = ref[...].astype(dp_ref.dtype)
        dp_ref[:, C_ZF + LANES:C_END] = jnp.zeros((tm, C_END - C_ZF - LANES), dp_ref.dtype)
        dh = _dot_nt(dp_ref[...], w_ref[...])
        dx, dg = _rms_bwd(dh, xh, r, gv)
        dx_ref[...] = do_ref[...] + dx
        dg_ref[...] += dg

    tok = lambda i: (i, 0)
    widths = (256, 512, 512, 512, 256, 256, 128)
    return _pc(
        body, "mix_in_bwd", (T // tm,),
        [BS((tm, D), tok), BS((tm, D), tok), BS((1, D), lambda i: (0, 0)), BS((D, C_END), lambda i: (0, 0))]
        + [BS((tm, wd), tok) for wd in widths],
        [BS((tm, D), tok), BS((tm, D), tok), BS((tm, C_END), tok), BS((1, D), lambda i: (0, 0))],
        [SDS((T, D), F32), SDS((T, D), MM), SDS((T, C_END), MM), SDS((1, D), F32)],
    )(x, dout, g, w, dup, dq, dk, dv, dca, dcg, dzf)


def _mix_out_fwd(x, ya, yb, yc, wo):
    T, D = x.shape
    tm = min(512, T)

    def body(x_ref, ya_ref, yb_ref, yc_ref, wo_ref, o_ref):
        o_ref[...] = (x_ref[...] + _dot(ya_ref[...].astype(MM), wo_ref[0:256, :])
                      + _dot(yb_ref[...].astype(MM), wo_ref[256:768, :])
                      + _dot(yc_ref[...].astype(MM), wo_ref[768:1024, :]))

    tok = lambda i: (i, 0)
    return _pc(
        body, "mix_out_fwd", (T // tm,),
        [BS((tm, D), tok), BS((tm, 256), tok), BS((tm, 512), tok), BS((tm, 256), tok), BS((D, D), lambda i: (0, 0))],
        BS((tm, D), tok), SDS((T, D), F32))(x, ya, yb, yc, wo)


def _mix_out_bwd(dx, wo):
    T, D = dx.shape
    tm = min(512, T)

    def body(dx_ref, wo_ref, dya_ref, dyb_ref, dyc_ref):
        dy = _dot_nt(dx_ref[...].astype(MM), wo_ref[...])
        dya_ref[...] = dy[:, 0:256]
        dyb_ref[...] = dy[:, 256:768]
        dyc_ref[...] = dy[:, 768:1024]

    tok = lambda i: (i, 0)
    return _pc(
        body, "mix_out_bwd", (T // tm,),
        [BS((tm, D), tok), BS((D, D), lambda i: (0, 0))],
        [BS((tm, 256), tok), BS((tm, 512), tok), BS((tm, 256), tok)],
        [SDS((T, 256), F32), SDS((T, 512), F32), SDS((T, 256), F32)])(dx, wo)


def _fgate_fwd(zf, bias):
    T = zf.shape[0]
    tc = min(256, T)

    def body(z_ref, b_ref, f_ref, carry):
        @pl.when(pl.program_id(0) == 0)
        def _():
            carry[...] = jnp.zeros_like(carry)

        z = z_ref[...] + b_ref[...]
        logf = jnp.minimum(z, 0.0) - jnp.log(1.0 + jnp.exp(-jnp.abs(z)))
        row = lax.broadcasted_iota(jnp.int32, (tc, tc), 0)
        col = lax.broadcasted_iota(jnp.int32, (tc, tc), 1)
        tri = (col <= row).astype(F32)
        f_ref[...] = jnp.dot(tri, logf, precision=lax.Precision.HIGHEST, preferred_element_type=F32) + carry[...]
        carry[...] += jnp.sum(logf, axis=0, keepdims=True)

    return _pc(
        body, "fgate_fwd", (T // tc,),
        [BS((tc, LANES), lambda i: (i, 0)), BS((1, LANES), lambda i: (0, 0))],
        BS((tc, LANES), lambda i: (i, 0)), SDS((T, LANES), F32),
        scratch=[pltpu.VMEM((1, LANES), F32)])(zf, bias)


def _fgate_bwd(zf, bias, dFq, dFk):
    T = zf.shape[0]
    tc = min(256, T)
    n = T // tc
    slabs = dFq.shape[0]

    def body(z_ref, b_ref, dfq_ref, dfk_ref, dz_ref, db_ref, carry):
        @pl.when(pl.program_id(0) == 0)
        def _():
            carry[...] = jnp.zeros_like(carry)
            db_ref[...] = jnp.zeros_like(db_ref)

        df = dfk_ref[...]
        for sl in range(slabs):
            df = df + dfq_ref[sl]
        row = lax.broadcasted_iota(jnp.int32, (tc, tc), 0)
        col = lax.broadcasted_iota(jnp.int32, (tc, tc), 1)
        tri = (col >= row).astype(F32)
        dlogf = jnp.dot(tri, df, precision=lax.Precision.HIGHEST, preferred_element_type=F32) + carry[...]
        carry[...] += jnp.sum(df, axis=0, keepdims=True)
        lane = lax.broadcasted_iota(jnp.int32, (1, LANES), 1)
        dz = jnp.where(lane < HEADS, dlogf * _sigmoid(-(z_ref[...] + b_ref[...])), 0.0)
        dz_ref[...] = dz
        db_ref[...] += jnp.sum(dz, axis=0, keepdims=True)

    rev = lambda i: (n - 1 - i, 0)
    return _pc(
        body, "fgate_bwd", (n,),
        [BS((tc, LANES), rev), BS((1, LANES), lambda i: (0, 0)), BS((slabs, tc, LANES), lambda i: (0, n - 1 - i, 0)),
         BS((tc, LANES), rev)],
        [BS((tc, LANES), rev), BS((1, LANES), lambda i: (0, 0))],
        [SDS((T, LANES), F32), SDS((1, LANES), F32)],
        scratch=[pltpu.VMEM((1, LANES), F32)])(zf, bias, dFq, dFk)


LOG2E = 1.4426950408889634


def _split3(x):
    hi = x.astype(MM)
    r1 = x - hi.astype(F32)
    mid = r1.astype(MM)
    return hi, mid, (r1 - mid.astype(F32)).astype(MM)


def _place(lane, base, cols):
    out = jnp.zeros((cols[0].shape[0], LANES), MM)
    for i, c in enumerate(cols):
        out = jnp.where(lane == base + i, c, out)
    return out


def _head_col(block, lane, h):
    return jnp.sum(jnp.where(lane == h, block, 0.0), axis=-1, keepdims=True)


def _own_lanes(lane, hh):
    return (lane < HEAD_DIM) if hh == 0 else (lane >= HEAD_DIM)


def _attn_k_side(k_ref, f_ref, kb_ref, hp, T, rows, lse_ones, v_ref=None, vb_ref=None):
    lane = lax.broadcasted_iota(jnp.int32, (1, LANES), 1)
    one = jnp.ones((rows, 1), MM)

    def chunk(c, _):
        r0 = pl.multiple_of(c * rows, rows)
        kp = k_ref[pl.ds(r0, rows), :]
        fblk = f_ref[pl.ds(r0, rows), :]
        for hh in range(2):
            hi, mid, lo = _split3(-_head_col(fblk, lane, 2 * hp + hh) * LOG2E)
            cols = [one, one, one, hi, mid, lo] + ([one, one, one] if lse_ones else [])
            bias = _place(lane, HEAD_DIM * (1 - hh), cols)
            kb_ref[hh, pl.ds(r0, rows), :] = jnp.where(_own_lanes(lane, hh), kp, bias)
            if vb_ref is not None:
                vb_ref[hh, pl.ds(r0, rows), :] = jnp.where(_own_lanes(lane, hh), v_ref[pl.ds(r0, rows), :],
                                                           jnp.ones((rows, LANES), MM))
        return 0

    lax.fori_loop(0, T // rows, chunk, 0)


def _attn_q_side(qp, fblk, lane, hp, scale, lse_blk=None):
    qc = qp.astype(F32) * (scale * LOG2E)
    qhi = qc.astype(MM)
    qlo = (qc - qhi.astype(F32)).astype(MM)
    one = jnp.ones((qp.shape[0], 1), MM)
    out = []
    for hh in range(2):
        cols = list(_split3(_head_col(fblk, lane, 2 * hp + hh) * LOG2E)) + [one, one, one]
        if lse_blk is not None:
            cols += list(_split3(-_head_col(lse_blk, lane, 2 * hp + hh)))
        bias = _place(lane, HEAD_DIM * (1 - hh), cols)
        own = _own_lanes(lane, hh)
        out.append(jnp.concatenate([jnp.where(own, qhi, jnp.zeros_like(qhi)), jnp.where(own, qlo, bias)], axis=1))
    return out


def _causal(tq, tk):
    return lax.broadcasted_iota(jnp.int32, (tq, tk), 1) <= lax.broadcasted_iota(jnp.int32, (tq, tk), 0)


def _hosted_specs(hosted):
    if hosted is None:
        return [], [], [], []
    kind, arrays = hosted
    n = len(arrays)
    return [ANY] * n, [ANY] * n, [kind.out_shape(a) for a in arrays], kind.scratch(n)


def _hosted_edges(hosted, refs, n_in, n_out, first, last, mid=None):
    if hosted is None:
        return refs, lambda: None
    kind, arrays = hosted
    n = len(arrays)
    nsem = len(kind.scratch(n))
    o0 = n_in + n + n_out
    ins, outs, sems = refs[n_in:n_in + n], refs[o0:o0 + n], refs[len(refs) - nsem:]
    relayed = mid is not None and hasattr(kind, "relay")

    @pl.when(first)
    def _():
        kind.start(ins, outs, *sems)

    if relayed:
        @pl.when(mid)
        def _():
            kind.relay(ins, outs, *sems)

    def finish():
        @pl.when(last)
        def _():
            kind.wait(ins, outs, *sems, **({"relayed": True} if relayed else {}))

    return refs[:n_in] + refs[n_in + n:o0] + refs[o0 + n:len(refs) - nsem], finish


def _attn_fwd(q, k, v, F, hosted=None):
    T = q.shape[0]
    tq = min(ATTN_FWD_TILE, T)
    tk = tq
    nq = T // tq
    scale = 1.0 / math.sqrt(HEAD_DIM)
    h_in, h_out, h_shape, h_scratch = _hosted_specs(hosted)

    def body(*refs):
        hp, ib = pl.program_id(0), pl.program_id(1)
        refs, finish = _hosted_edges(hosted, refs, 5, 2, (hp == 0) & (ib == 0), (hp == HEADS // 2 - 1) & (ib == nq - 1),
                                     mid=(hp == HEADS // 2 - 1) & (ib == 0))
        q_ref, k_ref, v_ref, fq_ref, f_ref, o_ref, lse_ref, kb_ref, vb_ref = refs
        lane = lax.broadcasted_iota(jnp.int32, (1, LANES), 1)

        @pl.when(ib == 0)
        def _():
            _attn_k_side(k_ref, f_ref, kb_ref, hp, T, min(512, T), False, v_ref, vb_ref)

        qa = _attn_q_side(q_ref[...], fq_ref[...], lane, hp, scale)

        def tile(jb, carry, masked):
            off = pl.multiple_of(jb * tk, tk)
            kp = k_ref[pl.ds(off, tk), :]
            new = []
            for hh in range(2):
                m, acc = carry[hh]
                s = _dot_nt(qa[hh], jnp.concatenate([kp, kb_ref[hh, pl.ds(off, tk), :]], axis=1))
                if masked:
                    s = jnp.where(_causal(tq, tk), s, -jnp.inf)
                m2 = jnp.maximum(m, jnp.max(s, axis=-1, keepdims=True))
                p = jnp.exp2(s - m2)
                new.append((m2, acc * jnp.exp2(m - m2) + _dot(p.astype(MM), vb_ref[hh, pl.ds(off, tk), :])))
            return tuple(new)

        init = tuple((jnp.full((tq, 1), -jnp.inf, F32), jnp.zeros((tq, LANES), F32)) for _ in range(2))
        carry = lax.fori_loop(0, ib, lambda jb, c: tile(jb, c, False), init)
        (m0, a0), (m1, a1) = tile(ib, carry, True)
        l0, l1 = a0[:, HEAD_DIM:HEAD_DIM + 1], a1[:, 0:1]
        o_ref[...] = jnp.where(lane < HEAD_DIM, a0 / l0, a1 / l1)
        lse_ref[...] = jnp.where(lane == 2 * hp, m0 + jnp.log2(l0), jnp.where(lane == 2 * hp + 1, m1 + jnp.log2(l1), 0.0))
        finish()

    blk = lambda h, i: (i, h)
    full = lambda h, i: (0, h)
    return _pc(
        body, "attn_fwd" + ("_hosting" if hosted else ""), (HEADS // 2, nq),
        [BS((tq, LANES), blk), BS((T, LANES), full), BS((T, LANES), full), BS((tq, LANES), lambda h, i: (i, 0)),
         BS((T, LANES), lambda h, i: (0, 0))] + h_in,
        [BS((tq, LANES), blk), BS((None, tq, LANES), lambda h, i: (h, i, 0))] + h_out,
        [SDS((T, HEADS * HEAD_DIM), F32), SDS((HEADS // 2, T, LANES), F32)] + h_shape,
        scratch=[pltpu.VMEM((2, T, LANES), MM)] * 2 + h_scratch)(q, k, v, F, F, *(hosted[1] if hosted else []))


def _attn_bwd(q, k, v, F, o, lse, do, hosted=None):
    T = q.shape[0]
    tq = min(512, T)
    tk = tq
    nq = T // tq
    scale = 1.0 / math.sqrt(HEAD_DIM)
    h_in, h_out, h_shape, h_scratch = _hosted_specs(hosted)

    def body(*refs):
        hp, ib = pl.program_id(0), pl.program_id(1)
        refs, finish = _hosted_edges(hosted, refs, 8, 5, (hp == 0) & (ib == 0), (hp == HEADS // 2 - 1) & (ib == nq - 1))
        (q_ref, k_ref, v_ref, fq_ref, f_ref, o_ref, lse_ref, do_ref,
         dq_ref, dk_ref, dv_ref, dfq_ref, dfk_ref, kb_ref, dk_acc, dv_acc) = refs
        lane = lax.broadcasted_iota(jnp.int32, (1, LANES), 1)

        @pl.when(ib == 0)
        def _():
            _attn_k_side(k_ref, f_ref, kb_ref, hp, T, tk, True)
            dk_acc[...] = jnp.zeros_like(dk_acc)
            dv_acc[...] = jnp.zeros_like(dv_acc)
            dfk_ref[...] = jnp.zeros_like(dfk_ref)

        qp = q_ref[...]
        qa = _attn_q_side(qp, fq_ref[...], lane, hp, scale, lse_ref[...])
        dob = do_ref[...].astype(MM)
        dprod = dob.astype(F32) * o_ref[...]
        qs = (qp.astype(F32) * scale).astype(MM)
        heads = []
        for hh in range(2):
            own = _own_lanes(lane, hh)
            heads.append((jnp.where(own, dob, jnp.zeros_like(dob)), jnp.where(own, qs, jnp.zeros_like(qs)),
                          jnp.sum(jnp.where(own, dprod, 0.0), axis=-1, keepdims=True)))

        def tile(jb, carry, masked):
            off = pl.multiple_of(jb * tk, tk)
            kp = k_ref[pl.ds(off, tk), :]
            vp = v_ref[pl.ds(off, tk), :]
            new = []
            dv_t = jnp.zeros((tk, LANES), F32)
            dk_t = jnp.zeros((tk, LANES), F32)
            for hh in range(2):
                dq, rs = carry[hh]
                dom, qm, delta = heads[hh]
                p = jnp.exp2(_dot_nt(qa[hh], jnp.concatenate([kp, kb_ref[hh, pl.ds(off, tk), :]], axis=1)))
                if masked:
                    p = jnp.where(_causal(tq, tk), p, 0.0)
                ds = p * (_dot_nt(dom, vp) - delta)
                dsb = ds.astype(MM)
                dv_t = dv_t + _dot_tn(p.astype(MM), dom)
                dk_t = dk_t + _dot_tn(dsb, qm)
                dfk_ref[jb, pl.ds(hh, 1), :] -= jnp.sum(ds, axis=0, keepdims=True)
                new.append((dq + _dot(dsb, kp), rs + jnp.sum(ds, axis=-1, keepdims=True)))
            dv_acc[pl.ds(off, tk), :] += dv_t
            dk_acc[pl.ds(off, tk), :] += dk_t
            return tuple(new)

        init = tuple((jnp.zeros((tq, LANES), F32), jnp.zeros((tq, 1), F32)) for _ in range(2))
        carry = lax.fori_loop(0, ib, lambda jb, c: tile(jb, c, False), init)
        (dq0, rs0), (dq1, rs1) = tile(ib, carry, True)
        dq_ref[...] = (jnp.where(lane < HEAD_DIM, dq0, dq1) * scale).astype(dq_ref.dtype)
        dfq_ref[...] = jnp.where(lane == 2 * hp, rs0, jnp.where(lane == 2 * hp + 1, rs1, 0.0))

        @pl.when(ib == nq - 1)
        def _():
            dk_ref[...] = dk_acc[...].astype(dk_ref.dtype)
            dv_ref[...] = dv_acc[...].astype(dv_ref.dtype)

        finish()

    blk = lambda h, i: (i, h)
    full = lambda h, i: (0, h)
    slab = BS((None, tq, LANES), lambda h, i: (h, i, 0))
    return _pc(
        body, "attn_bwd" + ("_hosting" if hosted else ""), (HEADS // 2, nq),
        [BS((tq, LANES), blk), BS((T, LANES), full), BS((T, LANES), full), BS((tq, LANES), lambda h, i: (i, 0)),
         BS((T, LANES), lambda h, i: (0, 0)), BS((tq, LANES), blk), slab, BS((tq, LANES), blk)] + h_in,
        [BS((tq, LANES), blk), BS((T, LANES), full), BS((T, LANES), full), slab,
         BS((None, nq, 2, tk), lambda h, i: (h, 0, 0, 0))] + h_out,
        [SDS((T, HEADS * HEAD_DIM), MM)] * 3 + [SDS((HEADS // 2, T, LANES), F32), SDS((HEADS // 2, nq, 2, tk), F32)]
        + h_shape,
        scratch=[pltpu.VMEM((2, T, LANES), MM), pltpu.VMEM((T, LANES), F32), pltpu.VMEM((T, LANES), F32)] + h_scratch,
    )(q, k, v, F, F, o, lse, do, *(hosted[1] if hosted else []))


POOL_HALO = 16
CONV_HALO = 32


def _group_select(lane, v0, v1, v2, v3):
    return jnp.where(lane < 64, v0, jnp.where(lane < 128, v1, jnp.where(lane < 192, v2, v3)))


def _roll_down(x, k):
    return x if k == 0 else pltpu.roll(x, k, 0)


def _roll_up(x, k):
    return x if k == 0 else pltpu.roll(x, x.shape[0] - k, 0)


def _pool_terms(u, u_prev, tile, tm):
    ext = jnp.concatenate([u_prev, u], axis=0)
    s2 = ext + _roll_down(ext, 1)
    s4 = s2 + _roll_down(s2, 2)
    s8 = s4 + _roll_down(s4, 4)
    s16 = s8 + _roll_down(s8, 8)
    lane = lax.broadcasted_iota(jnp.int32, (1, 256), 1)
    ws = _group_select(lane, s2, s4, s8, s16)[POOL_HALO:, :]
    wlen = _group_select(lane, *map(float, POOL_WINDOWS)).astype(F32)
    return ws / _pool_count(tile, tm, tm, wlen) - u


def _pool_count(tile, tm, rows, wlen):
    t = (tile * tm + 1 + lax.broadcasted_iota(jnp.int32, (rows, 1), 0)).astype(F32)
    return jnp.minimum(t, wlen)


def _layer_norm(y, lg, lb):
    mu = jnp.mean(y, axis=-1, keepdims=True)
    yc = y - mu
    rstd = lax.rsqrt(jnp.mean(yc * yc, axis=-1, keepdims=True) + NORM_EPS)
    yh = yc * rstd
    return yh, rstd, yh * lg + lb


def _halo_specs(tm, T, halo, prev):
    per = tm // halo
    if prev:
        return BS((halo, 256), lambda i: (jnp.maximum(i * per - 1, 0), 0))
    return BS((halo, 256), lambda i: (jnp.minimum((i + 1) * per, T // halo - 1), 0))


def _local_fwd(up, ca, cg, bd, pscale, cw, cb, lg, lb):
    T = up.shape[0]
    tm = min(512, T)

    def body(up_ref, uph_ref, ca_ref, cah_ref, cg_ref, cgh_ref, bd_ref, ps_ref, cw_ref, cb_ref, lg_ref, lb_ref,
             ya_ref, yc_ref, u_ref, y_ref):
        i = pl.program_id(0)
        first = i == 0
        pooled = _pool_terms(up_ref[...], jnp.where(first, 0.0, uph_ref[...]), i, tm)
        ya_ref[...] = (_dot(pooled.astype(MM), bd_ref[...]) * ps_ref[...]).astype(ya_ref.dtype)

        u = ca_ref[...] * _sigmoid(cg_ref[...])
        uh = jnp.where(first, 0.0, cah_ref[...] * _sigmoid(cgh_ref[...]))
        ext = jnp.concatenate([uh, u], axis=0)
        y = jnp.zeros((tm, 256), F32) + cb_ref[...]
        for kk in range(CONV_K):
            y = y + cw_ref[kk:kk + 1, :] * _roll_up(ext, CONV_HALO - (CONV_K - 1) + kk)[:tm, :]
        _, _, z = _layer_norm(y, lg_ref[...], lb_ref[...])
        yc_ref[...] = (z * _sigmoid(z)).astype(yc_ref.dtype)
        u_ref[...] = u
        y_ref[...] = y

    tok = lambda i: (i, 0)
    par = lambda i: (0, 0)
    t256 = BS((tm, 256), tok)
    return _pc(
        body, "local_fwd", (T // tm,),
        [t256, _halo_specs(tm, T, POOL_HALO, True), t256, _halo_specs(tm, T, CONV_HALO, True),
         t256, _halo_specs(tm, T, CONV_HALO, True),
         BS((256, 256), par), BS((1, 256), par), BS((32, 256), par), BS((1, 256), par), BS((1, 256), par),
         BS((1, 256), par)],
        [t256, t256, t256, t256],
        [SDS((T, 256), MM), SDS((T, 256), MM), SDS((T, 256), F32), SDS((T, 256), F32)],
    )(up, up, ca, ca, cg, cg, bd, pscale, cw, cb, lg, lb)


def _local_bwd(up, dya, ca, cg, u, y, dyc, bd, pscale, cw, lg, lb):
    T = up.shape[0]
    tm = min(512, T)
    n = T // tm

    def body(up_ref, uph_ref, dya_ref, dyan_ref, ca_ref, cg_ref, u_ref, y_ref, yn_ref, dyc_ref, dycn_ref,
             bd_ref, ps_ref, cw_ref, lg_ref, lb_ref,
             dup_ref, dca_ref, dcg_ref, dbd_ref, dps_ref, dcw_ref, dcb_ref, dlg_ref, dlb_ref):
        i = pl.program_id(0)
        first = i == 0
        last = i == n - 1

        @pl.when(first)
        def _():
            for ref in (dbd_ref, dps_ref, dcw_ref, dcb_ref, dlg_ref, dlb_ref):
                ref[...] = jnp.zeros_like(ref)

        ps = ps_ref[...]
        pooled = _pool_terms(up_ref[...], jnp.where(first, 0.0, uph_ref[...]), i, tm).astype(MM)
        dya_t = dya_ref[...]
        dps_ref[...] += jnp.sum(dya_t * _dot(pooled, bd_ref[...]), axis=0, keepdims=True)
        dm = (jnp.concatenate([dya_t, jnp.where(last, 0.0, dyan_ref[...])], axis=0) * ps).astype(MM)
        dbd_ref[...] += _dot_tn(pooled, dm[:tm, :])
        dpool = _dot_nt(dm, bd_ref[...])
        lane = lax.broadcasted_iota(jnp.int32, (1, 256), 1)
        wlen = _group_select(lane, *map(float, POOL_WINDOWS)).astype(F32)
        e = dpool / _pool_count(i, tm, tm + POOL_HALO, wlen)
        f2 = e + _roll_up(e, 1)
        f4 = f2 + _roll_up(f2, 2)
        f8 = f4 + _roll_up(f4, 4)
        f16 = f8 + _roll_up(f8, 8)
        dup_ref[...] = _group_select(lane, f2, f4, f8, f16)[:tm, :] - dpool[:tm, :]

        lgv = lg_ref[...]
        yext = jnp.concatenate([y_ref[...], yn_ref[...]], axis=0)
        dyc = jnp.concatenate([dyc_ref[...], jnp.where(last, 0.0, dycn_ref[...])], axis=0)
        yh, rstd, z = _layer_norm(yext, lgv, lb_ref[...])
        sig = _sigmoid(z)
        dz = dyc * (sig * (1.0 + z * (1.0 - sig)))
        dlg_ref[...] += jnp.sum((dz * yh)[:tm, :], axis=0, keepdims=True)
        dlb_ref[...] += jnp.sum(dz[:tm, :], axis=0, keepdims=True)
        dyh = dz * lgv
        dy = rstd * (dyh - jnp.mean(dyh, axis=-1, keepdims=True) - yh * jnp.mean(dyh * yh, axis=-1, keepdims=True))
        dcb_ref[...] += jnp.sum(dy[:tm, :], axis=0, keepdims=True)
        uv = u_ref[...]
        du = jnp.zeros((tm, 256), F32)
        for kk in range(CONV_K):
            ahead = _roll_up(dy, CONV_K - 1 - kk)[:tm, :]
            dcw_ref[kk:kk + 1, :] += jnp.sum(uv * ahead, axis=0, keepdims=True)
            du = du + cw_ref[kk:kk + 1, :] * ahead
        sg = _sigmoid(cg_ref[...])
        dca_ref[...] = du * sg
        dcg_ref[...] = du * ca_ref[...] * sg * (1.0 - sg)

    tok = lambda i: (i, 0)
    par = lambda i: (0, 0)
    t256 = BS((tm, 256), tok)
    p1 = BS((1, 256), par)
    return _pc(
        body, "local_bwd", (n,),
        [t256, _halo_specs(tm, T, POOL_HALO, True), t256, _halo_specs(tm, T, POOL_HALO, False), t256, t256,
         t256, t256, _halo_specs(tm, T, CONV_HALO, False), t256, _halo_specs(tm, T, CONV_HALO, False),
         BS((256, 256), par), p1, BS((32, 256), par), p1, p1],
        [t256, t256, t256, BS((256, 256), par), p1, BS((32, 256), par), p1, p1, p1],
        [SDS((T, 256), F32)] * 3 + [SDS((256, 256), F32), SDS((1, 256), F32), SDS((32, 256), F32)]
        + [SDS((1, 256), F32)] * 3,
    )(up, up, dya, dya, ca, cg, u, y, y, dyc, dyc, bd, pscale, cw, lg, lb)


def _head(x, g, target):
    T, D = x.shape
    tm = min(512, T)

    def body(x_ref, g_ref, t_ref, loss_ref, dx_ref, dg_ref):
        @pl.when(pl.program_id(0) == 0)
        def _():
            loss_ref[...] = jnp.zeros_like(loss_ref)
            dg_ref[...] = jnp.zeros_like(dg_ref)

        gv = g_ref[...]
        xh, r, yv = _rms_fwd(x_ref[...], gv)
        err = yv - t_ref[...]
        loss_ref[...] += 0.5 * jnp.sum(jnp.mean(err * err, axis=-1, keepdims=True), axis=0, keepdims=True)
        dx, dg = _rms_bwd(err * (1.0 / D), xh, r, gv)
        dx_ref[...] = dx
        dg_ref[...] += dg

    tok = lambda i: (i, 0)
    par = lambda i: (0, 0)
    return _pc(
        body, "head", (T // tm,),
        [BS((tm, D), tok), BS((1, D), par), BS((tm, D), tok)],
        [BS((1, LANES), par), BS((tm, D), tok), BS((1, D), par)],
        [SDS((1, LANES), F32), SDS((T, D), F32), SDS((1, D), F32)])(x, g, target)


def _adamw(w, gs, m, v, name):
    L, R, C = w.shape
    tr = R
    for cand in (512, 256, 128, 64, 32, 16, 8):
        if R % cand == 0:
            tr = cand
            break
    stacked = not isinstance(gs, (list, tuple))
    ng = 1 if stacked else len(gs)

    def body(*refs):
        w_ref, g_refs, m_ref, v_ref = refs[0], refs[1:1 + ng], refs[1 + ng], refs[2 + ng]
        g_ref, d_ref, m2_ref, v2_ref = refs[3 + ng:]
        terms = [g_refs[0][d] for d in range(gs.shape[0])] if stacked else [r[...] for r in g_refs]
        g = terms[0]
        for term in terms[1:]:
            g = g + term
        m2 = ADAM_B1 * m_ref[...] + (1.0 - ADAM_B1) * g
        v2 = ADAM_B2 * v_ref[...] + (1.0 - ADAM_B2) * jnp.square(g)
        m_hat = m2 / (1.0 - ADAM_B1 ** ADAM_STEP)
        v_hat = v2 / (1.0 - ADAM_B2 ** ADAM_STEP)
        g_ref[...] = g
        d_ref[...] = -ADAM_LR * (m_hat / (jnp.sqrt(v_hat) + ADAM_EPS) + ADAM_WD * w_ref[...])
        m2_ref[...] = m2
        v2_ref[...] = v2

    blk = BS((None, tr, C), lambda l, i: (l, i, 0))
    g_specs = [BS((gs.shape[0], None, tr, C), lambda l, i: (0, l, i, 0))] if stacked else [blk] * ng
    return _pc(body, name, (L, R // tr), [blk] + g_specs + [blk, blk], [blk] * 4,
               [SDS((L, R, C), F32)] * 4)(w, *([gs] if stacked else gs), m, v)


def _sum_parts(owns, recvs, name):
    L = len(owns)
    R, C = owns[0].shape
    tr = next(t for t in (512, 256, 128, 64, 32, 16) if R % t == 0)

    def body(*refs):
        l = pl.program_id(0)
        s_ref = refs[2 * L]
        for ll in range(L):
            @pl.when(l == ll)
            def _(o_ref=refs[ll], r_ref=refs[L + ll]):
                s_ref[...] = ((o_ref[...] + r_ref[0].astype(F32)) + r_ref[1].astype(F32)) + r_ref[2].astype(F32)

    own_specs = [BS((tr, C), lambda l, i, ll=ll: (jnp.where(l == ll, i, 0), 0)) for ll in range(L)]
    recv_specs = [BS((3, tr, C), lambda l, i, ll=ll: (0, jnp.where(l == ll, i, 0), 0)) for ll in range(L)]
    return _pc(body, name, (L, R // tr), own_specs + recv_specs,
               BS((None, tr, C), lambda l, i: (l, i, 0)), SDS((L, R, C), F32))(*owns, *recvs)


def _sum8(parts, name):
    _, R, C = parts.shape

    def body(p_ref, s_ref):
        acc = p_ref[0]
        for d in range(1, 8):
            acc = acc + p_ref[d]
        s_ref[...] = acc

    return _pc(body, name, (1,), [BS((8, R, C), lambda i: (0, 0, 0))], BS((R, C), lambda i: (0, 0)),
               SDS((R, C), F32))(parts)


def _position():
    return lax.axis_index("x"), lax.axis_index("y"), lax.axis_index("c")


CHIP_FLIPS = ((1, 0), (0, 1), (1, 1))


class _GatherChips:
    @staticmethod
    def scratch(n):
        return [pltpu.SemaphoreType.DMA((3 * n,)), pltpu.SemaphoreType.DMA((3 * n,)), pltpu.SemaphoreType.DMA((n,))]

    @staticmethod
    def out_shape(block):
        return SDS((4,) + tuple(block.shape), block.dtype)

    @staticmethod
    def _copies(ins, outs, send_sems, recv_sems, local_sems, arrivals):
        x, y, c = _position()
        local, remote = [], []
        for i, (in_ref, out_ref) in enumerate(zip(ins, outs)):
            local.append(pltpu.make_async_copy(in_ref, out_ref.at[2 * x + y], local_sems.at[i]))
            for k, (fx, fy) in enumerate(CHIP_FLIPS):
                slot = 2 * (x ^ fx) + (y ^ fy) if arrivals else 2 * x + y
                remote.append(pltpu.make_async_remote_copy(
                    src_ref=in_ref, dst_ref=out_ref.at[slot], send_sem=send_sems.at[3 * i + k],
                    recv_sem=recv_sems.at[3 * i + k], device_id=(x ^ fx, y ^ fy, c), device_id_type=MESH))
        return local, remote

    @classmethod
    def start(cls, ins, outs, *sems):
        local, sends = cls._copies(ins, outs, *sems, arrivals=False)
        for cp in local + sends:
            cp.start()

    @classmethod
    def wait(cls, ins, outs, *sems):
        local, arrivals = cls._copies(ins, outs, *sems, arrivals=True)
        for cp in arrivals:
            cp.wait_recv()
        for cp in arrivals:
            cp.wait_send()
        for cp in local:
            cp.wait()


class _GatherChipsSplit(_GatherChips):
    @staticmethod
    def scratch(n):
        return [pltpu.SemaphoreType.DMA((6 * n,)), pltpu.SemaphoreType.DMA((6 * n,)), pltpu.SemaphoreType.DMA((n,))]

    @staticmethod
    def _half(ref, which):
        rows = ref.shape[0] // 2
        return ref.at[pl.ds(pl.multiple_of(which * rows, 16), rows)]

    @staticmethod
    def _local(ins, outs, local_sems):
        x, y, _ = _position()
        return [pltpu.make_async_copy(in_ref, out_ref.at[2 * x + y], local_sems.at[i])
                for i, (in_ref, out_ref) in enumerate(zip(ins, outs))]

    @classmethod
    def _between_chips(cls, ins, outs, send_sems, recv_sems, arrivals):
        x, y, c = _position()
        return [
            pltpu.make_async_remote_copy(
                src_ref=cls._half(in_ref, c),
                dst_ref=cls._half(out_ref.at[2 * (x ^ fx) + (y ^ fy) if arrivals else 2 * x + y], c),
                send_sem=send_sems.at[6 * i + k], recv_sem=recv_sems.at[6 * i + k],
                device_id=(x ^ fx, y ^ fy, c), device_id_type=MESH)
            for i, (in_ref, out_ref) in enumerate(zip(ins, outs)) for k, (fx, fy) in enumerate(CHIP_FLIPS)]

    @classmethod
    def _between_cores(cls, outs, send_sems, recv_sems, arrivals):
        x, y, c = _position()
        copies = []
        for i, out_ref in enumerate(outs):
            for k, (fx, fy) in enumerate(CHIP_FLIPS):
                half = cls._half(out_ref.at[2 * (x ^ fx) + (y ^ fy)], 1 - c if arrivals else c)
                copies.append(pltpu.make_async_remote_copy(
                    src_ref=half, dst_ref=half, send_sem=send_sems.at[6 * i + 3 + k],
                    recv_sem=recv_sems.at[6 * i + 3 + k], device_id=(x, y, 1 - c), device_id_type=MESH))
        return copies

    @classmethod
    def start(cls, ins, outs, send_sems, recv_sems, local_sems):
        for cp in cls._local(ins, outs, local_sems) + cls._between_chips(ins, outs, send_sems, recv_sems, False):
            cp.start()

    @classmethod
    def relay(cls, ins, outs, send_sems, recv_sems, local_sems):
        arrivals = cls._between_chips(ins, outs, send_sems, recv_sems, True)
        onward = cls._between_cores(outs, send_sems, recv_sems, False)
        for cp, nxt in zip(arrivals, onward):
            cp.wait_recv()
            nxt.start()

    @classmethod
    def wait(cls, ins, outs, send_sems, recv_sems, local_sems, relayed=False):
        if not relayed:
            cls.relay(ins, outs, send_sems, recv_sems, local_sems)
        for cp in cls._between_cores(outs, send_sems, recv_sems, True):
            cp.wait_recv()
        for cp in (cls._between_chips(ins, outs, send_sems, recv_sems, True)
                   + cls._between_cores(outs, send_sems, recv_sems, False)):
            cp.wait_send()
        for cp in cls._local(ins, outs, local_sems):
            cp.wait()


class _Symmetric:
    @classmethod
    def start(cls, ins, outs, *sems):
        for cp in cls._copies(ins, outs, *sems):
            cp.start()

    @classmethod
    def wait(cls, ins, outs, *sems):
        copies = cls._copies(ins, outs, *sems)
        for cp in copies:
            cp.wait_recv()
        for cp in copies:
            cp.wait_send()


class _ScatterChips(_Symmetric):
    @staticmethod
    def scratch(n):
        return [pltpu.SemaphoreType.DMA((3 * n,)), pltpu.SemaphoreType.DMA((3 * n,))]

    @staticmethod
    def out_shape(parts):
        return SDS((3,) + tuple(parts.shape[1:]), parts.dtype)

    @staticmethod
    def _copies(ins, outs, send_sems, recv_sems):
        x, y, c = _position()
        return [
            pltpu.make_async_remote_copy(
                src_ref=in_ref.at[2 * (x ^ fx) + (y ^ fy)], dst_ref=out_ref.at[k],
                send_sem=send_sems.at[3 * i + k], recv_sem=recv_sems.at[3 * i + k],
                device_id=(x ^ fx, y ^ fy, c), device_id_type=MESH)
            for i, (in_ref, out_ref) in enumerate(zip(ins, outs)) for k, (fx, fy) in enumerate(CHIP_FLIPS)]


class _SwapCores(_Symmetric):
    @staticmethod
    def scratch(n):
        return [pltpu.SemaphoreType.DMA((n,)), pltpu.SemaphoreType.DMA((n,))]

    @staticmethod
    def out_shape(block):
        return SDS(block.shape, block.dtype)

    @staticmethod
    def _copies(ins, outs, send_sems, recv_sems):
        x, y, c = _position()
        return [
            pltpu.make_async_remote_copy(
                src_ref=in_ref, dst_ref=out_ref, send_sem=send_sems.at[i], recv_sem=recv_sems.at[i],
                device_id=(x, y, 1 - c), device_id_type=MESH)
            for i, (in_ref, out_ref) in enumerate(zip(ins, outs))]


def _exchange(kind, arrays, name):
    n = len(arrays)

    def body(*refs):
        ins, outs, sems = refs[:n], refs[n:2 * n], refs[2 * n:]
        kind.start(ins, outs, *sems)
        kind.wait(ins, outs, *sems)

    return pl.pallas_call(body, out_shape=[kind.out_shape(a) for a in arrays], in_specs=[ANY] * n,
                          out_specs=[ANY] * n, name=name, scratch_shapes=kind.scratch(n))(*arrays)


DEVICE_FLIPS = tuple((fx, fy, fc) for fx in (0, 1) for fy in (0, 1) for fc in (0, 1))[1:]


class _GatherDevices(_GatherChips):
    @staticmethod
    def scratch(n):
        return [pltpu.SemaphoreType.DMA((7 * n,)), pltpu.SemaphoreType.DMA((7 * n,)), pltpu.SemaphoreType.DMA((n,))]

    @staticmethod
    def out_shape(block):
        return SDS((8,) + tuple(block.shape), block.dtype)

    @staticmethod
    def _copies(ins, outs, send_sems, recv_sems, local_sems, arrivals):
        x, y, c = _position()
        local, remote = [], []
        for i, (in_ref, out_ref) in enumerate(zip(ins, outs)):
            local.append(pltpu.make_async_copy(in_ref, out_ref.at[4 * x + 2 * y + c], local_sems.at[i]))
            for k, (fx, fy, fc) in enumerate(DEVICE_FLIPS):
                slot = 4 * (x ^ fx) + 2 * (y ^ fy) + (c ^ fc) if arrivals else 4 * x + 2 * y + c
                remote.append(pltpu.make_async_remote_copy(
                    src_ref=in_ref, dst_ref=out_ref.at[slot], send_sem=send_sems.at[7 * i + k],
                    recv_sem=recv_sems.at[7 * i + k], device_id=(x ^ fx, y ^ fy, c ^ fc), device_id_type=MESH))
        return local, remote


BIG = ("ffn1_w_gate", "ffn1_w_up", "ffn1_w_down", "w_in", "w_out", "ffn2_w_gate", "ffn2_w_up", "ffn2_w_down")
COL_SHARDED = ("ffn1_w_gate", "ffn1_w_up", "w_in", "ffn2_w_gate", "ffn2_w_up")
FIRST = tuple((n, 0) for n in ("ffn1_w_gate", "ffn1_w_up", "ffn1_w_down"))
LATE = tuple((n, 0) for n in ("w_out", "ffn2_w_gate", "ffn2_w_up", "ffn2_w_down")) + tuple((n, 1) for n in BIG)


def _to_shards(name, full):
    r, c = full.shape
    if name in COL_SHARDED:
        return full.reshape(r, 4, c // 4).transpose(1, 0, 2)
    return full.reshape(4, r // 4, c)


def _own_shard(name, full, chip):
    r, c = full.shape
    if name in COL_SHARDED:
        return lax.dynamic_slice_in_dim(full, chip * (c // 4), c // 4, axis=1)
    return lax.dynamic_slice_in_dim(full, chip * (r // 4), r // 4, axis=0)


def _from_shards(name, sh):
    _, r, c = sh.shape
    if name in COL_SHARDED:
        return sh.transpose(1, 0, 2).reshape(r, 4 * c)
    return sh.reshape(4 * r, c)


def _pad_w_in(w):
    return jnp.concatenate([w[:, :1792], w[:, 1800:2312], w[:, 1792:1800], jnp.zeros((w.shape[0], 248), w.dtype)], axis=1)


def _unpad_w_in(g):
    return jnp.concatenate([g[:, :1792], g[:, 2304:2312], g[:, 1792:2304]], axis=1)


def _block_diag(pw):
    out = jnp.zeros((256, 256), pw.dtype)
    for gidx in range(4):
        out = lax.dynamic_update_slice(out, pw[gidx], (64 * gidx, 64 * gidx))
    return out


SMALL = ("ffn1_norm", "mix_norm", "pool_w", "pool_scale", "forget_bias", "conv_b", "conv_ln_g", "conv_ln_b",
         "ffn2_norm", "final_norm")


def _grad_parts(grads, pieces):
    return [_to_shards(n, grads[n][l]).astype(MM) for n, l in pieces]


def _forward_backward(x, target, W, shards=None):
    T = x.shape[0]
    L = W["ffn1_norm"].shape[0]
    saved = []
    recv = {}
    for l in range(L):
        g1, gm, g2 = (W[n][l][None, :] for n in ("ffn1_norm", "mix_norm", "ffn2_norm"))
        first = shards is not None and l == 0
        hosted = (_GatherChips, [shards["w_in"][0], shards["conv_w"]]) if first else None
        x1, a1, b1, *got = _ffn_fwd(x, g1, W["ffn1_w_gate"][l], W["ffn1_w_up"][l], W["ffn1_w_down"][l], hosted=hosted)
        if first:
            W["w_in"][0] = _from_shards("w_in", got[0])
            W["conv_w"] = got[1].transpose(1, 2, 0, 3).reshape(L, CONV_K, 256)
        w_in = _pad_w_in(W["w_in"][l])
        up, q, k, v, ca, cg, zf = _mix_in_fwd(x1, gm, w_in)
        fb = jnp.pad(W["forget_bias"][l], (0, LANES - HEADS))[None, :]
        F = _fgate_fwd(zf, fb)
        if first:
            yb, lse, *got = _attn_fwd(q, k, v, F, hosted=(_GatherChipsSplit, [shards[n][ll] for n, ll in LATE]))
            for (n, ll), sh in zip(LATE, got):
                W[n][ll] = _from_shards(n, sh)
        else:
            yb, lse = _attn_fwd(q, k, v, F)
        bd = _block_diag(W["pool_w"][l]).astype(MM)
        ps, cb, lg, lb = (W[n][l][None, :] for n in ("pool_scale", "conv_b", "conv_ln_g", "conv_ln_b"))
        cw = jnp.pad(W["conv_w"][l], ((0, 1), (0, 0)))
        ya, yc, cu, cy = _local_fwd(up, ca, cg, bd, ps, cw, cb, lg, lb)
        x2 = _mix_out_fwd(x1, ya, yb, yc, W["w_out"][l])
        x3, a2, b2 = _ffn_fwd(x2, g2, W["ffn2_w_gate"][l], W["ffn2_w_up"][l], W["ffn2_w_down"][l])
        saved.append(dict(x0=x, x1=x1, x2=x2, ab1=(a1, b1), ab2=(a2, b2), w_in=w_in, up=up, ca=ca, cg=cg, zf=zf, fb=fb, F=F,
                          q=q, k=k, v=v, lse=lse, bd=bd, cw=cw, cu=cu, cy=cy, ya=ya, yb=yb, yc=yc))
        x = x3

    loss, dx, dgf = _head(x, W["final_norm"][None, :], target)
    grads = {n: [None] * L for n in W if n != "final_norm"}
    grads["final_norm"] = dgf[0]
    for l in reversed(range(L)):
        s = saved[l]
        g1, gm, g2 = (W[n][l][None, :] for n in ("ffn1_norm", "mix_norm", "ffn2_norm"))
        ps, lg, lb = (W[n][l][None, :] for n in ("pool_scale", "conv_ln_g", "conv_ln_b"))
        dx, h, dy, da, db, sact, dg = _ffn_bwd(s["x2"], dx, g2, *s["ab2"], W["ffn2_w_gate"][l], W["ffn2_w_up"][l],
                                               W["ffn2_w_down"][l])
        grads["ffn2_norm"][l] = dg[0]
        grads["ffn2_w_gate"][l] = _wgrad(h, da, "wgrad_gate")
        grads["ffn2_w_up"][l] = _wgrad(h, db, "wgrad_up")
        grads["ffn2_w_down"][l] = _wgrad(sact, dy, "wgrad_down")
        dya, dyb, dyc = _mix_out_bwd(dx, W["w_out"][l])
        grads["w_out"][l] = jnp.concatenate(
            [_wgrad(s["ya"], dx, "wgrad_out_a"), _wgrad(s["yb"], dx, "wgrad_out_b"), _wgrad(s["yc"], dx, "wgrad_out_c")], axis=0)
        first = shards is not None and l == 0
        if first:
            dq, dk, dv, dfq, dfk, *got = _attn_bwd(s["q"], s["k"], s["v"], s["F"], s["yb"], s["lse"], dyb,
                                                  hosted=(_ScatterChips, _grad_parts(grads, LATE)))
            recv.update(zip(LATE, got))
        else:
            dq, dk, dv, dfq, dfk = _attn_bwd(s["q"], s["k"], s["v"], s["F"], s["yb"], s["lse"], dyb)
        dfk_cols = jnp.pad(dfk.transpose(0, 2, 1, 3).reshape(HEADS, T).T, ((0, 0), (0, LANES - HEADS)))
        dzf, dfb = _fgate_bwd(s["zf"], s["fb"], dfq, dfk_cols)
        grads["forget_bias"][l] = dfb[0, :HEADS]
        dup, dca, dcg, dbd, dps, dcw, dcb, dlg, dlb = _local_bwd(
            s["up"], dya, s["ca"], s["cg"], s["cu"], s["cy"], dyc, s["bd"], ps, s["cw"], lg, lb)
        grads["pool_w"][l] = jnp.stack([dbd[64 * i:64 * i + 64, 64 * i:64 * i + 64] for i in range(4)])
        grads["pool_scale"][l], grads["conv_b"][l] = dps[0], dcb[0]
        grads["conv_ln_g"][l], grads["conv_ln_b"][l] = dlg[0], dlb[0]
        grads["conv_w"][l] = dcw[:CONV_K]
        dx, h, dp, dg = _mix_in_bwd(s["x1"], dx, gm, s["w_in"], dup, dq, dk, dv, dca, dcg, dzf)
        grads["mix_norm"][l] = dg[0]
        grads["w_in"][l] = _unpad_w_in(_wgrad(h, dp, "wgrad_in"))
        ffn1 = (W["ffn1_w_gate"][l], W["ffn1_w_up"][l], W["ffn1_w_down"][l])
        dx, h, dy, da, db, sact, dg = _ffn_bwd(s["x0"], dx, g1, *s["ab1"], *ffn1)
        if not first:
            grads["ffn1_w_gate"][l] = _wgrad(h, da, "wgrad_gate")
            grads["ffn1_w_up"][l] = _wgrad(h, db, "wgrad_up")
            grads["ffn1_w_down"][l] = _wgrad(sact, dy, "wgrad_down")
        else:
            scatter = lambda n: (_ScatterChips, _grad_parts(grads, [(n, 0)]))
            grads["ffn1_w_gate"][0], recv[("w_in", 0)] = _wgrad(h, da, "wgrad_gate", hosted=scatter("w_in"))
            grads["ffn1_w_up"][0], recv[("ffn1_w_gate", 0)] = _wgrad(h, db, "wgrad_up", hosted=scatter("ffn1_w_gate"))
            grads["ffn1_w_down"][0], recv[("ffn1_w_up", 0)] = _wgrad(sact, dy, "wgrad_down", hosted=scatter("ffn1_w_up"))
            recv[("ffn1_w_down", 0)] = _exchange(*scatter("ffn1_w_down"), "scatter_last_grad")[0]
        grads["ffn1_norm"][l] = dg[0]
    grads = {n: (jnp.stack(g) if isinstance(g, list) and n not in BIG else g) for n, g in grads.items()}
    return loss, dx, grads, recv


NAMES = ("ffn1_norm", "ffn1_w_gate", "ffn1_w_up", "ffn1_w_down", "mix_norm", "w_in", "pool_w", "pool_scale",
         "forget_bias", "conv_w", "conv_b", "conv_ln_g", "conv_ln_b", "w_out", "ffn2_norm", "ffn2_w_gate",
         "ffn2_w_up", "ffn2_w_down", "final_norm")


def kernel(x, ffn1_norm, ffn1_w_gate, ffn1_w_up, ffn1_w_down, mix_norm, w_in, pool_w, pool_scale, forget_bias, conv_w, conv_b, conv_ln_g, conv_ln_b, w_out, ffn2_norm, ffn2_w_gate, ffn2_w_up, ffn2_w_down, final_norm, loss_target, m_ffn1_norm, m_ffn1_w_gate, m_ffn1_w_up, m_ffn1_w_down, m_mix_norm, m_w_in, m_pool_w, m_pool_scale, m_forget_bias, m_conv_w, m_conv_b, m_conv_ln_g, m_conv_ln_b, m_w_out, m_ffn2_norm, m_ffn2_w_gate, m_ffn2_w_up, m_ffn2_w_down, m_final_norm, v_ffn1_norm, v_ffn1_w_gate, v_ffn1_w_up, v_ffn1_w_down, v_mix_norm, v_w_in, v_pool_w, v_pool_scale, v_forget_bias, v_conv_w, v_conv_b, v_conv_ln_g, v_conv_ln_b, v_w_out, v_ffn2_norm, v_ffn2_w_gate, v_ffn2_w_up, v_ffn2_w_down, v_final_norm):
    args = (ffn1_norm, ffn1_w_gate, ffn1_w_up, ffn1_w_down, mix_norm, w_in, pool_w, pool_scale, forget_bias, conv_w, conv_b, conv_ln_g, conv_ln_b, w_out, ffn2_norm, ffn2_w_gate, ffn2_w_up, ffn2_w_down, final_norm)
    ms = (m_ffn1_norm, m_ffn1_w_gate, m_ffn1_w_up, m_ffn1_w_down, m_mix_norm, m_w_in, m_pool_w, m_pool_scale, m_forget_bias, m_conv_w, m_conv_b, m_conv_ln_g, m_conv_ln_b, m_w_out, m_ffn2_norm, m_ffn2_w_gate, m_ffn2_w_up, m_ffn2_w_down, m_final_norm)
    vs = (v_ffn1_norm, v_ffn1_w_gate, v_ffn1_w_up, v_ffn1_w_down, v_mix_norm, v_w_in, v_pool_w, v_pool_scale, v_forget_bias, v_conv_w, v_conv_b, v_conv_ln_g, v_conv_ln_b, v_w_out, v_ffn2_norm, v_ffn2_w_gate, v_ffn2_w_up, v_ffn2_w_down, v_final_norm)
    P = dict(zip(NAMES, args))
    M = dict(zip(NAMES, ms))
    V = dict(zip(NAMES, vs))
    xi, yi, _ = _position()
    chip = 2 * xi + yi

    W = {n: P[n] for n in SMALL}
    W.update({n: [None] * P[n].shape[0] for n in BIG})
    shards = {n: [P[n][l].astype(MM) for l in range(P[n].shape[0])] for n in BIG}
    shards["conv_w"] = P["conv_w"]
    for (n, l), sh in zip(FIRST, _exchange(_GatherChipsSplit, [shards[n][l] for n, l in FIRST], "gather_first_weights")):
        W[n][l] = _from_shards(n, sh)

    loss_part, dx, G, recv = _forward_backward(x[0], loss_target[0], W, shards)
    loss = lax.psum(loss_part[0, 0], ("x", "y", "c"))

    res = {}
    gathered = _exchange(_GatherDevices, [G[n] for n in SMALL] + [G["conv_w"]], "gather_small_grads")
    for n, g8 in zip(SMALL, gathered):
        shp = P[n].shape
        three_d = (1, 1, shp[0]) if len(shp) == 1 else (shp[0], math.prod(shp[1:-1]), shp[-1])
        outs = _adamw(P[n].reshape(three_d), g8.reshape((8,) + three_d), M[n].reshape(three_d),
                      V[n].reshape(three_d), "adamw_" + n)
        for kind, a in zip(("g", "d", "m", "v"), outs):
            res[(kind, n)] = a.reshape(shp)
    shp = P["conv_w"].shape
    g_cw_full = _sum8(gathered[-1].reshape(8, shp[0] * CONV_K, 256), "sum_conv_w_grads").reshape(shp[0], CONV_K, 256)
    g_cw = lax.dynamic_slice_in_dim(g_cw_full, chip * shp[2], shp[2], axis=2)
    outs = _adamw(P["conv_w"], [g_cw], M["conv_w"], V["conv_w"], "adamw_conv_w")
    res.update(zip([(kind, "conv_w") for kind in ("g", "d", "m", "v")], outs))

    parts = [_sum_parts([_own_shard(n, G[n][l], chip) for l in range(P[n].shape[0])],
                        [recv[(n, l)] for l in range(P[n].shape[0])], "sum_" + n) for n in BIG]
    others = _exchange(_SwapCores, parts, "swap_core_grads")
    for n, ga, gb in zip(BIG, parts, others):
        outs = _adamw(P[n], [ga, gb], M[n], V[n], "adamw_" + n)
        res.update(zip([(kind, n) for kind in ("g", "d", "m", "v")], outs))

    return (loss, dx[None], *[res[("g", n)] for n in NAMES], *[res[("d", n)] for n in NAMES],
            *[res[("m", n)] for n in NAMES], *[res[("v", n)] for n in NAMES])
```

```python
import math

import jax
import jax.numpy as jnp
from jax import lax
from jax.experimental import pallas as pl
from jax.experimental.pallas import tpu as pltpu

F32 = jnp.float32
MM = jnp.bfloat16
NORM_EPS = 1e-6
HEADS = 8
HEAD_DIM = 64
POOL_WINDOWS = (2, 4, 8, 16)
CONV_K = 31
LANES = 128
VMEM_LIMIT = 56 * 2**20
FFN_BWD_ROWS = 256
FFN_COLS = 768
ATTN_FWD_TILE = 1024
WGRAD_COLS = 768
WGRAD_ACC_BYTES = 12 * 2**20

ADAM_LR = 0.001
ADAM_B1 = 0.9
ADAM_B2 = 0.999
ADAM_EPS = 1e-08
ADAM_WD = 0.01
ADAM_STEP = 10

MESH = pl.DeviceIdType.MESH
BS = pl.BlockSpec
SDS = jax.ShapeDtypeStruct
ANY = pl.BlockSpec(memory_space=pl.ANY)


def _dot(a, b):
    return jnp.dot(a, b, preferred_element_type=F32)


def _dot_nt(a, b):
    return lax.dot_general(a, b, (((1,), (1,)), ((), ())), preferred_element_type=F32)


def _dot_tn(a, b):
    return lax.dot_general(a, b, (((0,), (0,)), ((), ())), preferred_element_type=F32)


def _pc(body, name, grid, in_specs, out_specs, out_shape, scratch=()):
    return pl.pallas_call(
        body, out_shape=out_shape, grid=grid, in_specs=in_specs, out_specs=out_specs,
        scratch_shapes=list(scratch), name=name,
        compiler_params=pltpu.CompilerParams(
            dimension_semantics=("arbitrary",) * len(grid), vmem_limit_bytes=VMEM_LIMIT))


def _rms_fwd(x, g):
    r = lax.rsqrt(jnp.mean(x * x, axis=-1, keepdims=True) + NORM_EPS)
    xh = x * r
    return xh, r, xh * g


def _rms_bwd(dh, xh, r, g):
    dxh = dh * g
    dx = r * (dxh - xh * jnp.mean(dxh * xh, axis=-1, keepdims=True))
    return dx, jnp.sum(dh * xh, axis=0, keepdims=True)


def _sigmoid(x):
    return jax.nn.sigmoid(x)


def _ffn_fwd(x, g, wg, wu, wd, hosted=None):
    T, D = x.shape
    F = wg.shape[1]
    tm = min(512, T)
    nt = T // tm
    pieces = [(c0, min(FFN_COLS, F - c0)) for c0 in range(0, F, FFN_COLS)]
    h_in, h_out, h_shape, h_scratch = _hosted_specs(hosted)

    def body(*refs):
        i = pl.program_id(0)
        refs, finish = _hosted_edges(hosted, refs, 5, 3, i == 0, i == nt - 1)
        x_ref, g_ref, wg_ref, wu_ref, wd_ref, o_ref, a_ref, b_ref = refs
        xv = x_ref[...]
        h = _rms_fwd(xv, g_ref[...])[2].astype(MM)
        acc = jnp.zeros((tm, D), F32)
        for c0, w in pieces:
            a = _dot(h, wg_ref[:, c0:c0 + w])
            b = _dot(h, wu_ref[:, c0:c0 + w])
            a_ref[:, c0:c0 + w] = a.astype(a_ref.dtype)
            b_ref[:, c0:c0 + w] = b.astype(b_ref.dtype)
            acc = acc + _dot(((a * _sigmoid(a)) * b).astype(MM), wd_ref[c0:c0 + w, :])
        o_ref[...] = xv + 0.5 * acc
        finish()

    tok = lambda i: (i, 0)
    par = lambda i: (0, 0)
    resident = lambda shape: BS(shape, par, pipeline_mode=pl.Buffered(1))
    return _pc(
        body, "ffn_fwd" + ("_hosting" if hosted else ""), (nt,),
        [BS((tm, D), tok), BS((1, D), par), resident((D, F)), resident((D, F)), resident((F, D))] + h_in,
        [BS((tm, D), tok), BS((tm, F), tok), BS((tm, F), tok)] + h_out,
        [SDS((T, D), F32), SDS((T, F), MM), SDS((T, F), MM)] + h_shape,
        scratch=h_scratch)(x, g, wg, wu, wd, *(hosted[1] if hosted else []))


def _ffn_bwd(x, dout, g, a, b, wg, wu, wd, hosted=None):
    T, D = x.shape
    F = wg.shape[1]
    tm = min(FFN_BWD_ROWS, T)
    nt = T // tm
    pieces = [(c0, min(FFN_COLS, F - c0)) for c0 in range(0, F, FFN_COLS)]
    h_in, h_out, h_shape, h_scratch = _hosted_specs(hosted)

    def body(*refs):
        i = pl.program_id(0)
        refs, finish = _hosted_edges(hosted, refs, 8, 7, i == 0, i == nt - 1)
        (x_ref, do_ref, g_ref, a_ref, b_ref, wg_ref, wu_ref, wd_ref,
         dx_ref, h_ref, dy_ref, da_ref, db_ref, s_ref, dg_ref) = refs

        @pl.when(i == 0)
        def _():
            dg_ref[...] = jnp.zeros_like(dg_ref)

        gv = g_ref[...]
        xh, r, hg = _rms_fwd(x_ref[...], gv)
        h_ref[...] = hg.astype(h_ref.dtype)
        dy = (0.5 * do_ref[...]).astype(MM)
        dy_ref[...] = dy
        dh = jnp.zeros((tm, D), F32)
        for c0, w in pieces:
            a = a_ref[:, c0:c0 + w].astype(F32)
            b = b_ref[:, c0:c0 + w].astype(F32)
            ds = _dot_nt(dy, wd_ref[c0:c0 + w, :])
            sig = _sigmoid(a)
            sl = a * sig
            s_ref[:, c0:c0 + w] = (sl * b).astype(s_ref.dtype)
            db = (ds * sl).astype(MM)
            da = (ds * b * (sig * (1.0 + a * (1.0 - sig)))).astype(MM)
            da_ref[:, c0:c0 + w] = da
            db_ref[:, c0:c0 + w] = db
            dh = dh + _dot_nt(da, wg_ref[:, c0:c0 + w]) + _dot_nt(db, wu_ref[:, c0:c0 + w])
        dx, dg = _rms_bwd(dh, xh, r, gv)
        dx_ref[...] = do_ref[...] + dx
        dg_ref[...] += dg
        finish()

    tok = lambda i: (i, 0)
    par = lambda i: (0, 0)
    hid = BS((tm, F), tok)
    resident = lambda shape: BS(shape, par, pipeline_mode=pl.Buffered(1))
    return _pc(
        body, "ffn_bwd" + ("_hosting" if hosted else ""), (nt,),
        [BS((tm, D), tok), BS((tm, D), tok), BS((1, D), par), hid, hid,
         resident((D, F)), resident((D, F)), resident((F, D))] + h_in,
        [BS((tm, D), tok), BS((tm, D), tok), BS((tm, D), tok), hid, hid, hid, BS((1, D), par)] + h_out,
        [SDS((T, D), F32), SDS((T, D), MM), SDS((T, D), MM),
         SDS((T, F), MM), SDS((T, F), MM), SDS((T, F), MM), SDS((1, D), F32)] + h_shape,
        scratch=h_scratch,
    )(x, dout, g, a, b, wg, wu, wd, *(hosted[1] if hosted else []))


def _wgrad(a, b, name, hosted=None):
    T, K = a.shape
    N = b.shape[1]
    tt = min(512, T)
    tn = next(c for c in (N, 1408, 1280, 1024, 512, 256, 128) if N % c == 0 and K * c * 4 <= WGRAD_ACC_BYTES)
    pieces = [(c0, min(WGRAD_COLS, tn - c0)) for c0 in range(0, tn, WGRAD_COLS)]
    nn, nt = N // tn, T // tt
    h_in, h_out, h_shape, h_scratch = _hosted_specs(hosted)

    def body(*refs):
        n, t = pl.program_id(0), pl.program_id(1)
        refs, finish = _hosted_edges(hosted, refs, 2, 1, (n == 0) & (t == 0), (n == nn - 1) & (t == nt - 1))
        a_ref, b_ref, o_ref = refs

        @pl.when(t == 0)
        def _():
            o_ref[...] = jnp.zeros_like(o_ref)

        av = a_ref[...].astype(MM)
        for c0, w in pieces:
            o_ref[:, c0:c0 + w] += _dot_tn(av, b_ref[:, c0:c0 + w].astype(MM))
        finish()

    res = _pc(
        body, name + ("_hosting" if hosted else ""), (nn, nt),
        [BS((tt, K), lambda n, t: (t, 0)), BS((tt, tn), lambda n, t: (t, n))] + h_in,
        [BS((K, tn), lambda n, t: (0, n))] + h_out, [SDS((K, N), F32)] + h_shape,
        scratch=h_scratch)(a, b, *(hosted[1] if hosted else []))
    return res if hosted else res[0]


C_POOL, C_Q, C_K, C_V, C_CA, C_CG, C_ZF, C_END = 0, 256, 768, 1280, 1792, 2048, 2304, 2560


def _mix_in_fwd(x, g, w):
    T, D = x.shape
    tm = min(512, T)

    def body(x_ref, g_ref, w_ref, up_ref, q_ref, k_ref, v_ref, ca_ref, cg_ref, zf_ref):
        _, _, hg = _rms_fwd(x_ref[...], g_ref[...])
        p = _dot(hg.astype(MM), w_ref[...])
        up_ref[...] = p[:, C_POOL:C_Q]
        q_ref[...] = p[:, C_Q:C_K].astype(q_ref.dtype)
        k_ref[...] = p[:, C_K:C_V].astype(k_ref.dtype)
        v_ref[...] = p[:, C_V:C_CA].astype(v_ref.dtype)
        ca_ref[...] = p[:, C_CA:C_CG]
        cg_ref[...] = p[:, C_CG:C_ZF]
        zf_ref[...] = p[:, C_ZF:C_ZF + LANES]

    tok = lambda i: (i, 0)
    widths = (256, 512, 512, 512, 256, 256, 128)
    dtypes = (F32, MM, MM, MM, F32, F32, F32)
    return _pc(
        body, "mix_in_fwd", (T // tm,),
        [BS((tm, D), tok), BS((1, D), lambda i: (0, 0)), BS((D, C_END), lambda i: (0, 0))],
        [BS((tm, wd), tok) for wd in widths],
        [SDS((T, wd), dt) for wd, dt in zip(widths, dtypes)])(x, g, w)


def _mix_in_bwd(x, dout, g, w, dup, dq, dk, dv, dca, dcg, dzf):
    T, D = x.shape
    tm = min(512, T)

    def body(x_ref, do_ref, g_ref, w_ref, dup_ref, dq_ref, dk_ref, dv_ref, dca_ref, dcg_ref, dzf_ref,
             dx_ref, h_ref, dp_ref, dg_ref):
        @pl.when(pl.program_id(0) == 0)
        def _():
            dg_ref[...] = jnp.zeros_like(dg_ref)

        gv = g_ref[...]
        xh, r, hg = _rms_fwd(x_ref[...], gv)
        h_ref[...] = hg.astype(h_ref.dtype)
        for ref, lo, hi in ((dup_ref, C_POOL, C_Q), (dq_ref, C_Q, C_K), (dk_ref, C_K, C_V), (dv_ref, C_V, C_CA),
                            (dca_ref, C_CA, C_CG), (dcg_ref, C_CG, C_ZF), (dzf_ref, C_ZF, C_ZF + LANES)):
            dp_ref[:, lo:hi] = ref[...].astype(dp_ref.dtype)
        dp_ref[:, C_ZF + LANES:C_END] = jnp.zeros((tm, C_END - C_ZF - LANES), dp_ref.dtype)
        dh = _dot_nt(dp_ref[...], w_ref[...])
        dx, dg = _rms_bwd(dh, xh, r, gv)
        dx_ref[...] = do_ref[...] + dx
        dg_ref[...] += dg

    tok = lambda i: (i, 0)
    widths = (256, 512, 512, 512, 256, 256, 128)
    return _pc(
        body, "mix_in_bwd", (T // tm,),
        [BS((tm, D), tok), BS((tm, D), tok), BS((1, D), lambda i: (0, 0)), BS((D, C_END), lambda i: (0, 0))]
        + [BS((tm, wd), tok) for wd in widths],
        [BS((tm, D), tok), BS((tm, D), tok), BS((tm, C_END), tok), BS((1, D), lambda i: (0, 0))],
        [SDS((T, D), F32), SDS((T, D), MM), SDS((T, C_END), MM), SDS((1, D), F32)],
    )(x, dout, g, w, dup, dq, dk, dv, dca, dcg, dzf)


def _mix_out_fwd(x, ya, yb, yc, wo):
    T, D = x.shape
    tm = min(512, T)

    def body(x_ref, ya_ref, yb_ref, yc_ref, wo_ref, o_ref):
        o_ref[...] = (x_ref[...] + _dot(ya_ref[...].astype(MM), wo_ref[0:256, :])
                      + _dot(yb_ref[...].astype(MM), wo_ref[256:768, :])
                      + _dot(yc_ref[...].astype(MM), wo_ref[768:1024, :]))

    tok = lambda i: (i, 0)
    return _pc(
        body, "mix_out_fwd", (T // tm,),
        [BS((tm, D), tok), BS((tm, 256), tok), BS((tm, 512), tok), BS((tm, 256), tok), BS((D, D), lambda i: (0, 0))],
        BS((tm, D), tok), SDS((T, D), F32))(x, ya, yb, yc, wo)


def _mix_out_bwd(dx, wo):
    T, D = dx.shape
    tm = min(512, T)

    def body(dx_ref, wo_ref, dya_ref, dyb_ref, dyc_ref):
        dy = _dot_nt(dx_ref[...].astype(MM), wo_ref[...])
        dya_ref[...] = dy[:, 0:256]
        dyb_ref[...] = dy[:, 256:768]
        dyc_ref[...] = dy[:, 768:1024]

    tok = lambda i: (i, 0)
    return _pc(
        body, "mix_out_bwd", (T // tm,),
        [BS((tm, D), tok), BS((D, D), lambda i: (0, 0))],
        [BS((tm, 256), tok), BS((tm, 512), tok), BS((tm, 256), tok)],
        [SDS((T, 256), F32), SDS((T, 512), F32), SDS((T, 256), F32)])(dx, wo)


def _fgate_fwd(zf, bias):
    T = zf.shape[0]
    tc = min(256, T)

    def body(z_ref, b_ref, f_ref, carry):
        @pl.when(pl.program_id(0) == 0)
        def _():
            carry[...] = jnp.zeros_like(carry)

        z = z_ref[...] + b_ref[...]
        logf = jnp.minimum(z, 0.0) - jnp.log(1.0 + jnp.exp(-jnp.abs(z)))
        row = lax.broadcasted_iota(jnp.int32, (tc, tc), 0)
        col = lax.broadcasted_iota(jnp.int32, (tc, tc), 1)
        tri = (col <= row).astype(F32)
        f_ref[...] = jnp.dot(tri, logf, precision=lax.Precision.HIGHEST, preferred_element_type=F32) + carry[...]
        carry[...] += jnp.sum(logf, axis=0, keepdims=True)

    return _pc(
        body, "fgate_fwd", (T // tc,),
        [BS((tc, LANES), lambda i: (i, 0)), BS((1, LANES), lambda i: (0, 0))],
        BS((tc, LANES), lambda i: (i, 0)), SDS((T, LANES), F32),
        scratch=[pltpu.VMEM((1, LANES), F32)])(zf, bias)


def _fgate_bwd(zf, bias, dFq, dFk):
    T = zf.shape[0]
    tc = min(256, T)
    n = T // tc
    slabs = dFq.shape[0]

    def body(z_ref, b_ref, dfq_ref, dfk_ref, dz_ref, db_ref, carry):
        @pl.when(pl.program_id(0) == 0)
        def _():
            carry[...] = jnp.zeros_like(carry)
            db_ref[...] = jnp.zeros_like(db_ref)

        df = dfk_ref[...]
        for sl in range(slabs):
            df = df + dfq_ref[sl]
        row = lax.broadcasted_iota(jnp.int32, (tc, tc), 0)
        col = lax.broadcasted_iota(jnp.int32, (tc, tc), 1)
        tri = (col >= row).astype(F32)
        dlogf = jnp.dot(tri, df, precision=lax.Precision.HIGHEST, preferred_element_type=F32) + carry[...]
        carry[...] += jnp.sum(df, axis=0, keepdims=True)
        lane = lax.broadcasted_iota(jnp.int32, (1, LANES), 1)
        dz = jnp.where(lane < HEADS, dlogf * _sigmoid(-(z_ref[...] + b_ref[...])), 0.0)
        dz_ref[...] = dz
        db_ref[...] += jnp.sum(dz, axis=0, keepdims=True)

    rev = lambda i: (n - 1 - i, 0)
    return _pc(
        body, "fgate_bwd", (n,),
        [BS((tc, LANES), rev), BS((1, LANES), lambda i: (0, 0)), BS((slabs, tc, LANES), lambda i: (0, n - 1 - i, 0)),
         BS((tc, LANES), rev)],
        [BS((tc, LANES), rev), BS((1, LANES), lambda i: (0, 0))],
        [SDS((T, LANES), F32), SDS((1, LANES), F32)],
        scratch=[pltpu.VMEM((1, LANES), F32)])(zf, bias, dFq, dFk)


LOG2E = 1.4426950408889634


def _split3(x):
    hi = x.astype(MM)
    r1 = x - hi.astype(F32)
    mid = r1.astype(MM)
    return hi, mid, (r1 - mid.astype(F32)).astype(MM)


def _place(lane, base, cols):
    out = jnp.zeros((cols[0].shape[0], LANES), MM)
    for i, c in enumerate(cols):
        out = jnp.where(lane == base + i, c, out)
    return out


def _head_col(block, lane, h):
    return jnp.sum(jnp.where(lane == h, block, 0.0), axis=-1, keepdims=True)


def _own_lanes(lane, hh):
    return (lane < HEAD_DIM) if hh == 0 else (lane >= HEAD_DIM)


def _attn_k_side(k_ref, f_ref, kb_ref, hp, T, rows, lse_ones, v_ref=None, vb_ref=None):
    lane = lax.broadcasted_iota(jnp.int32, (1, LANES), 1)
    one = jnp.ones((rows, 1), MM)

    def chunk(c, _):
        r0 = pl.multiple_of(c * rows, rows)
        kp = k_ref[pl.ds(r0, rows), :]
        fblk = f_ref[pl.ds(r0, rows), :]
        for hh in range(2):
            hi, mid, lo = _split3(-_head_col(fblk, lane, 2 * hp + hh) * LOG2E)
            cols = [one, one, one, hi, mid, lo] + ([one, one, one] if lse_ones else [])
            bias = _place(lane, HEAD_DIM * (1 - hh), cols)
            kb_ref[hh, pl.ds(r0, rows), :] = jnp.where(_own_lanes(lane, hh), kp, bias)
            if vb_ref is not None:
                vb_ref[hh, pl.ds(r0, rows), :] = jnp.where(_own_lanes(lane, hh), v_ref[pl.ds(r0, rows), :],
                                                           jnp.ones((rows, LANES), MM))
        return 0

    lax.fori_loop(0, T // rows, chunk, 0)


def _attn_q_side(qp, fblk, lane, hp, scale, lse_blk=None):
    qc = qp.astype(F32) * (scale * LOG2E)
    qhi = qc.astype(MM)
    qlo = (qc - qhi.astype(F32)).astype(MM)
    one = jnp.ones((qp.shape[0], 1), MM)
    out = []
    for hh in range(2):
        cols = list(_split3(_head_col(fblk, lane, 2 * hp + hh) * LOG2E)) + [one, one, one]
        if lse_blk is not None:
            cols += list(_split3(-_head_col(lse_blk, lane, 2 * hp + hh)))
        bias = _place(lane, HEAD_DIM * (1 - hh), cols)
        own = _own_lanes(lane, hh)
        out.append(jnp.concatenate([jnp.where(own, qhi, jnp.zeros_like(qhi)), jnp.where(own, qlo, bias)], axis=1))
    return out


def _causal(tq, tk):
    return lax.broadcasted_iota(jnp.int32, (tq, tk), 1) <= lax.broadcasted_iota(jnp.int32, (tq, tk), 0)


def _hosted_specs(hosted):
    if hosted is None:
        return [], [], [], []
    kind, arrays = hosted
    n = len(arrays)
    return [ANY] * n, [ANY] * n, [kind.out_shape(a) for a in arrays], kind.scratch(n)


def _hosted_edges(hosted, refs, n_in, n_out, first, last, mid=None):
    if hosted is None:
        return refs, lambda: None
    kind, arrays = hosted
    n = len(arrays)
    nsem = len(kind.scratch(n))
    o0 = n_in + n + n_out
    ins, outs, sems = refs[n_in:n_in + n], refs[o0:o0 + n], refs[len(refs) - nsem:]
    relayed = mid is not None and hasattr(kind, "relay")

    @pl.when(first)
    def _():
        kind.start(ins, outs, *sems)

    if relayed:
        @pl.when(mid)
        def _():
            kind.relay(ins, outs, *sems)

    def finish():
        @pl.when(last)
        def _():
            kind.wait(ins, outs, *sems, **({"relayed": True} if relayed else {}))

    return refs[:n_in] + refs[n_in + n:o0] + refs[o0 + n:len(refs) - nsem], finish


def _attn_fwd(q, k, v, F, hosted=None):
    T = q.shape[0]
    tq = min(ATTN_FWD_TILE, T)
    tk = tq
    nq = T // tq
    scale = 1.0 / math.sqrt(HEAD_DIM)
    h_in, h_out, h_shape, h_scratch = _hosted_specs(hosted)

    def body(*refs):
        hp, ib = pl.program_id(0), pl.program_id(1)
        refs, finish = _hosted_edges(hosted, refs, 5, 2, (hp == 0) & (ib == 0), (hp == HEADS // 2 - 1) & (ib == nq - 1),
                                     mid=(hp == HEADS // 2 - 1) & (ib == 0))
        q_ref, k_ref, v_ref, fq_ref, f_ref, o_ref, lse_ref, kb_ref, vb_ref = refs
        lane = lax.broadcasted_iota(jnp.int32, (1, LANES), 1)

        @pl.when(ib == 0)
        def _():
            _attn_k_side(k_ref, f_ref, kb_ref, hp, T, min(512, T), False, v_ref, vb_ref)

        qa = _attn_q_side(q_ref[...], fq_ref[...], lane, hp, scale)

        def tile(jb, carry, masked):
            off = pl.multiple_of(jb * tk, tk)
            kp = k_ref[pl.ds(off, tk), :]
            new = []
            for hh in range(2):
                m, acc = carry[hh]
                s = _dot_nt(qa[hh], jnp.concatenate([kp, kb_ref[hh, pl.ds(off, tk), :]], axis=1))
                if masked:
                    s = jnp.where(_causal(tq, tk), s, -jnp.inf)
                m2 = jnp.maximum(m, jnp.max(s, axis=-1, keepdims=True))
                p = jnp.exp2(s - m2)
                new.append((m2, acc * jnp.exp2(m - m2) + _dot(p.astype(MM), vb_ref[hh, pl.ds(off, tk), :])))
            return tuple(new)

        init = tuple((jnp.full((tq, 1), -jnp.inf, F32), jnp.zeros((tq, LANES), F32)) for _ in range(2))
        carry = lax.fori_loop(0, ib, lambda jb, c: tile(jb, c, False), init)
        (m0, a0), (m1, a1) = tile(ib, carry, True)
        l0, l1 = a0[:, HEAD_DIM:HEAD_DIM + 1], a1[:, 0:1]
        o_ref[...] = jnp.where(lane < HEAD_DIM, a0 / l0, a1 / l1)
        lse_ref[...] = jnp.where(lane == 2 * hp, m0 + jnp.log2(l0), jnp.where(lane == 2 * hp + 1, m1 + jnp.log2(l1), 0.0))
        finish()

    blk = lambda h, i: (i, h)
    full = lambda h, i: (0, h)
    return _pc(
        body, "attn_fwd" + ("_hosting" if hosted else ""), (HEADS // 2, nq),
        [BS((tq, LANES), blk), BS((T, LANES), full), BS((T, LANES), full), BS((tq, LANES), lambda h, i: (i, 0)),
         BS((T, LANES), lambda h, i: (0, 0))] + h_in,
        [BS((tq, LANES), blk), BS((None, tq, LANES), lambda h, i: (h, i, 0))] + h_out,
        [SDS((T, HEADS * HEAD_DIM), F32), SDS((HEADS // 2, T, LANES), F32)] + h_shape,
        scratch=[pltpu.VMEM((2, T, LANES), MM)] * 2 + h_scratch)(q, k, v, F, F, *(hosted[1] if hosted else []))


def _attn_bwd(q, k, v, F, o, lse, do, hosted=None):
    T = q.shape[0]
    tq = min(512, T)
    tk = tq
    nq = T // tq
    scale = 1.0 / math.sqrt(HEAD_DIM)
    h_in, h_out, h_shape, h_scratch = _hosted_specs(hosted)

    def body(*refs):
        hp, ib = pl.program_id(0), pl.program_id(1)
        refs, finish = _hosted_edges(hosted, refs, 8, 5, (hp == 0) & (ib == 0), (hp == HEADS // 2 - 1) & (ib == nq - 1))
        (q_ref, k_ref, v_ref, fq_ref, f_ref, o_ref, lse_ref, do_ref,
         dq_ref, dk_ref, dv_ref, dfq_ref, dfk_ref, kb_ref, dk_acc, dv_acc) = refs
        lane = lax.broadcasted_iota(jnp.int32, (1, LANES), 1)

        @pl.when(ib == 0)
        def _():
            _attn_k_side(k_ref, f_ref, kb_ref, hp, T, tk, True)
            dk_acc[...] = jnp.zeros_like(dk_acc)
            dv_acc[...] = jnp.zeros_like(dv_acc)
            dfk_ref[...] = jnp.zeros_like(dfk_ref)

        qp = q_ref[...]
        qa = _attn_q_side(qp, fq_ref[...], lane, hp, scale, lse_ref[...])
        dob = do_ref[...].astype(MM)
        dprod = dob.astype(F32) * o_ref[...]
        qs = (qp.astype(F32) * scale).astype(MM)
        heads = []
        for hh in range(2):
            own = _own_lanes(lane, hh)
            heads.append((jnp.where(own, dob, jnp.zeros_like(dob)), jnp.where(own, qs, jnp.zeros_like(qs)),
                          jnp.sum(jnp.where(own, dprod, 0.0), axis=-1, keepdims=True)))

        def tile(jb, carry, masked):
            off = pl.multiple_of(jb * tk, tk)
            kp = k_ref[pl.ds(off, tk), :]
            vp = v_ref[pl.ds(off, tk), :]
            new = []
            dv_t = jnp.zeros((tk, LANES), F32)
            dk_t = jnp.zeros((tk, LANES), F32)
            for hh in range(2):
                dq, rs = carry[hh]
                dom, qm, delta = heads[hh]
                p = jnp.exp2(_dot_nt(qa[hh], jnp.concatenate([kp, kb_ref[hh, pl.ds(off, tk), :]], axis=1)))
                if masked:
                    p = jnp.where(_causal(tq, tk), p, 0.0)
                ds = p * (_dot_nt(dom, vp) - delta)
                dsb = ds.astype(MM)
                dv_t = dv_t + _dot_tn(p.astype(MM), dom)
                dk_t = dk_t + _dot_tn(dsb, qm)
                dfk_ref[jb, pl.ds(hh, 1), :] -= jnp.sum(ds, axis=0, keepdims=True)
                new.append((dq + _dot(dsb, kp), rs + jnp.sum(ds, axis=-1, keepdims=True)))
            dv_acc[pl.ds(off, tk), :] += dv_t
            dk_acc[pl.ds(off, tk), :] += dk_t
            return tuple(new)

        init = tuple((jnp.zeros((tq, LANES), F32), jnp.zeros((tq, 1), F32)) for _ in range(2))
        carry = lax.fori_loop(0, ib, lambda jb, c: tile(jb, c, False), init)
        (dq0, rs0), (dq1, rs1) = tile(ib, carry, True)
        dq_ref[...] = (jnp.where(lane < HEAD_DIM, dq0, dq1) * scale).astype(dq_ref.dtype)
        dfq_ref[...] = jnp.where(lane == 2 * hp, rs0, jnp.where(lane == 2 * hp + 1, rs1, 0.0))

        @pl.when(ib == nq - 1)
        def _():
            dk_ref[...] = dk_acc[...].astype(dk_ref.dtype)
            dv_ref[...] = dv_acc[...].astype(dv_ref.dtype)

        finish()

    blk = lambda h, i: (i, h)
    full = lambda h, i: (0, h)
    slab = BS((None, tq, LANES), lambda h, i: (h, i, 0))
    return _pc(
        body, "attn_bwd" + ("_hosting" if hosted else ""), (HEADS // 2, nq),
        [BS((tq, LANES), blk), BS((T, LANES), full), BS((T, LANES), full), BS((tq, LANES), lambda h, i: (i, 0)),
         BS((T, LANES), lambda h, i: (0, 0)), BS((tq, LANES), blk), slab, BS((tq, LANES), blk)] + h_in,
        [BS((tq, LANES), blk), BS((T, LANES), full), BS((T, LANES), full), slab,
         BS((None, nq, 2, tk), lambda h, i: (h, 0, 0, 0))] + h_out,
        [SDS((T, HEADS * HEAD_DIM), MM)] * 3 + [SDS((HEADS // 2, T, LANES), F32), SDS((HEADS // 2, nq, 2, tk), F32)]
        + h_shape,
        scratch=[pltpu.VMEM((2, T, LANES), MM), pltpu.VMEM((T, LANES), F32), pltpu.VMEM((T, LANES), F32)] + h_scratch,
    )(q, k, v, F, F, o, lse, do, *(hosted[1] if hosted else []))


POOL_HALO = 16
CONV_HALO = 32


def _group_select(lane, v0, v1, v2, v3):
    return jnp.where(lane < 64, v0, jnp.where(lane < 128, v1, jnp.where(lane < 192, v2, v3)))


def _roll_down(x, k):
    return x if k == 0 else pltpu.roll(x, k, 0)


def _roll_up(x, k):
    return x if k == 0 else pltpu.roll(x, x.shape[0] - k, 0)


def _pool_terms(u, u_prev, tile, tm):
    ext = jnp.concatenate([u_prev, u], axis=0)
    s2 = ext + _roll_down(ext, 1)
    s4 = s2 + _roll_down(s2, 2)
    s8 = s4 + _roll_down(s4, 4)
    s16 = s8 + _roll_down(s8, 8)
    lane = lax.broadcasted_iota(jnp.int32, (1, 256), 1)
    ws = _group_select(lane, s2, s4, s8, s16)[POOL_HALO:, :]
    wlen = _group_select(lane, *map(float, POOL_WINDOWS)).astype(F32)
    return ws / _pool_count(tile, tm, tm, wlen) - u


def _pool_count(tile, tm, rows, wlen):
    t = (tile * tm + 1 + lax.broadcasted_iota(jnp.int32, (rows, 1), 0)).astype(F32)
    return jnp.minimum(t, wlen)


def _layer_norm(y, lg, lb):
    mu = jnp.mean(y, axis=-1, keepdims=True)
    yc = y - mu
    rstd = lax.rsqrt(jnp.mean(yc * yc, axis=-1, keepdims=True) + NORM_EPS)
    yh = yc * rstd
    return yh, rstd, yh * lg + lb


def _halo_specs(tm, T, halo, prev):
    per = tm // halo
    if prev:
        return BS((halo, 256), lambda i: (jnp.maximum(i * per - 1, 0), 0))
    return BS((halo, 256), lambda i: (jnp.minimum((i + 1) * per, T // halo - 1), 0))


def _local_fwd(up, ca, cg, bd, pscale, cw, cb, lg, lb):
    T = up.shape[0]
    tm = min(512, T)

    def body(up_ref, uph_ref, ca_ref, cah_ref, cg_ref, cgh_ref, bd_ref, ps_ref, cw_ref, cb_ref, lg_ref, lb_ref,
             ya_ref, yc_ref, u_ref, y_ref):
        i = pl.program_id(0)
        first = i == 0
        pooled = _pool_terms(up_ref[...], jnp.where(first, 0.0, uph_ref[...]), i, tm)
        ya_ref[...] = (_dot(pooled.astype(MM), bd_ref[...]) * ps_ref[...]).astype(ya_ref.dtype)

        u = ca_ref[...] * _sigmoid(cg_ref[...])
        uh = jnp.where(first, 0.0, cah_ref[...] * _sigmoid(cgh_ref[...]))
        ext = jnp.concatenate([uh, u], axis=0)
        y = jnp.zeros((tm, 256), F32) + cb_ref[...]
        for kk in range(CONV_K):
            y = y + cw_ref[kk:kk + 1, :] * _roll_up(ext, CONV_HALO - (CONV_K - 1) + kk)[:tm, :]
        _, _, z = _layer_norm(y, lg_ref[...], lb_ref[...])
        yc_ref[...] = (z * _sigmoid(z)).astype(yc_ref.dtype)
        u_ref[...] = u
        y_ref[...] = y

    tok = lambda i: (i, 0)
    par = lambda i: (0, 0)
    t256 = BS((tm, 256), tok)
    return _pc(
        body, "local_fwd", (T // tm,),
        [t256, _halo_specs(tm, T, POOL_HALO, True), t256, _halo_specs(tm, T, CONV_HALO, True),
         t256, _halo_specs(tm, T, CONV_HALO, True),
         BS((256, 256), par), BS((1, 256), par), BS((32, 256), par), BS((1, 256), par), BS((1, 256), par),
         BS((1, 256), par)],
        [t256, t256, t256, t256],
        [SDS((T, 256), MM), SDS((T, 256), MM), SDS((T, 256), F32), SDS((T, 256), F32)],
    )(up, up, ca, ca, cg, cg, bd, pscale, cw, cb, lg, lb)


def _local_bwd(up, dya, ca, cg, u, y, dyc, bd, pscale, cw, lg, lb):
    T = up.shape[0]
    tm = min(512, T)
    n = T // tm

    def body(up_ref, uph_ref, dya_ref, dyan_ref, ca_ref, cg_ref, u_ref, y_ref, yn_ref, dyc_ref, dycn_ref,
             bd_ref, ps_ref, cw_ref, lg_ref, lb_ref,
             dup_ref, dca_ref, dcg_ref, dbd_ref, dps_ref, dcw_ref, dcb_ref, dlg_ref, dlb_ref):
        i = pl.program_id(0)
        first = i == 0
        last = i == n - 1

        @pl.when(first)
        def _():
            for ref in (dbd_ref, dps_ref, dcw_ref, dcb_ref, dlg_ref, dlb_ref):
                ref[...] = jnp.zeros_like(ref)

        ps = ps_ref[...]
        pooled = _pool_terms(up_ref[...], jnp.where(first, 0.0, uph_ref[...]), i, tm).astype(MM)
        dya_t = dya_ref[...]
        dps_ref[...] += jnp.sum(dya_t * _dot(pooled, bd_ref[...]), axis=0, keepdims=True)
        dm = (jnp.concatenate([dya_t, jnp.where(last, 0.0, dyan_ref[...])], axis=0) * ps).astype(MM)
        dbd_ref[...] += _dot_tn(pooled, dm[:tm, :])
        dpool = _dot_nt(dm, bd_ref[...])
        lane = lax.broadcasted_iota(jnp.int32, (1, 256), 1)
        wlen = _group_select(lane, *map(float, POOL_WINDOWS)).astype(F32)
        e = dpool / _pool_count(i, tm, tm + POOL_HALO, wlen)
        f2 = e + _roll_up(e, 1)
        f4 = f2 + _roll_up(f2, 2)
        f8 = f4 + _roll_up(f4, 4)
        f16 = f8 + _roll_up(f8, 8)
        dup_ref[...] = _group_select(lane, f2, f4, f8, f16)[:tm, :] - dpool[:tm, :]

        lgv = lg_ref[...]
        yext = jnp.concatenate([y_ref[...], yn_ref[...]], axis=0)
        dyc = jnp.concatenate([dyc_ref[...], jnp.where(last, 0.0, dycn_ref[...])], axis=0)
        yh, rstd, z = _layer_norm(yext, lgv, lb_ref[...])
        sig = _sigmoid(z)
        dz = dyc * (sig * (1.0 + z * (1.0 - sig)))
        dlg_ref[...] += jnp.sum((dz * yh)[:tm, :], axis=0, keepdims=True)
        dlb_ref[...] += jnp.sum(dz[:tm, :], axis=0, keepdims=True)
        dyh = dz * lgv
        dy = rstd * (dyh - jnp.mean(dyh, axis=-1, keepdims=True) - yh * jnp.mean(dyh * yh, axis=-1, keepdims=True))
        dcb_ref[...] += jnp.sum(dy[:tm, :], axis=0, keepdims=True)
        uv = u_ref[...]
        du = jnp.zeros((tm, 256), F32)
        for kk in range(CONV_K):
            ahead = _roll_up(dy, CONV_K - 1 - kk)[:tm, :]
            dcw_ref[kk:kk + 1, :] += jnp.sum(uv * ahead, axis=0, keepdims=True)
            du = du + cw_ref[kk:kk + 1, :] * ahead
        sg = _sigmoid(cg_ref[...])
        dca_ref[...] = du * sg
        dcg_ref[...] = du * ca_ref[...] * sg * (1.0 - sg)

    tok = lambda i: (i, 0)
    par = lambda i: (0, 0)
    t256 = BS((tm, 256), tok)
    p1 = BS((1, 256), par)
    return _pc(
        body, "local_bwd", (n,),
        [t256, _halo_specs(tm, T, POOL_HALO, True), t256, _halo_specs(tm, T, POOL_HALO, False), t256, t256,
         t256, t256, _halo_specs(tm, T, CONV_HALO, False), t256, _halo_specs(tm, T, CONV_HALO, False),
         BS((256, 256), par), p1, BS((32, 256), par), p1, p1],
        [t256, t256, t256, BS((256, 256), par), p1, BS((32, 256), par), p1, p1, p1],
        [SDS((T, 256), F32)] * 3 + [SDS((256, 256), F32), SDS((1, 256), F32), SDS((32, 256), F32)]
        + [SDS((1, 256), F32)] * 3,
    )(up, up, dya, dya, ca, cg, u, y, y, dyc, dyc, bd, pscale, cw, lg, lb)


def _head(x, g, target):
    T, D = x.shape
    tm = min(512, T)

    def body(x_ref, g_ref, t_ref, loss_ref, dx_ref, dg_ref):
        @pl.when(pl.program_id(0) == 0)
        def _():
            loss_ref[...] = jnp.zeros_like(loss_ref)
            dg_ref[...] = jnp.zeros_like(dg_ref)

        gv = g_ref[...]
        xh, r, yv = _rms_fwd(x_ref[...], gv)
        err = yv - t_ref[...]
        loss_ref[...] += 0.5 * jnp.sum(jnp.mean(err * err, axis=-1, keepdims=True), axis=0, keepdims=True)
        dx, dg = _rms_bwd(err * (1.0 / D), xh, r, gv)
        dx_ref[...] = dx
        dg_ref[...] += dg

    tok = lambda i: (i, 0)
    par = lambda i: (0, 0)
    return _pc(
        body, "head", (T // tm,),
        [BS((tm, D), tok), BS((1, D), par), BS((tm, D), tok)],
        [BS((1, LANES), par), BS((tm, D), tok), BS((1, D), par)],
        [SDS((1, LANES), F32), SDS((T, D), F32), SDS((1, D), F32)])(x, g, target)


def _adamw(w, gs, m, v, name):
    R, C = w.shape
    tr = R
    for cand in (512, 256, 128, 64, 32, 16, 8):
        if R % cand == 0:
            tr = cand
            break
    stacked = not isinstance(gs, (list, tuple))
    ng = 1 if stacked else len(gs)

    def body(*refs):
        w_ref, g_refs, m_ref, v_ref = refs[0], refs[1:1 + ng], refs[1 + ng], refs[2 + ng]
        g_ref, d_ref, m2_ref, v2_ref = refs[3 + ng:]
        terms = [g_refs[0][d] for d in range(gs.shape[0])] if stacked else [r[...] for r in g_refs]
        g = terms[0]
        for term in terms[1:]:
            g = g + term
        m2 = ADAM_B1 * m_ref[...] + (1.0 - ADAM_B1) * g
        v2 = ADAM_B2 * v_ref[...] + (1.0 - ADAM_B2) * jnp.square(g)
        m_hat = m2 / (1.0 - ADAM_B1 ** ADAM_STEP)
        v_hat = v2 / (1.0 - ADAM_B2 ** ADAM_STEP)
        g_ref[...] = g
        d_ref[...] = -ADAM_LR * (m_hat / (jnp.sqrt(v_hat) + ADAM_EPS) + ADAM_WD * w_ref[...])
        m2_ref[...] = m2
        v2_ref[...] = v2

    blk = BS((tr, C), lambda i: (i, 0))
    g_specs = [BS((gs.shape[0], tr, C), lambda i: (0, i, 0))] if stacked else [blk] * ng
    return _pc(body, name, (R // tr,), [blk] + g_specs + [blk, blk], [blk] * 4,
               [SDS((R, C), F32)] * 4)(w, *([gs] if stacked else gs), m, v)


def _sum_parts(owns, recvs, name):
    L = len(owns)
    R, C = owns[0].shape
    tr = next(t for t in (512, 256, 128, 64, 32, 16) if R % t == 0)

    def body(*refs):
        l = pl.program_id(0)
        s_ref = refs[2 * L]
        for ll in range(L):
            @pl.when(l == ll)
            def _(o_ref=refs[ll], r_ref=refs[L + ll]):
                s_ref[...] = ((o_ref[...] + r_ref[0].astype(F32)) + r_ref[1].astype(F32)) + r_ref[2].astype(F32)

    own_specs = [BS((tr, C), lambda l, i, ll=ll: (jnp.where(l == ll, i, 0), 0)) for ll in range(L)]
    recv_specs = [BS((3, tr, C), lambda l, i, ll=ll: (0, jnp.where(l == ll, i, 0), 0)) for ll in range(L)]
    return _pc(body, name, (L, R // tr), own_specs + recv_specs,
               BS((None, tr, C), lambda l, i: (l, i, 0)), SDS((L, R, C), F32))(*owns, *recvs)


def _sum8(parts, name):
    _, R, C = parts.shape

    def body(p_ref, s_ref):
        acc = p_ref[0]
        for d in range(1, 8):
            acc = acc + p_ref[d]
        s_ref[...] = acc

    return _pc(body, name, (1,), [BS((8, R, C), lambda i: (0, 0, 0))], BS((R, C), lambda i: (0, 0)),
               SDS((R, C), F32))(parts)


def _position():
    return lax.axis_index("x"), lax.axis_index("y"), lax.axis_index("c")


CHIP_FLIPS = ((1, 0), (0, 1), (1, 1))


class _GatherChips:
    @staticmethod
    def scratch(n):
        return [pltpu.SemaphoreType.DMA((3 * n,)), pltpu.SemaphoreType.DMA((3 * n,)), pltpu.SemaphoreType.DMA((n,))]

    @staticmethod
    def out_shape(block):
        return SDS((4,) + tuple(block.shape), block.dtype)

    @staticmethod
    def _copies(ins, outs, send_sems, recv_sems, local_sems, arrivals):
        x, y, c = _position()
        local, remote = [], []
        for i, (in_ref, out_ref) in enumerate(zip(ins, outs)):
            local.append(pltpu.make_async_copy(in_ref, out_ref.at[2 * x + y], local_sems.at[i]))
            for k, (fx, fy) in enumerate(CHIP_FLIPS):
                slot = 2 * (x ^ fx) + (y ^ fy) if arrivals else 2 * x + y
                remote.append(pltpu.make_async_remote_copy(
                    src_ref=in_ref, dst_ref=out_ref.at[slot], send_sem=send_sems.at[3 * i + k],
                    recv_sem=recv_sems.at[3 * i + k], device_id=(x ^ fx, y ^ fy, c), device_id_type=MESH))
        return local, remote

    @classmethod
    def start(cls, ins, outs, *sems):
        local, sends = cls._copies(ins, outs, *sems, arrivals=False)
        for cp in local + sends:
            cp.start()

    @classmethod
    def wait(cls, ins, outs, *sems):
        local, arrivals = cls._copies(ins, outs, *sems, arrivals=True)
        for cp in arrivals:
            cp.wait_recv()
        for cp in arrivals:
            cp.wait_send()
        for cp in local:
            cp.wait()


class _GatherChipsSplit(_GatherChips):
    @staticmethod
    def scratch(n):
        return [pltpu.SemaphoreType.DMA((6 * n,)), pltpu.SemaphoreType.DMA((6 * n,)), pltpu.SemaphoreType.DMA((n,))]

    @staticmethod
    def _half(ref, which):
        rows = ref.shape[0] // 2
        return ref.at[pl.ds(pl.multiple_of(which * rows, 16), rows)]

    @staticmethod
    def _local(ins, outs, local_sems):
        x, y, _ = _position()
        return [pltpu.make_async_copy(in_ref, out_ref.at[2 * x + y], local_sems.at[i])
                for i, (in_ref, out_ref) in enumerate(zip(ins, outs))]

    @classmethod
    def _between_chips(cls, ins, outs, send_sems, recv_sems, arrivals):
        x, y, c = _position()
        return [
            pltpu.make_async_remote_copy(
                src_ref=cls._half(in_ref, c),
                dst_ref=cls._half(out_ref.at[2 * (x ^ fx) + (y ^ fy) if arrivals else 2 * x + y], c),
                send_sem=send_sems.at[6 * i + k], recv_sem=recv_sems.at[6 * i + k],
                device_id=(x ^ fx, y ^ fy, c), device_id_type=MESH)
            for i, (in_ref, out_ref) in enumerate(zip(ins, outs)) for k, (fx, fy) in enumerate(CHIP_FLIPS)]

    @classmethod
    def _between_cores(cls, outs, send_sems, recv_sems, arrivals):
        x, y, c = _position()
        copies = []
        for i, out_ref in enumerate(outs):
            for k, (fx, fy) in enumerate(CHIP_FLIPS):
                half = cls._half(out_ref.at[2 * (x ^ fx) + (y ^ fy)], 1 - c if arrivals else c)
                copies.append(pltpu.make_async_remote_copy(
                    src_ref=half, dst_ref=half, send_sem=send_sems.at[6 * i + 3 + k],
                    recv_sem=recv_sems.at[6 * i + 3 + k], device_id=(x, y, 1 - c), device_id_type=MESH))
        return copies

    @classmethod
    def start(cls, ins, outs, send_sems, recv_sems, local_sems):
        for cp in cls._local(ins, outs, local_sems) + cls._between_chips(ins, outs, send_sems, recv_sems, False):
            cp.start()

    @classmethod
    def relay(cls, ins, outs, send_sems, recv_sems, local_sems):
        arrivals = cls._between_chips(ins, outs, send_sems, recv_sems, True)
        onward = cls._between_cores(outs, send_sems, recv_sems, False)
        for cp, nxt in zip(arrivals, onward):
            cp.wait_recv()
            nxt.start()

    @classmethod
    def wait(cls, ins, outs, send_sems, recv_sems, local_sems, relayed=False):
        if not relayed:
            cls.relay(ins, outs, send_sems, recv_sems, local_sems)
        for cp in cls._between_cores(outs, send_sems, recv_sems, True):
            cp.wait_recv()
        for cp in (cls._between_chips(ins, outs, send_sems, recv_sems, True)
                   + cls._between_cores(outs, send_sems, recv_sems, False)):
            cp.wait_send()
        for cp in cls._local(ins, outs, local_sems):
            cp.wait()


class _Symmetric:
    @classmethod
    def start(cls, ins, outs, *sems):
        for cp in cls._copies(ins, outs, *sems):
            cp.start()

    @classmethod
    def wait(cls, ins, outs, *sems):
        copies = cls._copies(ins, outs, *sems)
        for cp in copies:
            cp.wait_recv()
        for cp in copies:
            cp.wait_send()


class _ScatterChips(_Symmetric):
    @staticmethod
    def scratch(n):
        return [pltpu.SemaphoreType.DMA((3 * n,)), pltpu.SemaphoreType.DMA((3 * n,))]

    @staticmethod
    def out_shape(parts):
        return SDS((3,) + tuple(parts.shape[1:]), parts.dtype)

    @staticmethod
    def _copies(ins, outs, send_sems, recv_sems):
        x, y, c = _position()
        return [
            pltpu.make_async_remote_copy(
                src_ref=in_ref.at[2 * (x ^ fx) + (y ^ fy)], dst_ref=out_ref.at[k],
                send_sem=send_sems.at[3 * i + k], recv_sem=recv_sems.at[3 * i + k],
                device_id=(x ^ fx, y ^ fy, c), device_id_type=MESH)
            for i, (in_ref, out_ref) in enumerate(zip(ins, outs)) for k, (fx, fy) in enumerate(CHIP_FLIPS)]


class _SwapCores(_Symmetric):
    @staticmethod
    def scratch(n):
        return [pltpu.SemaphoreType.DMA((n,)), pltpu.SemaphoreType.DMA((n,))]

    @staticmethod
    def out_shape(block):
        return SDS(block.shape, block.dtype)

    @staticmethod
    def _copies(ins, outs, send_sems, recv_sems):
        x, y, c = _position()
        return [
            pltpu.make_async_remote_copy(
                src_ref=in_ref, dst_ref=out_ref, send_sem=send_sems.at[i], recv_sem=recv_sems.at[i],
                device_id=(x, y, 1 - c), device_id_type=MESH)
            for i, (in_ref, out_ref) in enumerate(zip(ins, outs))]


def _exchange(kind, arrays, name):
    n = len(arrays)

    def body(*refs):
        ins, outs, sems = refs[:n], refs[n:2 * n], refs[2 * n:]
        kind.start(ins, outs, *sems)
        kind.wait(ins, outs, *sems)

    return pl.pallas_call(body, out_shape=[kind.out_shape(a) for a in arrays], in_specs=[ANY] * n,
                          out_specs=[ANY] * n, name=name, scratch_shapes=kind.scratch(n))(*arrays)


DEVICE_FLIPS = tuple((fx, fy, fc) for fx in (0, 1) for fy in (0, 1) for fc in (0, 1))[1:]


class _GatherDevices(_GatherChips):
    @staticmethod
    def scratch(n):
        return [pltpu.SemaphoreType.DMA((7 * n,)), pltpu.SemaphoreType.DMA((7 * n,)), pltpu.SemaphoreType.DMA((n,))]

    @staticmethod
    def out_shape(block):
        return SDS((8,) + tuple(block.shape), block.dtype)

    @staticmethod
    def _copies(ins, outs, send_sems, recv_sems, local_sems, arrivals):
        x, y, c = _position()
        local, remote = [], []
        for i, (in_ref, out_ref) in enumerate(zip(ins, outs)):
            local.append(pltpu.make_async_copy(in_ref, out_ref.at[4 * x + 2 * y + c], local_sems.at[i]))
            for k, (fx, fy, fc) in enumerate(DEVICE_FLIPS):
                slot = 4 * (x ^ fx) + 2 * (y ^ fy) + (c ^ fc) if arrivals else 4 * x + 2 * y + c
                remote.append(pltpu.make_async_remote_copy(
                    src_ref=in_ref, dst_ref=out_ref.at[slot], send_sem=send_sems.at[7 * i + k],
                    recv_sem=recv_sems.at[7 * i + k], device_id=(x ^ fx, y ^ fy, c ^ fc), device_id_type=MESH))
        return local, remote


BIG = ("ffn1_w_gate", "ffn1_w_up", "ffn1_w_down", "w_in", "w_out", "ffn2_w_gate", "ffn2_w_up", "ffn2_w_down")
COL_SHARDED = ("ffn1_w_gate", "ffn1_w_up", "w_in", "ffn2_w_gate", "ffn2_w_up")
FIRST = tuple((n, 0) for n in ("ffn1_w_gate", "ffn1_w_up", "ffn1_w_down"))
LATE = tuple((n, 0) for n in ("w_out", "ffn2_w_gate", "ffn2_w_up", "ffn2_w_down")) + tuple((n, 1) for n in BIG)


def _to_shards(name, full):
    r, c = full.shape
    if name in COL_SHARDED:
        return full.reshape(r, 4, c // 4).transpose(1, 0, 2)
    return full.reshape(4, r // 4, c)


def _own_shard(name, full, chip):
    r, c = full.shape
    if name in COL_SHARDED:
        return lax.dynamic_slice_in_dim(full, chip * (c // 4), c // 4, axis=1)
    return lax.dynamic_slice_in_dim(full, chip * (r // 4), r // 4, axis=0)


def _from_shards(name, sh):
    _, r, c = sh.shape
    if name in COL_SHARDED:
        return sh.transpose(1, 0, 2).reshape(r, 4 * c)
    return sh.reshape(4 * r, c)


def _pad_w_in(w):
    return jnp.concatenate([w[:, :1792], w[:, 1800:2312], w[:, 1792:1800], jnp.zeros((w.shape[0], 248), w.dtype)], axis=1)


def _unpad_w_in(g):
    return jnp.concatenate([g[:, :1792], g[:, 2304:2312], g[:, 1792:2304]], axis=1)


def _block_diag(pw):
    out = jnp.zeros((256, 256), pw.dtype)
    for gidx in range(4):
        out = lax.dynamic_update_slice(out, pw[gidx], (64 * gidx, 64 * gidx))
    return out


SMALL = ("ffn1_norm", "mix_norm", "pool_w", "pool_scale", "forget_bias", "conv_b", "conv_ln_g", "conv_ln_b",
         "ffn2_norm", "final_norm")


def _grad_parts(grads, pieces):
    return [_to_shards(n, grads[n][l]).astype(MM) for n, l in pieces]


def _forward_backward(x, target, W, shards=None):
    T = x.shape[0]
    L = W["ffn1_norm"].shape[0]
    saved = []
    recv = {}
    for l in range(L):
        g1, gm, g2 = (W[n][l][None, :] for n in ("ffn1_norm", "mix_norm", "ffn2_norm"))
        first = shards is not None and l == 0
        hosted = (_GatherChips, [shards["w_in"][0], shards["conv_w"]]) if first else None
        x1, a1, b1, *got = _ffn_fwd(x, g1, W["ffn1_w_gate"][l], W["ffn1_w_up"][l], W["ffn1_w_down"][l], hosted=hosted)
        if first:
            W["w_in"][0] = _from_shards("w_in", got[0])
            W["conv_w"] = got[1].transpose(1, 2, 0, 3).reshape(L, CONV_K, 256)
        w_in = _pad_w_in(W["w_in"][l])
        up, q, k, v, ca, cg, zf = _mix_in_fwd(x1, gm, w_in)
        fb = jnp.pad(W["forget_bias"][l], (0, LANES - HEADS))[None, :]
        F = _fgate_fwd(zf, fb)
        if first:
            yb, lse, *got = _attn_fwd(q, k, v, F, hosted=(_GatherChipsSplit, [shards[n][ll] for n, ll in LATE]))
            for (n, ll), sh in zip(LATE, got):
                W[n][ll] = _from_shards(n, sh)
        else:
            yb, lse = _attn_fwd(q, k, v, F)
        bd = _block_diag(W["pool_w"][l]).astype(MM)
        ps, cb, lg, lb = (W[n][l][None, :] for n in ("pool_scale", "conv_b", "conv_ln_g", "conv_ln_b"))
        cw = jnp.pad(W["conv_w"][l], ((0, 1), (0, 0)))
        ya, yc, cu, cy = _local_fwd(up, ca, cg, bd, ps, cw, cb, lg, lb)
        x2 = _mix_out_fwd(x1, ya, yb, yc, W["w_out"][l])
        x3, a2, b2 = _ffn_fwd(x2, g2, W["ffn2_w_gate"][l], W["ffn2_w_up"][l], W["ffn2_w_down"][l])
        saved.append(dict(x0=x, x1=x1, x2=x2, ab1=(a1, b1), ab2=(a2, b2), w_in=w_in, up=up, ca=ca, cg=cg, zf=zf, fb=fb, F=F,
                          q=q, k=k, v=v, lse=lse, bd=bd, cw=cw, cu=cu, cy=cy, ya=ya, yb=yb, yc=yc))
        x = x3

    loss, dx, dgf = _head(x, W["final_norm"][None, :], target)
    grads = {n: [None] * L for n in W if n != "final_norm"}
    grads["final_norm"] = dgf[0]
    for l in reversed(range(L)):
        s = saved[l]
        g1, gm, g2 = (W[n][l][None, :] for n in ("ffn1_norm", "mix_norm", "ffn2_norm"))
        ps, lg, lb = (W[n][l][None, :] for n in ("pool_scale", "conv_ln_g", "conv_ln_b"))
        dx, h, dy, da, db, sact, dg = _ffn_bwd(s["x2"], dx, g2, *s["ab2"], W["ffn2_w_gate"][l], W["ffn2_w_up"][l],
                                               W["ffn2_w_down"][l])
        grads["ffn2_norm"][l] = dg[0]
        grads["ffn2_w_gate"][l] = _wgrad(h, da, "wgrad_gate")
        grads["ffn2_w_up"][l] = _wgrad(h, db, "wgrad_up")
        grads["ffn2_w_down"][l] = _wgrad(sact, dy, "wgrad_down")
        dya, dyb, dyc = _mix_out_bwd(dx, W["w_out"][l])
        grads["w_out"][l] = jnp.concatenate(
            [_wgrad(s["ya"], dx, "wgrad_out_a"), _wgrad(s["yb"], dx, "wgrad_out_b"), _wgrad(s["yc"], dx, "wgrad_out_c")], axis=0)
        first = shards is not None and l == 0
        if first:
            dq, dk, dv, dfq, dfk, *got = _attn_bwd(s["q"], s["k"], s["v"], s["F"], s["yb"], s["lse"], dyb,
                                                  hosted=(_ScatterChips, _grad_parts(grads, LATE)))
            recv.update(zip(LATE, got))
        else:
            dq, dk, dv, dfq, dfk = _attn_bwd(s["q"], s["k"], s["v"], s["F"], s["yb"], s["lse"], dyb)
        dfk_cols = jnp.pad(dfk.transpose(0, 2, 1, 3).reshape(HEADS, T).T, ((0, 0), (0, LANES - HEADS)))
        dzf, dfb = _fgate_bwd(s["zf"], s["fb"], dfq, dfk_cols)
        grads["forget_bias"][l] = dfb[0, :HEADS]
        dup, dca, dcg, dbd, dps, dcw, dcb, dlg, dlb = _local_bwd(
            s["up"], dya, s["ca"], s["cg"], s["cu"], s["cy"], dyc, s["bd"], ps, s["cw"], lg, lb)
        grads["pool_w"][l] = jnp.stack([dbd[64 * i:64 * i + 64, 64 * i:64 * i + 64] for i in range(4)])
        grads["pool_scale"][l], grads["conv_b"][l] = dps[0], dcb[0]
        grads["conv_ln_g"][l], grads["conv_ln_b"][l] = dlg[0], dlb[0]
        grads["conv_w"][l] = dcw[:CONV_K]
        dx, h, dp, dg = _mix_in_bwd(s["x1"], dx, gm, s["w_in"], dup, dq, dk, dv, dca, dcg, dzf)
        grads["mix_norm"][l] = dg[0]
        grads["w_in"][l] = _unpad_w_in(_wgrad(h, dp, "wgrad_in"))
        ffn1 = (W["ffn1_w_gate"][l], W["ffn1_w_up"][l], W["ffn1_w_down"][l])
        dx, h, dy, da, db, sact, dg = _ffn_bwd(s["x0"], dx, g1, *s["ab1"], *ffn1)
        if not first:
            grads["ffn1_w_gate"][l] = _wgrad(h, da, "wgrad_gate")
            grads["ffn1_w_up"][l] = _wgrad(h, db, "wgrad_up")
            grads["ffn1_w_down"][l] = _wgrad(sact, dy, "wgrad_down")
        else:
            scatter = lambda n: (_ScatterChips, _grad_parts(grads, [(n, 0)]))
            grads["ffn1_w_gate"][0], recv[("w_in", 0)] = _wgrad(h, da, "wgrad_gate", hosted=scatter("w_in"))
            grads["ffn1_w_up"][0], recv[("ffn1_w_gate", 0)] = _wgrad(h, db, "wgrad_up", hosted=scatter("ffn1_w_gate"))
            grads["ffn1_w_down"][0], recv[("ffn1_w_up", 0)] = _wgrad(sact, dy, "wgrad_down", hosted=scatter("ffn1_w_up"))
            recv[("ffn1_w_down", 0)] = _exchange(*scatter("ffn1_w_down"), "scatter_last_grad")[0]
        grads["ffn1_norm"][l] = dg[0]
    grads = {n: (jnp.stack(g) if isinstance(g, list) and n not in BIG else g) for n, g in grads.items()}
    return loss, dx, grads, recv


NAMES = ("ffn1_norm", "ffn1_w_gate", "ffn1_w_up", "ffn1_w_down", "mix_norm", "w_in", "pool_w", "pool_scale",
         "forget_bias", "conv_w", "conv_b", "conv_ln_g", "conv_ln_b", "w_out", "ffn2_norm", "ffn2_w_gate",
         "ffn2_w_up", "ffn2_w_down", "final_norm")


def kernel(x, ffn1_norm, ffn1_w_gate, ffn1_w_up, ffn1_w_down, mix_norm, w_in, pool_w, pool_scale, forget_bias, conv_w, conv_b, conv_ln_g, conv_ln_b, w_out, ffn2_norm, ffn2_w_gate, ffn2_w_up, ffn2_w_down, final_norm, loss_target, m_ffn1_norm, m_ffn1_w_gate, m_ffn1_w_up, m_ffn1_w_down, m_mix_norm, m_w_in, m_pool_w, m_pool_scale, m_forget_bias, m_conv_w, m_conv_b, m_conv_ln_g, m_conv_ln_b, m_w_out, m_ffn2_norm, m_ffn2_w_gate, m_ffn2_w_up, m_ffn2_w_down, m_final_norm, v_ffn1_norm, v_ffn1_w_gate, v_ffn1_w_up, v_ffn1_w_down, v_mix_norm, v_w_in, v_pool_w, v_pool_scale, v_forget_bias, v_conv_w, v_conv_b, v_conv_ln_g, v_conv_ln_b, v_w_out, v_ffn2_norm, v_ffn2_w_gate, v_ffn2_w_up, v_ffn2_w_down, v_final_norm):
    args = (ffn1_norm, ffn1_w_gate, ffn1_w_up, ffn1_w_down, mix_norm, w_in, pool_w, pool_scale, forget_bias, conv_w, conv_b, conv_ln_g, conv_ln_b, w_out, ffn2_norm, ffn2_w_gate, ffn2_w_up, ffn2_w_down, final_norm)
    ms = (m_ffn1_norm, m_ffn1_w_gate, m_ffn1_w_up, m_ffn1_w_down, m_mix_norm, m_w_in, m_pool_w, m_pool_scale, m_forget_bias, m_conv_w, m_conv_b, m_conv_ln_g, m_conv_ln_b, m_w_out, m_ffn2_norm, m_ffn2_w_gate, m_ffn2_w_up, m_ffn2_w_down, m_final_norm)
    vs = (v_ffn1_norm, v_ffn1_w_gate, v_ffn1_w_up, v_ffn1_w_down, v_mix_norm, v_w_in, v_pool_w, v_pool_scale, v_forget_bias, v_conv_w, v_conv_b, v_conv_ln_g, v_conv_ln_b, v_w_out, v_ffn2_norm, v_ffn2_w_gate, v_ffn2_w_up, v_ffn2_w_down, v_final_norm)
    P = dict(zip(NAMES, args))
    M = dict(zip(NAMES, ms))
    V = dict(zip(NAMES, vs))
    xi, yi, _ = _position()
    chip = 2 * xi + yi

    W = {n: P[n] for n in SMALL}
    W.update({n: [None] * P[n].shape[0] for n in BIG})
    shards = {n: [P[n][l].astype(MM) for l in range(P[n].shape[0])] for n in BIG}
    shards["conv_w"] = P["conv_w"]
    for (n, l), sh in zip(FIRST, _exchange(_GatherChipsSplit, [shards[n][l] for n, l in FIRST], "gather_first_weights")):
        W[n][l] = _from_shards(n, sh)

    loss_part, dx, G, recv = _forward_backward(x[0], loss_target[0], W, shards)
    loss = lax.psum(loss_part[0, 0], ("x", "y", "c"))

    res = {}
    gathered = _exchange(_GatherDevices, [G[n] for n in SMALL] + [G["conv_w"]], "gather_small_grads")
    for n, g8 in zip(SMALL, gathered):
        shp = P[n].shape
        two_d = (math.prod(shp[:-1]), shp[-1])
        outs = _adamw(P[n].reshape(two_d), g8.reshape((8,) + two_d), M[n].reshape(two_d), V[n].reshape(two_d),
                      "adamw_" + n)
        for kind, a in zip(("g", "d", "m", "v"), outs):
            res[(kind, n)] = a.reshape(shp)
    shp = P["conv_w"].shape
    g_cw_full = _sum8(gathered[-1].reshape(8, shp[0] * CONV_K, 256), "sum_conv_w_grads")
    g_cw = lax.dynamic_slice_in_dim(g_cw_full, chip * shp[2], shp[2], axis=1)
    two_d = (shp[0] * CONV_K, shp[2])
    outs = _adamw(P["conv_w"].reshape(two_d), [g_cw], M["conv_w"].reshape(two_d), V["conv_w"].reshape(two_d),
                  "adamw_conv_w")
    for kind, a in zip(("g", "d", "m", "v"), outs):
        res[(kind, "conv_w")] = a.reshape(shp)

    parts = [_sum_parts([_own_shard(n, G[n][l], chip) for l in range(P[n].shape[0])],
                        [recv[(n, l)] for l in range(P[n].shape[0])], "sum_" + n) for n in BIG]
    others = _exchange(_SwapCores, parts, "swap_core_grads")
    for n, ga, gb in zip(BIG, parts, others):
        shp = P[n].shape
        two_d = (shp[0] * shp[1], shp[2])
        outs = _adamw(P[n].reshape(two_d), [ga.reshape(two_d), gb.reshape(two_d)], M[n].reshape(two_d),
                      V[n].reshape(two_d), "adamw_" + n)
        for kind, a in zip(("g", "d", "m", "v"), outs):
            res[(kind, n)] = a.reshape(shp)

    return (loss, dx[None], *[res[("g", n)] for n in NAMES], *[res[("d", n)] for n in NAMES],
            *[res[("m", n)] for n in NAMES], *[res[("v", n)] for n in NAMES])
```

```python
import math

import jax
import jax.numpy as jnp
from jax import lax
from jax.experimental import pallas as pl
from jax.experimental.pallas import tpu as pltpu

F32 = jnp.float32
MM = jnp.bfloat16
NORM_EPS = 1e-6
HEADS = 8
HEAD_DIM = 64
POOL_WINDOWS = (2, 4, 8, 16)
CONV_K = 31
LANES = 128
VMEM_LIMIT = 56 * 2**20
FFN_BWD_ROWS = 256
FFN_COLS = 768
ATTN_FWD_TILE = 1024
ATTN_BWD_ROWS, ATTN_BWD_COLS = 1024, 512
WGRAD_COLS = 768
WGRAD_ACC_BYTES = 12 * 2**20

ADAM_LR = 0.001
ADAM_B1 = 0.9
ADAM_B2 = 0.999
ADAM_EPS = 1e-08
ADAM_WD = 0.01
ADAM_STEP = 10

MESH = pl.DeviceIdType.MESH
BS = pl.BlockSpec
SDS = jax.ShapeDtypeStruct
ANY = pl.BlockSpec(memory_space=pl.ANY)


def _dot(a, b):
    return jnp.dot(a, b, preferred_element_type=F32)


def _dot_nt(a, b):
    return lax.dot_general(a, b, (((1,), (1,)), ((), ())), preferred_element_type=F32)


def _dot_tn(a, b):
    return lax.dot_general(a, b, (((0,), (0,)), ((), ())), preferred_element_type=F32)


def _pc(body, name, grid, in_specs, out_specs, out_shape, scratch=()):
    return pl.pallas_call(
        body, out_shape=out_shape, grid=grid, in_specs=in_specs, out_specs=out_specs,
        scratch_shapes=list(scratch), name=name,
        compiler_params=pltpu.CompilerParams(
            dimension_semantics=("arbitrary",) * len(grid), vmem_limit_bytes=VMEM_LIMIT))


def _rms_fwd(x, g):
    r = lax.rsqrt(jnp.mean(x * x, axis=-1, keepdims=True) + NORM_EPS)
    xh = x * r
    return xh, r, xh * g


def _rms_bwd(dh, xh, r, g):
    dxh = dh * g
    dx = r * (dxh - xh * jnp.mean(dxh * xh, axis=-1, keepdims=True))
    return dx, jnp.sum(dh * xh, axis=0, keepdims=True)


def _sigmoid(x):
    return jax.nn.sigmoid(x)


def _ffn_fwd(x, g, wg, wu, wd, hosted=None):
    T, D = x.shape
    F = wg.shape[1]
    tm = min(512, T)
    nt = T // tm
    pieces = [(c0, min(FFN_COLS, F - c0)) for c0 in range(0, F, FFN_COLS)]
    h_in, h_out, h_shape, h_scratch = _hosted_specs(hosted)

    def body(*refs):
        i = pl.program_id(0)
        refs, finish = _hosted_edges(hosted, refs, 5, 3, i == 0, i == nt - 1)
        x_ref, g_ref, wg_ref, wu_ref, wd_ref, o_ref, a_ref, b_ref = refs
        xv = x_ref[...]
        h = _rms_fwd(xv, g_ref[...])[2].astype(MM)
        acc = jnp.zeros((tm, D), F32)
        for c0, w in pieces:
            a = _dot(h, wg_ref[:, c0:c0 + w])
            b = _dot(h, wu_ref[:, c0:c0 + w])
            a_ref[:, c0:c0 + w] = a.astype(a_ref.dtype)
            b_ref[:, c0:c0 + w] = b.astype(b_ref.dtype)
            acc = acc + _dot(((a * _sigmoid(a)) * b).astype(MM), wd_ref[c0:c0 + w, :])
        o_ref[...] = xv + 0.5 * acc
        finish()

    tok = lambda i: (i, 0)
    par = lambda i: (0, 0)
    resident = lambda shape: BS(shape, par, pipeline_mode=pl.Buffered(1))
    return _pc(
        body, "ffn_fwd" + ("_hosting" if hosted else ""), (nt,),
        [BS((tm, D), tok), BS((1, D), par), resident((D, F)), resident((D, F)), resident((F, D))] + h_in,
        [BS((tm, D), tok), BS((tm, F), tok), BS((tm, F), tok)] + h_out,
        [SDS((T, D), F32), SDS((T, F), MM), SDS((T, F), MM)] + h_shape,
        scratch=h_scratch)(x, g, wg, wu, wd, *(hosted[1] if hosted else []))


def _ffn_bwd(x, dout, g, a, b, wg, wu, wd, hosted=None):
    T, D = x.shape
    F = wg.shape[1]
    tm = min(FFN_BWD_ROWS, T)
    nt = T // tm
    pieces = [(c0, min(FFN_COLS, F - c0)) for c0 in range(0, F, FFN_COLS)]
    h_in, h_out, h_shape, h_scratch = _hosted_specs(hosted)

    def body(*refs):
        i = pl.program_id(0)
        refs, finish = _hosted_edges(hosted, refs, 8, 7, i == 0, i == nt - 1)
        (x_ref, do_ref, g_ref, a_ref, b_ref, wg_ref, wu_ref, wd_ref,
         dx_ref, h_ref, dy_ref, da_ref, db_ref, s_ref, dg_ref) = refs

        @pl.when(i == 0)
        def _():
            dg_ref[...] = jnp.zeros_like(dg_ref)

        gv = g_ref[...]
        xh, r, hg = _rms_fwd(x_ref[...], gv)
        h_ref[...] = hg.astype(h_ref.dtype)
        dy = (0.5 * do_ref[...]).astype(MM)
        dy_ref[...] = dy
        dh = jnp.zeros((tm, D), F32)
        for c0, w in pieces:
            a = a_ref[:, c0:c0 + w].astype(F32)
            b = b_ref[:, c0:c0 + w].astype(F32)
            ds = _dot_nt(dy, wd_ref[c0:c0 + w, :])
            sig = _sigmoid(a)
            sl = a * sig
            s_ref[:, c0:c0 + w] = (sl * b).astype(s_ref.dtype)
            db = (ds * sl).astype(MM)
            da = (ds * b * (sig * (1.0 + a * (1.0 - sig)))).astype(MM)
            da_ref[:, c0:c0 + w] = da
            db_ref[:, c0:c0 + w] = db
            dh = dh + _dot_nt(da, wg_ref[:, c0:c0 + w]) + _dot_nt(db, wu_ref[:, c0:c0 + w])
        dx, dg = _rms_bwd(dh, xh, r, gv)
        dx_ref[...] = do_ref[...] + dx
        dg_ref[...] += dg
        finish()

    tok = lambda i: (i, 0)
    par = lambda i: (0, 0)
    hid = BS((tm, F), tok)
    resident = lambda shape: BS(shape, par, pipeline_mode=pl.Buffered(1))
    return _pc(
        body, "ffn_bwd" + ("_hosting" if hosted else ""), (nt,),
        [BS((tm, D), tok), BS((tm, D), tok), BS((1, D), par), hid, hid,
         resident((D, F)), resident((D, F)), resident((F, D))] + h_in,
        [BS((tm, D), tok), BS((tm, D), tok), BS((tm, D), tok), hid, hid, hid, BS((1, D), par)] + h_out,
        [SDS((T, D), F32), SDS((T, D), MM), SDS((T, D), MM),
         SDS((T, F), MM), SDS((T, F), MM), SDS((T, F), MM), SDS((1, D), F32)] + h_shape,
        scratch=h_scratch,
    )(x, dout, g, a, b, wg, wu, wd, *(hosted[1] if hosted else []))


def _wgrad(a, b, name, hosted=None):
    T, K = a.shape
    N = b.shape[1]
    tt = min(512, T)
    tn = next(c for c in (N, 1408, 1280, 1024, 512, 256, 128) if N % c == 0 and K * c * 4 <= WGRAD_ACC_BYTES)
    pieces = [(c0, min(WGRAD_COLS, tn - c0)) for c0 in range(0, tn, WGRAD_COLS)]
    nn, nt = N // tn, T // tt
    h_in, h_out, h_shape, h_scratch = _hosted_specs(hosted)

    def body(*refs):
        n, t = pl.program_id(0), pl.program_id(1)
        refs, finish = _hosted_edges(hosted, refs, 2, 1, (n == 0) & (t == 0), (n == nn - 1) & (t == nt - 1))
        a_ref, b_ref, o_ref = refs

        @pl.when(t == 0)
        def _():
            o_ref[...] = jnp.zeros_like(o_ref)

        av = a_ref[...].astype(MM)
        for c0, w in pieces:
            o_ref[:, c0:c0 + w] += _dot_tn(av, b_ref[:, c0:c0 + w].astype(MM))
        finish()

    res = _pc(
        body, name + ("_hosting" if hosted else ""), (nn, nt),
        [BS((tt, K), lambda n, t: (t, 0)), BS((tt, tn), lambda n, t: (t, n))] + h_in,
        [BS((K, tn), lambda n, t: (0, n))] + h_out, [SDS((K, N), F32)] + h_shape,
        scratch=h_scratch)(a, b, *(hosted[1] if hosted else []))
    return res if hosted else res[0]


C_POOL, C_Q, C_K, C_V, C_CA, C_CG, C_ZF, C_END = 0, 256, 768, 1280, 1792, 2048, 2304, 2560


def _mix_in_fwd(x, g, w):
    T, D = x.shape
    tm = min(512, T)

    def body(x_ref, g_ref, w_ref, up_ref, q_ref, k_ref, v_ref, ca_ref, cg_ref, zf_ref):
        _, _, hg = _rms_fwd(x_ref[...], g_ref[...])
        p = _dot(hg.astype(MM), w_ref[...])
        up_ref[...] = p[:, C_POOL:C_Q]
        q_ref[...] = p[:, C_Q:C_K].astype(q_ref.dtype)
        k_ref[...] = p[:, C_K:C_V].astype(k_ref.dtype)
        v_ref[...] = p[:, C_V:C_CA].astype(v_ref.dtype)
        ca_ref[...] = p[:, C_CA:C_CG]
        cg_ref[...] = p[:, C_CG:C_ZF]
        zf_ref[...] = p[:, C_ZF:C_ZF + LANES]

    tok = lambda i: (i, 0)
    widths = (256, 512, 512, 512, 256, 256, 128)
    dtypes = (F32, MM, MM, MM, F32, F32, F32)
    return _pc(
        body, "mix_in_fwd", (T // tm,),
        [BS((tm, D), tok), BS((1, D), lambda i: (0, 0)), BS((D, C_END), lambda i: (0, 0))],
        [BS((tm, wd), tok) for wd in widths],
        [SDS((T, wd), dt) for wd, dt in zip(widths, dtypes)])(x, g, w)


def _mix_in_bwd(x, dout, g, w, dup, dq, dk, dv, dca, dcg, dzf):
    T, D = x.shape
    tm = min(512, T)

    def body(x_ref, do_ref, g_ref, w_ref, dup_ref, dq_ref, dk_ref, dv_ref, dca_ref, dcg_ref, dzf_ref,
             dx_ref, h_ref, dp_ref, dg_ref):
        @pl.when(pl.program_id(0) == 0)
        def _():
            dg_ref[...] = jnp.zeros_like(dg_ref)

        gv = g_ref[...]
        xh, r, hg = _rms_fwd(x_ref[...], gv)
        h_ref[...] = hg.astype(h_ref.dtype)
        for ref, lo, hi in ((dup_ref, C_POOL, C_Q), (dq_ref, C_Q, C_K), (dk_ref, C_K, C_V), (dv_ref, C_V, C_CA),
                            (dca_ref, C_CA, C_CG), (dcg_ref, C_CG, C_ZF), (dzf_ref, C_ZF, C_ZF + LANES)):
            dp_ref[:, lo:hi] = ref[...].astype(dp_ref.dtype)
        dp_ref[:, C_ZF + LANES:C_END] = jnp.zeros((tm, C_END - C_ZF - LANES), dp_ref.dtype)
        dh = _dot_nt(dp_ref[...], w_ref[...])
        dx, dg = _rms_bwd(dh, xh, r, gv)
        dx_ref[...] = do_ref[...] + dx
        dg_ref[...] += dg

    tok = lambda i: (i, 0)
    widths = (256, 512, 512, 512, 256, 256, 128)
    return _pc(
        body, "mix_in_bwd", (T // tm,),
        [BS((tm, D), tok), BS((tm, D), tok), BS((1, D), lambda i: (0, 0)), BS((D, C_END), lambda i: (0, 0))]
        + [BS((tm, wd), tok) for wd in widths],
        [BS((tm, D), tok), BS((tm, D), tok), BS((tm, C_END), tok), BS((1, D), lambda i: (0, 0))],
        [SDS((T, D), F32), SDS((T, D), MM), SDS((T, C_END), MM), SDS((1, D), F32)],
    )(x, dout, g, w, dup, dq, dk, dv, dca, dcg, dzf)


def _mix_out_fwd(x, ya, yb, yc, wo):
    T, D = x.shape
    tm = min(512, T)

    def body(x_ref, ya_ref, yb_ref, yc_ref, wo_ref, o_ref):
        o_ref[...] = (x_ref[...] + _dot(ya_ref[...].astype(MM), wo_ref[0:256, :])
                      + _dot(yb_ref[...].astype(MM), wo_ref[256:768, :])
                      + _dot(yc_ref[...].astype(MM), wo_ref[768:1024, :]))

    tok = lambda i: (i, 0)
    return _pc(
        body, "mix_out_fwd", (T // tm,),
        [BS((tm, D), tok), BS((tm, 256), tok), BS((tm, 512), tok), BS((tm, 256), tok), BS((D, D), lambda i: (0, 0))],
        BS((tm, D), tok), SDS((T, D), F32))(x, ya, yb, yc, wo)


def _mix_out_bwd(dx, wo):
    T, D = dx.shape
    tm = min(512, T)

    def body(dx_ref, wo_ref, dya_ref, dyb_ref, dyc_ref):
        dy = _dot_nt(dx_ref[...].astype(MM), wo_ref[...])
        dya_ref[...] = dy[:, 0:256]
        dyb_ref[...] = dy[:, 256:768]
        dyc_ref[...] = dy[:, 768:1024]

    tok = lambda i: (i, 0)
    return _pc(
        body, "mix_out_bwd", (T // tm,),
        [BS((tm, D), tok), BS((D, D), lambda i: (0, 0))],
        [BS((tm, 256), tok), BS((tm, 512), tok), BS((tm, 256), tok)],
        [SDS((T, 256), F32), SDS((T, 512), F32), SDS((T, 256), F32)])(dx, wo)


def _fgate_fwd(zf, bias):
    T = zf.shape[0]
    tc = min(256, T)

    def body(z_ref, b_ref, f_ref, carry):
        @pl.when(pl.program_id(0) == 0)
        def _():
            carry[...] = jnp.zeros_like(carry)

        z = z_ref[...] + b_ref[...]
        logf = jnp.minimum(z, 0.0) - jnp.log(1.0 + jnp.exp(-jnp.abs(z)))
        row = lax.broadcasted_iota(jnp.int32, (tc, tc), 0)
        col = lax.broadcasted_iota(jnp.int32, (tc, tc), 1)
        tri = (col <= row).astype(F32)
        f_ref[...] = jnp.dot(tri, logf, precision=lax.Precision.HIGHEST, preferred_element_type=F32) + carry[...]
        carry[...] += jnp.sum(logf, axis=0, keepdims=True)

    return _pc(
        body, "fgate_fwd", (T // tc,),
        [BS((tc, LANES), lambda i: (i, 0)), BS((1, LANES), lambda i: (0, 0))],
        BS((tc, LANES), lambda i: (i, 0)), SDS((T, LANES), F32),
        scratch=[pltpu.VMEM((1, LANES), F32)])(zf, bias)


def _fgate_bwd(zf, bias, dFq, dFk):
    T = zf.shape[0]
    tc = min(256, T)
    n = T // tc
    slabs = dFq.shape[0]

    def body(z_ref, b_ref, dfq_ref, dfk_ref, dz_ref, db_ref, carry):
        @pl.when(pl.program_id(0) == 0)
        def _():
            carry[...] = jnp.zeros_like(carry)
            db_ref[...] = jnp.zeros_like(db_ref)

        df = dfk_ref[...]
        for sl in range(slabs):
            df = df + dfq_ref[sl]
        row = lax.broadcasted_iota(jnp.int32, (tc, tc), 0)
        col = lax.broadcasted_iota(jnp.int32, (tc, tc), 1)
        tri = (col >= row).astype(F32)
        dlogf = jnp.dot(tri, df, precision=lax.Precision.HIGHEST, preferred_element_type=F32) + carry[...]
        carry[...] += jnp.sum(df, axis=0, keepdims=True)
        lane = lax.broadcasted_iota(jnp.int32, (1, LANES), 1)
        dz = jnp.where(lane < HEADS, dlogf * _sigmoid(-(z_ref[...] + b_ref[...])), 0.0)
        dz_ref[...] = dz
        db_ref[...] += jnp.sum(dz, axis=0, keepdims=True)

    rev = lambda i: (n - 1 - i, 0)
    return _pc(
        body, "fgate_bwd", (n,),
        [BS((tc, LANES), rev), BS((1, LANES), lambda i: (0, 0)), BS((slabs, tc, LANES), lambda i: (0, n - 1 - i, 0)),
         BS((tc, LANES), rev)],
        [BS((tc, LANES), rev), BS((1, LANES), lambda i: (0, 0))],
        [SDS((T, LANES), F32), SDS((1, LANES), F32)],
        scratch=[pltpu.VMEM((1, LANES), F32)])(zf, bias, dFq, dFk)


LOG2E = 1.4426950408889634


def _split3(x):
    hi = x.astype(MM)
    r1 = x - hi.astype(F32)
    mid = r1.astype(MM)
    return hi, mid, (r1 - mid.astype(F32)).astype(MM)


def _place(lane, base, cols):
    out = jnp.zeros((cols[0].shape[0], LANES), MM)
    for i, c in enumerate(cols):
        out = jnp.where(lane == base + i, c, out)
    return out


def _head_col(block, lane, h):
    return jnp.sum(jnp.where(lane == h, block, 0.0), axis=-1, keepdims=True)


def _own_lanes(lane, hh):
    return (lane < HEAD_DIM) if hh == 0 else (lane >= HEAD_DIM)


def _attn_k_side(k_ref, f_ref, kb_ref, hp, T, rows, lse_ones, v_ref=None, vb_ref=None):
    lane = lax.broadcasted_iota(jnp.int32, (1, LANES), 1)
    one = jnp.ones((rows, 1), MM)

    def chunk(c, _):
        r0 = pl.multiple_of(c * rows, rows)
        kp = k_ref[pl.ds(r0, rows), :]
        fblk = f_ref[pl.ds(r0, rows), :]
        for hh in range(2):
            hi, mid, lo = _split3(-_head_col(fblk, lane, 2 * hp + hh) * LOG2E)
            cols = [one, one, one, hi, mid, lo] + ([one, one, one] if lse_ones else [])
            bias = _place(lane, HEAD_DIM * (1 - hh), cols)
            kb_ref[hh, pl.ds(r0, rows), :] = jnp.where(_own_lanes(lane, hh), kp, bias)
            if vb_ref is not None:
                vb_ref[hh, pl.ds(r0, rows), :] = jnp.where(_own_lanes(lane, hh), v_ref[pl.ds(r0, rows), :],
                                                           jnp.ones((rows, LANES), MM))
        return 0

    lax.fori_loop(0, T // rows, chunk, 0)


def _attn_q_side(qp, fblk, lane, hp, scale, lse_blk=None):
    qc = qp.astype(F32) * (scale * LOG2E)
    qhi = qc.astype(MM)
    qlo = (qc - qhi.astype(F32)).astype(MM)
    one = jnp.ones((qp.shape[0], 1), MM)
    out = []
    for hh in range(2):
        cols = list(_split3(_head_col(fblk, lane, 2 * hp + hh) * LOG2E)) + [one, one, one]
        if lse_blk is not None:
            cols += list(_split3(-_head_col(lse_blk, lane, 2 * hp + hh)))
        bias = _place(lane, HEAD_DIM * (1 - hh), cols)
        own = _own_lanes(lane, hh)
        out.append(jnp.concatenate([jnp.where(own, qhi, jnp.zeros_like(qhi)), jnp.where(own, qlo, bias)], axis=1))
    return out


def _causal(tq, tk, col0=0):
    return lax.broadcasted_iota(jnp.int32, (tq, tk), 1) + col0 <= lax.broadcasted_iota(jnp.int32, (tq, tk), 0)


def _hosted_specs(hosted):
    if hosted is None:
        return [], [], [], []
    kind, arrays = hosted
    n = len(arrays)
    return [ANY] * n, [ANY] * n, [kind.out_shape(a) for a in arrays], kind.scratch(n)


def _hosted_edges(hosted, refs, n_in, n_out, first, last, mid=None):
    if hosted is None:
        return refs, lambda: None
    kind, arrays = hosted
    n = len(arrays)
    nsem = len(kind.scratch(n))
    o0 = n_in + n + n_out
    ins, outs, sems = refs[n_in:n_in + n], refs[o0:o0 + n], refs[len(refs) - nsem:]
    relayed = mid is not None and hasattr(kind, "relay")

    @pl.when(first)
    def _():
        kind.start(ins, outs, *sems)

    if relayed:
        @pl.when(mid)
        def _():
            kind.relay(ins, outs, *sems)

    def finish():
        @pl.when(last)
        def _():
            kind.wait(ins, outs, *sems, **({"relayed": True} if relayed else {}))

    return refs[:n_in] + refs[n_in + n:o0] + refs[o0 + n:len(refs) - nsem], finish


def _attn_fwd(q, k, v, F, hosted=None):
    T = q.shape[0]
    tq = min(ATTN_FWD_TILE, T)
    tk = tq
    nq = T // tq
    scale = 1.0 / math.sqrt(HEAD_DIM)
    h_in, h_out, h_shape, h_scratch = _hosted_specs(hosted)

    def body(*refs):
        hp, ib = pl.program_id(0), pl.program_id(1)
        refs, finish = _hosted_edges(hosted, refs, 5, 2, (hp == 0) & (ib == 0), (hp == HEADS // 2 - 1) & (ib == nq - 1),
                                     mid=(hp == HEADS // 2 - 1) & (ib == 0))
        q_ref, k_ref, v_ref, fq_ref, f_ref, o_ref, lse_ref, kb_ref, vb_ref = refs
        lane = lax.broadcasted_iota(jnp.int32, (1, LANES), 1)

        @pl.when(ib == 0)
        def _():
            _attn_k_side(k_ref, f_ref, kb_ref, hp, T, min(512, T), False, v_ref, vb_ref)

        qa = _attn_q_side(q_ref[...], fq_ref[...], lane, hp, scale)

        def tile(jb, carry, masked):
            off = pl.multiple_of(jb * tk, tk)
            kp = k_ref[pl.ds(off, tk), :]
            new = []
            for hh in range(2):
                m, acc = carry[hh]
                s = _dot_nt(qa[hh], jnp.concatenate([kp, kb_ref[hh, pl.ds(off, tk), :]], axis=1))
                if masked:
                    s = jnp.where(_causal(tq, tk), s, -jnp.inf)
                m2 = jnp.maximum(m, jnp.max(s, axis=-1, keepdims=True))
                p = jnp.exp2(s - m2)
                new.append((m2, acc * jnp.exp2(m - m2) + _dot(p.astype(MM), vb_ref[hh, pl.ds(off, tk), :])))
            return tuple(new)

        init = tuple((jnp.full((tq, 1), -jnp.inf, F32), jnp.zeros((tq, LANES), F32)) for _ in range(2))
        carry = lax.fori_loop(0, ib, lambda jb, c: tile(jb, c, False), init)
        (m0, a0), (m1, a1) = tile(ib, carry, True)
        l0, l1 = a0[:, HEAD_DIM:HEAD_DIM + 1], a1[:, 0:1]
        o_ref[...] = jnp.where(lane < HEAD_DIM, a0 / l0, a1 / l1)
        lse_ref[...] = jnp.where(lane == 2 * hp, m0 + jnp.log2(l0), jnp.where(lane == 2 * hp + 1, m1 + jnp.log2(l1), 0.0))
        finish()

    blk = lambda h, i: (i, h)
    full = lambda h, i: (0, h)
    return _pc(
        body, "attn_fwd" + ("_hosting" if hosted else ""), (HEADS // 2, nq),
        [BS((tq, LANES), blk), BS((T, LANES), full), BS((T, LANES), full), BS((tq, LANES), lambda h, i: (i, 0)),
         BS((T, LANES), lambda h, i: (0, 0))] + h_in,
        [BS((tq, LANES), blk), BS((None, tq, LANES), lambda h, i: (h, i, 0))] + h_out,
        [SDS((T, HEADS * HEAD_DIM), F32), SDS((HEADS // 2, T, LANES), F32)] + h_shape,
        scratch=[pltpu.VMEM((2, T, LANES), MM)] * 2 + h_scratch)(q, k, v, F, F, *(hosted[1] if hosted else []))


def _attn_bwd(q, k, v, F, o, lse, do, hosted=None):
    T = q.shape[0]
    tq, tk = min(ATTN_BWD_ROWS, T), min(ATTN_BWD_COLS, T)
    nq, nk, per = T // tq, T // tk, tq // tk
    scale = 1.0 / math.sqrt(HEAD_DIM)
    h_in, h_out, h_shape, h_scratch = _hosted_specs(hosted)

    def body(*refs):
        hp, ib = pl.program_id(0), pl.program_id(1)
        refs, finish = _hosted_edges(hosted, refs, 8, 5, (hp == 0) & (ib == 0), (hp == HEADS // 2 - 1) & (ib == nq - 1))
        (q_ref, k_ref, v_ref, fq_ref, f_ref, o_ref, lse_ref, do_ref,
         dq_ref, dk_ref, dv_ref, dfq_ref, dfk_ref, kb_ref, dk_acc, dv_acc) = refs
        lane = lax.broadcasted_iota(jnp.int32, (1, LANES), 1)

        @pl.when(ib == 0)
        def _():
            _attn_k_side(k_ref, f_ref, kb_ref, hp, T, tk, True)
            dk_acc[...] = jnp.zeros_like(dk_acc)
            dv_acc[...] = jnp.zeros_like(dv_acc)
            dfk_ref[...] = jnp.zeros_like(dfk_ref)

        qp = q_ref[...]
        qa = _attn_q_side(qp, fq_ref[...], lane, hp, scale, lse_ref[...])
        dob = do_ref[...].astype(MM)
        dprod = dob.astype(F32) * o_ref[...]
        qs = (qp.astype(F32) * scale).astype(MM)
        heads = []
        for hh in range(2):
            own = _own_lanes(lane, hh)
            heads.append((jnp.where(own, dob, jnp.zeros_like(dob)), jnp.where(own, qs, jnp.zeros_like(qs)),
                          jnp.sum(jnp.where(own, dprod, 0.0), axis=-1, keepdims=True)))

        def tile(jb, carry, col0=None):
            off = pl.multiple_of(jb * tk, tk)
            kp = k_ref[pl.ds(off, tk), :]
            vp = v_ref[pl.ds(off, tk), :]
            new = []
            dv_t = jnp.zeros((tk, LANES), F32)
            dk_t = jnp.zeros((tk, LANES), F32)
            for hh in range(2):
                dq, rs = carry[hh]
                dom, qm, delta = heads[hh]
                p = jnp.exp2(_dot_nt(qa[hh], jnp.concatenate([kp, kb_ref[hh, pl.ds(off, tk), :]], axis=1)))
                if col0 is not None:
                    p = jnp.where(_causal(tq, tk, col0), p, 0.0)
                ds = p * (_dot_nt(dom, vp) - delta)
                dsb = ds.astype(MM)
                dv_t = dv_t + _dot_tn(p.astype(MM), dom)
                dk_t = dk_t + _dot_tn(dsb, qm)
                dfk_ref[jb, pl.ds(hh, 1), :] -= jnp.sum(ds, axis=0, keepdims=True)
                new.append((dq + _dot(dsb, kp), rs + jnp.sum(ds, axis=-1, keepdims=True)))
            dv_acc[pl.ds(off, tk), :] += dv_t
            dk_acc[pl.ds(off, tk), :] += dk_t
            return tuple(new)

        carry = tuple((jnp.zeros((tq, LANES), F32), jnp.zeros((tq, 1), F32)) for _ in range(2))
        carry = lax.fori_loop(0, ib * per, lambda jb, c: tile(jb, c), carry)
        for u in range(per):
            carry = tile(ib * per + u, carry, u * tk)
        (dq0, rs0), (dq1, rs1) = carry
        dq_ref[...] = (jnp.where(lane < HEAD_DIM, dq0, dq1) * scale).astype(dq_ref.dtype)
        dfq_ref[...] = jnp.where(lane == 2 * hp, rs0, jnp.where(lane == 2 * hp + 1, rs1, 0.0))

        @pl.when(ib == nq - 1)
        def _():
            dk_ref[...] = dk_acc[...].astype(dk_ref.dtype)
            dv_ref[...] = dv_acc[...].astype(dv_ref.dtype)

        finish()

    blk = lambda h, i: (i, h)
    full = lambda h, i: (0, h)
    slab = BS((None, tq, LANES), lambda h, i: (h, i, 0))
    once = pl.Buffered(1)
    return _pc(
        body, "attn_bwd" + ("_hosting" if hosted else ""), (HEADS // 2, nq),
        [BS((tq, LANES), blk), BS((T, LANES), full, pipeline_mode=once), BS((T, LANES), full, pipeline_mode=once),
         BS((tq, LANES), lambda h, i: (i, 0)), BS((T, LANES), lambda h, i: (0, 0), pipeline_mode=once),
         BS((tq, LANES), blk), slab, BS((tq, LANES), blk)] + h_in,
        [BS((tq, LANES), blk), BS((T, LANES), full), BS((T, LANES), full), slab,
         BS((None, nk, 2, tk), lambda h, i: (h, 0, 0, 0))] + h_out,
        [SDS((T, HEADS * HEAD_DIM), MM)] * 3 + [SDS((HEADS // 2, T, LANES), F32), SDS((HEADS // 2, nk, 2, tk), F32)]
        + h_shape,
        scratch=[pltpu.VMEM((2, T, LANES), MM), pltpu.VMEM((T, LANES), F32), pltpu.VMEM((T, LANES), F32)] + h_scratch,
    )(q, k, v, F, F, o, lse, do, *(hosted[1] if hosted else []))


POOL_HALO = 16
CONV_HALO = 32


def _group_select(lane, v0, v1, v2, v3):
    return jnp.where(lane < 64, v0, jnp.where(lane < 128, v1, jnp.where(lane < 192, v2, v3)))


def _roll_down(x, k):
    return x if k == 0 else pltpu.roll(x, k, 0)


def _roll_up(x, k):
    return x if k == 0 else pltpu.roll(x, x.shape[0] - k, 0)


def _pool_terms(u, u_prev, tile, tm):
    ext = jnp.concatenate([u_prev, u], axis=0)
    s2 = ext + _roll_down(ext, 1)
    s4 = s2 + _roll_down(s2, 2)
    s8 = s4 + _roll_down(s4, 4)
    s16 = s8 + _roll_down(s8, 8)
    lane = lax.broadcasted_iota(jnp.int32, (1, 256), 1)
    ws = _group_select(lane, s2, s4, s8, s16)[POOL_HALO:, :]
    wlen = _group_select(lane, *map(float, POOL_WINDOWS)).astype(F32)
    return ws / _pool_count(tile, tm, tm, wlen) - u


def _pool_count(tile, tm, rows, wlen):
    t = (tile * tm + 1 + lax.broadcasted_iota(jnp.int32, (rows, 1), 0)).astype(F32)
    return jnp.minimum(t, wlen)


def _layer_norm(y, lg, lb):
    mu = jnp.mean(y, axis=-1, keepdims=True)
    yc = y - mu
    rstd = lax.rsqrt(jnp.mean(yc * yc, axis=-1, keepdims=True) + NORM_EPS)
    yh = yc * rstd
    return yh, rstd, yh * lg + lb


def _halo_specs(tm, T, halo, prev):
    per = tm // halo
    if prev:
        return BS((halo, 256), lambda i: (jnp.maximum(i * per - 1, 0), 0))
    return BS((halo, 256), lambda i: (jnp.minimum((i + 1) * per, T // halo - 1), 0))


def _local_fwd(up, ca, cg, bd, pscale, cw, cb, lg, lb):
    T = up.shape[0]
    tm = min(512, T)

    def body(up_ref, uph_ref, ca_ref, cah_ref, cg_ref, cgh_ref, bd_ref, ps_ref, cw_ref, cb_ref, lg_ref, lb_ref,
             ya_ref, yc_ref, u_ref, y_ref):
        i = pl.program_id(0)
        first = i == 0
        pooled = _pool_terms(up_ref[...], jnp.where(first, 0.0, uph_ref[...]), i, tm)
        ya_ref[...] = (_dot(pooled.astype(MM), bd_ref[...]) * ps_ref[...]).astype(ya_ref.dtype)

        u = ca_ref[...] * _sigmoid(cg_ref[...])
        uh = jnp.where(first, 0.0, cah_ref[...] * _sigmoid(cgh_ref[...]))
        ext = jnp.concatenate([uh, u], axis=0)
        y = jnp.zeros((tm, 256), F32) + cb_ref[...]
        for kk in range(CONV_K):
            y = y + cw_ref[kk:kk + 1, :] * _roll_up(ext, CONV_HALO - (CONV_K - 1) + kk)[:tm, :]
        _, _, z = _layer_norm(y, lg_ref[...], lb_ref[...])
        yc_ref[...] = (z * _sigmoid(z)).astype(yc_ref.dtype)
        u_ref[...] = u
        y_ref[...] = y

    tok = lambda i: (i, 0)
    par = lambda i: (0, 0)
    t256 = BS((tm, 256), tok)
    return _pc(
        body, "local_fwd", (T // tm,),
        [t256, _halo_specs(tm, T, POOL_HALO, True), t256, _halo_specs(tm, T, CONV_HALO, True),
         t256, _halo_specs(tm, T, CONV_HALO, True),
         BS((256, 256), par), BS((1, 256), par), BS((32, 256), par), BS((1, 256), par), BS((1, 256), par),
         BS((1, 256), par)],
        [t256, t256, t256, t256],
        [SDS((T, 256), MM), SDS((T, 256), MM), SDS((T, 256), F32), SDS((T, 256), F32)],
    )(up, up, ca, ca, cg, cg, bd, pscale, cw, cb, lg, lb)


def _local_bwd(up, dya, ca, cg, u, y, dyc, bd, pscale, cw, lg, lb):
    T = up.shape[0]
    tm = min(512, T)
    n = T // tm

    def body(up_ref, uph_ref, dya_ref, dyan_ref, ca_ref, cg_ref, u_ref, y_ref, yn_ref, dyc_ref, dycn_ref,
             bd_ref, ps_ref, cw_ref, lg_ref, lb_ref,
             dup_ref, dca_ref, dcg_ref, dbd_ref, dps_ref, dcw_ref, dcb_ref, dlg_ref, dlb_ref):
        i = pl.program_id(0)
        first = i == 0
        last = i == n - 1

        @pl.when(first)
        def _():
            for ref in (dbd_ref, dps_ref, dcw_ref, dcb_ref, dlg_ref, dlb_ref):
                ref[...] = jnp.zeros_like(ref)

        ps = ps_ref[...]
        pooled = _pool_terms(up_ref[...], jnp.where(first, 0.0, uph_ref[...]), i, tm).astype(MM)
        dya_t = dya_ref[...]
        dps_ref[...] += jnp.sum(dya_t * _dot(pooled, bd_ref[...]), axis=0, keepdims=True)
        dm = (jnp.concatenate([dya_t, jnp.where(last, 0.0, dyan_ref[...])], axis=0) * ps).astype(MM)
        dbd_ref[...] += _dot_tn(pooled, dm[:tm, :])
        dpool = _dot_nt(dm, bd_ref[...])
        lane = lax.broadcasted_iota(jnp.int32, (1, 256), 1)
        wlen = _group_select(lane, *map(float, POOL_WINDOWS)).astype(F32)
        e = dpool / _pool_count(i, tm, tm + POOL_HALO, wlen)
        f2 = e + _roll_up(e, 1)
        f4 = f2 + _roll_up(f2, 2)
        f8 = f4 + _roll_up(f4, 4)
        f16 = f8 + _roll_up(f8, 8)
        dup_ref[...] = _group_select(lane, f2, f4, f8, f16)[:tm, :] - dpool[:tm, :]

        lgv = lg_ref[...]
        yext = jnp.concatenate([y_ref[...], yn_ref[...]], axis=0)
        dyc = jnp.concatenate([dyc_ref[...], jnp.where(last, 0.0, dycn_ref[...])], axis=0)
        yh, rstd, z = _layer_norm(yext, lgv, lb_ref[...])
        sig = _sigmoid(z)
        dz = dyc * (sig * (1.0 + z * (1.0 - sig)))
        dlg_ref[...] += jnp.sum((dz * yh)[:tm, :], axis=0, keepdims=True)
        dlb_ref[...] += jnp.sum(dz[:tm, :], axis=0, keepdims=True)
        dyh = dz * lgv
        dy = rstd * (dyh - jnp.mean(dyh, axis=-1, keepdims=True) - yh * jnp.mean(dyh * yh, axis=-1, keepdims=True))
        dcb_ref[...] += jnp.sum(dy[:tm, :], axis=0, keepdims=True)
        uv = u_ref[...]
        du = jnp.zeros((tm, 256), F32)
        for kk in range(CONV_K):
            ahead = _roll_up(dy, CONV_K - 1 - kk)[:tm, :]
            dcw_ref[kk:kk + 1, :] += jnp.sum(uv * ahead, axis=0, keepdims=True)
            du = du + cw_ref[kk:kk + 1, :] * ahead
        sg = _sigmoid(cg_ref[...])
        dca_ref[...] = du * sg
        dcg_ref[...] = du * ca_ref[...] * sg * (1.0 - sg)

    tok = lambda i: (i, 0)
    par = lambda i: (0, 0)
    t256 = BS((tm, 256), tok)
    p1 = BS((1, 256), par)
    return _pc(
        body, "local_bwd", (n,),
        [t256, _halo_specs(tm, T, POOL_HALO, True), t256, _halo_specs(tm, T, POOL_HALO, False), t256, t256,
         t256, t256, _halo_specs(tm, T, CONV_HALO, False), t256, _halo_specs(tm, T, CONV_HALO, False),
         BS((256, 256), par), p1, BS((32, 256), par), p1, p1],
        [t256, t256, t256, BS((256, 256), par), p1, BS((32, 256), par), p1, p1, p1],
        [SDS((T, 256), F32)] * 3 + [SDS((256, 256), F32), SDS((1, 256), F32), SDS((32, 256), F32)]
        + [SDS((1, 256), F32)] * 3,
    )(up, up, dya, dya, ca, cg, u, y, y, dyc, dyc, bd, pscale, cw, lg, lb)


def _head(x, g, target):
    T, D = x.shape
    tm = min(512, T)

    def body(x_ref, g_ref, t_ref, loss_ref, dx_ref, dg_ref):
        @pl.when(pl.program_id(0) == 0)
        def _():
            loss_ref[...] = jnp.zeros_like(loss_ref)
            dg_ref[...] = jnp.zeros_like(dg_ref)

        gv = g_ref[...]
        xh, r, yv = _rms_fwd(x_ref[...], gv)
        err = yv - t_ref[...]
        loss_ref[...] += 0.5 * jnp.sum(jnp.mean(err * err, axis=-1, keepdims=True), axis=0, keepdims=True)
        dx, dg = _rms_bwd(err * (1.0 / D), xh, r, gv)
        dx_ref[...] = dx
        dg_ref[...] += dg

    tok = lambda i: (i, 0)
    par = lambda i: (0, 0)
    return _pc(
        body, "head", (T // tm,),
        [BS((tm, D), tok), BS((1, D), par), BS((tm, D), tok)],
        [BS((1, LANES), par), BS((tm, D), tok), BS((1, D), par)],
        [SDS((1, LANES), F32), SDS((T, D), F32), SDS((1, D), F32)])(x, g, target)


def _adamw(w, gs, m, v, name):
    R, C = w.shape
    tr = R
    for cand in (512, 256, 128, 64, 32, 16, 8):
        if R % cand == 0:
            tr = cand
            break
    stacked = not isinstance(gs, (list, tuple))
    ng = 1 if stacked else len(gs)

    def body(*refs):
        w_ref, g_refs, m_ref, v_ref = refs[0], refs[1:1 + ng], refs[1 + ng], refs[2 + ng]
        g_ref, d_ref, m2_ref, v2_ref = refs[3 + ng:]
        terms = [g_refs[0][d] for d in range(gs.shape[0])] if stacked else [r[...] for r in g_refs]
        g = terms[0]
        for term in terms[1:]:
            g = g + term
        m2 = ADAM_B1 * m_ref[...] + (1.0 - ADAM_B1) * g
        v2 = ADAM_B2 * v_ref[...] + (1.0 - ADAM_B2) * jnp.square(g)
        m_hat = m2 / (1.0 - ADAM_B1 ** ADAM_STEP)
        v_hat = v2 / (1.0 - ADAM_B2 ** ADAM_STEP)
        g_ref[...] = g
        d_ref[...] = -ADAM_LR * (m_hat / (jnp.sqrt(v_hat) + ADAM_EPS) + ADAM_WD * w_ref[...])
        m2_ref[...] = m2
        v2_ref[...] = v2

    blk = BS((tr, C), lambda i: (i, 0))
    g_specs = [BS((gs.shape[0], tr, C), lambda i: (0, i, 0))] if stacked else [blk] * ng
    return _pc(body, name, (R // tr,), [blk] + g_specs + [blk, blk], [blk] * 4,
               [SDS((R, C), F32)] * 4)(w, *([gs] if stacked else gs), m, v)


def _sum_parts(owns, recvs, name):
    L = len(owns)
    R, C = owns[0].shape
    tr = next(t for t in (512, 256, 128, 64, 32, 16) if R % t == 0)

    def body(*refs):
        l = pl.program_id(0)
        s_ref = refs[2 * L]
        for ll in range(L):
            @pl.when(l == ll)
            def _(o_ref=refs[ll], r_ref=refs[L + ll]):
                s_ref[...] = ((o_ref[...] + r_ref[0].astype(F32)) + r_ref[1].astype(F32)) + r_ref[2].astype(F32)

    own_specs = [BS((tr, C), lambda l, i, ll=ll: (jnp.where(l == ll, i, 0), 0)) for ll in range(L)]
    recv_specs = [BS((3, tr, C), lambda l, i, ll=ll: (0, jnp.where(l == ll, i, 0), 0)) for ll in range(L)]
    return _pc(body, name, (L, R // tr), own_specs + recv_specs,
               BS((None, tr, C), lambda l, i: (l, i, 0)), SDS((L, R, C), F32))(*owns, *recvs)


def _sum8(parts, name):
    _, R, C = parts.shape

    def body(p_ref, s_ref):
        acc = p_ref[0]
        for d in range(1, 8):
            acc = acc + p_ref[d]
        s_ref[...] = acc

    return _pc(body, name, (1,), [BS((8, R, C), lambda i: (0, 0, 0))], BS((R, C), lambda i: (0, 0)),
               SDS((R, C), F32))(parts)


def _position():
    return lax.axis_index("x"), lax.axis_index("y"), lax.axis_index("c")


CHIP_FLIPS = ((1, 0), (0, 1), (1, 1))


class _GatherChips:
    @staticmethod
    def scratch(n):
        return [pltpu.SemaphoreType.DMA((3 * n,)), pltpu.SemaphoreType.DMA((3 * n,)), pltpu.SemaphoreType.DMA((n,))]

    @staticmethod
    def out_shape(block):
        return SDS((4,) + tuple(block.shape), block.dtype)

    @staticmethod
    def _copies(ins, outs, send_sems, recv_sems, local_sems, arrivals):
        x, y, c = _position()
        local, remote = [], []
        for i, (in_ref, out_ref) in enumerate(zip(ins, outs)):
            local.append(pltpu.make_async_copy(in_ref, out_ref.at[2 * x + y], local_sems.at[i]))
            for k, (fx, fy) in enumerate(CHIP_FLIPS):
                slot = 2 * (x ^ fx) + (y ^ fy) if arrivals else 2 * x + y
                remote.append(pltpu.make_async_remote_copy(
                    src_ref=in_ref, dst_ref=out_ref.at[slot], send_sem=send_sems.at[3 * i + k],
                    recv_sem=recv_sems.at[3 * i + k], device_id=(x ^ fx, y ^ fy, c), device_id_type=MESH))
        return local, remote

    @classmethod
    def start(cls, ins, outs, *sems):
        local, sends = cls._copies(ins, outs, *sems, arrivals=False)
        for cp in local + sends:
            cp.start()

    @classmethod
    def wait(cls, ins, outs, *sems):
        local, arrivals = cls._copies(ins, outs, *sems, arrivals=True)
        for cp in arrivals:
            cp.wait_recv()
        for cp in arrivals:
            cp.wait_send()
        for cp in local:
            cp.wait()


class _GatherChipsSplit(_GatherChips):
    @staticmethod
    def scratch(n):
        return [pltpu.SemaphoreType.DMA((6 * n,)), pltpu.SemaphoreType.DMA((6 * n,)), pltpu.SemaphoreType.DMA((n,))]

    @staticmethod
    def _half(ref, which):
        rows = ref.shape[0] // 2
        return ref.at[pl.ds(pl.multiple_of(which * rows, 16), rows)]

    @staticmethod
    def _local(ins, outs, local_sems):
        x, y, _ = _position()
        return [pltpu.make_async_copy(in_ref, out_ref.at[2 * x + y], local_sems.at[i])
                for i, (in_ref, out_ref) in enumerate(zip(ins, outs))]

    @classmethod
    def _between_chips(cls, ins, outs, send_sems, recv_sems, arrivals):
        x, y, c = _position()
        return [
            pltpu.make_async_remote_copy(
                src_ref=cls._half(in_ref, c),
                dst_ref=cls._half(out_ref.at[2 * (x ^ fx) + (y ^ fy) if arrivals else 2 * x + y], c),
                send_sem=send_sems.at[6 * i + k], recv_sem=recv_sems.at[6 * i + k],
                device_id=(x ^ fx, y ^ fy, c), device_id_type=MESH)
            for i, (in_ref, out_ref) in enumerate(zip(ins, outs)) for k, (fx, fy) in enumerate(CHIP_FLIPS)]

    @classmethod
    def _between_cores(cls, outs, send_sems, recv_sems, arrivals):
        x, y, c = _position()
        copies = []
        for i, out_ref in enumerate(outs):
            for k, (fx, fy) in enumerate(CHIP_FLIPS):
                half = cls._half(out_ref.at[2 * (x ^ fx) + (y ^ fy)], 1 - c if arrivals else c)
                copies.append(pltpu.make_async_remote_copy(
                    src_ref=half, dst_ref=half, send_sem=send_sems.at[6 * i + 3 + k],
                    recv_sem=recv_sems.at[6 * i + 3 + k], device_id=(x, y, 1 - c), device_id_type=MESH))
        return copies

    @classmethod
    def start(cls, ins, outs, send_sems, recv_sems, local_sems):
        for cp in cls._local(ins, outs, local_sems) + cls._between_chips(ins, outs, send_sems, recv_sems, False):
            cp.start()

    @classmethod
    def relay(cls, ins, outs, send_sems, recv_sems, local_sems):
        arrivals = cls._between_chips(ins, outs, send_sems, recv_sems, True)
        onward = cls._between_cores(outs, send_sems, recv_sems, False)
        for cp, nxt in zip(arrivals, onward):
            cp.wait_recv()
            nxt.start()

    @classmethod
    def wait(cls, ins, outs, send_sems, recv_sems, local_sems, relayed=False):
        if not relayed:
            cls.relay(ins, outs, send_sems, recv_sems, local_sems)
        for cp in cls._between_cores(outs, send_sems, recv_sems, True):
            cp.wait_recv()
        for cp in (cls._between_chips(ins, outs, send_sems, recv_sems, True)
                   + cls._between_cores(outs, send_sems, recv_sems, False)):
            cp.wait_send()
        for cp in cls._local(ins, outs, local_sems):
            cp.wait()


class _Symmetric:
    @classmethod
    def start(cls, ins, outs, *sems):
        for cp in cls._copies(ins, outs, *sems):
            cp.start()

    @classmethod
    def wait(cls, ins, outs, *sems):
        copies = cls._copies(ins, outs, *sems)
        for cp in copies:
            cp.wait_recv()
        for cp in copies:
            cp.wait_send()


class _ScatterChips(_Symmetric):
    @staticmethod
    def scratch(n):
        return [pltpu.SemaphoreType.DMA((3 * n,)), pltpu.SemaphoreType.DMA((3 * n,))]

    @staticmethod
    def out_shape(parts):
        return SDS((3,) + tuple(parts.shape[1:]), parts.dtype)

    @staticmethod
    def _copies(ins, outs, send_sems, recv_sems):
        x, y, c = _position()
        return [
            pltpu.make_async_remote_copy(
                src_ref=in_ref.at[2 * (x ^ fx) + (y ^ fy)], dst_ref=out_ref.at[k],
                send_sem=send_sems.at[3 * i + k], recv_sem=recv_sems.at[3 * i + k],
                device_id=(x ^ fx, y ^ fy, c), device_id_type=MESH)
            for i, (in_ref, out_ref) in enumerate(zip(ins, outs)) for k, (fx, fy) in enumerate(CHIP_FLIPS)]


class _SwapCores(_Symmetric):
    @staticmethod
    def scratch(n):
        return [pltpu.SemaphoreType.DMA((n,)), pltpu.SemaphoreType.DMA((n,))]

    @staticmethod
    def out_shape(block):
        return SDS(block.shape, block.dtype)

    @staticmethod
    def _copies(ins, outs, send_sems, recv_sems):
        x, y, c = _position()
        return [
            pltpu.make_async_remote_copy(
                src_ref=in_ref, dst_ref=out_ref, send_sem=send_sems.at[i], recv_sem=recv_sems.at[i],
                device_id=(x, y, 1 - c), device_id_type=MESH)
            for i, (in_ref, out_ref) in enumerate(zip(ins, outs))]


def _exchange(kind, arrays, name):
    n = len(arrays)

    def body(*refs):
        ins, outs, sems = refs[:n], refs[n:2 * n], refs[2 * n:]
        kind.start(ins, outs, *sems)
        kind.wait(ins, outs, *sems)

    return pl.pallas_call(body, out_shape=[kind.out_shape(a) for a in arrays], in_specs=[ANY] * n,
                          out_specs=[ANY] * n, name=name, scratch_shapes=kind.scratch(n))(*arrays)


DEVICE_FLIPS = tuple((fx, fy, fc) for fx in (0, 1) for fy in (0, 1) for fc in (0, 1))[1:]


class _GatherDevices(_GatherChips):
    @staticmethod
    def scratch(n):
        return [pltpu.SemaphoreType.DMA((7 * n,)), pltpu.SemaphoreType.DMA((7 * n,)), pltpu.SemaphoreType.DMA((n,))]

    @staticmethod
    def out_shape(block):
        return SDS((8,) + tuple(block.shape), block.dtype)

    @staticmethod
    def _copies(ins, outs, send_sems, recv_sems, local_sems, arrivals):
        x, y, c = _position()
        local, remote = [], []
        for i, (in_ref, out_ref) in enumerate(zip(ins, outs)):
            local.append(pltpu.make_async_copy(in_ref, out_ref.at[4 * x + 2 * y + c], local_sems.at[i]))
            for k, (fx, fy, fc) in enumerate(DEVICE_FLIPS):
                slot = 4 * (x ^ fx) + 2 * (y ^ fy) + (c ^ fc) if arrivals else 4 * x + 2 * y + c
                remote.append(pltpu.make_async_remote_copy(
                    src_ref=in_ref, dst_ref=out_ref.at[slot], send_sem=send_sems.at[7 * i + k],
                    recv_sem=recv_sems.at[7 * i + k], device_id=(x ^ fx, y ^ fy, c ^ fc), device_id_type=MESH))
        return local, remote


BIG = ("ffn1_w_gate", "ffn1_w_up", "ffn1_w_down", "w_in", "w_out", "ffn2_w_gate", "ffn2_w_up", "ffn2_w_down")
COL_SHARDED = ("ffn1_w_gate", "ffn1_w_up", "w_in", "ffn2_w_gate", "ffn2_w_up")
FIRST = tuple((n, 0) for n in ("ffn1_w_gate", "ffn1_w_up", "ffn1_w_down"))
LATE = tuple((n, 0) for n in ("w_out", "ffn2_w_gate", "ffn2_w_up", "ffn2_w_down")) + tuple((n, 1) for n in BIG)


def _to_shards(name, full):
    r, c = full.shape
    if name in COL_SHARDED:
        return full.reshape(r, 4, c // 4).transpose(1, 0, 2)
    return full.reshape(4, r // 4, c)


def _own_shard(name, full, chip):
    r, c = full.shape
    if name in COL_SHARDED:
        return lax.dynamic_slice_in_dim(full, chip * (c // 4), c // 4, axis=1)
    return lax.dynamic_slice_in_dim(full, chip * (r // 4), r // 4, axis=0)


def _from_shards(name, sh):
    _, r, c = sh.shape
    if name in COL_SHARDED:
        return sh.transpose(1, 0, 2).reshape(r, 4 * c)
    return sh.reshape(4 * r, c)


def _pad_w_in(w):
    return jnp.concatenate([w[:, :1792], w[:, 1800:2312], w[:, 1792:1800], jnp.zeros((w.shape[0], 248), w.dtype)], axis=1)


def _unpad_w_in(g):
    return jnp.concatenate([g[:, :1792], g[:, 2304:2312], g[:, 1792:2304]], axis=1)


def _block_diag(pw):
    out = jnp.zeros((256, 256), pw.dtype)
    for gidx in range(4):
        out = lax.dynamic_update_slice(out, pw[gidx], (64 * gidx, 64 * gidx))
    return out


SMALL = ("ffn1_norm", "mix_norm", "pool_w", "pool_scale", "forget_bias", "conv_b", "conv_ln_g", "conv_ln_b",
         "ffn2_norm", "final_norm")


def _grad_parts(grads, pieces):
    return [_to_shards(n, grads[n][l]).astype(MM) for n, l in pieces]


def _forward_backward(x, target, W, shards=None):
    T = x.shape[0]
    L = W["ffn1_norm"].shape[0]
    saved = []
    recv = {}
    for l in range(L):
        g1, gm, g2 = (W[n][l][None, :] for n in ("ffn1_norm", "mix_norm", "ffn2_norm"))
        first = shards is not None and l == 0
        hosted = (_GatherChips, [shards["w_in"][0], shards["conv_w"]]) if first else None
        x1, a1, b1, *got = _ffn_fwd(x, g1, W["ffn1_w_gate"][l], W["ffn1_w_up"][l], W["ffn1_w_down"][l], hosted=hosted)
        if first:
            W["w_in"][0] = _from_shards("w_in", got[0])
            W["conv_w"] = got[1].transpose(1, 2, 0, 3).reshape(L, CONV_K, 256)
        w_in = _pad_w_in(W["w_in"][l])
        up, q, k, v, ca, cg, zf = _mix_in_fwd(x1, gm, w_in)
        fb = jnp.pad(W["forget_bias"][l], (0, LANES - HEADS))[None, :]
        F = _fgate_fwd(zf, fb)
        if first:
            yb, lse, *got = _attn_fwd(q, k, v, F, hosted=(_GatherChipsSplit, [shards[n][ll] for n, ll in LATE]))
            for (n, ll), sh in zip(LATE, got):
                W[n][ll] = _from_shards(n, sh)
        else:
            yb, lse = _attn_fwd(q, k, v, F)
        bd = _block_diag(W["pool_w"][l]).astype(MM)
        ps, cb, lg, lb = (W[n][l][None, :] for n in ("pool_scale", "conv_b", "conv_ln_g", "conv_ln_b"))
        cw = jnp.pad(W["conv_w"][l], ((0, 1), (0, 0)))
        ya, yc, cu, cy = _local_fwd(up, ca, cg, bd, ps, cw, cb, lg, lb)
        x2 = _mix_out_fwd(x1, ya, yb, yc, W["w_out"][l])
        x3, a2, b2 = _ffn_fwd(x2, g2, W["ffn2_w_gate"][l], W["ffn2_w_up"][l], W["ffn2_w_down"][l])
        saved.append(dict(x0=x, x1=x1, x2=x2, ab1=(a1, b1), ab2=(a2, b2), w_in=w_in, up=up, ca=ca, cg=cg, zf=zf, fb=fb, F=F,
                          q=q, k=k, v=v, lse=lse, bd=bd, cw=cw, cu=cu, cy=cy, ya=ya, yb=yb, yc=yc))
        x = x3

    loss, dx, dgf = _head(x, W["final_norm"][None, :], target)
    grads = {n: [None] * L for n in W if n != "final_norm"}
    grads["final_norm"] = dgf[0]
    for l in reversed(range(L)):
        s = saved[l]
        g1, gm, g2 = (W[n][l][None, :] for n in ("ffn1_norm", "mix_norm", "ffn2_norm"))
        ps, lg, lb = (W[n][l][None, :] for n in ("pool_scale", "conv_ln_g", "conv_ln_b"))
        dx, h, dy, da, db, sact, dg = _ffn_bwd(s["x2"], dx, g2, *s["ab2"], W["ffn2_w_gate"][l], W["ffn2_w_up"][l],
                                               W["ffn2_w_down"][l])
        grads["ffn2_norm"][l] = dg[0]
        grads["ffn2_w_gate"][l] = _wgrad(h, da, "wgrad_gate")
        grads["ffn2_w_up"][l] = _wgrad(h, db, "wgrad_up")
        grads["ffn2_w_down"][l] = _wgrad(sact, dy, "wgrad_down")
        dya, dyb, dyc = _mix_out_bwd(dx, W["w_out"][l])
        grads["w_out"][l] = jnp.concatenate(
            [_wgrad(s["ya"], dx, "wgrad_out_a"), _wgrad(s["yb"], dx, "wgrad_out_b"), _wgrad(s["yc"], dx, "wgrad_out_c")], axis=0)
        first = shards is not None and l == 0
        if first:
            dq, dk, dv, dfq, dfk, *got = _attn_bwd(s["q"], s["k"], s["v"], s["F"], s["yb"], s["lse"], dyb,
                                                  hosted=(_ScatterChips, _grad_parts(grads, LATE)))
            recv.update(zip(LATE, got))
        else:
            dq, dk, dv, dfq, dfk = _attn_bwd(s["q"], s["k"], s["v"], s["F"], s["yb"], s["lse"], dyb)
        dfk_cols = jnp.pad(dfk.transpose(0, 2, 1, 3).reshape(HEADS, T).T, ((0, 0), (0, LANES - HEADS)))
        dzf, dfb = _fgate_bwd(s["zf"], s["fb"], dfq, dfk_cols)
        grads["forget_bias"][l] = dfb[0, :HEADS]
        dup, dca, dcg, dbd, dps, dcw, dcb, dlg, dlb = _local_bwd(
            s["up"], dya, s["ca"], s["cg"], s["cu"], s["cy"], dyc, s["bd"], ps, s["cw"], lg, lb)
        grads["pool_w"][l] = jnp.stack([dbd[64 * i:64 * i + 64, 64 * i:64 * i + 64] for i in range(4)])
        grads["pool_scale"][l], grads["conv_b"][l] = dps[0], dcb[0]
        grads["conv_ln_g"][l], grads["conv_ln_b"][l] = dlg[0], dlb[0]
        grads["conv_w"][l] = dcw[:CONV_K]
        dx, h, dp, dg = _mix_in_bwd(s["x1"], dx, gm, s["w_in"], dup, dq, dk, dv, dca, dcg, dzf)
        grads["mix_norm"][l] = dg[0]
        grads["w_in"][l] = _unpad_w_in(_wgrad(h, dp, "wgrad_in"))
        ffn1 = (W["ffn1_w_gate"][l], W["ffn1_w_up"][l], W["ffn1_w_down"][l])
        dx, h, dy, da, db, sact, dg = _ffn_bwd(s["x0"], dx, g1, *s["ab1"], *ffn1)
        if not first:
            grads["ffn1_w_gate"][l] = _wgrad(h, da, "wgrad_gate")
            grads["ffn1_w_up"][l] = _wgrad(h, db, "wgrad_up")
            grads["ffn1_w_down"][l] = _wgrad(sact, dy, "wgrad_down")
        else:
            scatter = lambda n: (_ScatterChips, _grad_parts(grads, [(n, 0)]))
            grads["ffn1_w_gate"][0], recv[("w_in", 0)] = _wgrad(h, da, "wgrad_gate", hosted=scatter("w_in"))
            grads["ffn1_w_up"][0], recv[("ffn1_w_gate", 0)] = _wgrad(h, db, "wgrad_up", hosted=scatter("ffn1_w_gate"))
            grads["ffn1_w_down"][0], recv[("ffn1_w_up", 0)] = _wgrad(sact, dy, "wgrad_down", hosted=scatter("ffn1_w_up"))
            recv[("ffn1_w_down", 0)] = _exchange(*scatter("ffn1_w_down"), "scatter_last_grad")[0]
        grads["ffn1_norm"][l] = dg[0]
    grads = {n: (jnp.stack(g) if isinstance(g, list) and n not in BIG else g) for n, g in grads.items()}
    return loss, dx, grads, recv


NAMES = ("ffn1_norm", "ffn1_w_gate", "ffn1_w_up", "ffn1_w_down", "mix_norm", "w_in", "pool_w", "pool_scale",
         "forget_bias", "conv_w", "conv_b", "conv_ln_g", "conv_ln_b", "w_out", "ffn2_norm", "ffn2_w_gate",
         "ffn2_w_up", "ffn2_w_down", "final_norm")


def kernel(x, ffn1_norm, ffn1_w_gate, ffn1_w_up, ffn1_w_down, mix_norm, w_in, pool_w, pool_scale, forget_bias, conv_w, conv_b, conv_ln_g, conv_ln_b, w_out, ffn2_norm, ffn2_w_gate, ffn2_w_up, ffn2_w_down, final_norm, loss_target, m_ffn1_norm, m_ffn1_w_gate, m_ffn1_w_up, m_ffn1_w_down, m_mix_norm, m_w_in, m_pool_w, m_pool_scale, m_forget_bias, m_conv_w, m_conv_b, m_conv_ln_g, m_conv_ln_b, m_w_out, m_ffn2_norm, m_ffn2_w_gate, m_ffn2_w_up, m_ffn2_w_down, m_final_norm, v_ffn1_norm, v_ffn1_w_gate, v_ffn1_w_up, v_ffn1_w_down, v_mix_norm, v_w_in, v_pool_w, v_pool_scale, v_forget_bias, v_conv_w, v_conv_b, v_conv_ln_g, v_conv_ln_b, v_w_out, v_ffn2_norm, v_ffn2_w_gate, v_ffn2_w_up, v_ffn2_w_down, v_final_norm):
    args = (ffn1_norm, ffn1_w_gate, ffn1_w_up, ffn1_w_down, mix_norm, w_in, pool_w, pool_scale, forget_bias, conv_w, conv_b, conv_ln_g, conv_ln_b, w_out, ffn2_norm, ffn2_w_gate, ffn2_w_up, ffn2_w_down, final_norm)
    ms = (m_ffn1_norm, m_ffn1_w_gate, m_ffn1_w_up, m_ffn1_w_down, m_mix_norm, m_w_in, m_pool_w, m_pool_scale, m_forget_bias, m_conv_w, m_conv_b, m_conv_ln_g, m_conv_ln_b, m_w_out, m_ffn2_norm, m_ffn2_w_gate, m_ffn2_w_up, m_ffn2_w_down, m_final_norm)
    vs = (v_ffn1_norm, v_ffn1_w_gate, v_ffn1_w_up, v_ffn1_w_down, v_mix_norm, v_w_in, v_pool_w, v_pool_scale, v_forget_bias, v_conv_w, v_conv_b, v_conv_ln_g, v_conv_ln_b, v_w_out, v_ffn2_norm, v_ffn2_w_gate, v_ffn2_w_up, v_ffn2_w_down, v_final_norm)
    P = dict(zip(NAMES, args))
    M = dict(zip(NAMES, ms))
    V = dict(zip(NAMES, vs))
    xi, yi, _ = _position()
    chip = 2 * xi + yi

    W = {n: P[n] for n in SMALL}
    W.update({n: [None] * P[n].shape[0] for n in BIG})
    shards = {n: [P[n][l].astype(MM) for l in range(P[n].shape[0])] for n in BIG}
    shards["conv_w"] = P["conv_w"]
    for (n, l), sh in zip(FIRST, _exchange(_GatherChipsSplit, [shards[n][l] for n, l in FIRST], "gather_first_weights")):
        W[n][l] = _from_shards(n, sh)

    loss_part, dx, G, recv = _forward_backward(x[0], loss_target[0], W, shards)
    loss = lax.psum(loss_part[0, 0], ("x", "y", "c"))

    res = {}
    gathered = _exchange(_GatherDevices, [G[n] for n in SMALL] + [G["conv_w"]], "gather_small_grads")
    for n, g8 in zip(SMALL, gathered):
        shp = P[n].shape
        two_d = (math.prod(shp[:-1]), shp[-1])
        outs = _adamw(P[n].reshape(two_d), g8.reshape((8,) + two_d), M[n].reshape(two_d), V[n].reshape(two_d),
                      "adamw_" + n)
        for kind, a in zip(("g", "d", "m", "v"), outs):
            res[(kind, n)] = a.reshape(shp)
    shp = P["conv_w"].shape
    g_cw_full = _sum8(gathered[-1].reshape(8, shp[0] * CONV_K, 256), "sum_conv_w_grads")
    g_cw = lax.dynamic_slice_in_dim(g_cw_full, chip * shp[2], shp[2], axis=1)
    two_d = (shp[0] * CONV_K, shp[2])
    outs = _adamw(P["conv_w"].reshape(two_d), [g_cw], M["conv_w"].reshape(two_d), V["conv_w"].reshape(two_d),
                  "adamw_conv_w")
    for kind, a in zip(("g", "d", "m", "v"), outs):
        res[(kind, "conv_w")] = a.reshape(shp)

    parts = [_sum_parts([_own_shard(n, G[n][l], chip) for l in range(P[n].shape[0])],
                        [recv[(n, l)] for l in range(P[n].shape[0])], "sum_" + n) for n in BIG]
    others = _exchange(_SwapCores, parts, "swap_core_grads")
    for n, ga, gb in zip(BIG, parts, others):
        shp = P[n].shape
        two_d = (shp[0] * shp[1], shp[2])
        outs = _adamw(P[n].reshape(two_d), [ga.reshape(two_d), gb.reshape(two_d)], M[n].reshape(two_d),
                      V[n].reshape(two_d), "adamw_" + n)
        for kind, a in zip(("g", "d", "m", "v"), outs):
            res[(kind, n)] = a.reshape(shp)

    return (loss, dx[None], *[res[("g", n)] for n in NAMES], *[res[("d", n)] for n in NAMES],
            *[res[("m", n)] for n in NAMES], *[res[("v", n)] for n in NAMES])
```

```python
import math

import jax
import jax.numpy as jnp
from jax import lax
from jax.experimental import pallas as pl
from jax.experimental.pallas import tpu as pltpu

F32 = jnp.float32
MM = jnp.bfloat16
NORM_EPS = 1e-6
HEADS = 8
HEAD_DIM = 64
POOL_WINDOWS = (2, 4, 8, 16)
CONV_K = 31
LANES = 128
VMEM_LIMIT = 56 * 2**20
FFN_BWD_ROWS = 256
FFN_COLS = 768
ATTN_FWD_TILE = 1024
ATTN_BWD_ROWS, ATTN_BWD_COLS = 1024, 512
WGRAD_COLS = 768
WGRAD_ACC_BYTES = 12 * 2**20

ADAM_LR = 0.001
ADAM_B1 = 0.9
ADAM_B2 = 0.999
ADAM_EPS = 1e-08
ADAM_WD = 0.01
ADAM_STEP = 10

MESH = pl.DeviceIdType.MESH
BS = pl.BlockSpec
SDS = jax.ShapeDtypeStruct
ANY = pl.BlockSpec(memory_space=pl.ANY)


def _dot(a, b):
    return jnp.dot(a, b, preferred_element_type=F32)


def _dot_nt(a, b):
    return lax.dot_general(a, b, (((1,), (1,)), ((), ())), preferred_element_type=F32)


def _dot_tn(a, b):
    return lax.dot_general(a, b, (((0,), (0,)), ((), ())), preferred_element_type=F32)


def _pc(body, name, grid, in_specs, out_specs, out_shape, scratch=()):
    return pl.pallas_call(
        body, out_shape=out_shape, grid=grid, in_specs=in_specs, out_specs=out_specs,
        scratch_shapes=list(scratch), name=name,
        compiler_params=pltpu.CompilerParams(
            dimension_semantics=("arbitrary",) * len(grid), vmem_limit_bytes=VMEM_LIMIT))


def _rms_fwd(x, g):
    r = lax.rsqrt(jnp.mean(x * x, axis=-1, keepdims=True) + NORM_EPS)
    xh = x * r
    return xh, r, xh * g


def _rms_bwd(dh, xh, r, g):
    dxh = dh * g
    dx = r * (dxh - xh * jnp.mean(dxh * xh, axis=-1, keepdims=True))
    return dx, jnp.sum(dh * xh, axis=0, keepdims=True)


def _sigmoid(x):
    return jax.nn.sigmoid(x)


def _ffn_fwd(x, g, wg, wu, wd, hosted=None):
    T, D = x.shape
    F = wg.shape[1]
    tm = min(512, T)
    nt = T // tm
    pieces = [(c0, min(FFN_COLS, F - c0)) for c0 in range(0, F, FFN_COLS)]
    h_in, h_out, h_shape, h_scratch = _hosted_specs(hosted)

    def body(*refs):
        i = pl.program_id(0)
        refs, finish = _hosted_edges(hosted, refs, 5, 3, i == 0, i == nt - 1)
        x_ref, g_ref, wg_ref, wu_ref, wd_ref, o_ref, a_ref, b_ref = refs
        xv = x_ref[...]
        h = _rms_fwd(xv, g_ref[...])[2].astype(MM)
        acc = jnp.zeros((tm, D), F32)
        for c0, w in pieces:
            a = _dot(h, wg_ref[:, c0:c0 + w])
            b = _dot(h, wu_ref[:, c0:c0 + w])
            a_ref[:, c0:c0 + w] = a.astype(a_ref.dtype)
            b_ref[:, c0:c0 + w] = b.astype(b_ref.dtype)
            acc = acc + _dot(((a * _sigmoid(a)) * b).astype(MM), wd_ref[c0:c0 + w, :])
        o_ref[...] = xv + 0.5 * acc
        finish()

    tok = lambda i: (i, 0)
    par = lambda i: (0, 0)
    resident = lambda shape: BS(shape, par, pipeline_mode=pl.Buffered(1))
    return _pc(
        body, "ffn_fwd" + ("_hosting" if hosted else ""), (nt,),
        [BS((tm, D), tok), BS((1, D), par), resident((D, F)), resident((D, F)), resident((F, D))] + h_in,
        [BS((tm, D), tok), BS((tm, F), tok), BS((tm, F), tok)] + h_out,
        [SDS((T, D), F32), SDS((T, F), MM), SDS((T, F), MM)] + h_shape,
        scratch=h_scratch)(x, g, wg, wu, wd, *(hosted[1] if hosted else []))


def _ffn_bwd(x, dout, g, a, b, wg, wu, wd, hosted=None):
    T, D = x.shape
    F = wg.shape[1]
    tm = min(FFN_BWD_ROWS, T)
    nt = T // tm
    pieces = [(c0, min(FFN_COLS, F - c0)) for c0 in range(0, F, FFN_COLS)]
    h_in, h_out, h_shape, h_scratch = _hosted_specs(hosted)

    def body(*refs):
        i = pl.program_id(0)
        refs, finish = _hosted_edges(hosted, refs, 8, 7, i == 0, i == nt - 1)
        (x_ref, do_ref, g_ref, a_ref, b_ref, wg_ref, wu_ref, wd_ref,
         dx_ref, h_ref, dy_ref, da_ref, db_ref, s_ref, dg_ref) = refs

        @pl.when(i == 0)
        def _():
            dg_ref[...] = jnp.zeros_like(dg_ref)

        gv = g_ref[...]
        xh, r, hg = _rms_fwd(x_ref[...], gv)
        h_ref[...] = hg.astype(h_ref.dtype)
        dy = (0.5 * do_ref[...]).astype(MM)
        dy_ref[...] = dy
        dh = jnp.zeros((tm, D), F32)
        for c0, w in pieces:
            a = a_ref[:, c0:c0 + w].astype(F32)
            b = b_ref[:, c0:c0 + w].astype(F32)
            ds = _dot_nt(dy, wd_ref[c0:c0 + w, :])
            sig = _sigmoid(a)
            sl = a * sig
            s_ref[:, c0:c0 + w] = (sl * b).astype(s_ref.dtype)
            db = (ds * sl).astype(MM)
            da = (ds * b * (sig * (1.0 + a * (1.0 - sig)))).astype(MM)
            da_ref[:, c0:c0 + w] = da
            db_ref[:, c0:c0 + w] = db
            dh = dh + _dot_nt(da, wg_ref[:, c0:c0 + w]) + _dot_nt(db, wu_ref[:, c0:c0 + w])
        dx, dg = _rms_bwd(dh, xh, r, gv)
        dx_ref[...] = do_ref[...] + dx
        dg_ref[...] += dg
        finish()

    tok = lambda i: (i, 0)
    par = lambda i: (0, 0)
    hid = BS((tm, F), tok)
    resident = lambda shape: BS(shape, par, pipeline_mode=pl.Buffered(1))
    return _pc(
        body, "ffn_bwd" + ("_hosting" if hosted else ""), (nt,),
        [BS((tm, D), tok), BS((tm, D), tok), BS((1, D), par), hid, hid,
         resident((D, F)), resident((D, F)), resident((F, D))] + h_in,
        [BS((tm, D), tok), BS((tm, D), tok), BS((tm, D), tok), hid, hid, hid, BS((1, D), par)] + h_out,
        [SDS((T, D), F32), SDS((T, D), MM), SDS((T, D), MM),
         SDS((T, F), MM), SDS((T, F), MM), SDS((T, F), MM), SDS((1, D), F32)] + h_shape,
        scratch=h_scratch,
    )(x, dout, g, a, b, wg, wu, wd, *(hosted[1] if hosted else []))


def _wgrad(a, b, name, hosted=None):
    T, K = a.shape
    N = b.shape[1]
    tt = min(512, T)
    tn = next(c for c in (N, 1408, 1280, 1024, 512, 256, 128) if N % c == 0 and K * c * 4 <= WGRAD_ACC_BYTES)
    pieces = [(c0, min(WGRAD_COLS, tn - c0)) for c0 in range(0, tn, WGRAD_COLS)]
    nn, nt = N // tn, T // tt
    h_in, h_out, h_shape, h_scratch = _hosted_specs(hosted)

    def body(*refs):
        n, t = pl.program_id(0), pl.program_id(1)
        refs, finish = _hosted_edges(hosted, refs, 2, 1, (n == 0) & (t == 0), (n == nn - 1) & (t == nt - 1))
        a_ref, b_ref, o_ref = refs

        @pl.when(t == 0)
        def _():
            o_ref[...] = jnp.zeros_like(o_ref)

        av = a_ref[...].astype(MM)
        for c0, w in pieces:
            o_ref[:, c0:c0 + w] += _dot_tn(av, b_ref[:, c0:c0 + w].astype(MM))
        finish()

    res = _pc(
        body, name + ("_hosting" if hosted else ""), (nn, nt),
        [BS((tt, K), lambda n, t: (t, 0)), BS((tt, tn), lambda n, t: (t, n))] + h_in,
        [BS((K, tn), lambda n, t: (0, n))] + h_out, [SDS((K, N), F32)] + h_shape,
        scratch=h_scratch)(a, b, *(hosted[1] if hosted else []))
    return res if hosted else res[0]


C_POOL, C_Q, C_K, C_V, C_CA, C_CG, C_ZF, C_END = 0, 256, 768, 1280, 1792, 2048, 2304, 2560


def _mix_in_fwd(x, g, w):
    T, D = x.shape
    tm = min(512, T)

    def body(x_ref, g_ref, w_ref, up_ref, q_ref, k_ref, v_ref, ca_ref, cg_ref, zf_ref):
        _, _, hg = _rms_fwd(x_ref[...], g_ref[...])
        p = _dot(hg.astype(MM), w_ref[...])
        up_ref[...] = p[:, C_POOL:C_Q]
        q_ref[...] = p[:, C_Q:C_K].astype(q_ref.dtype)
        k_ref[...] = p[:, C_K:C_V].astype(k_ref.dtype)
        v_ref[...] = p[:, C_V:C_CA].astype(v_ref.dtype)
        ca_ref[...] = p[:, C_CA:C_CG]
        cg_ref[...] = p[:, C_CG:C_ZF]
        zf_ref[...] = p[:, C_ZF:C_ZF + LANES]

    tok = lambda i: (i, 0)
    widths = (256, 512, 512, 512, 256, 256, 128)
    dtypes = (F32, MM, MM, MM, F32, F32, F32)
    return _pc(
        body, "mix_in_fwd", (T // tm,),
        [BS((tm, D), tok), BS((1, D), lambda i: (0, 0)), BS((D, C_END), lambda i: (0, 0))],
        [BS((tm, wd), tok) for wd in widths],
        [SDS((T, wd), dt) for wd, dt in zip(widths, dtypes)])(x, g, w)


def _mix_in_bwd(x, dout, g, w, dup, dq, dk, dv, dca, dcg, dzf):
    T, D = x.shape
    tm = min(512, T)

    def body(x_ref, do_ref, g_ref, w_ref, dup_ref, dq_ref, dk_ref, dv_ref, dca_ref, dcg_ref, dzf_ref,
             dx_ref, h_ref, dp_ref, dg_ref):
        @pl.when(pl.program_id(0) == 0)
        def _():
            dg_ref[...] = jnp.zeros_like(dg_ref)

        gv = g_ref[...]
        xh, r, hg = _rms_fwd(x_ref[...], gv)
        h_ref[...] = hg.astype(h_ref.dtype)
        for ref, lo, hi in ((dup_ref, C_POOL, C_Q), (dq_ref, C_Q, C_K), (dk_ref, C_K, C_V), (dv_ref, C_V, C_CA),
                            (dca_ref, C_CA, C_CG), (dcg_ref, C_CG, C_ZF), (dzf_ref, C_ZF, C_ZF + LANES)):
            dp_ref[:, lo:hi] = ref[...].astype(dp_ref.dtype)
        dp_ref[:, C_ZF + LANES:C_END] = jnp.zeros((tm, C_END - C_ZF - LANES), dp_ref.dtype)
        dh = _dot_nt(dp_ref[...], w_ref[...])
        dx, dg = _rms_bwd(dh, xh, r, gv)
        dx_ref[...] = do_ref[...] + dx
        dg_ref[...] += dg

    tok = lambda i: (i, 0)
    widths = (256, 512, 512, 512, 256, 256, 128)
    return _pc(
        body, "mix_in_bwd", (T // tm,),
        [BS((tm, D), tok), BS((tm, D), tok), BS((1, D), lambda i: (0, 0)), BS((D, C_END), lambda i: (0, 0))]
        + [BS((tm, wd), tok) for wd in widths],
        [BS((tm, D), tok), BS((tm, D), tok), BS((tm, C_END), tok), BS((1, D), lambda i: (0, 0))],
        [SDS((T, D), F32), SDS((T, D), MM), SDS((T, C_END), MM), SDS((1, D), F32)],
    )(x, dout, g, w, dup, dq, dk, dv, dca, dcg, dzf)


def _mix_out_fwd(x, ya, yb, yc, wo):
    T, D = x.shape
    tm = min(512, T)

    def body(x_ref, ya_ref, yb_ref, yc_ref, wo_ref, o_ref):
        o_ref[...] = (x_ref[...] + _dot(ya_ref[...].astype(MM), wo_ref[0:256, :])
                      + _dot(yb_ref[...].astype(MM), wo_ref[256:768, :])
                      + _dot(yc_ref[...].astype(MM), wo_ref[768:1024, :]))

    tok = lambda i: (i, 0)
    return _pc(
        body, "mix_out_fwd", (T // tm,),
        [BS((tm, D), tok), BS((tm, 256), tok), BS((tm, 512), tok), BS((tm, 256), tok), BS((D, D), lambda i: (0, 0))],
        BS((tm, D), tok), SDS((T, D), F32))(x, ya, yb, yc, wo)


def _mix_out_bwd(dx, wo):
    T, D = dx.shape
    tm = min(512, T)

    def body(dx_ref, wo_ref, dya_ref, dyb_ref, dyc_ref):
        dy = _dot_nt(dx_ref[...].astype(MM), wo_ref[...])
        dya_ref[...] = dy[:, 0:256]
        dyb_ref[...] = dy[:, 256:768]
        dyc_ref[...] = dy[:, 768:1024]

    tok = lambda i: (i, 0)
    return _pc(
        body, "mix_out_bwd", (T // tm,),
        [BS((tm, D), tok), BS((D, D), lambda i: (0, 0))],
        [BS((tm, 256), tok), BS((tm, 512), tok), BS((tm, 256), tok)],
        [SDS((T, 256), F32), SDS((T, 512), F32), SDS((T, 256), F32)])(dx, wo)


def _fgate_fwd(zf, bias):
    T = zf.shape[0]
    tc = min(256, T)

    def body(z_ref, b_ref, f_ref, carry):
        @pl.when(pl.program_id(0) == 0)
        def _():
            carry[...] = jnp.zeros_like(carry)

        z = z_ref[...] + b_ref[...]
        logf = jnp.minimum(z, 0.0) - jnp.log(1.0 + jnp.exp(-jnp.abs(z)))
        row = lax.broadcasted_iota(jnp.int32, (tc, tc), 0)
        col = lax.broadcasted_iota(jnp.int32, (tc, tc), 1)
        tri = (col <= row).astype(F32)
        f_ref[...] = jnp.dot(tri, logf, precision=lax.Precision.HIGHEST, preferred_element_type=F32) + carry[...]
        carry[...] += jnp.sum(logf, axis=0, keepdims=True)

    return _pc(
        body, "fgate_fwd", (T // tc,),
        [BS((tc, LANES), lambda i: (i, 0)), BS((1, LANES), lambda i: (0, 0))],
        BS((tc, LANES), lambda i: (i, 0)), SDS((T, LANES), F32),
        scratch=[pltpu.VMEM((1, LANES), F32)])(zf, bias)


def _fgate_bwd(zf, bias, dFq, dFk):
    T = zf.shape[0]
    tc = min(256, T)
    n = T // tc
    slabs = dFq.shape[0]

    def body(z_ref, b_ref, dfq_ref, dfk_ref, dz_ref, db_ref, carry):
        @pl.when(pl.program_id(0) == 0)
        def _():
            carry[...] = jnp.zeros_like(carry)
            db_ref[...] = jnp.zeros_like(db_ref)

        df = dfk_ref[...]
        for sl in range(slabs):
            df = df + dfq_ref[sl]
        row = lax.broadcasted_iota(jnp.int32, (tc, tc), 0)
        col = lax.broadcasted_iota(jnp.int32, (tc, tc), 1)
        tri = (col >= row).astype(F32)
        dlogf = jnp.dot(tri, df, precision=lax.Precision.HIGHEST, preferred_element_type=F32) + carry[...]
        carry[...] += jnp.sum(df, axis=0, keepdims=True)
        lane = lax.broadcasted_iota(jnp.int32, (1, LANES), 1)
        dz = jnp.where(lane < HEADS, dlogf * _sigmoid(-(z_ref[...] + b_ref[...])), 0.0)
        dz_ref[...] = dz
        db_ref[...] += jnp.sum(dz, axis=0, keepdims=True)

    rev = lambda i: (n - 1 - i, 0)
    return _pc(
        body, "fgate_bwd", (n,),
        [BS((tc, LANES), rev), BS((1, LANES), lambda i: (0, 0)), BS((slabs, tc, LANES), lambda i: (0, n - 1 - i, 0)),
         BS((tc, LANES), rev)],
        [BS((tc, LANES), rev), BS((1, LANES), lambda i: (0, 0))],
        [SDS((T, LANES), F32), SDS((1, LANES), F32)],
        scratch=[pltpu.VMEM((1, LANES), F32)])(zf, bias, dFq, dFk)


LOG2E = 1.4426950408889634


def _split3(x):
    hi = x.astype(MM)
    r1 = x - hi.astype(F32)
    mid = r1.astype(MM)
    return hi, mid, (r1 - mid.astype(F32)).astype(MM)


def _place(lane, base, cols):
    out = jnp.zeros((cols[0].shape[0], LANES), MM)
    for i, c in enumerate(cols):
        out = jnp.where(lane == base + i, c, out)
    return out


def _head_col(block, lane, h):
    return jnp.sum(jnp.where(lane == h, block, 0.0), axis=-1, keepdims=True)


def _own_lanes(lane, hh):
    return (lane < HEAD_DIM) if hh == 0 else (lane >= HEAD_DIM)


def _attn_k_side(k_ref, f_ref, kb_ref, hp, T, rows, lse_ones, v_ref=None, vb_ref=None):
    lane = lax.broadcasted_iota(jnp.int32, (1, LANES), 1)
    one = jnp.ones((rows, 1), MM)

    def chunk(c, _):
        r0 = pl.multiple_of(c * rows, rows)
        kp = k_ref[pl.ds(r0, rows), :]
        fblk = f_ref[pl.ds(r0, rows), :]
        for hh in range(2):
            hi, mid, lo = _split3(-_head_col(fblk, lane, 2 * hp + hh) * LOG2E)
            cols = [one, one, one, hi, mid, lo] + ([one, one, one] if lse_ones else [])
            bias = _place(lane, HEAD_DIM * (1 - hh), cols)
            kb_ref[hh, pl.ds(r0, rows), :] = jnp.where(_own_lanes(lane, hh), kp, bias)
            if vb_ref is not None:
                vb_ref[hh, pl.ds(r0, rows), :] = jnp.where(_own_lanes(lane, hh), v_ref[pl.ds(r0, rows), :],
                                                           jnp.ones((rows, LANES), MM))
        return 0

    lax.fori_loop(0, T // rows, chunk, 0)


def _attn_q_side(qp, fblk, lane, hp, scale, lse_blk=None):
    qc = qp.astype(F32) * (scale * LOG2E)
    qhi = qc.astype(MM)
    qlo = (qc - qhi.astype(F32)).astype(MM)
    one = jnp.ones((qp.shape[0], 1), MM)
    out = []
    for hh in range(2):
        cols = list(_split3(_head_col(fblk, lane, 2 * hp + hh) * LOG2E)) + [one, one, one]
        if lse_blk is not None:
            cols += list(_split3(-_head_col(lse_blk, lane, 2 * hp + hh)))
        bias = _place(lane, HEAD_DIM * (1 - hh), cols)
        own = _own_lanes(lane, hh)
        out.append(jnp.concatenate([jnp.where(own, qhi, jnp.zeros_like(qhi)), jnp.where(own, qlo, bias)], axis=1))
    return out


def _causal(tq, tk, col0=0):
    return lax.broadcasted_iota(jnp.int32, (tq, tk), 1) + col0 <= lax.broadcasted_iota(jnp.int32, (tq, tk), 0)


def _hosted_specs(hosted):
    if hosted is None:
        return [], [], [], []
    kind, arrays = hosted
    n = len(arrays)
    return [ANY] * n, [ANY] * n, [kind.out_shape(a) for a in arrays], kind.scratch(n)


def _hosted_edges(hosted, refs, n_in, n_out, first, last, mid=None):
    if hosted is None:
        return refs, lambda: None
    kind, arrays = hosted
    n = len(arrays)
    nsem = len(kind.scratch(n))
    o0 = n_in + n + n_out
    ins, outs, sems = refs[n_in:n_in + n], refs[o0:o0 + n], refs[len(refs) - nsem:]
    relayed = mid is not None and hasattr(kind, "relay")

    @pl.when(first)
    def _():
        kind.start(ins, outs, *sems)

    if relayed:
        @pl.when(mid)
        def _():
            kind.relay(ins, outs, *sems)

    def finish():
        @pl.when(last)
        def _():
            kind.wait(ins, outs, *sems, **({"relayed": True} if relayed else {}))

    return refs[:n_in] + refs[n_in + n:o0] + refs[o0 + n:len(refs) - nsem], finish


def _attn_fwd(q, k, v, F, hosted=None):
    T = q.shape[0]
    tq = min(ATTN_FWD_TILE, T)
    tk = tq
    nq = T // tq
    scale = 1.0 / math.sqrt(HEAD_DIM)
    h_in, h_out, h_shape, h_scratch = _hosted_specs(hosted)

    def body(*refs):
        hp, ib = pl.program_id(0), pl.program_id(1)
        refs, finish = _hosted_edges(hosted, refs, 5, 2, (hp == 0) & (ib == 0), (hp == HEADS // 2 - 1) & (ib == nq - 1),
                                     mid=(hp == HEADS // 2 - 1) & (ib == 0))
        q_ref, k_ref, v_ref, fq_ref, f_ref, o_ref, lse_ref, kb_ref, vb_ref = refs
        lane = lax.broadcasted_iota(jnp.int32, (1, LANES), 1)

        @pl.when(ib == 0)
        def _():
            _attn_k_side(k_ref, f_ref, kb_ref, hp, T, min(512, T), False, v_ref, vb_ref)

        qa = _attn_q_side(q_ref[...], fq_ref[...], lane, hp, scale)

        def tile(off, size, carry, r0=None):
            rs = r0 or 0
            kp = k_ref[pl.ds(off, size), :]
            new = []
            for hh in range(2):
                m, acc = carry[hh]
                s = _dot_nt(qa[hh][rs:], jnp.concatenate([kp, kb_ref[hh, pl.ds(off, size), :]], axis=1))
                if r0 is not None:
                    s = jnp.where(_causal(tq - rs, size), s, -jnp.inf)
                m2 = jnp.maximum(m[rs:], jnp.max(s, axis=-1, keepdims=True))
                p = jnp.exp2(s - m2)
                acc2 = acc[rs:] * jnp.exp2(m[rs:] - m2) + _dot(p.astype(MM), vb_ref[hh, pl.ds(off, size), :])
                if rs:
                    m2, acc2 = jnp.concatenate([m[:rs], m2], axis=0), jnp.concatenate([acc[:rs], acc2], axis=0)
                new.append((m2, acc2))
            return tuple(new)

        carry = tuple((jnp.full((tq, 1), -jnp.inf, F32), jnp.zeros((tq, LANES), F32)) for _ in range(2))
        carry = lax.fori_loop(0, ib, lambda jb, c: tile(pl.multiple_of(jb * tk, tk), tk, c), carry)
        half = tk // 2
        for u in range(2):
            carry = tile(pl.multiple_of(ib * tk + u * half, half), half, carry, r0=u * half)
        (m0, a0), (m1, a1) = carry
        l0, l1 = a0[:, HEAD_DIM:HEAD_DIM + 1], a1[:, 0:1]
        o_ref[...] = jnp.where(lane < HEAD_DIM, a0 / l0, a1 / l1)
        lse_ref[...] = jnp.where(lane == 2 * hp, m0 + jnp.log2(l0), jnp.where(lane == 2 * hp + 1, m1 + jnp.log2(l1), 0.0))
        finish()

    blk = lambda h, i: (i, h)
    full = lambda h, i: (0, h)
    return _pc(
        body, "attn_fwd" + ("_hosting" if hosted else ""), (HEADS // 2, nq),
        [BS((tq, LANES), blk), BS((T, LANES), full), BS((T, LANES), full), BS((tq, LANES), lambda h, i: (i, 0)),
         BS((T, LANES), lambda h, i: (0, 0))] + h_in,
        [BS((tq, LANES), blk), BS((None, tq, LANES), lambda h, i: (h, i, 0))] + h_out,
        [SDS((T, HEADS * HEAD_DIM), F32), SDS((HEADS // 2, T, LANES), F32)] + h_shape,
        scratch=[pltpu.VMEM((2, T, LANES), MM)] * 2 + h_scratch)(q, k, v, F, F, *(hosted[1] if hosted else []))


def _attn_bwd(q, k, v, F, o, lse, do, hosted=None):
    T = q.shape[0]
    tq, tk = min(ATTN_BWD_ROWS, T), min(ATTN_BWD_COLS, T)
    nq, nk, per = T // tq, T // tk, tq // tk
    scale = 1.0 / math.sqrt(HEAD_DIM)
    h_in, h_out, h_shape, h_scratch = _hosted_specs(hosted)

    def body(*refs):
        hp, ib = pl.program_id(0), pl.program_id(1)
        refs, finish = _hosted_edges(hosted, refs, 8, 5, (hp == 0) & (ib == 0), (hp == HEADS // 2 - 1) & (ib == nq - 1))
        (q_ref, k_ref, v_ref, fq_ref, f_ref, o_ref, lse_ref, do_ref,
         dq_ref, dk_ref, dv_ref, dfq_ref, dfk_ref, kb_ref, dk_acc, dv_acc) = refs
        lane = lax.broadcasted_iota(jnp.int32, (1, LANES), 1)

        @pl.when(ib == 0)
        def _():
            _attn_k_side(k_ref, f_ref, kb_ref, hp, T, tk, True)
            dk_acc[...] = jnp.zeros_like(dk_acc)
            dv_acc[...] = jnp.zeros_like(dv_acc)
            dfk_ref[...] = jnp.zeros_like(dfk_ref)

        qp = q_ref[...]
        qa = _attn_q_side(qp, fq_ref[...], lane, hp, scale, lse_ref[...])
        dob = do_ref[...].astype(MM)
        dprod = dob.astype(F32) * o_ref[...]
        qs = (qp.astype(F32) * scale).astype(MM)
        heads = []
        for hh in range(2):
            own = _own_lanes(lane, hh)
            heads.append((jnp.where(own, dob, jnp.zeros_like(dob)), jnp.where(own, qs, jnp.zeros_like(qs)),
                          jnp.sum(jnp.where(own, dprod, 0.0), axis=-1, keepdims=True)))

        def tile(jb, carry, col0=None):
            r0 = col0 or 0
            off = pl.multiple_of(jb * tk, tk)
            kp = k_ref[pl.ds(off, tk), :]
            vp = v_ref[pl.ds(off, tk), :]
            new = []
            dv_t = jnp.zeros((tk, LANES), F32)
            dk_t = jnp.zeros((tk, LANES), F32)
            for hh in range(2):
                dq, rs = carry[hh]
                dom, qm, delta = (a[r0:] for a in heads[hh])
                p = jnp.exp2(_dot_nt(qa[hh][r0:], jnp.concatenate([kp, kb_ref[hh, pl.ds(off, tk), :]], axis=1)))
                if col0 is not None:
                    p = jnp.where(_causal(tq - r0, tk), p, 0.0)
                ds = p * (_dot_nt(dom, vp) - delta)
                dsb = ds.astype(MM)
                dv_t = dv_t + _dot_tn(p.astype(MM), dom)
                dk_t = dk_t + _dot_tn(dsb, qm)
                dfk_ref[jb, pl.ds(hh, 1), :] -= jnp.sum(ds, axis=0, keepdims=True)
                ddq, drs = _dot(dsb, kp), jnp.sum(ds, axis=-1, keepdims=True)
                if r0:
                    ddq = jnp.concatenate([jnp.zeros((r0, LANES), F32), ddq], axis=0)
                    drs = jnp.concatenate([jnp.zeros((r0, 1), F32), drs], axis=0)
                new.append((dq + ddq, rs + drs))
            dv_acc[pl.ds(off, tk), :] += dv_t
            dk_acc[pl.ds(off, tk), :] += dk_t
            return tuple(new)

        carry = tuple((jnp.zeros((tq, LANES), F32), jnp.zeros((tq, 1), F32)) for _ in range(2))
        carry = lax.fori_loop(0, ib * per, lambda jb, c: tile(jb, c), carry)
        for u in range(per):
            carry = tile(ib * per + u, carry, u * tk)
        (dq0, rs0), (dq1, rs1) = carry
        dq_ref[...] = (jnp.where(lane < HEAD_DIM, dq0, dq1) * scale).astype(dq_ref.dtype)
        dfq_ref[...] = jnp.where(lane == 2 * hp, rs0, jnp.where(lane == 2 * hp + 1, rs1, 0.0))

        @pl.when(ib == nq - 1)
        def _():
            dk_ref[...] = dk_acc[...].astype(dk_ref.dtype)
            dv_ref[...] = dv_acc[...].astype(dv_ref.dtype)

        finish()

    blk = lambda h, i: (i, h)
    full = lambda h, i: (0, h)
    slab = BS((None, tq, LANES), lambda h, i: (h, i, 0))
    once = pl.Buffered(1)
    return _pc(
        body, "attn_bwd" + ("_hosting" if hosted else ""), (HEADS // 2, nq),
        [BS((tq, LANES), blk), BS((T, LANES), full, pipeline_mode=once), BS((T, LANES), full, pipeline_mode=once),
         BS((tq, LANES), lambda h, i: (i, 0)), BS((T, LANES), lambda h, i: (0, 0), pipeline_mode=once),
         BS((tq, LANES), blk), slab, BS((tq, LANES), blk)] + h_in,
        [BS((tq, LANES), blk), BS((T, LANES), full), BS((T, LANES), full), slab,
         BS((None, nk, 2, tk), lambda h, i: (h, 0, 0, 0))] + h_out,
        [SDS((T, HEADS * HEAD_DIM), MM)] * 3 + [SDS((HEADS // 2, T, LANES), F32), SDS((HEADS // 2, nk, 2, tk), F32)]
        + h_shape,
        scratch=[pltpu.VMEM((2, T, LANES), MM), pltpu.VMEM((T, LANES), F32), pltpu.VMEM((T, LANES), F32)] + h_scratch,
    )(q, k, v, F, F, o, lse, do, *(hosted[1] if hosted else []))


POOL_HALO = 16
CONV_HALO = 32


def _group_select(lane, v0, v1, v2, v3):
    return jnp.where(lane < 64, v0, jnp.where(lane < 128, v1, jnp.where(lane < 192, v2, v3)))


def _roll_down(x, k):
    return x if k == 0 else pltpu.roll(x, k, 0)


def _roll_up(x, k):
    return x if k == 0 else pltpu.roll(x, x.shape[0] - k, 0)


def _pool_terms(u, u_prev, tile, tm):
    ext = jnp.concatenate([u_prev, u], axis=0)
    s2 = ext + _roll_down(ext, 1)
    s4 = s2 + _roll_down(s2, 2)
    s8 = s4 + _roll_down(s4, 4)
    s16 = s8 + _roll_down(s8, 8)
    lane = lax.broadcasted_iota(jnp.int32, (1, 256), 1)
    ws = _group_select(lane, s2, s4, s8, s16)[POOL_HALO:, :]
    wlen = _group_select(lane, *map(float, POOL_WINDOWS)).astype(F32)
    return ws / _pool_count(tile, tm, tm, wlen) - u


def _pool_count(tile, tm, rows, wlen):
    t = (tile * tm + 1 + lax.broadcasted_iota(jnp.int32, (rows, 1), 0)).astype(F32)
    return jnp.minimum(t, wlen)


def _layer_norm(y, lg, lb):
    mu = jnp.mean(y, axis=-1, keepdims=True)
    yc = y - mu
    rstd = lax.rsqrt(jnp.mean(yc * yc, axis=-1, keepdims=True) + NORM_EPS)
    yh = yc * rstd
    return yh, rstd, yh * lg + lb


def _halo_specs(tm, T, halo, prev):
    per = tm // halo
    if prev:
        return BS((halo, 256), lambda i: (jnp.maximum(i * per - 1, 0), 0))
    return BS((halo, 256), lambda i: (jnp.minimum((i + 1) * per, T // halo - 1), 0))


def _local_fwd(up, ca, cg, bd, pscale, cw, cb, lg, lb):
    T = up.shape[0]
    tm = min(512, T)

    def body(up_ref, uph_ref, ca_ref, cah_ref, cg_ref, cgh_ref, bd_ref, ps_ref, cw_ref, cb_ref, lg_ref, lb_ref,
             ya_ref, yc_ref, u_ref, y_ref):
        i = pl.program_id(0)
        first = i == 0
        pooled = _pool_terms(up_ref[...], jnp.where(first, 0.0, uph_ref[...]), i, tm)
        ya_ref[...] = (_dot(pooled.astype(MM), bd_ref[...]) * ps_ref[...]).astype(ya_ref.dtype)

        u = ca_ref[...] * _sigmoid(cg_ref[...])
        uh = jnp.where(first, 0.0, cah_ref[...] * _sigmoid(cgh_ref[...]))
        ext = jnp.concatenate([uh, u], axis=0)
        y = jnp.zeros((tm, 256), F32) + cb_ref[...]
        for kk in range(CONV_K):
            y = y + cw_ref[kk:kk + 1, :] * _roll_up(ext, CONV_HALO - (CONV_K - 1) + kk)[:tm, :]
        _, _, z = _layer_norm(y, lg_ref[...], lb_ref[...])
        yc_ref[...] = (z * _sigmoid(z)).astype(yc_ref.dtype)
        u_ref[...] = u
        y_ref[...] = y

    tok = lambda i: (i, 0)
    par = lambda i: (0, 0)
    t256 = BS((tm, 256), tok)
    return _pc(
        body, "local_fwd", (T // tm,),
        [t256, _halo_specs(tm, T, POOL_HALO, True), t256, _halo_specs(tm, T, CONV_HALO, True),
         t256, _halo_specs(tm, T, CONV_HALO, True),
         BS((256, 256), par), BS((1, 256), par), BS((32, 256), par), BS((1, 256), par), BS((1, 256), par),
         BS((1, 256), par)],
        [t256, t256, t256, t256],
        [SDS((T, 256), MM), SDS((T, 256), MM), SDS((T, 256), F32), SDS((T, 256), F32)],
    )(up, up, ca, ca, cg, cg, bd, pscale, cw, cb, lg, lb)


def _local_bwd(up, dya, ca, cg, u, y, dyc, bd, pscale, cw, lg, lb):
    T = up.shape[0]
    tm = min(512, T)
    n = T // tm

    def body(up_ref, uph_ref, dya_ref, dyan_ref, ca_ref, cg_ref, u_ref, y_ref, yn_ref, dyc_ref, dycn_ref,
             bd_ref, ps_ref, cw_ref, lg_ref, lb_ref,
             dup_ref, dca_ref, dcg_ref, dbd_ref, dps_ref, dcw_ref, dcb_ref, dlg_ref, dlb_ref):
        i = pl.program_id(0)
        first = i == 0
        last = i == n - 1

        @pl.when(first)
        def _():
            for ref in (dbd_ref, dps_ref, dcw_ref, dcb_ref, dlg_ref, dlb_ref):
                ref[...] = jnp.zeros_like(ref)

        ps = ps_ref[...]
        pooled = _pool_terms(up_ref[...], jnp.where(first, 0.0, uph_ref[...]), i, tm).astype(MM)
        dya_t = dya_ref[...]
        dps_ref[...] += jnp.sum(dya_t * _dot(pooled, bd_ref[...]), axis=0, keepdims=True)
        dm = (jnp.concatenate([dya_t, jnp.where(last, 0.0, dyan_ref[...])], axis=0) * ps).astype(MM)
        dbd_ref[...] += _dot_tn(pooled, dm[:tm, :])
        dpool = _dot_nt(dm, bd_ref[...])
        lane = lax.broadcasted_iota(jnp.int32, (1, 256), 1)
        wlen = _group_select(lane, *map(float, POOL_WINDOWS)).astype(F32)
        e = dpool / _pool_count(i, tm, tm + POOL_HALO, wlen)
        f2 = e + _roll_up(e, 1)
        f4 = f2 + _roll_up(f2, 2)
        f8 = f4 + _roll_up(f4, 4)
        f16 = f8 + _roll_up(f8, 8)
        dup_ref[...] = _group_select(lane, f2, f4, f8, f16)[:tm, :] - dpool[:tm, :]

        lgv = lg_ref[...]
        yext = jnp.concatenate([y_ref[...], yn_ref[...]], axis=0)
        dyc = jnp.concatenate([dyc_ref[...], jnp.where(last, 0.0, dycn_ref[...])], axis=0)
        yh, rstd, z = _layer_norm(yext, lgv, lb_ref[...])
        sig = _sigmoid(z)
        dz = dyc * (sig * (1.0 + z * (1.0 - sig)))
        dlg_ref[...] += jnp.sum((dz * yh)[:tm, :], axis=0, keepdims=True)
        dlb_ref[...] += jnp.sum(dz[:tm, :], axis=0, keepdims=True)
        dyh = dz * lgv
        dy = rstd * (dyh - jnp.mean(dyh, axis=-1, keepdims=True) - yh * jnp.mean(dyh * yh, axis=-1, keepdims=True))
        dcb_ref[...] += jnp.sum(dy[:tm, :], axis=0, keepdims=True)
        uv = u_ref[...]
        du = jnp.zeros((tm, 256), F32)
        for kk in range(CONV_K):
            ahead = _roll_up(dy, CONV_K - 1 - kk)[:tm, :]
            dcw_ref[kk:kk + 1, :] += jnp.sum(uv * ahead, axis=0, keepdims=True)
            du = du + cw_ref[kk:kk + 1, :] * ahead
        sg = _sigmoid(cg_ref[...])
        dca_ref[...] = du * sg
        dcg_ref[...] = du * ca_ref[...] * sg * (1.0 - sg)

    tok = lambda i: (i, 0)
    par = lambda i: (0, 0)
    t256 = BS((tm, 256), tok)
    p1 = BS((1, 256), par)
    return _pc(
        body, "local_bwd", (n,),
        [t256, _halo_specs(tm, T, POOL_HALO, True), t256, _halo_specs(tm, T, POOL_HALO, False), t256, t256,
         t256, t256, _halo_specs(tm, T, CONV_HALO, False), t256, _halo_specs(tm, T, CONV_HALO, False),
         BS((256, 256), par), p1, BS((32, 256), par), p1, p1],
        [t256, t256, t256, BS((256, 256), par), p1, BS((32, 256), par), p1, p1, p1],
        [SDS((T, 256), F32)] * 3 + [SDS((256, 256), F32), SDS((1, 256), F32), SDS((32, 256), F32)]
        + [SDS((1, 256), F32)] * 3,
    )(up, up, dya, dya, ca, cg, u, y, y, dyc, dyc, bd, pscale, cw, lg, lb)


def _head(x, g, target):
    T, D = x.shape
    tm = min(512, T)

    def body(x_ref, g_ref, t_ref, loss_ref, dx_ref, dg_ref):
        @pl.when(pl.program_id(0) == 0)
        def _():
            loss_ref[...] = jnp.zeros_like(loss_ref)
            dg_ref[...] = jnp.zeros_like(dg_ref)

        gv = g_ref[...]
        xh, r, yv = _rms_fwd(x_ref[...], gv)
        err = yv - t_ref[...]
        loss_ref[...] += 0.5 * jnp.sum(jnp.mean(err * err, axis=-1, keepdims=True), axis=0, keepdims=True)
        dx, dg = _rms_bwd(err * (1.0 / D), xh, r, gv)
        dx_ref[...] = dx
        dg_ref[...] += dg

    tok = lambda i: (i, 0)
    par = lambda i: (0, 0)
    return _pc(
        body, "head", (T // tm,),
        [BS((tm, D), tok), BS((1, D), par), BS((tm, D), tok)],
        [BS((1, LANES), par), BS((tm, D), tok), BS((1, D), par)],
        [SDS((1, LANES), F32), SDS((T, D), F32), SDS((1, D), F32)])(x, g, target)


def _adamw(w, gs, m, v, name):
    R, C = w.shape
    tr = R
    for cand in (512, 256, 128, 64, 32, 16, 8):
        if R % cand == 0:
            tr = cand
            break
    stacked = not isinstance(gs, (list, tuple))
    ng = 1 if stacked else len(gs)

    def body(*refs):
        w_ref, g_refs, m_ref, v_ref = refs[0], refs[1:1 + ng], refs[1 + ng], refs[2 + ng]
        g_ref, d_ref, m2_ref, v2_ref = refs[3 + ng:]
        terms = [g_refs[0][d] for d in range(gs.shape[0])] if stacked else [r[...] for r in g_refs]
        g = terms[0]
        for term in terms[1:]:
            g = g + term
        m2 = ADAM_B1 * m_ref[...] + (1.0 - ADAM_B1) * g
        v2 = ADAM_B2 * v_ref[...] + (1.0 - ADAM_B2) * jnp.square(g)
        m_hat = m2 / (1.0 - ADAM_B1 ** ADAM_STEP)
        v_hat = v2 / (1.0 - ADAM_B2 ** ADAM_STEP)
        g_ref[...] = g
        d_ref[...] = -ADAM_LR * (m_hat / (jnp.sqrt(v_hat) + ADAM_EPS) + ADAM_WD * w_ref[...])
        m2_ref[...] = m2
        v2_ref[...] = v2

    blk = BS((tr, C), lambda i: (i, 0))
    g_specs = [BS((gs.shape[0], tr, C), lambda i: (0, i, 0))] if stacked else [blk] * ng
    return _pc(body, name, (R // tr,), [blk] + g_specs + [blk, blk], [blk] * 4,
               [SDS((R, C), F32)] * 4)(w, *([gs] if stacked else gs), m, v)


def _sum_parts(owns, recvs, name):
    L = len(owns)
    R, C = owns[0].shape
    tr = next(t for t in (512, 256, 128, 64, 32, 16) if R % t == 0)

    def body(*refs):
        l = pl.program_id(0)
        s_ref = refs[2 * L]
        for ll in range(L):
            @pl.when(l == ll)
            def _(o_ref=refs[ll], r_ref=refs[L + ll]):
                s_ref[...] = ((o_ref[...] + r_ref[0].astype(F32)) + r_ref[1].astype(F32)) + r_ref[2].astype(F32)

    own_specs = [BS((tr, C), lambda l, i, ll=ll: (jnp.where(l == ll, i, 0), 0)) for ll in range(L)]
    recv_specs = [BS((3, tr, C), lambda l, i, ll=ll: (0, jnp.where(l == ll, i, 0), 0)) for ll in range(L)]
    return _pc(body, name, (L, R // tr), own_specs + recv_specs,
               BS((None, tr, C), lambda l, i: (l, i, 0)), SDS((L, R, C), F32))(*owns, *recvs)


def _sum8(parts, name):
    _, R, C = parts.shape

    def body(p_ref, s_ref):
        acc = p_ref[0]
        for d in range(1, 8):
            acc = acc + p_ref[d]
        s_ref[...] = acc

    return _pc(body, name, (1,), [BS((8, R, C), lambda i: (0, 0, 0))], BS((R, C), lambda i: (0, 0)),
               SDS((R, C), F32))(parts)


def _position():
    return lax.axis_index("x"), lax.axis_index("y"), lax.axis_index("c")


CHIP_FLIPS = ((1, 0), (0, 1), (1, 1))


class _GatherChips:
    @staticmethod
    def scratch(n):
        return [pltpu.SemaphoreType.DMA((3 * n,)), pltpu.SemaphoreType.DMA((3 * n,)), pltpu.SemaphoreType.DMA((n,))]

    @staticmethod
    def out_shape(block):
        return SDS((4,) + tuple(block.shape), block.dtype)

    @staticmethod
    def _copies(ins, outs, send_sems, recv_sems, local_sems, arrivals):
        x, y, c = _position()
        local, remote = [], []
        for i, (in_ref, out_ref) in enumerate(zip(ins, outs)):
            local.append(pltpu.make_async_copy(in_ref, out_ref.at[2 * x + y], local_sems.at[i]))
            for k, (fx, fy) in enumerate(CHIP_FLIPS):
                slot = 2 * (x ^ fx) + (y ^ fy) if arrivals else 2 * x + y
                remote.append(pltpu.make_async_remote_copy(
                    src_ref=in_ref, dst_ref=out_ref.at[slot], send_sem=send_sems.at[3 * i + k],
                    recv_sem=recv_sems.at[3 * i + k], device_id=(x ^ fx, y ^ fy, c), device_id_type=MESH))
        return local, remote

    @classmethod
    def start(cls, ins, outs, *sems):
        local, sends = cls._copies(ins, outs, *sems, arrivals=False)
        for cp in local + sends:
            cp.start()

    @classmethod
    def wait(cls, ins, outs, *sems):
        local, arrivals = cls._copies(ins, outs, *sems, arrivals=True)
        for cp in arrivals:
            cp.wait_recv()
        for cp in arrivals:
            cp.wait_send()
        for cp in local:
            cp.wait()


class _GatherChipsSplit(_GatherChips):
    @staticmethod
    def scratch(n):
        return [pltpu.SemaphoreType.DMA((6 * n,)), pltpu.SemaphoreType.DMA((6 * n,)), pltpu.SemaphoreType.DMA((n,))]

    @staticmethod
    def _half(ref, which):
        rows = ref.shape[0] // 2
        return ref.at[pl.ds(pl.multiple_of(which * rows, 16), rows)]

    @staticmethod
    def _local(ins, outs, local_sems):
        x, y, _ = _position()
        return [pltpu.make_async_copy(in_ref, out_ref.at[2 * x + y], local_sems.at[i])
                for i, (in_ref, out_ref) in enumerate(zip(ins, outs))]

    @classmethod
    def _between_chips(cls, ins, outs, send_sems, recv_sems, arrivals):
        x, y, c = _position()
        return [
            pltpu.make_async_remote_copy(
                src_ref=cls._half(in_ref, c),
                dst_ref=cls._half(out_ref.at[2 * (x ^ fx) + (y ^ fy) if arrivals else 2 * x + y], c),
                send_sem=send_sems.at[6 * i + k], recv_sem=recv_sems.at[6 * i + k],
                device_id=(x ^ fx, y ^ fy, c), device_id_type=MESH)
            for i, (in_ref, out_ref) in enumerate(zip(ins, outs)) for k, (fx, fy) in enumerate(CHIP_FLIPS)]

    @classmethod
    def _between_cores(cls, outs, send_sems, recv_sems, arrivals):
        x, y, c = _position()
        copies = []
        for i, out_ref in enumerate(outs):
            for k, (fx, fy) in enumerate(CHIP_FLIPS):
                half = cls._half(out_ref.at[2 * (x ^ fx) + (y ^ fy)], 1 - c if arrivals else c)
                copies.append(pltpu.make_async_remote_copy(
                    src_ref=half, dst_ref=half, send_sem=send_sems.at[6 * i + 3 + k],
                    recv_sem=recv_sems.at[6 * i + 3 + k], device_id=(x, y, 1 - c), device_id_type=MESH))
        return copies

    @classmethod
    def start(cls, ins, outs, send_sems, recv_sems, local_sems):
        for cp in cls._local(ins, outs, local_sems) + cls._between_chips(ins, outs, send_sems, recv_sems, False):
            cp.start()

    @classmethod
    def relay(cls, ins, outs, send_sems, recv_sems, local_sems):
        arrivals = cls._between_chips(ins, outs, send_sems, recv_sems, True)
        onward = cls._between_cores(outs, send_sems, recv_sems, False)
        for cp, nxt in zip(arrivals, onward):
            cp.wait_recv()
            nxt.start()

    @classmethod
    def wait(cls, ins, outs, send_sems, recv_sems, local_sems, relayed=False):
        if not relayed:
            cls.relay(ins, outs, send_sems, recv_sems, local_sems)
        for cp in cls._between_cores(outs, send_sems, recv_sems, True):
            cp.wait_recv()
        for cp in (cls._between_chips(ins, outs, send_sems, recv_sems, True)
                   + cls._between_cores(outs, send_sems, recv_sems, False)):
            cp.wait_send()
        for cp in cls._local(ins, outs, local_sems):
            cp.wait()


class _Symmetric:
    @classmethod
    def start(cls, ins, outs, *sems):
        for cp in cls._copies(ins, outs, *sems):
            cp.start()

    @classmethod
    def wait(cls, ins, outs, *sems):
        copies = cls._copies(ins, outs, *sems)
        for cp in copies:
            cp.wait_recv()
        for cp in copies:
            cp.wait_send()


class _ScatterChips(_Symmetric):
    @staticmethod
    def scratch(n):
        return [pltpu.SemaphoreType.DMA((3 * n,)), pltpu.SemaphoreType.DMA((3 * n,))]

    @staticmethod
    def out_shape(parts):
        return SDS((3,) + tuple(parts.shape[1:]), parts.dtype)

    @staticmethod
    def _copies(ins, outs, send_sems, recv_sems):
        x, y, c = _position()
        return [
            pltpu.make_async_remote_copy(
                src_ref=in_ref.at[2 * (x ^ fx) + (y ^ fy)], dst_ref=out_ref.at[k],
                send_sem=send_sems.at[3 * i + k], recv_sem=recv_sems.at[3 * i + k],
                device_id=(x ^ fx, y ^ fy, c), device_id_type=MESH)
            for i, (in_ref, out_ref) in enumerate(zip(ins, outs)) for k, (fx, fy) in enumerate(CHIP_FLIPS)]


class _SwapCores(_Symmetric):
    @staticmethod
    def scratch(n):
        return [pltpu.SemaphoreType.DMA((n,)), pltpu.SemaphoreType.DMA((n,))]

    @staticmethod
    def out_shape(block):
        return SDS(block.shape, block.dtype)

    @staticmethod
    def _copies(ins, outs, send_sems, recv_sems):
        x, y, c = _position()
        return [
            pltpu.make_async_remote_copy(
                src_ref=in_ref, dst_ref=out_ref, send_sem=send_sems.at[i], recv_sem=recv_sems.at[i],
                device_id=(x, y, 1 - c), device_id_type=MESH)
            for i, (in_ref, out_ref) in enumerate(zip(ins, outs))]


def _exchange(kind, arrays, name):
    n = len(arrays)

    def body(*refs):
        ins, outs, sems = refs[:n], refs[n:2 * n], refs[2 * n:]
        kind.start(ins, outs, *sems)
        kind.wait(ins, outs, *sems)

    return pl.pallas_call(body, out_shape=[kind.out_shape(a) for a in arrays], in_specs=[ANY] * n,
                          out_specs=[ANY] * n, name=name, scratch_shapes=kind.scratch(n))(*arrays)


DEVICE_FLIPS = tuple((fx, fy, fc) for fx in (0, 1) for fy in (0, 1) for fc in (0, 1))[1:]


class _GatherDevices(_GatherChips):
    @staticmethod
    def scratch(n):
        return [pltpu.SemaphoreType.DMA((7 * n,)), pltpu.SemaphoreType.DMA((7 * n,)), pltpu.SemaphoreType.DMA((n,))]

    @staticmethod
    def out_shape(block):
        return SDS((8,) + tuple(block.shape), block.dtype)

    @staticmethod
    def _copies(ins, outs, send_sems, recv_sems, local_sems, arrivals):
        x, y, c = _position()
        local, remote = [], []
        for i, (in_ref, out_ref) in enumerate(zip(ins, outs)):
            local.append(pltpu.make_async_copy(in_ref, out_ref.at[4 * x + 2 * y + c], local_sems.at[i]))
            for k, (fx, fy, fc) in enumerate(DEVICE_FLIPS):
                slot = 4 * (x ^ fx) + 2 * (y ^ fy) + (c ^ fc) if arrivals else 4 * x + 2 * y + c
                remote.append(pltpu.make_async_remote_copy(
                    src_ref=in_ref, dst_ref=out_ref.at[slot], send_sem=send_sems.at[7 * i + k],
                    recv_sem=recv_sems.at[7 * i + k], device_id=(x ^ fx, y ^ fy, c ^ fc), device_id_type=MESH))
        return local, remote


BIG = ("ffn1_w_gate", "ffn1_w_up", "ffn1_w_down", "w_in", "w_out", "ffn2_w_gate", "ffn2_w_up", "ffn2_w_down")
COL_SHARDED = ("ffn1_w_gate", "ffn1_w_up", "w_in", "ffn2_w_gate", "ffn2_w_up")
FIRST = tuple((n, 0) for n in ("ffn1_w_gate", "ffn1_w_up", "ffn1_w_down"))
LATE = tuple((n, 0) for n in ("w_out", "ffn2_w_gate", "ffn2_w_up", "ffn2_w_down")) + tuple((n, 1) for n in BIG)


def _to_shards(name, full):
    r, c = full.shape
    if name in COL_SHARDED:
        return full.reshape(r, 4, c // 4).transpose(1, 0, 2)
    return full.reshape(4, r // 4, c)


def _own_shard(name, full, chip):
    r, c = full.shape
    if name in COL_SHARDED:
        return lax.dynamic_slice_in_dim(full, chip * (c // 4), c // 4, axis=1)
    return lax.dynamic_slice_in_dim(full, chip * (r // 4), r // 4, axis=0)


def _from_shards(name, sh):
    _, r, c = sh.shape
    if name in COL_SHARDED:
        return sh.transpose(1, 0, 2).reshape(r, 4 * c)
    return sh.reshape(4 * r, c)


def _pad_w_in(w):
    return jnp.concatenate([w[:, :1792], w[:, 1800:2312], w[:, 1792:1800], jnp.zeros((w.shape[0], 248), w.dtype)], axis=1)


def _unpad_w_in(g):
    return jnp.concatenate([g[:, :1792], g[:, 2304:2312], g[:, 1792:2304]], axis=1)


def _block_diag(pw):
    out = jnp.zeros((256, 256), pw.dtype)
    for gidx in range(4):
        out = lax.dynamic_update_slice(out, pw[gidx], (64 * gidx, 64 * gidx))
    return out


SMALL = ("ffn1_norm", "mix_norm", "pool_w", "pool_scale", "forget_bias", "conv_b", "conv_ln_g", "conv_ln_b",
         "ffn2_norm", "final_norm")


def _grad_parts(grads, pieces):
    return [_to_shards(n, grads[n][l]).astype(MM) for n, l in pieces]


def _forward_backward(x, target, W, shards=None):
    T = x.shape[0]
    L = W["ffn1_norm"].shape[0]
    saved = []
    recv = {}
    for l in range(L):
        g1, gm, g2 = (W[n][l][None, :] for n in ("ffn1_norm", "mix_norm", "ffn2_norm"))
        first = shards is not None and l == 0
        hosted = (_GatherChips, [shards["w_in"][0], shards["conv_w"]]) if first else None
        x1, a1, b1, *got = _ffn_fwd(x, g1, W["ffn1_w_gate"][l], W["ffn1_w_up"][l], W["ffn1_w_down"][l], hosted=hosted)
        if first:
            W["w_in"][0] = _from_shards("w_in", got[0])
            W["conv_w"] = got[1].transpose(1, 2, 0, 3).reshape(L, CONV_K, 256)
        w_in = _pad_w_in(W["w_in"][l])
        up, q, k, v, ca, cg, zf = _mix_in_fwd(x1, gm, w_in)
        fb = jnp.pad(W["forget_bias"][l], (0, LANES - HEADS))[None, :]
        F = _fgate_fwd(zf, fb)
        if first:
            yb, lse, *got = _attn_fwd(q, k, v, F, hosted=(_GatherChipsSplit, [shards[n][ll] for n, ll in LATE]))
            for (n, ll), sh in zip(LATE, got):
                W[n][ll] = _from_shards(n, sh)
        else:
            yb, lse = _attn_fwd(q, k, v, F)
        bd = _block_diag(W["pool_w"][l]).astype(MM)
        ps, cb, lg, lb = (W[n][l][None, :] for n in ("pool_scale", "conv_b", "conv_ln_g", "conv_ln_b"))
        cw = jnp.pad(W["conv_w"][l], ((0, 1), (0, 0)))
        ya, yc, cu, cy = _local_fwd(up, ca, cg, bd, ps, cw, cb, lg, lb)
        x2 = _mix_out_fwd(x1, ya, yb, yc, W["w_out"][l])
        x3, a2, b2 = _ffn_fwd(x2, g2, W["ffn2_w_gate"][l], W["ffn2_w_up"][l], W["ffn2_w_down"][l])
        saved.append(dict(x0=x, x1=x1, x2=x2, ab1=(a1, b1), ab2=(a2, b2), w_in=w_in, up=up, ca=ca, cg=cg, zf=zf, fb=fb, F=F,
                          q=q, k=k, v=v, lse=lse, bd=bd, cw=cw, cu=cu, cy=cy, ya=ya, yb=yb, yc=yc))
        x = x3

    loss, dx, dgf = _head(x, W["final_norm"][None, :], target)
    grads = {n: [None] * L for n in W if n != "final_norm"}
    grads["final_norm"] = dgf[0]
    for l in reversed(range(L)):
        s = saved[l]
        g1, gm, g2 = (W[n][l][None, :] for n in ("ffn1_norm", "mix_norm", "ffn2_norm"))
        ps, lg, lb = (W[n][l][None, :] for n in ("pool_scale", "conv_ln_g", "conv_ln_b"))
        dx, h, dy, da, db, sact, dg = _ffn_bwd(s["x2"], dx, g2, *s["ab2"], W["ffn2_w_gate"][l], W["ffn2_w_up"][l],
                                               W["ffn2_w_down"][l])
        grads["ffn2_norm"][l] = dg[0]
        grads["ffn2_w_gate"][l] = _wgrad(h, da, "wgrad_gate")
        grads["ffn2_w_up"][l] = _wgrad(h, db, "wgrad_up")
        grads["ffn2_w_down"][l] = _wgrad(sact, dy, "wgrad_down")
        dya, dyb, dyc = _mix_out_bwd(dx, W["w_out"][l])
        grads["w_out"][l] = jnp.concatenate(
            [_wgrad(s["ya"], dx, "wgrad_out_a"), _wgrad(s["yb"], dx, "wgrad_out_b"), _wgrad(s["yc"], dx, "wgrad_out_c")], axis=0)
        first = shards is not None and l == 0
        if first:
            dq, dk, dv, dfq, dfk, *got = _attn_bwd(s["q"], s["k"], s["v"], s["F"], s["yb"], s["lse"], dyb,
                                                  hosted=(_ScatterChips, _grad_parts(grads, LATE)))
            recv.update(zip(LATE, got))
        else:
            dq, dk, dv, dfq, dfk = _attn_bwd(s["q"], s["k"], s["v"], s["F"], s["yb"], s["lse"], dyb)
        dfk_cols = jnp.pad(dfk.transpose(0, 2, 1, 3).reshape(HEADS, T).T, ((0, 0), (0, LANES - HEADS)))
        dzf, dfb = _fgate_bwd(s["zf"], s["fb"], dfq, dfk_cols)
        grads["forget_bias"][l] = dfb[0, :HEADS]
        dup, dca, dcg, dbd, dps, dcw, dcb, dlg, dlb = _local_bwd(
            s["up"], dya, s["ca"], s["cg"], s["cu"], s["cy"], dyc, s["bd"], ps, s["cw"], lg, lb)
        grads["pool_w"][l] = jnp.stack([dbd[64 * i:64 * i + 64, 64 * i:64 * i + 64] for i in range(4)])
        grads["pool_scale"][l], grads["conv_b"][l] = dps[0], dcb[0]
        grads["conv_ln_g"][l], grads["conv_ln_b"][l] = dlg[0], dlb[0]
        grads["conv_w"][l] = dcw[:CONV_K]
        dx, h, dp, dg = _mix_in_bwd(s["x1"], dx, gm, s["w_in"], dup, dq, dk, dv, dca, dcg, dzf)
        grads["mix_norm"][l] = dg[0]
        grads["w_in"][l] = _unpad_w_in(_wgrad(h, dp, "wgrad_in"))
        ffn1 = (W["ffn1_w_gate"][l], W["ffn1_w_up"][l], W["ffn1_w_down"][l])
        dx, h, dy, da, db, sact, dg = _ffn_bwd(s["x0"], dx, g1, *s["ab1"], *ffn1)
        if not first:
            grads["ffn1_w_gate"][l] = _wgrad(h, da, "wgrad_gate")
            grads["ffn1_w_up"][l] = _wgrad(h, db, "wgrad_up")
            grads["ffn1_w_down"][l] = _wgrad(sact, dy, "wgrad_down")
        else:
            scatter = lambda n: (_ScatterChips, _grad_parts(grads, [(n, 0)]))
            grads["ffn1_w_gate"][0], recv[("w_in", 0)] = _wgrad(h, da, "wgrad_gate", hosted=scatter("w_in"))
            grads["ffn1_w_up"][0], recv[("ffn1_w_gate", 0)] = _wgrad(h, db, "wgrad_up", hosted=scatter("ffn1_w_gate"))
            grads["ffn1_w_down"][0], recv[("ffn1_w_up", 0)] = _wgrad(sact, dy, "wgrad_down", hosted=scatter("ffn1_w_up"))
            recv[("ffn1_w_down", 0)] = _exchange(*scatter("ffn1_w_down"), "scatter_last_grad")[0]
        grads["ffn1_norm"][l] = dg[0]
    grads = {n: (jnp.stack(g) if isinstance(g, list) and n not in BIG else g) for n, g in grads.items()}
    return loss, dx, grads, recv


NAMES = ("ffn1_norm", "ffn1_w_gate", "ffn1_w_up", "ffn1_w_down", "mix_norm", "w_in", "pool_w", "pool_scale",
         "forget_bias", "conv_w", "conv_b", "conv_ln_g", "conv_ln_b", "w_out", "ffn2_norm", "ffn2_w_gate",
         "ffn2_w_up", "ffn2_w_down", "final_norm")


def kernel(x, ffn1_norm, ffn1_w_gate, ffn1_w_up, ffn1_w_down, mix_norm, w_in, pool_w, pool_scale, forget_bias, conv_w, conv_b, conv_ln_g, conv_ln_b, w_out, ffn2_norm, ffn2_w_gate, ffn2_w_up, ffn2_w_down, final_norm, loss_target, m_ffn1_norm, m_ffn1_w_gate, m_ffn1_w_up, m_ffn1_w_down, m_mix_norm, m_w_in, m_pool_w, m_pool_scale, m_forget_bias, m_conv_w, m_conv_b, m_conv_ln_g, m_conv_ln_b, m_w_out, m_ffn2_norm, m_ffn2_w_gate, m_ffn2_w_up, m_ffn2_w_down, m_final_norm, v_ffn1_norm, v_ffn1_w_gate, v_ffn1_w_up, v_ffn1_w_down, v_mix_norm, v_w_in, v_pool_w, v_pool_scale, v_forget_bias, v_conv_w, v_conv_b, v_conv_ln_g, v_conv_ln_b, v_w_out, v_ffn2_norm, v_ffn2_w_gate, v_ffn2_w_up, v_ffn2_w_down, v_final_norm):
    args = (ffn1_norm, ffn1_w_gate, ffn1_w_up, ffn1_w_down, mix_norm, w_in, pool_w, pool_scale, forget_bias, conv_w, conv_b, conv_ln_g, conv_ln_b, w_out, ffn2_norm, ffn2_w_gate, ffn2_w_up, ffn2_w_down, final_norm)
    ms = (m_ffn1_norm, m_ffn1_w_gate, m_ffn1_w_up, m_ffn1_w_down, m_mix_norm, m_w_in, m_pool_w, m_pool_scale, m_forget_bias, m_conv_w, m_conv_b, m_conv_ln_g, m_conv_ln_b, m_w_out, m_ffn2_norm, m_ffn2_w_gate, m_ffn2_w_up, m_ffn2_w_down, m_final_norm)
    vs = (v_ffn1_norm, v_ffn1_w_gate, v_ffn1_w_up, v_ffn1_w_down, v_mix_norm, v_w_in, v_pool_w, v_pool_scale, v_forget_bias, v_conv_w, v_conv_b, v_conv_ln_g, v_conv_ln_b, v_w_out, v_ffn2_norm, v_ffn2_w_gate, v_ffn2_w_up, v_ffn2_w_down, v_final_norm)
    P = dict(zip(NAMES, args))
    M = dict(zip(NAMES, ms))
    V = dict(zip(NAMES, vs))
    xi, yi, _ = _position()
    chip = 2 * xi + yi

    W = {n: P[n] for n in SMALL}
    W.update({n: [None] * P[n].shape[0] for n in BIG})
    shards = {n: [P[n][l].astype(MM) for l in range(P[n].shape[0])] for n in BIG}
    shards["conv_w"] = P["conv_w"]
    for (n, l), sh in zip(FIRST, _exchange(_GatherChipsSplit, [shards[n][l] for n, l in FIRST], "gather_first_weights")):
        W[n][l] = _from_shards(n, sh)

    loss_part, dx, G, recv = _forward_backward(x[0], loss_target[0], W, shards)
    loss = lax.psum(loss_part[0, 0], ("x", "y", "c"))

    res = {}
    gathered = _exchange(_GatherDevices, [G[n] for n in SMALL] + [G["conv_w"]], "gather_small_grads")
    for n, g8 in zip(SMALL, gathered):
        shp = P[n].shape
        two_d = (math.prod(shp[:-1]), shp[-1])
        outs = _adamw(P[n].reshape(two_d), g8.reshape((8,) + two_d), M[n].reshape(two_d), V[n].reshape(two_d),
                      "adamw_" + n)
        for kind, a in zip(("g", "d", "m", "v"), outs):
            res[(kind, n)] = a.reshape(shp)
    shp = P["conv_w"].shape
    g_cw_full = _sum8(gathered[-1].reshape(8, shp[0] * CONV_K, 256), "sum_conv_w_grads")
    g_cw = lax.dynamic_slice_in_dim(g_cw_full, chip * shp[2], shp[2], axis=1)
    two_d = (shp[0] * CONV_K, shp[2])
    outs = _adamw(P["conv_w"].reshape(two_d), [g_cw], M["conv_w"].reshape(two_d), V["conv_w"].reshape(two_d),
                  "adamw_conv_w")
    for kind, a in zip(("g", "d", "m", "v"), outs):
        res[(kind, "conv_w")] = a.reshape(shp)

    parts = [_sum_parts([_own_shard(n, G[n][l], chip) for l in range(P[n].shape[0])],
                        [recv[(n, l)] for l in range(P[n].shape[0])], "sum_" + n) for n in BIG]
    others = _exchange(_SwapCores, parts, "swap_core_grads")
    for n, ga, gb in zip(BIG, parts, others):
        shp = P[n].shape
        two_d = (shp[0] * shp[1], shp[2])
        outs = _adamw(P[n].reshape(two_d), [ga.reshape(two_d), gb.reshape(two_d)], M[n].reshape(two_d),
                      V[n].reshape(two_d), "adamw_" + n)
        for kind, a in zip(("g", "d", "m", "v"), outs):
            res[(kind, n)] = a.reshape(shp)

    return (loss, dx[None], *[res[("g", n)] for n in NAMES], *[res[("d", n)] for n in NAMES],
            *[res[("m", n)] for n in NAMES], *[res[("v", n)] for n in NAMES])
```
